```python
import math
import jax, jax.numpy as jnp
from jax import lax
import numpy as np

D_MODEL = 1024
BATCH = 4
SEQ = 4096
DEPTH = 2

GRID_W = 64
P_DIM = 256
EPS = 1e-6

A_HEADS = 4
A_DK = 128
A_DV = 128
A_KDIM = A_HEADS * A_DK
A_WIDTH = A_HEADS * A_DV
A_CHUNK = 64

B_HEADS = 8
B_HEADDIM = 64
B_WIDTH = B_HEADS * B_HEADDIM
B_GROUPS = 2
B_STATE = 128
B_CONV = 5
B_CONV_DIM = B_WIDTH + 2 * B_GROUPS * B_STATE
B_CHUNK = 128

C_HEADS = 16
C_HEADDIM = 64
C_WIDTH = C_HEADS * C_HEADDIM
NA_ROWS = 8
NA_COLS = 16
NA_COL_BLOCK = 16
NA_KEY_COLS = 32

D_FF = 2816
N_EXPERTS = 8
TOP_K = 2
D_FF_EXPERT = 3584

N_EVEN = (DEPTH + 1) // 2
N_ODD = DEPTH // 2
EVEN_SPLITS = (A_KDIM, A_KDIM, A_KDIM, A_WIDTH, A_WIDTH, B_WIDTH, B_CONV_DIM, B_HEADS, B_HEADS)
EVEN_IN = sum(EVEN_SPLITS)

kernel_name = 'hybrid_hgrn2_ssd_natten_moe_encoder'


def _split(u, sizes):
    idx = np.cumsum(np.array(sizes))[:-1].tolist()
    return jnp.split(u, idx, axis=-1)


def rmsnorm(x, g):
    xf = x.astype(jnp.float32)
    y = xf * lax.rsqrt(jnp.mean(xf * xf, axis=-1, keepdims=True) + EPS)
    return (y * g.astype(jnp.float32)).astype(x.dtype)


def segsum(a):
    T = a.shape[-1]
    cs = jnp.cumsum(a, axis=-1)
    diff = cs[..., :, None] - cs[..., None, :]
    mask = jnp.tril(jnp.ones((T, T), dtype=bool))
    return jnp.where(mask, diff, -jnp.inf)


def gla_chunk_scan(q, k, v, logf):
    Bsz, H, S, dk = q.shape
    dv = v.shape[-1]
    C = A_CHUNK
    nc = S // C
    chunks = lambda t: t.reshape(Bsz, H, nc, C, t.shape[-1]).transpose(2, 0, 1, 3, 4)
    causal = jnp.tril(jnp.ones((C, C), dtype=bool))[:, :, None]

    def step(state, inp):
        qi, ki, vi, gi = inp
        b = jnp.cumsum(gi, axis=2)
        diff = b[:, :, :, None, :] - b[:, :, None, :, :]
        decay = jnp.where(causal, jnp.exp(jnp.where(causal, diff, 0.0)), 0.0)
        attn = jnp.einsum('bhtk,bhtsk,bhsk->bhts', qi, decay, ki)
        o = jnp.einsum('bhts,bhsv->bhtv', attn, vi) + jnp.einsum('bhtk,bhkv->bhtv', qi * jnp.exp(b), state)
        b_last = b[:, :, -1:, :]
        new_state = state * jnp.exp(b_last[:, :, 0, :, None]) + jnp.einsum('bhsk,bhsv->bhkv', ki * jnp.exp(b_last - b), vi)
        return new_state, o

    state0 = jnp.zeros((Bsz, H, dk, dv), jnp.float32)
    _, out = lax.scan(step, state0, (chunks(q), chunks(k), chunks(v), chunks(logf)))
    return out.transpose(1, 2, 0, 3, 4).reshape(Bsz, H, S, dv)


def hgrn2_mixer(q_raw, ff_raw, fb_raw, i_raw, g_raw, lb, norm_g):
    Bsz, S, _ = q_raw.shape
    f32 = jnp.float32
    heads = lambda t, d: t.reshape(Bsz, S, A_HEADS, d).transpose(0, 2, 1, 3).astype(f32)
    q = heads(jax.nn.silu(q_raw), A_DK)
    v = heads(i_raw, A_DV)
    lbh = lb.reshape(A_HEADS, A_DK)[None, :, None, :]

    def gates(f_raw):
        f = lbh + (1.0 - lbh) * jax.nn.sigmoid(heads(f_raw, A_DK))
        return 1.0 - f, jnp.log(f)

    k_f, lf_f = gates(ff_raw)
    k_b, lf_b = gates(fb_raw)
    flip = lambda t: jnp.flip(t, axis=2)
    o = gla_chunk_scan(q, k_f, v, lf_f) + flip(gla_chunk_scan(flip(q), flip(k_b), flip(v), flip(lf_b)))
    o = o.transpose(0, 2, 1, 3).astype(q_raw.dtype)
    o = rmsnorm(o, norm_g) * jax.nn.silu(g_raw.reshape(Bsz, S, A_HEADS, A_DV))
    return o.reshape(Bsz, S, A_WIDTH)


def ssd_chunked(xs, dt, a_coef, Bm, Cm):
    Bsz, S, H, P = xs.shape
    G, N = Bm.shape[-2:]
    R = H // G
    Q = B_CHUNK
    nc = S // Q
    f32 = jnp.float32
    xc = (xs.astype(f32) * dt[..., None]).reshape(Bsz, nc, Q, G, R, P)
    Bc = Bm.astype(f32).reshape(Bsz, nc, Q, G, N)
    Cc = Cm.astype(f32).reshape(Bsz, nc, Q, G, N)
    a = (dt * a_coef).reshape(Bsz, nc, Q, G, R).transpose(0, 3, 4, 1, 2)
    a_cs = jnp.cumsum(a, axis=-1)
    L = jnp.exp(segsum(a))
    scores = jnp.einsum('bctgn,bcsgn->bgcts', Cc, Bc)
    y_diag = jnp.einsum('bgcts,bgrcts,bcsgrp->bctgrp', scores, L, xc)
    decay_states = jnp.exp(a_cs[..., -1:] - a_cs)
    states = jnp.einsum('bcsgn,bgrcs,bcsgrp->bcgrpn', Bc, decay_states, xc)
    chunk_a = jnp.pad(a_cs[..., -1], [(0, 0), (0, 0), (0, 0), (1, 0)])
    decay_chunk = jnp.exp(segsum(chunk_a))
    states = jnp.concatenate([jnp.zeros_like(states[:, :1]), states], axis=1)
    states = jnp.einsum('bgrzc,bcgrpn->bzgrpn', decay_chunk, states)[:, :-1]
    y_off = jnp.einsum('bctgn,bcgrpn,bgrct->bctgrp', Cc, states, jnp.exp(a_cs))
    return (y_diag + y_off).reshape(Bsz, S, H, P)


def mamba2_mixer(z, xBC, dt_f_raw, dt_b_raw, conv_w, conv_b, A_log, dt_bias, D_skip, norm_g):
    Bsz, S, Cdim = xBC.shape
    f32 = jnp.float32
    xBC = lax.conv_general_dilated(xBC, conv_w[:, None, :], window_strides=(1,),
                                   padding=[(B_CONV // 2, B_CONV // 2)],
                                   dimension_numbers=('NWC', 'WIO', 'NWC'),
                                   feature_group_count=Cdim)
    xBC = jax.nn.silu(xBC + conv_b)
    xs, Bm, Cm = _split(xBC, (B_WIDTH, B_GROUPS * B_STATE, B_GROUPS * B_STATE))
    xs = xs.reshape(Bsz, S, B_HEADS, B_HEADDIM)
    Bm = Bm.reshape(Bsz, S, B_GROUPS, B_STATE)
    Cm = Cm.reshape(Bsz, S, B_GROUPS, B_STATE)
    A = -jnp.exp(A_log.astype(f32))
    dtb = dt_bias.astype(f32)
    dt_f = jax.nn.softplus(dt_f_raw.astype(f32) + dtb[0])
    dt_b = jax.nn.softplus(dt_b_raw.astype(f32) + dtb[1])
    flip = lambda t: jnp.flip(t, axis=1)
    y = (ssd_chunked(xs, dt_f, A[0], Bm, Cm)
         + flip(ssd_chunked(flip(xs), flip(dt_b), A[1], flip(Bm), flip(Cm)))
         + D_skip.astype(f32)[:, None] * xs.astype(f32))
    y = y.reshape(Bsz, S, B_WIDTH).astype(z.dtype) * jax.nn.silu(z)
    y = rmsnorm(y.reshape(Bsz, S, B_GROUPS, B_WIDTH // B_GROUPS), norm_g.reshape(B_GROUPS, B_WIDTH // B_GROUPS))
    return y.reshape(Bsz, S, B_WIDTH)


def _na_column_tables():
    n_cb = GRID_W // NA_COL_BLOCK
    j = np.arange(n_cb)
    blk_start = np.clip(j * NA_COL_BLOCK - NA_COLS // 2, 0, GRID_W - NA_KEY_COLS)
    col_idx = blk_start[:, None] + np.arange(NA_KEY_COLS)[None, :]
    qcol = j[:, None] * NA_COL_BLOCK + np.arange(NA_COL_BLOCK)[None, :]
    qs = np.clip(qcol - NA_COLS // 2, 0, GRID_W - NA_COLS)
    kc = col_idx[:, None, :]
    valid = (kc >= qs[:, :, None]) & (kc < qs[:, :, None] + NA_COLS)
    dc_idx = np.clip(kc - qcol[:, :, None] + (NA_COLS - 1), 0, 2 * NA_COLS - 2)
    return col_idx.astype(np.int32), valid, dc_idx.astype(np.int32)


def neighborhood_attention(q, k, v, rpb):
    Bsz, S, H, dh = q.shape
    rows = S // GRID_W
    kh = min(NA_ROWS, rows)
    n_cb = GRID_W // NA_COL_BLOCK
    col_idx, valid, dc_idx = _na_column_tables()
    valid = jnp.asarray(valid)[:, :, None, :]
    grid = lambda t: t.reshape(Bsz, rows, GRID_W, H, dh).transpose(1, 0, 3, 2, 4)
    qg, kg, vg = grid(q), grid(k), grid(v)
    scale = C_HEADDIM ** -0.5

    def row_block(args):
        r, q_row = args
        qb = q_row.reshape(Bsz, H, n_cb, NA_COL_BLOCK, dh)
        rs = jnp.clip(r - kh // 2, 0, rows - kh)
        kr = lax.dynamic_slice_in_dim(kg, rs, kh, axis=0)[:, :, :, col_idx, :]
        vr = lax.dynamic_slice_in_dim(vg, rs, kh, axis=0)[:, :, :, col_idx, :]
        s = jnp.einsum('bhjqd,rbhjkd->bhjqrk', qb, kr).astype(jnp.float32) * scale
        dr = rs + jnp.arange(kh) - r + (NA_ROWS - 1)
        bias = rpb[:, dr][:, :, dc_idx].transpose(0, 2, 3, 1, 4)
        s = jnp.where(valid, s + bias.astype(jnp.float32)[None], -jnp.inf)
        pr = jax.nn.softmax(s.reshape(Bsz, H, n_cb, NA_COL_BLOCK, kh * NA_KEY_COLS), axis=-1)
        pr = pr.reshape(s.shape).astype(v.dtype)
        o = jnp.einsum('bhjqrk,rbhjkd->bhjqd', pr, vr)
        return o.reshape(Bsz, H, GRID_W, dh)

    out = lax.map(row_block, (jnp.arange(rows), qg))
    return out.transpose(1, 0, 3, 2, 4).reshape(Bsz, S, H * dh)


def swiglu(h, w_gate, w_up, w_down):
    return (jax.nn.silu(h @ w_gate) * (h @ w_up)) @ w_down


def moe_swiglu(h, w_router, w_gate, w_up, w_down):
    logits = (h @ w_router).astype(jnp.float32)
    top_val, top_idx = lax.top_k(logits, TOP_K)
    wts = jax.nn.softmax(top_val, axis=-1)
    gates = jnp.sum(jax.nn.one_hot(top_idx, N_EXPERTS, dtype=jnp.float32) * wts[..., None], axis=-2)
    gates = gates.astype(h.dtype)
    out = jnp.zeros_like(h)
    for e in range(N_EXPERTS):
        out = out + gates[..., e:e + 1] * swiglu(h, w_gate[e], w_up[e], w_down[e])
    return out


def setup_inputs(seed: int = 0) -> dict:
    key = jax.random.key(seed)
    keys = iter(jax.random.split(key, 64))
    nk = lambda: next(keys)
    f32 = jnp.float32

    def w(shape, fan_in):
        return jax.random.normal(nk(), shape, f32) * fan_in ** -0.5

    def gain(shape):
        return 1.0 + 0.05 * jax.random.normal(nk(), shape, f32)

    def small(shape, s=0.02):
        return s * jax.random.normal(nk(), shape, f32)

    NE, NO, D = N_EVEN, N_ODD, D_MODEL
    x = jax.random.normal(nk(), (BATCH, SEQ, D), f32)
    p = jax.random.normal(nk(), (DEPTH, BATCH, SEQ, P_DIM), f32)
    hgrn_lb_raw = 0.5 * jax.random.normal(nk(), (DEPTH + 1, A_KDIM), f32)
    dt0 = jnp.exp(jax.random.uniform(nk(), (NE, 2, B_HEADS), f32, math.log(1e-3), math.log(1e-1)))
    return {
        'x': x,
        'p': p,
        'hgrn_lb_raw': hgrn_lb_raw,
        'e_norm_mix_pre': gain((NE, D)),
        'e_w_in': w((NE, D, EVEN_IN), D),
        'e_conv_w': w((NE, B_CONV, B_CONV_DIM), B_CONV),
        'e_conv_b': small((NE, B_CONV_DIM)),
        'e_A_log': jnp.log(jax.random.uniform(nk(), (NE, 2, B_HEADS), f32, 1.0, 16.0)),
        'e_dt_bias': dt0 + jnp.log(-jnp.expm1(-dt0)),
        'e_D': gain((NE, B_HEADS)),
        'e_a_norm': gain((NE, A_DV)),
        'e_b_norm': gain((NE, B_WIDTH)),
        'e_w_out': w((NE, A_WIDTH + B_WIDTH, D), A_WIDTH + B_WIDTH),
        'e_norm_mix_post': gain((NE, D)),
        'e_norm_ffn_pre': gain((NE, D)),
        'e_w_ffn_gate': w((NE, D, D_FF), D),
        'e_w_ffn_up': w((NE, D, D_FF), D),
        'e_w_ffn_down': w((NE, D_FF, D), D_FF),
        'e_norm_ffn_post': gain((NE, D)),
        'o_norm_mix_pre': gain((NO, D)),
        'o_w_qkv': w((NO, D, 3 * C_WIDTH), D),
        'o_rpb': small((NO, C_HEADS, 2 * NA_ROWS - 1, 2 * NA_COLS - 1), 0.5),
        'o_w_out': w((NO, C_WIDTH, D), C_WIDTH),
        'o_norm_mix_post': gain((NO, D)),
        'o_norm_ffn_pre': gain((NO, D)),
        'o_w_router': w((NO, D, N_EXPERTS), D),
        'o_w_exp_gate': w((NO, N_EXPERTS, D, D_FF_EXPERT), D),
        'o_w_exp_up': w((NO, N_EXPERTS, D, D_FF_EXPERT), D),
        'o_w_exp_down': w((NO, N_EXPERTS, D_FF_EXPERT, D), D_FF_EXPERT),
        'o_norm_ffn_post': gain((NO, D)),
        'ple_norm_in': gain((DEPTH, D)),
        'ple_w_gate': w((DEPTH, D, D), D),
        'ple_w_proj': w((DEPTH, P_DIM, D), P_DIM),
        'ple_norm_post': gain((DEPTH, D)),
    }


def reference(x, p, hgrn_lb_raw,
              e_norm_mix_pre, e_w_in, e_conv_w, e_conv_b, e_A_log, e_dt_bias, e_D,
              e_a_norm, e_b_norm, e_w_out, e_norm_mix_post,
              e_norm_ffn_pre, e_w_ffn_gate, e_w_ffn_up, e_w_ffn_down, e_norm_ffn_post,
              o_norm_mix_pre, o_w_qkv, o_rpb, o_w_out, o_norm_mix_post,
              o_norm_ffn_pre, o_w_router, o_w_exp_gate, o_w_exp_up, o_w_exp_down, o_norm_ffn_post,
              ple_norm_in, ple_w_gate, ple_w_proj, ple_norm_post):
    Bsz, S, D = x.shape
    lb_all = jnp.cumsum(jax.nn.softmax(hgrn_lb_raw.astype(jnp.float32), axis=0), axis=0)
    for li in range(DEPTH):
        j = li // 2
        if li % 2 == 0:
            h = rmsnorm(x, e_norm_mix_pre[j])
            u = h @ e_w_in[j]
            q_a, ff_a, fb_a, i_a, g_a, z_b, xbc_b, dtf_b, dtb_b = _split(u, EVEN_SPLITS)
            o_a = hgrn2_mixer(q_a, ff_a, fb_a, i_a, g_a, lb_all[li], e_a_norm[j])
            o_b = mamba2_mixer(z_b, xbc_b, dtf_b, dtb_b, e_conv_w[j], e_conv_b[j],
                               e_A_log[j], e_dt_bias[j], e_D[j], e_b_norm[j])
            mix = jnp.concatenate([o_a, o_b], axis=-1) @ e_w_out[j]
            x = x + rmsnorm(mix, e_norm_mix_post[j])
            h = rmsnorm(x, e_norm_ffn_pre[j])
            x = x + rmsnorm(swiglu(h, e_w_ffn_gate[j], e_w_ffn_up[j], e_w_ffn_down[j]), e_norm_ffn_post[j])
        else:
            h = rmsnorm(x, o_norm_mix_pre[j])
            q_c, k_c, v_c = jnp.split(h @ o_w_qkv[j], 3, axis=-1)
            heads = lambda t: t.reshape(Bsz, S, C_HEADS, C_HEADDIM)
            o_c = neighborhood_attention(heads(q_c), heads(k_c), heads(v_c), o_rpb[j])
            x = x + rmsnorm(o_c @ o_w_out[j], o_norm_mix_post[j])
            h = rmsnorm(x, o_norm_ffn_pre[j])
            moe = moe_swiglu(h, o_w_router[j], o_w_exp_gate[j], o_w_exp_up[j], o_w_exp_down[j])
            x = x + rmsnorm(moe, o_norm_ffn_post[j])
        gate = jax.nn.sigmoid(rmsnorm(x, ple_norm_in[li]) @ ple_w_gate[li])
        x = x + rmsnorm(gate * (p[li] @ ple_w_proj[li]), ple_norm_post[li])
    return x
```

```python
import functools
import math

import numpy as np
import jax
import jax.numpy as jnp
from jax import lax
from jax.experimental import pallas as pl
from jax.experimental.pallas import tpu as pltpu

F32 = jnp.float32
BF16 = jnp.bfloat16
EPS = 1e-6

LANES = 128
SUBLANES = 8
VMEM_LIMIT_BYTES = 56 * 1024 * 1024

GRID_W = 64
A_HEADS, A_DK, A_CHUNK = 4, 128, 64
B_HEADS, B_HEADDIM, B_GROUPS, B_STATE, B_CONV, B_CHUNK = 8, 64, 2, 128, 5, 128
B_HPG = B_HEADS // B_GROUPS
B_GW = B_HPG * B_HEADDIM
C_HEADS, C_HEADDIM = 16, 64
NA_ROWS, NA_COLS = 8, 16
NA_HG = 4
N_EXPERTS, TOP_K = 8, 2
MOE_TM = 256


def _cparams(sem):
    return pltpu.CompilerParams(dimension_semantics=sem, vmem_limit_bytes=VMEM_LIMIT_BYTES)


def _rms(x, g):
    return x * lax.rsqrt(jnp.mean(x * x, axis=-1, keepdims=True) + EPS) * g


def _silu(x):
    return x * jax.nn.sigmoid(x)


def _dot(a, b):
    return jnp.dot(a, b, preferred_element_type=F32)


def _dot_nt(a, b):
    return lax.dot_general(a, b, (((1,), (1,)), ((), ())), preferred_element_type=F32)


def _dot_tn(a, b):
    return lax.dot_general(a, b, (((0,), (0,)), ((), ())), preferred_element_type=F32)


def _norm_proj_kernel(x_ref, g_ref, w_ref, *o_refs, col_chunk):
    h = _rms(x_ref[...], g_ref[...]).astype(BF16)
    off = 0
    for o_ref in o_refs:
        n = o_ref.shape[1]
        for c0 in range(0, n, col_chunk):
            c1 = min(c0 + col_chunk, n)
            o_ref[:, c0:c1] = _dot(h, w_ref[:, off + c0:off + c1]).astype(o_ref.dtype)
        off += n


def _norm_proj(x, g, w, widths, dtypes, tm=512, col_chunk=1024):
    T, D = x.shape
    N = w.shape[1]
    assert sum(widths) == N and T % tm == 0
    return pl.pallas_call(
        functools.partial(_norm_proj_kernel, col_chunk=col_chunk),
        grid=(T // tm,),
        in_specs=[pl.BlockSpec((tm, D), lambda i: (i, 0)),
                  pl.BlockSpec((1, D), lambda i: (0, 0)),
                  pl.BlockSpec((D, N), lambda i: (0, 0))],
        out_specs=[pl.BlockSpec((tm, n), lambda i: (i, 0)) for n in widths],
        out_shape=[jax.ShapeDtypeStruct((T, n), dt) for n, dt in zip(widths, dtypes)],
        compiler_params=_cparams(("parallel",)),
        name="norm_proj",
    )(x, g.reshape(1, D), w)


def _proj_norm_res_kernel(*refs, n_in):
    a_refs, w_refs = refs[:n_in], refs[n_in:2 * n_in]
    g_ref, x_ref, o_ref = refs[2 * n_in:]
    acc = _dot(a_refs[0][...], w_refs[0][...])
    for a_ref, w_ref in zip(a_refs[1:], w_refs[1:]):
        acc = acc + _dot(a_ref[...], w_ref[...])
    o_ref[...] = x_ref[...] + _rms(acc, g_ref[...])


def _proj_norm_res(acts, ws, g, x, tm=512):
    T, D = x.shape
    n_in = len(acts)
    in_specs = ([pl.BlockSpec((tm, a.shape[1]), lambda i: (i, 0)) for a in acts]
                + [pl.BlockSpec(w.shape, lambda i: (0, 0)) for w in ws]
                + [pl.BlockSpec((1, D), lambda i: (0, 0)), pl.BlockSpec((tm, D), lambda i: (i, 0))])
    return pl.pallas_call(
        functools.partial(_proj_norm_res_kernel, n_in=n_in),
        grid=(T // tm,),
        in_specs=in_specs,
        out_specs=pl.BlockSpec((tm, D), lambda i: (i, 0)),
        out_shape=jax.ShapeDtypeStruct((T, D), F32),
        compiler_params=_cparams(("parallel",)),
        name="proj_norm_res",
    )(*acts, *ws, g.reshape(1, D), x)


def _ffn_kernel(x_ref, gpre_ref, wg_ref, wu_ref, wd_ref, gpost_ref, o_ref, h_scr, acc_scr):
    j = pl.program_id(1)

    @pl.when(j == 0)
    def _():
        h_scr[...] = _rms(x_ref[...], gpre_ref[...]).astype(BF16)
        acc_scr[...] = jnp.zeros_like(acc_scr)

    h = h_scr[...]
    a = (_silu(_dot(h, wg_ref[...])) * _dot(h, wu_ref[...])).astype(BF16)
    acc_scr[...] += _dot(a, wd_ref[...])

    @pl.when(j == pl.num_programs(1) - 1)
    def _():
        o_ref[...] = x_ref[...] + _rms(acc_scr[...], gpost_ref[...])


def _ffn(x, gpre, wg, wu, wd, gpost, tm=512, tf=1408):
    T, D = x.shape
    F = wg.shape[1]
    assert F % tf == 0 and T % tm == 0
    return pl.pallas_call(
        _ffn_kernel,
        grid=(T // tm, F // tf),
        in_specs=[pl.BlockSpec((tm, D), lambda i, j: (i, 0)),
                  pl.BlockSpec((1, D), lambda i, j: (0, 0)),
                  pl.BlockSpec((D, tf), lambda i, j: (0, j)),
                  pl.BlockSpec((D, tf), lambda i, j: (0, j)),
                  pl.BlockSpec((tf, D), lambda i, j: (j, 0)),
                  pl.BlockSpec((1, D), lambda i, j: (0, 0))],
        out_specs=pl.BlockSpec((tm, D), lambda i, j: (i, 0)),
        out_shape=jax.ShapeDtypeStruct((T, D), F32),
        scratch_shapes=[pltpu.VMEM((tm, D), BF16), pltpu.VMEM((tm, D), F32)],
        compiler_params=_cparams(("parallel", "arbitrary")),
        name="ffn",
    )(x, gpre.reshape(1, D), wg, wu, wd, gpost.reshape(1, D))


def _ple_kernel(x_ref, p_ref, gin_ref, wg_ref, wp_ref, gpost_ref, o_ref):
    x = x_ref[...]
    h = _rms(x, gin_ref[...]).astype(BF16)
    gate = jax.nn.sigmoid(_dot(h, wg_ref[...]))
    proj = _dot(p_ref[...].astype(BF16), wp_ref[...])
    o_ref[...] = x + _rms(gate * proj, gpost_ref[...])


def _ple(x, p, gin, wg, wp, gpost, tm=512):
    T, D = x.shape
    P = p.shape[1]
    return pl.pallas_call(
        _ple_kernel,
        grid=(T // tm,),
        in_specs=[pl.BlockSpec((tm, D), lambda i: (i, 0)),
                  pl.BlockSpec((tm, P), lambda i: (i, 0)),
                  pl.BlockSpec((1, D), lambda i: (0, 0)),
                  pl.BlockSpec((D, D), lambda i: (0, 0)),
                  pl.BlockSpec((P, D), lambda i: (0, 0)),
                  pl.BlockSpec((1, D), lambda i: (0, 0))],
        out_specs=pl.BlockSpec((tm, D), lambda i: (i, 0)),
        out_shape=jax.ShapeDtypeStruct((T, D), F32),
        compiler_params=_cparams(("parallel",)),
        name="ple",
    )(x, p, gin.reshape(1, D), wg, wp, gpost.reshape(1, D))


def _roll_rows(x, s, rev):
    n = x.shape[0]
    return pltpu.roll(x, (n - s) if rev else s, 0)


def _cumsum_rows(x, tau, rev):
    n = x.shape[0]
    s = 1
    while s < n:
        x = x + jnp.where(tau >= s, _roll_rows(x, s, rev), 0.0)
        s *= 2
    return x


def _hgrn_kernel(q_ref, ff_ref, fb_ref, i_ref, g_ref, lb_ref, ng_ref, o_ref, acc_scr, st_scr):
    S, DK = q_ref.shape
    C = A_CHUNK
    nc = S // C
    lb = lb_ref[0]
    ng = ng_ref[...]
    row = lax.broadcasted_iota(jnp.int32, (C, DK), 0)
    t2 = lax.broadcasted_iota(jnp.int32, (C, C), 0)
    s2 = lax.broadcasted_iota(jnp.int32, (C, C), 1)
    levels = [L for L in (32, 16, 8) if 2 * L <= C]
    same_blk = {L: (t2 & -(2 * L)) == (s2 & -(2 * L)) for L in levels if 2 * L < C}

    def run(rev, f_ref, final):
        tau = (C - 1 - row) if rev else row
        upper_half = {L: (tau & (2 * L - 1)) >= L for L in levels}
        near = [(tau[:, :1] & (SUBLANES - 1)) >= d for d in range(SUBLANES)]
        last = 0 if rev else C - 1
        st_scr[...] = jnp.zeros_like(st_scr)

        def body(ci, carry):
            c = (nc - 1 - ci) if rev else ci
            r0 = pl.multiple_of(c * C, C)
            rows = pl.ds(r0, C)
            q = _silu(q_ref[rows, :].astype(F32))
            f = lb + (1.0 - lb) * jax.nn.sigmoid(f_ref[rows, :].astype(F32))
            k = 1.0 - f
            v = i_ref[rows, :].astype(F32)
            vb = v.astype(BF16)
            b = _cumsum_rows(jnp.log(f), tau, rev)

            st = st_scr[...]
            o = _dot_nt((q * jnp.exp(b)).astype(BF16), st.astype(BF16))

            attn = jnp.zeros((C, C), F32)
            for L in levels:
                pieces = []
                for j in range(C // (2 * L)):
                    ref_row = j * 2 * L + (L if rev else L - 1)
                    pieces.append(jnp.broadcast_to(b[ref_row:ref_row + 1, :], (2 * L, DK)))
                ref = pieces[0] if len(pieces) == 1 else jnp.concatenate(pieces, axis=0)
                upper = upper_half[L]
                qt = jnp.where(upper, q * jnp.exp(jnp.minimum(b - ref, 0.0)), 0.0).astype(BF16)
                kt = jnp.where(upper, 0.0, k * jnp.exp(jnp.minimum(ref - b, 0.0))).astype(BF16)
                part = _dot_nt(qt, kt)
                if L in same_blk:
                    part = jnp.where(same_blk[L], part, 0.0)
                attn = attn + part
            o = o + _dot(attn.astype(BF16), vb)

            e = None
            for d in range(SUBLANES):
                kd = k if d == 0 else _roll_rows(k, d, rev)
                vd = v if d == 0 else _roll_rows(v, d, rev)
                if d == 1:
                    e = f
                elif d > 1:
                    e = e * _roll_rows(f, d - 1, rev)
                w = q * kd if d == 0 else q * kd * e
                a = jnp.sum(w, axis=-1, keepdims=True)
                a = jnp.where(near[d], a, 0.0)
                o = o + a * vd

            b_last = b[last:last + 1, :]
            khat = (k * jnp.exp(b_last - b)).astype(BF16)
            st_scr[...] = st * jnp.exp(b_last) + _dot_tn(vb, khat)

            if final:
                tot = acc_scr[rows, :] + o
                gate = _silu(g_ref[rows, :].astype(F32))
                o_ref[rows, :] = (_rms(tot, ng) * gate).astype(o_ref.dtype)
            else:
                acc_scr[rows, :] = o
            return carry

        lax.fori_loop(0, nc, body, 0)

    run(False, ff_ref, False)
    run(True, fb_ref, True)


def _hgrn_mixer(u, lb, norm_g, batch, seq):
    T = u.shape[0]
    H, DK = A_HEADS, A_DK

    def spec(off):
        return pl.BlockSpec((seq, DK), lambda b, h, off=off: (b, off + h))

    return pl.pallas_call(
        _hgrn_kernel,
        grid=(batch, H),
        in_specs=[spec(0), spec(H), spec(2 * H), spec(3 * H), spec(4 * H),
                  pl.BlockSpec((1, 1, DK), lambda b, h: (h, 0, 0)),
                  pl.BlockSpec((1, DK), lambda b, h: (0, 0))],
        out_specs=pl.BlockSpec((seq, DK), lambda b, h: (b, h)),
        out_shape=jax.ShapeDtypeStruct((T, H * DK), BF16),
        scratch_shapes=[pltpu.VMEM((seq, DK), F32), pltpu.VMEM((DK, DK), F32)],
        compiler_params=_cparams(("parallel", "parallel")),
        name="hgrn2",
    )(u, u, u, u, u, lb.reshape(H, 1, DK), norm_g.reshape(1, DK))


def _expand_heads(col, j0, width):
    q = col.shape[0]
    lane = lax.broadcasted_iota(jnp.int32, (q, width), 1)
    out = jnp.broadcast_to(col[:, j0 + B_HPG - 1:j0 + B_HPG], (q, width))
    for j in range(B_HPG - 2, -1, -1):
        out = jnp.where(lane < (j + 1) * B_HEADDIM,
                        jnp.broadcast_to(col[:, j0 + j:j0 + j + 1], (q, width)), out)
    return out


def _ssd_kernel(z_ref, x_ref, bm_ref, cm_ref, dt_ref, cwx_ref, cwb_ref, cwc_ref,
                cbx_ref, cbb_ref, cbc_ref, hp_ref, dsk_ref, ng_ref, o_ref,
                xs_scr, b_scr, c_scr, y_scr, st_scr):
    S = x_ref.shape[0]
    Q = B_CHUNK
    nc = S // Q
    GW = B_GW
    hp = hp_ref[0]
    a_row, dtb_row = hp[0:1, :], hp[1:2, :]
    row = lax.broadcasted_iota(jnp.int32, (Q, LANES), 0)
    t2 = lax.broadcasted_iota(jnp.int32, (Q, Q), 0)
    s2 = lax.broadcasted_iota(jnp.int32, (Q, Q), 1)
    lane_gw = lax.broadcasted_iota(jnp.int32, (Q, GW), 1)
    halo = SUBLANES

    def conv_body(c, carry):
        r0 = pl.multiple_of(c * Q, Q)
        rp = pl.multiple_of(jnp.maximum(r0 - halo, 0), halo)
        rn = pl.multiple_of(jnp.minimum(r0 + Q, S - halo), halo)
        has_prev = c > 0
        has_next = c < nc - 1

        def conv(src_ref, w_ref, bias_ref):
            prev = jnp.where(has_prev, src_ref[pl.ds(rp, halo), :].astype(F32), 0.0)
            nxt = jnp.where(has_next, src_ref[pl.ds(rn, halo), :].astype(F32), 0.0)
            xx = jnp.concatenate([prev, src_ref[pl.ds(r0, Q), :].astype(F32), nxt], axis=0)
            n = Q + 2 * halo
            w = w_ref[0]
            acc = None
            for j in range(B_CONV):
                delta = j - B_CONV // 2
                sh = xx if delta == 0 else pltpu.roll(xx, (-delta) % n, 0)
                term = sh[halo:halo + Q, :] * w[j:j + 1, :]
                acc = term if acc is None else acc + term
            return _silu(acc + bias_ref[0])

        xs_scr[pl.ds(r0, Q), :] = conv(x_ref, cwx_ref, cbx_ref)
        b_scr[pl.ds(r0, Q), :] = conv(bm_ref, cwb_ref, cbb_ref).astype(BF16)
        c_scr[pl.ds(r0, Q), :] = conv(cm_ref, cwc_ref, cbc_ref).astype(BF16)
        return carry

    lax.fori_loop(0, nc, conv_body, 0)

    def run(rev, final):
        tau = (Q - 1 - row) if rev else row
        last = 0 if rev else Q - 1
        j0 = B_HPG if rev else 0
        pair_ok = (s2 >= t2) if rev else (t2 >= s2)
        st_scr[...] = jnp.zeros_like(st_scr)

        def body(ci, carry):
            c = (nc - 1 - ci) if rev else ci
            r0 = pl.multiple_of(c * Q, Q)
            rows = pl.ds(r0, Q)
            dt = jax.nn.softplus(dt_ref[rows, :] + dtb_row)
            cs = _cumsum_rows(dt * a_row, tau, rev)
            cs_t = cs.T
            xs = xs_scr[rows, :]
            bm = b_scr[rows, :]
            cm = c_scr[rows, :]
            xdt = xs * _expand_heads(dt, j0, GW)
            ecs = _expand_heads(cs, j0, GW)
            ecs_last = ecs[last:last + 1, :]

            st = st_scr[...]
            y = _dot(cm, st.astype(BF16)) * jnp.exp(ecs)

            scores = _dot_nt(cm, bm)
            for j in range(B_HPG):
                col = jnp.broadcast_to(cs[:, j0 + j:j0 + j + 1], (Q, Q))
                rw = jnp.broadcast_to(cs_t[j0 + j:j0 + j + 1, :], (Q, Q))
                decay = jnp.where(pair_ok, jnp.exp(jnp.minimum(col - rw, 0.0)), 0.0)
                in_head = (lane_gw >= j * B_HEADDIM) & (lane_gw < (j + 1) * B_HEADDIM)
                xh = jnp.where(in_head, xdt, 0.0).astype(BF16)
                y = y + _dot((scores * decay).astype(BF16), xh)

            xdec = (xdt * jnp.exp(ecs_last - ecs)).astype(BF16)
            st_scr[...] = st * jnp.exp(ecs_last) + _dot_tn(bm, xdec)

            if final:
                tot = y_scr[rows, :] + y + dsk_ref[0] * xs
                tot = tot * _silu(z_ref[rows, :].astype(F32))
                o_ref[rows, :] = _rms(tot, ng_ref[0]).astype(o_ref.dtype)
            else:
                y_scr[rows, :] = y
            return carry

        lax.fori_loop(0, nc, body, 0)

    run(False, False)
    run(True, True)


def _ssd_mixer(u, dt, conv_w, conv_b, a_log, dt_bias, d_skip, norm_g, batch, seq, col0):
    T = u.shape[0]
    G, GW, N = B_GROUPS, B_GW, B_STATE
    W = B_HEADS * B_HEADDIM
    assert col0 % GW == 0 and GW == 2 * N
    z_blk = col0 // GW
    x_blk = z_blk + W // GW
    b_blk = (col0 + 2 * W) // N
    c_blk = b_blk + G

    def pad_rows(w):
        return jnp.pad(w, ((0, SUBLANES - w.shape[0]), (0, 0)))

    cwx = pad_rows(conv_w[:, :W]).reshape(SUBLANES, G, GW).transpose(1, 0, 2)
    cwb = pad_rows(conv_w[:, W:W + G * N]).reshape(SUBLANES, G, N).transpose(1, 0, 2)
    cwc = pad_rows(conv_w[:, W + G * N:]).reshape(SUBLANES, G, N).transpose(1, 0, 2)
    cbx = conv_b[:W].reshape(G, 1, GW)
    cbb = conv_b[W:W + G * N].reshape(G, 1, N)
    cbc = conv_b[W + G * N:].reshape(G, 1, N)
    a_neg = -jnp.exp(a_log.astype(F32))
    per_dir = lambda v: jnp.concatenate([v[0].reshape(G, B_HPG), v[1].reshape(G, B_HPG)], axis=1)
    hp = jnp.stack([per_dir(a_neg), per_dir(dt_bias.astype(F32))], axis=1)
    hp = jnp.pad(hp, ((0, 0), (0, SUBLANES - 2), (0, LANES - 2 * B_HPG)))
    dsk = jnp.repeat(d_skip.astype(F32), B_HEADDIM).reshape(G, 1, GW)
    ng = norm_g.reshape(G, 1, GW)

    gspec = lambda shape: pl.BlockSpec((1,) + shape, lambda b, g: (g, 0, 0))
    return pl.pallas_call(
        _ssd_kernel,
        grid=(batch, G),
        in_specs=[pl.BlockSpec((seq, GW), lambda b, g: (b, z_blk + g)),
                  pl.BlockSpec((seq, GW), lambda b, g: (b, x_blk + g)),
                  pl.BlockSpec((seq, N), lambda b, g: (b, b_blk + g)),
                  pl.BlockSpec((seq, N), lambda b, g: (b, c_blk + g)),
                  pl.BlockSpec((seq, LANES), lambda b, g: (b, g)),
                  gspec((SUBLANES, GW)), gspec((SUBLANES, N)), gspec((SUBLANES, N)),
                  gspec((1, GW)), gspec((1, N)), gspec((1, N)),
                  gspec((SUBLANES, LANES)), gspec((1, GW)), gspec((1, GW))],
        out_specs=pl.BlockSpec((seq, GW), lambda b, g: (b, g)),
        out_shape=jax.ShapeDtypeStruct((T, W), BF16),
        scratch_shapes=[pltpu.VMEM((seq, GW), F32), pltpu.VMEM((seq, N), BF16),
                        pltpu.VMEM((seq, N), BF16), pltpu.VMEM((seq, GW), F32),
                        pltpu.VMEM((N, GW), F32)],
        compiler_params=_cparams(("parallel", "parallel")),
        name="ssd",
    )(u, u, u, u, dt, cwx, cwb, cwc, cbx, cbb, cbc, hp, dsk, ng)


def _na_kernel(q_ref, k_ref, v_ref, bias_ref, o_ref):
    S, HW = q_ref.shape
    W = GRID_W
    n_rows = S // W
    kh = min(NA_ROWS, n_rows)
    nk = kh * W
    lane = lax.broadcasted_iota(jnp.int32, (W, HW), 1)
    head_masks = [(lane >= h * C_HEADDIM) & (lane < (h + 1) * C_HEADDIM) for h in range(NA_HG)]
    scale = C_HEADDIM ** -0.5

    def body(r, carry):
        rs = jnp.clip(r - kh // 2, 0, n_rows - kh)
        q = q_ref[pl.ds(pl.multiple_of(r * W, W), W), :] * jnp.asarray(scale, q_ref.dtype)
        kw = k_ref[pl.ds(pl.multiple_of(rs * W, W), nk), :]
        vw = v_ref[pl.ds(pl.multiple_of(rs * W, W), nk), :]
        zero = jnp.zeros_like(q)
        qs = jnp.concatenate([jnp.where(m, q, zero) for m in head_masks], axis=0)
        s = _dot_nt(qs, kw) + bias_ref[0, r - rs]
        m = jnp.max(s, axis=-1, keepdims=True)
        p = jnp.exp(s - m)
        l = jnp.sum(p, axis=-1, keepdims=True)
        res = _dot(p.astype(BF16), vw) / l
        out = jnp.zeros((W, HW), F32)
        for h in range(NA_HG):
            out = jnp.where(head_masks[h], res[h * W:(h + 1) * W, :], out)
        o_ref[pl.ds(pl.multiple_of(r * W, W), W), :] = out.astype(o_ref.dtype)
        return carry

    lax.fori_loop(0, n_rows, body, 0)


def _na_bias_table(rpb, n_rows):
    H = rpb.shape[0]
    W = GRID_W
    kh = min(NA_ROWS, n_rows)
    var = np.arange(kh)
    dr = np.arange(kh)[None, :] - var[:, None] + (NA_ROWS - 1)
    c = np.arange(W)
    qs = np.clip(c - NA_COLS // 2, 0, W - NA_COLS)
    valid = (c[None, :] >= qs[:, None]) & (c[None, :] < qs[:, None] + NA_COLS)
    dc = np.clip(c[None, :] - c[:, None] + (NA_COLS - 1), 0, 2 * NA_COLS - 2)
    tab = rpb.astype(F32)[:, dr][:, :, :, dc]
    tab = jnp.where(jnp.asarray(valid)[None, None, None], tab, -jnp.inf)
    tab = tab.transpose(0, 1, 3, 2, 4).reshape(H // NA_HG, NA_HG, kh, W, kh * W)
    return tab.transpose(0, 2, 1, 3, 4).reshape(H // NA_HG, kh, NA_HG * W, kh * W)


def _neighborhood_attention(qkv, rpb, batch, seq):
    T = qkv.shape[0]
    HW = NA_HG * C_HEADDIM
    n_hg = C_HEADS // NA_HG
    n_rows = seq // GRID_W
    kh = min(NA_ROWS, n_rows)
    table = _na_bias_table(rpb, n_rows)
    return pl.pallas_call(
        _na_kernel,
        grid=(batch, n_hg),
        in_specs=[pl.BlockSpec((seq, HW), lambda b, g: (b, g)),
                  pl.BlockSpec((seq, HW), lambda b, g: (b, n_hg + g)),
                  pl.BlockSpec((seq, HW), lambda b, g: (b, 2 * n_hg + g)),
                  pl.BlockSpec((1, kh, NA_HG * GRID_W, kh * GRID_W), lambda b, g: (g, 0, 0, 0))],
        out_specs=pl.BlockSpec((seq, HW), lambda b, g: (b, g)),
        out_shape=jax.ShapeDtypeStruct((T, C_HEADS * C_HEADDIM), BF16),
        compiler_params=_cparams(("parallel", "parallel")),
        name="natten",
    )(qkv, qkv, qkv, table)


def _router_kernel(x_ref, g_ref, wr_ref, o_ref):
    h = _rms(x_ref[...], g_ref[...])
    logits = jnp.dot(h, wr_ref[...], preferred_element_type=F32, precision=lax.Precision.HIGHEST)
    lane = lax.broadcasted_iota(jnp.int32, logits.shape, 1)
    neg = -jnp.inf
    l1 = jnp.where(lane < N_EXPERTS, logits, neg)
    m1 = jnp.max(l1, axis=-1, keepdims=True)
    i1 = jnp.min(jnp.where(l1 == m1, lane, LANES), axis=-1, keepdims=True)
    l2 = jnp.where(lane == i1, neg, l1)
    m2 = jnp.max(l2, axis=-1, keepdims=True)
    i2 = jnp.min(jnp.where(l2 == m2, lane, LANES), axis=-1, keepdims=True)
    e = jnp.exp(m2 - m1)
    w1 = 1.0 / (1.0 + e)
    w2 = e / (1.0 + e)
    out = jnp.where(lane == 0, i1.astype(F32),
                    jnp.where(lane == 1, i2.astype(F32),
                              jnp.where(lane == 2, w1, jnp.where(lane == 3, w2, 0.0))))
    o_ref[...] = out


def _router(x, g, w_router, tm=512):
    T, D = x.shape
    wr = jnp.pad(w_router.astype(F32), ((0, 0), (0, LANES - w_router.shape[1])))
    return pl.pallas_call(
        _router_kernel,
        grid=(T // tm,),
        in_specs=[pl.BlockSpec((tm, D), lambda i: (i, 0)),
                  pl.BlockSpec((1, D), lambda i: (0, 0)),
                  pl.BlockSpec((D, LANES), lambda i: (0, 0))],
        out_specs=pl.BlockSpec((tm, LANES), lambda i: (i, 0)),
        out_shape=jax.ShapeDtypeStruct((T, LANES), F32),
        compiler_params=_cparams(("parallel",)),
        name="router",
    )(x, g.reshape(1, D), wr)


def _moe_ffn_kernel(te_ref, na_ref, src_ref, x_hbm, g_ref, wg_ref, wu_ref, wd_ref, o_ref,
                    xs_scr, sem, *, tf):
    i = pl.program_id(0)
    tm = xs_scr.shape[0]
    F = wg_ref.shape[2]

    def row_copy(r):
        tok = src_ref[0, 0, r]
        return pltpu.make_async_copy(x_hbm.at[pl.ds(tok, 1)], xs_scr.at[pl.ds(r, 1)], sem)

    @pl.when(i < na_ref[0])
    def _():
        def issue(r, carry):
            row_copy(r).start()
            return carry

        lax.fori_loop(0, tm, issue, 0)

        def drain(r, carry):
            row_copy(r).wait()
            return carry

        lax.fori_loop(0, tm, drain, 0)
        h = _rms(xs_scr[...], g_ref[...]).astype(BF16)
        for f0 in range(0, F, tf):
            a = (_silu(_dot(h, wg_ref[0, :, f0:f0 + tf])) * _dot(h, wu_ref[0, :, f0:f0 + tf])).astype(BF16)
            part = _dot(a, wd_ref[0, f0:f0 + tf, :])
            if f0 == 0:
                o_ref[...] = part
            else:
                o_ref[...] += part

    @pl.when(i >= na_ref[0])
    def _():
        o_ref[...] = jnp.zeros_like(o_ref)


def _moe_ffn(x, g, wg, wu, wd, tile_expert, n_active, src, tm=MOE_TM, tf=512):
    T, D = x.shape
    E, _, F = wg.shape
    nt = tile_expert.shape[0]
    grid_spec = pltpu.PrefetchScalarGridSpec(
        num_scalar_prefetch=2,
        grid=(nt,),
        in_specs=[pl.BlockSpec((1, 1, tm), lambda i, te, na: (i, 0, 0), memory_space=pltpu.SMEM),
                  pl.BlockSpec(memory_space=pl.ANY),
                  pl.BlockSpec((1, D), lambda i, te, na: (0, 0)),
                  pl.BlockSpec((1, D, F), lambda i, te, na: (te[i], 0, 0)),
                  pl.BlockSpec((1, D, F), lambda i, te, na: (te[i], 0, 0)),
                  pl.BlockSpec((1, F, D), lambda i, te, na: (te[i], 0, 0))],
        out_specs=pl.BlockSpec((tm, D), lambda i, te, na: (i, 0)),
        scratch_shapes=[pltpu.VMEM((tm, D), F32), pltpu.SemaphoreType.DMA(())],
    )
    return pl.pallas_call(
        functools.partial(_moe_ffn_kernel, tf=tf),
        grid_spec=grid_spec,
        out_shape=jax.ShapeDtypeStruct((nt * tm, D), F32),
        compiler_params=_cparams(("arbitrary",)),
        name="moe_ffn",
    )(tile_expert, n_active, src.reshape(nt, 1, tm), x, g.reshape(1, D), wg, wu, wd)


def _moe_combine_kernel(pos_ref, y_hbm, route_ref, x_ref, g_ref, o_ref, y0_scr, y1_scr, sem):
    tm = x_ref.shape[0]

    def row_copies(r):
        c0 = pltpu.make_async_copy(y_hbm.at[pl.ds(pos_ref[0, 0, r], 1)], y0_scr.at[pl.ds(r, 1)], sem.at[0])
        c1 = pltpu.make_async_copy(y_hbm.at[pl.ds(pos_ref[0, 1, r], 1)], y1_scr.at[pl.ds(r, 1)], sem.at[1])
        return c0, c1

    def issue(r, carry):
        c0, c1 = row_copies(r)
        c0.start()
        c1.start()
        return carry

    lax.fori_loop(0, tm, issue, 0)

    def drain(r, carry):
        c0, c1 = row_copies(r)
        c0.wait()
        c1.wait()
        return carry

    lax.fori_loop(0, tm, drain, 0)
    route = route_ref[...]
    moe = route[:, 2:3] * y0_scr[...] + route[:, 3:4] * y1_scr[...]
    o_ref[...] = x_ref[...] + _rms(moe, g_ref[...])


def _moe_combine(y_sorted, pos, route, x, g, tm=256):
    T, D = x.shape
    nt = T // tm
    pos_t = pos.reshape(nt, tm, TOP_K).transpose(0, 2, 1)
    return pl.pallas_call(
        _moe_combine_kernel,
        grid=(nt,),
        in_specs=[pl.BlockSpec((1, TOP_K, tm), lambda i: (i, 0, 0), memory_space=pltpu.SMEM),
                  pl.BlockSpec(memory_space=pl.ANY),
                  pl.BlockSpec((tm, LANES), lambda i: (i, 0)),
                  pl.BlockSpec((tm, D), lambda i: (i, 0)),
                  pl.BlockSpec((1, D), lambda i: (0, 0))],
        out_specs=pl.BlockSpec((tm, D), lambda i: (i, 0)),
        out_shape=jax.ShapeDtypeStruct((T, D), F32),
        scratch_shapes=[pltpu.VMEM((tm, D), F32), pltpu.VMEM((tm, D), F32),
                        pltpu.SemaphoreType.DMA((2,))],
        compiler_params=_cparams(("arbitrary",)),
        name="moe_combine",
    )(pos_t, y_sorted, route, x, g.reshape(1, D))


def _moe_plan(route, tm=MOE_TM):
    T = route.shape[0]
    e = route[:, :TOP_K].astype(jnp.int32).reshape(-1)
    onehot = (e[:, None] == jnp.arange(N_EXPERTS, dtype=jnp.int32)[None, :]).astype(jnp.int32)
    rank = jnp.take_along_axis(jnp.cumsum(onehot, axis=0), e[:, None], axis=1)[:, 0] - 1
    counts = jnp.sum(onehot, axis=0)
    padded = ((counts + tm - 1) // tm) * tm
    ends = jnp.cumsum(padded)
    starts = ends - padded
    dest = starts[e] + rank
    nt = (T * TOP_K) // tm + N_EXPERTS
    token = jnp.arange(T * TOP_K, dtype=jnp.int32) // TOP_K
    src = jnp.zeros((nt * tm,), jnp.int32).at[dest].set(token)
    tile_start = jnp.arange(nt, dtype=jnp.int32) * tm
    tile_expert = jnp.minimum(jnp.sum((tile_start[:, None] >= ends[None, :]).astype(jnp.int32), axis=1),
                              N_EXPERTS - 1).astype(jnp.int32)
    n_active = (ends[-1] // tm).astype(jnp.int32).reshape(1)
    return tile_expert, n_active, src, dest.reshape(T, TOP_K).astype(jnp.int32)


def kernel(x, p, hgrn_lb_raw, e_norm_mix_pre, e_w_in, e_conv_w, e_conv_b, e_A_log, e_dt_bias, e_D, e_a_norm, e_b_norm, e_w_out, e_norm_mix_post, e_norm_ffn_pre, e_w_ffn_gate, e_w_ffn_up, e_w_ffn_down, e_norm_ffn_post, o_norm_mix_pre, o_w_qkv, o_rpb, o_w_out, o_norm_mix_post, o_norm_ffn_pre, o_w_router, o_w_exp_gate, o_w_exp_up, o_w_exp_down, o_norm_ffn_post, ple_norm_in, ple_w_gate, ple_w_proj, ple_norm_post):
    batch, seq, d_model = x.shape
    depth = p.shape[0]
    T = batch * seq
    xt = x.reshape(T, d_model)
    lb_all = jnp.cumsum(jax.nn.softmax(hgrn_lb_raw.astype(F32), axis=0), axis=0)

    a_kdim = A_HEADS * A_DK
    b_width = B_HEADS * B_HEADDIM
    conv_dim = b_width + 2 * B_GROUPS * B_STATE
    main_w = 5 * a_kdim + b_width + conv_dim

    for li in range(depth):
        j = li // 2
        if li % 2 == 0:
            w_in = e_w_in[j]
            dtf = w_in[:, main_w:main_w + B_HEADS].reshape(d_model, B_GROUPS, B_HPG)
            dtb = w_in[:, main_w + B_HEADS:].reshape(d_model, B_GROUPS, B_HPG)
            w_dt = jnp.pad(jnp.concatenate([dtf, dtb], axis=2), ((0, 0), (0, 0), (0, LANES - 2 * B_HPG)))
            w_all = jnp.concatenate([w_in[:, :main_w], w_dt.reshape(d_model, B_GROUPS * LANES)], axis=1)
            u, dt = _norm_proj(xt, e_norm_mix_pre[j], w_all.astype(BF16),
                               (main_w, B_GROUPS * LANES), (BF16, F32))
            o_a = _hgrn_mixer(u, lb_all[li], e_a_norm[j], batch, seq)
            o_b = _ssd_mixer(u, dt, e_conv_w[j], e_conv_b[j], e_A_log[j], e_dt_bias[j], e_D[j],
                             e_b_norm[j], batch, seq, 5 * a_kdim)
            w_out = e_w_out[j].astype(BF16)
            xt = _proj_norm_res([o_a, o_b], [w_out[:a_kdim], w_out[a_kdim:]], e_norm_mix_post[j], xt)
            xt = _ffn(xt, e_norm_ffn_pre[j], e_w_ffn_gate[j].astype(BF16), e_w_ffn_up[j].astype(BF16),
                      e_w_ffn_down[j].astype(BF16), e_norm_ffn_post[j])
        else:
            (qkv,) = _norm_proj(xt, o_norm_mix_pre[j], o_w_qkv[j].astype(BF16),
                                (3 * C_HEADS * C_HEADDIM,), (BF16,))
            o_c = _neighborhood_attention(qkv, o_rpb[j], batch, seq)
            xt = _proj_norm_res([o_c], [o_w_out[j].astype(BF16)], o_norm_mix_post[j], xt)
            route = _router(xt, o_norm_ffn_pre[j], o_w_router[j])
            tile_expert, n_active, src, pos = _moe_plan(route)
            y_sorted = _moe_ffn(xt, o_norm_ffn_pre[j], o_w_exp_gate[j].astype(BF16),
                                o_w_exp_up[j].astype(BF16), o_w_exp_down[j].astype(BF16),
                                tile_expert, n_active, src)
            xt = _moe_combine(y_sorted, pos, route, xt, o_norm_ffn_post[j])
        xt = _ple(xt, p[li].reshape(T, -1), ple_norm_in[li], ple_w_gate[li].astype(BF16),
                  ple_w_proj[li].astype(BF16), ple_norm_post[li])
    return xt.reshape(batch, seq, d_model)
```

```python
import functools
import math

import numpy as np
import jax
import jax.numpy as jnp
from jax import lax
from jax.experimental import pallas as pl
from jax.experimental.pallas import tpu as pltpu
from jax.experimental.pallas import tpu_sc as plsc

F32 = jnp.float32
BF16 = jnp.bfloat16
EPS = 1e-6

LANES = 128
SUBLANES = 8
VMEM_LIMIT_BYTES = 56 * 1024 * 1024

GRID_W = 64
A_HEADS, A_DK, A_CHUNK = 4, 128, 64
B_HEADS, B_HEADDIM, B_GROUPS, B_STATE, B_CONV, B_CHUNK = 8, 64, 2, 128, 5, 128
B_HPG = B_HEADS // B_GROUPS
B_GW = B_HPG * B_HEADDIM
C_HEADS, C_HEADDIM = 16, 64
NA_ROWS, NA_COLS = 8, 16
NA_HG = 4
N_EXPERTS, TOP_K = 8, 2
MOE_TM = 256


def _cparams(sem):
    return pltpu.CompilerParams(dimension_semantics=sem, vmem_limit_bytes=VMEM_LIMIT_BYTES)


def _rms(x, g):
    return x * lax.rsqrt(jnp.mean(x * x, axis=-1, keepdims=True) + EPS) * g


def _silu(x):
    return x * jax.nn.sigmoid(x)


def _dot(a, b):
    return jnp.dot(a, b, preferred_element_type=F32)


def _dot_nt(a, b):
    return lax.dot_general(a, b, (((1,), (1,)), ((), ())), preferred_element_type=F32)


def _dot_tn(a, b):
    return lax.dot_general(a, b, (((0,), (0,)), ((), ())), preferred_element_type=F32)


def _norm_proj_kernel(x_ref, g_ref, w_ref, *o_refs, col_chunk):
    h = _rms(x_ref[...], g_ref[...]).astype(BF16)
    off = 0
    for o_ref in o_refs:
        n = o_ref.shape[1]
        for c0 in range(0, n, col_chunk):
            c1 = min(c0 + col_chunk, n)
            o_ref[:, c0:c1] = _dot(h, w_ref[:, off + c0:off + c1]).astype(o_ref.dtype)
        off += n


def _norm_proj(x, g, w, widths, dtypes, tm=512, col_chunk=1024):
    T, D = x.shape
    N = w.shape[1]
    assert sum(widths) == N and T % tm == 0
    return pl.pallas_call(
        functools.partial(_norm_proj_kernel, col_chunk=col_chunk),
        grid=(T // tm,),
        in_specs=[pl.BlockSpec((tm, D), lambda i: (i, 0)),
                  pl.BlockSpec((1, D), lambda i: (0, 0)),
                  pl.BlockSpec((D, N), lambda i: (0, 0))],
        out_specs=[pl.BlockSpec((tm, n), lambda i: (i, 0)) for n in widths],
        out_shape=[jax.ShapeDtypeStruct((T, n), dt) for n, dt in zip(widths, dtypes)],
        compiler_params=_cparams(("parallel",)),
        name="norm_proj",
    )(x, g.reshape(1, D), w)


def _proj_norm_res_kernel(*refs, n_in):
    a_refs, w_refs = refs[:n_in], refs[n_in:2 * n_in]
    g_ref, x_ref, o_ref = refs[2 * n_in:]
    acc = _dot(a_refs[0][...], w_refs[0][...])
    for a_ref, w_ref in zip(a_refs[1:], w_refs[1:]):
        acc = acc + _dot(a_ref[...], w_ref[...])
    o_ref[...] = x_ref[...] + _rms(acc, g_ref[...])


def _proj_norm_res(acts, ws, g, x, tm=512):
    T, D = x.shape
    n_in = len(acts)
    in_specs = ([pl.BlockSpec((tm, a.shape[1]), lambda i: (i, 0)) for a in acts]
                + [pl.BlockSpec(w.shape, lambda i: (0, 0)) for w in ws]
                + [pl.BlockSpec((1, D), lambda i: (0, 0)), pl.BlockSpec((tm, D), lambda i: (i, 0))])
    return pl.pallas_call(
        functools.partial(_proj_norm_res_kernel, n_in=n_in),
        grid=(T // tm,),
        in_specs=in_specs,
        out_specs=pl.BlockSpec((tm, D), lambda i: (i, 0)),
        out_shape=jax.ShapeDtypeStruct((T, D), F32),
        compiler_params=_cparams(("parallel",)),
        name="proj_norm_res",
    )(*acts, *ws, g.reshape(1, D), x)


def _ffn_kernel(x_ref, gpre_ref, wg_ref, wu_ref, wd_ref, gpost_ref, o_ref, h_scr, acc_scr):
    j = pl.program_id(1)

    @pl.when(j == 0)
    def _():
        h_scr[...] = _rms(x_ref[...], gpre_ref[...]).astype(BF16)
        acc_scr[...] = jnp.zeros_like(acc_scr)

    h = h_scr[...]
    a = (_silu(_dot(h, wg_ref[...])) * _dot(h, wu_ref[...])).astype(BF16)
    acc_scr[...] += _dot(a, wd_ref[...])

    @pl.when(j == pl.num_programs(1) - 1)
    def _():
        o_ref[...] = x_ref[...] + _rms(acc_scr[...], gpost_ref[...])


def _ffn(x, gpre, wg, wu, wd, gpost, tm=512, tf=1408):
    T, D = x.shape
    F = wg.shape[1]
    assert F % tf == 0 and T % tm == 0
    return pl.pallas_call(
        _ffn_kernel,
        grid=(T // tm, F // tf),
        in_specs=[pl.BlockSpec((tm, D), lambda i, j: (i, 0)),
                  pl.BlockSpec((1, D), lambda i, j: (0, 0)),
                  pl.BlockSpec((D, tf), lambda i, j: (0, j)),
                  pl.BlockSpec((D, tf), lambda i, j: (0, j)),
                  pl.BlockSpec((tf, D), lambda i, j: (j, 0)),
                  pl.BlockSpec((1, D), lambda i, j: (0, 0))],
        out_specs=pl.BlockSpec((tm, D), lambda i, j: (i, 0)),
        out_shape=jax.ShapeDtypeStruct((T, D), F32),
        scratch_shapes=[pltpu.VMEM((tm, D), BF16), pltpu.VMEM((tm, D), F32)],
        compiler_params=_cparams(("parallel", "arbitrary")),
        name="ffn",
    )(x, gpre.reshape(1, D), wg, wu, wd, gpost.reshape(1, D))


def _ple_kernel(x_ref, p_ref, gin_ref, wg_ref, wp_ref, gpost_ref, o_ref):
    x = x_ref[...]
    h = _rms(x, gin_ref[...]).astype(BF16)
    gate = jax.nn.sigmoid(_dot(h, wg_ref[...]))
    proj = _dot(p_ref[...].astype(BF16), wp_ref[...])
    o_ref[...] = x + _rms(gate * proj, gpost_ref[...])


def _ple(x, p, gin, wg, wp, gpost, tm=512):
    T, D = x.shape
    P = p.shape[1]
    return pl.pallas_call(
        _ple_kernel,
        grid=(T // tm,),
        in_specs=[pl.BlockSpec((tm, D), lambda i: (i, 0)),
                  pl.BlockSpec((tm, P), lambda i: (i, 0)),
                  pl.BlockSpec((1, D), lambda i: (0, 0)),
                  pl.BlockSpec((D, D), lambda i: (0, 0)),
                  pl.BlockSpec((P, D), lambda i: (0, 0)),
                  pl.BlockSpec((1, D), lambda i: (0, 0))],
        out_specs=pl.BlockSpec((tm, D), lambda i: (i, 0)),
        out_shape=jax.ShapeDtypeStruct((T, D), F32),
        compiler_params=_cparams(("parallel",)),
        name="ple",
    )(x, p, gin.reshape(1, D), wg, wp, gpost.reshape(1, D))


def _roll_rows(x, s, rev):
    n = x.shape[0]
    return pltpu.roll(x, (n - s) if rev else s, 0)


def _cumsum_rows(x, tau, rev):
    n = x.shape[0]
    s = 1
    while s < n:
        x = x + jnp.where(tau >= s, _roll_rows(x, s, rev), 0.0)
        s *= 2
    return x


def _hgrn_kernel(q_ref, ff_ref, fb_ref, i_ref, g_ref, lb_ref, ng_ref, o_ref, acc_scr, st_scr):
    S, DK = q_ref.shape
    C = A_CHUNK
    nc = S // C
    lb = lb_ref[0]
    ng = ng_ref[...]
    row = lax.broadcasted_iota(jnp.int32, (C, DK), 0)
    t2 = lax.broadcasted_iota(jnp.int32, (C, C), 0)
    s2 = lax.broadcasted_iota(jnp.int32, (C, C), 1)
    levels = [L for L in (32, 16, 8) if 2 * L <= C]
    same_blk = {L: (t2 & -(2 * L)) == (s2 & -(2 * L)) for L in levels if 2 * L < C}

    def run(rev, f_ref, final):
        tau = (C - 1 - row) if rev else row
        upper_half = {L: (tau & (2 * L - 1)) >= L for L in levels}
        near = [(tau[:, :1] & (SUBLANES - 1)) >= d for d in range(SUBLANES)]
        last = 0 if rev else C - 1
        st_scr[...] = jnp.zeros_like(st_scr)

        def body(ci, carry):
            c = (nc - 1 - ci) if rev else ci
            r0 = pl.multiple_of(c * C, C)
            rows = pl.ds(r0, C)
            q = _silu(q_ref[rows, :].astype(F32))
            f = lb + (1.0 - lb) * jax.nn.sigmoid(f_ref[rows, :].astype(F32))
            k = 1.0 - f
            v = i_ref[rows, :].astype(F32)
            vb = v.astype(BF16)
            b = _cumsum_rows(jnp.log(f), tau, rev)

            st = st_scr[...]
            o = _dot_nt((q * jnp.exp(b)).astype(BF16), st.astype(BF16))

            attn = jnp.zeros((C, C), F32)
            for L in levels:
                pieces = []
                for j in range(C // (2 * L)):
                    ref_row = j * 2 * L + (L if rev else L - 1)
                    pieces.append(jnp.broadcast_to(b[ref_row:ref_row + 1, :], (2 * L, DK)))
                ref = pieces[0] if len(pieces) == 1 else jnp.concatenate(pieces, axis=0)
                upper = upper_half[L]
                qt = jnp.where(upper, q * jnp.exp(jnp.minimum(b - ref, 0.0)), 0.0).astype(BF16)
                kt = jnp.where(upper, 0.0, k * jnp.exp(jnp.minimum(ref - b, 0.0))).astype(BF16)
                part = _dot_nt(qt, kt)
                if L in same_blk:
                    part = jnp.where(same_blk[L], part, 0.0)
                attn = attn + part
            o = o + _dot(attn.astype(BF16), vb)

            e = None
            for d in range(SUBLANES):
                kd = k if d == 0 else _roll_rows(k, d, rev)
                vd = v if d == 0 else _roll_rows(v, d, rev)
                if d == 1:
                    e = f
                elif d > 1:
                    e = e * _roll_rows(f, d - 1, rev)
                w = q * kd if d == 0 else q * kd * e
                a = jnp.sum(w, axis=-1, keepdims=True)
                a = jnp.where(near[d], a, 0.0)
                o = o + a * vd

            b_last = b[last:last + 1, :]
            khat = (k * jnp.exp(b_last - b)).astype(BF16)
            st_scr[...] = st * jnp.exp(b_last) + _dot_tn(vb, khat)

            if final:
                tot = acc_scr[rows, :] + o
                gate = _silu(g_ref[rows, :].astype(F32))
                o_ref[rows, :] = (_rms(tot, ng) * gate).astype(o_ref.dtype)
            else:
                acc_scr[rows, :] = o
            return carry

        lax.fori_loop(0, nc, body, 0)

    run(False, ff_ref, False)
    run(True, fb_ref, True)


def _hgrn_mixer(u, lb, norm_g, batch, seq):
    T = u.shape[0]
    H, DK = A_HEADS, A_DK

    def spec(off):
        return pl.BlockSpec((seq, DK), lambda b, h, off=off: (b, off + h))

    return pl.pallas_call(
        _hgrn_kernel,
        grid=(batch, H),
        in_specs=[spec(0), spec(H), spec(2 * H), spec(3 * H), spec(4 * H),
                  pl.BlockSpec((1, 1, DK), lambda b, h: (h, 0, 0)),
                  pl.BlockSpec((1, DK), lambda b, h: (0, 0))],
        out_specs=pl.BlockSpec((seq, DK), lambda b, h: (b, h)),
        out_shape=jax.ShapeDtypeStruct((T, H * DK), BF16),
        scratch_shapes=[pltpu.VMEM((seq, DK), F32), pltpu.VMEM((DK, DK), F32)],
        compiler_params=_cparams(("parallel", "parallel")),
        name="hgrn2",
    )(u, u, u, u, u, lb.reshape(H, 1, DK), norm_g.reshape(1, DK))


def _expand_heads(col, j0, width):
    q = col.shape[0]
    lane = lax.broadcasted_iota(jnp.int32, (q, width), 1)
    out = jnp.broadcast_to(col[:, j0 + B_HPG - 1:j0 + B_HPG], (q, width))
    for j in range(B_HPG - 2, -1, -1):
        out = jnp.where(lane < (j + 1) * B_HEADDIM,
                        jnp.broadcast_to(col[:, j0 + j:j0 + j + 1], (q, width)), out)
    return out


def _ssd_kernel(z_ref, x_ref, bm_ref, cm_ref, dt_ref, cwx_ref, cwb_ref, cwc_ref,
                cbx_ref, cbb_ref, cbc_ref, hp_ref, dsk_ref, ng_ref, o_ref,
                xs_scr, b_scr, c_scr, y_scr, st_scr):
    S = x_ref.shape[0]
    Q = B_CHUNK
    nc = S // Q
    GW = B_GW
    hp = hp_ref[0]
    a_row, dtb_row = hp[0:1, :], hp[1:2, :]
    row = lax.broadcasted_iota(jnp.int32, (Q, LANES), 0)
    t2 = lax.broadcasted_iota(jnp.int32, (Q, Q), 0)
    s2 = lax.broadcasted_iota(jnp.int32, (Q, Q), 1)
    lane_gw = lax.broadcasted_iota(jnp.int32, (Q, GW), 1)
    halo = SUBLANES

    def conv_body(c, carry):
        r0 = pl.multiple_of(c * Q, Q)
        rp = pl.multiple_of(jnp.maximum(r0 - halo, 0), halo)
        rn = pl.multiple_of(jnp.minimum(r0 + Q, S - halo), halo)
        has_prev = c > 0
        has_next = c < nc - 1

        def conv(src_ref, w_ref, bias_ref):
            prev = jnp.where(has_prev, src_ref[pl.ds(rp, halo), :].astype(F32), 0.0)
            nxt = jnp.where(has_next, src_ref[pl.ds(rn, halo), :].astype(F32), 0.0)
            xx = jnp.concatenate([prev, src_ref[pl.ds(r0, Q), :].astype(F32), nxt], axis=0)
            n = Q + 2 * halo
            w = w_ref[0]
            acc = None
            for j in range(B_CONV):
                delta = j - B_CONV // 2
                sh = xx if delta == 0 else pltpu.roll(xx, (-delta) % n, 0)
                term = sh[halo:halo + Q, :] * w[j:j + 1, :]
                acc = term if acc is None else acc + term
            return _silu(acc + bias_ref[0])

        xs_scr[pl.ds(r0, Q), :] = conv(x_ref, cwx_ref, cbx_ref)
        b_scr[pl.ds(r0, Q), :] = conv(bm_ref, cwb_ref, cbb_ref).astype(BF16)
        c_scr[pl.ds(r0, Q), :] = conv(cm_ref, cwc_ref, cbc_ref).astype(BF16)
        return carry

    lax.fori_loop(0, nc, conv_body, 0)

    def run(rev, final):
        tau = (Q - 1 - row) if rev else row
        last = 0 if rev else Q - 1
        j0 = B_HPG if rev else 0
        pair_ok = (s2 >= t2) if rev else (t2 >= s2)
        st_scr[...] = jnp.zeros_like(st_scr)

        def body(ci, carry):
            c = (nc - 1 - ci) if rev else ci
            r0 = pl.multiple_of(c * Q, Q)
            rows = pl.ds(r0, Q)
            dt = jax.nn.softplus(dt_ref[rows, :] + dtb_row)
            cs = _cumsum_rows(dt * a_row, tau, rev)
            cs_t = cs.T
            xs = xs_scr[rows, :]
            bm = b_scr[rows, :]
            cm = c_scr[rows, :]
            xdt = xs * _expand_heads(dt, j0, GW)
            ecs = _expand_heads(cs, j0, GW)
            ecs_last = ecs[last:last + 1, :]

            st = st_scr[...]
            y = _dot(cm, st.astype(BF16)) * jnp.exp(ecs)

            scores = _dot_nt(cm, bm)
            for j in range(B_HPG):
                col = jnp.broadcast_to(cs[:, j0 + j:j0 + j + 1], (Q, Q))
                rw = jnp.broadcast_to(cs_t[j0 + j:j0 + j + 1, :], (Q, Q))
                decay = jnp.where(pair_ok, jnp.exp(jnp.minimum(col - rw, 0.0)), 0.0)
                in_head = (lane_gw >= j * B_HEADDIM) & (lane_gw < (j + 1) * B_HEADDIM)
                xh = jnp.where(in_head, xdt, 0.0).astype(BF16)
                y = y + _dot((scores * decay).astype(BF16), xh)

            xdec = (xdt * jnp.exp(ecs_last - ecs)).astype(BF16)
            st_scr[...] = st * jnp.exp(ecs_last) + _dot_tn(bm, xdec)

            if final:
                tot = y_scr[rows, :] + y + dsk_ref[0] * xs
                tot = tot * _silu(z_ref[rows, :].astype(F32))
                o_ref[rows, :] = _rms(tot, ng_ref[0]).astype(o_ref.dtype)
            else:
                y_scr[rows, :] = y
            return carry

        lax.fori_loop(0, nc, body, 0)

    run(False, False)
    run(True, True)


def _ssd_mixer(u, dt, conv_w, conv_b, a_log, dt_bias, d_skip, norm_g, batch, seq, col0):
    T = u.shape[0]
    G, GW, N = B_GROUPS, B_GW, B_STATE
    W = B_HEADS * B_HEADDIM
    assert col0 % GW == 0 and GW == 2 * N
    z_blk = col0 // GW
    x_blk = z_blk + W // GW
    b_blk = (col0 + 2 * W) // N
    c_blk = b_blk + G

    def pad_rows(w):
        return jnp.pad(w, ((0, SUBLANES - w.shape[0]), (0, 0)))

    cwx = pad_rows(conv_w[:, :W]).reshape(SUBLANES, G, GW).transpose(1, 0, 2)
    cwb = pad_rows(conv_w[:, W:W + G * N]).reshape(SUBLANES, G, N).transpose(1, 0, 2)
    cwc = pad_rows(conv_w[:, W + G * N:]).reshape(SUBLANES, G, N).transpose(1, 0, 2)
    cbx = conv_b[:W].reshape(G, 1, GW)
    cbb = conv_b[W:W + G * N].reshape(G, 1, N)
    cbc = conv_b[W + G * N:].reshape(G, 1, N)
    a_neg = -jnp.exp(a_log.astype(F32))
    per_dir = lambda v: jnp.concatenate([v[0].reshape(G, B_HPG), v[1].reshape(G, B_HPG)], axis=1)
    hp = jnp.stack([per_dir(a_neg), per_dir(dt_bias.astype(F32))], axis=1)
    hp = jnp.pad(hp, ((0, 0), (0, SUBLANES - 2), (0, LANES - 2 * B_HPG)))
    dsk = jnp.repeat(d_skip.astype(F32), B_HEADDIM).reshape(G, 1, GW)
    ng = norm_g.reshape(G, 1, GW)

    gspec = lambda shape: pl.BlockSpec((1,) + shape, lambda b, g: (g, 0, 0))
    return pl.pallas_call(
        _ssd_kernel,
        grid=(batch, G),
        in_specs=[pl.BlockSpec((seq, GW), lambda b, g: (b, z_blk + g)),
                  pl.BlockSpec((seq, GW), lambda b, g: (b, x_blk + g)),
                  pl.BlockSpec((seq, N), lambda b, g: (b, b_blk + g)),
                  pl.BlockSpec((seq, N), lambda b, g: (b, c_blk + g)),
                  pl.BlockSpec((seq, LANES), lambda b, g: (b, g)),
                  gspec((SUBLANES, GW)), gspec((SUBLANES, N)), gspec((SUBLANES, N)),
                  gspec((1, GW)), gspec((1, N)), gspec((1, N)),
                  gspec((SUBLANES, LANES)), gspec((1, GW)), gspec((1, GW))],
        out_specs=pl.BlockSpec((seq, GW), lambda b, g: (b, g)),
        out_shape=jax.ShapeDtypeStruct((T, W), BF16),
        scratch_shapes=[pltpu.VMEM((seq, GW), F32), pltpu.VMEM((seq, N), BF16),
                        pltpu.VMEM((seq, N), BF16), pltpu.VMEM((seq, GW), F32),
                        pltpu.VMEM((N, GW), F32)],
        compiler_params=_cparams(("parallel", "parallel")),
        name="ssd",
    )(u, u, u, u, dt, cwx, cwb, cwc, cbx, cbb, cbc, hp, dsk, ng)


def _na_kernel(q_ref, k_ref, v_ref, bias_ref, o_ref):
    S, HW = q_ref.shape
    W = GRID_W
    n_rows = S // W
    kh = min(NA_ROWS, n_rows)
    nk = kh * W
    lane = lax.broadcasted_iota(jnp.int32, (W, HW), 1)
    head_masks = [(lane >= h * C_HEADDIM) & (lane < (h + 1) * C_HEADDIM) for h in range(NA_HG)]
    scale = C_HEADDIM ** -0.5

    def body(r, carry):
        rs = jnp.clip(r - kh // 2, 0, n_rows - kh)
        q = q_ref[pl.ds(pl.multiple_of(r * W, W), W), :] * jnp.asarray(scale, q_ref.dtype)
        kw = k_ref[pl.ds(pl.multiple_of(rs * W, W), nk), :]
        vw = v_ref[pl.ds(pl.multiple_of(rs * W, W), nk), :]
        zero = jnp.zeros_like(q)
        qs = jnp.concatenate([jnp.where(m, q, zero) for m in head_masks], axis=0)
        s = _dot_nt(qs, kw) + bias_ref[0, r - rs]
        m = jnp.max(s, axis=-1, keepdims=True)
        p = jnp.exp(s - m)
        l = jnp.sum(p, axis=-1, keepdims=True)
        res = _dot(p.astype(BF16), vw) / l
        out = jnp.zeros((W, HW), F32)
        for h in range(NA_HG):
            out = jnp.where(head_masks[h], res[h * W:(h + 1) * W, :], out)
        o_ref[pl.ds(pl.multiple_of(r * W, W), W), :] = out.astype(o_ref.dtype)
        return carry

    lax.fori_loop(0, n_rows, body, 0)


def _na_bias_table(rpb, n_rows):
    H = rpb.shape[0]
    W = GRID_W
    kh = min(NA_ROWS, n_rows)
    var = np.arange(kh)
    dr = np.arange(kh)[None, :] - var[:, None] + (NA_ROWS - 1)
    c = np.arange(W)
    qs = np.clip(c - NA_COLS // 2, 0, W - NA_COLS)
    valid = (c[None, :] >= qs[:, None]) & (c[None, :] < qs[:, None] + NA_COLS)
    dc = np.clip(c[None, :] - c[:, None] + (NA_COLS - 1), 0, 2 * NA_COLS - 2)
    tab = rpb.astype(F32)[:, dr][:, :, :, dc]
    tab = jnp.where(jnp.asarray(valid)[None, None, None], tab, -jnp.inf)
    tab = tab.transpose(0, 1, 3, 2, 4).reshape(H // NA_HG, NA_HG, kh, W, kh * W)
    return tab.transpose(0, 2, 1, 3, 4).reshape(H // NA_HG, kh, NA_HG * W, kh * W)


def _neighborhood_attention(qkv, rpb, batch, seq):
    T = qkv.shape[0]
    HW = NA_HG * C_HEADDIM
    n_hg = C_HEADS // NA_HG
    n_rows = seq // GRID_W
    kh = min(NA_ROWS, n_rows)
    table = _na_bias_table(rpb, n_rows)
    return pl.pallas_call(
        _na_kernel,
        grid=(batch, n_hg),
        in_specs=[pl.BlockSpec((seq, HW), lambda b, g: (b, g)),
                  pl.BlockSpec((seq, HW), lambda b, g: (b, n_hg + g)),
                  pl.BlockSpec((seq, HW), lambda b, g: (b, 2 * n_hg + g)),
                  pl.BlockSpec((1, kh, NA_HG * GRID_W, kh * GRID_W), lambda b, g: (g, 0, 0, 0))],
        out_specs=pl.BlockSpec((seq, HW), lambda b, g: (b, g)),
        out_shape=jax.ShapeDtypeStruct((T, C_HEADS * C_HEADDIM), BF16),
        compiler_params=_cparams(("parallel", "parallel")),
        name="natten",
    )(qkv, qkv, qkv, table)


def _store_token_tiles(ref, val):
    tm, width = val.shape
    n = width // LANES
    for j in range(n):
        ref[pl.ds(j, tm, stride=n), :] = val[:, j * LANES:(j + 1) * LANES]


def _load_token_tiles(ref, n):
    tm = ref.shape[0] // n
    return jnp.concatenate([ref[pl.ds(j, tm, stride=n), :] for j in range(n)], axis=1)


def _router_kernel(x_ref, g_ref, wr_ref, o_ref, h_ref):
    h = _rms(x_ref[...], g_ref[...])
    _store_token_tiles(h_ref, h)
    logits = jnp.dot(h, wr_ref[...], preferred_element_type=F32, precision=lax.Precision.HIGHEST)
    lane = lax.broadcasted_iota(jnp.int32, logits.shape, 1)
    neg = -jnp.inf
    l1 = jnp.where(lane < N_EXPERTS, logits, neg)
    m1 = jnp.max(l1, axis=-1, keepdims=True)
    i1 = jnp.min(jnp.where(l1 == m1, lane, LANES), axis=-1, keepdims=True)
    l2 = jnp.where(lane == i1, neg, l1)
    m2 = jnp.max(l2, axis=-1, keepdims=True)
    i2 = jnp.min(jnp.where(l2 == m2, lane, LANES), axis=-1, keepdims=True)
    e = jnp.exp(m2 - m1)
    w1 = 1.0 / (1.0 + e)
    w2 = e / (1.0 + e)
    out = jnp.where(lane == 0, i1.astype(F32),
                    jnp.where(lane == 1, i2.astype(F32),
                              jnp.where(lane == 2, w1, jnp.where(lane == 3, w2, 0.0))))
    o_ref[...] = out


def _router(x, g, w_router, tm=512):
    T, D = x.shape
    n = D // LANES
    wr = jnp.pad(w_router.astype(F32), ((0, 0), (0, LANES - w_router.shape[1])))
    route, h = pl.pallas_call(
        _router_kernel,
        grid=(T // tm,),
        in_specs=[pl.BlockSpec((tm, D), lambda i: (i, 0)),
                  pl.BlockSpec((1, D), lambda i: (0, 0)),
                  pl.BlockSpec((D, LANES), lambda i: (0, 0))],
        out_specs=[pl.BlockSpec((tm, LANES), lambda i: (i, 0)),
                   pl.BlockSpec((tm * n, LANES), lambda i: (i, 0))],
        out_shape=[jax.ShapeDtypeStruct((T, LANES), F32),
                   jax.ShapeDtypeStruct((T * n, LANES), F32)],
        compiler_params=_cparams(("parallel",)),
        name="router",
    )(x, g.reshape(1, D), wr)
    return route, h.reshape(T, n, LANES)


SC_CORES = 2
SC_SUBCORES = 16
SC_WORKERS = SC_CORES * SC_SUBCORES
SC_CHUNK = 32


def _sc_mesh():
    return plsc.VectorSubcoreMesh(core_axis_name="c", subcore_axis_name="s",
                                  num_cores=SC_CORES, num_subcores=SC_SUBCORES)


def _sc_index_blocks(idx):
    return idx.reshape(SC_WORKERS, -1, SC_CHUNK)


def _sc_scatter_tokens(h, dest, n_rows):
    T, n, _ = h.shape
    per_worker = T // SC_WORKERS
    n_chunks = per_worker // SC_CHUNK
    assert per_worker * SC_WORKERS == T and n_chunks * SC_CHUNK == per_worker

    @functools.partial(
        pl.kernel, mesh=_sc_mesh(),
        out_type=jax.ShapeDtypeStruct((n_rows, n, LANES), h.dtype),
        scratch_types=[pltpu.VMEM((n_chunks, SC_CHUNK), jnp.int32),
                       pltpu.VMEM((n_chunks, SC_CHUNK), jnp.int32),
                       pltpu.VMEM((SC_CHUNK, n, LANES), h.dtype)],
        name="sc_scatter_tokens",
    )
    def scatter(h_hbm, d0_hbm, d1_hbm, o_hbm, i0_v, i1_v, rows_v):
        wid = lax.axis_index("s") * SC_CORES + lax.axis_index("c")
        pltpu.sync_copy(d0_hbm.at[wid], i0_v)
        pltpu.sync_copy(d1_hbm.at[wid], i1_v)

        @pl.loop(0, n_chunks)
        def _(j):
            pltpu.sync_copy(h_hbm.at[pl.ds(wid * per_worker + j * SC_CHUNK, SC_CHUNK)], rows_v)
            pltpu.sync_copy(rows_v, o_hbm.at[i0_v.at[j]])
            pltpu.sync_copy(rows_v, o_hbm.at[i1_v.at[j]])

    return scatter(h, _sc_index_blocks(dest[:, 0]), _sc_index_blocks(dest[:, 1]))


def _sc_gather_tokens(y, dest):
    T = dest.shape[0]
    _, n, _ = y.shape
    per_worker = T // SC_WORKERS
    n_chunks = per_worker // SC_CHUNK
    assert per_worker * SC_WORKERS == T and n_chunks * SC_CHUNK == per_worker
    out = jax.ShapeDtypeStruct((T, n, LANES), y.dtype)

    @functools.partial(
        pl.kernel, mesh=_sc_mesh(), out_type=(out, out),
        scratch_types=[pltpu.VMEM((n_chunks, SC_CHUNK), jnp.int32),
                       pltpu.VMEM((n_chunks, SC_CHUNK), jnp.int32),
                       pltpu.VMEM((SC_CHUNK, n, LANES), y.dtype)],
        name="sc_gather_tokens",
    )
    def gather(y_hbm, d0_hbm, d1_hbm, o0_hbm, o1_hbm, i0_v, i1_v, rows_v):
        wid = lax.axis_index("s") * SC_CORES + lax.axis_index("c")
        pltpu.sync_copy(d0_hbm.at[wid], i0_v)
        pltpu.sync_copy(d1_hbm.at[wid], i1_v)

        @pl.loop(0, n_chunks)
        def _(j):
            rows = pl.ds(wid * per_worker + j * SC_CHUNK, SC_CHUNK)
            pltpu.sync_copy(y_hbm.at[i0_v.at[j]], rows_v)
            pltpu.sync_copy(rows_v, o0_hbm.at[rows])
            pltpu.sync_copy(y_hbm.at[i1_v.at[j]], rows_v)
            pltpu.sync_copy(rows_v, o1_hbm.at[rows])

    return gather(y, _sc_index_blocks(dest[:, 0]), _sc_index_blocks(dest[:, 1]))


def _moe_ffn_kernel(te_ref, na_ref, h_ref, wg_ref, wu_ref, wd_ref, o_ref, acc_scr, *, tf):
    i = pl.program_id(0)
    D = acc_scr.shape[1]
    F = wg_ref.shape[2]

    @pl.when(i < na_ref[0])
    def _():
        h = _load_token_tiles(h_ref, D // LANES).astype(BF16)
        for f0 in range(0, F, tf):
            a = (_silu(_dot(h, wg_ref[0, :, f0:f0 + tf])) * _dot(h, wu_ref[0, :, f0:f0 + tf])).astype(BF16)
            part = _dot(a, wd_ref[0, f0:f0 + tf, :])
            if f0 == 0:
                acc_scr[...] = part
            else:
                acc_scr[...] += part
        _store_token_tiles(o_ref, acc_scr[...])

    @pl.when(i >= na_ref[0])
    def _():
        o_ref[...] = jnp.zeros_like(o_ref)


def _moe_ffn(hs, wg, wu, wd, tile_expert, n_active, tm=MOE_TM, tf=512):
    rows, n, _ = hs.shape
    E, D, F = wg.shape
    nt = tile_expert.shape[0]
    assert rows == nt * tm and n * LANES == D
    grid_spec = pltpu.PrefetchScalarGridSpec(
        num_scalar_prefetch=2,
        grid=(nt,),
        in_specs=[pl.BlockSpec((tm * n, LANES), lambda i, te, na: (i, 0)),
                  pl.BlockSpec((1, D, F), lambda i, te, na: (te[i], 0, 0)),
                  pl.BlockSpec((1, D, F), lambda i, te, na: (te[i], 0, 0)),
                  pl.BlockSpec((1, F, D), lambda i, te, na: (te[i], 0, 0))],
        out_specs=pl.BlockSpec((tm * n, LANES), lambda i, te, na: (i, 0)),
        scratch_shapes=[pltpu.VMEM((tm, D), F32)],
    )
    y = pl.pallas_call(
        functools.partial(_moe_ffn_kernel, tf=tf),
        grid_spec=grid_spec,
        out_shape=jax.ShapeDtypeStruct((rows * n, LANES), F32),
        compiler_params=_cparams(("arbitrary",)),
        name="moe_ffn",
    )(tile_expert, n_active, hs.reshape(rows * n, LANES), wg, wu, wd)
    return y.reshape(rows, n, LANES)


def _moe_combine_kernel(y0_ref, y1_ref, route_ref, x_ref, g_ref, o_ref):
    n = x_ref.shape[1] // LANES
    route = route_ref[...]
    moe = route[:, 2:3] * _load_token_tiles(y0_ref, n) + route[:, 3:4] * _load_token_tiles(y1_ref, n)
    o_ref[...] = x_ref[...] + _rms(moe, g_ref[...])


def _moe_combine(y0, y1, route, x, g, tm=512):
    T, D = x.shape
    n = D // LANES
    return pl.pallas_call(
        _moe_combine_kernel,
        grid=(T // tm,),
        in_specs=[pl.BlockSpec((tm * n, LANES), lambda i: (i, 0)),
                  pl.BlockSpec((tm * n, LANES), lambda i: (i, 0)),
                  pl.BlockSpec((tm, LANES), lambda i: (i, 0)),
                  pl.BlockSpec((tm, D), lambda i: (i, 0)),
                  pl.BlockSpec((1, D), lambda i: (0, 0))],
        out_specs=pl.BlockSpec((tm, D), lambda i: (i, 0)),
        out_shape=jax.ShapeDtypeStruct((T, D), F32),
        compiler_params=_cparams(("parallel",)),
        name="moe_combine",
    )(y0.reshape(T * n, LANES), y1.reshape(T * n, LANES), route, x, g.reshape(1, D))


def _moe_plan(route, tm=MOE_TM):
    T = route.shape[0]
    e = route[:, :TOP_K].astype(jnp.int32).reshape(-1)
    onehot = (e[:, None] == jnp.arange(N_EXPERTS, dtype=jnp.int32)[None, :]).astype(jnp.int32)
    csum = jnp.cumsum(onehot, axis=0)
    rank = jnp.sum(csum * onehot, axis=1) - 1
    counts = csum[-1]
    padded = ((counts + tm - 1) // tm) * tm
    ends = jnp.cumsum(padded)
    starts = ends - padded
    dest = jnp.sum(starts[None, :] * onehot, axis=1) + rank
    nt = (T * TOP_K) // tm + N_EXPERTS
    tile_start = jnp.arange(nt, dtype=jnp.int32) * tm
    tile_expert = jnp.minimum(jnp.sum((tile_start[:, None] >= ends[None, :]).astype(jnp.int32), axis=1),
                              N_EXPERTS - 1).astype(jnp.int32)
    n_active = (ends[-1] // tm).astype(jnp.int32).reshape(1)
    return tile_expert, n_active, dest.reshape(T, TOP_K).astype(jnp.int32)


def kernel(x, p, hgrn_lb_raw, e_norm_mix_pre, e_w_in, e_conv_w, e_conv_b, e_A_log, e_dt_bias, e_D, e_a_norm, e_b_norm, e_w_out, e_norm_mix_post, e_norm_ffn_pre, e_w_ffn_gate, e_w_ffn_up, e_w_ffn_down, e_norm_ffn_post, o_norm_mix_pre, o_w_qkv, o_rpb, o_w_out, o_norm_mix_post, o_norm_ffn_pre, o_w_router, o_w_exp_gate, o_w_exp_up, o_w_exp_down, o_norm_ffn_post, ple_norm_in, ple_w_gate, ple_w_proj, ple_norm_post):
    batch, seq, d_model = x.shape
    depth = p.shape[0]
    T = batch * seq
    xt = x.reshape(T, d_model)
    lb_all = jnp.cumsum(jax.nn.softmax(hgrn_lb_raw.astype(F32), axis=0), axis=0)

    a_kdim = A_HEADS * A_DK
    b_width = B_HEADS * B_HEADDIM
    conv_dim = b_width + 2 * B_GROUPS * B_STATE
    main_w = 5 * a_kdim + b_width + conv_dim

    for li in range(depth):
        j = li // 2
        if li % 2 == 0:
            w_in = e_w_in[j]
            dtf = w_in[:, main_w:main_w + B_HEADS].reshape(d_model, B_GROUPS, B_HPG)
            dtb = w_in[:, main_w + B_HEADS:].reshape(d_model, B_GROUPS, B_HPG)
            w_dt = jnp.pad(jnp.concatenate([dtf, dtb], axis=2), ((0, 0), (0, 0), (0, LANES - 2 * B_HPG)))
            w_all = jnp.concatenate([w_in[:, :main_w], w_dt.reshape(d_model, B_GROUPS * LANES)], axis=1)
            u, dt = _norm_proj(xt, e_norm_mix_pre[j], w_all.astype(BF16),
                               (main_w, B_GROUPS * LANES), (BF16, F32))
            o_a = _hgrn_mixer(u, lb_all[li], e_a_norm[j], batch, seq)
            o_b = _ssd_mixer(u, dt, e_conv_w[j], e_conv_b[j], e_A_log[j], e_dt_bias[j], e_D[j],
                             e_b_norm[j], batch, seq, 5 * a_kdim)
            w_out = e_w_out[j].astype(BF16)
            xt = _proj_norm_res([o_a, o_b], [w_out[:a_kdim], w_out[a_kdim:]], e_norm_mix_post[j], xt)
            xt = _ffn(xt, e_norm_ffn_pre[j], e_w_ffn_gate[j].astype(BF16), e_w_ffn_up[j].astype(BF16),
                      e_w_ffn_down[j].astype(BF16), e_norm_ffn_post[j])
        else:
            (qkv,) = _norm_proj(xt, o_norm_mix_pre[j], o_w_qkv[j].astype(BF16),
                                (3 * C_HEADS * C_HEADDIM,), (BF16,))
            o_c = _neighborhood_attention(qkv, o_rpb[j], batch, seq)
            xt = _proj_norm_res([o_c], [o_w_out[j].astype(BF16)], o_norm_mix_post[j], xt)
            route, h = _router(xt, o_norm_ffn_pre[j], o_w_router[j])
            tile_expert, n_active, dest = _moe_plan(route)
            hs = _sc_scatter_tokens(h, dest, tile_expert.shape[0] * MOE_TM)
            ys = _moe_ffn(hs, o_w_exp_gate[j].astype(BF16), o_w_exp_up[j].astype(BF16),
                          o_w_exp_down[j].astype(BF16), tile_expert, n_active)
            y0, y1 = _sc_gather_tokens(ys, dest)
            xt = _moe_combine(y0, y1, route, xt, o_norm_ffn_post[j])
        xt = _ple(xt, p[li].reshape(T, -1), ple_norm_in[li], ple_w_gate[li].astype(BF16),
                  ple_w_proj[li].astype(BF16), ple_norm_post[li])
    return xt.reshape(batch, seq, d_model)
```

```python
import functools
import math

import numpy as np
import jax
import jax.numpy as jnp
from jax import lax
from jax.experimental import pallas as pl
from jax.experimental.pallas import tpu as pltpu
from jax.experimental.pallas import tpu_sc as plsc

F32 = jnp.float32
BF16 = jnp.bfloat16
EPS = 1e-6

LANES = 128
SUBLANES = 8
VMEM_LIMIT_BYTES = 56 * 1024 * 1024

GRID_W = 64
A_HEADS, A_DK, A_CHUNK = 4, 128, 64
A_UNROLL = 4
B_HEADS, B_HEADDIM, B_GROUPS, B_STATE, B_CONV, B_CHUNK = 8, 64, 2, 128, 5, 128
B_HPG = B_HEADS // B_GROUPS
B_GW = B_HPG * B_HEADDIM
C_HEADS, C_HEADDIM = 16, 64
NA_ROWS, NA_COLS = 8, 16
NA_HG = 4
NA_UNROLL = 4
N_EXPERTS, TOP_K = 8, 2
MOE_TM = 256


def _cparams(sem):
    return pltpu.CompilerParams(dimension_semantics=sem, vmem_limit_bytes=VMEM_LIMIT_BYTES)


def _rms(x, g):
    return x * lax.rsqrt(jnp.mean(x * x, axis=-1, keepdims=True) + EPS) * g


def _silu(x):
    return x * jax.nn.sigmoid(x)


def _dot(a, b):
    return jnp.dot(a, b, preferred_element_type=F32)


def _dot_nt(a, b):
    return lax.dot_general(a, b, (((1,), (1,)), ((), ())), preferred_element_type=F32)


def _dot_tn(a, b):
    return lax.dot_general(a, b, (((0,), (0,)), ((), ())), preferred_element_type=F32)


def _norm_proj_kernel(x_ref, g_ref, w_ref, *o_refs, col_chunk):
    h = _rms(x_ref[...], g_ref[...]).astype(BF16)
    off = 0
    for o_ref in o_refs:
        n = o_ref.shape[1]
        for c0 in range(0, n, col_chunk):
            c1 = min(c0 + col_chunk, n)
            o_ref[:, c0:c1] = _dot(h, w_ref[:, off + c0:off + c1]).astype(o_ref.dtype)
        off += n


def _norm_proj(x, g, w, widths, dtypes, tm=512, col_chunk=1024):
    T, D = x.shape
    N = w.shape[1]
    assert sum(widths) == N and T % tm == 0
    return pl.pallas_call(
        functools.partial(_norm_proj_kernel, col_chunk=col_chunk),
        grid=(T // tm,),
        in_specs=[pl.BlockSpec((tm, D), lambda i: (i, 0)),
                  pl.BlockSpec((1, D), lambda i: (0, 0)),
                  pl.BlockSpec((D, N), lambda i: (0, 0))],
        out_specs=[pl.BlockSpec((tm, n), lambda i: (i, 0)) for n in widths],
        out_shape=[jax.ShapeDtypeStruct((T, n), dt) for n, dt in zip(widths, dtypes)],
        compiler_params=_cparams(("parallel",)),
        name="norm_proj",
    )(x, g.reshape(1, D), w)


def _proj_norm_res_kernel(*refs, n_in):
    a_refs, w_refs = refs[:n_in], refs[n_in:2 * n_in]
    g_ref, x_ref, o_ref = refs[2 * n_in:]
    acc = _dot(a_refs[0][...], w_refs[0][...])
    for a_ref, w_ref in zip(a_refs[1:], w_refs[1:]):
        acc = acc + _dot(a_ref[...], w_ref[...])
    o_ref[...] = x_ref[...] + _rms(acc, g_ref[...])


def _proj_norm_res(acts, ws, g, x, tm=512):
    T, D = x.shape
    n_in = len(acts)
    in_specs = ([pl.BlockSpec((tm, a.shape[1]), lambda i: (i, 0)) for a in acts]
                + [pl.BlockSpec(w.shape, lambda i: (0, 0)) for w in ws]
                + [pl.BlockSpec((1, D), lambda i: (0, 0)), pl.BlockSpec((tm, D), lambda i: (i, 0))])
    return pl.pallas_call(
        functools.partial(_proj_norm_res_kernel, n_in=n_in),
        grid=(T // tm,),
        in_specs=in_specs,
        out_specs=pl.BlockSpec((tm, D), lambda i: (i, 0)),
        out_shape=jax.ShapeDtypeStruct((T, D), F32),
        compiler_params=_cparams(("parallel",)),
        name="proj_norm_res",
    )(*acts, *ws, g.reshape(1, D), x)


def _ffn_kernel(x_ref, gpre_ref, wg_ref, wu_ref, wd_ref, gpost_ref, o_ref, h_scr, acc_scr):
    j = pl.program_id(1)

    @pl.when(j == 0)
    def _():
        h_scr[...] = _rms(x_ref[...], gpre_ref[...]).astype(BF16)
        acc_scr[...] = jnp.zeros_like(acc_scr)

    h = h_scr[...]
    a = (_silu(_dot(h, wg_ref[...])) * _dot(h, wu_ref[...])).astype(BF16)
    acc_scr[...] += _dot(a, wd_ref[...])

    @pl.when(j == pl.num_programs(1) - 1)
    def _():
        o_ref[...] = x_ref[...] + _rms(acc_scr[...], gpost_ref[...])


def _ffn(x, gpre, wg, wu, wd, gpost, tm=512, tf=1408):
    T, D = x.shape
    F = wg.shape[1]
    assert F % tf == 0 and T % tm == 0
    return pl.pallas_call(
        _ffn_kernel,
        grid=(T // tm, F // tf),
        in_specs=[pl.BlockSpec((tm, D), lambda i, j: (i, 0)),
                  pl.BlockSpec((1, D), lambda i, j: (0, 0)),
                  pl.BlockSpec((D, tf), lambda i, j: (0, j)),
                  pl.BlockSpec((D, tf), lambda i, j: (0, j)),
                  pl.BlockSpec((tf, D), lambda i, j: (j, 0)),
                  pl.BlockSpec((1, D), lambda i, j: (0, 0))],
        out_specs=pl.BlockSpec((tm, D), lambda i, j: (i, 0)),
        out_shape=jax.ShapeDtypeStruct((T, D), F32),
        scratch_shapes=[pltpu.VMEM((tm, D), BF16), pltpu.VMEM((tm, D), F32)],
        compiler_params=_cparams(("parallel", "arbitrary")),
        name="ffn",
    )(x, gpre.reshape(1, D), wg, wu, wd, gpost.reshape(1, D))


def _ple_kernel(x_ref, p_ref, gin_ref, wg_ref, wp_ref, gpost_ref, o_ref):
    x = x_ref[...]
    h = _rms(x, gin_ref[...]).astype(BF16)
    gate = jax.nn.sigmoid(_dot(h, wg_ref[...]))
    proj = _dot(p_ref[...].astype(BF16), wp_ref[...])
    o_ref[...] = x + _rms(gate * proj, gpost_ref[...])


def _ple(x, p, gin, wg, wp, gpost, tm=512):
    T, D = x.shape
    P = p.shape[1]
    return pl.pallas_call(
        _ple_kernel,
        grid=(T // tm,),
        in_specs=[pl.BlockSpec((tm, D), lambda i: (i, 0)),
                  pl.BlockSpec((tm, P), lambda i: (i, 0)),
                  pl.BlockSpec((1, D), lambda i: (0, 0)),
                  pl.BlockSpec((D, D), lambda i: (0, 0)),
                  pl.BlockSpec((P, D), lambda i: (0, 0)),
                  pl.BlockSpec((1, D), lambda i: (0, 0))],
        out_specs=pl.BlockSpec((tm, D), lambda i: (i, 0)),
        out_shape=jax.ShapeDtypeStruct((T, D), F32),
        compiler_params=_cparams(("parallel",)),
        name="ple",
    )(x, p, gin.reshape(1, D), wg, wp, gpost.reshape(1, D))


def _roll_rows(x, s, rev):
    n = x.shape[0]
    return pltpu.roll(x, (n - s) if rev else s, 0)


def _cumsum_rows(x, tau, rev):
    n = x.shape[0]
    s = 1
    while s < n:
        if s % SUBLANES:
            shifted = jnp.where(tau >= s, _roll_rows(x, s, rev), 0.0)
        else:
            zeros = jnp.zeros((s,) + x.shape[1:], x.dtype)
            shifted = (jnp.concatenate([x[s:], zeros], axis=0) if rev
                       else jnp.concatenate([zeros, x[:n - s]], axis=0))
        x = x + shifted
        s *= 2
    return x


def _hgrn_kernel(q_ref, ff_ref, fb_ref, i_ref, g_ref, lb_ref, ng_ref, o_ref, acc_scr, *st_scrs):
    S, DK = q_ref.shape
    C = A_CHUNK
    nc = S // C
    assert nc % (2 * A_UNROLL) == 0
    lb = lb_ref[0]
    ng = ng_ref[...]
    row = lax.broadcasted_iota(jnp.int32, (C, DK), 0)
    t2 = lax.broadcasted_iota(jnp.int32, (C, C), 0)
    s2 = lax.broadcasted_iota(jnp.int32, (C, C), 1)
    levels = [C >> (i + 1) for i in range(C.bit_length() - 1)]

    def make_chunk(rev, f_ref):
        tau = (C - 1 - row) if rev else row
        upper_half = {L: (tau & (2 * L - 1)) >= L for L in levels}
        tau2 = (C - 1 - t2) if rev else t2
        sig2 = (C - 1 - s2) if rev else s2
        pair = {L: ((t2 & -(2 * L)) == (s2 & -(2 * L)))
                   & ((tau2 & (2 * L - 1)) >= L) & ((sig2 & (2 * L - 1)) < L) for L in levels}
        last = 0 if rev else C - 1

        def ref_rows(b, L):
            off = L if rev else L - 1
            if 2 * L >= SUBLANES:
                pieces = [jnp.broadcast_to(b[j * 2 * L + off:j * 2 * L + off + 1, :], (2 * L, DK))
                          for j in range(C // (2 * L))]
                return pieces[0] if len(pieces) == 1 else jnp.concatenate(pieces, axis=0)
            if L == 1:
                return jnp.where(upper_half[1], _roll_rows(b, 1, rev), b)
            b3 = b.reshape(C // SUBLANES, SUBLANES, DK)
            sub = lax.broadcasted_iota(jnp.int32, b3.shape, 1)
            out = None
            for j in range(SUBLANES // (2 * L)):
                piece = jnp.broadcast_to(b3[:, j * 2 * L + off:j * 2 * L + off + 1, :], b3.shape)
                out = piece if out is None else jnp.where(sub >= j * 2 * L, piece, out)
            return out.reshape(C, DK)

        def chunk(rows, st):
            q = _silu(q_ref[rows, :].astype(F32))
            f = lb + (1.0 - lb) * jax.nn.sigmoid(f_ref[rows, :].astype(F32))
            k = 1.0 - f
            v = i_ref[rows, :].astype(F32)
            vb = v.astype(BF16)
            b = _cumsum_rows(jnp.log(f), tau, rev)

            o = _dot_nt((q * jnp.exp(b)).astype(BF16), st.astype(BF16))

            attn = jnp.where(t2 == s2, _dot_nt(q.astype(BF16), k.astype(BF16)), 0.0)
            for L in levels:
                scaled = (jnp.where(upper_half[L], q, k) * jnp.exp(-jnp.abs(b - ref_rows(b, L)))).astype(BF16)
                attn = jnp.where(pair[L], _dot_nt(scaled, scaled), attn)
            o = o + _dot(attn.astype(BF16), vb)

            b_last = b[last:last + 1, :]
            khat = (k * jnp.exp(b_last - b)).astype(BF16)
            return o, st * jnp.exp(b_last) + _dot_tn(vb, khat)

        return chunk

    chunk_fns = (make_chunk(False, ff_ref), make_chunk(True, fb_ref))
    for st_scr in st_scrs:
        st_scr[...] = jnp.zeros_like(st_scr)

    def make_body(final):
        def body(it, carry):
            for rev, chunk, st_scr in zip((False, True), chunk_fns, st_scrs):
                st = st_scr[...]
                for u in range(A_UNROLL):
                    ci = it * A_UNROLL + u
                    c = (nc - 1 - ci) if rev else ci
                    rows = pl.ds(pl.multiple_of(c * C, C), C)
                    o, st = chunk(rows, st)
                    if final:
                        tot = acc_scr[rows, :] + o
                        gate = _silu(g_ref[rows, :].astype(F32))
                        o_ref[rows, :] = (_rms(tot, ng) * gate).astype(o_ref.dtype)
                    else:
                        acc_scr[rows, :] = o
                st_scr[...] = st
            return carry

        return body

    trips = nc // A_UNROLL
    lax.fori_loop(0, trips // 2, make_body(False), 0)
    lax.fori_loop(trips // 2, trips, make_body(True), 0)


def _hgrn_mixer(u, lb, norm_g, batch, seq):
    T = u.shape[0]
    H, DK = A_HEADS, A_DK

    def spec(off):
        return pl.BlockSpec((seq, DK), lambda b, h, off=off: (b, off + h))

    return pl.pallas_call(
        _hgrn_kernel,
        grid=(batch, H),
        in_specs=[spec(0), spec(H), spec(2 * H), spec(3 * H), spec(4 * H),
                  pl.BlockSpec((1, 1, DK), lambda b, h: (h, 0, 0)),
                  pl.BlockSpec((1, DK), lambda b, h: (0, 0))],
        out_specs=pl.BlockSpec((seq, DK), lambda b, h: (b, h)),
        out_shape=jax.ShapeDtypeStruct((T, H * DK), BF16),
        scratch_shapes=[pltpu.VMEM((seq, DK), F32), pltpu.VMEM((DK, DK), F32),
                        pltpu.VMEM((DK, DK), F32)],
        compiler_params=_cparams(("parallel", "parallel")),
        name="hgrn2",
    )(u, u, u, u, u, lb.reshape(H, 1, DK), norm_g.reshape(1, DK))


def _expand_heads(col, j0, width):
    q = col.shape[0]
    lane = lax.broadcasted_iota(jnp.int32, (q, width), 1)
    out = jnp.broadcast_to(col[:, j0 + B_HPG - 1:j0 + B_HPG], (q, width))
    for j in range(B_HPG - 2, -1, -1):
        out = jnp.where(lane < (j + 1) * B_HEADDIM,
                        jnp.broadcast_to(col[:, j0 + j:j0 + j + 1], (q, width)), out)
    return out


def _ssd_kernel(z_ref, x_ref, bm_ref, cm_ref, dt_ref, cwx_ref, cwb_ref, cwc_ref,
                cbx_ref, cbb_ref, cbc_ref, hp_ref, dsk_ref, ng_ref, o_ref,
                xs_scr, b_scr, c_scr, y_scr, *st_scrs):
    S = x_ref.shape[0]
    Q = B_CHUNK
    nc = S // Q
    GW = B_GW
    hp = hp_ref[0]
    a_row, dtb_row = hp[0:1, :], hp[1:2, :]
    row = lax.broadcasted_iota(jnp.int32, (Q, LANES), 0)
    t2 = lax.broadcasted_iota(jnp.int32, (Q, Q), 0)
    s2 = lax.broadcasted_iota(jnp.int32, (Q, Q), 1)
    lane_gw = lax.broadcasted_iota(jnp.int32, (Q, GW), 1)
    halo = SUBLANES

    def conv_body(c, carry):
        r0 = pl.multiple_of(c * Q, Q)
        rp = pl.multiple_of(jnp.maximum(r0 - halo, 0), halo)
        rn = pl.multiple_of(jnp.minimum(r0 + Q, S - halo), halo)
        has_prev = c > 0
        has_next = c < nc - 1

        def conv(src_ref, w_ref, bias_ref):
            prev = jnp.where(has_prev, src_ref[pl.ds(rp, halo), :].astype(F32), 0.0)
            nxt = jnp.where(has_next, src_ref[pl.ds(rn, halo), :].astype(F32), 0.0)
            xx = jnp.concatenate([prev, src_ref[pl.ds(r0, Q), :].astype(F32), nxt], axis=0)
            n = Q + 2 * halo
            w = w_ref[0]
            acc = None
            for j in range(B_CONV):
                delta = j - B_CONV // 2
                sh = xx if delta == 0 else pltpu.roll(xx, (-delta) % n, 0)
                term = sh[halo:halo + Q, :] * w[j:j + 1, :]
                acc = term if acc is None else acc + term
            return _silu(acc + bias_ref[0])

        xs_scr[pl.ds(r0, Q), :] = conv(x_ref, cwx_ref, cbx_ref)
        b_scr[pl.ds(r0, Q), :] = conv(bm_ref, cwb_ref, cbb_ref).astype(BF16)
        c_scr[pl.ds(r0, Q), :] = conv(cm_ref, cwc_ref, cbc_ref).astype(BF16)
        return carry

    lax.fori_loop(0, nc, conv_body, 0)

    def make_chunk(rev):
        tau = (Q - 1 - row) if rev else row
        last = 0 if rev else Q - 1
        j0 = B_HPG if rev else 0
        pair_ok = (s2 >= t2) if rev else (t2 >= s2)

        def chunk(rows, st):
            dt = jax.nn.softplus(dt_ref[rows, :] + dtb_row)
            cs = _cumsum_rows(dt * a_row, tau, rev)
            cs_t = cs.T
            xs = xs_scr[rows, :]
            bm = b_scr[rows, :]
            cm = c_scr[rows, :]
            xdt = xs * _expand_heads(dt, j0, GW)
            ecs = _expand_heads(cs, j0, GW)
            ecs_last = ecs[last:last + 1, :]

            y = _dot(cm, st.astype(BF16)) * jnp.exp(ecs)

            scores = _dot_nt(cm, bm)
            for j in range(B_HPG):
                col = jnp.broadcast_to(cs[:, j0 + j:j0 + j + 1], (Q, Q))
                rw = jnp.broadcast_to(cs_t[j0 + j:j0 + j + 1, :], (Q, Q))
                decay = jnp.where(pair_ok, jnp.exp(jnp.minimum(col - rw, 0.0)), 0.0)
                in_head = (lane_gw >= j * B_HEADDIM) & (lane_gw < (j + 1) * B_HEADDIM)
                xh = jnp.where(in_head, xdt, 0.0).astype(BF16)
                y = y + _dot((scores * decay).astype(BF16), xh)

            xdec = (xdt * jnp.exp(ecs_last - ecs)).astype(BF16)
            return y, st * jnp.exp(ecs_last) + _dot_tn(bm, xdec)

        return chunk

    chunk_fns = (make_chunk(False), make_chunk(True))
    for st_scr in st_scrs:
        st_scr[...] = jnp.zeros_like(st_scr)

    def make_body(final):
        def body(ci, carry):
            for rev, chunk, st_scr in zip((False, True), chunk_fns, st_scrs):
                c = (nc - 1 - ci) if rev else ci
                rows = pl.ds(pl.multiple_of(c * Q, Q), Q)
                y, st = chunk(rows, st_scr[...])
                st_scr[...] = st
                if final:
                    tot = y_scr[rows, :] + y + dsk_ref[0] * xs_scr[rows, :]
                    tot = tot * _silu(z_ref[rows, :].astype(F32))
                    o_ref[rows, :] = _rms(tot, ng_ref[0]).astype(o_ref.dtype)
                else:
                    y_scr[rows, :] = y
            return carry

        return body

    assert nc % 2 == 0
    lax.fori_loop(0, nc // 2, make_body(False), 0)
    lax.fori_loop(nc // 2, nc, make_body(True), 0)


def _ssd_mixer(u, dt, conv_w, conv_b, a_log, dt_bias, d_skip, norm_g, batch, seq, col0):
    T = u.shape[0]
    G, GW, N = B_GROUPS, B_GW, B_STATE
    W = B_HEADS * B_HEADDIM
    assert col0 % GW == 0 and GW == 2 * N
    z_blk = col0 // GW
    x_blk = z_blk + W // GW
    b_blk = (col0 + 2 * W) // N
    c_blk = b_blk + G

    def pad_rows(w):
        return jnp.pad(w, ((0, SUBLANES - w.shape[0]), (0, 0)))

    cwx = pad_rows(conv_w[:, :W]).reshape(SUBLANES, G, GW).transpose(1, 0, 2)
    cwb = pad_rows(conv_w[:, W:W + G * N]).reshape(SUBLANES, G, N).transpose(1, 0, 2)
    cwc = pad_rows(conv_w[:, W + G * N:]).reshape(SUBLANES, G, N).transpose(1, 0, 2)
    cbx = conv_b[:W].reshape(G, 1, GW)
    cbb = conv_b[W:W + G * N].reshape(G, 1, N)
    cbc = conv_b[W + G * N:].reshape(G, 1, N)
    a_neg = -jnp.exp(a_log.astype(F32))
    per_dir = lambda v: jnp.concatenate([v[0].reshape(G, B_HPG), v[1].reshape(G, B_HPG)], axis=1)
    hp = jnp.stack([per_dir(a_neg), per_dir(dt_bias.astype(F32))], axis=1)
    hp = jnp.pad(hp, ((0, 0), (0, SUBLANES - 2), (0, LANES - 2 * B_HPG)))
    dsk = jnp.repeat(d_skip.astype(F32), B_HEADDIM).reshape(G, 1, GW)
    ng = norm_g.reshape(G, 1, GW)

    gspec = lambda shape: pl.BlockSpec((1,) + shape, lambda b, g: (g, 0, 0))
    return pl.pallas_call(
        _ssd_kernel,
        grid=(batch, G),
        in_specs=[pl.BlockSpec((seq, GW), lambda b, g: (b, z_blk + g)),
                  pl.BlockSpec((seq, GW), lambda b, g: (b, x_blk + g)),
                  pl.BlockSpec((seq, N), lambda b, g: (b, b_blk + g)),
                  pl.BlockSpec((seq, N), lambda b, g: (b, c_blk + g)),
                  pl.BlockSpec((seq, LANES), lambda b, g: (b, g)),
                  gspec((SUBLANES, GW)), gspec((SUBLANES, N)), gspec((SUBLANES, N)),
                  gspec((1, GW)), gspec((1, N)), gspec((1, N)),
                  gspec((SUBLANES, LANES)), gspec((1, GW)), gspec((1, GW))],
        out_specs=pl.BlockSpec((seq, GW), lambda b, g: (b, g)),
        out_shape=jax.ShapeDtypeStruct((T, W), BF16),
        scratch_shapes=[pltpu.VMEM((seq, GW), F32), pltpu.VMEM((seq, N), BF16),
                        pltpu.VMEM((seq, N), BF16), pltpu.VMEM((seq, GW), F32),
                        pltpu.VMEM((N, GW), F32), pltpu.VMEM((N, GW), F32)],
        compiler_params=_cparams(("parallel", "parallel")),
        name="ssd",
    )(u, u, u, u, dt, cwx, cwb, cwc, cbx, cbb, cbc, hp, dsk, ng)


def _na_kernel(q_ref, k_ref, v_ref, bias_ref, o_ref):
    S, HW = q_ref.shape
    W = GRID_W
    n_rows = S // W
    kh = min(NA_ROWS, n_rows)
    nk = kh * W
    n_pairs = bias_ref.shape[1] // NA_HG
    lane = lax.broadcasted_iota(jnp.int32, (W, HW), 1)
    head_masks = [(lane >= h * C_HEADDIM) & (lane < (h + 1) * C_HEADDIM) for h in range(NA_HG)]
    scale = C_HEADDIM ** -0.5

    def body(r, carry):
        rs = jnp.clip(r - kh // 2, 0, n_rows - kh)
        q = q_ref[pl.ds(pl.multiple_of(r * W, W), W), :] * jnp.asarray(scale, q_ref.dtype)
        kw = k_ref[pl.ds(pl.multiple_of(rs * W, W), nk), :]
        vw = v_ref[pl.ds(pl.multiple_of(rs * W, W), nk), :]
        zero = jnp.zeros_like(q)
        qs = jnp.concatenate([jnp.where(m, q, zero) for m in head_masks], axis=0)
        d0 = (NA_ROWS - 1) - (r - rs)
        bias = jnp.concatenate(
            [jnp.concatenate([bias_ref[0, h * n_pairs + d0 + 2 * m] for m in range(kh // 2)], axis=1)
             for h in range(NA_HG)], axis=0)
        s = _dot_nt(qs, kw) + bias
        m = jnp.max(s, axis=-1, keepdims=True)
        p = jnp.exp(s - m)
        l = jnp.sum(p, axis=-1, keepdims=True)
        res = _dot(p.astype(BF16), vw) / l
        out = jnp.zeros((W, HW), F32)
        for h in range(NA_HG):
            out = jnp.where(head_masks[h], res[h * W:(h + 1) * W, :], out)
        o_ref[pl.ds(pl.multiple_of(r * W, W), W), :] = out.astype(o_ref.dtype)
        return carry

    lax.fori_loop(0, n_rows, body, 0, unroll=NA_UNROLL)


def _na_bias_table(rpb):
    H, n_dr, n_dc = rpb.shape
    W = GRID_W
    c = np.arange(W)
    qs = np.clip(c - NA_COLS // 2, 0, W - NA_COLS)
    valid = (c[None, :] >= qs[:, None]) & (c[None, :] < qs[:, None] + NA_COLS)
    dc = np.clip(c[None, :] - c[:, None] + (NA_COLS - 1), 0, n_dc - 1)
    onehot = (dc[None] == np.arange(n_dc)[:, None, None]).astype(np.float32)
    tab = jnp.einsum("hrd,dcx->hrcx", rpb.astype(F32), jnp.asarray(onehot),
                     precision=lax.Precision.HIGHEST)
    tab = jnp.where(jnp.asarray(valid)[None, None], tab, -jnp.inf)
    pairs = jnp.concatenate([tab[:, :-1], tab[:, 1:]], axis=-1)
    return pairs.reshape(H // NA_HG, NA_HG * (n_dr - 1), W, 2 * W)


def _neighborhood_attention(qkv, rpb, batch, seq):
    T = qkv.shape[0]
    HW = NA_HG * C_HEADDIM
    n_hg = C_HEADS // NA_HG
    n_rows = seq // GRID_W
    assert n_rows >= NA_ROWS and NA_ROWS % 2 == 0
    table = _na_bias_table(rpb)
    return pl.pallas_call(
        _na_kernel,
        grid=(batch, n_hg),
        in_specs=[pl.BlockSpec((seq, HW), lambda b, g: (b, g)),
                  pl.BlockSpec((seq, HW), lambda b, g: (b, n_hg + g)),
                  pl.BlockSpec((seq, HW), lambda b, g: (b, 2 * n_hg + g)),
                  pl.BlockSpec((1,) + table.shape[1:], lambda b, g: (g, 0, 0, 0))],
        out_specs=pl.BlockSpec((seq, HW), lambda b, g: (b, g)),
        out_shape=jax.ShapeDtypeStruct((T, C_HEADS * C_HEADDIM), BF16),
        compiler_params=_cparams(("parallel", "parallel")),
        name="natten",
    )(qkv, qkv, qkv, table)


def _store_token_tiles(ref, val):
    tm, width = val.shape
    n = width // LANES
    for j in range(n):
        ref[pl.ds(j, tm, stride=n), :] = val[:, j * LANES:(j + 1) * LANES]


def _load_token_tiles(ref, n):
    tm = ref.shape[0] // n
    return jnp.concatenate([ref[pl.ds(j, tm, stride=n), :] for j in range(n)], axis=1)


def _router_kernel(x_ref, g_ref, wr_ref, o_ref, h_ref):
    h = _rms(x_ref[...], g_ref[...])
    _store_token_tiles(h_ref, h)
    logits = jnp.dot(h, wr_ref[...], preferred_element_type=F32, precision=lax.Precision.HIGHEST)
    lane = lax.broadcasted_iota(jnp.int32, logits.shape, 1)
    neg = -jnp.inf
    l1 = jnp.where(lane < N_EXPERTS, logits, neg)
    m1 = jnp.max(l1, axis=-1, keepdims=True)
    i1 = jnp.min(jnp.where(l1 == m1, lane, LANES), axis=-1, keepdims=True)
    l2 = jnp.where(lane == i1, neg, l1)
    m2 = jnp.max(l2, axis=-1, keepdims=True)
    i2 = jnp.min(jnp.where(l2 == m2, lane, LANES), axis=-1, keepdims=True)
    e = jnp.exp(m2 - m1)
    w1 = 1.0 / (1.0 + e)
    w2 = e / (1.0 + e)
    out = jnp.where(lane == 0, i1.astype(F32),
                    jnp.where(lane == 1, i2.astype(F32),
                              jnp.where(lane == 2, w1, jnp.where(lane == 3, w2, 0.0))))
    o_ref[...] = out


def _router(x, g, w_router, tm=512):
    T, D = x.shape
    n = D // LANES
    wr = jnp.pad(w_router.astype(F32), ((0, 0), (0, LANES - w_router.shape[1])))
    route, h = pl.pallas_call(
        _router_kernel,
        grid=(T // tm,),
        in_specs=[pl.BlockSpec((tm, D), lambda i: (i, 0)),
                  pl.BlockSpec((1, D), lambda i: (0, 0)),
                  pl.BlockSpec((D, LANES), lambda i: (0, 0))],
        out_specs=[pl.BlockSpec((tm, LANES), lambda i: (i, 0)),
                   pl.BlockSpec((tm * n, LANES), lambda i: (i, 0))],
        out_shape=[jax.ShapeDtypeStruct((T, LANES), F32),
                   jax.ShapeDtypeStruct((T * n, LANES), F32)],
        compiler_params=_cparams(("parallel",)),
        name="router",
    )(x, g.reshape(1, D), wr)
    return route, h.reshape(T, n, LANES)


SC_CORES = 2
SC_SUBCORES = 16
SC_WORKERS = SC_CORES * SC_SUBCORES
SC_CHUNK = 32


def _sc_mesh():
    return plsc.VectorSubcoreMesh(core_axis_name="c", subcore_axis_name="s",
                                  num_cores=SC_CORES, num_subcores=SC_SUBCORES)


def _sc_index_blocks(idx):
    return idx.reshape(SC_WORKERS, -1, SC_CHUNK)


def _sc_scatter_tokens(h, dest, n_rows):
    T, n, _ = h.shape
    per_worker = T // SC_WORKERS
    n_chunks = per_worker // SC_CHUNK
    assert per_worker * SC_WORKERS == T and n_chunks * SC_CHUNK == per_worker

    @functools.partial(
        pl.kernel, mesh=_sc_mesh(),
        out_type=jax.ShapeDtypeStruct((n_rows, n, LANES), h.dtype),
        scratch_types=[pltpu.VMEM((n_chunks, SC_CHUNK), jnp.int32),
                       pltpu.VMEM((n_chunks, SC_CHUNK), jnp.int32),
                       pltpu.VMEM((SC_CHUNK, n, LANES), h.dtype)],
        name="sc_scatter_tokens",
    )
    def scatter(h_hbm, d0_hbm, d1_hbm, o_hbm, i0_v, i1_v, rows_v):
        wid = lax.axis_index("s") * SC_CORES + lax.axis_index("c")
        pltpu.sync_copy(d0_hbm.at[wid], i0_v)
        pltpu.sync_copy(d1_hbm.at[wid], i1_v)

        @pl.loop(0, n_chunks)
        def _(j):
            pltpu.sync_copy(h_hbm.at[pl.ds(wid * per_worker + j * SC_CHUNK, SC_CHUNK)], rows_v)
            pltpu.sync_copy(rows_v, o_hbm.at[i0_v.at[j]])
            pltpu.sync_copy(rows_v, o_hbm.at[i1_v.at[j]])

    return scatter(h, _sc_index_blocks(dest[:, 0]), _sc_index_blocks(dest[:, 1]))


def _sc_gather_tokens(y, dest):
    T = dest.shape[0]
    _, n, _ = y.shape
    per_worker = T // SC_WORKERS
    n_chunks = per_worker // SC_CHUNK
    assert per_worker * SC_WORKERS == T and n_chunks * SC_CHUNK == per_worker
    out = jax.ShapeDtypeStruct((T, n, LANES), y.dtype)

    @functools.partial(
        pl.kernel, mesh=_sc_mesh(), out_type=(out, out),
        scratch_types=[pltpu.VMEM((n_chunks, SC_CHUNK), jnp.int32),
                       pltpu.VMEM((n_chunks, SC_CHUNK), jnp.int32),
                       pltpu.VMEM((SC_CHUNK, n, LANES), y.dtype)],
        name="sc_gather_tokens",
    )
    def gather(y_hbm, d0_hbm, d1_hbm, o0_hbm, o1_hbm, i0_v, i1_v, rows_v):
        wid = lax.axis_index("s") * SC_CORES + lax.axis_index("c")
        pltpu.sync_copy(d0_hbm.at[wid], i0_v)
        pltpu.sync_copy(d1_hbm.at[wid], i1_v)

        @pl.loop(0, n_chunks)
        def _(j):
            rows = pl.ds(wid * per_worker + j * SC_CHUNK, SC_CHUNK)
            pltpu.sync_copy(y_hbm.at[i0_v.at[j]], rows_v)
            pltpu.sync_copy(rows_v, o0_hbm.at[rows])
            pltpu.sync_copy(y_hbm.at[i1_v.at[j]], rows_v)
            pltpu.sync_copy(rows_v, o1_hbm.at[rows])

    return gather(y, _sc_index_blocks(dest[:, 0]), _sc_index_blocks(dest[:, 1]))


def _moe_ffn_kernel(te_ref, na_ref, h_ref, wg_ref, wu_ref, wd_ref, o_ref, acc_scr, *, tf):
    i = pl.program_id(0)
    D = acc_scr.shape[1]
    F = wg_ref.shape[2]

    @pl.when(i < na_ref[0])
    def _():
        h = _load_token_tiles(h_ref, D // LANES).astype(BF16)
        for f0 in range(0, F, tf):
            a = (_silu(_dot(h, wg_ref[0, :, f0:f0 + tf])) * _dot(h, wu_ref[0, :, f0:f0 + tf])).astype(BF16)
            part = _dot(a, wd_ref[0, f0:f0 + tf, :])
            if f0 == 0:
                acc_scr[...] = part
            else:
                acc_scr[...] += part
        _store_token_tiles(o_ref, acc_scr[...])

    @pl.when(i >= na_ref[0])
    def _():
        o_ref[...] = jnp.zeros_like(o_ref)


def _moe_ffn(hs, wg, wu, wd, tile_expert, n_active, tm=MOE_TM, tf=512):
    rows, n, _ = hs.shape
    E, D, F = wg.shape
    nt = tile_expert.shape[0]
    assert rows == nt * tm and n * LANES == D
    grid_spec = pltpu.PrefetchScalarGridSpec(
        num_scalar_prefetch=2,
        grid=(nt,),
        in_specs=[pl.BlockSpec((tm * n, LANES), lambda i, te, na: (i, 0)),
                  pl.BlockSpec((1, D, F), lambda i, te, na: (te[i], 0, 0)),
                  pl.BlockSpec((1, D, F), lambda i, te, na: (te[i], 0, 0)),
                  pl.BlockSpec((1, F, D), lambda i, te, na: (te[i], 0, 0))],
        out_specs=pl.BlockSpec((tm * n, LANES), lambda i, te, na: (i, 0)),
        scratch_shapes=[pltpu.VMEM((tm, D), F32)],
    )
    y = pl.pallas_call(
        functools.partial(_moe_ffn_kernel, tf=tf),
        grid_spec=grid_spec,
        out_shape=jax.ShapeDtypeStruct((rows * n, LANES), F32),
        compiler_params=_cparams(("arbitrary",)),
        name="moe_ffn",
    )(tile_expert, n_active, hs.reshape(rows * n, LANES), wg, wu, wd)
    return y.reshape(rows, n, LANES)


def _moe_combine_kernel(y0_ref, y1_ref, route_ref, x_ref, g_ref, o_ref):
    n = x_ref.shape[1] // LANES
    route = route_ref[...]
    moe = route[:, 2:3] * _load_token_tiles(y0_ref, n) + route[:, 3:4] * _load_token_tiles(y1_ref, n)
    o_ref[...] = x_ref[...] + _rms(moe, g_ref[...])


def _moe_combine(y0, y1, route, x, g, tm=512):
    T, D = x.shape
    n = D // LANES
    return pl.pallas_call(
        _moe_combine_kernel,
        grid=(T // tm,),
        in_specs=[pl.BlockSpec((tm * n, LANES), lambda i: (i, 0)),
                  pl.BlockSpec((tm * n, LANES), lambda i: (i, 0)),
                  pl.BlockSpec((tm, LANES), lambda i: (i, 0)),
                  pl.BlockSpec((tm, D), lambda i: (i, 0)),
                  pl.BlockSpec((1, D), lambda i: (0, 0))],
        out_specs=pl.BlockSpec((tm, D), lambda i: (i, 0)),
        out_shape=jax.ShapeDtypeStruct((T, D), F32),
        compiler_params=_cparams(("parallel",)),
        name="moe_combine",
    )(y0.reshape(T * n, LANES), y1.reshape(T * n, LANES), route, x, g.reshape(1, D))


def _moe_plan(route, tm=MOE_TM):
    T = route.shape[0]
    e = route[:, :TOP_K].astype(jnp.int32).reshape(-1)
    onehot = (e[:, None] == jnp.arange(N_EXPERTS, dtype=jnp.int32)[None, :]).astype(jnp.int32)
    csum = jnp.cumsum(onehot, axis=0)
    rank = jnp.sum(csum * onehot, axis=1) - 1
    counts = csum[-1]
    padded = ((counts + tm - 1) // tm) * tm
    ends = jnp.cumsum(padded)
    starts = ends - padded
    dest = jnp.sum(starts[None, :] * onehot, axis=1) + rank
    nt = (T * TOP_K) // tm + N_EXPERTS
    tile_start = jnp.arange(nt, dtype=jnp.int32) * tm
    tile_expert = jnp.minimum(jnp.sum((tile_start[:, None] >= ends[None, :]).astype(jnp.int32), axis=1),
                              N_EXPERTS - 1).astype(jnp.int32)
    n_active = (ends[-1] // tm).astype(jnp.int32).reshape(1)
    return tile_expert, n_active, dest.reshape(T, TOP_K).astype(jnp.int32)


def kernel(x, p, hgrn_lb_raw, e_norm_mix_pre, e_w_in, e_conv_w, e_conv_b, e_A_log, e_dt_bias, e_D, e_a_norm, e_b_norm, e_w_out, e_norm_mix_post, e_norm_ffn_pre, e_w_ffn_gate, e_w_ffn_up, e_w_ffn_down, e_norm_ffn_post, o_norm_mix_pre, o_w_qkv, o_rpb, o_w_out, o_norm_mix_post, o_norm_ffn_pre, o_w_router, o_w_exp_gate, o_w_exp_up, o_w_exp_down, o_norm_ffn_post, ple_norm_in, ple_w_gate, ple_w_proj, ple_norm_post):
    batch, seq, d_model = x.shape
    depth = p.shape[0]
    T = batch * seq
    xt = x.reshape(T, d_model)
    lb_all = jnp.cumsum(jax.nn.softmax(hgrn_lb_raw.astype(F32), axis=0), axis=0)

    a_kdim = A_HEADS * A_DK
    b_width = B_HEADS * B_HEADDIM
    conv_dim = b_width + 2 * B_GROUPS * B_STATE
    main_w = 5 * a_kdim + b_width + conv_dim

    for li in range(depth):
        j = li // 2
        if li % 2 == 0:
            w_in = e_w_in[j]
            dtf = w_in[:, main_w:main_w + B_HEADS].reshape(d_model, B_GROUPS, B_HPG)
            dtb = w_in[:, main_w + B_HEADS:].reshape(d_model, B_GROUPS, B_HPG)
            w_dt = jnp.pad(jnp.concatenate([dtf, dtb], axis=2), ((0, 0), (0, 0), (0, LANES - 2 * B_HPG)))
            w_all = jnp.concatenate([w_in[:, :main_w], w_dt.reshape(d_model, B_GROUPS * LANES)], axis=1)
            u, dt = _norm_proj(xt, e_norm_mix_pre[j], w_all.astype(BF16),
                               (main_w, B_GROUPS * LANES), (BF16, F32))
            o_a = _hgrn_mixer(u, lb_all[li], e_a_norm[j], batch, seq)
            o_b = _ssd_mixer(u, dt, e_conv_w[j], e_conv_b[j], e_A_log[j], e_dt_bias[j], e_D[j],
                             e_b_norm[j], batch, seq, 5 * a_kdim)
            w_out = e_w_out[j].astype(BF16)
            xt = _proj_norm_res([o_a, o_b], [w_out[:a_kdim], w_out[a_kdim:]], e_norm_mix_post[j], xt)
            xt = _ffn(xt, e_norm_ffn_pre[j], e_w_ffn_gate[j].astype(BF16), e_w_ffn_up[j].astype(BF16),
                      e_w_ffn_down[j].astype(BF16), e_norm_ffn_post[j])
        else:
            (qkv,) = _norm_proj(xt, o_norm_mix_pre[j], o_w_qkv[j].astype(BF16),
                                (3 * C_HEADS * C_HEADDIM,), (BF16,))
            o_c = _neighborhood_attention(qkv, o_rpb[j], batch, seq)
            xt = _proj_norm_res([o_c], [o_w_out[j].astype(BF16)], o_norm_mix_post[j], xt)
            route, h = _router(xt, o_norm_ffn_pre[j], o_w_router[j])
            tile_expert, n_active, dest = _moe_plan(route)
            hs = _sc_scatter_tokens(h, dest, tile_expert.shape[0] * MOE_TM)
            ys = _moe_ffn(hs, o_w_exp_gate[j].astype(BF16), o_w_exp_up[j].astype(BF16),
                          o_w_exp_down[j].astype(BF16), tile_expert, n_active)
            y0, y1 = _sc_gather_tokens(ys, dest)
            xt = _moe_combine(y0, y1, route, xt, o_norm_ffn_post[j])
        xt = _ple(xt, p[li].reshape(T, -1), ple_norm_in[li], ple_w_gate[li].astype(BF16),
                  ple_w_proj[li].astype(BF16), ple_norm_post[li])
    return xt.reshape(batch, seq, d_model)
```

```python
import functools
import math

import numpy as np
import jax
import jax.numpy as jnp
from jax import lax
from jax.experimental import pallas as pl
from jax.experimental.pallas import tpu as pltpu
from jax.experimental.pallas import tpu_sc as plsc

F32 = jnp.float32
BF16 = jnp.bfloat16
EPS = 1e-6

LANES = 128
SUBLANES = 8
VMEM_LIMIT_BYTES = 56 * 1024 * 1024

GRID_W = 64
A_HEADS, A_DK, A_CHUNK = 4, 128, 64
A_UNROLL = 4
B_HEADS, B_HEADDIM, B_GROUPS, B_STATE, B_CONV, B_CHUNK = 8, 64, 2, 128, 5, 128
B_HPG = B_HEADS // B_GROUPS
B_GW = B_HPG * B_HEADDIM
C_HEADS, C_HEADDIM = 16, 64
NA_ROWS, NA_COLS = 8, 16
NA_HG = 4
NA_UNROLL = 4
N_EXPERTS, TOP_K = 8, 2
MOE_TM = 256


def _cparams(sem):
    return pltpu.CompilerParams(dimension_semantics=sem, vmem_limit_bytes=VMEM_LIMIT_BYTES)


def _rms(x, g):
    return x * lax.rsqrt(jnp.mean(x * x, axis=-1, keepdims=True) + EPS) * g


def _silu(x):
    return x * jax.nn.sigmoid(x)


def _dot(a, b):
    return jnp.dot(a, b, preferred_element_type=F32)


def _dot_nt(a, b):
    return lax.dot_general(a, b, (((1,), (1,)), ((), ())), preferred_element_type=F32)


def _dot_tn(a, b):
    return lax.dot_general(a, b, (((0,), (0,)), ((), ())), preferred_element_type=F32)


def _norm_proj_kernel(x_ref, g_ref, w_ref, *o_refs, col_chunk):
    h = _rms(x_ref[...], g_ref[...]).astype(BF16)
    off = 0
    for o_ref in o_refs:
        n = o_ref.shape[1]
        for c0 in range(0, n, col_chunk):
            c1 = min(c0 + col_chunk, n)
            o_ref[:, c0:c1] = _dot(h, w_ref[:, off + c0:off + c1]).astype(o_ref.dtype)
        off += n


def _norm_proj(x, g, w, widths, dtypes, tm=512, col_chunk=1024):
    T, D = x.shape
    N = w.shape[1]
    assert sum(widths) == N and T % tm == 0
    return pl.pallas_call(
        functools.partial(_norm_proj_kernel, col_chunk=col_chunk),
        grid=(T // tm,),
        in_specs=[pl.BlockSpec((tm, D), lambda i: (i, 0)),
                  pl.BlockSpec((1, D), lambda i: (0, 0)),
                  pl.BlockSpec((D, N), lambda i: (0, 0))],
        out_specs=[pl.BlockSpec((tm, n), lambda i: (i, 0)) for n in widths],
        out_shape=[jax.ShapeDtypeStruct((T, n), dt) for n, dt in zip(widths, dtypes)],
        compiler_params=_cparams(("parallel",)),
        name="norm_proj",
    )(x, g.reshape(1, D), w)


def _ple_update(x, p_ref, gin_ref, wg_ref, wp_ref, gpost_ref):
    h = _rms(x, gin_ref[...]).astype(BF16)
    gate = jax.nn.sigmoid(_dot(h, wg_ref[...]))
    proj = _dot(p_ref[...].astype(BF16), wp_ref[...])
    return x + _rms(gate * proj, gpost_ref[...])


def _row_spec(tm, width, n_grid):
    return pl.BlockSpec((tm, width), (lambda i: (i, 0)) if n_grid == 1 else (lambda i, j: (i, 0)))


def _whole_spec(shape, n_grid):
    zeros = (0,) * len(shape)
    return pl.BlockSpec(shape, (lambda i: zeros) if n_grid == 1 else (lambda i, j: zeros))


def _mix_ffn_ple_kernel(oa_ref, ob_ref, wa_ref, wb_ref, gmix_ref, x_ref, gpre_ref, wg_ref, wu_ref, wd_ref,
                        gpost_ref, p_ref, gin_ref, wpg_ref, wpp_ref, gple_ref, o_ref,
                        x1_scr, h_scr, acc_scr):
    j = pl.program_id(1)

    @pl.when(j == 0)
    def _():
        mix = _dot(oa_ref[...], wa_ref[...]) + _dot(ob_ref[...], wb_ref[...])
        x1 = x_ref[...] + _rms(mix, gmix_ref[...])
        x1_scr[...] = x1
        h_scr[...] = _rms(x1, gpre_ref[...]).astype(BF16)
        acc_scr[...] = jnp.zeros_like(acc_scr)

    h = h_scr[...]
    a = (_silu(_dot(h, wg_ref[...])) * _dot(h, wu_ref[...])).astype(BF16)
    acc_scr[...] += _dot(a, wd_ref[...])

    @pl.when(j == pl.num_programs(1) - 1)
    def _():
        x2 = x1_scr[...] + _rms(acc_scr[...], gpost_ref[...])
        o_ref[...] = _ple_update(x2, p_ref, gin_ref, wpg_ref, wpp_ref, gple_ref)


def _mix_ffn_ple(o_a, o_b, w_a, w_b, g_mix, x, g_pre, wg, wu, wd, g_post, p, g_in, wpg, wpp, g_ple,
                 tm=512, tf=1408):
    T, D = x.shape
    F = wg.shape[1]
    assert F % tf == 0 and T % tm == 0
    vec = lambda g: g.reshape(1, D)
    return pl.pallas_call(
        _mix_ffn_ple_kernel,
        grid=(T // tm, F // tf),
        in_specs=[_row_spec(tm, o_a.shape[1], 2), _row_spec(tm, o_b.shape[1], 2),
                  _whole_spec(w_a.shape, 2), _whole_spec(w_b.shape, 2), _whole_spec((1, D), 2),
                  _row_spec(tm, D, 2), _whole_spec((1, D), 2),
                  pl.BlockSpec((D, tf), lambda i, j: (0, j)),
                  pl.BlockSpec((D, tf), lambda i, j: (0, j)),
                  pl.BlockSpec((tf, D), lambda i, j: (j, 0)),
                  _whole_spec((1, D), 2),
                  _row_spec(tm, p.shape[1], 2), _whole_spec((1, D), 2),
                  _whole_spec(wpg.shape, 2), _whole_spec(wpp.shape, 2), _whole_spec((1, D), 2)],
        out_specs=_row_spec(tm, D, 2),
        out_shape=jax.ShapeDtypeStruct((T, D), F32),
        scratch_shapes=[pltpu.VMEM((tm, D), F32), pltpu.VMEM((tm, D), BF16), pltpu.VMEM((tm, D), F32)],
        compiler_params=_cparams(("parallel", "arbitrary")),
        name="mix_ffn_ple",
    )(o_a, o_b, w_a, w_b, vec(g_mix), x, vec(g_pre), wg, wu, wd, vec(g_post), p, vec(g_in), wpg, wpp,
      vec(g_ple))


def _roll_rows(x, s, rev):
    n = x.shape[0]
    return pltpu.roll(x, (n - s) if rev else s, 0)


def _cumsum_rows(x, tau, rev):
    n = x.shape[0]
    s = 1
    while s < n:
        if s % SUBLANES:
            shifted = jnp.where(tau >= s, _roll_rows(x, s, rev), 0.0)
        else:
            zeros = jnp.zeros((s,) + x.shape[1:], x.dtype)
            shifted = (jnp.concatenate([x[s:], zeros], axis=0) if rev
                       else jnp.concatenate([zeros, x[:n - s]], axis=0))
        x = x + shifted
        s *= 2
    return x


def _hgrn_kernel(q_ref, ff_ref, fb_ref, i_ref, g_ref, lb_ref, ng_ref, o_ref, acc_scr, *st_scrs):
    S, DK = q_ref.shape
    C = A_CHUNK
    nc = S // C
    assert nc % (2 * A_UNROLL) == 0
    lb = lb_ref[0]
    ng = ng_ref[...]
    row = lax.broadcasted_iota(jnp.int32, (C, DK), 0)
    t2 = lax.broadcasted_iota(jnp.int32, (C, C), 0)
    s2 = lax.broadcasted_iota(jnp.int32, (C, C), 1)
    levels = [C >> (i + 1) for i in range(C.bit_length() - 1)]

    def make_chunk(rev, f_ref):
        tau = (C - 1 - row) if rev else row
        upper_half = {L: (tau & (2 * L - 1)) >= L for L in levels}
        tau2 = (C - 1 - t2) if rev else t2
        sig2 = (C - 1 - s2) if rev else s2
        pair = {L: ((t2 & -(2 * L)) == (s2 & -(2 * L)))
                   & ((tau2 & (2 * L - 1)) >= L) & ((sig2 & (2 * L - 1)) < L) for L in levels}
        last = 0 if rev else C - 1

        def ref_rows(b, L):
            off = L if rev else L - 1
            if 2 * L >= SUBLANES:
                pieces = [jnp.broadcast_to(b[j * 2 * L + off:j * 2 * L + off + 1, :], (2 * L, DK))
                          for j in range(C // (2 * L))]
                return pieces[0] if len(pieces) == 1 else jnp.concatenate(pieces, axis=0)
            if L == 1:
                return jnp.where(upper_half[1], _roll_rows(b, 1, rev), b)
            b3 = b.reshape(C // SUBLANES, SUBLANES, DK)
            sub = lax.broadcasted_iota(jnp.int32, b3.shape, 1)
            out = None
            for j in range(SUBLANES // (2 * L)):
                piece = jnp.broadcast_to(b3[:, j * 2 * L + off:j * 2 * L + off + 1, :], b3.shape)
                out = piece if out is None else jnp.where(sub >= j * 2 * L, piece, out)
            return out.reshape(C, DK)

        def chunk(rows, st):
            q = _silu(q_ref[rows, :].astype(F32))
            f = lb + (1.0 - lb) * jax.nn.sigmoid(f_ref[rows, :].astype(F32))
            k = 1.0 - f
            v = i_ref[rows, :].astype(F32)
            vb = v.astype(BF16)
            b = _cumsum_rows(jnp.log(f), tau, rev)

            o = _dot_nt((q * jnp.exp(b)).astype(BF16), st.astype(BF16))

            attn = jnp.where(t2 == s2, _dot_nt(q.astype(BF16), k.astype(BF16)), 0.0)
            for L in levels:
                scaled = (jnp.where(upper_half[L], q, k) * jnp.exp(-jnp.abs(b - ref_rows(b, L)))).astype(BF16)
                attn = jnp.where(pair[L], _dot_nt(scaled, scaled), attn)
            o = o + _dot(attn.astype(BF16), vb)

            b_last = b[last:last + 1, :]
            khat = (k * jnp.exp(b_last - b)).astype(BF16)
            return o, st * jnp.exp(b_last) + _dot_tn(vb, khat)

        return chunk

    chunk_fns = (make_chunk(False, ff_ref), make_chunk(True, fb_ref))
    for st_scr in st_scrs:
        st_scr[...] = jnp.zeros_like(st_scr)

    def make_body(final):
        def body(it, carry):
            for rev, chunk, st_scr in zip((False, True), chunk_fns, st_scrs):
                st = st_scr[...]
                for u in range(A_UNROLL):
                    ci = it * A_UNROLL + u
                    c = (nc - 1 - ci) if rev else ci
                    rows = pl.ds(pl.multiple_of(c * C, C), C)
                    o, st = chunk(rows, st)
                    if final:
                        tot = acc_scr[rows, :] + o
                        gate = _silu(g_ref[rows, :].astype(F32))
                        o_ref[rows, :] = (_rms(tot, ng) * gate).astype(o_ref.dtype)
                    else:
                        acc_scr[rows, :] = o
                st_scr[...] = st
            return carry

        return body

    trips = nc // A_UNROLL
    lax.fori_loop(0, trips // 2, make_body(False), 0)
    lax.fori_loop(trips // 2, trips, make_body(True), 0)


def _hgrn_mixer(u, lb, norm_g, batch, seq):
    T = u.shape[0]
    H, DK = A_HEADS, A_DK

    def spec(off):
        return pl.BlockSpec((seq, DK), lambda b, h, off=off: (b, off + h))

    return pl.pallas_call(
        _hgrn_kernel,
        grid=(batch, H),
        in_specs=[spec(0), spec(H), spec(2 * H), spec(3 * H), spec(4 * H),
                  pl.BlockSpec((1, 1, DK), lambda b, h: (h, 0, 0)),
                  pl.BlockSpec((1, DK), lambda b, h: (0, 0))],
        out_specs=pl.BlockSpec((seq, DK), lambda b, h: (b, h)),
        out_shape=jax.ShapeDtypeStruct((T, H * DK), BF16),
        scratch_shapes=[pltpu.VMEM((seq, DK), F32), pltpu.VMEM((DK, DK), F32),
                        pltpu.VMEM((DK, DK), F32)],
        compiler_params=_cparams(("parallel", "parallel")),
        name="hgrn2",
    )(u, u, u, u, u, lb.reshape(H, 1, DK), norm_g.reshape(1, DK))


def _expand_heads(col, j0, width):
    q = col.shape[0]
    lane = lax.broadcasted_iota(jnp.int32, (q, width), 1)
    out = jnp.broadcast_to(col[:, j0 + B_HPG - 1:j0 + B_HPG], (q, width))
    for j in range(B_HPG - 2, -1, -1):
        out = jnp.where(lane < (j + 1) * B_HEADDIM,
                        jnp.broadcast_to(col[:, j0 + j:j0 + j + 1], (q, width)), out)
    return out


def _ssd_kernel(z_ref, x_ref, bm_ref, cm_ref, dt_ref, cwx_ref, cwb_ref, cwc_ref,
                cbx_ref, cbb_ref, cbc_ref, hp_ref, dsk_ref, ng_ref, o_ref,
                xs_scr, b_scr, c_scr, y_scr, *st_scrs):
    S = x_ref.shape[0]
    Q = B_CHUNK
    nc = S // Q
    GW = B_GW
    hp = hp_ref[0]
    a_row, dtb_row = hp[0:1, :], hp[1:2, :]
    row = lax.broadcasted_iota(jnp.int32, (Q, LANES), 0)
    t2 = lax.broadcasted_iota(jnp.int32, (Q, Q), 0)
    s2 = lax.broadcasted_iota(jnp.int32, (Q, Q), 1)
    lane_gw = lax.broadcasted_iota(jnp.int32, (Q, GW), 1)
    halo = SUBLANES

    def conv_body(c, carry):
        r0 = pl.multiple_of(c * Q, Q)
        rp = pl.multiple_of(jnp.maximum(r0 - halo, 0), halo)
        rn = pl.multiple_of(jnp.minimum(r0 + Q, S - halo), halo)
        has_prev = c > 0
        has_next = c < nc - 1

        def conv(src_ref, w_ref, bias_ref):
            prev = jnp.where(has_prev, src_ref[pl.ds(rp, halo), :].astype(F32), 0.0)
            nxt = jnp.where(has_next, src_ref[pl.ds(rn, halo), :].astype(F32), 0.0)
            xx = jnp.concatenate([prev, src_ref[pl.ds(r0, Q), :].astype(F32), nxt], axis=0)
            n = Q + 2 * halo
            w = w_ref[0]
            acc = None
            for j in range(B_CONV):
                delta = j - B_CONV // 2
                sh = xx if delta == 0 else pltpu.roll(xx, (-delta) % n, 0)
                term = sh[halo:halo + Q, :] * w[j:j + 1, :]
                acc = term if acc is None else acc + term
            return _silu(acc + bias_ref[0])

        xs_scr[pl.ds(r0, Q), :] = conv(x_ref, cwx_ref, cbx_ref)
        b_scr[pl.ds(r0, Q), :] = conv(bm_ref, cwb_ref, cbb_ref).astype(BF16)
        c_scr[pl.ds(r0, Q), :] = conv(cm_ref, cwc_ref, cbc_ref).astype(BF16)
        return carry

    lax.fori_loop(0, nc, conv_body, 0)

    def make_chunk(rev):
        tau = (Q - 1 - row) if rev else row
        last = 0 if rev else Q - 1
        j0 = B_HPG if rev else 0
        pair_ok = (s2 >= t2) if rev else (t2 >= s2)

        def chunk(rows, st):
            dt = jax.nn.softplus(dt_ref[rows, :] + dtb_row)
            cs = _cumsum_rows(dt * a_row, tau, rev)
            cs_t = cs.T
            xs = xs_scr[rows, :]
            bm = b_scr[rows, :]
            cm = c_scr[rows, :]
            xdt = xs * _expand_heads(dt, j0, GW)
            ecs = _expand_heads(cs, j0, GW)
            ecs_last = ecs[last:last + 1, :]

            y = _dot(cm, st.astype(BF16)) * jnp.exp(ecs)

            scores = _dot_nt(cm, bm)
            for j in range(B_HPG):
                col = jnp.broadcast_to(cs[:, j0 + j:j0 + j + 1], (Q, Q))
                rw = jnp.broadcast_to(cs_t[j0 + j:j0 + j + 1, :], (Q, Q))
                decay = jnp.where(pair_ok, jnp.exp(jnp.minimum(col - rw, 0.0)), 0.0)
                in_head = (lane_gw >= j * B_HEADDIM) & (lane_gw < (j + 1) * B_HEADDIM)
                xh = jnp.where(in_head, xdt, 0.0).astype(BF16)
                y = y + _dot((scores * decay).astype(BF16), xh)

            xdec = (xdt * jnp.exp(ecs_last - ecs)).astype(BF16)
            return y, st * jnp.exp(ecs_last) + _dot_tn(bm, xdec)

        return chunk

    chunk_fns = (make_chunk(False), make_chunk(True))
    for st_scr in st_scrs:
        st_scr[...] = jnp.zeros_like(st_scr)

    def make_body(final):
        def body(ci, carry):
            for rev, chunk, st_scr in zip((False, True), chunk_fns, st_scrs):
                c = (nc - 1 - ci) if rev else ci
                rows = pl.ds(pl.multiple_of(c * Q, Q), Q)
                y, st = chunk(rows, st_scr[...])
                st_scr[...] = st
                if final:
                    tot = y_scr[rows, :] + y + dsk_ref[0] * xs_scr[rows, :]
                    tot = tot * _silu(z_ref[rows, :].astype(F32))
                    o_ref[rows, :] = _rms(tot, ng_ref[0]).astype(o_ref.dtype)
                else:
                    y_scr[rows, :] = y
            return carry

        return body

    assert nc % 2 == 0
    lax.fori_loop(0, nc // 2, make_body(False), 0)
    lax.fori_loop(nc // 2, nc, make_body(True), 0)


def _ssd_mixer(u, dt, conv_w, conv_b, a_log, dt_bias, d_skip, norm_g, batch, seq, col0):
    T = u.shape[0]
    G, GW, N = B_GROUPS, B_GW, B_STATE
    W = B_HEADS * B_HEADDIM
    assert col0 % GW == 0 and GW == 2 * N
    z_blk = col0 // GW
    x_blk = z_blk + W // GW
    b_blk = (col0 + 2 * W) // N
    c_blk = b_blk + G

    def pad_rows(w):
        return jnp.pad(w, ((0, SUBLANES - w.shape[0]), (0, 0)))

    cwx = pad_rows(conv_w[:, :W]).reshape(SUBLANES, G, GW).transpose(1, 0, 2)
    cwb = pad_rows(conv_w[:, W:W + G * N]).reshape(SUBLANES, G, N).transpose(1, 0, 2)
    cwc = pad_rows(conv_w[:, W + G * N:]).reshape(SUBLANES, G, N).transpose(1, 0, 2)
    cbx = conv_b[:W].reshape(G, 1, GW)
    cbb = conv_b[W:W + G * N].reshape(G, 1, N)
    cbc = conv_b[W + G * N:].reshape(G, 1, N)
    a_neg = -jnp.exp(a_log.astype(F32))
    per_dir = lambda v: jnp.concatenate([v[0].reshape(G, B_HPG), v[1].reshape(G, B_HPG)], axis=1)
    hp = jnp.stack([per_dir(a_neg), per_dir(dt_bias.astype(F32))], axis=1)
    hp = jnp.pad(hp, ((0, 0), (0, SUBLANES - 2), (0, LANES - 2 * B_HPG)))
    dsk = jnp.repeat(d_skip.astype(F32), B_HEADDIM).reshape(G, 1, GW)
    ng = norm_g.reshape(G, 1, GW)

    gspec = lambda shape: pl.BlockSpec((1,) + shape, lambda b, g: (g, 0, 0))
    return pl.pallas_call(
        _ssd_kernel,
        grid=(batch, G),
        in_specs=[pl.BlockSpec((seq, GW), lambda b, g: (b, z_blk + g)),
                  pl.BlockSpec((seq, GW), lambda b, g: (b, x_blk + g)),
                  pl.BlockSpec((seq, N), lambda b, g: (b, b_blk + g)),
                  pl.BlockSpec((seq, N), lambda b, g: (b, c_blk + g)),
                  pl.BlockSpec((seq, LANES), lambda b, g: (b, g)),
                  gspec((SUBLANES, GW)), gspec((SUBLANES, N)), gspec((SUBLANES, N)),
                  gspec((1, GW)), gspec((1, N)), gspec((1, N)),
                  gspec((SUBLANES, LANES)), gspec((1, GW)), gspec((1, GW))],
        out_specs=pl.BlockSpec((seq, GW), lambda b, g: (b, g)),
        out_shape=jax.ShapeDtypeStruct((T, W), BF16),
        scratch_shapes=[pltpu.VMEM((seq, GW), F32), pltpu.VMEM((seq, N), BF16),
                        pltpu.VMEM((seq, N), BF16), pltpu.VMEM((seq, GW), F32),
                        pltpu.VMEM((N, GW), F32), pltpu.VMEM((N, GW), F32)],
        compiler_params=_cparams(("parallel", "parallel")),
        name="ssd",
    )(u, u, u, u, dt, cwx, cwb, cwc, cbx, cbb, cbc, hp, dsk, ng)


def _na_kernel(q_ref, k_ref, v_ref, bias_ref, o_ref):
    S, HW = q_ref.shape
    W = GRID_W
    n_rows = S // W
    kh = min(NA_ROWS, n_rows)
    nk = kh * W
    n_pairs = bias_ref.shape[1] // NA_HG
    lane = lax.broadcasted_iota(jnp.int32, (W, HW), 1)
    head_masks = [(lane >= h * C_HEADDIM) & (lane < (h + 1) * C_HEADDIM) for h in range(NA_HG)]
    scale = C_HEADDIM ** -0.5

    def body(r, carry):
        rs = jnp.clip(r - kh // 2, 0, n_rows - kh)
        q = q_ref[pl.ds(pl.multiple_of(r * W, W), W), :] * jnp.asarray(scale, q_ref.dtype)
        kw = k_ref[pl.ds(pl.multiple_of(rs * W, W), nk), :]
        vw = v_ref[pl.ds(pl.multiple_of(rs * W, W), nk), :]
        zero = jnp.zeros_like(q)
        qs = jnp.concatenate([jnp.where(m, q, zero) for m in head_masks], axis=0)
        d0 = (NA_ROWS - 1) - (r - rs)
        bias = jnp.concatenate(
            [jnp.concatenate([bias_ref[0, h * n_pairs + d0 + 2 * m] for m in range(kh // 2)], axis=1)
             for h in range(NA_HG)], axis=0)
        s = _dot_nt(qs, kw) + bias
        m = jnp.max(s, axis=-1, keepdims=True)
        p = jnp.exp(s - m)
        l = jnp.sum(p, axis=-1, keepdims=True)
        res = _dot(p.astype(BF16), vw) / l
        out = jnp.zeros((W, HW), F32)
        for h in range(NA_HG):
            out = jnp.where(head_masks[h], res[h * W:(h + 1) * W, :], out)
        o_ref[pl.ds(pl.multiple_of(r * W, W), W), :] = out.astype(o_ref.dtype)
        return carry

    lax.fori_loop(0, n_rows, body, 0, unroll=NA_UNROLL)


def _na_bias_table(rpb):
    H, n_dr, n_dc = rpb.shape
    W = GRID_W
    c = np.arange(W)
    qs = np.clip(c - NA_COLS // 2, 0, W - NA_COLS)
    valid = (c[None, :] >= qs[:, None]) & (c[None, :] < qs[:, None] + NA_COLS)
    dc = np.clip(c[None, :] - c[:, None] + (NA_COLS - 1), 0, n_dc - 1)
    onehot = (dc[None] == np.arange(n_dc)[:, None, None]).astype(np.float32)
    tab = jnp.einsum("hrd,dcx->hrcx", rpb.astype(F32), jnp.asarray(onehot),
                     precision=lax.Precision.HIGHEST)
    tab = jnp.where(jnp.asarray(valid)[None, None], tab, -jnp.inf)
    pairs = jnp.concatenate([tab[:, :-1], tab[:, 1:]], axis=-1)
    return pairs.reshape(H // NA_HG, NA_HG * (n_dr - 1), W, 2 * W)


def _neighborhood_attention(qkv, rpb, batch, seq):
    T = qkv.shape[0]
    HW = NA_HG * C_HEADDIM
    n_hg = C_HEADS // NA_HG
    n_rows = seq // GRID_W
    assert n_rows >= NA_ROWS and NA_ROWS % 2 == 0
    table = _na_bias_table(rpb)
    return pl.pallas_call(
        _na_kernel,
        grid=(batch, n_hg),
        in_specs=[pl.BlockSpec((seq, HW), lambda b, g: (b, g)),
                  pl.BlockSpec((seq, HW), lambda b, g: (b, n_hg + g)),
                  pl.BlockSpec((seq, HW), lambda b, g: (b, 2 * n_hg + g)),
                  pl.BlockSpec((1,) + table.shape[1:], lambda b, g: (g, 0, 0, 0))],
        out_specs=pl.BlockSpec((seq, HW), lambda b, g: (b, g)),
        out_shape=jax.ShapeDtypeStruct((T, C_HEADS * C_HEADDIM), BF16),
        compiler_params=_cparams(("parallel", "parallel")),
        name="natten",
    )(qkv, qkv, qkv, table)


def _store_token_tiles(ref, val):
    tm, width = val.shape
    n = width // LANES
    for j in range(n):
        ref[pl.ds(j, tm, stride=n), :] = val[:, j * LANES:(j + 1) * LANES]


def _load_token_tiles(ref, n):
    tm = ref.shape[0] // n
    return jnp.concatenate([ref[pl.ds(j, tm, stride=n), :] for j in range(n)], axis=1)


def _proj_router_kernel(a_ref, w_ref, gmix_ref, x_ref, g_ref, wr_ref, x1_ref, o_ref, h_ref):
    x1 = x_ref[...] + _rms(_dot(a_ref[...], w_ref[...]), gmix_ref[...])
    x1_ref[...] = x1
    h = _rms(x1, g_ref[...])
    _store_token_tiles(h_ref, h)
    logits = jnp.dot(h, wr_ref[...], preferred_element_type=F32, precision=lax.Precision.HIGHEST)
    lane = lax.broadcasted_iota(jnp.int32, logits.shape, 1)
    neg = -jnp.inf
    l1 = jnp.where(lane < N_EXPERTS, logits, neg)
    m1 = jnp.max(l1, axis=-1, keepdims=True)
    i1 = jnp.min(jnp.where(l1 == m1, lane, LANES), axis=-1, keepdims=True)
    l2 = jnp.where(lane == i1, neg, l1)
    m2 = jnp.max(l2, axis=-1, keepdims=True)
    i2 = jnp.min(jnp.where(l2 == m2, lane, LANES), axis=-1, keepdims=True)
    e = jnp.exp(m2 - m1)
    w1 = 1.0 / (1.0 + e)
    w2 = e / (1.0 + e)
    out = jnp.where(lane == 0, i1.astype(F32),
                    jnp.where(lane == 1, i2.astype(F32),
                              jnp.where(lane == 2, w1, jnp.where(lane == 3, w2, 0.0))))
    o_ref[...] = out


def _proj_router(a, w, g_mix, x, g, w_router, tm=512):
    T, D = x.shape
    n = D // LANES
    wr = jnp.pad(w_router.astype(F32), ((0, 0), (0, LANES - w_router.shape[1])))
    x1, route, h = pl.pallas_call(
        _proj_router_kernel,
        grid=(T // tm,),
        in_specs=[_row_spec(tm, a.shape[1], 1), _whole_spec(w.shape, 1), _whole_spec((1, D), 1),
                  _row_spec(tm, D, 1), _whole_spec((1, D), 1), _whole_spec((D, LANES), 1)],
        out_specs=[_row_spec(tm, D, 1), _row_spec(tm, LANES, 1), _row_spec(tm * n, LANES, 1)],
        out_shape=[jax.ShapeDtypeStruct((T, D), F32),
                   jax.ShapeDtypeStruct((T, LANES), F32),
                   jax.ShapeDtypeStruct((T * n, LANES), F32)],
        compiler_params=_cparams(("parallel",)),
        name="proj_router",
    )(a, w, g_mix.reshape(1, D), x, g.reshape(1, D), wr)
    return x1, route, h.reshape(T, n, LANES)


SC_CORES = 2
SC_SUBCORES = 16
SC_WORKERS = SC_CORES * SC_SUBCORES
SC_CHUNK = 32


def _sc_mesh():
    return plsc.VectorSubcoreMesh(core_axis_name="c", subcore_axis_name="s",
                                  num_cores=SC_CORES, num_subcores=SC_SUBCORES)


def _sc_index_blocks(idx):
    return idx.reshape(SC_WORKERS, -1, SC_CHUNK)


def _sc_scatter_tokens(h, dest, n_rows):
    T, n, _ = h.shape
    per_worker = T // SC_WORKERS
    n_chunks = per_worker // SC_CHUNK
    assert per_worker * SC_WORKERS == T and n_chunks * SC_CHUNK == per_worker

    @functools.partial(
        pl.kernel, mesh=_sc_mesh(),
        out_type=jax.ShapeDtypeStruct((n_rows, n, LANES), h.dtype),
        scratch_types=[pltpu.VMEM((n_chunks, SC_CHUNK), jnp.int32),
                       pltpu.VMEM((n_chunks, SC_CHUNK), jnp.int32),
                       pltpu.VMEM((SC_CHUNK, n, LANES), h.dtype)],
        name="sc_scatter_tokens",
    )
    def scatter(h_hbm, d0_hbm, d1_hbm, o_hbm, i0_v, i1_v, rows_v):
        wid = lax.axis_index("s") * SC_CORES + lax.axis_index("c")
        pltpu.sync_copy(d0_hbm.at[wid], i0_v)
        pltpu.sync_copy(d1_hbm.at[wid], i1_v)

        @pl.loop(0, n_chunks)
        def _(j):
            pltpu.sync_copy(h_hbm.at[pl.ds(wid * per_worker + j * SC_CHUNK, SC_CHUNK)], rows_v)
            pltpu.sync_copy(rows_v, o_hbm.at[i0_v.at[j]])
            pltpu.sync_copy(rows_v, o_hbm.at[i1_v.at[j]])

    return scatter(h, _sc_index_blocks(dest[:, 0]), _sc_index_blocks(dest[:, 1]))


def _sc_gather_tokens(y, dest):
    T = dest.shape[0]
    _, n, _ = y.shape
    per_worker = T // SC_WORKERS
    n_chunks = per_worker // SC_CHUNK
    assert per_worker * SC_WORKERS == T and n_chunks * SC_CHUNK == per_worker
    out = jax.ShapeDtypeStruct((T, n, LANES), y.dtype)

    @functools.partial(
        pl.kernel, mesh=_sc_mesh(), out_type=(out, out),
        scratch_types=[pltpu.VMEM((n_chunks, SC_CHUNK), jnp.int32),
                       pltpu.VMEM((n_chunks, SC_CHUNK), jnp.int32),
                       pltpu.VMEM((SC_CHUNK, n, LANES), y.dtype)],
        name="sc_gather_tokens",
    )
    def gather(y_hbm, d0_hbm, d1_hbm, o0_hbm, o1_hbm, i0_v, i1_v, rows_v):
        wid = lax.axis_index("s") * SC_CORES + lax.axis_index("c")
        pltpu.sync_copy(d0_hbm.at[wid], i0_v)
        pltpu.sync_copy(d1_hbm.at[wid], i1_v)

        @pl.loop(0, n_chunks)
        def _(j):
            rows = pl.ds(wid * per_worker + j * SC_CHUNK, SC_CHUNK)
            pltpu.sync_copy(y_hbm.at[i0_v.at[j]], rows_v)
            pltpu.sync_copy(rows_v, o0_hbm.at[rows])
            pltpu.sync_copy(y_hbm.at[i1_v.at[j]], rows_v)
            pltpu.sync_copy(rows_v, o1_hbm.at[rows])

    return gather(y, _sc_index_blocks(dest[:, 0]), _sc_index_blocks(dest[:, 1]))


def _moe_ffn_kernel(te_ref, na_ref, h_ref, wg_ref, wu_ref, wd_ref, o_ref, acc_scr, *, tf):
    i = pl.program_id(0)
    D = acc_scr.shape[1]
    F = wg_ref.shape[2]

    @pl.when(i < na_ref[0])
    def _():
        h = _load_token_tiles(h_ref, D // LANES).astype(BF16)
        for f0 in range(0, F, tf):
            a = (_silu(_dot(h, wg_ref[0, :, f0:f0 + tf])) * _dot(h, wu_ref[0, :, f0:f0 + tf])).astype(BF16)
            part = _dot(a, wd_ref[0, f0:f0 + tf, :])
            if f0 == 0:
                acc_scr[...] = part
            else:
                acc_scr[...] += part
        _store_token_tiles(o_ref, acc_scr[...])

    @pl.when(i >= na_ref[0])
    def _():
        o_ref[...] = jnp.zeros_like(o_ref)


def _moe_ffn(hs, wg, wu, wd, tile_expert, n_active, tm=MOE_TM, tf=512):
    rows, n, _ = hs.shape
    E, D, F = wg.shape
    nt = tile_expert.shape[0]
    assert rows == nt * tm and n * LANES == D
    grid_spec = pltpu.PrefetchScalarGridSpec(
        num_scalar_prefetch=2,
        grid=(nt,),
        in_specs=[pl.BlockSpec((tm * n, LANES), lambda i, te, na: (i, 0)),
                  pl.BlockSpec((1, D, F), lambda i, te, na: (te[i], 0, 0)),
                  pl.BlockSpec((1, D, F), lambda i, te, na: (te[i], 0, 0)),
                  pl.BlockSpec((1, F, D), lambda i, te, na: (te[i], 0, 0))],
        out_specs=pl.BlockSpec((tm * n, LANES), lambda i, te, na: (i, 0)),
        scratch_shapes=[pltpu.VMEM((tm, D), F32)],
    )
    y = pl.pallas_call(
        functools.partial(_moe_ffn_kernel, tf=tf),
        grid_spec=grid_spec,
        out_shape=jax.ShapeDtypeStruct((rows * n, LANES), F32),
        compiler_params=_cparams(("arbitrary",)),
        name="moe_ffn",
    )(tile_expert, n_active, hs.reshape(rows * n, LANES), wg, wu, wd)
    return y.reshape(rows, n, LANES)


def _combine_ple_kernel(y0_ref, y1_ref, route_ref, x_ref, g_ref, p_ref, gin_ref, wpg_ref, wpp_ref,
                        gple_ref, o_ref):
    n = x_ref.shape[1] // LANES
    route = route_ref[...]
    moe = route[:, 2:3] * _load_token_tiles(y0_ref, n) + route[:, 3:4] * _load_token_tiles(y1_ref, n)
    x2 = x_ref[...] + _rms(moe, g_ref[...])
    o_ref[...] = _ple_update(x2, p_ref, gin_ref, wpg_ref, wpp_ref, gple_ref)


def _combine_ple(y0, y1, route, x, g, p, g_in, wpg, wpp, g_ple, tm=512):
    T, D = x.shape
    n = D // LANES
    vec = lambda v: v.reshape(1, D)
    return pl.pallas_call(
        _combine_ple_kernel,
        grid=(T // tm,),
        in_specs=[_row_spec(tm * n, LANES, 1), _row_spec(tm * n, LANES, 1), _row_spec(tm, LANES, 1),
                  _row_spec(tm, D, 1), _whole_spec((1, D), 1),
                  _row_spec(tm, p.shape[1], 1), _whole_spec((1, D), 1),
                  _whole_spec(wpg.shape, 1), _whole_spec(wpp.shape, 1), _whole_spec((1, D), 1)],
        out_specs=_row_spec(tm, D, 1),
        out_shape=jax.ShapeDtypeStruct((T, D), F32),
        compiler_params=_cparams(("parallel",)),
        name="combine_ple",
    )(y0.reshape(T * n, LANES), y1.reshape(T * n, LANES), route, x, vec(g), p, vec(g_in), wpg, wpp,
      vec(g_ple))


def _moe_plan(route, tm=MOE_TM):
    T = route.shape[0]
    e = route[:, :TOP_K].astype(jnp.int32).reshape(-1)
    onehot = (e[:, None] == jnp.arange(N_EXPERTS, dtype=jnp.int32)[None, :]).astype(jnp.int32)
    csum = jnp.cumsum(onehot, axis=0)
    rank = jnp.sum(csum * onehot, axis=1) - 1
    counts = csum[-1]
    padded = ((counts + tm - 1) // tm) * tm
    ends = jnp.cumsum(padded)
    starts = ends - padded
    dest = jnp.sum(starts[None, :] * onehot, axis=1) + rank
    nt = (T * TOP_K) // tm + N_EXPERTS
    tile_start = jnp.arange(nt, dtype=jnp.int32) * tm
    tile_expert = jnp.minimum(jnp.sum((tile_start[:, None] >= ends[None, :]).astype(jnp.int32), axis=1),
                              N_EXPERTS - 1).astype(jnp.int32)
    n_active = (ends[-1] // tm).astype(jnp.int32).reshape(1)
    return tile_expert, n_active, dest.reshape(T, TOP_K).astype(jnp.int32)


def kernel(x, p, hgrn_lb_raw, e_norm_mix_pre, e_w_in, e_conv_w, e_conv_b, e_A_log, e_dt_bias, e_D, e_a_norm, e_b_norm, e_w_out, e_norm_mix_post, e_norm_ffn_pre, e_w_ffn_gate, e_w_ffn_up, e_w_ffn_down, e_norm_ffn_post, o_norm_mix_pre, o_w_qkv, o_rpb, o_w_out, o_norm_mix_post, o_norm_ffn_pre, o_w_router, o_w_exp_gate, o_w_exp_up, o_w_exp_down, o_norm_ffn_post, ple_norm_in, ple_w_gate, ple_w_proj, ple_norm_post):
    batch, seq, d_model = x.shape
    depth = p.shape[0]
    T = batch * seq
    xt = x.reshape(T, d_model)
    lb_all = jnp.cumsum(jax.nn.softmax(hgrn_lb_raw.astype(F32), axis=0), axis=0)

    a_kdim = A_HEADS * A_DK
    b_width = B_HEADS * B_HEADDIM
    conv_dim = b_width + 2 * B_GROUPS * B_STATE
    main_w = 5 * a_kdim + b_width + conv_dim

    for li in range(depth):
        j = li // 2
        p_li = p[li].reshape(T, -1)
        ple = (ple_norm_in[li], ple_w_gate[li].astype(BF16), ple_w_proj[li].astype(BF16), ple_norm_post[li])
        if li % 2 == 0:
            w_in = e_w_in[j]
            dtf = w_in[:, main_w:main_w + B_HEADS].reshape(d_model, B_GROUPS, B_HPG)
            dtb = w_in[:, main_w + B_HEADS:].reshape(d_model, B_GROUPS, B_HPG)
            w_dt = jnp.pad(jnp.concatenate([dtf, dtb], axis=2), ((0, 0), (0, 0), (0, LANES - 2 * B_HPG)))
            w_all = jnp.concatenate([w_in[:, :main_w], w_dt.reshape(d_model, B_GROUPS * LANES)], axis=1)
            u, dt = _norm_proj(xt, e_norm_mix_pre[j], w_all.astype(BF16),
                               (main_w, B_GROUPS * LANES), (BF16, F32))
            o_a = _hgrn_mixer(u, lb_all[li], e_a_norm[j], batch, seq)
            o_b = _ssd_mixer(u, dt, e_conv_w[j], e_conv_b[j], e_A_log[j], e_dt_bias[j], e_D[j],
                             e_b_norm[j], batch, seq, 5 * a_kdim)
            w_out = e_w_out[j].astype(BF16)
            xt = _mix_ffn_ple(o_a, o_b, w_out[:a_kdim], w_out[a_kdim:], e_norm_mix_post[j], xt,
                              e_norm_ffn_pre[j], e_w_ffn_gate[j].astype(BF16), e_w_ffn_up[j].astype(BF16),
                              e_w_ffn_down[j].astype(BF16), e_norm_ffn_post[j], p_li, *ple)
        else:
            (qkv,) = _norm_proj(xt, o_norm_mix_pre[j], o_w_qkv[j].astype(BF16),
                                (3 * C_HEADS * C_HEADDIM,), (BF16,))
            o_c = _neighborhood_attention(qkv, o_rpb[j], batch, seq)
            xt, route, h = _proj_router(o_c, o_w_out[j].astype(BF16), o_norm_mix_post[j], xt,
                                        o_norm_ffn_pre[j], o_w_router[j])
            tile_expert, n_active, dest = _moe_plan(route)
            hs = _sc_scatter_tokens(h, dest, tile_expert.shape[0] * MOE_TM)
            ys = _moe_ffn(hs, o_w_exp_gate[j].astype(BF16), o_w_exp_up[j].astype(BF16),
                          o_w_exp_down[j].astype(BF16), tile_expert, n_active)
            y0, y1 = _sc_gather_tokens(ys, dest)
            xt = _combine_ple(y0, y1, route, xt, o_norm_ffn_post[j], p_li, *ple)
    return xt.reshape(batch, seq, d_model)
```

```python
import functools
import math

import numpy as np
import jax
import jax.numpy as jnp
from jax import lax
from jax.experimental import pallas as pl
from jax.experimental.pallas import tpu as pltpu
from jax.experimental.pallas import tpu_sc as plsc

F32 = jnp.float32
BF16 = jnp.bfloat16
EPS = 1e-6

LANES = 128
SUBLANES = 8
VMEM_LIMIT_BYTES = 56 * 1024 * 1024

GRID_W = 64
A_HEADS, A_DK, A_CHUNK = 4, 128, 64
A_UNROLL = 4
B_HEADS, B_HEADDIM, B_GROUPS, B_STATE, B_CONV, B_CHUNK = 8, 64, 2, 128, 5, 128
B_HPG = B_HEADS // B_GROUPS
B_GW = B_HPG * B_HEADDIM
C_HEADS, C_HEADDIM = 16, 64
NA_ROWS, NA_COLS = 8, 16
NA_HG = 4
NA_UNROLL = 8
N_EXPERTS, TOP_K = 8, 2
MOE_TM = 256


def _cparams(sem):
    return pltpu.CompilerParams(dimension_semantics=sem, vmem_limit_bytes=VMEM_LIMIT_BYTES)


def _rms(x, g):
    return x * lax.rsqrt(jnp.mean(x * x, axis=-1, keepdims=True) + EPS) * g


def _silu(x):
    return x * jax.nn.sigmoid(x)


def _dot(a, b):
    return jnp.dot(a, b, preferred_element_type=F32)


def _dot_nt(a, b):
    return lax.dot_general(a, b, (((1,), (1,)), ((), ())), preferred_element_type=F32)


def _dot_tn(a, b):
    return lax.dot_general(a, b, (((0,), (0,)), ((), ())), preferred_element_type=F32)


def _norm_proj_kernel(x_ref, g_ref, w_ref, *o_refs, col_chunk):
    h = _rms(x_ref[...], g_ref[...]).astype(BF16)
    off = 0
    for o_ref in o_refs:
        n = o_ref.shape[1]
        for c0 in range(0, n, col_chunk):
            c1 = min(c0 + col_chunk, n)
            o_ref[:, c0:c1] = _dot(h, w_ref[:, off + c0:off + c1]).astype(o_ref.dtype)
        off += n


def _norm_proj(x, g, w, widths, dtypes, tm=512, col_chunk=1024):
    T, D = x.shape
    N = w.shape[1]
    assert sum(widths) == N and T % tm == 0
    return pl.pallas_call(
        functools.partial(_norm_proj_kernel, col_chunk=col_chunk),
        grid=(T // tm,),
        in_specs=[pl.BlockSpec((tm, D), lambda i: (i, 0)),
                  pl.BlockSpec((1, D), lambda i: (0, 0)),
                  pl.BlockSpec((D, N), lambda i: (0, 0))],
        out_specs=[pl.BlockSpec((tm, n), lambda i: (i, 0)) for n in widths],
        out_shape=[jax.ShapeDtypeStruct((T, n), dt) for n, dt in zip(widths, dtypes)],
        compiler_params=_cparams(("parallel",)),
        name="norm_proj",
    )(x, g.reshape(1, D), w)


def _ple_update(x, p_ref, gin_ref, wg_ref, wp_ref, gpost_ref):
    h = _rms(x, gin_ref[...]).astype(BF16)
    gate = jax.nn.sigmoid(_dot(h, wg_ref[...]))
    proj = _dot(p_ref[...].astype(BF16), wp_ref[...])
    return x + _rms(gate * proj, gpost_ref[...])


def _row_spec(tm, width, n_grid):
    return pl.BlockSpec((tm, width), (lambda i: (i, 0)) if n_grid == 1 else (lambda i, j: (i, 0)))


def _whole_spec(shape, n_grid):
    zeros = (0,) * len(shape)
    return pl.BlockSpec(shape, (lambda i: zeros) if n_grid == 1 else (lambda i, j: zeros))


def _mix_ffn_ple_kernel(oa_ref, ob_ref, wa_ref, wb_ref, gmix_ref, x_ref, gpre_ref, wg_ref, wu_ref, wd_ref,
                        gpost_ref, p_ref, gin_ref, wpg_ref, wpp_ref, gple_ref, o_ref,
                        x1_scr, h_scr, acc_scr):
    j = pl.program_id(1)

    @pl.when(j == 0)
    def _():
        mix = _dot(oa_ref[...], wa_ref[...]) + _dot(ob_ref[...], wb_ref[...])
        x1 = x_ref[...] + _rms(mix, gmix_ref[...])
        x1_scr[...] = x1
        h_scr[...] = _rms(x1, gpre_ref[...]).astype(BF16)
        acc_scr[...] = jnp.zeros_like(acc_scr)

    h = h_scr[...]
    a = (_silu(_dot(h, wg_ref[...])) * _dot(h, wu_ref[...])).astype(BF16)
    acc_scr[...] += _dot(a, wd_ref[...])

    @pl.when(j == pl.num_programs(1) - 1)
    def _():
        x2 = x1_scr[...] + _rms(acc_scr[...], gpost_ref[...])
        o_ref[...] = _ple_update(x2, p_ref, gin_ref, wpg_ref, wpp_ref, gple_ref)


def _mix_ffn_ple(o_a, o_b, w_a, w_b, g_mix, x, g_pre, wg, wu, wd, g_post, p, g_in, wpg, wpp, g_ple,
                 tm=512, tf=1408):
    T, D = x.shape
    F = wg.shape[1]
    assert F % tf == 0 and T % tm == 0
    vec = lambda g: g.reshape(1, D)
    return pl.pallas_call(
        _mix_ffn_ple_kernel,
        grid=(T // tm, F // tf),
        in_specs=[_row_spec(tm, o_a.shape[1], 2), _row_spec(tm, o_b.shape[1], 2),
                  _whole_spec(w_a.shape, 2), _whole_spec(w_b.shape, 2), _whole_spec((1, D), 2),
                  _row_spec(tm, D, 2), _whole_spec((1, D), 2),
                  pl.BlockSpec((D, tf), lambda i, j: (0, j)),
                  pl.BlockSpec((D, tf), lambda i, j: (0, j)),
                  pl.BlockSpec((tf, D), lambda i, j: (j, 0)),
                  _whole_spec((1, D), 2),
                  _row_spec(tm, p.shape[1], 2), _whole_spec((1, D), 2),
                  _whole_spec(wpg.shape, 2), _whole_spec(wpp.shape, 2), _whole_spec((1, D), 2)],
        out_specs=_row_spec(tm, D, 2),
        out_shape=jax.ShapeDtypeStruct((T, D), F32),
        scratch_shapes=[pltpu.VMEM((tm, D), F32), pltpu.VMEM((tm, D), BF16), pltpu.VMEM((tm, D), F32)],
        compiler_params=_cparams(("parallel", "arbitrary")),
        name="mix_ffn_ple",
    )(o_a, o_b, w_a, w_b, vec(g_mix), x, vec(g_pre), wg, wu, wd, vec(g_post), p, vec(g_in), wpg, wpp,
      vec(g_ple))


def _roll_rows(x, s, rev):
    n = x.shape[0]
    return pltpu.roll(x, (n - s) if rev else s, 0)


def _cumsum_rows(x, tau, rev):
    n = x.shape[0]
    s = 1
    while s < n:
        if s % SUBLANES:
            shifted = jnp.where(tau >= s, _roll_rows(x, s, rev), 0.0)
        else:
            zeros = jnp.zeros((s,) + x.shape[1:], x.dtype)
            shifted = (jnp.concatenate([x[s:], zeros], axis=0) if rev
                       else jnp.concatenate([zeros, x[:n - s]], axis=0))
        x = x + shifted
        s *= 2
    return x


def _hgrn_kernel(q_ref, ff_ref, fb_ref, i_ref, g_ref, lb_ref, ng_ref, o_ref, acc_scr, *st_scrs):
    S, DK = q_ref.shape
    C = A_CHUNK
    nc = S // C
    assert nc % (2 * A_UNROLL) == 0
    lb = lb_ref[0]
    ng = ng_ref[...]
    row = lax.broadcasted_iota(jnp.int32, (C, DK), 0)
    t2 = lax.broadcasted_iota(jnp.int32, (C, C), 0)
    s2 = lax.broadcasted_iota(jnp.int32, (C, C), 1)
    levels = [C >> (i + 1) for i in range(C.bit_length() - 1)]

    def make_chunk(rev, f_ref):
        tau = (C - 1 - row) if rev else row
        upper_half = {L: (tau & (2 * L - 1)) >= L for L in levels}
        tau2 = (C - 1 - t2) if rev else t2
        sig2 = (C - 1 - s2) if rev else s2
        pair = {L: ((t2 & -(2 * L)) == (s2 & -(2 * L)))
                   & ((tau2 & (2 * L - 1)) >= L) & ((sig2 & (2 * L - 1)) < L) for L in levels}
        last = 0 if rev else C - 1

        def ref_rows(b, L):
            off = L if rev else L - 1
            if 2 * L >= SUBLANES:
                pieces = [jnp.broadcast_to(b[j * 2 * L + off:j * 2 * L + off + 1, :], (2 * L, DK))
                          for j in range(C // (2 * L))]
                return pieces[0] if len(pieces) == 1 else jnp.concatenate(pieces, axis=0)
            if L == 1:
                return jnp.where(upper_half[1], _roll_rows(b, 1, rev), b)
            b3 = b.reshape(C // SUBLANES, SUBLANES, DK)
            sub = lax.broadcasted_iota(jnp.int32, b3.shape, 1)
            out = None
            for j in range(SUBLANES // (2 * L)):
                piece = jnp.broadcast_to(b3[:, j * 2 * L + off:j * 2 * L + off + 1, :], b3.shape)
                out = piece if out is None else jnp.where(sub >= j * 2 * L, piece, out)
            return out.reshape(C, DK)

        def chunk(rows, st):
            q = _silu(q_ref[rows, :].astype(F32))
            f = lb + (1.0 - lb) * jax.nn.sigmoid(f_ref[rows, :].astype(F32))
            k = 1.0 - f
            v = i_ref[rows, :].astype(F32)
            vb = v.astype(BF16)
            b = _cumsum_rows(jnp.log(f), tau, rev)

            o = _dot_nt((q * jnp.exp(b)).astype(BF16), st.astype(BF16))

            attn = jnp.where(t2 == s2, _dot_nt(q.astype(BF16), k.astype(BF16)), 0.0)
            for L in levels:
                scaled = (jnp.where(upper_half[L], q, k) * jnp.exp(-jnp.abs(b - ref_rows(b, L)))).astype(BF16)
                attn = jnp.where(pair[L], _dot_nt(scaled, scaled), attn)
            o = o + _dot(attn.astype(BF16), vb)

            b_last = b[last:last + 1, :]
            khat = (k * jnp.exp(b_last - b)).astype(BF16)
            return o, st * jnp.exp(b_last) + _dot_tn(vb, khat)

        return chunk

    chunk_fns = (make_chunk(False, ff_ref), make_chunk(True, fb_ref))
    for st_scr in st_scrs:
        st_scr[...] = jnp.zeros_like(st_scr)

    def make_body(final):
        def body(it, carry):
            for rev, chunk, st_scr in zip((False, True), chunk_fns, st_scrs):
                st = st_scr[...]
                for u in range(A_UNROLL):
                    ci = it * A_UNROLL + u
                    c = (nc - 1 - ci) if rev else ci
                    rows = pl.ds(pl.multiple_of(c * C, C), C)
                    o, st = chunk(rows, st)
                    if final:
                        tot = acc_scr[rows, :] + o
                        gate = _silu(g_ref[rows, :].astype(F32))
                        o_ref[rows, :] = (_rms(tot, ng) * gate).astype(o_ref.dtype)
                    else:
                        acc_scr[rows, :] = o
                st_scr[...] = st
            return carry

        return body

    trips = nc // A_UNROLL
    lax.fori_loop(0, trips // 2, make_body(False), 0)
    lax.fori_loop(trips // 2, trips, make_body(True), 0)


def _hgrn_mixer(u, lb, norm_g, batch, seq):
    T = u.shape[0]
    H, DK = A_HEADS, A_DK

    def spec(off):
        return pl.BlockSpec((seq, DK), lambda b, h, off=off: (b, off + h))

    return pl.pallas_call(
        _hgrn_kernel,
        grid=(batch, H),
        in_specs=[spec(0), spec(H), spec(2 * H), spec(3 * H), spec(4 * H),
                  pl.BlockSpec((1, 1, DK), lambda b, h: (h, 0, 0)),
                  pl.BlockSpec((1, DK), lambda b, h: (0, 0))],
        out_specs=pl.BlockSpec((seq, DK), lambda b, h: (b, h)),
        out_shape=jax.ShapeDtypeStruct((T, H * DK), BF16),
        scratch_shapes=[pltpu.VMEM((seq, DK), F32), pltpu.VMEM((DK, DK), F32),
                        pltpu.VMEM((DK, DK), F32)],
        compiler_params=_cparams(("parallel", "parallel")),
        name="hgrn2",
    )(u, u, u, u, u, lb.reshape(H, 1, DK), norm_g.reshape(1, DK))


def _expand_heads(col, j0, width):
    q = col.shape[0]
    lane = lax.broadcasted_iota(jnp.int32, (q, width), 1)
    out = jnp.broadcast_to(col[:, j0 + B_HPG - 1:j0 + B_HPG], (q, width))
    for j in range(B_HPG - 2, -1, -1):
        out = jnp.where(lane < (j + 1) * B_HEADDIM,
                        jnp.broadcast_to(col[:, j0 + j:j0 + j + 1], (q, width)), out)
    return out


def _ssd_kernel(z_ref, x_ref, bm_ref, cm_ref, dt_ref, cwx_ref, cwb_ref, cwc_ref,
                cbx_ref, cbb_ref, cbc_ref, hp_ref, dsk_ref, ng_ref, o_ref,
                xs_scr, b_scr, c_scr, y_scr, *st_scrs):
    S = x_ref.shape[0]
    Q = B_CHUNK
    nc = S // Q
    GW = B_GW
    hp = hp_ref[0]
    a_row, dtb_row = hp[0:1, :], hp[1:2, :]
    row = lax.broadcasted_iota(jnp.int32, (Q, LANES), 0)
    t2 = lax.broadcasted_iota(jnp.int32, (Q, Q), 0)
    s2 = lax.broadcasted_iota(jnp.int32, (Q, Q), 1)
    lane_gw = lax.broadcasted_iota(jnp.int32, (Q, GW), 1)
    halo = SUBLANES

    def conv_body(c, carry):
        r0 = pl.multiple_of(c * Q, Q)
        rp = pl.multiple_of(jnp.maximum(r0 - halo, 0), halo)
        rn = pl.multiple_of(jnp.minimum(r0 + Q, S - halo), halo)
        has_prev = c > 0
        has_next = c < nc - 1

        def conv(src_ref, w_ref, bias_ref):
            prev = jnp.where(has_prev, src_ref[pl.ds(rp, halo), :].astype(F32), 0.0)
            nxt = jnp.where(has_next, src_ref[pl.ds(rn, halo), :].astype(F32), 0.0)
            xx = jnp.concatenate([prev, src_ref[pl.ds(r0, Q), :].astype(F32), nxt], axis=0)
            n = Q + 2 * halo
            w = w_ref[0]
            acc = None
            for j in range(B_CONV):
                delta = j - B_CONV // 2
                sh = xx if delta == 0 else pltpu.roll(xx, (-delta) % n, 0)
                term = sh[halo:halo + Q, :] * w[j:j + 1, :]
                acc = term if acc is None else acc + term
            return _silu(acc + bias_ref[0])

        xs_scr[pl.ds(r0, Q), :] = conv(x_ref, cwx_ref, cbx_ref)
        b_scr[pl.ds(r0, Q), :] = conv(bm_ref, cwb_ref, cbb_ref).astype(BF16)
        c_scr[pl.ds(r0, Q), :] = conv(cm_ref, cwc_ref, cbc_ref).astype(BF16)
        return carry

    lax.fori_loop(0, nc, conv_body, 0)

    def make_chunk(rev):
        tau = (Q - 1 - row) if rev else row
        last = 0 if rev else Q - 1
        j0 = B_HPG if rev else 0
        pair_ok = (s2 >= t2) if rev else (t2 >= s2)

        def chunk(rows, st):
            dt = jax.nn.softplus(dt_ref[rows, :] + dtb_row)
            cs = _cumsum_rows(dt * a_row, tau, rev)
            cs_t = cs.T
            xs = xs_scr[rows, :]
            bm = b_scr[rows, :]
            cm = c_scr[rows, :]
            xdt = xs * _expand_heads(dt, j0, GW)
            ecs = _expand_heads(cs, j0, GW)
            ecs_last = ecs[last:last + 1, :]

            y = _dot(cm, st.astype(BF16)) * jnp.exp(ecs)

            scores = _dot_nt(cm, bm)
            for j in range(B_HPG):
                col = jnp.broadcast_to(cs[:, j0 + j:j0 + j + 1], (Q, Q))
                rw = jnp.broadcast_to(cs_t[j0 + j:j0 + j + 1, :], (Q, Q))
                decay = jnp.where(pair_ok, jnp.exp(jnp.minimum(col - rw, 0.0)), 0.0)
                in_head = (lane_gw >= j * B_HEADDIM) & (lane_gw < (j + 1) * B_HEADDIM)
                xh = jnp.where(in_head, xdt, 0.0).astype(BF16)
                y = y + _dot((scores * decay).astype(BF16), xh)

            xdec = (xdt * jnp.exp(ecs_last - ecs)).astype(BF16)
            return y, st * jnp.exp(ecs_last) + _dot_tn(bm, xdec)

        return chunk

    chunk_fns = (make_chunk(False), make_chunk(True))
    for st_scr in st_scrs:
        st_scr[...] = jnp.zeros_like(st_scr)

    def make_body(final):
        def body(ci, carry):
            for rev, chunk, st_scr in zip((False, True), chunk_fns, st_scrs):
                c = (nc - 1 - ci) if rev else ci
                rows = pl.ds(pl.multiple_of(c * Q, Q), Q)
                y, st = chunk(rows, st_scr[...])
                st_scr[...] = st
                if final:
                    tot = y_scr[rows, :] + y + dsk_ref[0] * xs_scr[rows, :]
                    tot = tot * _silu(z_ref[rows, :].astype(F32))
                    o_ref[rows, :] = _rms(tot, ng_ref[0]).astype(o_ref.dtype)
                else:
                    y_scr[rows, :] = y
            return carry

        return body

    assert nc % 2 == 0
    lax.fori_loop(0, nc // 2, make_body(False), 0)
    lax.fori_loop(nc // 2, nc, make_body(True), 0)


def _ssd_mixer(u, dt, conv_w, conv_b, a_log, dt_bias, d_skip, norm_g, batch, seq, col0):
    T = u.shape[0]
    G, GW, N = B_GROUPS, B_GW, B_STATE
    W = B_HEADS * B_HEADDIM
    assert col0 % GW == 0 and GW == 2 * N
    z_blk = col0 // GW
    x_blk = z_blk + W // GW
    b_blk = (col0 + 2 * W) // N
    c_blk = b_blk + G

    def pad_rows(w):
        return jnp.pad(w, ((0, SUBLANES - w.shape[0]), (0, 0)))

    cwx = pad_rows(conv_w[:, :W]).reshape(SUBLANES, G, GW).transpose(1, 0, 2)
    cwb = pad_rows(conv_w[:, W:W + G * N]).reshape(SUBLANES, G, N).transpose(1, 0, 2)
    cwc = pad_rows(conv_w[:, W + G * N:]).reshape(SUBLANES, G, N).transpose(1, 0, 2)
    cbx = conv_b[:W].reshape(G, 1, GW)
    cbb = conv_b[W:W + G * N].reshape(G, 1, N)
    cbc = conv_b[W + G * N:].reshape(G, 1, N)
    a_neg = -jnp.exp(a_log.astype(F32))
    per_dir = lambda v: jnp.concatenate([v[0].reshape(G, B_HPG), v[1].reshape(G, B_HPG)], axis=1)
    hp = jnp.stack([per_dir(a_neg), per_dir(dt_bias.astype(F32))], axis=1)
    hp = jnp.pad(hp, ((0, 0), (0, SUBLANES - 2), (0, LANES - 2 * B_HPG)))
    dsk = jnp.repeat(d_skip.astype(F32), B_HEADDIM).reshape(G, 1, GW)
    ng = norm_g.reshape(G, 1, GW)

    gspec = lambda shape: pl.BlockSpec((1,) + shape, lambda b, g: (g, 0, 0))
    return pl.pallas_call(
        _ssd_kernel,
        grid=(batch, G),
        in_specs=[pl.BlockSpec((seq, GW), lambda b, g: (b, z_blk + g)),
                  pl.BlockSpec((seq, GW), lambda b, g: (b, x_blk + g)),
                  pl.BlockSpec((seq, N), lambda b, g: (b, b_blk + g)),
                  pl.BlockSpec((seq, N), lambda b, g: (b, c_blk + g)),
                  pl.BlockSpec((seq, LANES), lambda b, g: (b, g)),
                  gspec((SUBLANES, GW)), gspec((SUBLANES, N)), gspec((SUBLANES, N)),
                  gspec((1, GW)), gspec((1, N)), gspec((1, N)),
                  gspec((SUBLANES, LANES)), gspec((1, GW)), gspec((1, GW))],
        out_specs=pl.BlockSpec((seq, GW), lambda b, g: (b, g)),
        out_shape=jax.ShapeDtypeStruct((T, W), BF16),
        scratch_shapes=[pltpu.VMEM((seq, GW), F32), pltpu.VMEM((seq, N), BF16),
                        pltpu.VMEM((seq, N), BF16), pltpu.VMEM((seq, GW), F32),
                        pltpu.VMEM((N, GW), F32), pltpu.VMEM((N, GW), F32)],
        compiler_params=_cparams(("parallel", "parallel")),
        name="ssd",
    )(u, u, u, u, dt, cwx, cwb, cwc, cbx, cbb, cbc, hp, dsk, ng)


def _na_kernel(q_ref, k_ref, v_ref, bias_ref, o_ref):
    S, HW = q_ref.shape
    W = GRID_W
    n_rows = S // W
    kh = min(NA_ROWS, n_rows)
    nk = kh * W
    n_pairs = bias_ref.shape[1] // NA_HG
    lane = lax.broadcasted_iota(jnp.int32, (W, HW), 1)
    head_masks = [(lane >= h * C_HEADDIM) & (lane < (h + 1) * C_HEADDIM) for h in range(NA_HG)]
    scale = C_HEADDIM ** -0.5

    def body(r, carry):
        rs = jnp.clip(r - kh // 2, 0, n_rows - kh)
        q = q_ref[pl.ds(pl.multiple_of(r * W, W), W), :] * jnp.asarray(scale, q_ref.dtype)
        kw = k_ref[pl.ds(pl.multiple_of(rs * W, W), nk), :]
        vw = v_ref[pl.ds(pl.multiple_of(rs * W, W), nk), :]
        zero = jnp.zeros_like(q)
        qs = jnp.concatenate([jnp.where(m, q, zero) for m in head_masks], axis=0)
        d0 = (NA_ROWS - 1) - (r - rs)
        bias = jnp.concatenate(
            [jnp.concatenate([bias_ref[0, h * n_pairs + d0 + 2 * m] for m in range(kh // 2)], axis=1)
             for h in range(NA_HG)], axis=0)
        s = _dot_nt(qs, kw) + bias
        m = jnp.max(s, axis=-1, keepdims=True)
        p = jnp.exp(s - m)
        l = jnp.sum(p, axis=-1, keepdims=True)
        res = _dot(p.astype(BF16), vw) / l
        out = jnp.zeros((W, HW), F32)
        for h in range(NA_HG):
            out = jnp.where(head_masks[h], res[h * W:(h + 1) * W, :], out)
        o_ref[pl.ds(pl.multiple_of(r * W, W), W), :] = out.astype(o_ref.dtype)
        return carry

    lax.fori_loop(0, n_rows, body, 0, unroll=NA_UNROLL)


def _na_bias_table(rpb):
    H, n_dr, n_dc = rpb.shape
    W = GRID_W
    c = np.arange(W)
    qs = np.clip(c - NA_COLS // 2, 0, W - NA_COLS)
    valid = (c[None, :] >= qs[:, None]) & (c[None, :] < qs[:, None] + NA_COLS)
    dc = np.clip(c[None, :] - c[:, None] + (NA_COLS - 1), 0, n_dc - 1)
    onehot = (dc[None] == np.arange(n_dc)[:, None, None]).astype(np.float32)
    tab = jnp.einsum("hrd,dcx->hrcx", rpb.astype(F32), jnp.asarray(onehot),
                     precision=lax.Precision.HIGHEST)
    tab = jnp.where(jnp.asarray(valid)[None, None], tab, -jnp.inf)
    pairs = jnp.concatenate([tab[:, :-1], tab[:, 1:]], axis=-1)
    return pairs.reshape(H // NA_HG, NA_HG * (n_dr - 1), W, 2 * W)


def _neighborhood_attention(qkv, rpb, batch, seq):
    T = qkv.shape[0]
    HW = NA_HG * C_HEADDIM
    n_hg = C_HEADS // NA_HG
    n_rows = seq // GRID_W
    assert n_rows >= NA_ROWS and NA_ROWS % 2 == 0
    table = _na_bias_table(rpb)
    return pl.pallas_call(
        _na_kernel,
        grid=(batch, n_hg),
        in_specs=[pl.BlockSpec((seq, HW), lambda b, g: (b, g)),
                  pl.BlockSpec((seq, HW), lambda b, g: (b, n_hg + g)),
                  pl.BlockSpec((seq, HW), lambda b, g: (b, 2 * n_hg + g)),
                  pl.BlockSpec((1,) + table.shape[1:], lambda b, g: (g, 0, 0, 0))],
        out_specs=pl.BlockSpec((seq, HW), lambda b, g: (b, g)),
        out_shape=jax.ShapeDtypeStruct((T, C_HEADS * C_HEADDIM), BF16),
        compiler_params=_cparams(("parallel", "parallel")),
        name="natten",
    )(qkv, qkv, qkv, table)


def _store_token_tiles(ref, val):
    tm, width = val.shape
    n = width // LANES
    for j in range(n):
        ref[pl.ds(j, tm, stride=n), :] = val[:, j * LANES:(j + 1) * LANES]


def _load_token_tiles(ref, n):
    tm = ref.shape[0] // n
    return jnp.concatenate([ref[pl.ds(j, tm, stride=n), :] for j in range(n)], axis=1)


def _proj_router_kernel(a_ref, w_ref, gmix_ref, x_ref, g_ref, wr_ref, x1_ref, o_ref, h_ref):
    x1 = x_ref[...] + _rms(_dot(a_ref[...], w_ref[...]), gmix_ref[...])
    x1_ref[...] = x1
    h = _rms(x1, g_ref[...])
    _store_token_tiles(h_ref, h)
    h_hi = h.astype(BF16)
    h_lo = (h - h_hi.astype(F32)).astype(BF16)
    both = _dot(h_hi, wr_ref[...])
    logits = both[:, :LANES] + both[:, LANES:] + _dot(h_lo, wr_ref[:, :LANES])
    lane = lax.broadcasted_iota(jnp.int32, logits.shape, 1)
    neg = -jnp.inf
    l1 = jnp.where(lane < N_EXPERTS, logits, neg)
    m1 = jnp.max(l1, axis=-1, keepdims=True)
    i1 = jnp.min(jnp.where(l1 == m1, lane, LANES), axis=-1, keepdims=True)
    l2 = jnp.where(lane == i1, neg, l1)
    m2 = jnp.max(l2, axis=-1, keepdims=True)
    i2 = jnp.min(jnp.where(l2 == m2, lane, LANES), axis=-1, keepdims=True)
    e = jnp.exp(m2 - m1)
    w1 = 1.0 / (1.0 + e)
    w2 = e / (1.0 + e)
    out = jnp.where(lane == 0, i1.astype(F32),
                    jnp.where(lane == 1, i2.astype(F32),
                              jnp.where(lane == 2, w1, jnp.where(lane == 3, w2, 0.0))))
    o_ref[...] = out


def _proj_router(a, w, g_mix, x, g, w_router, tm=512):
    T, D = x.shape
    n = D // LANES
    wr = jnp.pad(w_router.astype(F32), ((0, 0), (0, LANES - w_router.shape[1])))
    wr_hi = wr.astype(BF16)
    wr = jnp.concatenate([wr_hi, (wr - wr_hi.astype(F32)).astype(BF16)], axis=1)
    x1, route, h = pl.pallas_call(
        _proj_router_kernel,
        grid=(T // tm,),
        in_specs=[_row_spec(tm, a.shape[1], 1), _whole_spec(w.shape, 1), _whole_spec((1, D), 1),
                  _row_spec(tm, D, 1), _whole_spec((1, D), 1), _whole_spec((D, 2 * LANES), 1)],
        out_specs=[_row_spec(tm, D, 1), _row_spec(tm, LANES, 1), _row_spec(tm * n, LANES, 1)],
        out_shape=[jax.ShapeDtypeStruct((T, D), F32),
                   jax.ShapeDtypeStruct((T, LANES), F32),
                   jax.ShapeDtypeStruct((T * n, LANES), F32)],
        compiler_params=_cparams(("parallel",)),
        name="proj_router",
    )(a, w, g_mix.reshape(1, D), x, g.reshape(1, D), wr)
    return x1, route, h.reshape(T, n, LANES)


SC_CORES = 2
SC_SUBCORES = 16
SC_WORKERS = SC_CORES * SC_SUBCORES
SC_CHUNK = 32


def _sc_mesh():
    return plsc.VectorSubcoreMesh(core_axis_name="c", subcore_axis_name="s",
                                  num_cores=SC_CORES, num_subcores=SC_SUBCORES)


def _sc_index_blocks(idx):
    return idx.reshape(SC_WORKERS, -1, SC_CHUNK)


def _sc_scatter_tokens(h, dest, n_rows):
    T, n, _ = h.shape
    per_worker = T // SC_WORKERS
    n_chunks = per_worker // SC_CHUNK
    assert per_worker * SC_WORKERS == T and n_chunks * SC_CHUNK == per_worker

    @functools.partial(
        pl.kernel, mesh=_sc_mesh(),
        out_type=jax.ShapeDtypeStruct((n_rows, n, LANES), h.dtype),
        scratch_types=[pltpu.VMEM((n_chunks, SC_CHUNK), jnp.int32),
                       pltpu.VMEM((n_chunks, SC_CHUNK), jnp.int32),
                       pltpu.VMEM((SC_CHUNK, n, LANES), h.dtype)],
        name="sc_scatter_tokens",
    )
    def scatter(h_hbm, d0_hbm, d1_hbm, o_hbm, i0_v, i1_v, rows_v):
        wid = lax.axis_index("s") * SC_CORES + lax.axis_index("c")
        pltpu.sync_copy(d0_hbm.at[wid], i0_v)
        pltpu.sync_copy(d1_hbm.at[wid], i1_v)

        @pl.loop(0, n_chunks)
        def _(j):
            pltpu.sync_copy(h_hbm.at[pl.ds(wid * per_worker + j * SC_CHUNK, SC_CHUNK)], rows_v)
            pltpu.sync_copy(rows_v, o_hbm.at[i0_v.at[j]])
            pltpu.sync_copy(rows_v, o_hbm.at[i1_v.at[j]])

    return scatter(h, _sc_index_blocks(dest[:, 0]), _sc_index_blocks(dest[:, 1]))


def _sc_gather_tokens(y, dest):
    T = dest.shape[0]
    _, n, _ = y.shape
    per_worker = T // SC_WORKERS
    n_chunks = per_worker // SC_CHUNK
    assert per_worker * SC_WORKERS == T and n_chunks * SC_CHUNK == per_worker
    out = jax.ShapeDtypeStruct((T, n, LANES), y.dtype)

    @functools.partial(
        pl.kernel, mesh=_sc_mesh(), out_type=(out, out),
        scratch_types=[pltpu.VMEM((n_chunks, SC_CHUNK), jnp.int32),
                       pltpu.VMEM((n_chunks, SC_CHUNK), jnp.int32),
                       pltpu.VMEM((SC_CHUNK, n, LANES), y.dtype)],
        name="sc_gather_tokens",
    )
    def gather(y_hbm, d0_hbm, d1_hbm, o0_hbm, o1_hbm, i0_v, i1_v, rows_v):
        wid = lax.axis_index("s") * SC_CORES + lax.axis_index("c")
        pltpu.sync_copy(d0_hbm.at[wid], i0_v)
        pltpu.sync_copy(d1_hbm.at[wid], i1_v)

        @pl.loop(0, n_chunks)
        def _(j):
            rows = pl.ds(wid * per_worker + j * SC_CHUNK, SC_CHUNK)
            pltpu.sync_copy(y_hbm.at[i0_v.at[j]], rows_v)
            pltpu.sync_copy(rows_v, o0_hbm.at[rows])
            pltpu.sync_copy(y_hbm.at[i1_v.at[j]], rows_v)
            pltpu.sync_copy(rows_v, o1_hbm.at[rows])

    return gather(y, _sc_index_blocks(dest[:, 0]), _sc_index_blocks(dest[:, 1]))


SC_PACK_PAIRS = 8
SC_LANES = 16


def _sc_pack_bf16_rows(w):
    R, C = w.shape
    pairs = R // 2
    per_worker = pairs // SC_WORKERS
    n_chunks = per_worker // SC_PACK_PAIRS
    assert n_chunks * SC_PACK_PAIRS * SC_WORKERS * 2 == R and C % SC_LANES == 0

    def round_bits(v):
        u = plsc.bitcast(v, jnp.int32)
        return u + 0x7FFF + (lax.shift_right_logical(u, 16) & 1)

    @functools.partial(
        pl.kernel, mesh=_sc_mesh(),
        out_type=jax.ShapeDtypeStruct((pairs, C), jnp.int32),
        scratch_types=[pltpu.VMEM((2 * SC_PACK_PAIRS, C), F32), pltpu.VMEM((SC_PACK_PAIRS, C), jnp.int32)],
        compiler_params=pltpu.CompilerParams(use_tc_tiling_on_sc=True, needs_layout_passes=False),
        name="sc_pack_bf16_rows",
    )
    def pack(w_hbm, o_hbm, in_v, out_v):
        wid = lax.axis_index("s") * SC_CORES + lax.axis_index("c")

        @pl.loop(0, n_chunks)
        def _(j):
            p0 = (wid * n_chunks + j) * SC_PACK_PAIRS
            pltpu.sync_copy(w_hbm.at[pl.ds(2 * p0, 2 * SC_PACK_PAIRS)], in_v)

            @pl.loop(0, C, step=SC_LANES)
            def _(c):
                cols = pl.ds(c, SC_LANES)
                for i in range(SC_PACK_PAIRS):
                    lo = lax.shift_right_logical(round_bits(in_v[2 * i, cols]), 16)
                    hi = round_bits(in_v[2 * i + 1, cols]) & jnp.int32(-65536)
                    out_v[i, cols] = lo | hi

            pltpu.sync_copy(out_v, o_hbm.at[pl.ds(p0, SC_PACK_PAIRS)])

    return pack(w)


def _moe_ffn_kernel(te_ref, na_ref, h_ref, wg_ref, wu_ref, wd_ref, o_ref, acc_scr, *, tf):
    i = pl.program_id(0)
    D = acc_scr.shape[1]
    F = wg_ref.shape[2]
    unpack = lambda packed: pltpu.bitcast(packed, BF16)

    @pl.when(i < na_ref[0])
    def _():
        h = _load_token_tiles(h_ref, D // LANES).astype(BF16)
        for f0 in range(0, F, tf):
            a = (_silu(_dot(h, unpack(wg_ref[0, :, f0:f0 + tf])))
                 * _dot(h, unpack(wu_ref[0, :, f0:f0 + tf]))).astype(BF16)
            part = _dot(a, unpack(wd_ref[0, f0 // 2:(f0 + tf) // 2, :]))
            if f0 == 0:
                acc_scr[...] = part
            else:
                acc_scr[...] += part
        _store_token_tiles(o_ref, acc_scr[...])

    @pl.when(i >= na_ref[0])
    def _():
        o_ref[...] = jnp.zeros_like(o_ref)


def _moe_ffn(hs, wg, wu, wd, tile_expert, n_active, tm=MOE_TM, tf=512):
    rows, n, _ = hs.shape
    E, half_d, F = wg.shape
    D = 2 * half_d
    nt = tile_expert.shape[0]
    assert rows == nt * tm and n * LANES == D and wd.shape == (E, F // 2, D)
    grid_spec = pltpu.PrefetchScalarGridSpec(
        num_scalar_prefetch=2,
        grid=(nt,),
        in_specs=[pl.BlockSpec((tm * n, LANES), lambda i, te, na: (i, 0)),
                  pl.BlockSpec((1, D // 2, F), lambda i, te, na: (te[i], 0, 0)),
                  pl.BlockSpec((1, D // 2, F), lambda i, te, na: (te[i], 0, 0)),
                  pl.BlockSpec((1, F // 2, D), lambda i, te, na: (te[i], 0, 0))],
        out_specs=pl.BlockSpec((tm * n, LANES), lambda i, te, na: (i, 0)),
        scratch_shapes=[pltpu.VMEM((tm, D), F32)],
    )
    y = pl.pallas_call(
        functools.partial(_moe_ffn_kernel, tf=tf),
        grid_spec=grid_spec,
        out_shape=jax.ShapeDtypeStruct((rows * n, LANES), F32),
        compiler_params=_cparams(("arbitrary",)),
        name="moe_ffn",
    )(tile_expert, n_active, hs.reshape(rows * n, LANES), wg, wu, wd)
    return y.reshape(rows, n, LANES)


def _combine_ple_kernel(y0_ref, y1_ref, route_ref, x_ref, g_ref, p_ref, gin_ref, wpg_ref, wpp_ref,
                        gple_ref, o_ref):
    n = x_ref.shape[1] // LANES
    route = route_ref[...]
    moe = route[:, 2:3] * _load_token_tiles(y0_ref, n) + route[:, 3:4] * _load_token_tiles(y1_ref, n)
    x2 = x_ref[...] + _rms(moe, g_ref[...])
    o_ref[...] = _ple_update(x2, p_ref, gin_ref, wpg_ref, wpp_ref, gple_ref)


def _combine_ple(y0, y1, route, x, g, p, g_in, wpg, wpp, g_ple, tm=512):
    T, D = x.shape
    n = D // LANES
    vec = lambda v: v.reshape(1, D)
    return pl.pallas_call(
        _combine_ple_kernel,
        grid=(T // tm,),
        in_specs=[_row_spec(tm * n, LANES, 1), _row_spec(tm * n, LANES, 1), _row_spec(tm, LANES, 1),
                  _row_spec(tm, D, 1), _whole_spec((1, D), 1),
                  _row_spec(tm, p.shape[1], 1), _whole_spec((1, D), 1),
                  _whole_spec(wpg.shape, 1), _whole_spec(wpp.shape, 1), _whole_spec((1, D), 1)],
        out_specs=_row_spec(tm, D, 1),
        out_shape=jax.ShapeDtypeStruct((T, D), F32),
        compiler_params=_cparams(("parallel",)),
        name="combine_ple",
    )(y0.reshape(T * n, LANES), y1.reshape(T * n, LANES), route, x, vec(g), p, vec(g_in), wpg, wpp,
      vec(g_ple))


def _moe_plan(route, tm=MOE_TM):
    T = route.shape[0]
    e = route[:, :TOP_K].astype(jnp.int32).reshape(-1)
    onehot = (e[:, None] == jnp.arange(N_EXPERTS, dtype=jnp.int32)[None, :]).astype(jnp.int32)
    csum = jnp.cumsum(onehot, axis=0)
    rank = jnp.sum(csum * onehot, axis=1) - 1
    counts = csum[-1]
    padded = ((counts + tm - 1) // tm) * tm
    ends = jnp.cumsum(padded)
    starts = ends - padded
    dest = jnp.sum(starts[None, :] * onehot, axis=1) + rank
    nt = (T * TOP_K) // tm + N_EXPERTS
    tile_start = jnp.arange(nt, dtype=jnp.int32) * tm
    tile_expert = jnp.minimum(jnp.sum((tile_start[:, None] >= ends[None, :]).astype(jnp.int32), axis=1),
                              N_EXPERTS - 1).astype(jnp.int32)
    n_active = (ends[-1] // tm).astype(jnp.int32).reshape(1)
    return tile_expert, n_active, dest.reshape(T, TOP_K).astype(jnp.int32)


def kernel(x, p, hgrn_lb_raw, e_norm_mix_pre, e_w_in, e_conv_w, e_conv_b, e_A_log, e_dt_bias, e_D, e_a_norm, e_b_norm, e_w_out, e_norm_mix_post, e_norm_ffn_pre, e_w_ffn_gate, e_w_ffn_up, e_w_ffn_down, e_norm_ffn_post, o_norm_mix_pre, o_w_qkv, o_rpb, o_w_out, o_norm_mix_post, o_norm_ffn_pre, o_w_router, o_w_exp_gate, o_w_exp_up, o_w_exp_down, o_norm_ffn_post, ple_norm_in, ple_w_gate, ple_w_proj, ple_norm_post):
    batch, seq, d_model = x.shape
    depth = p.shape[0]
    T = batch * seq
    xt = x.reshape(T, d_model)
    lb_all = jnp.cumsum(jax.nn.softmax(hgrn_lb_raw.astype(F32), axis=0), axis=0)

    a_kdim = A_HEADS * A_DK
    b_width = B_HEADS * B_HEADDIM
    conv_dim = b_width + 2 * B_GROUPS * B_STATE
    main_w = 5 * a_kdim + b_width + conv_dim

    def pack_experts(w):
        E, R, C = w.shape
        return _sc_pack_bf16_rows(w.reshape(E * R, C)).reshape(E, R // 2, C)

    for li in range(depth):
        j = li // 2
        p_li = p[li].reshape(T, -1)
        ple = (ple_norm_in[li], ple_w_gate[li].astype(BF16), ple_w_proj[li].astype(BF16), ple_norm_post[li])
        if li % 2 == 0:
            w_in = e_w_in[j]
            dtf = w_in[:, main_w:main_w + B_HEADS].reshape(d_model, B_GROUPS, B_HPG)
            dtb = w_in[:, main_w + B_HEADS:].reshape(d_model, B_GROUPS, B_HPG)
            w_dt = jnp.pad(jnp.concatenate([dtf, dtb], axis=2), ((0, 0), (0, 0), (0, LANES - 2 * B_HPG)))
            w_all = jnp.concatenate([w_in[:, :main_w], w_dt.reshape(d_model, B_GROUPS * LANES)], axis=1)
            u, dt = _norm_proj(xt, e_norm_mix_pre[j], w_all.astype(BF16),
                               (main_w, B_GROUPS * LANES), (BF16, F32))
            o_a = _hgrn_mixer(u, lb_all[li], e_a_norm[j], batch, seq)
            o_b = _ssd_mixer(u, dt, e_conv_w[j], e_conv_b[j], e_A_log[j], e_dt_bias[j], e_D[j],
                             e_b_norm[j], batch, seq, 5 * a_kdim)
            w_out = e_w_out[j].astype(BF16)
            xt = _mix_ffn_ple(o_a, o_b, w_out[:a_kdim], w_out[a_kdim:], e_norm_mix_post[j], xt,
                              e_norm_ffn_pre[j], e_w_ffn_gate[j].astype(BF16), e_w_ffn_up[j].astype(BF16),
                              e_w_ffn_down[j].astype(BF16), e_norm_ffn_post[j], p_li, *ple)
        else:
            (qkv,) = _norm_proj(xt, o_norm_mix_pre[j], o_w_qkv[j].astype(BF16),
                                (3 * C_HEADS * C_HEADDIM,), (BF16,))
            o_c = _neighborhood_attention(qkv, o_rpb[j], batch, seq)
            xt, route, h = _proj_router(o_c, o_w_out[j].astype(BF16), o_norm_mix_post[j], xt,
                                        o_norm_ffn_pre[j], o_w_router[j])
            tile_expert, n_active, dest = _moe_plan(route)
            hs = _sc_scatter_tokens(h, dest, tile_expert.shape[0] * MOE_TM)
            expert_w = [pack_experts(w[j]) for w in (o_w_exp_gate, o_w_exp_up, o_w_exp_down)]
            ys = _moe_ffn(hs, *expert_w, tile_expert, n_active)
            y0, y1 = _sc_gather_tokens(ys, dest)
            xt = _combine_ple(y0, y1, route, xt, o_norm_ffn_post[j], p_li, *ple)
    return xt.reshape(batch, seq, d_model)
```

```python
import functools
import math

import numpy as np
import jax
import jax.numpy as jnp
from jax import lax
from jax.experimental import pallas as pl
from jax.experimental.pallas import tpu as pltpu
from jax.experimental.pallas import tpu_sc as plsc

F32 = jnp.float32
BF16 = jnp.bfloat16
EPS = 1e-6

LANES = 128
SUBLANES = 8
VMEM_LIMIT_BYTES = 56 * 1024 * 1024

GRID_W = 64
A_HEADS, A_DK, A_CHUNK = 4, 128, 64
A_UNROLL = 4
B_HEADS, B_HEADDIM, B_GROUPS, B_STATE, B_CONV, B_CHUNK = 8, 64, 2, 128, 5, 128
B_HPG = B_HEADS // B_GROUPS
B_GW = B_HPG * B_HEADDIM
C_HEADS, C_HEADDIM = 16, 64
NA_ROWS, NA_COLS = 8, 16
NA_HG = 4
NA_UNROLL = 8
N_EXPERTS, TOP_K = 8, 2
MOE_TM = 512


def _cparams(sem):
    return pltpu.CompilerParams(dimension_semantics=sem, vmem_limit_bytes=VMEM_LIMIT_BYTES)


def _rms(x, g):
    return x * lax.rsqrt(jnp.mean(x * x, axis=-1, keepdims=True) + EPS) * g


def _silu(x):
    return x * jax.nn.sigmoid(x)


def _dot(a, b):
    return jnp.dot(a, b, preferred_element_type=F32)


def _dot_nt(a, b):
    return lax.dot_general(a, b, (((1,), (1,)), ((), ())), preferred_element_type=F32)


def _dot_tn(a, b):
    return lax.dot_general(a, b, (((0,), (0,)), ((), ())), preferred_element_type=F32)


def _norm_proj_kernel(x_ref, g_ref, w_ref, lb_ref, *o_refs, acts, col_chunk):
    h = _rms(x_ref[...], g_ref[...]).astype(BF16)
    off = 0
    for o_ref, act in zip(o_refs, acts):
        n = o_ref.shape[1]
        for c0 in range(0, n, col_chunk):
            c1 = min(c0 + col_chunk, n)
            y = _dot(h, w_ref[:, off + c0:off + c1])
            if act == "silu":
                y = _silu(y)
            elif act == "log_gate":
                lb = lb_ref[:, c0:c1]
                y = jnp.log(lb + (1.0 - lb) * jax.nn.sigmoid(y))
            o_ref[:, c0:c1] = y.astype(o_ref.dtype)
        off += n


def _norm_proj(x, g, w, outs, lb=None, tm=512, col_chunk=1024):
    T, D = x.shape
    N = w.shape[1]
    widths = [o[0] for o in outs]
    assert sum(widths) == N and T % tm == 0
    if lb is None:
        lb = jnp.zeros((LANES,), F32)
    return pl.pallas_call(
        functools.partial(_norm_proj_kernel, acts=tuple(o[2] for o in outs), col_chunk=col_chunk),
        grid=(T // tm,),
        in_specs=[_row_spec(tm, D, 1), _whole_spec((1, D), 1), _whole_spec((D, N), 1),
                  _whole_spec((1, lb.shape[0]), 1)],
        out_specs=[_row_spec(tm, n, 1) for n in widths],
        out_shape=[jax.ShapeDtypeStruct((T, n), dt) for n, dt, _ in outs],
        compiler_params=_cparams(("parallel",)),
        name="norm_proj",
    )(x, g.reshape(1, D), w, lb.reshape(1, -1))


def _ple_update(x, p_ref, gin_ref, wg_ref, wp_ref, gpost_ref):
    h = _rms(x, gin_ref[...]).astype(BF16)
    gate = jax.nn.sigmoid(_dot(h, wg_ref[...]))
    proj = _dot(p_ref[...].astype(BF16), wp_ref[...])
    return x + _rms(gate * proj, gpost_ref[...])


def _row_spec(tm, width, n_grid):
    return pl.BlockSpec((tm, width), (lambda i: (i, 0)) if n_grid == 1 else (lambda i, j: (i, 0)))


def _whole_spec(shape, n_grid):
    zeros = (0,) * len(shape)
    return pl.BlockSpec(shape, (lambda i: zeros) if n_grid == 1 else (lambda i, j: zeros))


def _mix_ffn_ple_kernel(oa_ref, ob_ref, wa_ref, wb_ref, gmix_ref, x_ref, gpre_ref, wg_ref, wu_ref, wd_ref,
                        gpost_ref, p_ref, gin_ref, wpg_ref, wpp_ref, gple_ref, o_ref,
                        x1_scr, h_scr, acc_scr):
    j = pl.program_id(1)

    @pl.when(j == 0)
    def _():
        mix = _dot(oa_ref[...], wa_ref[...]) + _dot(ob_ref[...], wb_ref[...])
        x1 = x_ref[...] + _rms(mix, gmix_ref[...])
        x1_scr[...] = x1
        h_scr[...] = _rms(x1, gpre_ref[...]).astype(BF16)
        acc_scr[...] = jnp.zeros_like(acc_scr)

    h = h_scr[...]
    a = (_silu(_dot(h, wg_ref[...])) * _dot(h, wu_ref[...])).astype(BF16)
    acc_scr[...] += _dot(a, wd_ref[...])

    @pl.when(j == pl.num_programs(1) - 1)
    def _():
        x2 = x1_scr[...] + _rms(acc_scr[...], gpost_ref[...])
        o_ref[...] = _ple_update(x2, p_ref, gin_ref, wpg_ref, wpp_ref, gple_ref)


def _mix_ffn_ple(o_a, o_b, w_a, w_b, g_mix, x, g_pre, wg, wu, wd, g_post, p, g_in, wpg, wpp, g_ple,
                 tm=512, tf=1408):
    T, D = x.shape
    F = wg.shape[1]
    assert F % tf == 0 and T % tm == 0
    vec = lambda g: g.reshape(1, D)
    return pl.pallas_call(
        _mix_ffn_ple_kernel,
        grid=(T // tm, F // tf),
        in_specs=[_row_spec(tm, o_a.shape[1], 2), _row_spec(tm, o_b.shape[1], 2),
                  _whole_spec(w_a.shape, 2), _whole_spec(w_b.shape, 2), _whole_spec((1, D), 2),
                  _row_spec(tm, D, 2), _whole_spec((1, D), 2),
                  pl.BlockSpec((D, tf), lambda i, j: (0, j)),
                  pl.BlockSpec((D, tf), lambda i, j: (0, j)),
                  pl.BlockSpec((tf, D), lambda i, j: (j, 0)),
                  _whole_spec((1, D), 2),
                  _row_spec(tm, p.shape[1], 2), _whole_spec((1, D), 2),
                  _whole_spec(wpg.shape, 2), _whole_spec(wpp.shape, 2), _whole_spec((1, D), 2)],
        out_specs=_row_spec(tm, D, 2),
        out_shape=jax.ShapeDtypeStruct((T, D), F32),
        scratch_shapes=[pltpu.VMEM((tm, D), F32), pltpu.VMEM((tm, D), BF16), pltpu.VMEM((tm, D), F32)],
        compiler_params=_cparams(("parallel", "arbitrary")),
        name="mix_ffn_ple",
    )(o_a, o_b, w_a, w_b, vec(g_mix), x, vec(g_pre), wg, wu, wd, vec(g_post), p, vec(g_in), wpg, wpp,
      vec(g_ple))


def _roll_rows(x, s, rev):
    n = x.shape[0]
    return pltpu.roll(x, (n - s) if rev else s, 0)


def _cumsum_rows(x, tau, rev):
    n = x.shape[0]
    s = 1
    while s < n:
        if s % SUBLANES:
            shifted = jnp.where(tau >= s, _roll_rows(x, s, rev), 0.0)
        else:
            zeros = jnp.zeros((s,) + x.shape[1:], x.dtype)
            shifted = (jnp.concatenate([x[s:], zeros], axis=0) if rev
                       else jnp.concatenate([zeros, x[:n - s]], axis=0))
        x = x + shifted
        s *= 2
    return x


def _hgrn_levels(C):
    return [C >> (i + 1) for i in range(C.bit_length() - 1)]


def _hgrn_pair_classes(C):
    t = np.arange(C)[:, None]
    s = np.arange(C)[None, :]
    out = np.full((2, C, C), -1, np.int32)
    for d, rev in enumerate((False, True)):
        tau, sig = (C - 1 - t, C - 1 - s) if rev else (t, s)
        out[d][t == s] = 0
        for i, L in enumerate(_hgrn_levels(C)):
            m = ((t & -(2 * L)) == (s & -(2 * L))) & ((tau & (2 * L - 1)) >= L) & ((sig & (2 * L - 1)) < L)
            out[d][m] = i + 1
    return out


def _hgrn_kernel(q_ref, gf_ref, gb_ref, i_ref, gate_ref, cls_ref, ng_ref, o_ref, acc_scr, *st_scrs):
    S, DK = q_ref.shape
    C = A_CHUNK
    nc = S // C
    assert nc % (2 * A_UNROLL) == 0
    ng = ng_ref[...]
    row = lax.broadcasted_iota(jnp.int32, (C, DK), 0)
    levels = _hgrn_levels(C)

    def make_chunk(rev, g_ref):
        tau = (C - 1 - row) if rev else row
        odd_rank = (tau & 1) == 1
        pair_cls = cls_ref[1 if rev else 0]
        last = 0 if rev else C - 1

        def ref_rows(b, L):
            off = L if rev else L - 1
            if 2 * L >= SUBLANES:
                pieces = [jnp.broadcast_to(b[j * 2 * L + off:j * 2 * L + off + 1, :], (2 * L, DK))
                          for j in range(C // (2 * L))]
                return pieces[0] if len(pieces) == 1 else jnp.concatenate(pieces, axis=0)
            if L == 1:
                return jnp.where(odd_rank, _roll_rows(b, 1, rev), b)
            b3 = b.reshape(C // SUBLANES, SUBLANES, DK)
            sub = lax.broadcasted_iota(jnp.int32, b3.shape, 1)
            out = None
            for j in range(SUBLANES // (2 * L)):
                piece = jnp.broadcast_to(b3[:, j * 2 * L + off:j * 2 * L + off + 1, :], b3.shape)
                out = piece if out is None else jnp.where(sub >= j * 2 * L, piece, out)
            return out.reshape(C, DK)

        def chunk(rows, st):
            q = q_ref[rows, :].astype(F32)
            g = g_ref[rows, :]
            k = 1.0 - jnp.exp(g)
            vb = i_ref[rows, :]
            b = _cumsum_rows(g, tau, rev)

            o = _dot_nt((q * jnp.exp(b)).astype(BF16), st.astype(BF16))

            attn = jnp.where(pair_cls == 0, _dot_nt(q.astype(BF16), k.astype(BF16)), 0.0)
            for i, L in enumerate(levels):
                e = jnp.exp(-jnp.abs(b - ref_rows(b, L)))
                attn = jnp.where(pair_cls == i + 1, _dot_nt((q * e).astype(BF16), (k * e).astype(BF16)), attn)
            o = o + _dot(attn.astype(BF16), vb)

            b_last = b[last:last + 1, :]
            khat = (k * jnp.exp(b_last - b)).astype(BF16)
            return o, st * jnp.exp(b_last) + _dot_tn(vb, khat)

        return chunk

    chunk_fns = (make_chunk(False, gf_ref), make_chunk(True, gb_ref))
    for st_scr in st_scrs:
        st_scr[...] = jnp.zeros_like(st_scr)

    def make_body(final):
        def body(it, carry):
            for rev, chunk, st_scr in zip((False, True), chunk_fns, st_scrs):
                st = st_scr[...]
                for u in range(A_UNROLL):
                    ci = it * A_UNROLL + u
                    c = (nc - 1 - ci) if rev else ci
                    rows = pl.ds(pl.multiple_of(c * C, C), C)
                    o, st = chunk(rows, st)
                    if final:
                        tot = acc_scr[rows, :] + o
                        o_ref[rows, :] = (_rms(tot, ng) * gate_ref[rows, :].astype(F32)).astype(o_ref.dtype)
                    else:
                        acc_scr[rows, :] = o
                st_scr[...] = st
            return carry

        return body

    trips = nc // A_UNROLL
    lax.fori_loop(0, trips // 2, make_body(False), 0)
    lax.fori_loop(trips // 2, trips, make_body(True), 0)


def _hgrn_mixer(q, g_fwd, g_bwd, v, gate, norm_g, batch, seq):
    T = q.shape[0]
    H, DK = A_HEADS, A_DK
    head = pl.BlockSpec((seq, DK), lambda b, h: (b, h))
    classes = jnp.asarray(_hgrn_pair_classes(A_CHUNK))
    return pl.pallas_call(
        _hgrn_kernel,
        grid=(batch, H),
        in_specs=[head, head, head, head, head,
                  pl.BlockSpec(classes.shape, lambda b, h: (0, 0, 0)),
                  pl.BlockSpec((1, DK), lambda b, h: (0, 0))],
        out_specs=pl.BlockSpec((seq, DK), lambda b, h: (b, h)),
        out_shape=jax.ShapeDtypeStruct((T, H * DK), BF16),
        scratch_shapes=[pltpu.VMEM((seq, DK), F32), pltpu.VMEM((DK, DK), F32),
                        pltpu.VMEM((DK, DK), F32)],
        compiler_params=_cparams(("parallel", "parallel")),
        name="hgrn2",
    )(q, g_fwd, g_bwd, v, gate, classes, norm_g.reshape(1, DK))


def _expand_heads(col, j0, width):
    q = col.shape[0]
    lane = lax.broadcasted_iota(jnp.int32, (q, width), 1)
    out = jnp.broadcast_to(col[:, j0 + B_HPG - 1:j0 + B_HPG], (q, width))
    for j in range(B_HPG - 2, -1, -1):
        out = jnp.where(lane < (j + 1) * B_HEADDIM,
                        jnp.broadcast_to(col[:, j0 + j:j0 + j + 1], (q, width)), out)
    return out


def _ssd_kernel(z_ref, x_ref, bm_ref, cm_ref, dt_ref, cwx_ref, cwb_ref, cwc_ref,
                cbx_ref, cbb_ref, cbc_ref, hp_ref, dsk_ref, ng_ref, o_ref,
                xs_scr, b_scr, c_scr, y_scr, *st_scrs):
    S = x_ref.shape[0]
    Q = B_CHUNK
    nc = S // Q
    GW = B_GW
    hp = hp_ref[0]
    a_row, dtb_row = hp[0:1, :], hp[1:2, :]
    row = lax.broadcasted_iota(jnp.int32, (Q, LANES), 0)
    t2 = lax.broadcasted_iota(jnp.int32, (Q, Q), 0)
    s2 = lax.broadcasted_iota(jnp.int32, (Q, Q), 1)
    lane_gw = lax.broadcasted_iota(jnp.int32, (Q, GW), 1)
    halo = SUBLANES

    def conv_body(c, carry):
        r0 = pl.multiple_of(c * Q, Q)
        rp = pl.multiple_of(jnp.maximum(r0 - halo, 0), halo)
        rn = pl.multiple_of(jnp.minimum(r0 + Q, S - halo), halo)
        has_prev = c > 0
        has_next = c < nc - 1

        def conv(src_ref, w_ref, bias_ref):
            prev = jnp.where(has_prev, src_ref[pl.ds(rp, halo), :].astype(F32), 0.0)
            nxt = jnp.where(has_next, src_ref[pl.ds(rn, halo), :].astype(F32), 0.0)
            xx = jnp.concatenate([prev, src_ref[pl.ds(r0, Q), :].astype(F32), nxt], axis=0)
            n = Q + 2 * halo
            w = w_ref[0]
            acc = None
            for j in range(B_CONV):
                delta = j - B_CONV // 2
                sh = xx if delta == 0 else pltpu.roll(xx, (-delta) % n, 0)
                term = sh[halo:halo + Q, :] * w[j:j + 1, :]
                acc = term if acc is None else acc + term
            return _silu(acc + bias_ref[0])

        xs_scr[pl.ds(r0, Q), :] = conv(x_ref, cwx_ref, cbx_ref)
        b_scr[pl.ds(r0, Q), :] = conv(bm_ref, cwb_ref, cbb_ref).astype(BF16)
        c_scr[pl.ds(r0, Q), :] = conv(cm_ref, cwc_ref, cbc_ref).astype(BF16)
        return carry

    lax.fori_loop(0, nc, conv_body, 0)

    def make_chunk(rev):
        tau = (Q - 1 - row) if rev else row
        last = 0 if rev else Q - 1
        j0 = B_HPG if rev else 0
        pair_ok = (s2 >= t2) if rev else (t2 >= s2)

        def chunk(rows, st):
            dt = jax.nn.softplus(dt_ref[rows, :] + dtb_row)
            cs = _cumsum_rows(dt * a_row, tau, rev)
            cs_t = cs.T
            xs = xs_scr[rows, :]
            bm = b_scr[rows, :]
            cm = c_scr[rows, :]
            xdt = xs * _expand_heads(dt, j0, GW)
            ecs = _expand_heads(cs, j0, GW)
            ecs_last = ecs[last:last + 1, :]

            y = _dot(cm, st.astype(BF16)) * jnp.exp(ecs)

            scores = _dot_nt(cm, bm)
            for j in range(B_HPG):
                col = jnp.broadcast_to(cs[:, j0 + j:j0 + j + 1], (Q, Q))
                rw = jnp.broadcast_to(cs_t[j0 + j:j0 + j + 1, :], (Q, Q))
                decay = jnp.where(pair_ok, jnp.exp(jnp.minimum(col - rw, 0.0)), 0.0)
                in_head = (lane_gw >= j * B_HEADDIM) & (lane_gw < (j + 1) * B_HEADDIM)
                xh = jnp.where(in_head, xdt, 0.0).astype(BF16)
                y = y + _dot((scores * decay).astype(BF16), xh)

            xdec = (xdt * jnp.exp(ecs_last - ecs)).astype(BF16)
            return y, st * jnp.exp(ecs_last) + _dot_tn(bm, xdec)

        return chunk

    chunk_fns = (make_chunk(False), make_chunk(True))
    for st_scr in st_scrs:
        st_scr[...] = jnp.zeros_like(st_scr)

    def make_body(final):
        def body(ci, carry):
            for rev, chunk, st_scr in zip((False, True), chunk_fns, st_scrs):
                c = (nc - 1 - ci) if rev else ci
                rows = pl.ds(pl.multiple_of(c * Q, Q), Q)
                y, st = chunk(rows, st_scr[...])
                st_scr[...] = st
                if final:
                    tot = y_scr[rows, :] + y + dsk_ref[0] * xs_scr[rows, :]
                    tot = tot * _silu(z_ref[rows, :].astype(F32))
                    o_ref[rows, :] = _rms(tot, ng_ref[0]).astype(o_ref.dtype)
                else:
                    y_scr[rows, :] = y
            return carry

        return body

    assert nc % 2 == 0
    lax.fori_loop(0, nc // 2, make_body(False), 0)
    lax.fori_loop(nc // 2, nc, make_body(True), 0)


def _ssd_mixer(u, dt, conv_w, conv_b, a_log, dt_bias, d_skip, norm_g, batch, seq, col0):
    T = u.shape[0]
    G, GW, N = B_GROUPS, B_GW, B_STATE
    W = B_HEADS * B_HEADDIM
    assert col0 % GW == 0 and GW == 2 * N
    z_blk = col0 // GW
    x_blk = z_blk + W // GW
    b_blk = (col0 + 2 * W) // N
    c_blk = b_blk + G

    def pad_rows(w):
        return jnp.pad(w, ((0, SUBLANES - w.shape[0]), (0, 0)))

    cwx = pad_rows(conv_w[:, :W]).reshape(SUBLANES, G, GW).transpose(1, 0, 2)
    cwb = pad_rows(conv_w[:, W:W + G * N]).reshape(SUBLANES, G, N).transpose(1, 0, 2)
    cwc = pad_rows(conv_w[:, W + G * N:]).reshape(SUBLANES, G, N).transpose(1, 0, 2)
    cbx = conv_b[:W].reshape(G, 1, GW)
    cbb = conv_b[W:W + G * N].reshape(G, 1, N)
    cbc = conv_b[W + G * N:].reshape(G, 1, N)
    a_neg = -jnp.exp(a_log.astype(F32))
    per_dir = lambda v: jnp.concatenate([v[0].reshape(G, B_HPG), v[1].reshape(G, B_HPG)], axis=1)
    hp = jnp.stack([per_dir(a_neg), per_dir(dt_bias.astype(F32))], axis=1)
    hp = jnp.pad(hp, ((0, 0), (0, SUBLANES - 2), (0, LANES - 2 * B_HPG)))
    dsk = jnp.repeat(d_skip.astype(F32), B_HEADDIM).reshape(G, 1, GW)
    ng = norm_g.reshape(G, 1, GW)

    gspec = lambda shape: pl.BlockSpec((1,) + shape, lambda b, g: (g, 0, 0))
    return pl.pallas_call(
        _ssd_kernel,
        grid=(batch, G),
        in_specs=[pl.BlockSpec((seq, GW), lambda b, g: (b, z_blk + g)),
                  pl.BlockSpec((seq, GW), lambda b, g: (b, x_blk + g)),
                  pl.BlockSpec((seq, N), lambda b, g: (b, b_blk + g)),
                  pl.BlockSpec((seq, N), lambda b, g: (b, c_blk + g)),
                  pl.BlockSpec((seq, LANES), lambda b, g: (b, g)),
                  gspec((SUBLANES, GW)), gspec((SUBLANES, N)), gspec((SUBLANES, N)),
                  gspec((1, GW)), gspec((1, N)), gspec((1, N)),
                  gspec((SUBLANES, LANES)), gspec((1, GW)), gspec((1, GW))],
        out_specs=pl.BlockSpec((seq, GW), lambda b, g: (b, g)),
        out_shape=jax.ShapeDtypeStruct((T, W), BF16),
        scratch_shapes=[pltpu.VMEM((seq, GW), F32), pltpu.VMEM((seq, N), BF16),
                        pltpu.VMEM((seq, N), BF16), pltpu.VMEM((seq, GW), F32),
                        pltpu.VMEM((N, GW), F32), pltpu.VMEM((N, GW), F32)],
        compiler_params=_cparams(("parallel", "parallel")),
        name="ssd",
    )(u, u, u, u, dt, cwx, cwb, cwc, cbx, cbb, cbc, hp, dsk, ng)


def _na_kernel(q_ref, k_ref, v_ref, bias_ref, o_ref):
    S, HW = q_ref.shape
    W = GRID_W
    n_rows = S // W
    kh = min(NA_ROWS, n_rows)
    nk = kh * W
    n_pairs = bias_ref.shape[1] // NA_HG
    lane = lax.broadcasted_iota(jnp.int32, (W, HW), 1)
    head_masks = [(lane >= h * C_HEADDIM) & (lane < (h + 1) * C_HEADDIM) for h in range(NA_HG)]
    scale = C_HEADDIM ** -0.5

    def body(r, carry):
        rs = jnp.clip(r - kh // 2, 0, n_rows - kh)
        q = q_ref[pl.ds(pl.multiple_of(r * W, W), W), :] * jnp.asarray(scale, q_ref.dtype)
        kw = k_ref[pl.ds(pl.multiple_of(rs * W, W), nk), :]
        vw = v_ref[pl.ds(pl.multiple_of(rs * W, W), nk), :]
        zero = jnp.zeros_like(q)
        qs = jnp.concatenate([jnp.where(m, q, zero) for m in head_masks], axis=0)
        d0 = (NA_ROWS - 1) - (r - rs)
        bias = jnp.concatenate(
            [jnp.concatenate([bias_ref[0, h * n_pairs + d0 + 2 * m] for m in range(kh // 2)], axis=1)
             for h in range(NA_HG)], axis=0)
        s = _dot_nt(qs, kw) + bias
        m = jnp.max(s, axis=-1, keepdims=True)
        p = jnp.exp(s - m)
        l = jnp.sum(p, axis=-1, keepdims=True)
        res = _dot(p.astype(BF16), vw) / l
        out = jnp.zeros((W, HW), F32)
        for h in range(NA_HG):
            out = jnp.where(head_masks[h], res[h * W:(h + 1) * W, :], out)
        o_ref[pl.ds(pl.multiple_of(r * W, W), W), :] = out.astype(o_ref.dtype)
        return carry

    lax.fori_loop(0, n_rows, body, 0, unroll=NA_UNROLL)


def _na_bias_table(rpb):
    H, n_dr, n_dc = rpb.shape
    W = GRID_W
    c = np.arange(W)
    qs = np.clip(c - NA_COLS // 2, 0, W - NA_COLS)
    valid = (c[None, :] >= qs[:, None]) & (c[None, :] < qs[:, None] + NA_COLS)
    dc = np.clip(c[None, :] - c[:, None] + (NA_COLS - 1), 0, n_dc - 1)
    onehot = (dc[None] == np.arange(n_dc)[:, None, None]).astype(np.float32)
    tab = jnp.einsum("hrd,dcx->hrcx", rpb.astype(F32), jnp.asarray(onehot),
                     precision=lax.Precision.HIGHEST)
    tab = jnp.where(jnp.asarray(valid)[None, None], tab, -jnp.inf)
    pairs = jnp.concatenate([tab[:, :-1], tab[:, 1:]], axis=-1)
    return pairs.reshape(H // NA_HG, NA_HG * (n_dr - 1), W, 2 * W)


def _neighborhood_attention(qkv, rpb, batch, seq):
    T = qkv.shape[0]
    HW = NA_HG * C_HEADDIM
    n_hg = C_HEADS // NA_HG
    n_rows = seq // GRID_W
    assert n_rows >= NA_ROWS and NA_ROWS % 2 == 0
    table = _na_bias_table(rpb)
    return pl.pallas_call(
        _na_kernel,
        grid=(batch, n_hg),
        in_specs=[pl.BlockSpec((seq, HW), lambda b, g: (b, g)),
                  pl.BlockSpec((seq, HW), lambda b, g: (b, n_hg + g)),
                  pl.BlockSpec((seq, HW), lambda b, g: (b, 2 * n_hg + g)),
                  pl.BlockSpec((1,) + table.shape[1:], lambda b, g: (g, 0, 0, 0))],
        out_specs=pl.BlockSpec((seq, HW), lambda b, g: (b, g)),
        out_shape=jax.ShapeDtypeStruct((T, C_HEADS * C_HEADDIM), BF16),
        compiler_params=_cparams(("parallel", "parallel")),
        name="natten",
    )(qkv, qkv, qkv, table)


def _store_token_tiles(ref, val):
    tm, width = val.shape
    n = width // LANES
    for j in range(n):
        ref[pl.ds(j, tm, stride=n), :] = val[:, j * LANES:(j + 1) * LANES]


def _load_token_tiles(ref, n):
    tm = ref.shape[0] // n
    return jnp.concatenate([ref[pl.ds(j, tm, stride=n), :] for j in range(n)], axis=1)


def _proj_router_kernel(a_ref, w_ref, gmix_ref, x_ref, g_ref, wr_ref, x1_ref, o_ref, h_ref):
    x1 = x_ref[...] + _rms(_dot(a_ref[...], w_ref[...]), gmix_ref[...])
    x1_ref[...] = x1
    h = _rms(x1, g_ref[...])
    _store_token_tiles(h_ref, h)
    h_hi = h.astype(BF16)
    h_lo = (h - h_hi.astype(F32)).astype(BF16)
    both = _dot(h_hi, wr_ref[...])
    logits = both[:, :LANES] + both[:, LANES:] + _dot(h_lo, wr_ref[:, :LANES])
    lane = lax.broadcasted_iota(jnp.int32, logits.shape, 1)
    neg = -jnp.inf
    l1 = jnp.where(lane < N_EXPERTS, logits, neg)
    m1 = jnp.max(l1, axis=-1, keepdims=True)
    i1 = jnp.min(jnp.where(l1 == m1, lane, LANES), axis=-1, keepdims=True)
    l2 = jnp.where(lane == i1, neg, l1)
    m2 = jnp.max(l2, axis=-1, keepdims=True)
    i2 = jnp.min(jnp.where(l2 == m2, lane, LANES), axis=-1, keepdims=True)
    e = jnp.exp(m2 - m1)
    w1 = 1.0 / (1.0 + e)
    w2 = e / (1.0 + e)
    out = jnp.where(lane == 0, i1.astype(F32),
                    jnp.where(lane == 1, i2.astype(F32),
                              jnp.where(lane == 2, w1, jnp.where(lane == 3, w2, 0.0))))
    o_ref[...] = out


def _proj_router(a, w, g_mix, x, g, w_router, tm=512):
    T, D = x.shape
    n = D // LANES
    wr = jnp.pad(w_router.astype(F32), ((0, 0), (0, LANES - w_router.shape[1])))
    wr_hi = wr.astype(BF16)
    wr = jnp.concatenate([wr_hi, (wr - wr_hi.astype(F32)).astype(BF16)], axis=1)
    x1, route, h = pl.pallas_call(
        _proj_router_kernel,
        grid=(T // tm,),
        in_specs=[_row_spec(tm, a.shape[1], 1), _whole_spec(w.shape, 1), _whole_spec((1, D), 1),
                  _row_spec(tm, D, 1), _whole_spec((1, D), 1), _whole_spec((D, 2 * LANES), 1)],
        out_specs=[_row_spec(tm, D, 1), _row_spec(tm, LANES, 1), _row_spec(tm * n, LANES, 1)],
        out_shape=[jax.ShapeDtypeStruct((T, D), F32),
                   jax.ShapeDtypeStruct((T, LANES), F32),
                   jax.ShapeDtypeStruct((T * n, LANES), F32)],
        compiler_params=_cparams(("parallel",)),
        name="proj_router",
    )(a, w, g_mix.reshape(1, D), x, g.reshape(1, D), wr)
    return x1, route, h.reshape(T, n, LANES)


SC_CORES = 2
SC_SUBCORES = 16
SC_WORKERS = SC_CORES * SC_SUBCORES
SC_CHUNK = 32


def _sc_mesh():
    return plsc.VectorSubcoreMesh(core_axis_name="c", subcore_axis_name="s",
                                  num_cores=SC_CORES, num_subcores=SC_SUBCORES)


def _sc_index_blocks(idx):
    return idx.reshape(SC_WORKERS, -1, SC_CHUNK)


def _sc_scatter_tokens(h, dest, n_rows):
    T, n, _ = h.shape
    per_worker = T // SC_WORKERS
    n_chunks = per_worker // SC_CHUNK
    assert per_worker * SC_WORKERS == T and n_chunks * SC_CHUNK == per_worker

    @functools.partial(
        pl.kernel, mesh=_sc_mesh(),
        out_type=jax.ShapeDtypeStruct((n_rows, n, LANES), h.dtype),
        scratch_types=[pltpu.VMEM((n_chunks, SC_CHUNK), jnp.int32),
                       pltpu.VMEM((n_chunks, SC_CHUNK), jnp.int32),
                       pltpu.VMEM((SC_CHUNK, n, LANES), h.dtype)],
        name="sc_scatter_tokens",
    )
    def scatter(h_hbm, d0_hbm, d1_hbm, o_hbm, i0_v, i1_v, rows_v):
        wid = lax.axis_index("s") * SC_CORES + lax.axis_index("c")
        pltpu.sync_copy(d0_hbm.at[wid], i0_v)
        pltpu.sync_copy(d1_hbm.at[wid], i1_v)

        @pl.loop(0, n_chunks)
        def _(j):
            pltpu.sync_copy(h_hbm.at[pl.ds(wid * per_worker + j * SC_CHUNK, SC_CHUNK)], rows_v)
            pltpu.sync_copy(rows_v, o_hbm.at[i0_v.at[j]])
            pltpu.sync_copy(rows_v, o_hbm.at[i1_v.at[j]])

    return scatter(h, _sc_index_blocks(dest[:, 0]), _sc_index_blocks(dest[:, 1]))


def _sc_gather_tokens(y, dest):
    T = dest.shape[0]
    _, n, _ = y.shape
    per_worker = T // SC_WORKERS
    n_chunks = per_worker // SC_CHUNK
    assert per_worker * SC_WORKERS == T and n_chunks * SC_CHUNK == per_worker
    out = jax.ShapeDtypeStruct((T, n, LANES), y.dtype)

    @functools.partial(
        pl.kernel, mesh=_sc_mesh(), out_type=(out, out),
        scratch_types=[pltpu.VMEM((n_chunks, SC_CHUNK), jnp.int32),
                       pltpu.VMEM((n_chunks, SC_CHUNK), jnp.int32),
                       pltpu.VMEM((SC_CHUNK, n, LANES), y.dtype)],
        name="sc_gather_tokens",
    )
    def gather(y_hbm, d0_hbm, d1_hbm, o0_hbm, o1_hbm, i0_v, i1_v, rows_v):
        wid = lax.axis_index("s") * SC_CORES + lax.axis_index("c")
        pltpu.sync_copy(d0_hbm.at[wid], i0_v)
        pltpu.sync_copy(d1_hbm.at[wid], i1_v)

        @pl.loop(0, n_chunks)
        def _(j):
            rows = pl.ds(wid * per_worker + j * SC_CHUNK, SC_CHUNK)
            pltpu.sync_copy(y_hbm.at[i0_v.at[j]], rows_v)
            pltpu.sync_copy(rows_v, o0_hbm.at[rows])
            pltpu.sync_copy(y_hbm.at[i1_v.at[j]], rows_v)
            pltpu.sync_copy(rows_v, o1_hbm.at[rows])

    return gather(y, _sc_index_blocks(dest[:, 0]), _sc_index_blocks(dest[:, 1]))


SC_PACK_PAIRS = 8
SC_LANES = 16


def _sc_pack_bf16_rows(w):
    R, C = w.shape
    pairs = R // 2
    per_worker = pairs // SC_WORKERS
    n_chunks = per_worker // SC_PACK_PAIRS
    assert n_chunks * SC_PACK_PAIRS * SC_WORKERS * 2 == R and C % SC_LANES == 0

    def round_bits(v):
        u = plsc.bitcast(v, jnp.int32)
        return u + 0x7FFF + (lax.shift_right_logical(u, 16) & 1)

    @functools.partial(
        pl.kernel, mesh=_sc_mesh(),
        out_type=jax.ShapeDtypeStruct((pairs, C), jnp.int32),
        scratch_types=[pltpu.VMEM((2 * SC_PACK_PAIRS, C), F32), pltpu.VMEM((SC_PACK_PAIRS, C), jnp.int32)],
        compiler_params=pltpu.CompilerParams(use_tc_tiling_on_sc=True, needs_layout_passes=False),
        name="sc_pack_bf16_rows",
    )
    def pack(w_hbm, o_hbm, in_v, out_v):
        wid = lax.axis_index("s") * SC_CORES + lax.axis_index("c")

        @pl.loop(0, n_chunks)
        def _(j):
            p0 = (wid * n_chunks + j) * SC_PACK_PAIRS
            pltpu.sync_copy(w_hbm.at[pl.ds(2 * p0, 2 * SC_PACK_PAIRS)], in_v)

            @pl.loop(0, C, step=SC_LANES)
            def _(c):
                cols = pl.ds(c, SC_LANES)
                for i in range(SC_PACK_PAIRS):
                    lo = lax.shift_right_logical(round_bits(in_v[2 * i, cols]), 16)
                    hi = round_bits(in_v[2 * i + 1, cols]) & jnp.int32(-65536)
                    out_v[i, cols] = lo | hi

            pltpu.sync_copy(out_v, o_hbm.at[pl.ds(p0, SC_PACK_PAIRS)])

    return pack(w)


def _moe_ffn_kernel(te_ref, na_ref, h_ref, wg_ref, wu_ref, wd_ref, o_ref, acc_scr, *, tf):
    i = pl.program_id(0)
    D = acc_scr.shape[1]
    F = wg_ref.shape[2]
    unpack = lambda packed: pltpu.bitcast(packed, BF16)

    @pl.when(i < na_ref[0])
    def _():
        h = _load_token_tiles(h_ref, D // LANES).astype(BF16)
        for f0 in range(0, F, tf):
            a = (_silu(_dot(h, unpack(wg_ref[0, :, f0:f0 + tf])))
                 * _dot(h, unpack(wu_ref[0, :, f0:f0 + tf]))).astype(BF16)
            part = _dot(a, unpack(wd_ref[0, f0 // 2:(f0 + tf) // 2, :]))
            if f0 == 0:
                acc_scr[...] = part
            else:
                acc_scr[...] += part
        _store_token_tiles(o_ref, acc_scr[...])

    @pl.when(i >= na_ref[0])
    def _():
        o_ref[...] = jnp.zeros_like(o_ref)


def _moe_ffn(hs, wg, wu, wd, tile_expert, n_active, tm=MOE_TM, tf=512):
    rows, n, _ = hs.shape
    E, half_d, F = wg.shape
    D = 2 * half_d
    nt = tile_expert.shape[0]
    assert rows == nt * tm and n * LANES == D and wd.shape == (E, F // 2, D)
    grid_spec = pltpu.PrefetchScalarGridSpec(
        num_scalar_prefetch=2,
        grid=(nt,),
        in_specs=[pl.BlockSpec((tm * n, LANES), lambda i, te, na: (i, 0)),
                  pl.BlockSpec((1, D // 2, F), lambda i, te, na: (te[i], 0, 0)),
                  pl.BlockSpec((1, D // 2, F), lambda i, te, na: (te[i], 0, 0)),
                  pl.BlockSpec((1, F // 2, D), lambda i, te, na: (te[i], 0, 0))],
        out_specs=pl.BlockSpec((tm * n, LANES), lambda i, te, na: (i, 0)),
        scratch_shapes=[pltpu.VMEM((tm, D), F32)],
    )
    y = pl.pallas_call(
        functools.partial(_moe_ffn_kernel, tf=tf),
        grid_spec=grid_spec,
        out_shape=jax.ShapeDtypeStruct((rows * n, LANES), F32),
        compiler_params=_cparams(("arbitrary",)),
        name="moe_ffn",
    )(tile_expert, n_active, hs.reshape(rows * n, LANES), wg, wu, wd)
    return y.reshape(rows, n, LANES)


def _combine_ple_kernel(y0_ref, y1_ref, route_ref, x_ref, g_ref, p_ref, gin_ref, wpg_ref, wpp_ref,
                        gple_ref, o_ref):
    n = x_ref.shape[1] // LANES
    route = route_ref[...]
    moe = route[:, 2:3] * _load_token_tiles(y0_ref, n) + route[:, 3:4] * _load_token_tiles(y1_ref, n)
    x2 = x_ref[...] + _rms(moe, g_ref[...])
    o_ref[...] = _ple_update(x2, p_ref, gin_ref, wpg_ref, wpp_ref, gple_ref)


def _combine_ple(y0, y1, route, x, g, p, g_in, wpg, wpp, g_ple, tm=512):
    T, D = x.shape
    n = D // LANES
    vec = lambda v: v.reshape(1, D)
    return pl.pallas_call(
        _combine_ple_kernel,
        grid=(T // tm,),
        in_specs=[_row_spec(tm * n, LANES, 1), _row_spec(tm * n, LANES, 1), _row_spec(tm, LANES, 1),
                  _row_spec(tm, D, 1), _whole_spec((1, D), 1),
                  _row_spec(tm, p.shape[1], 1), _whole_spec((1, D), 1),
                  _whole_spec(wpg.shape, 1), _whole_spec(wpp.shape, 1), _whole_spec((1, D), 1)],
        out_specs=_row_spec(tm, D, 1),
        out_shape=jax.ShapeDtypeStruct((T, D), F32),
        compiler_params=_cparams(("parallel",)),
        name="combine_ple",
    )(y0.reshape(T * n, LANES), y1.reshape(T * n, LANES), route, x, vec(g), p, vec(g_in), wpg, wpp,
      vec(g_ple))


def _moe_plan(route, tm=MOE_TM):
    T = route.shape[0]
    e = route[:, :TOP_K].astype(jnp.int32).reshape(-1)
    onehot = (e[:, None] == jnp.arange(N_EXPERTS, dtype=jnp.int32)[None, :]).astype(jnp.int32)
    csum = jnp.cumsum(onehot, axis=0)
    rank = jnp.sum(csum * onehot, axis=1) - 1
    counts = csum[-1]
    padded = ((counts + tm - 1) // tm) * tm
    ends = jnp.cumsum(padded)
    starts = ends - padded
    dest = jnp.sum(starts[None, :] * onehot, axis=1) + rank
    nt = (T * TOP_K) // tm + N_EXPERTS
    tile_start = jnp.arange(nt, dtype=jnp.int32) * tm
    tile_expert = jnp.minimum(jnp.sum((tile_start[:, None] >= ends[None, :]).astype(jnp.int32), axis=1),
                              N_EXPERTS - 1).astype(jnp.int32)
    n_active = (ends[-1] // tm).astype(jnp.int32).reshape(1)
    return tile_expert, n_active, dest.reshape(T, TOP_K).astype(jnp.int32)


def kernel(x, p, hgrn_lb_raw, e_norm_mix_pre, e_w_in, e_conv_w, e_conv_b, e_A_log, e_dt_bias, e_D, e_a_norm, e_b_norm, e_w_out, e_norm_mix_post, e_norm_ffn_pre, e_w_ffn_gate, e_w_ffn_up, e_w_ffn_down, e_norm_ffn_post, o_norm_mix_pre, o_w_qkv, o_rpb, o_w_out, o_norm_mix_post, o_norm_ffn_pre, o_w_router, o_w_exp_gate, o_w_exp_up, o_w_exp_down, o_norm_ffn_post, ple_norm_in, ple_w_gate, ple_w_proj, ple_norm_post):
    batch, seq, d_model = x.shape
    depth = p.shape[0]
    T = batch * seq
    xt = x.reshape(T, d_model)
    lb_all = jnp.cumsum(jax.nn.softmax(hgrn_lb_raw.astype(F32), axis=0), axis=0)

    a_kdim = A_HEADS * A_DK
    b_width = B_HEADS * B_HEADDIM
    conv_dim = b_width + 2 * B_GROUPS * B_STATE
    main_w = 5 * a_kdim + b_width + conv_dim

    def pack_experts(w):
        E, R, C = w.shape
        return _sc_pack_bf16_rows(w.reshape(E * R, C)).reshape(E, R // 2, C)

    for li in range(depth):
        j = li // 2
        p_li = p[li].reshape(T, -1)
        ple = (ple_norm_in[li], ple_w_gate[li].astype(BF16), ple_w_proj[li].astype(BF16), ple_norm_post[li])
        if li % 2 == 0:
            w_in = e_w_in[j]
            dtf = w_in[:, main_w:main_w + B_HEADS].reshape(d_model, B_GROUPS, B_HPG)
            dtb = w_in[:, main_w + B_HEADS:].reshape(d_model, B_GROUPS, B_HPG)
            w_dt = jnp.pad(jnp.concatenate([dtf, dtb], axis=2), ((0, 0), (0, 0), (0, LANES - 2 * B_HPG)))
            w_all = jnp.concatenate([w_in[:, :main_w], w_dt.reshape(d_model, B_GROUPS * LANES)], axis=1)
            outs = [(a_kdim, BF16, "silu"), (a_kdim, F32, "log_gate"), (a_kdim, F32, "log_gate"),
                    (a_kdim, BF16, None), (a_kdim, BF16, "silu"),
                    (b_width + conv_dim, BF16, None), (B_GROUPS * LANES, F32, None)]
            q_a, gf_a, gb_a, i_a, gate_a, u_b, dt = _norm_proj(
                xt, e_norm_mix_pre[j], w_all.astype(BF16), outs, lb=lb_all[li])
            o_a = _hgrn_mixer(q_a, gf_a, gb_a, i_a, gate_a, e_a_norm[j], batch, seq)
            o_b = _ssd_mixer(u_b, dt, e_conv_w[j], e_conv_b[j], e_A_log[j], e_dt_bias[j], e_D[j],
                             e_b_norm[j], batch, seq, 0)
            w_out = e_w_out[j].astype(BF16)
            xt = _mix_ffn_ple(o_a, o_b, w_out[:a_kdim], w_out[a_kdim:], e_norm_mix_post[j], xt,
                              e_norm_ffn_pre[j], e_w_ffn_gate[j].astype(BF16), e_w_ffn_up[j].astype(BF16),
                              e_w_ffn_down[j].astype(BF16), e_norm_ffn_post[j], p_li, *ple)
        else:
            (qkv,) = _norm_proj(xt, o_norm_mix_pre[j], o_w_qkv[j].astype(BF16),
                                [(3 * C_HEADS * C_HEADDIM, BF16, None)])
            o_c = _neighborhood_attention(qkv, o_rpb[j], batch, seq)
            xt, route, h = _proj_router(o_c, o_w_out[j].astype(BF16), o_norm_mix_post[j], xt,
                                        o_norm_ffn_pre[j], o_w_router[j])
            tile_expert, n_active, dest = _moe_plan(route)
            hs = _sc_scatter_tokens(h, dest, tile_expert.shape[0] * MOE_TM)
            expert_w = [pack_experts(w[j]) for w in (o_w_exp_gate, o_w_exp_up, o_w_exp_down)]
            ys = _moe_ffn(hs, *expert_w, tile_expert, n_active)
            y0, y1 = _sc_gather_tokens(ys, dest)
            xt = _combine_ple(y0, y1, route, xt, o_norm_ffn_post[j], p_li, *ple)
    return xt.reshape(batch, seq, d_model)
```

```python
import functools
import math

import numpy as np
import jax
import jax.numpy as jnp
from jax import lax
from jax.experimental import pallas as pl
from jax.experimental.pallas import tpu as pltpu
from jax.experimental.pallas import tpu_sc as plsc

F32 = jnp.float32
BF16 = jnp.bfloat16
EPS = 1e-6

LANES = 128
SUBLANES = 8
VMEM_LIMIT_BYTES = 56 * 1024 * 1024

GRID_W = 64
A_HEADS, A_DK, A_CHUNK = 4, 128, 64
A_UNROLL = 8
B_HEADS, B_HEADDIM, B_GROUPS, B_STATE, B_CONV, B_CHUNK = 8, 64, 2, 128, 5, 128
B_HPG = B_HEADS // B_GROUPS
B_GW = B_HPG * B_HEADDIM
C_HEADS, C_HEADDIM = 16, 64
NA_ROWS, NA_COLS = 8, 16
NA_HG = 4
NA_UNROLL = 8
N_EXPERTS, TOP_K = 8, 2
MOE_TM = 512


def _cparams(sem):
    return pltpu.CompilerParams(dimension_semantics=sem, vmem_limit_bytes=VMEM_LIMIT_BYTES)


def _rms(x, g):
    return x * lax.rsqrt(jnp.mean(x * x, axis=-1, keepdims=True) + EPS) * g


def _silu(x):
    return x * jax.nn.sigmoid(x)


def _dot(a, b):
    return jnp.dot(a, b, preferred_element_type=F32)


def _dot_nt(a, b):
    return lax.dot_general(a, b, (((1,), (1,)), ((), ())), preferred_element_type=F32)


def _dot_tn(a, b):
    return lax.dot_general(a, b, (((0,), (0,)), ((), ())), preferred_element_type=F32)


def _norm_proj_kernel(x_ref, g_ref, w_ref, lb_ref, *o_refs, acts, col_chunk):
    h = _rms(x_ref[...], g_ref[...]).astype(BF16)
    off = 0
    for o_ref, act in zip(o_refs, acts):
        n = o_ref.shape[1]
        for c0 in range(0, n, col_chunk):
            c1 = min(c0 + col_chunk, n)
            y = _dot(h, w_ref[:, off + c0:off + c1])
            if act == "silu":
                y = _silu(y)
            elif act == "log_gate":
                lb = lb_ref[:, c0:c1]
                y = jnp.log(lb + (1.0 - lb) * jax.nn.sigmoid(y))
            o_ref[:, c0:c1] = y.astype(o_ref.dtype)
        off += n


def _norm_proj(x, g, w, outs, lb=None, tm=512, col_chunk=1024):
    T, D = x.shape
    N = w.shape[1]
    widths = [o[0] for o in outs]
    assert sum(widths) == N and T % tm == 0
    if lb is None:
        lb = jnp.zeros((LANES,), F32)
    return pl.pallas_call(
        functools.partial(_norm_proj_kernel, acts=tuple(o[2] for o in outs), col_chunk=col_chunk),
        grid=(T // tm,),
        in_specs=[_row_spec(tm, D, 1), _whole_spec((1, D), 1), _whole_spec((D, N), 1),
                  _whole_spec((1, lb.shape[0]), 1)],
        out_specs=[_row_spec(tm, n, 1) for n in widths],
        out_shape=[jax.ShapeDtypeStruct((T, n), dt) for n, dt, _ in outs],
        compiler_params=_cparams(("parallel",)),
        name="norm_proj",
    )(x, g.reshape(1, D), w, lb.reshape(1, -1))


def _ple_update(x, p_ref, gin_ref, wg_ref, wp_ref, gpost_ref):
    h = _rms(x, gin_ref[...]).astype(BF16)
    gate = jax.nn.sigmoid(_dot(h, wg_ref[...]))
    proj = _dot(p_ref[...].astype(BF16), wp_ref[...])
    return x + _rms(gate * proj, gpost_ref[...])


def _row_spec(tm, width, n_grid):
    return pl.BlockSpec((tm, width), (lambda i: (i, 0)) if n_grid == 1 else (lambda i, j: (i, 0)))


def _whole_spec(shape, n_grid):
    zeros = (0,) * len(shape)
    return pl.BlockSpec(shape, (lambda i: zeros) if n_grid == 1 else (lambda i, j: zeros))


def _mix_ffn_ple_kernel(oa_ref, ob_ref, wa_ref, wb_ref, gmix_ref, x_ref, gpre_ref, wg_ref, wu_ref, wd_ref,
                        gpost_ref, p_ref, gin_ref, wpg_ref, wpp_ref, gple_ref, o_ref,
                        x1_scr, h_scr, acc_scr):
    j = pl.program_id(1)

    @pl.when(j == 0)
    def _():
        mix = _dot(oa_ref[...], wa_ref[...]) + _dot(ob_ref[...], wb_ref[...])
        x1 = x_ref[...] + _rms(mix, gmix_ref[...])
        x1_scr[...] = x1
        h_scr[...] = _rms(x1, gpre_ref[...]).astype(BF16)
        acc_scr[...] = jnp.zeros_like(acc_scr)

    h = h_scr[...]
    a = (_silu(_dot(h, wg_ref[...])) * _dot(h, wu_ref[...])).astype(BF16)
    acc_scr[...] += _dot(a, wd_ref[...])

    @pl.when(j == pl.num_programs(1) - 1)
    def _():
        x2 = x1_scr[...] + _rms(acc_scr[...], gpost_ref[...])
        o_ref[...] = _ple_update(x2, p_ref, gin_ref, wpg_ref, wpp_ref, gple_ref)


def _mix_ffn_ple(o_a, o_b, w_a, w_b, g_mix, x, g_pre, wg, wu, wd, g_post, p, g_in, wpg, wpp, g_ple,
                 tm=512, tf=1408):
    T, D = x.shape
    F = wg.shape[1]
    assert F % tf == 0 and T % tm == 0
    vec = lambda g: g.reshape(1, D)
    return pl.pallas_call(
        _mix_ffn_ple_kernel,
        grid=(T // tm, F // tf),
        in_specs=[_row_spec(tm, o_a.shape[1], 2), _row_spec(tm, o_b.shape[1], 2),
                  _whole_spec(w_a.shape, 2), _whole_spec(w_b.shape, 2), _whole_spec((1, D), 2),
                  _row_spec(tm, D, 2), _whole_spec((1, D), 2),
                  pl.BlockSpec((D, tf), lambda i, j: (0, j)),
                  pl.BlockSpec((D, tf), lambda i, j: (0, j)),
                  pl.BlockSpec((tf, D), lambda i, j: (j, 0)),
                  _whole_spec((1, D), 2),
                  _row_spec(tm, p.shape[1], 2), _whole_spec((1, D), 2),
                  _whole_spec(wpg.shape, 2), _whole_spec(wpp.shape, 2), _whole_spec((1, D), 2)],
        out_specs=_row_spec(tm, D, 2),
        out_shape=jax.ShapeDtypeStruct((T, D), F32),
        scratch_shapes=[pltpu.VMEM((tm, D), F32), pltpu.VMEM((tm, D), BF16), pltpu.VMEM((tm, D), F32)],
        compiler_params=_cparams(("parallel", "arbitrary")),
        name="mix_ffn_ple",
    )(o_a, o_b, w_a, w_b, vec(g_mix), x, vec(g_pre), wg, wu, wd, vec(g_post), p, vec(g_in), wpg, wpp,
      vec(g_ple))


def _roll_rows(x, s, rev):
    n = x.shape[0]
    return pltpu.roll(x, (n - s) if rev else s, 0)


def _cumsum_rows(x, tau, rev):
    n = x.shape[0]
    s = 1
    while s < n:
        if s % SUBLANES:
            shifted = jnp.where(tau >= s, _roll_rows(x, s, rev), 0.0)
        else:
            zeros = jnp.zeros((s,) + x.shape[1:], x.dtype)
            shifted = (jnp.concatenate([x[s:], zeros], axis=0) if rev
                       else jnp.concatenate([zeros, x[:n - s]], axis=0))
        x = x + shifted
        s *= 2
    return x


def _hgrn_levels(C):
    return [C >> (i + 1) for i in range(C.bit_length() - 1)]


def _hgrn_pair_classes(C):
    t = np.arange(C)[:, None]
    s = np.arange(C)[None, :]
    out = np.full((2 * C, 2 * C), -1, np.int32)
    for d, rev in enumerate((False, True)):
        tau, sig = (C - 1 - t, C - 1 - s) if rev else (t, s)
        blk = np.full((C, C), -1, np.int32)
        blk[t == s] = 0
        for i, L in enumerate(_hgrn_levels(C)):
            m = ((t & -(2 * L)) == (s & -(2 * L))) & ((tau & (2 * L - 1)) >= L) & ((sig & (2 * L - 1)) < L)
            blk[m] = i + 1
        out[d * C:(d + 1) * C, d * C:(d + 1) * C] = blk
    return out


def _hgrn_kernel(q_ref, gf_ref, gb_ref, i_ref, gate_ref, cls_ref, ng_ref, o_ref, acc_scr, st_scr):
    S, DK = q_ref.shape
    C = A_CHUNK
    nc = S // C
    assert nc % (2 * A_UNROLL) == 0
    ng = ng_ref[...]
    row = lax.broadcasted_iota(jnp.int32, (C, DK), 0)
    levels = _hgrn_levels(C)
    zero_half = jnp.zeros((C, DK), BF16)

    def stack(top, bottom):
        return jnp.concatenate([top, bottom], axis=0)

    def block_diag(x):
        return stack(jnp.concatenate([x[:C], zero_half], axis=1), jnp.concatenate([zero_half, x[C:]], axis=1))

    def make_decay(rev):
        tau = (C - 1 - row) if rev else row
        odd_rank = (tau & 1) == 1
        last = 0 if rev else C - 1

        def ref_rows(b, L):
            off = L if rev else L - 1
            if 2 * L >= SUBLANES:
                pieces = [jnp.broadcast_to(b[j * 2 * L + off:j * 2 * L + off + 1, :], (2 * L, DK))
                          for j in range(C // (2 * L))]
                return pieces[0] if len(pieces) == 1 else jnp.concatenate(pieces, axis=0)
            if L == 1:
                return jnp.where(odd_rank, _roll_rows(b, 1, rev), b)
            b3 = b.reshape(C // SUBLANES, SUBLANES, DK)
            sub = lax.broadcasted_iota(jnp.int32, b3.shape, 1)
            out = None
            for j in range(SUBLANES // (2 * L)):
                piece = jnp.broadcast_to(b3[:, j * 2 * L + off:j * 2 * L + off + 1, :], b3.shape)
                out = piece if out is None else jnp.where(sub >= j * 2 * L, piece, out)
            return out.reshape(C, DK)

        def decay(g):
            b = _cumsum_rows(g, tau, rev)
            return b, [ref_rows(b, L) for L in levels], b[last:last + 1, :]

        return decay

    decay_fwd, decay_bwd = make_decay(False), make_decay(True)
    pair_cls = cls_ref[...]
    st_scr[...] = jnp.zeros_like(st_scr)

    def chunk_pair(rows_f, rows_b, st):
        q = stack(q_ref[rows_f, :], q_ref[rows_b, :]).astype(F32)
        g_f, g_b = gf_ref[rows_f, :], gb_ref[rows_b, :]
        k = 1.0 - jnp.exp(stack(g_f, g_b))
        vb = stack(i_ref[rows_f, :], i_ref[rows_b, :])
        b_f, refs_f, last_f = decay_fwd(g_f)
        b_b, refs_b, last_b = decay_bwd(g_b)
        b = stack(b_f, b_b)

        o = _dot_nt(block_diag((q * jnp.exp(b)).astype(BF16)), st.astype(BF16))

        attn = jnp.where(pair_cls == 0, _dot_nt(q.astype(BF16), k.astype(BF16)), 0.0)
        for i in range(len(levels)):
            e = jnp.exp(-jnp.abs(b - stack(refs_f[i], refs_b[i])))
            attn = jnp.where(pair_cls == i + 1, _dot_nt((q * e).astype(BF16), (k * e).astype(BF16)), attn)
        o = o + _dot(attn.astype(BF16), vb)

        b_last = stack(jnp.broadcast_to(last_f, (C, DK)), jnp.broadcast_to(last_b, (C, DK)))
        khat = block_diag((k * jnp.exp(b_last - b)).astype(BF16))
        keep = jnp.exp(jnp.concatenate([last_f, last_b], axis=1))
        return o, st * keep + _dot_tn(vb, khat)

    def make_body(final):
        def body(it, carry):
            st = st_scr[...]
            for u in range(A_UNROLL):
                ci = it * A_UNROLL + u
                rows_f = pl.ds(pl.multiple_of(ci * C, C), C)
                rows_b = pl.ds(pl.multiple_of((nc - 1 - ci) * C, C), C)
                o, st = chunk_pair(rows_f, rows_b, st)
                for rows, part in ((rows_f, o[:C]), (rows_b, o[C:])):
                    if final:
                        tot = acc_scr[rows, :] + part
                        o_ref[rows, :] = (_rms(tot, ng) * gate_ref[rows, :].astype(F32)).astype(o_ref.dtype)
                    else:
                        acc_scr[rows, :] = part
            st_scr[...] = st
            return carry

        return body

    trips = nc // A_UNROLL
    lax.fori_loop(0, trips // 2, make_body(False), 0)
    lax.fori_loop(trips // 2, trips, make_body(True), 0)


def _hgrn_mixer(q, g_fwd, g_bwd, v, gate, norm_g, batch, seq):
    T = q.shape[0]
    H, DK = A_HEADS, A_DK
    head = pl.BlockSpec((seq, DK), lambda b, h: (b, h))
    classes = jnp.asarray(_hgrn_pair_classes(A_CHUNK))
    return pl.pallas_call(
        _hgrn_kernel,
        grid=(batch, H),
        in_specs=[head, head, head, head, head,
                  pl.BlockSpec(classes.shape, lambda b, h: (0, 0)),
                  pl.BlockSpec((1, DK), lambda b, h: (0, 0))],
        out_specs=pl.BlockSpec((seq, DK), lambda b, h: (b, h)),
        out_shape=jax.ShapeDtypeStruct((T, H * DK), BF16),
        scratch_shapes=[pltpu.VMEM((seq, DK), F32), pltpu.VMEM((DK, 2 * DK), F32)],
        compiler_params=_cparams(("parallel", "parallel")),
        name="hgrn2",
    )(q, g_fwd, g_bwd, v, gate, classes, norm_g.reshape(1, DK))


def _expand_heads(col, j0, width):
    q = col.shape[0]
    lane = lax.broadcasted_iota(jnp.int32, (q, width), 1)
    out = jnp.broadcast_to(col[:, j0 + B_HPG - 1:j0 + B_HPG], (q, width))
    for j in range(B_HPG - 2, -1, -1):
        out = jnp.where(lane < (j + 1) * B_HEADDIM,
                        jnp.broadcast_to(col[:, j0 + j:j0 + j + 1], (q, width)), out)
    return out


def _ssd_kernel(z_ref, x_ref, bm_ref, cm_ref, dt_ref, cwx_ref, cwb_ref, cwc_ref,
                cbx_ref, cbb_ref, cbc_ref, hp_ref, dsk_ref, ng_ref, o_ref,
                xs_scr, b_scr, c_scr, y_scr, *st_scrs):
    S = x_ref.shape[0]
    Q = B_CHUNK
    nc = S // Q
    GW = B_GW
    hp = hp_ref[0]
    a_row, dtb_row = hp[0:1, :], hp[1:2, :]
    row = lax.broadcasted_iota(jnp.int32, (Q, LANES), 0)
    t2 = lax.broadcasted_iota(jnp.int32, (Q, Q), 0)
    s2 = lax.broadcasted_iota(jnp.int32, (Q, Q), 1)
    lane_gw = lax.broadcasted_iota(jnp.int32, (Q, GW), 1)
    halo = SUBLANES

    def conv_body(c, carry):
        r0 = pl.multiple_of(c * Q, Q)
        rp = pl.multiple_of(jnp.maximum(r0 - halo, 0), halo)
        rn = pl.multiple_of(jnp.minimum(r0 + Q, S - halo), halo)
        has_prev = c > 0
        has_next = c < nc - 1

        def conv(src_ref, w_ref, bias_ref):
            prev = jnp.where(has_prev, src_ref[pl.ds(rp, halo), :].astype(F32), 0.0)
            nxt = jnp.where(has_next, src_ref[pl.ds(rn, halo), :].astype(F32), 0.0)
            xx = jnp.concatenate([prev, src_ref[pl.ds(r0, Q), :].astype(F32), nxt], axis=0)
            n = Q + 2 * halo
            w = w_ref[0]
            acc = None
            for j in range(B_CONV):
                delta = j - B_CONV // 2
                sh = xx if delta == 0 else pltpu.roll(xx, (-delta) % n, 0)
                term = sh[halo:halo + Q, :] * w[j:j + 1, :]
                acc = term if acc is None else acc + term
            return _silu(acc + bias_ref[0])

        xs_scr[pl.ds(r0, Q), :] = conv(x_ref, cwx_ref, cbx_ref)
        b_scr[pl.ds(r0, Q), :] = conv(bm_ref, cwb_ref, cbb_ref).astype(BF16)
        c_scr[pl.ds(r0, Q), :] = conv(cm_ref, cwc_ref, cbc_ref).astype(BF16)
        return carry

    lax.fori_loop(0, nc, conv_body, 0)

    def make_chunk(rev):
        tau = (Q - 1 - row) if rev else row
        last = 0 if rev else Q - 1
        j0 = B_HPG if rev else 0
        pair_ok = (s2 >= t2) if rev else (t2 >= s2)

        def chunk(rows, st):
            dt = jax.nn.softplus(dt_ref[rows, :] + dtb_row)
            cs = _cumsum_rows(dt * a_row, tau, rev)
            cs_t = cs.T
            xs = xs_scr[rows, :]
            bm = b_scr[rows, :]
            cm = c_scr[rows, :]
            xdt = xs * _expand_heads(dt, j0, GW)
            ecs = _expand_heads(cs, j0, GW)
            ecs_last = ecs[last:last + 1, :]

            y = _dot(cm, st.astype(BF16)) * jnp.exp(ecs)

            scores = _dot_nt(cm, bm)
            for j in range(B_HPG):
                col = jnp.broadcast_to(cs[:, j0 + j:j0 + j + 1], (Q, Q))
                rw = jnp.broadcast_to(cs_t[j0 + j:j0 + j + 1, :], (Q, Q))
                decay = jnp.where(pair_ok, jnp.exp(jnp.minimum(col - rw, 0.0)), 0.0)
                in_head = (lane_gw >= j * B_HEADDIM) & (lane_gw < (j + 1) * B_HEADDIM)
                xh = jnp.where(in_head, xdt, 0.0).astype(BF16)
                y = y + _dot((scores * decay).astype(BF16), xh)

            xdec = (xdt * jnp.exp(ecs_last - ecs)).astype(BF16)
            return y, st * jnp.exp(ecs_last) + _dot_tn(bm, xdec)

        return chunk

    chunk_fns = (make_chunk(False), make_chunk(True))
    for st_scr in st_scrs:
        st_scr[...] = jnp.zeros_like(st_scr)

    def make_body(final):
        def body(ci, carry):
            for rev, chunk, st_scr in zip((False, True), chunk_fns, st_scrs):
                c = (nc - 1 - ci) if rev else ci
                rows = pl.ds(pl.multiple_of(c * Q, Q), Q)
                y, st = chunk(rows, st_scr[...])
                st_scr[...] = st
                if final:
                    tot = y_scr[rows, :] + y + dsk_ref[0] * xs_scr[rows, :]
                    tot = tot * _silu(z_ref[rows, :].astype(F32))
                    o_ref[rows, :] = _rms(tot, ng_ref[0]).astype(o_ref.dtype)
                else:
                    y_scr[rows, :] = y
            return carry

        return body

    assert nc % 2 == 0
    lax.fori_loop(0, nc // 2, make_body(False), 0)
    lax.fori_loop(nc // 2, nc, make_body(True), 0)


def _ssd_mixer(u, dt, conv_w, conv_b, a_log, dt_bias, d_skip, norm_g, batch, seq, col0):
    T = u.shape[0]
    G, GW, N = B_GROUPS, B_GW, B_STATE
    W = B_HEADS * B_HEADDIM
    assert col0 % GW == 0 and GW == 2 * N
    z_blk = col0 // GW
    x_blk = z_blk + W // GW
    b_blk = (col0 + 2 * W) // N
    c_blk = b_blk + G

    def pad_rows(w):
        return jnp.pad(w, ((0, SUBLANES - w.shape[0]), (0, 0)))

    cwx = pad_rows(conv_w[:, :W]).reshape(SUBLANES, G, GW).transpose(1, 0, 2)
    cwb = pad_rows(conv_w[:, W:W + G * N]).reshape(SUBLANES, G, N).transpose(1, 0, 2)
    cwc = pad_rows(conv_w[:, W + G * N:]).reshape(SUBLANES, G, N).transpose(1, 0, 2)
    cbx = conv_b[:W].reshape(G, 1, GW)
    cbb = conv_b[W:W + G * N].reshape(G, 1, N)
    cbc = conv_b[W + G * N:].reshape(G, 1, N)
    a_neg = -jnp.exp(a_log.astype(F32))
    per_dir = lambda v: jnp.concatenate([v[0].reshape(G, B_HPG), v[1].reshape(G, B_HPG)], axis=1)
    hp = jnp.stack([per_dir(a_neg), per_dir(dt_bias.astype(F32))], axis=1)
    hp = jnp.pad(hp, ((0, 0), (0, SUBLANES - 2), (0, LANES - 2 * B_HPG)))
    dsk = jnp.repeat(d_skip.astype(F32), B_HEADDIM).reshape(G, 1, GW)
    ng = norm_g.reshape(G, 1, GW)

    gspec = lambda shape: pl.BlockSpec((1,) + shape, lambda b, g: (g, 0, 0))
    return pl.pallas_call(
        _ssd_kernel,
        grid=(batch, G),
        in_specs=[pl.BlockSpec((seq, GW), lambda b, g: (b, z_blk + g)),
                  pl.BlockSpec((seq, GW), lambda b, g: (b, x_blk + g)),
                  pl.BlockSpec((seq, N), lambda b, g: (b, b_blk + g)),
                  pl.BlockSpec((seq, N), lambda b, g: (b, c_blk + g)),
                  pl.BlockSpec((seq, LANES), lambda b, g: (b, g)),
                  gspec((SUBLANES, GW)), gspec((SUBLANES, N)), gspec((SUBLANES, N)),
                  gspec((1, GW)), gspec((1, N)), gspec((1, N)),
                  gspec((SUBLANES, LANES)), gspec((1, GW)), gspec((1, GW))],
        out_specs=pl.BlockSpec((seq, GW), lambda b, g: (b, g)),
        out_shape=jax.ShapeDtypeStruct((T, W), BF16),
        scratch_shapes=[pltpu.VMEM((seq, GW), F32), pltpu.VMEM((seq, N), BF16),
                        pltpu.VMEM((seq, N), BF16), pltpu.VMEM((seq, GW), F32),
                        pltpu.VMEM((N, GW), F32), pltpu.VMEM((N, GW), F32)],
        compiler_params=_cparams(("parallel", "parallel")),
        name="ssd",
    )(u, u, u, u, dt, cwx, cwb, cwc, cbx, cbb, cbc, hp, dsk, ng)


def _na_kernel(q_ref, k_ref, v_ref, bias_ref, o_ref):
    S, HW = q_ref.shape
    W = GRID_W
    n_rows = S // W
    kh = min(NA_ROWS, n_rows)
    nk = kh * W
    n_pairs = bias_ref.shape[1] // NA_HG
    lane = lax.broadcasted_iota(jnp.int32, (W, HW), 1)
    head_masks = [(lane >= h * C_HEADDIM) & (lane < (h + 1) * C_HEADDIM) for h in range(NA_HG)]
    scale = C_HEADDIM ** -0.5

    def body(r, carry):
        rs = jnp.clip(r - kh // 2, 0, n_rows - kh)
        q = q_ref[pl.ds(pl.multiple_of(r * W, W), W), :] * jnp.asarray(scale, q_ref.dtype)
        kw = k_ref[pl.ds(pl.multiple_of(rs * W, W), nk), :]
        vw = v_ref[pl.ds(pl.multiple_of(rs * W, W), nk), :]
        zero = jnp.zeros_like(q)
        qs = jnp.concatenate([jnp.where(m, q, zero) for m in head_masks], axis=0)
        d0 = (NA_ROWS - 1) - (r - rs)
        bias = jnp.concatenate(
            [jnp.concatenate([bias_ref[0, h * n_pairs + d0 + 2 * m] for m in range(kh // 2)], axis=1)
             for h in range(NA_HG)], axis=0)
        s = _dot_nt(qs, kw) + bias
        m = jnp.max(s, axis=-1, keepdims=True)
        p = jnp.exp(s - m)
        l = jnp.sum(p, axis=-1, keepdims=True)
        res = _dot(p.astype(BF16), vw) / l
        out = jnp.zeros((W, HW), F32)
        for h in range(NA_HG):
            out = jnp.where(head_masks[h], res[h * W:(h + 1) * W, :], out)
        o_ref[pl.ds(pl.multiple_of(r * W, W), W), :] = out.astype(o_ref.dtype)
        return carry

    lax.fori_loop(0, n_rows, body, 0, unroll=NA_UNROLL)


def _na_bias_table(rpb):
    H, n_dr, n_dc = rpb.shape
    W = GRID_W
    c = np.arange(W)
    qs = np.clip(c - NA_COLS // 2, 0, W - NA_COLS)
    valid = (c[None, :] >= qs[:, None]) & (c[None, :] < qs[:, None] + NA_COLS)
    dc = np.clip(c[None, :] - c[:, None] + (NA_COLS - 1), 0, n_dc - 1)
    onehot = (dc[None] == np.arange(n_dc)[:, None, None]).astype(np.float32)
    tab = jnp.einsum("hrd,dcx->hrcx", rpb.astype(F32), jnp.asarray(onehot),
                     precision=lax.Precision.HIGHEST)
    tab = jnp.where(jnp.asarray(valid)[None, None], tab, -jnp.inf)
    pairs = jnp.concatenate([tab[:, :-1], tab[:, 1:]], axis=-1)
    return pairs.reshape(H // NA_HG, NA_HG * (n_dr - 1), W, 2 * W)


def _neighborhood_attention(qkv, rpb, batch, seq):
    T = qkv.shape[0]
    HW = NA_HG * C_HEADDIM
    n_hg = C_HEADS // NA_HG
    n_rows = seq // GRID_W
    assert n_rows >= NA_ROWS and NA_ROWS % 2 == 0
    table = _na_bias_table(rpb)
    return pl.pallas_call(
        _na_kernel,
        grid=(batch, n_hg),
        in_specs=[pl.BlockSpec((seq, HW), lambda b, g: (b, g)),
                  pl.BlockSpec((seq, HW), lambda b, g: (b, n_hg + g)),
                  pl.BlockSpec((seq, HW), lambda b, g: (b, 2 * n_hg + g)),
                  pl.BlockSpec((1,) + table.shape[1:], lambda b, g: (g, 0, 0, 0))],
        out_specs=pl.BlockSpec((seq, HW), lambda b, g: (b, g)),
        out_shape=jax.ShapeDtypeStruct((T, C_HEADS * C_HEADDIM), BF16),
        compiler_params=_cparams(("parallel", "parallel")),
        name="natten",
    )(qkv, qkv, qkv, table)


def _store_token_tiles(ref, val):
    tm, width = val.shape
    n = width // LANES
    for j in range(n):
        ref[pl.ds(j, tm, stride=n), :] = val[:, j * LANES:(j + 1) * LANES]


def _load_token_tiles(ref, n):
    tm = ref.shape[0] // n
    return jnp.concatenate([ref[pl.ds(j, tm, stride=n), :] for j in range(n)], axis=1)


def _proj_router_kernel(a_ref, w_ref, gmix_ref, x_ref, g_ref, wr_ref, x1_ref, o_ref, h_ref):
    x1 = x_ref[...] + _rms(_dot(a_ref[...], w_ref[...]), gmix_ref[...])
    x1_ref[...] = x1
    h = _rms(x1, g_ref[...])
    _store_token_tiles(h_ref, h)
    h_hi = h.astype(BF16)
    h_lo = (h - h_hi.astype(F32)).astype(BF16)
    both = _dot(h_hi, wr_ref[...])
    logits = both[:, :LANES] + both[:, LANES:] + _dot(h_lo, wr_ref[:, :LANES])
    lane = lax.broadcasted_iota(jnp.int32, logits.shape, 1)
    neg = -jnp.inf
    l1 = jnp.where(lane < N_EXPERTS, logits, neg)
    m1 = jnp.max(l1, axis=-1, keepdims=True)
    i1 = jnp.min(jnp.where(l1 == m1, lane, LANES), axis=-1, keepdims=True)
    l2 = jnp.where(lane == i1, neg, l1)
    m2 = jnp.max(l2, axis=-1, keepdims=True)
    i2 = jnp.min(jnp.where(l2 == m2, lane, LANES), axis=-1, keepdims=True)
    e = jnp.exp(m2 - m1)
    w1 = 1.0 / (1.0 + e)
    w2 = e / (1.0 + e)
    out = jnp.where(lane == 0, i1.astype(F32),
                    jnp.where(lane == 1, i2.astype(F32),
                              jnp.where(lane == 2, w1, jnp.where(lane == 3, w2, 0.0))))
    o_ref[...] = out


def _proj_router(a, w, g_mix, x, g, w_router, tm=512):
    T, D = x.shape
    n = D // LANES
    wr = jnp.pad(w_router.astype(F32), ((0, 0), (0, LANES - w_router.shape[1])))
    wr_hi = wr.astype(BF16)
    wr = jnp.concatenate([wr_hi, (wr - wr_hi.astype(F32)).astype(BF16)], axis=1)
    x1, route, h = pl.pallas_call(
        _proj_router_kernel,
        grid=(T // tm,),
        in_specs=[_row_spec(tm, a.shape[1], 1), _whole_spec(w.shape, 1), _whole_spec((1, D), 1),
                  _row_spec(tm, D, 1), _whole_spec((1, D), 1), _whole_spec((D, 2 * LANES), 1)],
        out_specs=[_row_spec(tm, D, 1), _row_spec(tm, LANES, 1), _row_spec(tm * n, LANES, 1)],
        out_shape=[jax.ShapeDtypeStruct((T, D), F32),
                   jax.ShapeDtypeStruct((T, LANES), F32),
                   jax.ShapeDtypeStruct((T * n, LANES), F32)],
        compiler_params=_cparams(("parallel",)),
        name="proj_router",
    )(a, w, g_mix.reshape(1, D), x, g.reshape(1, D), wr)
    return x1, route, h.reshape(T, n, LANES)


SC_CORES = 2
SC_SUBCORES = 16
SC_WORKERS = SC_CORES * SC_SUBCORES
SC_CHUNK = 32


def _sc_mesh():
    return plsc.VectorSubcoreMesh(core_axis_name="c", subcore_axis_name="s",
                                  num_cores=SC_CORES, num_subcores=SC_SUBCORES)


def _sc_index_blocks(idx):
    return idx.reshape(SC_WORKERS, -1, SC_CHUNK)


def _sc_scatter_tokens(h, dest, n_rows):
    T, n, _ = h.shape
    per_worker = T // SC_WORKERS
    n_chunks = per_worker // SC_CHUNK
    assert per_worker * SC_WORKERS == T and n_chunks * SC_CHUNK == per_worker

    @functools.partial(
        pl.kernel, mesh=_sc_mesh(),
        out_type=jax.ShapeDtypeStruct((n_rows, n, LANES), h.dtype),
        scratch_types=[pltpu.VMEM((n_chunks, SC_CHUNK), jnp.int32),
                       pltpu.VMEM((n_chunks, SC_CHUNK), jnp.int32),
                       pltpu.VMEM((SC_CHUNK, n, LANES), h.dtype)],
        name="sc_scatter_tokens",
    )
    def scatter(h_hbm, d0_hbm, d1_hbm, o_hbm, i0_v, i1_v, rows_v):
        wid = lax.axis_index("s") * SC_CORES + lax.axis_index("c")
        pltpu.sync_copy(d0_hbm.at[wid], i0_v)
        pltpu.sync_copy(d1_hbm.at[wid], i1_v)

        @pl.loop(0, n_chunks)
        def _(j):
            pltpu.sync_copy(h_hbm.at[pl.ds(wid * per_worker + j * SC_CHUNK, SC_CHUNK)], rows_v)
            pltpu.sync_copy(rows_v, o_hbm.at[i0_v.at[j]])
            pltpu.sync_copy(rows_v, o_hbm.at[i1_v.at[j]])

    return scatter(h, _sc_index_blocks(dest[:, 0]), _sc_index_blocks(dest[:, 1]))


def _sc_gather_tokens(y, dest):
    T = dest.shape[0]
    _, n, _ = y.shape
    per_worker = T // SC_WORKERS
    n_chunks = per_worker // SC_CHUNK
    assert per_worker * SC_WORKERS == T and n_chunks * SC_CHUNK == per_worker
    out = jax.ShapeDtypeStruct((T, n, LANES), y.dtype)

    @functools.partial(
        pl.kernel, mesh=_sc_mesh(), out_type=(out, out),
        scratch_types=[pltpu.VMEM((n_chunks, SC_CHUNK), jnp.int32),
                       pltpu.VMEM((n_chunks, SC_CHUNK), jnp.int32),
                       pltpu.VMEM((SC_CHUNK, n, LANES), y.dtype)],
        name="sc_gather_tokens",
    )
    def gather(y_hbm, d0_hbm, d1_hbm, o0_hbm, o1_hbm, i0_v, i1_v, rows_v):
        wid = lax.axis_index("s") * SC_CORES + lax.axis_index("c")
        pltpu.sync_copy(d0_hbm.at[wid], i0_v)
        pltpu.sync_copy(d1_hbm.at[wid], i1_v)

        @pl.loop(0, n_chunks)
        def _(j):
            rows = pl.ds(wid * per_worker + j * SC_CHUNK, SC_CHUNK)
            pltpu.sync_copy(y_hbm.at[i0_v.at[j]], rows_v)
            pltpu.sync_copy(rows_v, o0_hbm.at[rows])
            pltpu.sync_copy(y_hbm.at[i1_v.at[j]], rows_v)
            pltpu.sync_copy(rows_v, o1_hbm.at[rows])

    return gather(y, _sc_index_blocks(dest[:, 0]), _sc_index_blocks(dest[:, 1]))


SC_PACK_PAIRS = 8
SC_LANES = 16


def _sc_pack_bf16_rows(w):
    R, C = w.shape
    pairs = R // 2
    per_worker = pairs // SC_WORKERS
    n_chunks = per_worker // SC_PACK_PAIRS
    assert n_chunks * SC_PACK_PAIRS * SC_WORKERS * 2 == R and C % SC_LANES == 0

    def round_bits(v):
        u = plsc.bitcast(v, jnp.int32)
        return u + 0x7FFF + (lax.shift_right_logical(u, 16) & 1)

    @functools.partial(
        pl.kernel, mesh=_sc_mesh(),
        out_type=jax.ShapeDtypeStruct((pairs, C), jnp.int32),
        scratch_types=[pltpu.VMEM((2 * SC_PACK_PAIRS, C), F32), pltpu.VMEM((SC_PACK_PAIRS, C), jnp.int32)],
        compiler_params=pltpu.CompilerParams(use_tc_tiling_on_sc=True, needs_layout_passes=False),
        name="sc_pack_bf16_rows",
    )
    def pack(w_hbm, o_hbm, in_v, out_v):
        wid = lax.axis_index("s") * SC_CORES + lax.axis_index("c")

        @pl.loop(0, n_chunks)
        def _(j):
            p0 = (wid * n_chunks + j) * SC_PACK_PAIRS
            pltpu.sync_copy(w_hbm.at[pl.ds(2 * p0, 2 * SC_PACK_PAIRS)], in_v)

            @pl.loop(0, C, step=SC_LANES)
            def _(c):
                cols = pl.ds(c, SC_LANES)
                for i in range(SC_PACK_PAIRS):
                    lo = lax.shift_right_logical(round_bits(in_v[2 * i, cols]), 16)
                    hi = round_bits(in_v[2 * i + 1, cols]) & jnp.int32(-65536)
                    out_v[i, cols] = lo | hi

            pltpu.sync_copy(out_v, o_hbm.at[pl.ds(p0, SC_PACK_PAIRS)])

    return pack(w)


def _moe_ffn_kernel(te_ref, na_ref, h_ref, wg_ref, wu_ref, wd_ref, o_ref, acc_scr, *, tf):
    i = pl.program_id(0)
    D = acc_scr.shape[1]
    F = wg_ref.shape[2]
    unpack = lambda packed: pltpu.bitcast(packed, BF16)

    @pl.when(i < na_ref[0])
    def _():
        h = _load_token_tiles(h_ref, D // LANES).astype(BF16)
        for f0 in range(0, F, tf):
            a = (_silu(_dot(h, unpack(wg_ref[0, :, f0:f0 + tf])))
                 * _dot(h, unpack(wu_ref[0, :, f0:f0 + tf]))).astype(BF16)
            part = _dot(a, unpack(wd_ref[0, f0 // 2:(f0 + tf) // 2, :]))
            if f0 == 0:
                acc_scr[...] = part
            else:
                acc_scr[...] += part
        _store_token_tiles(o_ref, acc_scr[...])

    @pl.when(i >= na_ref[0])
    def _():
        o_ref[...] = jnp.zeros_like(o_ref)


def _moe_ffn(hs, wg, wu, wd, tile_expert, n_active, tm=MOE_TM, tf=512):
    rows, n, _ = hs.shape
    E, half_d, F = wg.shape
    D = 2 * half_d
    nt = tile_expert.shape[0]
    assert rows == nt * tm and n * LANES == D and wd.shape == (E, F // 2, D)
    grid_spec = pltpu.PrefetchScalarGridSpec(
        num_scalar_prefetch=2,
        grid=(nt,),
        in_specs=[pl.BlockSpec((tm * n, LANES), lambda i, te, na: (i, 0)),
                  pl.BlockSpec((1, D // 2, F), lambda i, te, na: (te[i], 0, 0)),
                  pl.BlockSpec((1, D // 2, F), lambda i, te, na: (te[i], 0, 0)),
                  pl.BlockSpec((1, F // 2, D), lambda i, te, na: (te[i], 0, 0))],
        out_specs=pl.BlockSpec((tm * n, LANES), lambda i, te, na: (i, 0)),
        scratch_shapes=[pltpu.VMEM((tm, D), F32)],
    )
    y = pl.pallas_call(
        functools.partial(_moe_ffn_kernel, tf=tf),
        grid_spec=grid_spec,
        out_shape=jax.ShapeDtypeStruct((rows * n, LANES), F32),
        compiler_params=_cparams(("arbitrary",)),
        name="moe_ffn",
    )(tile_expert, n_active, hs.reshape(rows * n, LANES), wg, wu, wd)
    return y.reshape(rows, n, LANES)


def _combine_ple_kernel(y0_ref, y1_ref, route_ref, x_ref, g_ref, p_ref, gin_ref, wpg_ref, wpp_ref,
                        gple_ref, o_ref):
    n = x_ref.shape[1] // LANES
    route = route_ref[...]
    moe = route[:, 2:3] * _load_token_tiles(y0_ref, n) + route[:, 3:4] * _load_token_tiles(y1_ref, n)
    x2 = x_ref[...] + _rms(moe, g_ref[...])
    o_ref[...] = _ple_update(x2, p_ref, gin_ref, wpg_ref, wpp_ref, gple_ref)


def _combine_ple(y0, y1, route, x, g, p, g_in, wpg, wpp, g_ple, tm=512):
    T, D = x.shape
    n = D // LANES
    vec = lambda v: v.reshape(1, D)
    return pl.pallas_call(
        _combine_ple_kernel,
        grid=(T // tm,),
        in_specs=[_row_spec(tm * n, LANES, 1), _row_spec(tm * n, LANES, 1), _row_spec(tm, LANES, 1),
                  _row_spec(tm, D, 1), _whole_spec((1, D), 1),
                  _row_spec(tm, p.shape[1], 1), _whole_spec((1, D), 1),
                  _whole_spec(wpg.shape, 1), _whole_spec(wpp.shape, 1), _whole_spec((1, D), 1)],
        out_specs=_row_spec(tm, D, 1),
        out_shape=jax.ShapeDtypeStruct((T, D), F32),
        compiler_params=_cparams(("parallel",)),
        name="combine_ple",
    )(y0.reshape(T * n, LANES), y1.reshape(T * n, LANES), route, x, vec(g), p, vec(g_in), wpg, wpp,
      vec(g_ple))


def _moe_plan(route, tm=MOE_TM):
    T = route.shape[0]
    e = route[:, :TOP_K].astype(jnp.int32).reshape(-1)
    onehot = (e[:, None] == jnp.arange(N_EXPERTS, dtype=jnp.int32)[None, :]).astype(jnp.int32)
    csum = jnp.cumsum(onehot, axis=0)
    rank = jnp.sum(csum * onehot, axis=1) - 1
    counts = csum[-1]
    padded = ((counts + tm - 1) // tm) * tm
    ends = jnp.cumsum(padded)
    starts = ends - padded
    dest = jnp.sum(starts[None, :] * onehot, axis=1) + rank
    nt = (T * TOP_K) // tm + N_EXPERTS
    tile_start = jnp.arange(nt, dtype=jnp.int32) * tm
    tile_expert = jnp.minimum(jnp.sum((tile_start[:, None] >= ends[None, :]).astype(jnp.int32), axis=1),
                              N_EXPERTS - 1).astype(jnp.int32)
    n_active = (ends[-1] // tm).astype(jnp.int32).reshape(1)
    return tile_expert, n_active, dest.reshape(T, TOP_K).astype(jnp.int32)


def kernel(x, p, hgrn_lb_raw, e_norm_mix_pre, e_w_in, e_conv_w, e_conv_b, e_A_log, e_dt_bias, e_D, e_a_norm, e_b_norm, e_w_out, e_norm_mix_post, e_norm_ffn_pre, e_w_ffn_gate, e_w_ffn_up, e_w_ffn_down, e_norm_ffn_post, o_norm_mix_pre, o_w_qkv, o_rpb, o_w_out, o_norm_mix_post, o_norm_ffn_pre, o_w_router, o_w_exp_gate, o_w_exp_up, o_w_exp_down, o_norm_ffn_post, ple_norm_in, ple_w_gate, ple_w_proj, ple_norm_post):
    batch, seq, d_model = x.shape
    depth = p.shape[0]
    T = batch * seq
    xt = x.reshape(T, d_model)
    lb_all = jnp.cumsum(jax.nn.softmax(hgrn_lb_raw.astype(F32), axis=0), axis=0)

    a_kdim = A_HEADS * A_DK
    b_width = B_HEADS * B_HEADDIM
    conv_dim = b_width + 2 * B_GROUPS * B_STATE
    main_w = 5 * a_kdim + b_width + conv_dim

    def pack_experts(w):
        E, R, C = w.shape
        return _sc_pack_bf16_rows(w.reshape(E * R, C)).reshape(E, R // 2, C)

    for li in range(depth):
        j = li // 2
        p_li = p[li].reshape(T, -1)
        ple = (ple_norm_in[li], ple_w_gate[li].astype(BF16), ple_w_proj[li].astype(BF16), ple_norm_post[li])
        if li % 2 == 0:
            w_in = e_w_in[j]
            dtf = w_in[:, main_w:main_w + B_HEADS].reshape(d_model, B_GROUPS, B_HPG)
            dtb = w_in[:, main_w + B_HEADS:].reshape(d_model, B_GROUPS, B_HPG)
            w_dt = jnp.pad(jnp.concatenate([dtf, dtb], axis=2), ((0, 0), (0, 0), (0, LANES - 2 * B_HPG)))
            w_all = jnp.concatenate([w_in[:, :main_w], w_dt.reshape(d_model, B_GROUPS * LANES)], axis=1)
            outs = [(a_kdim, BF16, "silu"), (a_kdim, F32, "log_gate"), (a_kdim, F32, "log_gate"),
                    (a_kdim, BF16, None), (a_kdim, BF16, "silu"),
                    (b_width + conv_dim, BF16, None), (B_GROUPS * LANES, F32, None)]
            q_a, gf_a, gb_a, i_a, gate_a, u_b, dt = _norm_proj(
                xt, e_norm_mix_pre[j], w_all.astype(BF16), outs, lb=lb_all[li])
            o_a = _hgrn_mixer(q_a, gf_a, gb_a, i_a, gate_a, e_a_norm[j], batch, seq)
            o_b = _ssd_mixer(u_b, dt, e_conv_w[j], e_conv_b[j], e_A_log[j], e_dt_bias[j], e_D[j],
                             e_b_norm[j], batch, seq, 0)
            w_out = e_w_out[j].astype(BF16)
            xt = _mix_ffn_ple(o_a, o_b, w_out[:a_kdim], w_out[a_kdim:], e_norm_mix_post[j], xt,
                              e_norm_ffn_pre[j], e_w_ffn_gate[j].astype(BF16), e_w_ffn_up[j].astype(BF16),
                              e_w_ffn_down[j].astype(BF16), e_norm_ffn_post[j], p_li, *ple)
        else:
            (qkv,) = _norm_proj(xt, o_norm_mix_pre[j], o_w_qkv[j].astype(BF16),
                                [(3 * C_HEADS * C_HEADDIM, BF16, None)])
            o_c = _neighborhood_attention(qkv, o_rpb[j], batch, seq)
            xt, route, h = _proj_router(o_c, o_w_out[j].astype(BF16), o_norm_mix_post[j], xt,
                                        o_norm_ffn_pre[j], o_w_router[j])
            tile_expert, n_active, dest = _moe_plan(route)
            hs = _sc_scatter_tokens(h, dest, tile_expert.shape[0] * MOE_TM)
            expert_w = [pack_experts(w[j]) for w in (o_w_exp_gate, o_w_exp_up, o_w_exp_down)]
            ys = _moe_ffn(hs, *expert_w, tile_expert, n_active)
            y0, y1 = _sc_gather_tokens(ys, dest)
            xt = _combine_ple(y0, y1, route, xt, o_norm_ffn_post[j], p_li, *ple)
    return xt.reshape(batch, seq, d_model)
```

```python
import functools
import math

import numpy as np
import jax
import jax.numpy as jnp
from jax import lax
from jax.experimental import pallas as pl
from jax.experimental.pallas import tpu as pltpu
from jax.experimental.pallas import tpu_sc as plsc

F32 = jnp.float32
BF16 = jnp.bfloat16
EPS = 1e-6

LANES = 128
SUBLANES = 8
VMEM_LIMIT_BYTES = 56 * 1024 * 1024

GRID_W = 64
A_HEADS, A_DK, A_CHUNK = 4, 128, 64
A_UNROLL = 8
B_HEADS, B_HEADDIM, B_GROUPS, B_STATE, B_CONV, B_CHUNK = 8, 64, 2, 128, 5, 128
B_HPG = B_HEADS // B_GROUPS
B_GW = B_HPG * B_HEADDIM
C_HEADS, C_HEADDIM = 16, 64
NA_ROWS, NA_COLS = 8, 16
NA_HG = 4
NA_UNROLL = 8
N_EXPERTS, TOP_K = 8, 2
MOE_GROUPS = 2
MOE_TM = 512


def _cparams(sem):
    return pltpu.CompilerParams(dimension_semantics=sem, vmem_limit_bytes=VMEM_LIMIT_BYTES)


def _rms(x, g):
    return x * lax.rsqrt(jnp.mean(x * x, axis=-1, keepdims=True) + EPS) * g


def _silu(x):
    return x * jax.nn.sigmoid(x)


def _dot(a, b):
    return jnp.dot(a, b, preferred_element_type=F32)


def _dot_nt(a, b):
    return lax.dot_general(a, b, (((1,), (1,)), ((), ())), preferred_element_type=F32)


def _dot_tn(a, b):
    return lax.dot_general(a, b, (((0,), (0,)), ((), ())), preferred_element_type=F32)


def _norm_proj_kernel(x_ref, g_ref, w_ref, lb_ref, *o_refs, acts, col_chunk):
    h = _rms(x_ref[...], g_ref[...]).astype(BF16)
    off = 0
    for o_ref, act in zip(o_refs, acts):
        n = o_ref.shape[1]
        for c0 in range(0, n, col_chunk):
            c1 = min(c0 + col_chunk, n)
            y = _dot(h, w_ref[:, off + c0:off + c1])
            if act == "silu":
                y = _silu(y)
            elif act == "log_gate":
                lb = lb_ref[:, c0:c1]
                y = jnp.log(lb + (1.0 - lb) * jax.nn.sigmoid(y))
            o_ref[:, c0:c1] = y.astype(o_ref.dtype)
        off += n


def _norm_proj(x, g, w, outs, lb=None, tm=512, col_chunk=1024):
    T, D = x.shape
    N = w.shape[1]
    widths = [o[0] for o in outs]
    assert sum(widths) == N and T % tm == 0
    if lb is None:
        lb = jnp.zeros((LANES,), F32)
    return pl.pallas_call(
        functools.partial(_norm_proj_kernel, acts=tuple(o[2] for o in outs), col_chunk=col_chunk),
        grid=(T // tm,),
        in_specs=[_row_spec(tm, D, 1), _whole_spec((1, D), 1), _whole_spec((D, N), 1),
                  _whole_spec((1, lb.shape[0]), 1)],
        out_specs=[_row_spec(tm, n, 1) for n in widths],
        out_shape=[jax.ShapeDtypeStruct((T, n), dt) for n, dt, _ in outs],
        compiler_params=_cparams(("parallel",)),
        name="norm_proj",
    )(x, g.reshape(1, D), w, lb.reshape(1, -1))


def _ple_update(x, p_ref, gin_ref, wg_ref, wp_ref, gpost_ref):
    h = _rms(x, gin_ref[...]).astype(BF16)
    gate = jax.nn.sigmoid(_dot(h, wg_ref[...]))
    proj = _dot(p_ref[...].astype(BF16), wp_ref[...])
    return x + _rms(gate * proj, gpost_ref[...])


def _row_spec(tm, width, n_grid):
    return pl.BlockSpec((tm, width), (lambda i: (i, 0)) if n_grid == 1 else (lambda i, j: (i, 0)))


def _whole_spec(shape, n_grid):
    zeros = (0,) * len(shape)
    return pl.BlockSpec(shape, (lambda i: zeros) if n_grid == 1 else (lambda i, j: zeros))


def _mix_ffn_ple_kernel(oa_ref, ob_ref, wa_ref, wb_ref, gmix_ref, x_ref, gpre_ref, wg_ref, wu_ref, wd_ref,
                        gpost_ref, p_ref, gin_ref, wpg_ref, wpp_ref, gple_ref, o_ref,
                        x1_scr, h_scr, acc_scr):
    j = pl.program_id(1)

    @pl.when(j == 0)
    def _():
        mix = _dot(oa_ref[...], wa_ref[...]) + _dot(ob_ref[...], wb_ref[...])
        x1 = x_ref[...] + _rms(mix, gmix_ref[...])
        x1_scr[...] = x1
        h_scr[...] = _rms(x1, gpre_ref[...]).astype(BF16)
        acc_scr[...] = jnp.zeros_like(acc_scr)

    h = h_scr[...]
    a = (_silu(_dot(h, wg_ref[...])) * _dot(h, wu_ref[...])).astype(BF16)
    acc_scr[...] += _dot(a, wd_ref[...])

    @pl.when(j == pl.num_programs(1) - 1)
    def _():
        x2 = x1_scr[...] + _rms(acc_scr[...], gpost_ref[...])
        o_ref[...] = _ple_update(x2, p_ref, gin_ref, wpg_ref, wpp_ref, gple_ref)


def _mix_ffn_ple(o_a, o_b, w_a, w_b, g_mix, x, g_pre, wg, wu, wd, g_post, p, g_in, wpg, wpp, g_ple,
                 tm=512, tf=1408):
    T, D = x.shape
    F = wg.shape[1]
    assert F % tf == 0 and T % tm == 0
    vec = lambda g: g.reshape(1, D)
    return pl.pallas_call(
        _mix_ffn_ple_kernel,
        grid=(T // tm, F // tf),
        in_specs=[_row_spec(tm, o_a.shape[1], 2), _row_spec(tm, o_b.shape[1], 2),
                  _whole_spec(w_a.shape, 2), _whole_spec(w_b.shape, 2), _whole_spec((1, D), 2),
                  _row_spec(tm, D, 2), _whole_spec((1, D), 2),
                  pl.BlockSpec((D, tf), lambda i, j: (0, j)),
                  pl.BlockSpec((D, tf), lambda i, j: (0, j)),
                  pl.BlockSpec((tf, D), lambda i, j: (j, 0)),
                  _whole_spec((1, D), 2),
                  _row_spec(tm, p.shape[1], 2), _whole_spec((1, D), 2),
                  _whole_spec(wpg.shape, 2), _whole_spec(wpp.shape, 2), _whole_spec((1, D), 2)],
        out_specs=_row_spec(tm, D, 2),
        out_shape=jax.ShapeDtypeStruct((T, D), F32),
        scratch_shapes=[pltpu.VMEM((tm, D), F32), pltpu.VMEM((tm, D), BF16), pltpu.VMEM((tm, D), F32)],
        compiler_params=_cparams(("parallel", "arbitrary")),
        name="mix_ffn_ple",
    )(o_a, o_b, w_a, w_b, vec(g_mix), x, vec(g_pre), wg, wu, wd, vec(g_post), p, vec(g_in), wpg, wpp,
      vec(g_ple))


def _roll_rows(x, s, rev):
    n = x.shape[0]
    return pltpu.roll(x, (n - s) if rev else s, 0)


def _cumsum_rows(x, tau, rev):
    n = x.shape[0]
    s = 1
    while s < n:
        if s % SUBLANES:
            shifted = jnp.where(tau >= s, _roll_rows(x, s, rev), 0.0)
        else:
            zeros = jnp.zeros((s,) + x.shape[1:], x.dtype)
            shifted = (jnp.concatenate([x[s:], zeros], axis=0) if rev
                       else jnp.concatenate([zeros, x[:n - s]], axis=0))
        x = x + shifted
        s *= 2
    return x


def _hgrn_levels(C):
    return [C >> (i + 1) for i in range(C.bit_length() - 1)]


def _hgrn_pair_classes(C):
    t = np.arange(C)[:, None]
    s = np.arange(C)[None, :]
    out = np.full((2 * C, 2 * C), -1, np.int32)
    for d, rev in enumerate((False, True)):
        tau, sig = (C - 1 - t, C - 1 - s) if rev else (t, s)
        blk = np.full((C, C), -1, np.int32)
        blk[t == s] = 0
        for i, L in enumerate(_hgrn_levels(C)):
            m = ((t & -(2 * L)) == (s & -(2 * L))) & ((tau & (2 * L - 1)) >= L) & ((sig & (2 * L - 1)) < L)
            blk[m] = i + 1
        out[d * C:(d + 1) * C, d * C:(d + 1) * C] = blk
    return out


def _hgrn_kernel(q_ref, gf_ref, gb_ref, i_ref, gate_ref, cls_ref, ng_ref, o_ref, acc_scr, st_scr):
    S, DK = q_ref.shape
    C = A_CHUNK
    nc = S // C
    assert nc % (2 * A_UNROLL) == 0
    ng = ng_ref[...]
    row = lax.broadcasted_iota(jnp.int32, (C, DK), 0)
    levels = _hgrn_levels(C)
    zero_half = jnp.zeros((C, DK), BF16)

    def stack(top, bottom):
        return jnp.concatenate([top, bottom], axis=0)

    def block_diag(x):
        return stack(jnp.concatenate([x[:C], zero_half], axis=1), jnp.concatenate([zero_half, x[C:]], axis=1))

    def make_decay(rev):
        tau = (C - 1 - row) if rev else row
        odd_rank = (tau & 1) == 1
        last = 0 if rev else C - 1

        def ref_rows(b, L):
            off = L if rev else L - 1
            if 2 * L >= SUBLANES:
                pieces = [jnp.broadcast_to(b[j * 2 * L + off:j * 2 * L + off + 1, :], (2 * L, DK))
                          for j in range(C // (2 * L))]
                return pieces[0] if len(pieces) == 1 else jnp.concatenate(pieces, axis=0)
            if L == 1:
                return jnp.where(odd_rank, _roll_rows(b, 1, rev), b)
            b3 = b.reshape(C // SUBLANES, SUBLANES, DK)
            sub = lax.broadcasted_iota(jnp.int32, b3.shape, 1)
            out = None
            for j in range(SUBLANES // (2 * L)):
                piece = jnp.broadcast_to(b3[:, j * 2 * L + off:j * 2 * L + off + 1, :], b3.shape)
                out = piece if out is None else jnp.where(sub >= j * 2 * L, piece, out)
            return out.reshape(C, DK)

        def decay(g):
            b = _cumsum_rows(g, tau, rev)
            return b, [ref_rows(b, L) for L in levels], b[last:last + 1, :]

        return decay

    decay_fwd, decay_bwd = make_decay(False), make_decay(True)
    pair_cls = cls_ref[...]
    st_scr[...] = jnp.zeros_like(st_scr)

    def chunk_pair(rows_f, rows_b, st):
        q = stack(q_ref[rows_f, :], q_ref[rows_b, :]).astype(F32)
        g_f, g_b = gf_ref[rows_f, :], gb_ref[rows_b, :]
        k = 1.0 - jnp.exp(stack(g_f, g_b))
        vb = stack(i_ref[rows_f, :], i_ref[rows_b, :])
        b_f, refs_f, last_f = decay_fwd(g_f)
        b_b, refs_b, last_b = decay_bwd(g_b)
        b = stack(b_f, b_b)

        o = _dot_nt(block_diag((q * jnp.exp(b)).astype(BF16)), st.astype(BF16))

        attn = jnp.where(pair_cls == 0, _dot_nt(q.astype(BF16), k.astype(BF16)), 0.0)
        for i in range(len(levels)):
            e = jnp.exp(-jnp.abs(b - stack(refs_f[i], refs_b[i])))
            attn = jnp.where(pair_cls == i + 1, _dot_nt((q * e).astype(BF16), (k * e).astype(BF16)), attn)
        o = o + _dot(attn.astype(BF16), vb)

        b_last = stack(jnp.broadcast_to(last_f, (C, DK)), jnp.broadcast_to(last_b, (C, DK)))
        khat = block_diag((k * jnp.exp(b_last - b)).astype(BF16))
        keep = jnp.exp(jnp.concatenate([last_f, last_b], axis=1))
        return o, st * keep + _dot_tn(vb, khat)

    def make_body(final):
        def body(it, carry):
            st = st_scr[...]
            for u in range(A_UNROLL):
                ci = it * A_UNROLL + u
                rows_f = pl.ds(pl.multiple_of(ci * C, C), C)
                rows_b = pl.ds(pl.multiple_of((nc - 1 - ci) * C, C), C)
                o, st = chunk_pair(rows_f, rows_b, st)
                for rows, part in ((rows_f, o[:C]), (rows_b, o[C:])):
                    if final:
                        tot = acc_scr[rows, :] + part
                        o_ref[rows, :] = (_rms(tot, ng) * gate_ref[rows, :].astype(F32)).astype(o_ref.dtype)
                    else:
                        acc_scr[rows, :] = part
            st_scr[...] = st
            return carry

        return body

    trips = nc // A_UNROLL
    lax.fori_loop(0, trips // 2, make_body(False), 0)
    lax.fori_loop(trips // 2, trips, make_body(True), 0)


def _hgrn_mixer(q, g_fwd, g_bwd, v, gate, norm_g, batch, seq):
    T = q.shape[0]
    H, DK = A_HEADS, A_DK
    head = pl.BlockSpec((seq, DK), lambda b, h: (b, h))
    classes = jnp.asarray(_hgrn_pair_classes(A_CHUNK))
    return pl.pallas_call(
        _hgrn_kernel,
        grid=(batch, H),
        in_specs=[head, head, head, head, head,
                  pl.BlockSpec(classes.shape, lambda b, h: (0, 0)),
                  pl.BlockSpec((1, DK), lambda b, h: (0, 0))],
        out_specs=pl.BlockSpec((seq, DK), lambda b, h: (b, h)),
        out_shape=jax.ShapeDtypeStruct((T, H * DK), BF16),
        scratch_shapes=[pltpu.VMEM((seq, DK), F32), pltpu.VMEM((DK, 2 * DK), F32)],
        compiler_params=_cparams(("parallel", "parallel")),
        name="hgrn2",
    )(q, g_fwd, g_bwd, v, gate, classes, norm_g.reshape(1, DK))


def _expand_heads(col, j0, width):
    q = col.shape[0]
    lane = lax.broadcasted_iota(jnp.int32, (q, width), 1)
    out = jnp.broadcast_to(col[:, j0 + B_HPG - 1:j0 + B_HPG], (q, width))
    for j in range(B_HPG - 2, -1, -1):
        out = jnp.where(lane < (j + 1) * B_HEADDIM,
                        jnp.broadcast_to(col[:, j0 + j:j0 + j + 1], (q, width)), out)
    return out


def _ssd_kernel(z_ref, x_ref, bm_ref, cm_ref, dt_ref, cwx_ref, cwb_ref, cwc_ref,
                cbx_ref, cbb_ref, cbc_ref, hp_ref, dsk_ref, ng_ref, o_ref,
                xs_scr, b_scr, c_scr, y_scr, *st_scrs):
    S = x_ref.shape[0]
    Q = B_CHUNK
    nc = S // Q
    GW = B_GW
    hp = hp_ref[0]
    a_row, dtb_row = hp[0:1, :], hp[1:2, :]
    row = lax.broadcasted_iota(jnp.int32, (Q, LANES), 0)
    t2 = lax.broadcasted_iota(jnp.int32, (Q, Q), 0)
    s2 = lax.broadcasted_iota(jnp.int32, (Q, Q), 1)
    lane_gw = lax.broadcasted_iota(jnp.int32, (Q, GW), 1)
    halo = SUBLANES

    def conv_body(c, carry):
        r0 = pl.multiple_of(c * Q, Q)
        rp = pl.multiple_of(jnp.maximum(r0 - halo, 0), halo)
        rn = pl.multiple_of(jnp.minimum(r0 + Q, S - halo), halo)
        has_prev = c > 0
        has_next = c < nc - 1

        def conv(src_ref, w_ref, bias_ref):
            prev = jnp.where(has_prev, src_ref[pl.ds(rp, halo), :].astype(F32), 0.0)
            nxt = jnp.where(has_next, src_ref[pl.ds(rn, halo), :].astype(F32), 0.0)
            xx = jnp.concatenate([prev, src_ref[pl.ds(r0, Q), :].astype(F32), nxt], axis=0)
            n = Q + 2 * halo
            w = w_ref[0]
            acc = None
            for j in range(B_CONV):
                delta = j - B_CONV // 2
                sh = xx if delta == 0 else pltpu.roll(xx, (-delta) % n, 0)
                term = sh[halo:halo + Q, :] * w[j:j + 1, :]
                acc = term if acc is None else acc + term
            return _silu(acc + bias_ref[0])

        xs_scr[pl.ds(r0, Q), :] = conv(x_ref, cwx_ref, cbx_ref)
        b_scr[pl.ds(r0, Q), :] = conv(bm_ref, cwb_ref, cbb_ref).astype(BF16)
        c_scr[pl.ds(r0, Q), :] = conv(cm_ref, cwc_ref, cbc_ref).astype(BF16)
        return carry

    lax.fori_loop(0, nc, conv_body, 0)

    def make_chunk(rev):
        tau = (Q - 1 - row) if rev else row
        last = 0 if rev else Q - 1
        j0 = B_HPG if rev else 0
        pair_ok = (s2 >= t2) if rev else (t2 >= s2)

        def chunk(rows, st):
            dt = jax.nn.softplus(dt_ref[rows, :] + dtb_row)
            cs = _cumsum_rows(dt * a_row, tau, rev)
            cs_t = cs.T
            xs = xs_scr[rows, :]
            bm = b_scr[rows, :]
            cm = c_scr[rows, :]
            xdt = xs * _expand_heads(dt, j0, GW)
            ecs = _expand_heads(cs, j0, GW)
            ecs_last = ecs[last:last + 1, :]

            y = _dot(cm, st.astype(BF16)) * jnp.exp(ecs)

            scores = _dot_nt(cm, bm)
            for j in range(B_HPG):
                col = jnp.broadcast_to(cs[:, j0 + j:j0 + j + 1], (Q, Q))
                rw = jnp.broadcast_to(cs_t[j0 + j:j0 + j + 1, :], (Q, Q))
                decay = jnp.where(pair_ok, jnp.exp(jnp.minimum(col - rw, 0.0)), 0.0)
                in_head = (lane_gw >= j * B_HEADDIM) & (lane_gw < (j + 1) * B_HEADDIM)
                xh = jnp.where(in_head, xdt, 0.0).astype(BF16)
                y = y + _dot((scores * decay).astype(BF16), xh)

            xdec = (xdt * jnp.exp(ecs_last - ecs)).astype(BF16)
            return y, st * jnp.exp(ecs_last) + _dot_tn(bm, xdec)

        return chunk

    chunk_fns = (make_chunk(False), make_chunk(True))
    for st_scr in st_scrs:
        st_scr[...] = jnp.zeros_like(st_scr)

    def make_body(final):
        def body(ci, carry):
            for rev, chunk, st_scr in zip((False, True), chunk_fns, st_scrs):
                c = (nc - 1 - ci) if rev else ci
                rows = pl.ds(pl.multiple_of(c * Q, Q), Q)
                y, st = chunk(rows, st_scr[...])
                st_scr[...] = st
                if final:
                    tot = y_scr[rows, :] + y + dsk_ref[0] * xs_scr[rows, :]
                    tot = tot * _silu(z_ref[rows, :].astype(F32))
                    o_ref[rows, :] = _rms(tot, ng_ref[0]).astype(o_ref.dtype)
                else:
                    y_scr[rows, :] = y
            return carry

        return body

    assert nc % 2 == 0
    lax.fori_loop(0, nc // 2, make_body(False), 0)
    lax.fori_loop(nc // 2, nc, make_body(True), 0)


def _ssd_mixer(u, dt, conv_w, conv_b, a_log, dt_bias, d_skip, norm_g, batch, seq, col0):
    T = u.shape[0]
    G, GW, N = B_GROUPS, B_GW, B_STATE
    W = B_HEADS * B_HEADDIM
    assert col0 % GW == 0 and GW == 2 * N
    z_blk = col0 // GW
    x_blk = z_blk + W // GW
    b_blk = (col0 + 2 * W) // N
    c_blk = b_blk + G

    def pad_rows(w):
        return jnp.pad(w, ((0, SUBLANES - w.shape[0]), (0, 0)))

    cwx = pad_rows(conv_w[:, :W]).reshape(SUBLANES, G, GW).transpose(1, 0, 2)
    cwb = pad_rows(conv_w[:, W:W + G * N]).reshape(SUBLANES, G, N).transpose(1, 0, 2)
    cwc = pad_rows(conv_w[:, W + G * N:]).reshape(SUBLANES, G, N).transpose(1, 0, 2)
    cbx = conv_b[:W].reshape(G, 1, GW)
    cbb = conv_b[W:W + G * N].reshape(G, 1, N)
    cbc = conv_b[W + G * N:].reshape(G, 1, N)
    a_neg = -jnp.exp(a_log.astype(F32))
    per_dir = lambda v: jnp.concatenate([v[0].reshape(G, B_HPG), v[1].reshape(G, B_HPG)], axis=1)
    hp = jnp.stack([per_dir(a_neg), per_dir(dt_bias.astype(F32))], axis=1)
    hp = jnp.pad(hp, ((0, 0), (0, SUBLANES - 2), (0, LANES - 2 * B_HPG)))
    dsk = jnp.repeat(d_skip.astype(F32), B_HEADDIM).reshape(G, 1, GW)
    ng = norm_g.reshape(G, 1, GW)

    gspec = lambda shape: pl.BlockSpec((1,) + shape, lambda b, g: (g, 0, 0))
    return pl.pallas_call(
        _ssd_kernel,
        grid=(batch, G),
        in_specs=[pl.BlockSpec((seq, GW), lambda b, g: (b, z_blk + g)),
                  pl.BlockSpec((seq, GW), lambda b, g: (b, x_blk + g)),
                  pl.BlockSpec((seq, N), lambda b, g: (b, b_blk + g)),
                  pl.BlockSpec((seq, N), lambda b, g: (b, c_blk + g)),
                  pl.BlockSpec((seq, LANES), lambda b, g: (b, g)),
                  gspec((SUBLANES, GW)), gspec((SUBLANES, N)), gspec((SUBLANES, N)),
                  gspec((1, GW)), gspec((1, N)), gspec((1, N)),
                  gspec((SUBLANES, LANES)), gspec((1, GW)), gspec((1, GW))],
        out_specs=pl.BlockSpec((seq, GW), lambda b, g: (b, g)),
        out_shape=jax.ShapeDtypeStruct((T, W), BF16),
        scratch_shapes=[pltpu.VMEM((seq, GW), F32), pltpu.VMEM((seq, N), BF16),
                        pltpu.VMEM((seq, N), BF16), pltpu.VMEM((seq, GW), F32),
                        pltpu.VMEM((N, GW), F32), pltpu.VMEM((N, GW), F32)],
        compiler_params=_cparams(("parallel", "parallel")),
        name="ssd",
    )(u, u, u, u, dt, cwx, cwb, cwc, cbx, cbb, cbc, hp, dsk, ng)


def _na_kernel(q_ref, k_ref, v_ref, bias_ref, o_ref):
    S, HW = q_ref.shape
    W = GRID_W
    n_rows = S // W
    kh = min(NA_ROWS, n_rows)
    nk = kh * W
    n_pairs = bias_ref.shape[1] // NA_HG
    lane = lax.broadcasted_iota(jnp.int32, (W, HW), 1)
    head_masks = [(lane >= h * C_HEADDIM) & (lane < (h + 1) * C_HEADDIM) for h in range(NA_HG)]
    scale = C_HEADDIM ** -0.5

    def body(r, carry):
        rs = jnp.clip(r - kh // 2, 0, n_rows - kh)
        q = q_ref[pl.ds(pl.multiple_of(r * W, W), W), :] * jnp.asarray(scale, q_ref.dtype)
        kw = k_ref[pl.ds(pl.multiple_of(rs * W, W), nk), :]
        vw = v_ref[pl.ds(pl.multiple_of(rs * W, W), nk), :]
        zero = jnp.zeros_like(q)
        qs = jnp.concatenate([jnp.where(m, q, zero) for m in head_masks], axis=0)
        d0 = (NA_ROWS - 1) - (r - rs)
        bias = jnp.concatenate(
            [jnp.concatenate([bias_ref[0, h * n_pairs + d0 + 2 * m] for m in range(kh // 2)], axis=1)
             for h in range(NA_HG)], axis=0)
        s = _dot_nt(qs, kw) + bias
        m = jnp.max(s, axis=-1, keepdims=True)
        p = jnp.exp(s - m)
        l = jnp.sum(p, axis=-1, keepdims=True)
        res = _dot(p.astype(BF16), vw) / l
        out = jnp.zeros((W, HW), F32)
        for h in range(NA_HG):
            out = jnp.where(head_masks[h], res[h * W:(h + 1) * W, :], out)
        o_ref[pl.ds(pl.multiple_of(r * W, W), W), :] = out.astype(o_ref.dtype)
        return carry

    lax.fori_loop(0, n_rows, body, 0, unroll=NA_UNROLL)


def _na_bias_table(rpb):
    H, n_dr, n_dc = rpb.shape
    W = GRID_W
    c = np.arange(W)
    qs = np.clip(c - NA_COLS // 2, 0, W - NA_COLS)
    valid = (c[None, :] >= qs[:, None]) & (c[None, :] < qs[:, None] + NA_COLS)
    dc = np.clip(c[None, :] - c[:, None] + (NA_COLS - 1), 0, n_dc - 1)
    onehot = (dc[None] == np.arange(n_dc)[:, None, None]).astype(np.float32)
    tab = jnp.einsum("hrd,dcx->hrcx", rpb.astype(F32), jnp.asarray(onehot),
                     precision=lax.Precision.HIGHEST)
    tab = jnp.where(jnp.asarray(valid)[None, None], tab, -jnp.inf)
    pairs = jnp.concatenate([tab[:, :-1], tab[:, 1:]], axis=-1)
    return pairs.reshape(H // NA_HG, NA_HG * (n_dr - 1), W, 2 * W)


def _neighborhood_attention(qkv, rpb, batch, seq):
    T = qkv.shape[0]
    HW = NA_HG * C_HEADDIM
    n_hg = C_HEADS // NA_HG
    n_rows = seq // GRID_W
    assert n_rows >= NA_ROWS and NA_ROWS % 2 == 0
    table = _na_bias_table(rpb)
    return pl.pallas_call(
        _na_kernel,
        grid=(batch, n_hg),
        in_specs=[pl.BlockSpec((seq, HW), lambda b, g: (b, g)),
                  pl.BlockSpec((seq, HW), lambda b, g: (b, n_hg + g)),
                  pl.BlockSpec((seq, HW), lambda b, g: (b, 2 * n_hg + g)),
                  pl.BlockSpec((1,) + table.shape[1:], lambda b, g: (g, 0, 0, 0))],
        out_specs=pl.BlockSpec((seq, HW), lambda b, g: (b, g)),
        out_shape=jax.ShapeDtypeStruct((T, C_HEADS * C_HEADDIM), BF16),
        compiler_params=_cparams(("parallel", "parallel")),
        name="natten",
    )(qkv, qkv, qkv, table)


def _store_token_tiles(ref, val):
    tm, width = val.shape
    n = width // LANES
    for j in range(n):
        ref[pl.ds(j, tm, stride=n), :] = val[:, j * LANES:(j + 1) * LANES]


def _load_token_tiles(ref, n):
    tm = ref.shape[0] // n
    return jnp.concatenate([ref[pl.ds(j, tm, stride=n), :] for j in range(n)], axis=1)


def _proj_router_kernel(a_ref, w_ref, gmix_ref, x_ref, g_ref, wr_ref, x1_ref, o_ref, h_ref):
    x1 = x_ref[...] + _rms(_dot(a_ref[...], w_ref[...]), gmix_ref[...])
    x1_ref[...] = x1
    h = _rms(x1, g_ref[...])
    _store_token_tiles(h_ref, h)
    h_hi = h.astype(BF16)
    h_lo = (h - h_hi.astype(F32)).astype(BF16)
    both = _dot(h_hi, wr_ref[...])
    logits = both[:, :LANES] + both[:, LANES:] + _dot(h_lo, wr_ref[:, :LANES])
    lane = lax.broadcasted_iota(jnp.int32, logits.shape, 1)
    neg = -jnp.inf
    l1 = jnp.where(lane < N_EXPERTS, logits, neg)
    m1 = jnp.max(l1, axis=-1, keepdims=True)
    i1 = jnp.min(jnp.where(l1 == m1, lane, LANES), axis=-1, keepdims=True)
    l2 = jnp.where(lane == i1, neg, l1)
    m2 = jnp.max(l2, axis=-1, keepdims=True)
    i2 = jnp.min(jnp.where(l2 == m2, lane, LANES), axis=-1, keepdims=True)
    e = jnp.exp(m2 - m1)
    w1 = 1.0 / (1.0 + e)
    w2 = e / (1.0 + e)
    out = jnp.where(lane == 0, i1.astype(F32),
                    jnp.where(lane == 1, i2.astype(F32),
                              jnp.where(lane == 2, w1, jnp.where(lane == 3, w2, 0.0))))
    o_ref[...] = out


def _proj_router(a, w, g_mix, x, g, w_router, tm=512):
    T, D = x.shape
    n = D // LANES
    wr = jnp.pad(w_router.astype(F32), ((0, 0), (0, LANES - w_router.shape[1])))
    wr_hi = wr.astype(BF16)
    wr = jnp.concatenate([wr_hi, (wr - wr_hi.astype(F32)).astype(BF16)], axis=1)
    x1, route, h = pl.pallas_call(
        _proj_router_kernel,
        grid=(T // tm,),
        in_specs=[_row_spec(tm, a.shape[1], 1), _whole_spec(w.shape, 1), _whole_spec((1, D), 1),
                  _row_spec(tm, D, 1), _whole_spec((1, D), 1), _whole_spec((D, 2 * LANES), 1)],
        out_specs=[_row_spec(tm, D, 1), _row_spec(tm, LANES, 1), _row_spec(tm * n, LANES, 1)],
        out_shape=[jax.ShapeDtypeStruct((T, D), F32),
                   jax.ShapeDtypeStruct((T, LANES), F32),
                   jax.ShapeDtypeStruct((T * n, LANES), F32)],
        compiler_params=_cparams(("parallel",)),
        name="proj_router",
    )(a, w, g_mix.reshape(1, D), x, g.reshape(1, D), wr)
    return x1, route, h.reshape(T, n, LANES)


SC_CORES = 2
SC_SUBCORES = 16
SC_WORKERS = SC_CORES * SC_SUBCORES
SC_CHUNK = 32


def _sc_mesh():
    return plsc.VectorSubcoreMesh(core_axis_name="c", subcore_axis_name="s",
                                  num_cores=SC_CORES, num_subcores=SC_SUBCORES)


def _sc_index_blocks(idx):
    return idx.reshape(SC_WORKERS, -1, SC_CHUNK)


def _sc_scatter_tokens(h, dest, n_rows, row0=0):
    T = dest.shape[0]
    _, n, _ = h.shape
    per_worker = T // SC_WORKERS
    n_chunks = per_worker // SC_CHUNK
    assert per_worker * SC_WORKERS == T and n_chunks * SC_CHUNK == per_worker

    @functools.partial(
        pl.kernel, mesh=_sc_mesh(),
        out_type=jax.ShapeDtypeStruct((n_rows, n, LANES), h.dtype),
        scratch_types=[pltpu.VMEM((n_chunks, SC_CHUNK), jnp.int32),
                       pltpu.VMEM((n_chunks, SC_CHUNK), jnp.int32),
                       pltpu.VMEM((SC_CHUNK, n, LANES), h.dtype)],
        name="sc_scatter_tokens",
    )
    def scatter(h_hbm, d0_hbm, d1_hbm, o_hbm, i0_v, i1_v, rows_v):
        wid = lax.axis_index("s") * SC_CORES + lax.axis_index("c")
        pltpu.sync_copy(d0_hbm.at[wid], i0_v)
        pltpu.sync_copy(d1_hbm.at[wid], i1_v)

        @pl.loop(0, n_chunks)
        def _(j):
            pltpu.sync_copy(h_hbm.at[pl.ds(row0 + wid * per_worker + j * SC_CHUNK, SC_CHUNK)], rows_v)
            pltpu.sync_copy(rows_v, o_hbm.at[i0_v.at[j]])
            pltpu.sync_copy(rows_v, o_hbm.at[i1_v.at[j]])

    return scatter(h, _sc_index_blocks(dest[:, 0]), _sc_index_blocks(dest[:, 1]))


def _sc_gather_tokens(y, dest):
    T = dest.shape[0]
    _, n, _ = y.shape
    per_worker = T // SC_WORKERS
    n_chunks = per_worker // SC_CHUNK
    assert per_worker * SC_WORKERS == T and n_chunks * SC_CHUNK == per_worker
    out = jax.ShapeDtypeStruct((T, n, LANES), y.dtype)

    @functools.partial(
        pl.kernel, mesh=_sc_mesh(), out_type=(out, out),
        scratch_types=[pltpu.VMEM((n_chunks, SC_CHUNK), jnp.int32),
                       pltpu.VMEM((n_chunks, SC_CHUNK), jnp.int32),
                       pltpu.VMEM((SC_CHUNK, n, LANES), y.dtype)],
        name="sc_gather_tokens",
    )
    def gather(y_hbm, d0_hbm, d1_hbm, o0_hbm, o1_hbm, i0_v, i1_v, rows_v):
        wid = lax.axis_index("s") * SC_CORES + lax.axis_index("c")
        pltpu.sync_copy(d0_hbm.at[wid], i0_v)
        pltpu.sync_copy(d1_hbm.at[wid], i1_v)

        @pl.loop(0, n_chunks)
        def _(j):
            rows = pl.ds(wid * per_worker + j * SC_CHUNK, SC_CHUNK)
            pltpu.sync_copy(y_hbm.at[i0_v.at[j]], rows_v)
            pltpu.sync_copy(rows_v, o0_hbm.at[rows])
            pltpu.sync_copy(y_hbm.at[i1_v.at[j]], rows_v)
            pltpu.sync_copy(rows_v, o1_hbm.at[rows])

    return gather(y, _sc_index_blocks(dest[:, 0]), _sc_index_blocks(dest[:, 1]))


SC_PACK_PAIRS = 8
SC_LANES = 16


def _sc_pack_bf16_rows(w):
    R, C = w.shape
    pairs = R // 2
    per_worker = pairs // SC_WORKERS
    n_chunks = per_worker // SC_PACK_PAIRS
    assert n_chunks * SC_PACK_PAIRS * SC_WORKERS * 2 == R and C % SC_LANES == 0

    def round_bits(v):
        u = plsc.bitcast(v, jnp.int32)
        return u + 0x7FFF + (lax.shift_right_logical(u, 16) & 1)

    @functools.partial(
        pl.kernel, mesh=_sc_mesh(),
        out_type=jax.ShapeDtypeStruct((pairs, C), jnp.int32),
        scratch_types=[pltpu.VMEM((2 * SC_PACK_PAIRS, C), F32), pltpu.VMEM((SC_PACK_PAIRS, C), jnp.int32)],
        compiler_params=pltpu.CompilerParams(use_tc_tiling_on_sc=True, needs_layout_passes=False),
        name="sc_pack_bf16_rows",
    )
    def pack(w_hbm, o_hbm, in_v, out_v):
        wid = lax.axis_index("s") * SC_CORES + lax.axis_index("c")

        @pl.loop(0, n_chunks)
        def _(j):
            p0 = (wid * n_chunks + j) * SC_PACK_PAIRS
            pltpu.sync_copy(w_hbm.at[pl.ds(2 * p0, 2 * SC_PACK_PAIRS)], in_v)

            @pl.loop(0, C, step=SC_LANES)
            def _(c):
                cols = pl.ds(c, SC_LANES)
                for i in range(SC_PACK_PAIRS):
                    lo = lax.shift_right_logical(round_bits(in_v[2 * i, cols]), 16)
                    hi = round_bits(in_v[2 * i + 1, cols]) & jnp.int32(-65536)
                    out_v[i, cols] = lo | hi

            pltpu.sync_copy(out_v, o_hbm.at[pl.ds(p0, SC_PACK_PAIRS)])

    return pack(w)


def _moe_ffn_kernel(te_ref, na_ref, h_ref, wg_ref, wu_ref, wd_ref, o_ref, acc_scr, *, tf):
    i = pl.program_id(0)
    D = acc_scr.shape[1]
    F = wg_ref.shape[2]
    unpack = lambda packed: pltpu.bitcast(packed, BF16)

    @pl.when(i < na_ref[0])
    def _():
        h = _load_token_tiles(h_ref, D // LANES).astype(BF16)
        for f0 in range(0, F, tf):
            a = (_silu(_dot(h, unpack(wg_ref[0, :, f0:f0 + tf])))
                 * _dot(h, unpack(wu_ref[0, :, f0:f0 + tf]))).astype(BF16)
            part = _dot(a, unpack(wd_ref[0, f0 // 2:(f0 + tf) // 2, :]))
            if f0 == 0:
                acc_scr[...] = part
            else:
                acc_scr[...] += part
        _store_token_tiles(o_ref, acc_scr[...])

    @pl.when(i >= na_ref[0])
    def _():
        o_ref[...] = jnp.zeros_like(o_ref)


def _moe_ffn(hs, wg, wu, wd, tile_expert, n_active, tm=MOE_TM, tf=512):
    rows, n, _ = hs.shape
    E, half_d, F = wg.shape
    D = 2 * half_d
    nt = tile_expert.shape[0]
    assert rows == nt * tm and n * LANES == D and wd.shape == (E, F // 2, D)
    grid_spec = pltpu.PrefetchScalarGridSpec(
        num_scalar_prefetch=2,
        grid=(nt,),
        in_specs=[pl.BlockSpec((tm * n, LANES), lambda i, te, na: (i, 0)),
                  pl.BlockSpec((1, D // 2, F), lambda i, te, na: (te[i], 0, 0)),
                  pl.BlockSpec((1, D // 2, F), lambda i, te, na: (te[i], 0, 0)),
                  pl.BlockSpec((1, F // 2, D), lambda i, te, na: (te[i], 0, 0))],
        out_specs=pl.BlockSpec((tm * n, LANES), lambda i, te, na: (i, 0)),
        scratch_shapes=[pltpu.VMEM((tm, D), F32)],
    )
    y = pl.pallas_call(
        functools.partial(_moe_ffn_kernel, tf=tf),
        grid_spec=grid_spec,
        out_shape=jax.ShapeDtypeStruct((rows * n, LANES), F32),
        compiler_params=_cparams(("arbitrary",)),
        name="moe_ffn",
    )(tile_expert, n_active, hs.reshape(rows * n, LANES), wg, wu, wd)
    return y.reshape(rows, n, LANES)


def _combine_ple_kernel(y0_ref, y1_ref, route_ref, x_ref, g_ref, p_ref, gin_ref, wpg_ref, wpp_ref,
                        gple_ref, *rest):
    o_ref = rest[-1]
    n = x_ref.shape[1] // LANES
    route = route_ref[...]
    moe = route[:, 2:3] * _load_token_tiles(y0_ref, n) + route[:, 3:4] * _load_token_tiles(y1_ref, n)
    x2 = x_ref[...] + _rms(moe, g_ref[...])
    o_ref[...] = _ple_update(x2, p_ref, gin_ref, wpg_ref, wpp_ref, gple_ref)


def _combine_ple(y0, y1, route, x, g, p, g_in, wpg, wpp, g_ple, row0=0, prev=None, tm=512):
    T, D = x.shape
    Tp, n, _ = y0.shape
    assert row0 % tm == 0 and Tp % tm == 0 and n * LANES == D
    blk0 = row0 // tm
    vec = lambda v: v.reshape(1, D)
    shifted = lambda width: pl.BlockSpec((tm, width), lambda i: (i + blk0, 0))
    in_specs = [_row_spec(tm * n, LANES, 1), _row_spec(tm * n, LANES, 1), shifted(LANES),
                shifted(D), _whole_spec((1, D), 1),
                shifted(p.shape[1]), _whole_spec((1, D), 1),
                _whole_spec(wpg.shape, 1), _whole_spec(wpp.shape, 1), _whole_spec((1, D), 1)]
    args = [y0.reshape(Tp * n, LANES), y1.reshape(Tp * n, LANES), route, x, vec(g), p, vec(g_in), wpg, wpp,
            vec(g_ple)]
    aliases = {}
    if prev is not None:
        in_specs.append(pl.BlockSpec(memory_space=pl.ANY))
        args.append(prev)
        aliases = {len(args) - 1: 0}
    return pl.pallas_call(
        _combine_ple_kernel,
        grid=(Tp // tm,),
        in_specs=in_specs,
        out_specs=shifted(D),
        out_shape=jax.ShapeDtypeStruct((T, D), F32),
        input_output_aliases=aliases,
        compiler_params=_cparams(("parallel",)),
        name="combine_ple",
    )(*args)


def _moe_plan(route, tm=MOE_TM):
    T = route.shape[0]
    e = route[:, :TOP_K].astype(jnp.int32).reshape(-1)
    onehot = (e[:, None] == jnp.arange(N_EXPERTS, dtype=jnp.int32)[None, :]).astype(jnp.int32)
    csum = jnp.cumsum(onehot, axis=0)
    rank = jnp.sum(csum * onehot, axis=1) - 1
    counts = csum[-1]
    padded = ((counts + tm - 1) // tm) * tm
    ends = jnp.cumsum(padded)
    starts = ends - padded
    dest = jnp.sum(starts[None, :] * onehot, axis=1) + rank
    nt = (T * TOP_K) // tm + N_EXPERTS
    tile_start = jnp.arange(nt, dtype=jnp.int32) * tm
    tile_expert = jnp.minimum(jnp.sum((tile_start[:, None] >= ends[None, :]).astype(jnp.int32), axis=1),
                              N_EXPERTS - 1).astype(jnp.int32)
    n_active = (ends[-1] // tm).astype(jnp.int32).reshape(1)
    return tile_expert, n_active, dest.reshape(T, TOP_K).astype(jnp.int32)


def kernel(x, p, hgrn_lb_raw, e_norm_mix_pre, e_w_in, e_conv_w, e_conv_b, e_A_log, e_dt_bias, e_D, e_a_norm, e_b_norm, e_w_out, e_norm_mix_post, e_norm_ffn_pre, e_w_ffn_gate, e_w_ffn_up, e_w_ffn_down, e_norm_ffn_post, o_norm_mix_pre, o_w_qkv, o_rpb, o_w_out, o_norm_mix_post, o_norm_ffn_pre, o_w_router, o_w_exp_gate, o_w_exp_up, o_w_exp_down, o_norm_ffn_post, ple_norm_in, ple_w_gate, ple_w_proj, ple_norm_post):
    batch, seq, d_model = x.shape
    depth = p.shape[0]
    T = batch * seq
    xt = x.reshape(T, d_model)
    lb_all = jnp.cumsum(jax.nn.softmax(hgrn_lb_raw.astype(F32), axis=0), axis=0)

    a_kdim = A_HEADS * A_DK
    b_width = B_HEADS * B_HEADDIM
    conv_dim = b_width + 2 * B_GROUPS * B_STATE
    main_w = 5 * a_kdim + b_width + conv_dim

    def pack_experts(w):
        E, R, C = w.shape
        return _sc_pack_bf16_rows(w.reshape(E * R, C)).reshape(E, R // 2, C)

    for li in range(depth):
        j = li // 2
        p_li = p[li].reshape(T, -1)
        ple = (ple_norm_in[li], ple_w_gate[li].astype(BF16), ple_w_proj[li].astype(BF16), ple_norm_post[li])
        if li % 2 == 0:
            w_in = e_w_in[j]
            dtf = w_in[:, main_w:main_w + B_HEADS].reshape(d_model, B_GROUPS, B_HPG)
            dtb = w_in[:, main_w + B_HEADS:].reshape(d_model, B_GROUPS, B_HPG)
            w_dt = jnp.pad(jnp.concatenate([dtf, dtb], axis=2), ((0, 0), (0, 0), (0, LANES - 2 * B_HPG)))
            w_all = jnp.concatenate([w_in[:, :main_w], w_dt.reshape(d_model, B_GROUPS * LANES)], axis=1)
            outs = [(a_kdim, BF16, "silu"), (a_kdim, F32, "log_gate"), (a_kdim, F32, "log_gate"),
                    (a_kdim, BF16, None), (a_kdim, BF16, "silu"),
                    (b_width + conv_dim, BF16, None), (B_GROUPS * LANES, F32, None)]
            q_a, gf_a, gb_a, i_a, gate_a, u_b, dt = _norm_proj(
                xt, e_norm_mix_pre[j], w_all.astype(BF16), outs, lb=lb_all[li])
            o_a = _hgrn_mixer(q_a, gf_a, gb_a, i_a, gate_a, e_a_norm[j], batch, seq)
            o_b = _ssd_mixer(u_b, dt, e_conv_w[j], e_conv_b[j], e_A_log[j], e_dt_bias[j], e_D[j],
                             e_b_norm[j], batch, seq, 0)
            w_out = e_w_out[j].astype(BF16)
            xt = _mix_ffn_ple(o_a, o_b, w_out[:a_kdim], w_out[a_kdim:], e_norm_mix_post[j], xt,
                              e_norm_ffn_pre[j], e_w_ffn_gate[j].astype(BF16), e_w_ffn_up[j].astype(BF16),
                              e_w_ffn_down[j].astype(BF16), e_norm_ffn_post[j], p_li, *ple)
        else:
            (qkv,) = _norm_proj(xt, o_norm_mix_pre[j], o_w_qkv[j].astype(BF16),
                                [(3 * C_HEADS * C_HEADDIM, BF16, None)])
            o_c = _neighborhood_attention(qkv, o_rpb[j], batch, seq)
            xt, route, h = _proj_router(o_c, o_w_out[j].astype(BF16), o_norm_mix_post[j], xt,
                                        o_norm_ffn_pre[j], o_w_router[j])
            expert_w = [pack_experts(w[j]) for w in (o_w_exp_gate, o_w_exp_up, o_w_exp_down)]
            x_mid, xt = xt, None
            t_grp = T // MOE_GROUPS
            for s in range(MOE_GROUPS):
                tile_expert, n_active, dest = _moe_plan(route[s * t_grp:(s + 1) * t_grp])
                hs = _sc_scatter_tokens(h, dest, tile_expert.shape[0] * MOE_TM, row0=s * t_grp)
                ys = _moe_ffn(hs, *expert_w, tile_expert, n_active)
                y0, y1 = _sc_gather_tokens(ys, dest)
                xt = _combine_ple(y0, y1, route, x_mid, o_norm_ffn_post[j], p_li, *ple,
                                  row0=s * t_grp, prev=xt)
    return xt.reshape(batch, seq, d_model)
```

```python
import functools
import math

import numpy as np
import jax
import jax.numpy as jnp
from jax import lax
from jax.experimental import pallas as pl
from jax.experimental.pallas import tpu as pltpu
from jax.experimental.pallas import tpu_sc as plsc

F32 = jnp.float32
BF16 = jnp.bfloat16
EPS = 1e-6

LANES = 128
SUBLANES = 8
VMEM_LIMIT_BYTES = 56 * 1024 * 1024

GRID_W = 64
A_HEADS, A_DK, A_CHUNK = 4, 128, 64
A_UNROLL = 8
B_HEADS, B_HEADDIM, B_GROUPS, B_STATE, B_CONV, B_CHUNK = 8, 64, 2, 128, 5, 128
B_UNROLL = 2
B_HPG = B_HEADS // B_GROUPS
B_GW = B_HPG * B_HEADDIM
C_HEADS, C_HEADDIM = 16, 64
NA_ROWS, NA_COLS = 8, 16
NA_HG = 4
NA_UNROLL = 8
N_EXPERTS, TOP_K = 8, 2
MOE_GROUPS = 2
MOE_TM = 512


def _cparams(sem):
    return pltpu.CompilerParams(dimension_semantics=sem, vmem_limit_bytes=VMEM_LIMIT_BYTES)


def _rms(x, g):
    return x * lax.rsqrt(jnp.mean(x * x, axis=-1, keepdims=True) + EPS) * g


def _silu(x):
    return x * jax.nn.sigmoid(x)


def _dot(a, b):
    return jnp.dot(a, b, preferred_element_type=F32)


def _dot_nt(a, b):
    return lax.dot_general(a, b, (((1,), (1,)), ((), ())), preferred_element_type=F32)


def _dot_tn(a, b):
    return lax.dot_general(a, b, (((0,), (0,)), ((), ())), preferred_element_type=F32)


def _norm_proj_kernel(x_ref, g_ref, w_ref, lb_ref, *o_refs, acts, col_chunk):
    h = _rms(x_ref[...], g_ref[...]).astype(BF16)
    off = 0
    for o_ref, act in zip(o_refs, acts):
        n = o_ref.shape[1]
        for c0 in range(0, n, col_chunk):
            c1 = min(c0 + col_chunk, n)
            y = _dot(h, w_ref[:, off + c0:off + c1])
            if act == "silu":
                y = _silu(y)
            elif act == "log_gate":
                lb = lb_ref[:, c0:c1]
                y = jnp.log(lb + (1.0 - lb) * jax.nn.sigmoid(y))
            o_ref[:, c0:c1] = y.astype(o_ref.dtype)
        off += n


def _norm_proj(x, g, w, outs, lb=None, tm=512, col_chunk=1024):
    T, D = x.shape
    N = w.shape[1]
    widths = [o[0] for o in outs]
    assert sum(widths) == N and T % tm == 0
    if lb is None:
        lb = jnp.zeros((LANES,), F32)
    return pl.pallas_call(
        functools.partial(_norm_proj_kernel, acts=tuple(o[2] for o in outs), col_chunk=col_chunk),
        grid=(T // tm,),
        in_specs=[_row_spec(tm, D, 1), _whole_spec((1, D), 1), _whole_spec((D, N), 1),
                  _whole_spec((1, lb.shape[0]), 1)],
        out_specs=[_row_spec(tm, n, 1) for n in widths],
        out_shape=[jax.ShapeDtypeStruct((T, n), dt) for n, dt, _ in outs],
        compiler_params=_cparams(("parallel",)),
        name="norm_proj",
    )(x, g.reshape(1, D), w, lb.reshape(1, -1))


def _ple_update(x, p, gin_ref, wg_ref, wp_ref, gpost_ref):
    h = _rms(x, gin_ref[...]).astype(BF16)
    gate = jax.nn.sigmoid(_dot(h, wg_ref[...]))
    proj = _dot(p.astype(BF16), wp_ref[...])
    return x + _rms(gate * proj, gpost_ref[...])


def _row_spec(tm, width, n_grid):
    return pl.BlockSpec((tm, width), (lambda i: (i, 0)) if n_grid == 1 else (lambda i, j: (i, 0)))


def _whole_spec(shape, n_grid):
    zeros = (0,) * len(shape)
    return pl.BlockSpec(shape, (lambda i: zeros) if n_grid == 1 else (lambda i, j: zeros))


def _mix_ffn_ple_kernel(oa_ref, ob_ref, wa_ref, wb_ref, gmix_ref, x_ref, gpre_ref, wg_ref, wu_ref, wd_ref,
                        gpost_ref, p_ref, gin_ref, wpg_ref, wpp_ref, gple_ref, o_ref, *, sub, tf):
    tm = x_ref.shape[0]
    F = wg_ref.shape[1]
    def stages(r0):
        rows = pl.ds(r0, sub)
        mix = _dot(oa_ref[rows, :], wa_ref[...]) + _dot(ob_ref[rows, :], wb_ref[...])
        x1 = x_ref[rows, :] + _rms(mix, gmix_ref[...])
        h = _rms(x1, gpre_ref[...]).astype(BF16)
        yield
        acc = None
        for f0 in range(0, F, tf):
            a = (_silu(_dot(h, wg_ref[:, f0:f0 + tf])) * _dot(h, wu_ref[:, f0:f0 + tf])).astype(BF16)
            part = _dot(a, wd_ref[f0:f0 + tf, :])
            acc = part if acc is None else acc + part
            yield
        x2 = x1 + _rms(acc, gpost_ref[...])
        o_ref[rows, :] = _ple_update(x2, p_ref[rows, :], gin_ref, wpg_ref, wpp_ref, gple_ref)
        yield

    tiles = [stages(r0) for r0 in range(0, tm, sub)]
    n_stage = 2 + F // tf
    for step in range(n_stage + len(tiles) - 1):
        for k, tile in enumerate(tiles):
            if 0 <= step - k < n_stage:
                next(tile)


def _mix_ffn_ple(o_a, o_b, w_a, w_b, g_mix, x, g_pre, wg, wu, wd, g_post, p, g_in, wpg, wpp, g_ple,
                 tm=512, sub=256, tf=1408):
    T, D = x.shape
    F = wg.shape[1]
    assert F % tf == 0 and T % tm == 0 and tm % sub == 0
    vec = lambda g: g.reshape(1, D)
    resident = lambda w: pl.BlockSpec(w.shape, lambda i: (0, 0), pipeline_mode=pl.Buffered(1))
    return pl.pallas_call(
        functools.partial(_mix_ffn_ple_kernel, sub=sub, tf=tf),
        grid=(T // tm,),
        in_specs=[_row_spec(tm, o_a.shape[1], 1), _row_spec(tm, o_b.shape[1], 1),
                  resident(w_a), resident(w_b), _whole_spec((1, D), 1),
                  _row_spec(tm, D, 1), _whole_spec((1, D), 1),
                  resident(wg), resident(wu), resident(wd),
                  _whole_spec((1, D), 1),
                  _row_spec(tm, p.shape[1], 1), _whole_spec((1, D), 1),
                  resident(wpg), resident(wpp), _whole_spec((1, D), 1)],
        out_specs=_row_spec(tm, D, 1),
        out_shape=jax.ShapeDtypeStruct((T, D), F32),
        compiler_params=_cparams(("parallel",)),
        name="mix_ffn_ple",
    )(o_a, o_b, w_a, w_b, vec(g_mix), x, vec(g_pre), wg, wu, wd, vec(g_post), p, vec(g_in), wpg, wpp,
      vec(g_ple))


def _roll_rows(x, s, rev):
    n = x.shape[0]
    return pltpu.roll(x, (n - s) if rev else s, 0)


def _cumsum_rows(x, tau, rev):
    n = x.shape[0]
    s = 1
    while s < n:
        if s % SUBLANES:
            shifted = jnp.where(tau >= s, _roll_rows(x, s, rev), 0.0)
        else:
            zeros = jnp.zeros((s,) + x.shape[1:], x.dtype)
            shifted = (jnp.concatenate([x[s:], zeros], axis=0) if rev
                       else jnp.concatenate([zeros, x[:n - s]], axis=0))
        x = x + shifted
        s *= 2
    return x


def _hgrn_levels(C):
    return [C >> (i + 1) for i in range(C.bit_length() - 1)]


def _hgrn_pair_classes(C):
    t = np.arange(C)[:, None]
    s = np.arange(C)[None, :]
    out = np.full((2 * C, 2 * C), -1, np.int32)
    for d, rev in enumerate((False, True)):
        tau, sig = (C - 1 - t, C - 1 - s) if rev else (t, s)
        blk = np.full((C, C), -1, np.int32)
        blk[t == s] = 0
        for i, L in enumerate(_hgrn_levels(C)):
            m = ((t & -(2 * L)) == (s & -(2 * L))) & ((tau & (2 * L - 1)) >= L) & ((sig & (2 * L - 1)) < L)
            blk[m] = i + 1
        out[d * C:(d + 1) * C, d * C:(d + 1) * C] = blk
    return out


def _hgrn_kernel(q_ref, gf_ref, gb_ref, i_ref, gate_ref, cls_ref, ng_ref, o_ref, acc_scr, st_scr):
    S, DK = q_ref.shape
    C = A_CHUNK
    nc = S // C
    assert nc % (2 * A_UNROLL) == 0
    ng = ng_ref[...]
    row = lax.broadcasted_iota(jnp.int32, (C, DK), 0)
    levels = _hgrn_levels(C)
    zero_half = jnp.zeros((C, DK), BF16)

    def stack(top, bottom):
        return jnp.concatenate([top, bottom], axis=0)

    def block_diag(x):
        return stack(jnp.concatenate([x[:C], zero_half], axis=1), jnp.concatenate([zero_half, x[C:]], axis=1))

    def make_decay(rev):
        tau = (C - 1 - row) if rev else row
        odd_rank = (tau & 1) == 1
        last = 0 if rev else C - 1

        def ref_rows(b, L):
            off = L if rev else L - 1
            if 2 * L >= SUBLANES:
                pieces = [jnp.broadcast_to(b[j * 2 * L + off:j * 2 * L + off + 1, :], (2 * L, DK))
                          for j in range(C // (2 * L))]
                return pieces[0] if len(pieces) == 1 else jnp.concatenate(pieces, axis=0)
            if L == 1:
                return jnp.where(odd_rank, _roll_rows(b, 1, rev), b)
            b3 = b.reshape(C // SUBLANES, SUBLANES, DK)
            sub = lax.broadcasted_iota(jnp.int32, b3.shape, 1)
            out = None
            for j in range(SUBLANES // (2 * L)):
                piece = jnp.broadcast_to(b3[:, j * 2 * L + off:j * 2 * L + off + 1, :], b3.shape)
                out = piece if out is None else jnp.where(sub >= j * 2 * L, piece, out)
            return out.reshape(C, DK)

        def decay(g):
            b = _cumsum_rows(g, tau, rev)
            return b, [ref_rows(b, L) for L in levels], b[last:last + 1, :]

        return decay

    decay_fwd, decay_bwd = make_decay(False), make_decay(True)
    pair_cls = cls_ref[...]
    st_scr[...] = jnp.zeros_like(st_scr)

    def chunk_pair(rows_f, rows_b, st):
        q = stack(q_ref[rows_f, :], q_ref[rows_b, :]).astype(F32)
        g_f, g_b = gf_ref[rows_f, :], gb_ref[rows_b, :]
        k = 1.0 - jnp.exp(stack(g_f, g_b))
        vb = stack(i_ref[rows_f, :], i_ref[rows_b, :])
        b_f, refs_f, last_f = decay_fwd(g_f)
        b_b, refs_b, last_b = decay_bwd(g_b)
        b = stack(b_f, b_b)

        o = _dot_nt(block_diag((q * jnp.exp(b)).astype(BF16)), st.astype(BF16))

        attn = jnp.where(pair_cls == 0, _dot_nt(q.astype(BF16), k.astype(BF16)), 0.0)
        for i in range(len(levels)):
            e = jnp.exp(-jnp.abs(b - stack(refs_f[i], refs_b[i])))
            attn = jnp.where(pair_cls == i + 1, _dot_nt((q * e).astype(BF16), (k * e).astype(BF16)), attn)
        o = o + _dot(attn.astype(BF16), vb)

        b_last = stack(jnp.broadcast_to(last_f, (C, DK)), jnp.broadcast_to(last_b, (C, DK)))
        khat = block_diag((k * jnp.exp(b_last - b)).astype(BF16))
        keep = jnp.exp(jnp.concatenate([last_f, last_b], axis=1))
        return o, st * keep + _dot_tn(vb, khat)

    def make_body(final):
        def body(it, carry):
            st = st_scr[...]
            for u in range(A_UNROLL):
                ci = it * A_UNROLL + u
                rows_f = pl.ds(pl.multiple_of(ci * C, C), C)
                rows_b = pl.ds(pl.multiple_of((nc - 1 - ci) * C, C), C)
                o, st = chunk_pair(rows_f, rows_b, st)
                for rows, part in ((rows_f, o[:C]), (rows_b, o[C:])):
                    if final:
                        tot = acc_scr[rows, :] + part
                        o_ref[rows, :] = (_rms(tot, ng) * gate_ref[rows, :].astype(F32)).astype(o_ref.dtype)
                    else:
                        acc_scr[rows, :] = part
            st_scr[...] = st
            return carry

        return body

    trips = nc // A_UNROLL
    lax.fori_loop(0, trips // 2, make_body(False), 0)
    lax.fori_loop(trips // 2, trips, make_body(True), 0)


def _hgrn_mixer(q, g_fwd, g_bwd, v, gate, norm_g, batch, seq):
    T = q.shape[0]
    H, DK = A_HEADS, A_DK
    head = pl.BlockSpec((seq, DK), lambda b, h: (b, h))
    classes = jnp.asarray(_hgrn_pair_classes(A_CHUNK))
    return pl.pallas_call(
        _hgrn_kernel,
        grid=(batch, H),
        in_specs=[head, head, head, head, head,
                  pl.BlockSpec(classes.shape, lambda b, h: (0, 0)),
                  pl.BlockSpec((1, DK), lambda b, h: (0, 0))],
        out_specs=pl.BlockSpec((seq, DK), lambda b, h: (b, h)),
        out_shape=jax.ShapeDtypeStruct((T, H * DK), BF16),
        scratch_shapes=[pltpu.VMEM((seq, DK), F32), pltpu.VMEM((DK, 2 * DK), F32)],
        compiler_params=_cparams(("parallel", "parallel")),
        name="hgrn2",
    )(q, g_fwd, g_bwd, v, gate, classes, norm_g.reshape(1, DK))


def _expand_heads(col, j0, width):
    q = col.shape[0]
    lane = lax.broadcasted_iota(jnp.int32, (q, width), 1)
    out = jnp.broadcast_to(col[:, j0 + B_HPG - 1:j0 + B_HPG], (q, width))
    for j in range(B_HPG - 2, -1, -1):
        out = jnp.where(lane < (j + 1) * B_HEADDIM,
                        jnp.broadcast_to(col[:, j0 + j:j0 + j + 1], (q, width)), out)
    return out


def _ssd_kernel(z_ref, x_ref, bm_ref, cm_ref, dt_ref, cwx_ref, cwb_ref, cwc_ref,
                cbx_ref, cbb_ref, cbc_ref, hp_ref, dsk_ref, ng_ref, o_ref,
                xs_scr, b_scr, c_scr, y_scr, *st_scrs):
    S = x_ref.shape[0]
    Q = B_CHUNK
    nc = S // Q
    GW = B_GW
    hp = hp_ref[0]
    a_row, dtb_row = hp[0:1, :], hp[1:2, :]
    row = lax.broadcasted_iota(jnp.int32, (Q, LANES), 0)
    t2 = lax.broadcasted_iota(jnp.int32, (Q, Q), 0)
    s2 = lax.broadcasted_iota(jnp.int32, (Q, Q), 1)
    lane_gw = lax.broadcasted_iota(jnp.int32, (Q, GW), 1)
    head_mask = [(lane_gw >= j * B_HEADDIM) & (lane_gw < (j + 1) * B_HEADDIM) for j in range(B_HPG)]
    halo = SUBLANES

    def conv_body(c, carry):
        r0 = pl.multiple_of(c * Q, Q)
        rp = pl.multiple_of(jnp.maximum(r0 - halo, 0), halo)
        rn = pl.multiple_of(jnp.minimum(r0 + Q, S - halo), halo)
        has_prev = c > 0
        has_next = c < nc - 1

        def conv(src_ref, w_ref, bias_ref):
            prev = jnp.where(has_prev, src_ref[pl.ds(rp, halo), :].astype(F32), 0.0)
            nxt = jnp.where(has_next, src_ref[pl.ds(rn, halo), :].astype(F32), 0.0)
            xx = jnp.concatenate([prev, src_ref[pl.ds(r0, Q), :].astype(F32), nxt], axis=0)
            n = Q + 2 * halo
            w = w_ref[0]
            acc = None
            for j in range(B_CONV):
                delta = j - B_CONV // 2
                sh = xx if delta == 0 else pltpu.roll(xx, (-delta) % n, 0)
                term = sh[halo:halo + Q, :] * w[j:j + 1, :]
                acc = term if acc is None else acc + term
            return _silu(acc + bias_ref[0])

        xs_scr[pl.ds(r0, Q), :] = conv(x_ref, cwx_ref, cbx_ref)
        b_scr[pl.ds(r0, Q), :] = conv(bm_ref, cwb_ref, cbb_ref).astype(BF16)
        c_scr[pl.ds(r0, Q), :] = conv(cm_ref, cwc_ref, cbc_ref).astype(BF16)
        return carry

    lax.fori_loop(0, nc, conv_body, 0)

    def make_chunk(rev):
        tau = (Q - 1 - row) if rev else row
        last = 0 if rev else Q - 1
        j0 = B_HPG if rev else 0
        pair_ok = (s2 >= t2) if rev else (t2 >= s2)

        def chunk(rows, st):
            dt = jax.nn.softplus(dt_ref[rows, :] + dtb_row)
            cs = _cumsum_rows(dt * a_row, tau, rev)
            cs_t = cs.T
            xs = xs_scr[rows, :]
            bm = b_scr[rows, :]
            cm = c_scr[rows, :]
            xdt = xs * _expand_heads(dt, j0, GW)
            ecs = _expand_heads(cs, j0, GW)
            ecs_last = ecs[last:last + 1, :]

            y = _dot(cm, st.astype(BF16)) * jnp.exp(ecs)

            scores = _dot_nt(cm, bm)
            for j in range(B_HPG):
                col = jnp.broadcast_to(cs[:, j0 + j:j0 + j + 1], (Q, Q))
                rw = jnp.broadcast_to(cs_t[j0 + j:j0 + j + 1, :], (Q, Q))
                decay = jnp.where(pair_ok, jnp.exp(jnp.minimum(col - rw, 0.0)), 0.0)
                xh = jnp.where(head_mask[j], xdt, 0.0).astype(BF16)
                y = y + _dot((scores * decay).astype(BF16), xh)

            xdec = (xdt * jnp.exp(ecs_last - ecs)).astype(BF16)
            return y, st * jnp.exp(ecs_last) + _dot_tn(bm, xdec)

        return chunk

    chunk_fns = (make_chunk(False), make_chunk(True))
    for st_scr in st_scrs:
        st_scr[...] = jnp.zeros_like(st_scr)

    def make_body(final, unroll):
        def body(it, carry):
            for rev, chunk, st_scr in zip((False, True), chunk_fns, st_scrs):
                st = st_scr[...]
                for u in range(unroll):
                    ci = it * unroll + u
                    c = (nc - 1 - ci) if rev else ci
                    rows = pl.ds(pl.multiple_of(c * Q, Q), Q)
                    y, st = chunk(rows, st)
                    if final:
                        tot = y_scr[rows, :] + y + dsk_ref[0] * xs_scr[rows, :]
                        tot = tot * _silu(z_ref[rows, :].astype(F32))
                        o_ref[rows, :] = _rms(tot, ng_ref[0]).astype(o_ref.dtype)
                    else:
                        y_scr[rows, :] = y
                st_scr[...] = st
            return carry

        return body

    assert nc % (2 * B_UNROLL) == 0
    lax.fori_loop(0, nc // (2 * B_UNROLL), make_body(False, B_UNROLL), 0)
    lax.fori_loop(nc // 2, nc, make_body(True, 1), 0)


def _ssd_mixer(u, dt, conv_w, conv_b, a_log, dt_bias, d_skip, norm_g, batch, seq, col0):
    T = u.shape[0]
    G, GW, N = B_GROUPS, B_GW, B_STATE
    W = B_HEADS * B_HEADDIM
    assert col0 % GW == 0 and GW == 2 * N
    z_blk = col0 // GW
    x_blk = z_blk + W // GW
    b_blk = (col0 + 2 * W) // N
    c_blk = b_blk + G

    def pad_rows(w):
        return jnp.pad(w, ((0, SUBLANES - w.shape[0]), (0, 0)))

    cwx = pad_rows(conv_w[:, :W]).reshape(SUBLANES, G, GW).transpose(1, 0, 2)
    cwb = pad_rows(conv_w[:, W:W + G * N]).reshape(SUBLANES, G, N).transpose(1, 0, 2)
    cwc = pad_rows(conv_w[:, W + G * N:]).reshape(SUBLANES, G, N).transpose(1, 0, 2)
    cbx = conv_b[:W].reshape(G, 1, GW)
    cbb = conv_b[W:W + G * N].reshape(G, 1, N)
    cbc = conv_b[W + G * N:].reshape(G, 1, N)
    a_neg = -jnp.exp(a_log.astype(F32))
    per_dir = lambda v: jnp.concatenate([v[0].reshape(G, B_HPG), v[1].reshape(G, B_HPG)], axis=1)
    hp = jnp.stack([per_dir(a_neg), per_dir(dt_bias.astype(F32))], axis=1)
    hp = jnp.pad(hp, ((0, 0), (0, SUBLANES - 2), (0, LANES - 2 * B_HPG)))
    dsk = jnp.repeat(d_skip.astype(F32), B_HEADDIM).reshape(G, 1, GW)
    ng = norm_g.reshape(G, 1, GW)

    gspec = lambda shape: pl.BlockSpec((1,) + shape, lambda b, g: (g, 0, 0))
    return pl.pallas_call(
        _ssd_kernel,
        grid=(batch, G),
        in_specs=[pl.BlockSpec((seq, GW), lambda b, g: (b, z_blk + g)),
                  pl.BlockSpec((seq, GW), lambda b, g: (b, x_blk + g)),
                  pl.BlockSpec((seq, N), lambda b, g: (b, b_blk + g)),
                  pl.BlockSpec((seq, N), lambda b, g: (b, c_blk + g)),
                  pl.BlockSpec((seq, LANES), lambda b, g: (b, g)),
                  gspec((SUBLANES, GW)), gspec((SUBLANES, N)), gspec((SUBLANES, N)),
                  gspec((1, GW)), gspec((1, N)), gspec((1, N)),
                  gspec((SUBLANES, LANES)), gspec((1, GW)), gspec((1, GW))],
        out_specs=pl.BlockSpec((seq, GW), lambda b, g: (b, g)),
        out_shape=jax.ShapeDtypeStruct((T, W), BF16),
        scratch_shapes=[pltpu.VMEM((seq, GW), F32), pltpu.VMEM((seq, N), BF16),
                        pltpu.VMEM((seq, N), BF16), pltpu.VMEM((seq, GW), F32),
                        pltpu.VMEM((N, GW), F32), pltpu.VMEM((N, GW), F32)],
        compiler_params=_cparams(("parallel", "parallel")),
        name="ssd",
    )(u, u, u, u, dt, cwx, cwb, cwc, cbx, cbb, cbc, hp, dsk, ng)


def _na_kernel(q_ref, k_ref, v_ref, bias_ref, o_ref):
    S, HW = q_ref.shape
    W = GRID_W
    n_rows = S // W
    kh = min(NA_ROWS, n_rows)
    nk = kh * W
    n_pairs = bias_ref.shape[1] // NA_HG
    lane = lax.broadcasted_iota(jnp.int32, (W, HW), 1)
    head_masks = [(lane >= h * C_HEADDIM) & (lane < (h + 1) * C_HEADDIM) for h in range(NA_HG)]
    scale = C_HEADDIM ** -0.5

    def body(r, carry):
        rs = jnp.clip(r - kh // 2, 0, n_rows - kh)
        q = q_ref[pl.ds(pl.multiple_of(r * W, W), W), :] * jnp.asarray(scale, q_ref.dtype)
        kw = k_ref[pl.ds(pl.multiple_of(rs * W, W), nk), :]
        vw = v_ref[pl.ds(pl.multiple_of(rs * W, W), nk), :]
        zero = jnp.zeros_like(q)
        qs = jnp.concatenate([jnp.where(m, q, zero) for m in head_masks], axis=0)
        d0 = (NA_ROWS - 1) - (r - rs)
        bias = jnp.concatenate(
            [jnp.concatenate([bias_ref[0, h * n_pairs + d0 + 2 * m] for m in range(kh // 2)], axis=1)
             for h in range(NA_HG)], axis=0)
        s = _dot_nt(qs, kw) + bias
        m = jnp.max(s, axis=-1, keepdims=True)
        p = jnp.exp(s - m)
        l = jnp.sum(p, axis=-1, keepdims=True)
        res = _dot(p.astype(BF16), vw) / l
        out = jnp.zeros((W, HW), F32)
        for h in range(NA_HG):
            out = jnp.where(head_masks[h], res[h * W:(h + 1) * W, :], out)
        o_ref[pl.ds(pl.multiple_of(r * W, W), W), :] = out.astype(o_ref.dtype)
        return carry

    lax.fori_loop(0, n_rows, body, 0, unroll=NA_UNROLL)


def _na_bias_table(rpb):
    H, n_dr, n_dc = rpb.shape
    W = GRID_W
    c = np.arange(W)
    qs = np.clip(c - NA_COLS // 2, 0, W - NA_COLS)
    valid = (c[None, :] >= qs[:, None]) & (c[None, :] < qs[:, None] + NA_COLS)
    dc = np.clip(c[None, :] - c[:, None] + (NA_COLS - 1), 0, n_dc - 1)
    onehot = (dc[None] == np.arange(n_dc)[:, None, None]).astype(np.float32)
    tab = jnp.einsum("hrd,dcx->hrcx", rpb.astype(F32), jnp.asarray(onehot),
                     precision=lax.Precision.HIGHEST)
    tab = jnp.where(jnp.asarray(valid)[None, None], tab, -jnp.inf)
    pairs = jnp.concatenate([tab[:, :-1], tab[:, 1:]], axis=-1)
    return pairs.reshape(H // NA_HG, NA_HG * (n_dr - 1), W, 2 * W)


def _neighborhood_attention(qkv, rpb, batch, seq):
    T = qkv.shape[0]
    HW = NA_HG * C_HEADDIM
    n_hg = C_HEADS // NA_HG
    n_rows = seq // GRID_W
    assert n_rows >= NA_ROWS and NA_ROWS % 2 == 0
    table = _na_bias_table(rpb)
    return pl.pallas_call(
        _na_kernel,
        grid=(batch, n_hg),
        in_specs=[pl.BlockSpec((seq, HW), lambda b, g: (b, g)),
                  pl.BlockSpec((seq, HW), lambda b, g: (b, n_hg + g)),
                  pl.BlockSpec((seq, HW), lambda b, g: (b, 2 * n_hg + g)),
                  pl.BlockSpec((1,) + table.shape[1:], lambda b, g: (g, 0, 0, 0))],
        out_specs=pl.BlockSpec((seq, HW), lambda b, g: (b, g)),
        out_shape=jax.ShapeDtypeStruct((T, C_HEADS * C_HEADDIM), BF16),
        compiler_params=_cparams(("parallel", "parallel")),
        name="natten",
    )(qkv, qkv, qkv, table)


def _store_token_tiles(ref, val):
    tm, width = val.shape
    n = width // LANES
    for j in range(n):
        ref[pl.ds(j, tm, stride=n), :] = val[:, j * LANES:(j + 1) * LANES]


def _load_token_tiles(ref, n):
    tm = ref.shape[0] // n
    return jnp.concatenate([ref[pl.ds(j, tm, stride=n), :] for j in range(n)], axis=1)


def _proj_router_kernel(a_ref, w_ref, gmix_ref, x_ref, g_ref, wr_ref, x1_ref, o_ref, h_ref):
    x1 = x_ref[...] + _rms(_dot(a_ref[...], w_ref[...]), gmix_ref[...])
    x1_ref[...] = x1
    h = _rms(x1, g_ref[...])
    _store_token_tiles(h_ref, h)
    h_hi = h.astype(BF16)
    h_lo = (h - h_hi.astype(F32)).astype(BF16)
    both = _dot(h_hi, wr_ref[...])
    logits = both[:, :LANES] + both[:, LANES:] + _dot(h_lo, wr_ref[:, :LANES])
    lane = lax.broadcasted_iota(jnp.int32, logits.shape, 1)
    neg = -jnp.inf
    l1 = jnp.where(lane < N_EXPERTS, logits, neg)
    m1 = jnp.max(l1, axis=-1, keepdims=True)
    i1 = jnp.min(jnp.where(l1 == m1, lane, LANES), axis=-1, keepdims=True)
    l2 = jnp.where(lane == i1, neg, l1)
    m2 = jnp.max(l2, axis=-1, keepdims=True)
    i2 = jnp.min(jnp.where(l2 == m2, lane, LANES), axis=-1, keepdims=True)
    e = jnp.exp(m2 - m1)
    w1 = 1.0 / (1.0 + e)
    w2 = e / (1.0 + e)
    out = jnp.where(lane == 0, i1.astype(F32),
                    jnp.where(lane == 1, i2.astype(F32),
                              jnp.where(lane == 2, w1, jnp.where(lane == 3, w2, 0.0))))
    o_ref[...] = out


def _proj_router(a, w, g_mix, x, g, w_router, tm=512):
    T, D = x.shape
    n = D // LANES
    wr = jnp.pad(w_router.astype(F32), ((0, 0), (0, LANES - w_router.shape[1])))
    wr_hi = wr.astype(BF16)
    wr = jnp.concatenate([wr_hi, (wr - wr_hi.astype(F32)).astype(BF16)], axis=1)
    x1, route, h = pl.pallas_call(
        _proj_router_kernel,
        grid=(T // tm,),
        in_specs=[_row_spec(tm, a.shape[1], 1), _whole_spec(w.shape, 1), _whole_spec((1, D), 1),
                  _row_spec(tm, D, 1), _whole_spec((1, D), 1), _whole_spec((D, 2 * LANES), 1)],
        out_specs=[_row_spec(tm, D, 1), _row_spec(tm, LANES, 1), _row_spec(tm * n, LANES, 1)],
        out_shape=[jax.ShapeDtypeStruct((T, D), F32),
                   jax.ShapeDtypeStruct((T, LANES), F32),
                   jax.ShapeDtypeStruct((T * n, LANES), F32)],
        compiler_params=_cparams(("parallel",)),
        name="proj_router",
    )(a, w, g_mix.reshape(1, D), x, g.reshape(1, D), wr)
    return x1, route, h.reshape(T, n, LANES)


SC_CORES = 2
SC_SUBCORES = 16
SC_WORKERS = SC_CORES * SC_SUBCORES
SC_CHUNK = 32


def _sc_mesh():
    return plsc.VectorSubcoreMesh(core_axis_name="c", subcore_axis_name="s",
                                  num_cores=SC_CORES, num_subcores=SC_SUBCORES)


def _sc_index_blocks(idx):
    return idx.reshape(SC_WORKERS, -1, SC_CHUNK)


def _sc_scatter_tokens(h, dest, n_rows, row0=0):
    T = dest.shape[0]
    _, n, _ = h.shape
    per_worker = T // SC_WORKERS
    n_chunks = per_worker // SC_CHUNK
    assert per_worker * SC_WORKERS == T and n_chunks * SC_CHUNK == per_worker

    @functools.partial(
        pl.kernel, mesh=_sc_mesh(),
        out_type=jax.ShapeDtypeStruct((n_rows, n, LANES), h.dtype),
        scratch_types=[pltpu.VMEM((n_chunks, SC_CHUNK), jnp.int32),
                       pltpu.VMEM((n_chunks, SC_CHUNK), jnp.int32),
                       pltpu.VMEM((SC_CHUNK, n, LANES), h.dtype)],
        name="sc_scatter_tokens",
    )
    def scatter(h_hbm, d0_hbm, d1_hbm, o_hbm, i0_v, i1_v, rows_v):
        wid = lax.axis_index("s") * SC_CORES + lax.axis_index("c")
        pltpu.sync_copy(d0_hbm.at[wid], i0_v)
        pltpu.sync_copy(d1_hbm.at[wid], i1_v)

        @pl.loop(0, n_chunks)
        def _(j):
            pltpu.sync_copy(h_hbm.at[pl.ds(row0 + wid * per_worker + j * SC_CHUNK, SC_CHUNK)], rows_v)
            pltpu.sync_copy(rows_v, o_hbm.at[i0_v.at[j]])
            pltpu.sync_copy(rows_v, o_hbm.at[i1_v.at[j]])

    return scatter(h, _sc_index_blocks(dest[:, 0]), _sc_index_blocks(dest[:, 1]))


def _sc_gather_tokens(y, dest):
    T = dest.shape[0]
    _, n, _ = y.shape
    per_worker = T // SC_WORKERS
    n_chunks = per_worker // SC_CHUNK
    assert per_worker * SC_WORKERS == T and n_chunks * SC_CHUNK == per_worker
    out = jax.ShapeDtypeStruct((T, n, LANES), y.dtype)

    @functools.partial(
        pl.kernel, mesh=_sc_mesh(), out_type=(out, out),
        scratch_types=[pltpu.VMEM((n_chunks, SC_CHUNK), jnp.int32),
                       pltpu.VMEM((n_chunks, SC_CHUNK), jnp.int32),
                       pltpu.VMEM((SC_CHUNK, n, LANES), y.dtype)],
        name="sc_gather_tokens",
    )
    def gather(y_hbm, d0_hbm, d1_hbm, o0_hbm, o1_hbm, i0_v, i1_v, rows_v):
        wid = lax.axis_index("s") * SC_CORES + lax.axis_index("c")
        pltpu.sync_copy(d0_hbm.at[wid], i0_v)
        pltpu.sync_copy(d1_hbm.at[wid], i1_v)

        @pl.loop(0, n_chunks)
        def _(j):
            rows = pl.ds(wid * per_worker + j * SC_CHUNK, SC_CHUNK)
            pltpu.sync_copy(y_hbm.at[i0_v.at[j]], rows_v)
            pltpu.sync_copy(rows_v, o0_hbm.at[rows])
            pltpu.sync_copy(y_hbm.at[i1_v.at[j]], rows_v)
            pltpu.sync_copy(rows_v, o1_hbm.at[rows])

    return gather(y, _sc_index_blocks(dest[:, 0]), _sc_index_blocks(dest[:, 1]))


SC_PACK_PAIRS = 8
SC_LANES = 16


def _sc_pack_bf16_rows(w):
    R, C = w.shape
    pairs = R // 2
    per_worker = pairs // SC_WORKERS
    n_chunks = per_worker // SC_PACK_PAIRS
    assert n_chunks * SC_PACK_PAIRS * SC_WORKERS * 2 == R and C % SC_LANES == 0

    def round_bits(v):
        u = plsc.bitcast(v, jnp.int32)
        return u + 0x7FFF + (lax.shift_right_logical(u, 16) & 1)

    @functools.partial(
        pl.kernel, mesh=_sc_mesh(),
        out_type=jax.ShapeDtypeStruct((pairs, C), jnp.int32),
        scratch_types=[pltpu.VMEM((2 * SC_PACK_PAIRS, C), F32), pltpu.VMEM((SC_PACK_PAIRS, C), jnp.int32)],
        compiler_params=pltpu.CompilerParams(use_tc_tiling_on_sc=True, needs_layout_passes=False),
        name="sc_pack_bf16_rows",
    )
    def pack(w_hbm, o_hbm, in_v, out_v):
        wid = lax.axis_index("s") * SC_CORES + lax.axis_index("c")

        @pl.loop(0, n_chunks)
        def _(j):
            p0 = (wid * n_chunks + j) * SC_PACK_PAIRS
            pltpu.sync_copy(w_hbm.at[pl.ds(2 * p0, 2 * SC_PACK_PAIRS)], in_v)

            @pl.loop(0, C, step=SC_LANES)
            def _(c):
                cols = pl.ds(c, SC_LANES)
                for i in range(SC_PACK_PAIRS):
                    lo = lax.shift_right_logical(round_bits(in_v[2 * i, cols]), 16)
                    hi = round_bits(in_v[2 * i + 1, cols]) & jnp.int32(-65536)
                    out_v[i, cols] = lo | hi

            pltpu.sync_copy(out_v, o_hbm.at[pl.ds(p0, SC_PACK_PAIRS)])

    return pack(w)


def _moe_ffn_kernel(te_ref, na_ref, h_ref, wg_ref, wu_ref, wd_ref, o_ref, acc_scr, *, tf):
    i = pl.program_id(0)
    D = acc_scr.shape[1]
    F = wg_ref.shape[2]
    unpack = lambda packed: pltpu.bitcast(packed, BF16)

    @pl.when(i < na_ref[0])
    def _():
        h = _load_token_tiles(h_ref, D // LANES).astype(BF16)
        for f0 in range(0, F, tf):
            a = (_silu(_dot(h, unpack(wg_ref[0, :, f0:f0 + tf])))
                 * _dot(h, unpack(wu_ref[0, :, f0:f0 + tf]))).astype(BF16)
            part = _dot(a, unpack(wd_ref[0, f0 // 2:(f0 + tf) // 2, :]))
            if f0 == 0:
                acc_scr[...] = part
            else:
                acc_scr[...] += part
        _store_token_tiles(o_ref, acc_scr[...])

    @pl.when(i >= na_ref[0])
    def _():
        o_ref[...] = jnp.zeros_like(o_ref)


def _moe_ffn(hs, wg, wu, wd, tile_expert, n_active, tm=MOE_TM, tf=512):
    rows, n, _ = hs.shape
    E, half_d, F = wg.shape
    D = 2 * half_d
    nt = tile_expert.shape[0]
    assert rows == nt * tm and n * LANES == D and wd.shape == (E, F // 2, D)
    grid_spec = pltpu.PrefetchScalarGridSpec(
        num_scalar_prefetch=2,
        grid=(nt,),
        in_specs=[pl.BlockSpec((tm * n, LANES), lambda i, te, na: (i, 0)),
                  pl.BlockSpec((1, D // 2, F), lambda i, te, na: (te[i], 0, 0)),
                  pl.BlockSpec((1, D // 2, F), lambda i, te, na: (te[i], 0, 0)),
                  pl.BlockSpec((1, F // 2, D), lambda i, te, na: (te[i], 0, 0))],
        out_specs=pl.BlockSpec((tm * n, LANES), lambda i, te, na: (i, 0)),
        scratch_shapes=[pltpu.VMEM((tm, D), F32)],
    )
    y = pl.pallas_call(
        functools.partial(_moe_ffn_kernel, tf=tf),
        grid_spec=grid_spec,
        out_shape=jax.ShapeDtypeStruct((rows * n, LANES), F32),
        compiler_params=_cparams(("arbitrary",)),
        name="moe_ffn",
    )(tile_expert, n_active, hs.reshape(rows * n, LANES), wg, wu, wd)
    return y.reshape(rows, n, LANES)


def _combine_ple_kernel(y0_ref, y1_ref, route_ref, x_ref, g_ref, p_ref, gin_ref, wpg_ref, wpp_ref,
                        gple_ref, *rest):
    o_ref = rest[-1]
    n = x_ref.shape[1] // LANES
    route = route_ref[...]
    moe = route[:, 2:3] * _load_token_tiles(y0_ref, n) + route[:, 3:4] * _load_token_tiles(y1_ref, n)
    x2 = x_ref[...] + _rms(moe, g_ref[...])
    o_ref[...] = _ple_update(x2, p_ref[...], gin_ref, wpg_ref, wpp_ref, gple_ref)


def _combine_ple(y0, y1, route, x, g, p, g_in, wpg, wpp, g_ple, row0=0, prev=None, tm=512):
    T, D = x.shape
    Tp, n, _ = y0.shape
    assert row0 % tm == 0 and Tp % tm == 0 and n * LANES == D
    blk0 = row0 // tm
    vec = lambda v: v.reshape(1, D)
    shifted = lambda width: pl.BlockSpec((tm, width), lambda i: (i + blk0, 0))
    in_specs = [_row_spec(tm * n, LANES, 1), _row_spec(tm * n, LANES, 1), shifted(LANES),
                shifted(D), _whole_spec((1, D), 1),
                shifted(p.shape[1]), _whole_spec((1, D), 1),
                _whole_spec(wpg.shape, 1), _whole_spec(wpp.shape, 1), _whole_spec((1, D), 1)]
    args = [y0.reshape(Tp * n, LANES), y1.reshape(Tp * n, LANES), route, x, vec(g), p, vec(g_in), wpg, wpp,
            vec(g_ple)]
    aliases = {}
    if prev is not None:
        in_specs.append(pl.BlockSpec(memory_space=pl.ANY))
        args.append(prev)
        aliases = {len(args) - 1: 0}
    return pl.pallas_call(
        _combine_ple_kernel,
        grid=(Tp // tm,),
        in_specs=in_specs,
        out_specs=shifted(D),
        out_shape=jax.ShapeDtypeStruct((T, D), F32),
        input_output_aliases=aliases,
        compiler_params=_cparams(("parallel",)),
        name="combine_ple",
    )(*args)


def _moe_plan(route, tm=MOE_TM):
    T = route.shape[0]
    e = route[:, :TOP_K].astype(jnp.int32).reshape(-1)
    onehot = (e[:, None] == jnp.arange(N_EXPERTS, dtype=jnp.int32)[None, :]).astype(jnp.int32)
    csum = jnp.cumsum(onehot, axis=0)
    rank = jnp.sum(csum * onehot, axis=1) - 1
    counts = csum[-1]
    padded = ((counts + tm - 1) // tm) * tm
    ends = jnp.cumsum(padded)
    starts = ends - padded
    dest = jnp.sum(starts[None, :] * onehot, axis=1) + rank
    nt = (T * TOP_K) // tm + N_EXPERTS
    tile_start = jnp.arange(nt, dtype=jnp.int32) * tm
    tile_expert = jnp.minimum(jnp.sum((tile_start[:, None] >= ends[None, :]).astype(jnp.int32), axis=1),
                              N_EXPERTS - 1).astype(jnp.int32)
    n_active = (ends[-1] // tm).astype(jnp.int32).reshape(1)
    return tile_expert, n_active, dest.reshape(T, TOP_K).astype(jnp.int32)


def kernel(x, p, hgrn_lb_raw, e_norm_mix_pre, e_w_in, e_conv_w, e_conv_b, e_A_log, e_dt_bias, e_D, e_a_norm, e_b_norm, e_w_out, e_norm_mix_post, e_norm_ffn_pre, e_w_ffn_gate, e_w_ffn_up, e_w_ffn_down, e_norm_ffn_post, o_norm_mix_pre, o_w_qkv, o_rpb, o_w_out, o_norm_mix_post, o_norm_ffn_pre, o_w_router, o_w_exp_gate, o_w_exp_up, o_w_exp_down, o_norm_ffn_post, ple_norm_in, ple_w_gate, ple_w_proj, ple_norm_post):
    batch, seq, d_model = x.shape
    depth = p.shape[0]
    T = batch * seq
    xt = x.reshape(T, d_model)
    lb_all = jnp.cumsum(jax.nn.softmax(hgrn_lb_raw.astype(F32), axis=0), axis=0)

    a_kdim = A_HEADS * A_DK
    b_width = B_HEADS * B_HEADDIM
    conv_dim = b_width + 2 * B_GROUPS * B_STATE
    main_w = 5 * a_kdim + b_width + conv_dim

    def pack_experts(w):
        E, R, C = w.shape
        return _sc_pack_bf16_rows(w.reshape(E * R, C)).reshape(E, R // 2, C)

    for li in range(depth):
        j = li // 2
        p_li = p[li].reshape(T, -1)
        ple = (ple_norm_in[li], ple_w_gate[li].astype(BF16), ple_w_proj[li].astype(BF16), ple_norm_post[li])
        if li % 2 == 0:
            w_in = e_w_in[j]
            dtf = w_in[:, main_w:main_w + B_HEADS].reshape(d_model, B_GROUPS, B_HPG)
            dtb = w_in[:, main_w + B_HEADS:].reshape(d_model, B_GROUPS, B_HPG)
            w_dt = jnp.pad(jnp.concatenate([dtf, dtb], axis=2), ((0, 0), (0, 0), (0, LANES - 2 * B_HPG)))
            w_all = jnp.concatenate([w_in[:, :main_w], w_dt.reshape(d_model, B_GROUPS * LANES)], axis=1)
            outs = [(a_kdim, BF16, "silu"), (a_kdim, F32, "log_gate"), (a_kdim, F32, "log_gate"),
                    (a_kdim, BF16, None), (a_kdim, BF16, "silu"),
                    (b_width + conv_dim, BF16, None), (B_GROUPS * LANES, F32, None)]
            q_a, gf_a, gb_a, i_a, gate_a, u_b, dt = _norm_proj(
                xt, e_norm_mix_pre[j], w_all.astype(BF16), outs, lb=lb_all[li])
            o_a = _hgrn_mixer(q_a, gf_a, gb_a, i_a, gate_a, e_a_norm[j], batch, seq)
            o_b = _ssd_mixer(u_b, dt, e_conv_w[j], e_conv_b[j], e_A_log[j], e_dt_bias[j], e_D[j],
                             e_b_norm[j], batch, seq, 0)
            w_out = e_w_out[j].astype(BF16)
            xt = _mix_ffn_ple(o_a, o_b, w_out[:a_kdim], w_out[a_kdim:], e_norm_mix_post[j], xt,
                              e_norm_ffn_pre[j], e_w_ffn_gate[j].astype(BF16), e_w_ffn_up[j].astype(BF16),
                              e_w_ffn_down[j].astype(BF16), e_norm_ffn_post[j], p_li, *ple)
        else:
            (qkv,) = _norm_proj(xt, o_norm_mix_pre[j], o_w_qkv[j].astype(BF16),
                                [(3 * C_HEADS * C_HEADDIM, BF16, None)])
            o_c = _neighborhood_attention(qkv, o_rpb[j], batch, seq)
            xt, route, h = _proj_router(o_c, o_w_out[j].astype(BF16), o_norm_mix_post[j], xt,
                                        o_norm_ffn_pre[j], o_w_router[j])
            expert_w = [pack_experts(w[j]) for w in (o_w_exp_gate, o_w_exp_up, o_w_exp_down)]
            x_mid, xt = xt, None
            t_grp = T // MOE_GROUPS
            for s in range(MOE_GROUPS):
                tile_expert, n_active, dest = _moe_plan(route[s * t_grp:(s + 1) * t_grp])
                hs = _sc_scatter_tokens(h, dest, tile_expert.shape[0] * MOE_TM, row0=s * t_grp)
                ys = _moe_ffn(hs, *expert_w, tile_expert, n_active)
                y0, y1 = _sc_gather_tokens(ys, dest)
                xt = _combine_ple(y0, y1, route, x_mid, o_norm_ffn_post[j], p_li, *ple,
                                  row0=s * t_grp, prev=xt)
    return xt.reshape(batch, seq, d_model)
```

```python
import functools
import math

import numpy as np
import jax
import jax.numpy as jnp
from jax import lax
from jax.experimental import pallas as pl
from jax.experimental.pallas import tpu as pltpu
from jax.experimental.pallas import tpu_sc as plsc

F32 = jnp.float32
BF16 = jnp.bfloat16
EPS = 1e-6

LANES = 128
SUBLANES = 8
VMEM_LIMIT_BYTES = 56 * 1024 * 1024

GRID_W = 64
A_HEADS, A_DK, A_CHUNK = 4, 128, 64
A_UNROLL = 8
B_HEADS, B_HEADDIM, B_GROUPS, B_STATE, B_CONV, B_CHUNK = 8, 64, 2, 128, 5, 128
B_UNROLL = 2
B_HPG = B_HEADS // B_GROUPS
B_GW = B_HPG * B_HEADDIM
C_HEADS, C_HEADDIM = 16, 64
NA_ROWS, NA_COLS = 8, 16
NA_HG = 4
NA_UNROLL = 16
N_EXPERTS, TOP_K = 8, 2
MOE_GROUPS = 2
MOE_TM = 512


def _cparams(sem):
    return pltpu.CompilerParams(dimension_semantics=sem, vmem_limit_bytes=VMEM_LIMIT_BYTES)


def _rms(x, g):
    return x * lax.rsqrt(jnp.mean(x * x, axis=-1, keepdims=True) + EPS) * g


def _silu(x):
    return x * jax.nn.sigmoid(x)


def _dot(a, b):
    return jnp.dot(a, b, preferred_element_type=F32)


def _dot_nt(a, b):
    return lax.dot_general(a, b, (((1,), (1,)), ((), ())), preferred_element_type=F32)


def _dot_tn(a, b):
    return lax.dot_general(a, b, (((0,), (0,)), ((), ())), preferred_element_type=F32)


def _staggered(stage_fns):
    live = list(stage_fns)
    step = 0
    done = [False] * len(live)
    while not all(done):
        for k, gen in enumerate(live):
            if step >= k and not done[k]:
                try:
                    next(gen)
                except StopIteration:
                    done[k] = True
        step += 1


def _norm_proj_kernel(x_ref, g_ref, w_ref, lb_ref, *o_refs, acts, col_chunk, sub):
    def stages(r0):
        rows = pl.ds(r0, sub)
        h = _rms(x_ref[rows, :], g_ref[...]).astype(BF16)
        yield
        off = 0
        for o_ref, act in zip(o_refs, acts):
            n = o_ref.shape[1]
            for c0 in range(0, n, col_chunk):
                c1 = min(c0 + col_chunk, n)
                y = _dot(h, w_ref[:, off + c0:off + c1])
                if act == "silu":
                    y = _silu(y)
                elif act == "log_gate":
                    lb = lb_ref[:, c0:c1]
                    y = jnp.log(lb + (1.0 - lb) * jax.nn.sigmoid(y))
                o_ref[rows, c0:c1] = y.astype(o_ref.dtype)
                yield
            off += n

    _staggered([stages(r0) for r0 in range(0, x_ref.shape[0], sub)])


def _norm_proj(x, g, w, outs, lb=None, tm=512, sub=256, col_chunk=1024):
    T, D = x.shape
    N = w.shape[1]
    widths = [o[0] for o in outs]
    assert sum(widths) == N and T % tm == 0
    if lb is None:
        lb = jnp.zeros((LANES,), F32)
    return pl.pallas_call(
        functools.partial(_norm_proj_kernel, acts=tuple(o[2] for o in outs), col_chunk=col_chunk, sub=sub),
        grid=(T // tm,),
        in_specs=[_row_spec(tm, D, 1), _whole_spec((1, D), 1), _whole_spec((D, N), 1),
                  _whole_spec((1, lb.shape[0]), 1)],
        out_specs=[_row_spec(tm, n, 1) for n in widths],
        out_shape=[jax.ShapeDtypeStruct((T, n), dt) for n, dt, _ in outs],
        compiler_params=_cparams(("parallel",)),
        name="norm_proj",
    )(x, g.reshape(1, D), w, lb.reshape(1, -1))


def _ple_update(x, p, gin_ref, wg_ref, wp_ref, gpost_ref):
    h = _rms(x, gin_ref[...]).astype(BF16)
    gate = jax.nn.sigmoid(_dot(h, wg_ref[...]))
    proj = _dot(p.astype(BF16), wp_ref[...])
    return x + _rms(gate * proj, gpost_ref[...])


def _row_spec(tm, width, n_grid):
    return pl.BlockSpec((tm, width), (lambda i: (i, 0)) if n_grid == 1 else (lambda i, j: (i, 0)))


def _whole_spec(shape, n_grid):
    zeros = (0,) * len(shape)
    return pl.BlockSpec(shape, (lambda i: zeros) if n_grid == 1 else (lambda i, j: zeros))


def _mix_ffn_ple_kernel(oa_ref, ob_ref, wa_ref, wb_ref, gmix_ref, x_ref, gpre_ref, wg_ref, wu_ref, wd_ref,
                        gpost_ref, p_ref, gin_ref, wpg_ref, wpp_ref, gple_ref, o_ref, *, sub, tf):
    tm = x_ref.shape[0]
    F = wg_ref.shape[1]
    def stages(r0):
        rows = pl.ds(r0, sub)
        mix = _dot(oa_ref[rows, :], wa_ref[...]) + _dot(ob_ref[rows, :], wb_ref[...])
        x1 = x_ref[rows, :] + _rms(mix, gmix_ref[...])
        h = _rms(x1, gpre_ref[...]).astype(BF16)
        yield
        acc = None
        for f0 in range(0, F, tf):
            a = (_silu(_dot(h, wg_ref[:, f0:f0 + tf])) * _dot(h, wu_ref[:, f0:f0 + tf])).astype(BF16)
            part = _dot(a, wd_ref[f0:f0 + tf, :])
            acc = part if acc is None else acc + part
            yield
        x2 = x1 + _rms(acc, gpost_ref[...])
        o_ref[rows, :] = _ple_update(x2, p_ref[rows, :], gin_ref, wpg_ref, wpp_ref, gple_ref)
        yield

    _staggered([stages(r0) for r0 in range(0, tm, sub)])


def _mix_ffn_ple(o_a, o_b, w_a, w_b, g_mix, x, g_pre, wg, wu, wd, g_post, p, g_in, wpg, wpp, g_ple,
                 tm=512, sub=256, tf=1408):
    T, D = x.shape
    F = wg.shape[1]
    assert F % tf == 0 and T % tm == 0 and tm % sub == 0
    vec = lambda g: g.reshape(1, D)
    resident = lambda w: pl.BlockSpec(w.shape, lambda i: (0, 0), pipeline_mode=pl.Buffered(1))
    return pl.pallas_call(
        functools.partial(_mix_ffn_ple_kernel, sub=sub, tf=tf),
        grid=(T // tm,),
        in_specs=[_row_spec(tm, o_a.shape[1], 1), _row_spec(tm, o_b.shape[1], 1),
                  resident(w_a), resident(w_b), _whole_spec((1, D), 1),
                  _row_spec(tm, D, 1), _whole_spec((1, D), 1),
                  resident(wg), resident(wu), resident(wd),
                  _whole_spec((1, D), 1),
                  _row_spec(tm, p.shape[1], 1), _whole_spec((1, D), 1),
                  resident(wpg), resident(wpp), _whole_spec((1, D), 1)],
        out_specs=_row_spec(tm, D, 1),
        out_shape=jax.ShapeDtypeStruct((T, D), F32),
        compiler_params=_cparams(("parallel",)),
        name="mix_ffn_ple",
    )(o_a, o_b, w_a, w_b, vec(g_mix), x, vec(g_pre), wg, wu, wd, vec(g_post), p, vec(g_in), wpg, wpp,
      vec(g_ple))


def _roll_rows(x, s, rev):
    n = x.shape[0]
    return pltpu.roll(x, (n - s) if rev else s, 0)


def _cumsum_rows(x, tau, rev):
    n = x.shape[0]
    s = 1
    while s < n:
        if s % SUBLANES:
            shifted = jnp.where(tau >= s, _roll_rows(x, s, rev), 0.0)
        else:
            zeros = jnp.zeros((s,) + x.shape[1:], x.dtype)
            shifted = (jnp.concatenate([x[s:], zeros], axis=0) if rev
                       else jnp.concatenate([zeros, x[:n - s]], axis=0))
        x = x + shifted
        s *= 2
    return x


def _hgrn_levels(C):
    return [C >> (i + 1) for i in range(C.bit_length() - 1)]


def _hgrn_pair_classes(C):
    t = np.arange(C)[:, None]
    s = np.arange(C)[None, :]
    out = np.full((2 * C, 2 * C), -1, np.int32)
    for d, rev in enumerate((False, True)):
        tau, sig = (C - 1 - t, C - 1 - s) if rev else (t, s)
        blk = np.full((C, C), -1, np.int32)
        blk[t == s] = 0
        for i, L in enumerate(_hgrn_levels(C)):
            m = ((t & -(2 * L)) == (s & -(2 * L))) & ((tau & (2 * L - 1)) >= L) & ((sig & (2 * L - 1)) < L)
            blk[m] = i + 1
        out[d * C:(d + 1) * C, d * C:(d + 1) * C] = blk
    return out


def _hgrn_kernel(q_ref, gf_ref, gb_ref, i_ref, gate_ref, cls_ref, ng_ref, o_ref, acc_scr, st_scr):
    S, DK = q_ref.shape
    C = A_CHUNK
    nc = S // C
    assert nc % (2 * A_UNROLL) == 0
    ng = ng_ref[...]
    row = lax.broadcasted_iota(jnp.int32, (C, DK), 0)
    levels = _hgrn_levels(C)
    zero_half = jnp.zeros((C, DK), BF16)

    def stack(top, bottom):
        return jnp.concatenate([top, bottom], axis=0)

    def block_diag(x):
        return stack(jnp.concatenate([x[:C], zero_half], axis=1), jnp.concatenate([zero_half, x[C:]], axis=1))

    def make_decay(rev):
        tau = (C - 1 - row) if rev else row
        odd_rank = (tau & 1) == 1
        last = 0 if rev else C - 1

        def ref_rows(b, L):
            off = L if rev else L - 1
            if 2 * L >= SUBLANES:
                pieces = [jnp.broadcast_to(b[j * 2 * L + off:j * 2 * L + off + 1, :], (2 * L, DK))
                          for j in range(C // (2 * L))]
                return pieces[0] if len(pieces) == 1 else jnp.concatenate(pieces, axis=0)
            if L == 1:
                return jnp.where(odd_rank, _roll_rows(b, 1, rev), b)
            b3 = b.reshape(C // SUBLANES, SUBLANES, DK)
            sub = lax.broadcasted_iota(jnp.int32, b3.shape, 1)
            out = None
            for j in range(SUBLANES // (2 * L)):
                piece = jnp.broadcast_to(b3[:, j * 2 * L + off:j * 2 * L + off + 1, :], b3.shape)
                out = piece if out is None else jnp.where(sub >= j * 2 * L, piece, out)
            return out.reshape(C, DK)

        def decay(g):
            b = _cumsum_rows(g, tau, rev)
            return b, [ref_rows(b, L) for L in levels], b[last:last + 1, :]

        return decay

    decay_fwd, decay_bwd = make_decay(False), make_decay(True)
    pair_cls = cls_ref[...]
    st_scr[...] = jnp.zeros_like(st_scr)

    def chunk_pair(rows_f, rows_b, st):
        q = stack(q_ref[rows_f, :], q_ref[rows_b, :]).astype(F32)
        g_f, g_b = gf_ref[rows_f, :], gb_ref[rows_b, :]
        k = 1.0 - jnp.exp(stack(g_f, g_b))
        vb = stack(i_ref[rows_f, :], i_ref[rows_b, :])
        b_f, refs_f, last_f = decay_fwd(g_f)
        b_b, refs_b, last_b = decay_bwd(g_b)
        b = stack(b_f, b_b)

        o = _dot_nt(block_diag((q * jnp.exp(b)).astype(BF16)), st.astype(BF16))

        attn = jnp.where(pair_cls == 0, _dot_nt(q.astype(BF16), k.astype(BF16)), 0.0)
        for i in range(len(levels)):
            e = jnp.exp(-jnp.abs(b - stack(refs_f[i], refs_b[i])))
            attn = jnp.where(pair_cls == i + 1, _dot_nt((q * e).astype(BF16), (k * e).astype(BF16)), attn)
        o = o + _dot(attn.astype(BF16), vb)

        b_last = stack(jnp.broadcast_to(last_f, (C, DK)), jnp.broadcast_to(last_b, (C, DK)))
        khat = block_diag((k * jnp.exp(b_last - b)).astype(BF16))
        keep = jnp.exp(jnp.concatenate([last_f, last_b], axis=1))
        return o, st * keep + _dot_tn(vb, khat)

    def make_body(final):
        def body(it, carry):
            st = st_scr[...]
            for u in range(A_UNROLL):
                ci = it * A_UNROLL + u
                rows_f = pl.ds(pl.multiple_of(ci * C, C), C)
                rows_b = pl.ds(pl.multiple_of((nc - 1 - ci) * C, C), C)
                o, st = chunk_pair(rows_f, rows_b, st)
                for rows, part in ((rows_f, o[:C]), (rows_b, o[C:])):
                    if final:
                        tot = acc_scr[rows, :] + part
                        o_ref[rows, :] = (_rms(tot, ng) * gate_ref[rows, :].astype(F32)).astype(o_ref.dtype)
                    else:
                        acc_scr[rows, :] = part
            st_scr[...] = st
            return carry

        return body

    trips = nc // A_UNROLL
    lax.fori_loop(0, trips // 2, make_body(False), 0)
    lax.fori_loop(trips // 2, trips, make_body(True), 0)


def _hgrn_mixer(q, g_fwd, g_bwd, v, gate, norm_g, batch, seq):
    T = q.shape[0]
    H, DK = A_HEADS, A_DK
    head = pl.BlockSpec((seq, DK), lambda b, h: (b, h))
    classes = jnp.asarray(_hgrn_pair_classes(A_CHUNK))
    return pl.pallas_call(
        _hgrn_kernel,
        grid=(batch, H),
        in_specs=[head, head, head, head, head,
                  pl.BlockSpec(classes.shape, lambda b, h: (0, 0)),
                  pl.BlockSpec((1, DK), lambda b, h: (0, 0))],
        out_specs=pl.BlockSpec((seq, DK), lambda b, h: (b, h)),
        out_shape=jax.ShapeDtypeStruct((T, H * DK), BF16),
        scratch_shapes=[pltpu.VMEM((seq, DK), F32), pltpu.VMEM((DK, 2 * DK), F32)],
        compiler_params=_cparams(("parallel", "parallel")),
        name="hgrn2",
    )(q, g_fwd, g_bwd, v, gate, classes, norm_g.reshape(1, DK))


def _expand_heads(col, j0, width):
    q = col.shape[0]
    lane = lax.broadcasted_iota(jnp.int32, (q, width), 1)
    out = jnp.broadcast_to(col[:, j0 + B_HPG - 1:j0 + B_HPG], (q, width))
    for j in range(B_HPG - 2, -1, -1):
        out = jnp.where(lane < (j + 1) * B_HEADDIM,
                        jnp.broadcast_to(col[:, j0 + j:j0 + j + 1], (q, width)), out)
    return out


def _ssd_kernel(z_ref, x_ref, bm_ref, cm_ref, dt_ref, cwx_ref, cwb_ref, cwc_ref,
                cbx_ref, cbb_ref, cbc_ref, hp_ref, dsk_ref, ng_ref, o_ref,
                xs_scr, b_scr, c_scr, y_scr, *st_scrs):
    S = x_ref.shape[0]
    Q = B_CHUNK
    nc = S // Q
    GW = B_GW
    hp = hp_ref[0]
    a_row, dtb_row = hp[0:1, :], hp[1:2, :]
    row = lax.broadcasted_iota(jnp.int32, (Q, LANES), 0)
    t2 = lax.broadcasted_iota(jnp.int32, (Q, Q), 0)
    s2 = lax.broadcasted_iota(jnp.int32, (Q, Q), 1)
    lane_gw = lax.broadcasted_iota(jnp.int32, (Q, GW), 1)
    head_mask = [(lane_gw >= j * B_HEADDIM) & (lane_gw < (j + 1) * B_HEADDIM) for j in range(B_HPG)]
    halo = SUBLANES

    def conv_body(c, carry):
        r0 = pl.multiple_of(c * Q, Q)
        rp = pl.multiple_of(jnp.maximum(r0 - halo, 0), halo)
        rn = pl.multiple_of(jnp.minimum(r0 + Q, S - halo), halo)
        has_prev = c > 0
        has_next = c < nc - 1

        def conv(src_ref, w_ref, bias_ref):
            prev = jnp.where(has_prev, src_ref[pl.ds(rp, halo), :].astype(F32), 0.0)
            nxt = jnp.where(has_next, src_ref[pl.ds(rn, halo), :].astype(F32), 0.0)
            xx = jnp.concatenate([prev, src_ref[pl.ds(r0, Q), :].astype(F32), nxt], axis=0)
            n = Q + 2 * halo
            w = w_ref[0]
            acc = None
            for j in range(B_CONV):
                delta = j - B_CONV // 2
                sh = xx if delta == 0 else pltpu.roll(xx, (-delta) % n, 0)
                term = sh[halo:halo + Q, :] * w[j:j + 1, :]
                acc = term if acc is None else acc + term
            return _silu(acc + bias_ref[0])

        xs_scr[pl.ds(r0, Q), :] = conv(x_ref, cwx_ref, cbx_ref)
        b_scr[pl.ds(r0, Q), :] = conv(bm_ref, cwb_ref, cbb_ref).astype(BF16)
        c_scr[pl.ds(r0, Q), :] = conv(cm_ref, cwc_ref, cbc_ref).astype(BF16)
        return carry

    lax.fori_loop(0, nc, conv_body, 0)

    def make_chunk(rev):
        tau = (Q - 1 - row) if rev else row
        last = 0 if rev else Q - 1
        j0 = B_HPG if rev else 0
        pair_ok = (s2 >= t2) if rev else (t2 >= s2)

        def chunk(rows, st):
            dt = jax.nn.softplus(dt_ref[rows, :] + dtb_row)
            cs = _cumsum_rows(dt * a_row, tau, rev)
            cs_t = cs.T
            xs = xs_scr[rows, :]
            bm = b_scr[rows, :]
            cm = c_scr[rows, :]
            xdt = xs * _expand_heads(dt, j0, GW)
            ecs = _expand_heads(cs, j0, GW)
            ecs_last = ecs[last:last + 1, :]

            y = _dot(cm, st.astype(BF16)) * jnp.exp(ecs)

            scores = _dot_nt(cm, bm)
            for j in range(B_HPG):
                col = jnp.broadcast_to(cs[:, j0 + j:j0 + j + 1], (Q, Q))
                rw = jnp.broadcast_to(cs_t[j0 + j:j0 + j + 1, :], (Q, Q))
                decay = jnp.where(pair_ok, jnp.exp(jnp.minimum(col - rw, 0.0)), 0.0)
                xh = jnp.where(head_mask[j], xdt, 0.0).astype(BF16)
                y = y + _dot((scores * decay).astype(BF16), xh)

            xdec = (xdt * jnp.exp(ecs_last - ecs)).astype(BF16)
            return y, st * jnp.exp(ecs_last) + _dot_tn(bm, xdec)

        return chunk

    chunk_fns = (make_chunk(False), make_chunk(True))
    for st_scr in st_scrs:
        st_scr[...] = jnp.zeros_like(st_scr)

    def make_body(final, unroll):
        def body(it, carry):
            for rev, chunk, st_scr in zip((False, True), chunk_fns, st_scrs):
                st = st_scr[...]
                for u in range(unroll):
                    ci = it * unroll + u
                    c = (nc - 1 - ci) if rev else ci
                    rows = pl.ds(pl.multiple_of(c * Q, Q), Q)
                    y, st = chunk(rows, st)
                    if final:
                        tot = y_scr[rows, :] + y + dsk_ref[0] * xs_scr[rows, :]
                        tot = tot * _silu(z_ref[rows, :].astype(F32))
                        o_ref[rows, :] = _rms(tot, ng_ref[0]).astype(o_ref.dtype)
                    else:
                        y_scr[rows, :] = y
                st_scr[...] = st
            return carry

        return body

    assert nc % (2 * B_UNROLL) == 0
    lax.fori_loop(0, nc // (2 * B_UNROLL), make_body(False, B_UNROLL), 0)
    lax.fori_loop(nc // 2, nc, make_body(True, 1), 0)


def _ssd_mixer(u, dt, conv_w, conv_b, a_log, dt_bias, d_skip, norm_g, batch, seq, col0):
    T = u.shape[0]
    G, GW, N = B_GROUPS, B_GW, B_STATE
    W = B_HEADS * B_HEADDIM
    assert col0 % GW == 0 and GW == 2 * N
    z_blk = col0 // GW
    x_blk = z_blk + W // GW
    b_blk = (col0 + 2 * W) // N
    c_blk = b_blk + G

    def pad_rows(w):
        return jnp.pad(w, ((0, SUBLANES - w.shape[0]), (0, 0)))

    cwx = pad_rows(conv_w[:, :W]).reshape(SUBLANES, G, GW).transpose(1, 0, 2)
    cwb = pad_rows(conv_w[:, W:W + G * N]).reshape(SUBLANES, G, N).transpose(1, 0, 2)
    cwc = pad_rows(conv_w[:, W + G * N:]).reshape(SUBLANES, G, N).transpose(1, 0, 2)
    cbx = conv_b[:W].reshape(G, 1, GW)
    cbb = conv_b[W:W + G * N].reshape(G, 1, N)
    cbc = conv_b[W + G * N:].reshape(G, 1, N)
    a_neg = -jnp.exp(a_log.astype(F32))
    per_dir = lambda v: jnp.concatenate([v[0].reshape(G, B_HPG), v[1].reshape(G, B_HPG)], axis=1)
    hp = jnp.stack([per_dir(a_neg), per_dir(dt_bias.astype(F32))], axis=1)
    hp = jnp.pad(hp, ((0, 0), (0, SUBLANES - 2), (0, LANES - 2 * B_HPG)))
    dsk = jnp.repeat(d_skip.astype(F32), B_HEADDIM).reshape(G, 1, GW)
    ng = norm_g.reshape(G, 1, GW)

    gspec = lambda shape: pl.BlockSpec((1,) + shape, lambda b, g: (g, 0, 0))
    return pl.pallas_call(
        _ssd_kernel,
        grid=(batch, G),
        in_specs=[pl.BlockSpec((seq, GW), lambda b, g: (b, z_blk + g)),
                  pl.BlockSpec((seq, GW), lambda b, g: (b, x_blk + g)),
                  pl.BlockSpec((seq, N), lambda b, g: (b, b_blk + g)),
                  pl.BlockSpec((seq, N), lambda b, g: (b, c_blk + g)),
                  pl.BlockSpec((seq, LANES), lambda b, g: (b, g)),
                  gspec((SUBLANES, GW)), gspec((SUBLANES, N)), gspec((SUBLANES, N)),
                  gspec((1, GW)), gspec((1, N)), gspec((1, N)),
                  gspec((SUBLANES, LANES)), gspec((1, GW)), gspec((1, GW))],
        out_specs=pl.BlockSpec((seq, GW), lambda b, g: (b, g)),
        out_shape=jax.ShapeDtypeStruct((T, W), BF16),
        scratch_shapes=[pltpu.VMEM((seq, GW), F32), pltpu.VMEM((seq, N), BF16),
                        pltpu.VMEM((seq, N), BF16), pltpu.VMEM((seq, GW), F32),
                        pltpu.VMEM((N, GW), F32), pltpu.VMEM((N, GW), F32)],
        compiler_params=_cparams(("parallel", "parallel")),
        name="ssd",
    )(u, u, u, u, dt, cwx, cwb, cwc, cbx, cbb, cbc, hp, dsk, ng)


def _na_kernel(q_ref, k_ref, v_ref, bias_ref, o_ref):
    S, HW = q_ref.shape
    W = GRID_W
    n_rows = S // W
    kh = min(NA_ROWS, n_rows)
    nk = kh * W
    n_pairs = bias_ref.shape[1] // NA_HG
    lane = lax.broadcasted_iota(jnp.int32, (W, HW), 1)
    head_masks = [(lane >= h * C_HEADDIM) & (lane < (h + 1) * C_HEADDIM) for h in range(NA_HG)]
    scale = C_HEADDIM ** -0.5

    def body(r, carry):
        rs = jnp.clip(r - kh // 2, 0, n_rows - kh)
        q = q_ref[pl.ds(pl.multiple_of(r * W, W), W), :] * jnp.asarray(scale, q_ref.dtype)
        kw = k_ref[pl.ds(pl.multiple_of(rs * W, W), nk), :]
        vw = v_ref[pl.ds(pl.multiple_of(rs * W, W), nk), :]
        zero = jnp.zeros_like(q)
        qs = jnp.concatenate([jnp.where(m, q, zero) for m in head_masks], axis=0)
        d0 = (NA_ROWS - 1) - (r - rs)
        bias = jnp.concatenate(
            [jnp.concatenate([bias_ref[0, h * n_pairs + d0 + 2 * m] for m in range(kh // 2)], axis=1)
             for h in range(NA_HG)], axis=0)
        s = _dot_nt(qs, kw) + bias
        m = jnp.max(s, axis=-1, keepdims=True)
        p = jnp.exp(s - m)
        l = jnp.sum(p, axis=-1, keepdims=True)
        res = _dot(p.astype(BF16), vw) / l
        out = jnp.zeros((W, HW), F32)
        for h in range(NA_HG):
            out = jnp.where(head_masks[h], res[h * W:(h + 1) * W, :], out)
        o_ref[pl.ds(pl.multiple_of(r * W, W), W), :] = out.astype(o_ref.dtype)
        return carry

    lax.fori_loop(0, n_rows, body, 0, unroll=NA_UNROLL)


def _na_bias_table(rpb):
    H, n_dr, n_dc = rpb.shape
    W = GRID_W
    c = np.arange(W)
    qs = np.clip(c - NA_COLS // 2, 0, W - NA_COLS)
    valid = (c[None, :] >= qs[:, None]) & (c[None, :] < qs[:, None] + NA_COLS)
    dc = np.clip(c[None, :] - c[:, None] + (NA_COLS - 1), 0, n_dc - 1)
    onehot = (dc[None] == np.arange(n_dc)[:, None, None]).astype(np.float32)
    tab = jnp.einsum("hrd,dcx->hrcx", rpb.astype(F32), jnp.asarray(onehot),
                     precision=lax.Precision.HIGHEST)
    tab = jnp.where(jnp.asarray(valid)[None, None], tab, -jnp.inf)
    pairs = jnp.concatenate([tab[:, :-1], tab[:, 1:]], axis=-1)
    return pairs.reshape(H // NA_HG, NA_HG * (n_dr - 1), W, 2 * W)


def _neighborhood_attention(qkv, rpb, batch, seq):
    T = qkv.shape[0]
    HW = NA_HG * C_HEADDIM
    n_hg = C_HEADS // NA_HG
    n_rows = seq // GRID_W
    assert n_rows >= NA_ROWS and NA_ROWS % 2 == 0
    table = _na_bias_table(rpb)
    return pl.pallas_call(
        _na_kernel,
        grid=(batch, n_hg),
        in_specs=[pl.BlockSpec((seq, HW), lambda b, g: (b, g)),
                  pl.BlockSpec((seq, HW), lambda b, g: (b, n_hg + g)),
                  pl.BlockSpec((seq, HW), lambda b, g: (b, 2 * n_hg + g)),
                  pl.BlockSpec((1,) + table.shape[1:], lambda b, g: (g, 0, 0, 0))],
        out_specs=pl.BlockSpec((seq, HW), lambda b, g: (b, g)),
        out_shape=jax.ShapeDtypeStruct((T, C_HEADS * C_HEADDIM), BF16),
        compiler_params=_cparams(("parallel", "parallel")),
        name="natten",
    )(qkv, qkv, qkv, table)


def _store_token_tiles(ref, val):
    tm, width = val.shape
    n = width // LANES
    for j in range(n):
        ref[pl.ds(j, tm, stride=n), :] = val[:, j * LANES:(j + 1) * LANES]


def _load_token_tiles(ref, n):
    tm = ref.shape[0] // n
    return jnp.concatenate([ref[pl.ds(j, tm, stride=n), :] for j in range(n)], axis=1)


def _proj_router_kernel(a_ref, w_ref, gmix_ref, x_ref, g_ref, wr_ref, x1_ref, o_ref, h_ref):
    x1 = x_ref[...] + _rms(_dot(a_ref[...], w_ref[...]), gmix_ref[...])
    x1_ref[...] = x1
    h = _rms(x1, g_ref[...])
    _store_token_tiles(h_ref, h)
    h_hi = h.astype(BF16)
    h_lo = (h - h_hi.astype(F32)).astype(BF16)
    both = _dot(h_hi, wr_ref[...])
    logits = both[:, :LANES] + both[:, LANES:] + _dot(h_lo, wr_ref[:, :LANES])
    lane = lax.broadcasted_iota(jnp.int32, logits.shape, 1)
    neg = -jnp.inf
    l1 = jnp.where(lane < N_EXPERTS, logits, neg)
    m1 = jnp.max(l1, axis=-1, keepdims=True)
    i1 = jnp.min(jnp.where(l1 == m1, lane, LANES), axis=-1, keepdims=True)
    l2 = jnp.where(lane == i1, neg, l1)
    m2 = jnp.max(l2, axis=-1, keepdims=True)
    i2 = jnp.min(jnp.where(l2 == m2, lane, LANES), axis=-1, keepdims=True)
    e = jnp.exp(m2 - m1)
    w1 = 1.0 / (1.0 + e)
    w2 = e / (1.0 + e)
    out = jnp.where(lane == 0, i1.astype(F32),
                    jnp.where(lane == 1, i2.astype(F32),
                              jnp.where(lane == 2, w1, jnp.where(lane == 3, w2, 0.0))))
    o_ref[...] = out


def _proj_router(a, w, g_mix, x, g, w_router, tm=512):
    T, D = x.shape
    n = D // LANES
    wr = jnp.pad(w_router.astype(F32), ((0, 0), (0, LANES - w_router.shape[1])))
    wr_hi = wr.astype(BF16)
    wr = jnp.concatenate([wr_hi, (wr - wr_hi.astype(F32)).astype(BF16)], axis=1)
    x1, route, h = pl.pallas_call(
        _proj_router_kernel,
        grid=(T // tm,),
        in_specs=[_row_spec(tm, a.shape[1], 1), _whole_spec(w.shape, 1), _whole_spec((1, D), 1),
                  _row_spec(tm, D, 1), _whole_spec((1, D), 1), _whole_spec((D, 2 * LANES), 1)],
        out_specs=[_row_spec(tm, D, 1), _row_spec(tm, LANES, 1), _row_spec(tm * n, LANES, 1)],
        out_shape=[jax.ShapeDtypeStruct((T, D), F32),
                   jax.ShapeDtypeStruct((T, LANES), F32),
                   jax.ShapeDtypeStruct((T * n, LANES), F32)],
        compiler_params=_cparams(("parallel",)),
        name="proj_router",
    )(a, w, g_mix.reshape(1, D), x, g.reshape(1, D), wr)
    return x1, route, h.reshape(T, n, LANES)


SC_CORES = 2
SC_SUBCORES = 16
SC_WORKERS = SC_CORES * SC_SUBCORES
SC_CHUNK = 32


def _sc_mesh():
    return plsc.VectorSubcoreMesh(core_axis_name="c", subcore_axis_name="s",
                                  num_cores=SC_CORES, num_subcores=SC_SUBCORES)


def _sc_index_blocks(idx):
    return idx.reshape(SC_WORKERS, -1, SC_CHUNK)


def _sc_scatter_tokens(h, dest, n_rows, row0=0):
    T = dest.shape[0]
    _, n, _ = h.shape
    per_worker = T // SC_WORKERS
    n_chunks = per_worker // SC_CHUNK
    assert per_worker * SC_WORKERS == T and n_chunks * SC_CHUNK == per_worker

    @functools.partial(
        pl.kernel, mesh=_sc_mesh(),
        out_type=jax.ShapeDtypeStruct((n_rows, n, LANES), h.dtype),
        scratch_types=[pltpu.VMEM((n_chunks, SC_CHUNK), jnp.int32),
                       pltpu.VMEM((n_chunks, SC_CHUNK), jnp.int32),
                       pltpu.VMEM((SC_CHUNK, n, LANES), h.dtype)],
        name="sc_scatter_tokens",
    )
    def scatter(h_hbm, d0_hbm, d1_hbm, o_hbm, i0_v, i1_v, rows_v):
        wid = lax.axis_index("s") * SC_CORES + lax.axis_index("c")
        pltpu.sync_copy(d0_hbm.at[wid], i0_v)
        pltpu.sync_copy(d1_hbm.at[wid], i1_v)

        @pl.loop(0, n_chunks)
        def _(j):
            pltpu.sync_copy(h_hbm.at[pl.ds(row0 + wid * per_worker + j * SC_CHUNK, SC_CHUNK)], rows_v)
            pltpu.sync_copy(rows_v, o_hbm.at[i0_v.at[j]])
            pltpu.sync_copy(rows_v, o_hbm.at[i1_v.at[j]])

    return scatter(h, _sc_index_blocks(dest[:, 0]), _sc_index_blocks(dest[:, 1]))


def _sc_gather_tokens(y, dest):
    T = dest.shape[0]
    _, n, _ = y.shape
    per_worker = T // SC_WORKERS
    n_chunks = per_worker // SC_CHUNK
    assert per_worker * SC_WORKERS == T and n_chunks * SC_CHUNK == per_worker
    out = jax.ShapeDtypeStruct((T, n, LANES), y.dtype)

    @functools.partial(
        pl.kernel, mesh=_sc_mesh(), out_type=(out, out),
        scratch_types=[pltpu.VMEM((n_chunks, SC_CHUNK), jnp.int32),
                       pltpu.VMEM((n_chunks, SC_CHUNK), jnp.int32),
                       pltpu.VMEM((SC_CHUNK, n, LANES), y.dtype)],
        name="sc_gather_tokens",
    )
    def gather(y_hbm, d0_hbm, d1_hbm, o0_hbm, o1_hbm, i0_v, i1_v, rows_v):
        wid = lax.axis_index("s") * SC_CORES + lax.axis_index("c")
        pltpu.sync_copy(d0_hbm.at[wid], i0_v)
        pltpu.sync_copy(d1_hbm.at[wid], i1_v)

        @pl.loop(0, n_chunks)
        def _(j):
            rows = pl.ds(wid * per_worker + j * SC_CHUNK, SC_CHUNK)
            pltpu.sync_copy(y_hbm.at[i0_v.at[j]], rows_v)
            pltpu.sync_copy(rows_v, o0_hbm.at[rows])
            pltpu.sync_copy(y_hbm.at[i1_v.at[j]], rows_v)
            pltpu.sync_copy(rows_v, o1_hbm.at[rows])

    return gather(y, _sc_index_blocks(dest[:, 0]), _sc_index_blocks(dest[:, 1]))


SC_PACK_PAIRS = 8
SC_LANES = 16


def _sc_pack_bf16_rows(w):
    R, C = w.shape
    pairs = R // 2
    per_worker = pairs // SC_WORKERS
    n_chunks = per_worker // SC_PACK_PAIRS
    assert n_chunks * SC_PACK_PAIRS * SC_WORKERS * 2 == R and C % SC_LANES == 0

    def round_bits(v):
        u = plsc.bitcast(v, jnp.int32)
        return u + 0x7FFF + (lax.shift_right_logical(u, 16) & 1)

    @functools.partial(
        pl.kernel, mesh=_sc_mesh(),
        out_type=jax.ShapeDtypeStruct((pairs, C), jnp.int32),
        scratch_types=[pltpu.VMEM((2 * SC_PACK_PAIRS, C), F32), pltpu.VMEM((SC_PACK_PAIRS, C), jnp.int32)],
        compiler_params=pltpu.CompilerParams(use_tc_tiling_on_sc=True, needs_layout_passes=False),
        name="sc_pack_bf16_rows",
    )
    def pack(w_hbm, o_hbm, in_v, out_v):
        wid = lax.axis_index("s") * SC_CORES + lax.axis_index("c")

        @pl.loop(0, n_chunks)
        def _(j):
            p0 = (wid * n_chunks + j) * SC_PACK_PAIRS
            pltpu.sync_copy(w_hbm.at[pl.ds(2 * p0, 2 * SC_PACK_PAIRS)], in_v)

            @pl.loop(0, C, step=SC_LANES)
            def _(c):
                cols = pl.ds(c, SC_LANES)
                for i in range(SC_PACK_PAIRS):
                    lo = lax.shift_right_logical(round_bits(in_v[2 * i, cols]), 16)
                    hi = round_bits(in_v[2 * i + 1, cols]) & jnp.int32(-65536)
                    out_v[i, cols] = lo | hi

            pltpu.sync_copy(out_v, o_hbm.at[pl.ds(p0, SC_PACK_PAIRS)])

    return pack(w)


def _moe_ffn_kernel(te_ref, na_ref, h_ref, wg_ref, wu_ref, wd_ref, o_ref, acc_scr, *, tf):
    i = pl.program_id(0)
    D = acc_scr.shape[1]
    F = wg_ref.shape[2]
    unpack = lambda packed: pltpu.bitcast(packed, BF16)

    @pl.when(i < na_ref[0])
    def _():
        h = _load_token_tiles(h_ref, D // LANES).astype(BF16)
        for f0 in range(0, F, tf):
            a = (_silu(_dot(h, unpack(wg_ref[0, :, f0:f0 + tf])))
                 * _dot(h, unpack(wu_ref[0, :, f0:f0 + tf]))).astype(BF16)
            part = _dot(a, unpack(wd_ref[0, f0 // 2:(f0 + tf) // 2, :]))
            if f0 == 0:
                acc_scr[...] = part
            else:
                acc_scr[...] += part
        _store_token_tiles(o_ref, acc_scr[...])

    @pl.when(i >= na_ref[0])
    def _():
        o_ref[...] = jnp.zeros_like(o_ref)


def _moe_ffn(hs, wg, wu, wd, tile_expert, n_active, tm=MOE_TM, tf=512):
    rows, n, _ = hs.shape
    E, half_d, F = wg.shape
    D = 2 * half_d
    nt = tile_expert.shape[0]
    assert rows == nt * tm and n * LANES == D and wd.shape == (E, F // 2, D)
    grid_spec = pltpu.PrefetchScalarGridSpec(
        num_scalar_prefetch=2,
        grid=(nt,),
        in_specs=[pl.BlockSpec((tm * n, LANES), lambda i, te, na: (i, 0)),
                  pl.BlockSpec((1, D // 2, F), lambda i, te, na: (te[i], 0, 0)),
                  pl.BlockSpec((1, D // 2, F), lambda i, te, na: (te[i], 0, 0)),
                  pl.BlockSpec((1, F // 2, D), lambda i, te, na: (te[i], 0, 0))],
        out_specs=pl.BlockSpec((tm * n, LANES), lambda i, te, na: (i, 0)),
        scratch_shapes=[pltpu.VMEM((tm, D), F32)],
    )
    y = pl.pallas_call(
        functools.partial(_moe_ffn_kernel, tf=tf),
        grid_spec=grid_spec,
        out_shape=jax.ShapeDtypeStruct((rows * n, LANES), F32),
        compiler_params=_cparams(("arbitrary",)),
        name="moe_ffn",
    )(tile_expert, n_active, hs.reshape(rows * n, LANES), wg, wu, wd)
    return y.reshape(rows, n, LANES)


def _combine_ple_kernel(y0_ref, y1_ref, route_ref, x_ref, g_ref, p_ref, gin_ref, wpg_ref, wpp_ref,
                        gple_ref, *rest):
    o_ref = rest[-1]
    n = x_ref.shape[1] // LANES
    route = route_ref[...]
    moe = route[:, 2:3] * _load_token_tiles(y0_ref, n) + route[:, 3:4] * _load_token_tiles(y1_ref, n)
    x2 = x_ref[...] + _rms(moe, g_ref[...])
    o_ref[...] = _ple_update(x2, p_ref[...], gin_ref, wpg_ref, wpp_ref, gple_ref)


def _combine_ple(y0, y1, route, x, g, p, g_in, wpg, wpp, g_ple, row0=0, prev=None, tm=512):
    T, D = x.shape
    Tp, n, _ = y0.shape
    assert row0 % tm == 0 and Tp % tm == 0 and n * LANES == D
    blk0 = row0 // tm
    vec = lambda v: v.reshape(1, D)
    shifted = lambda width: pl.BlockSpec((tm, width), lambda i: (i + blk0, 0))
    in_specs = [_row_spec(tm * n, LANES, 1), _row_spec(tm * n, LANES, 1), shifted(LANES),
                shifted(D), _whole_spec((1, D), 1),
                shifted(p.shape[1]), _whole_spec((1, D), 1),
                _whole_spec(wpg.shape, 1), _whole_spec(wpp.shape, 1), _whole_spec((1, D), 1)]
    args = [y0.reshape(Tp * n, LANES), y1.reshape(Tp * n, LANES), route, x, vec(g), p, vec(g_in), wpg, wpp,
            vec(g_ple)]
    aliases = {}
    if prev is not None:
        in_specs.append(pl.BlockSpec(memory_space=pl.ANY))
        args.append(prev)
        aliases = {len(args) - 1: 0}
    return pl.pallas_call(
        _combine_ple_kernel,
        grid=(Tp // tm,),
        in_specs=in_specs,
        out_specs=shifted(D),
        out_shape=jax.ShapeDtypeStruct((T, D), F32),
        input_output_aliases=aliases,
        compiler_params=_cparams(("parallel",)),
        name="combine_ple",
    )(*args)


def _moe_plan(route, tm=MOE_TM):
    T = route.shape[0]
    e = route[:, :TOP_K].astype(jnp.int32).reshape(-1)
    onehot = (e[:, None] == jnp.arange(N_EXPERTS, dtype=jnp.int32)[None, :]).astype(jnp.int32)
    csum = jnp.cumsum(onehot, axis=0)
    rank = jnp.sum(csum * onehot, axis=1) - 1
    counts = csum[-1]
    padded = ((counts + tm - 1) // tm) * tm
    ends = jnp.cumsum(padded)
    starts = ends - padded
    dest = jnp.sum(starts[None, :] * onehot, axis=1) + rank
    nt = (T * TOP_K) // tm + N_EXPERTS
    tile_start = jnp.arange(nt, dtype=jnp.int32) * tm
    tile_expert = jnp.minimum(jnp.sum((tile_start[:, None] >= ends[None, :]).astype(jnp.int32), axis=1),
                              N_EXPERTS - 1).astype(jnp.int32)
    n_active = (ends[-1] // tm).astype(jnp.int32).reshape(1)
    return tile_expert, n_active, dest.reshape(T, TOP_K).astype(jnp.int32)


def kernel(x, p, hgrn_lb_raw, e_norm_mix_pre, e_w_in, e_conv_w, e_conv_b, e_A_log, e_dt_bias, e_D, e_a_norm, e_b_norm, e_w_out, e_norm_mix_post, e_norm_ffn_pre, e_w_ffn_gate, e_w_ffn_up, e_w_ffn_down, e_norm_ffn_post, o_norm_mix_pre, o_w_qkv, o_rpb, o_w_out, o_norm_mix_post, o_norm_ffn_pre, o_w_router, o_w_exp_gate, o_w_exp_up, o_w_exp_down, o_norm_ffn_post, ple_norm_in, ple_w_gate, ple_w_proj, ple_norm_post):
    batch, seq, d_model = x.shape
    depth = p.shape[0]
    T = batch * seq
    xt = x.reshape(T, d_model)
    lb_all = jnp.cumsum(jax.nn.softmax(hgrn_lb_raw.astype(F32), axis=0), axis=0)

    a_kdim = A_HEADS * A_DK
    b_width = B_HEADS * B_HEADDIM
    conv_dim = b_width + 2 * B_GROUPS * B_STATE
    main_w = 5 * a_kdim + b_width + conv_dim

    def pack_experts(w):
        E, R, C = w.shape
        return _sc_pack_bf16_rows(w.reshape(E * R, C)).reshape(E, R // 2, C)

    for li in range(depth):
        j = li // 2
        p_li = p[li].reshape(T, -1)
        ple = (ple_norm_in[li], ple_w_gate[li].astype(BF16), ple_w_proj[li].astype(BF16), ple_norm_post[li])
        if li % 2 == 0:
            w_in = e_w_in[j]
            dtf = w_in[:, main_w:main_w + B_HEADS].reshape(d_model, B_GROUPS, B_HPG)
            dtb = w_in[:, main_w + B_HEADS:].reshape(d_model, B_GROUPS, B_HPG)
            w_dt = jnp.pad(jnp.concatenate([dtf, dtb], axis=2), ((0, 0), (0, 0), (0, LANES - 2 * B_HPG)))
            w_all = jnp.concatenate([w_in[:, :main_w], w_dt.reshape(d_model, B_GROUPS * LANES)], axis=1)
            outs = [(a_kdim, BF16, "silu"), (a_kdim, F32, "log_gate"), (a_kdim, F32, "log_gate"),
                    (a_kdim, BF16, None), (a_kdim, BF16, "silu"),
                    (b_width + conv_dim, BF16, None), (B_GROUPS * LANES, F32, None)]
            q_a, gf_a, gb_a, i_a, gate_a, u_b, dt = _norm_proj(
                xt, e_norm_mix_pre[j], w_all.astype(BF16), outs, lb=lb_all[li])
            o_a = _hgrn_mixer(q_a, gf_a, gb_a, i_a, gate_a, e_a_norm[j], batch, seq)
            o_b = _ssd_mixer(u_b, dt, e_conv_w[j], e_conv_b[j], e_A_log[j], e_dt_bias[j], e_D[j],
                             e_b_norm[j], batch, seq, 0)
            w_out = e_w_out[j].astype(BF16)
            xt = _mix_ffn_ple(o_a, o_b, w_out[:a_kdim], w_out[a_kdim:], e_norm_mix_post[j], xt,
                              e_norm_ffn_pre[j], e_w_ffn_gate[j].astype(BF16), e_w_ffn_up[j].astype(BF16),
                              e_w_ffn_down[j].astype(BF16), e_norm_ffn_post[j], p_li, *ple)
        else:
            (qkv,) = _norm_proj(xt, o_norm_mix_pre[j], o_w_qkv[j].astype(BF16),
                                [(3 * C_HEADS * C_HEADDIM, BF16, None)])
            o_c = _neighborhood_attention(qkv, o_rpb[j], batch, seq)
            xt, route, h = _proj_router(o_c, o_w_out[j].astype(BF16), o_norm_mix_post[j], xt,
                                        o_norm_ffn_pre[j], o_w_router[j])
            expert_w = [pack_experts(w[j]) for w in (o_w_exp_gate, o_w_exp_up, o_w_exp_down)]
            x_mid, xt = xt, None
            t_grp = T // MOE_GROUPS
            for s in range(MOE_GROUPS):
                tile_expert, n_active, dest = _moe_plan(route[s * t_grp:(s + 1) * t_grp])
                hs = _sc_scatter_tokens(h, dest, tile_expert.shape[0] * MOE_TM, row0=s * t_grp)
                ys = _moe_ffn(hs, *expert_w, tile_expert, n_active)
                y0, y1 = _sc_gather_tokens(ys, dest)
                xt = _combine_ple(y0, y1, route, x_mid, o_norm_ffn_post[j], p_li, *ple,
                                  row0=s * t_grp, prev=xt)
    return xt.reshape(batch, seq, d_model)
```

```python
import functools
import math

import numpy as np
import jax
import jax.numpy as jnp
from jax import lax
from jax.experimental import pallas as pl
from jax.experimental.pallas import tpu as pltpu
from jax.experimental.pallas import tpu_sc as plsc

F32 = jnp.float32
BF16 = jnp.bfloat16
EPS = 1e-6

LANES = 128
SUBLANES = 8
VMEM_LIMIT_BYTES = 56 * 1024 * 1024

GRID_W = 64
A_HEADS, A_DK, A_CHUNK = 4, 128, 64
A_UNROLL = 8
B_HEADS, B_HEADDIM, B_GROUPS, B_STATE, B_CONV, B_CHUNK = 8, 64, 2, 128, 5, 128
B_UNROLL = 2
B_HPG = B_HEADS // B_GROUPS
B_GW = B_HPG * B_HEADDIM
C_HEADS, C_HEADDIM = 16, 64
NA_ROWS, NA_COLS = 8, 16
NA_HG = 4
NA_UNROLL = 16
N_EXPERTS, TOP_K = 8, 2
MOE_GROUPS = 2
MOE_TM = 512


def _cparams(sem):
    return pltpu.CompilerParams(dimension_semantics=sem, vmem_limit_bytes=VMEM_LIMIT_BYTES)


def _rms(x, g):
    return x * lax.rsqrt(jnp.mean(x * x, axis=-1, keepdims=True) + EPS) * g


def _silu(x):
    return x * jax.nn.sigmoid(x)


def _dot(a, b):
    return jnp.dot(a, b, preferred_element_type=F32)


def _dot_nt(a, b):
    return lax.dot_general(a, b, (((1,), (1,)), ((), ())), preferred_element_type=F32)


def _dot_tn(a, b):
    return lax.dot_general(a, b, (((0,), (0,)), ((), ())), preferred_element_type=F32)


def _staggered(stage_fns):
    live = list(stage_fns)
    step = 0
    done = [False] * len(live)
    while not all(done):
        for k, gen in enumerate(live):
            if step >= k and not done[k]:
                try:
                    next(gen)
                except StopIteration:
                    done[k] = True
        step += 1


def _norm_proj_kernel(x_ref, g_ref, lb_ref, *refs, acts, col_chunk, sub):
    o_refs = refs[len(refs) - len(acts):]
    w_refs = list(refs[:len(refs) - len(acts)])

    def stages(r0):
        rows = pl.ds(r0, sub)
        h = _rms(x_ref[rows, :], g_ref[...]).astype(BF16)
        yield
        w_iter = iter(w_refs)
        w_ref, off = next(w_iter), 0
        for o_ref, act in zip(o_refs, acts):
            n = o_ref.shape[1]
            if off == w_ref.shape[1]:
                w_ref, off = next(w_iter), 0
            for c0 in range(0, n, col_chunk):
                c1 = min(c0 + col_chunk, n)
                y = _dot(h, w_ref[:, off + c0:off + c1])
                if act == "silu":
                    y = _silu(y)
                elif act == "log_gate":
                    lb = lb_ref[:, c0:c1]
                    y = jnp.log(lb + (1.0 - lb) * jax.nn.sigmoid(y))
                o_ref[rows, c0:c1] = y.astype(o_ref.dtype)
                yield
            off += n

    _staggered([stages(r0) for r0 in range(0, x_ref.shape[0], sub)])


def _norm_proj(x, g, w, outs, lb=None, tm=512, sub=256, col_chunk=1024):
    T, D = x.shape
    ws = list(w) if isinstance(w, (list, tuple)) else [w]
    widths = [o[0] for o in outs]
    assert sum(widths) == sum(v.shape[1] for v in ws) and T % tm == 0
    if lb is None:
        lb = jnp.zeros((LANES,), F32)
    return pl.pallas_call(
        functools.partial(_norm_proj_kernel, acts=tuple(o[2] for o in outs), col_chunk=col_chunk, sub=sub),
        grid=(T // tm,),
        in_specs=[_row_spec(tm, D, 1), _whole_spec((1, D), 1), _whole_spec((1, lb.shape[0]), 1)]
                 + [_whole_spec(v.shape, 1) for v in ws],
        out_specs=[_row_spec(tm, n, 1) for n in widths],
        out_shape=[jax.ShapeDtypeStruct((T, n), dt) for n, dt, _ in outs],
        compiler_params=_cparams(("parallel",)),
        name="norm_proj",
    )(x, g.reshape(1, D), lb.reshape(1, -1), *ws)


def _ple_update(x, p, gin_ref, wg_ref, wp_ref, gpost_ref):
    h = _rms(x, gin_ref[...]).astype(BF16)
    gate = jax.nn.sigmoid(_dot(h, wg_ref[...]))
    proj = _dot(p.astype(BF16), wp_ref[...])
    return x + _rms(gate * proj, gpost_ref[...])


def _row_spec(tm, width, n_grid):
    return pl.BlockSpec((tm, width), (lambda i: (i, 0)) if n_grid == 1 else (lambda i, j: (i, 0)))


def _whole_spec(shape, n_grid):
    zeros = (0,) * len(shape)
    return pl.BlockSpec(shape, (lambda i: zeros) if n_grid == 1 else (lambda i, j: zeros))


def _mix_ffn_ple_kernel(oa_ref, ob_ref, wa_ref, wb_ref, gmix_ref, x_ref, gpre_ref, wg_ref, wu_ref, wd_ref,
                        gpost_ref, p_ref, gin_ref, wpg_ref, wpp_ref, gple_ref, o_ref, *, sub, tf):
    tm = x_ref.shape[0]
    F = wg_ref.shape[1]
    def stages(r0):
        rows = pl.ds(r0, sub)
        mix = _dot(oa_ref[rows, :], wa_ref[...]) + _dot(ob_ref[rows, :], wb_ref[...])
        x1 = x_ref[rows, :] + _rms(mix, gmix_ref[...])
        h = _rms(x1, gpre_ref[...]).astype(BF16)
        yield
        acc = None
        for f0 in range(0, F, tf):
            a = (_silu(_dot(h, wg_ref[:, f0:f0 + tf])) * _dot(h, wu_ref[:, f0:f0 + tf])).astype(BF16)
            part = _dot(a, wd_ref[f0:f0 + tf, :])
            acc = part if acc is None else acc + part
            yield
        x2 = x1 + _rms(acc, gpost_ref[...])
        o_ref[rows, :] = _ple_update(x2, p_ref[rows, :], gin_ref, wpg_ref, wpp_ref, gple_ref)
        yield

    _staggered([stages(r0) for r0 in range(0, tm, sub)])


def _mix_ffn_ple(o_a, o_b, w_a, w_b, g_mix, x, g_pre, wg, wu, wd, g_post, p, g_in, wpg, wpp, g_ple,
                 p_row0=0, tm=512, sub=256, tf=1408):
    T, D = x.shape
    F = wg.shape[1]
    assert F % tf == 0 and T % tm == 0 and tm % sub == 0 and p_row0 % tm == 0
    p_blk0 = p_row0 // tm
    vec = lambda g: g.reshape(1, D)
    resident = lambda w: pl.BlockSpec(w.shape, lambda i: (0, 0), pipeline_mode=pl.Buffered(1))
    return pl.pallas_call(
        functools.partial(_mix_ffn_ple_kernel, sub=sub, tf=tf),
        grid=(T // tm,),
        in_specs=[_row_spec(tm, o_a.shape[1], 1), _row_spec(tm, o_b.shape[1], 1),
                  resident(w_a), resident(w_b), _whole_spec((1, D), 1),
                  _row_spec(tm, D, 1), _whole_spec((1, D), 1),
                  resident(wg), resident(wu), resident(wd),
                  _whole_spec((1, D), 1),
                  pl.BlockSpec((tm, p.shape[1]), lambda i: (i + p_blk0, 0)), _whole_spec((1, D), 1),
                  resident(wpg), resident(wpp), _whole_spec((1, D), 1)],
        out_specs=_row_spec(tm, D, 1),
        out_shape=jax.ShapeDtypeStruct((T, D), F32),
        compiler_params=_cparams(("parallel",)),
        name="mix_ffn_ple",
    )(o_a, o_b, w_a, w_b, vec(g_mix), x, vec(g_pre), wg, wu, wd, vec(g_post), p, vec(g_in), wpg, wpp,
      vec(g_ple))


def _roll_rows(x, s, rev):
    n = x.shape[0]
    return pltpu.roll(x, (n - s) if rev else s, 0)


def _cumsum_rows(x, tau, rev):
    n = x.shape[0]
    s = 1
    while s < n:
        if s % SUBLANES:
            shifted = jnp.where(tau >= s, _roll_rows(x, s, rev), 0.0)
        else:
            zeros = jnp.zeros((s,) + x.shape[1:], x.dtype)
            shifted = (jnp.concatenate([x[s:], zeros], axis=0) if rev
                       else jnp.concatenate([zeros, x[:n - s]], axis=0))
        x = x + shifted
        s *= 2
    return x


def _hgrn_levels(C):
    return [C >> (i + 1) for i in range(C.bit_length() - 1)]


def _hgrn_pair_classes(C):
    t = np.arange(C)[:, None]
    s = np.arange(C)[None, :]
    out = np.full((2 * C, 2 * C), -1, np.int32)
    for d, rev in enumerate((False, True)):
        tau, sig = (C - 1 - t, C - 1 - s) if rev else (t, s)
        blk = np.full((C, C), -1, np.int32)
        blk[t == s] = 0
        for i, L in enumerate(_hgrn_levels(C)):
            m = ((t & -(2 * L)) == (s & -(2 * L))) & ((tau & (2 * L - 1)) >= L) & ((sig & (2 * L - 1)) < L)
            blk[m] = i + 1
        out[d * C:(d + 1) * C, d * C:(d + 1) * C] = blk
    return out


def _hgrn_kernel(q_ref, gf_ref, gb_ref, i_ref, gate_ref, cls_ref, ng_ref, o_ref, acc_scr, st_scr):
    S, DK = q_ref.shape
    C = A_CHUNK
    nc = S // C
    assert nc % (2 * A_UNROLL) == 0
    ng = ng_ref[...]
    row = lax.broadcasted_iota(jnp.int32, (C, DK), 0)
    levels = _hgrn_levels(C)
    zero_half = jnp.zeros((C, DK), BF16)

    def stack(top, bottom):
        return jnp.concatenate([top, bottom], axis=0)

    def block_diag(x):
        return stack(jnp.concatenate([x[:C], zero_half], axis=1), jnp.concatenate([zero_half, x[C:]], axis=1))

    def make_decay(rev):
        tau = (C - 1 - row) if rev else row
        odd_rank = (tau & 1) == 1
        last = 0 if rev else C - 1

        def ref_rows(b, L):
            off = L if rev else L - 1
            if 2 * L >= SUBLANES:
                pieces = [jnp.broadcast_to(b[j * 2 * L + off:j * 2 * L + off + 1, :], (2 * L, DK))
                          for j in range(C // (2 * L))]
                return pieces[0] if len(pieces) == 1 else jnp.concatenate(pieces, axis=0)
            if L == 1:
                return jnp.where(odd_rank, _roll_rows(b, 1, rev), b)
            b3 = b.reshape(C // SUBLANES, SUBLANES, DK)
            sub = lax.broadcasted_iota(jnp.int32, b3.shape, 1)
            out = None
            for j in range(SUBLANES // (2 * L)):
                piece = jnp.broadcast_to(b3[:, j * 2 * L + off:j * 2 * L + off + 1, :], b3.shape)
                out = piece if out is None else jnp.where(sub >= j * 2 * L, piece, out)
            return out.reshape(C, DK)

        def decay(g):
            b = _cumsum_rows(g, tau, rev)
            return b, [ref_rows(b, L) for L in levels], b[last:last + 1, :]

        return decay

    decay_fwd, decay_bwd = make_decay(False), make_decay(True)
    pair_cls = cls_ref[...]
    st_scr[...] = jnp.zeros_like(st_scr)

    def chunk_pair(rows_f, rows_b, st):
        q = stack(q_ref[rows_f, :], q_ref[rows_b, :]).astype(F32)
        g_f, g_b = gf_ref[rows_f, :], gb_ref[rows_b, :]
        k = 1.0 - jnp.exp(stack(g_f, g_b))
        vb = stack(i_ref[rows_f, :], i_ref[rows_b, :])
        b_f, refs_f, last_f = decay_fwd(g_f)
        b_b, refs_b, last_b = decay_bwd(g_b)
        b = stack(b_f, b_b)

        o = _dot_nt(block_diag((q * jnp.exp(b)).astype(BF16)), st.astype(BF16))

        attn = jnp.where(pair_cls == 0, _dot_nt(q.astype(BF16), k.astype(BF16)), 0.0)
        for i in range(len(levels)):
            e = jnp.exp(-jnp.abs(b - stack(refs_f[i], refs_b[i])))
            attn = jnp.where(pair_cls == i + 1, _dot_nt((q * e).astype(BF16), (k * e).astype(BF16)), attn)
        o = o + _dot(attn.astype(BF16), vb)

        b_last = stack(jnp.broadcast_to(last_f, (C, DK)), jnp.broadcast_to(last_b, (C, DK)))
        khat = block_diag((k * jnp.exp(b_last - b)).astype(BF16))
        keep = jnp.exp(jnp.concatenate([last_f, last_b], axis=1))
        return o, st * keep + _dot_tn(vb, khat)

    def make_body(final):
        def body(it, carry):
            st = st_scr[...]
            for u in range(A_UNROLL):
                ci = it * A_UNROLL + u
                rows_f = pl.ds(pl.multiple_of(ci * C, C), C)
                rows_b = pl.ds(pl.multiple_of((nc - 1 - ci) * C, C), C)
                o, st = chunk_pair(rows_f, rows_b, st)
                for rows, part in ((rows_f, o[:C]), (rows_b, o[C:])):
                    if final:
                        tot = acc_scr[rows, :] + part
                        o_ref[rows, :] = (_rms(tot, ng) * gate_ref[rows, :].astype(F32)).astype(o_ref.dtype)
                    else:
                        acc_scr[rows, :] = part
            st_scr[...] = st
            return carry

        return body

    trips = nc // A_UNROLL
    lax.fori_loop(0, trips // 2, make_body(False), 0)
    lax.fori_loop(trips // 2, trips, make_body(True), 0)


def _hgrn_mixer(q, g_fwd, g_bwd, v, gate, norm_g, batch, seq):
    T = q.shape[0]
    H, DK = A_HEADS, A_DK
    head = pl.BlockSpec((seq, DK), lambda b, h: (b, h))
    classes = jnp.asarray(_hgrn_pair_classes(A_CHUNK))
    return pl.pallas_call(
        _hgrn_kernel,
        grid=(batch, H),
        in_specs=[head, head, head, head, head,
                  pl.BlockSpec(classes.shape, lambda b, h: (0, 0)),
                  pl.BlockSpec((1, DK), lambda b, h: (0, 0))],
        out_specs=pl.BlockSpec((seq, DK), lambda b, h: (b, h)),
        out_shape=jax.ShapeDtypeStruct((T, H * DK), BF16),
        scratch_shapes=[pltpu.VMEM((seq, DK), F32), pltpu.VMEM((DK, 2 * DK), F32)],
        compiler_params=_cparams(("parallel", "parallel")),
        name="hgrn2",
    )(q, g_fwd, g_bwd, v, gate, classes, norm_g.reshape(1, DK))


def _expand_heads(col, j0, width):
    q = col.shape[0]
    lane = lax.broadcasted_iota(jnp.int32, (q, width), 1)
    out = jnp.broadcast_to(col[:, j0 + B_HPG - 1:j0 + B_HPG], (q, width))
    for j in range(B_HPG - 2, -1, -1):
        out = jnp.where(lane < (j + 1) * B_HEADDIM,
                        jnp.broadcast_to(col[:, j0 + j:j0 + j + 1], (q, width)), out)
    return out


def _ssd_kernel(z_ref, x_ref, bm_ref, cm_ref, dt_ref, cwx_ref, cwb_ref, cwc_ref,
                cbx_ref, cbb_ref, cbc_ref, hp_ref, dsk_ref, ng_ref, o_ref,
                xs_scr, b_scr, c_scr, y_scr, *st_scrs):
    S = x_ref.shape[0]
    Q = B_CHUNK
    nc = S // Q
    GW = B_GW
    hp = hp_ref[0]
    a_row, dtb_row = hp[0:1, :], hp[1:2, :]
    row = lax.broadcasted_iota(jnp.int32, (Q, LANES), 0)
    t2 = lax.broadcasted_iota(jnp.int32, (Q, Q), 0)
    s2 = lax.broadcasted_iota(jnp.int32, (Q, Q), 1)
    lane_gw = lax.broadcasted_iota(jnp.int32, (Q, GW), 1)
    head_mask = [(lane_gw >= j * B_HEADDIM) & (lane_gw < (j + 1) * B_HEADDIM) for j in range(B_HPG)]
    halo = SUBLANES

    def conv_body(c, carry):
        r0 = pl.multiple_of(c * Q, Q)
        rp = pl.multiple_of(jnp.maximum(r0 - halo, 0), halo)
        rn = pl.multiple_of(jnp.minimum(r0 + Q, S - halo), halo)
        has_prev = c > 0
        has_next = c < nc - 1

        def conv(src_ref, w_ref, bias_ref):
            prev = jnp.where(has_prev, src_ref[pl.ds(rp, halo), :].astype(F32), 0.0)
            nxt = jnp.where(has_next, src_ref[pl.ds(rn, halo), :].astype(F32), 0.0)
            xx = jnp.concatenate([prev, src_ref[pl.ds(r0, Q), :].astype(F32), nxt], axis=0)
            n = Q + 2 * halo
            w = w_ref[0]
            acc = None
            for j in range(B_CONV):
                delta = j - B_CONV // 2
                sh = xx if delta == 0 else pltpu.roll(xx, (-delta) % n, 0)
                term = sh[halo:halo + Q, :] * w[j:j + 1, :]
                acc = term if acc is None else acc + term
            return _silu(acc + bias_ref[0])

        xs_scr[pl.ds(r0, Q), :] = conv(x_ref, cwx_ref, cbx_ref)
        b_scr[pl.ds(r0, Q), :] = conv(bm_ref, cwb_ref, cbb_ref).astype(BF16)
        c_scr[pl.ds(r0, Q), :] = conv(cm_ref, cwc_ref, cbc_ref).astype(BF16)
        return carry

    lax.fori_loop(0, nc, conv_body, 0)

    def make_chunk(rev):
        tau = (Q - 1 - row) if rev else row
        last = 0 if rev else Q - 1
        j0 = B_HPG if rev else 0
        pair_ok = (s2 >= t2) if rev else (t2 >= s2)

        def chunk(rows, st):
            dt = jax.nn.softplus(dt_ref[rows, :] + dtb_row)
            cs = _cumsum_rows(dt * a_row, tau, rev)
            cs_t = cs.T
            xs = xs_scr[rows, :]
            bm = b_scr[rows, :]
            cm = c_scr[rows, :]
            xdt = xs * _expand_heads(dt, j0, GW)
            ecs = _expand_heads(cs, j0, GW)
            ecs_last = ecs[last:last + 1, :]

            y = _dot(cm, st.astype(BF16)) * jnp.exp(ecs)

            scores = _dot_nt(cm, bm)
            for j in range(B_HPG):
                col = jnp.broadcast_to(cs[:, j0 + j:j0 + j + 1], (Q, Q))
                rw = jnp.broadcast_to(cs_t[j0 + j:j0 + j + 1, :], (Q, Q))
                decay = jnp.where(pair_ok, jnp.exp(jnp.minimum(col - rw, 0.0)), 0.0)
                xh = jnp.where(head_mask[j], xdt, 0.0).astype(BF16)
                y = y + _dot((scores * decay).astype(BF16), xh)

            xdec = (xdt * jnp.exp(ecs_last - ecs)).astype(BF16)
            return y, st * jnp.exp(ecs_last) + _dot_tn(bm, xdec)

        return chunk

    chunk_fns = (make_chunk(False), make_chunk(True))
    for st_scr in st_scrs:
        st_scr[...] = jnp.zeros_like(st_scr)

    def make_body(final, unroll):
        def body(it, carry):
            for rev, chunk, st_scr in zip((False, True), chunk_fns, st_scrs):
                st = st_scr[...]
                for u in range(unroll):
                    ci = it * unroll + u
                    c = (nc - 1 - ci) if rev else ci
                    rows = pl.ds(pl.multiple_of(c * Q, Q), Q)
                    y, st = chunk(rows, st)
                    if final:
                        tot = y_scr[rows, :] + y + dsk_ref[0] * xs_scr[rows, :]
                        tot = tot * _silu(z_ref[rows, :].astype(F32))
                        o_ref[rows, :] = _rms(tot, ng_ref[0]).astype(o_ref.dtype)
                    else:
                        y_scr[rows, :] = y
                st_scr[...] = st
            return carry

        return body

    assert nc % (2 * B_UNROLL) == 0
    lax.fori_loop(0, nc // (2 * B_UNROLL), make_body(False, B_UNROLL), 0)
    lax.fori_loop(nc // 2, nc, make_body(True, 1), 0)


def _ssd_mixer(u, dt, conv_w, conv_b, a_log, dt_bias, d_skip, norm_g, batch, seq, col0):
    T = u.shape[0]
    G, GW, N = B_GROUPS, B_GW, B_STATE
    W = B_HEADS * B_HEADDIM
    assert col0 % GW == 0 and GW == 2 * N
    z_blk = col0 // GW
    x_blk = z_blk + W // GW
    b_blk = (col0 + 2 * W) // N
    c_blk = b_blk + G

    def pad_rows(w):
        return jnp.pad(w, ((0, SUBLANES - w.shape[0]), (0, 0)))

    cwx = pad_rows(conv_w[:, :W]).reshape(SUBLANES, G, GW).transpose(1, 0, 2)
    cwb = pad_rows(conv_w[:, W:W + G * N]).reshape(SUBLANES, G, N).transpose(1, 0, 2)
    cwc = pad_rows(conv_w[:, W + G * N:]).reshape(SUBLANES, G, N).transpose(1, 0, 2)
    cbx = conv_b[:W].reshape(G, 1, GW)
    cbb = conv_b[W:W + G * N].reshape(G, 1, N)
    cbc = conv_b[W + G * N:].reshape(G, 1, N)
    a_neg = -jnp.exp(a_log.astype(F32))
    per_dir = lambda v: jnp.concatenate([v[0].reshape(G, B_HPG), v[1].reshape(G, B_HPG)], axis=1)
    hp = jnp.stack([per_dir(a_neg), per_dir(dt_bias.astype(F32))], axis=1)
    hp = jnp.pad(hp, ((0, 0), (0, SUBLANES - 2), (0, LANES - 2 * B_HPG)))
    dsk = jnp.repeat(d_skip.astype(F32), B_HEADDIM).reshape(G, 1, GW)
    ng = norm_g.reshape(G, 1, GW)

    gspec = lambda shape: pl.BlockSpec((1,) + shape, lambda b, g: (g, 0, 0))
    return pl.pallas_call(
        _ssd_kernel,
        grid=(batch, G),
        in_specs=[pl.BlockSpec((seq, GW), lambda b, g: (b, z_blk + g)),
                  pl.BlockSpec((seq, GW), lambda b, g: (b, x_blk + g)),
                  pl.BlockSpec((seq, N), lambda b, g: (b, b_blk + g)),
                  pl.BlockSpec((seq, N), lambda b, g: (b, c_blk + g)),
                  pl.BlockSpec((seq, LANES), lambda b, g: (b, g)),
                  gspec((SUBLANES, GW)), gspec((SUBLANES, N)), gspec((SUBLANES, N)),
                  gspec((1, GW)), gspec((1, N)), gspec((1, N)),
                  gspec((SUBLANES, LANES)), gspec((1, GW)), gspec((1, GW))],
        out_specs=pl.BlockSpec((seq, GW), lambda b, g: (b, g)),
        out_shape=jax.ShapeDtypeStruct((T, W), BF16),
        scratch_shapes=[pltpu.VMEM((seq, GW), F32), pltpu.VMEM((seq, N), BF16),
                        pltpu.VMEM((seq, N), BF16), pltpu.VMEM((seq, GW), F32),
                        pltpu.VMEM((N, GW), F32), pltpu.VMEM((N, GW), F32)],
        compiler_params=_cparams(("parallel", "parallel")),
        name="ssd",
    )(u, u, u, u, dt, cwx, cwb, cwc, cbx, cbb, cbc, hp, dsk, ng)


def _na_kernel(q_ref, k_ref, v_ref, bias_ref, o_ref):
    S, HW = q_ref.shape
    W = GRID_W
    n_rows = S // W
    kh = min(NA_ROWS, n_rows)
    nk = kh * W
    n_pairs = bias_ref.shape[1] // NA_HG
    lane = lax.broadcasted_iota(jnp.int32, (W, HW), 1)
    head_masks = [(lane >= h * C_HEADDIM) & (lane < (h + 1) * C_HEADDIM) for h in range(NA_HG)]
    scale = C_HEADDIM ** -0.5

    def body(r, carry):
        rs = jnp.clip(r - kh // 2, 0, n_rows - kh)
        q = q_ref[pl.ds(pl.multiple_of(r * W, W), W), :] * jnp.asarray(scale, q_ref.dtype)
        kw = k_ref[pl.ds(pl.multiple_of(rs * W, W), nk), :]
        vw = v_ref[pl.ds(pl.multiple_of(rs * W, W), nk), :]
        zero = jnp.zeros_like(q)
        qs = jnp.concatenate([jnp.where(m, q, zero) for m in head_masks], axis=0)
        d0 = (NA_ROWS - 1) - (r - rs)
        bias = jnp.concatenate(
            [jnp.concatenate([bias_ref[0, h * n_pairs + d0 + 2 * m] for m in range(kh // 2)], axis=1)
             for h in range(NA_HG)], axis=0)
        s = _dot_nt(qs, kw) + bias
        m = jnp.max(s, axis=-1, keepdims=True)
        p = jnp.exp(s - m)
        l = jnp.sum(p, axis=-1, keepdims=True)
        res = _dot(p.astype(BF16), vw) / l
        out = jnp.zeros((W, HW), F32)
        for h in range(NA_HG):
            out = jnp.where(head_masks[h], res[h * W:(h + 1) * W, :], out)
        o_ref[pl.ds(pl.multiple_of(r * W, W), W), :] = out.astype(o_ref.dtype)
        return carry

    lax.fori_loop(0, n_rows, body, 0, unroll=NA_UNROLL)


def _na_bias_table(rpb):
    H, n_dr, n_dc = rpb.shape
    W = GRID_W
    c = np.arange(W)
    qs = np.clip(c - NA_COLS // 2, 0, W - NA_COLS)
    valid = (c[None, :] >= qs[:, None]) & (c[None, :] < qs[:, None] + NA_COLS)
    dc = np.clip(c[None, :] - c[:, None] + (NA_COLS - 1), 0, n_dc - 1)
    onehot = (dc[None] == np.arange(n_dc)[:, None, None]).astype(np.float32)
    both = jnp.stack([rpb[:, :-1], rpb[:, 1:]], axis=2).astype(F32)
    pairs = jnp.einsum("hrpd,dcx->hrcpx", both, jnp.asarray(onehot),
                       precision=lax.Precision.HIGHEST)
    pairs = jnp.where(jnp.asarray(valid)[None, None, :, None, :], pairs, -jnp.inf)
    return pairs.reshape(H // NA_HG, NA_HG * (n_dr - 1), W, 2 * W)


def _neighborhood_attention(qkv, rpb, batch, seq):
    T = qkv.shape[0]
    HW = NA_HG * C_HEADDIM
    n_hg = C_HEADS // NA_HG
    n_rows = seq // GRID_W
    assert n_rows >= NA_ROWS and NA_ROWS % 2 == 0
    table = _na_bias_table(rpb)
    return pl.pallas_call(
        _na_kernel,
        grid=(batch, n_hg),
        in_specs=[pl.BlockSpec((seq, HW), lambda b, g: (b, g)),
                  pl.BlockSpec((seq, HW), lambda b, g: (b, n_hg + g)),
                  pl.BlockSpec((seq, HW), lambda b, g: (b, 2 * n_hg + g)),
                  pl.BlockSpec((1,) + table.shape[1:], lambda b, g: (g, 0, 0, 0))],
        out_specs=pl.BlockSpec((seq, HW), lambda b, g: (b, g)),
        out_shape=jax.ShapeDtypeStruct((T, C_HEADS * C_HEADDIM), BF16),
        compiler_params=_cparams(("parallel", "parallel")),
        name="natten",
    )(qkv, qkv, qkv, table)


def _store_token_tiles(ref, val):
    tm, width = val.shape
    n = width // LANES
    for j in range(n):
        ref[pl.ds(j, tm, stride=n), :] = val[:, j * LANES:(j + 1) * LANES]


def _load_token_tiles(ref, n):
    tm = ref.shape[0] // n
    return jnp.concatenate([ref[pl.ds(j, tm, stride=n), :] for j in range(n)], axis=1)


def _proj_router_kernel(a_ref, w_ref, gmix_ref, x_ref, g_ref, wr_ref, x1_ref, o_ref, h_ref):
    x1 = x_ref[...] + _rms(_dot(a_ref[...], w_ref[...]), gmix_ref[...])
    x1_ref[...] = x1
    h = _rms(x1, g_ref[...])
    _store_token_tiles(h_ref, h)
    h_hi = h.astype(BF16)
    h_lo = (h - h_hi.astype(F32)).astype(BF16)
    both = _dot(h_hi, wr_ref[...])
    logits = both[:, :LANES] + both[:, LANES:] + _dot(h_lo, wr_ref[:, :LANES])
    lane = lax.broadcasted_iota(jnp.int32, logits.shape, 1)
    neg = -jnp.inf
    l1 = jnp.where(lane < N_EXPERTS, logits, neg)
    m1 = jnp.max(l1, axis=-1, keepdims=True)
    i1 = jnp.min(jnp.where(l1 == m1, lane, LANES), axis=-1, keepdims=True)
    l2 = jnp.where(lane == i1, neg, l1)
    m2 = jnp.max(l2, axis=-1, keepdims=True)
    i2 = jnp.min(jnp.where(l2 == m2, lane, LANES), axis=-1, keepdims=True)
    e = jnp.exp(m2 - m1)
    w1 = 1.0 / (1.0 + e)
    w2 = e / (1.0 + e)
    out = jnp.where(lane == 0, i1.astype(F32),
                    jnp.where(lane == 1, i2.astype(F32),
                              jnp.where(lane == 2, w1, jnp.where(lane == 3, w2, 0.0))))
    o_ref[...] = out


def _proj_router(a, w, g_mix, x, g, w_router, tm=512):
    T, D = x.shape
    n = D // LANES
    wr = jnp.pad(w_router.astype(F32), ((0, 0), (0, LANES - w_router.shape[1])))
    wr_hi = wr.astype(BF16)
    wr = jnp.concatenate([wr_hi, (wr - wr_hi.astype(F32)).astype(BF16)], axis=1)
    x1, route, h = pl.pallas_call(
        _proj_router_kernel,
        grid=(T // tm,),
        in_specs=[_row_spec(tm, a.shape[1], 1), _whole_spec(w.shape, 1), _whole_spec((1, D), 1),
                  _row_spec(tm, D, 1), _whole_spec((1, D), 1), _whole_spec((D, 2 * LANES), 1)],
        out_specs=[_row_spec(tm, D, 1), _row_spec(tm, LANES, 1), _row_spec(tm * n, LANES, 1)],
        out_shape=[jax.ShapeDtypeStruct((T, D), F32),
                   jax.ShapeDtypeStruct((T, LANES), F32),
                   jax.ShapeDtypeStruct((T * n, LANES), F32)],
        compiler_params=_cparams(("parallel",)),
        name="proj_router",
    )(a, w, g_mix.reshape(1, D), x, g.reshape(1, D), wr)
    return x1, route, h.reshape(T, n, LANES)


SC_CORES = 2
SC_SUBCORES = 16
SC_WORKERS = SC_CORES * SC_SUBCORES
SC_CHUNK = 32


def _sc_mesh():
    return plsc.VectorSubcoreMesh(core_axis_name="c", subcore_axis_name="s",
                                  num_cores=SC_CORES, num_subcores=SC_SUBCORES)


def _sc_index_blocks(idx):
    return idx.reshape(SC_WORKERS, -1, SC_CHUNK)


def _sc_scatter_tokens(h, dest, n_rows, row0=0):
    T = dest.shape[0]
    _, n, _ = h.shape
    per_worker = T // SC_WORKERS
    n_chunks = per_worker // SC_CHUNK
    assert per_worker * SC_WORKERS == T and n_chunks * SC_CHUNK == per_worker

    @functools.partial(
        pl.kernel, mesh=_sc_mesh(),
        out_type=jax.ShapeDtypeStruct((n_rows, n, LANES), h.dtype),
        scratch_types=[pltpu.VMEM((n_chunks, SC_CHUNK), jnp.int32),
                       pltpu.VMEM((n_chunks, SC_CHUNK), jnp.int32),
                       pltpu.VMEM((SC_CHUNK, n, LANES), h.dtype)],
        name="sc_scatter_tokens",
    )
    def scatter(h_hbm, d0_hbm, d1_hbm, o_hbm, i0_v, i1_v, rows_v):
        wid = lax.axis_index("s") * SC_CORES + lax.axis_index("c")
        pltpu.sync_copy(d0_hbm.at[wid], i0_v)
        pltpu.sync_copy(d1_hbm.at[wid], i1_v)

        @pl.loop(0, n_chunks)
        def _(j):
            pltpu.sync_copy(h_hbm.at[pl.ds(row0 + wid * per_worker + j * SC_CHUNK, SC_CHUNK)], rows_v)
            pltpu.sync_copy(rows_v, o_hbm.at[i0_v.at[j]])
            pltpu.sync_copy(rows_v, o_hbm.at[i1_v.at[j]])

    return scatter(h, _sc_index_blocks(dest[:, 0]), _sc_index_blocks(dest[:, 1]))


def _sc_gather_tokens(y, dest):
    T = dest.shape[0]
    _, n, _ = y.shape
    per_worker = T // SC_WORKERS
    n_chunks = per_worker // SC_CHUNK
    assert per_worker * SC_WORKERS == T and n_chunks * SC_CHUNK == per_worker
    out = jax.ShapeDtypeStruct((T, n, LANES), y.dtype)

    @functools.partial(
        pl.kernel, mesh=_sc_mesh(), out_type=(out, out),
        scratch_types=[pltpu.VMEM((n_chunks, SC_CHUNK), jnp.int32),
                       pltpu.VMEM((n_chunks, SC_CHUNK), jnp.int32),
                       pltpu.VMEM((SC_CHUNK, n, LANES), y.dtype)],
        name="sc_gather_tokens",
    )
    def gather(y_hbm, d0_hbm, d1_hbm, o0_hbm, o1_hbm, i0_v, i1_v, rows_v):
        wid = lax.axis_index("s") * SC_CORES + lax.axis_index("c")
        pltpu.sync_copy(d0_hbm.at[wid], i0_v)
        pltpu.sync_copy(d1_hbm.at[wid], i1_v)

        @pl.loop(0, n_chunks)
        def _(j):
            rows = pl.ds(wid * per_worker + j * SC_CHUNK, SC_CHUNK)
            pltpu.sync_copy(y_hbm.at[i0_v.at[j]], rows_v)
            pltpu.sync_copy(rows_v, o0_hbm.at[rows])
            pltpu.sync_copy(y_hbm.at[i1_v.at[j]], rows_v)
            pltpu.sync_copy(rows_v, o1_hbm.at[rows])

    return gather(y, _sc_index_blocks(dest[:, 0]), _sc_index_blocks(dest[:, 1]))


SC_PACK_PAIRS = 8
SC_LANES = 16


def _sc_pack_bf16_rows(w):
    R, C = w.shape
    pairs = R // 2
    per_worker = pairs // SC_WORKERS
    n_chunks = per_worker // SC_PACK_PAIRS
    assert n_chunks * SC_PACK_PAIRS * SC_WORKERS * 2 == R and C % SC_LANES == 0

    def round_bits(v):
        u = plsc.bitcast(v, jnp.int32)
        return u + 0x7FFF + (lax.shift_right_logical(u, 16) & 1)

    @functools.partial(
        pl.kernel, mesh=_sc_mesh(),
        out_type=jax.ShapeDtypeStruct((pairs, C), jnp.int32),
        scratch_types=[pltpu.VMEM((2 * SC_PACK_PAIRS, C), F32), pltpu.VMEM((SC_PACK_PAIRS, C), jnp.int32)],
        compiler_params=pltpu.CompilerParams(use_tc_tiling_on_sc=True, needs_layout_passes=False),
        name="sc_pack_bf16_rows",
    )
    def pack(w_hbm, o_hbm, in_v, out_v):
        wid = lax.axis_index("s") * SC_CORES + lax.axis_index("c")

        @pl.loop(0, n_chunks)
        def _(j):
            p0 = (wid * n_chunks + j) * SC_PACK_PAIRS
            pltpu.sync_copy(w_hbm.at[pl.ds(2 * p0, 2 * SC_PACK_PAIRS)], in_v)

            @pl.loop(0, C, step=SC_LANES)
            def _(c):
                cols = pl.ds(c, SC_LANES)
                for i in range(SC_PACK_PAIRS):
                    lo = lax.shift_right_logical(round_bits(in_v[2 * i, cols]), 16)
                    hi = round_bits(in_v[2 * i + 1, cols]) & jnp.int32(-65536)
                    out_v[i, cols] = lo | hi

            pltpu.sync_copy(out_v, o_hbm.at[pl.ds(p0, SC_PACK_PAIRS)])

    return pack(w)


def _moe_ffn_kernel(te_ref, na_ref, h_ref, wg_ref, wu_ref, wd_ref, o_ref, acc_scr, *, tf):
    i = pl.program_id(0)
    D = acc_scr.shape[1]
    F = wg_ref.shape[2]
    unpack = lambda packed: pltpu.bitcast(packed, BF16)

    @pl.when(i < na_ref[0])
    def _():
        h = _load_token_tiles(h_ref, D // LANES).astype(BF16)
        for f0 in range(0, F, tf):
            a = (_silu(_dot(h, unpack(wg_ref[0, :, f0:f0 + tf])))
                 * _dot(h, unpack(wu_ref[0, :, f0:f0 + tf]))).astype(BF16)
            part = _dot(a, unpack(wd_ref[0, f0 // 2:(f0 + tf) // 2, :]))
            if f0 == 0:
                acc_scr[...] = part
            else:
                acc_scr[...] += part
        _store_token_tiles(o_ref, acc_scr[...])

    @pl.when(i >= na_ref[0])
    def _():
        o_ref[...] = jnp.zeros_like(o_ref)


def _moe_ffn(hs, wg, wu, wd, tile_expert, n_active, tm=MOE_TM, tf=512):
    rows, n, _ = hs.shape
    E, half_d, F = wg.shape
    D = 2 * half_d
    nt = tile_expert.shape[0]
    assert rows == nt * tm and n * LANES == D and wd.shape == (E, F // 2, D)
    grid_spec = pltpu.PrefetchScalarGridSpec(
        num_scalar_prefetch=2,
        grid=(nt,),
        in_specs=[pl.BlockSpec((tm * n, LANES), lambda i, te, na: (i, 0)),
                  pl.BlockSpec((1, D // 2, F), lambda i, te, na: (te[i], 0, 0)),
                  pl.BlockSpec((1, D // 2, F), lambda i, te, na: (te[i], 0, 0)),
                  pl.BlockSpec((1, F // 2, D), lambda i, te, na: (te[i], 0, 0))],
        out_specs=pl.BlockSpec((tm * n, LANES), lambda i, te, na: (i, 0)),
        scratch_shapes=[pltpu.VMEM((tm, D), F32)],
    )
    y = pl.pallas_call(
        functools.partial(_moe_ffn_kernel, tf=tf),
        grid_spec=grid_spec,
        out_shape=jax.ShapeDtypeStruct((rows * n, LANES), F32),
        compiler_params=_cparams(("arbitrary",)),
        name="moe_ffn",
    )(tile_expert, n_active, hs.reshape(rows * n, LANES), wg, wu, wd)
    return y.reshape(rows, n, LANES)


def _combine_ple_kernel(y0_ref, y1_ref, route_ref, x_ref, g_ref, p_ref, gin_ref, wpg_ref, wpp_ref,
                        gple_ref, *rest):
    o_ref = rest[-1]
    n = x_ref.shape[1] // LANES
    route = route_ref[...]
    moe = route[:, 2:3] * _load_token_tiles(y0_ref, n) + route[:, 3:4] * _load_token_tiles(y1_ref, n)
    x2 = x_ref[...] + _rms(moe, g_ref[...])
    o_ref[...] = _ple_update(x2, p_ref[...], gin_ref, wpg_ref, wpp_ref, gple_ref)


def _combine_ple(y0, y1, route, x, g, p, g_in, wpg, wpp, g_ple, row0=0, prev=None, p_row0=0, tm=512):
    T, D = x.shape
    Tp, n, _ = y0.shape
    assert row0 % tm == 0 and Tp % tm == 0 and n * LANES == D and p_row0 % tm == 0
    blk0 = row0 // tm
    p_blk0 = (row0 + p_row0) // tm
    vec = lambda v: v.reshape(1, D)
    shifted = lambda width: pl.BlockSpec((tm, width), lambda i: (i + blk0, 0))
    in_specs = [_row_spec(tm * n, LANES, 1), _row_spec(tm * n, LANES, 1), shifted(LANES),
                shifted(D), _whole_spec((1, D), 1),
                pl.BlockSpec((tm, p.shape[1]), lambda i: (i + p_blk0, 0)), _whole_spec((1, D), 1),
                _whole_spec(wpg.shape, 1), _whole_spec(wpp.shape, 1), _whole_spec((1, D), 1)]
    args = [y0.reshape(Tp * n, LANES), y1.reshape(Tp * n, LANES), route, x, vec(g), p, vec(g_in), wpg, wpp,
            vec(g_ple)]
    aliases = {}
    if prev is not None:
        in_specs.append(pl.BlockSpec(memory_space=pl.ANY))
        args.append(prev)
        aliases = {len(args) - 1: 0}
    return pl.pallas_call(
        _combine_ple_kernel,
        grid=(Tp // tm,),
        in_specs=in_specs,
        out_specs=shifted(D),
        out_shape=jax.ShapeDtypeStruct((T, D), F32),
        input_output_aliases=aliases,
        compiler_params=_cparams(("parallel",)),
        name="combine_ple",
    )(*args)


def _moe_plan(route, tm=MOE_TM):
    T = route.shape[0]
    e = route[:, :TOP_K].astype(jnp.int32).reshape(-1)
    onehot = (e[:, None] == jnp.arange(N_EXPERTS, dtype=jnp.int32)[None, :]).astype(jnp.int32)
    csum = jnp.cumsum(onehot, axis=0)
    rank = jnp.sum(csum * onehot, axis=1) - 1
    counts = csum[-1]
    padded = ((counts + tm - 1) // tm) * tm
    ends = jnp.cumsum(padded)
    starts = ends - padded
    dest = jnp.sum(starts[None, :] * onehot, axis=1) + rank
    nt = (T * TOP_K) // tm + N_EXPERTS
    tile_start = jnp.arange(nt, dtype=jnp.int32) * tm
    tile_expert = jnp.minimum(jnp.sum((tile_start[:, None] >= ends[None, :]).astype(jnp.int32), axis=1),
                              N_EXPERTS - 1).astype(jnp.int32)
    n_active = (ends[-1] // tm).astype(jnp.int32).reshape(1)
    return tile_expert, n_active, dest.reshape(T, TOP_K).astype(jnp.int32)


def kernel(x, p, hgrn_lb_raw, e_norm_mix_pre, e_w_in, e_conv_w, e_conv_b, e_A_log, e_dt_bias, e_D, e_a_norm, e_b_norm, e_w_out, e_norm_mix_post, e_norm_ffn_pre, e_w_ffn_gate, e_w_ffn_up, e_w_ffn_down, e_norm_ffn_post, o_norm_mix_pre, o_w_qkv, o_rpb, o_w_out, o_norm_mix_post, o_norm_ffn_pre, o_w_router, o_w_exp_gate, o_w_exp_up, o_w_exp_down, o_norm_ffn_post, ple_norm_in, ple_w_gate, ple_w_proj, ple_norm_post):
    batch, seq, d_model = x.shape
    depth = p.shape[0]
    T = batch * seq
    xt = x.reshape(T, d_model)
    p_all = p.reshape(depth * T, -1)
    lb_all =jnp.cumsum(jax.nn.softmax(hgrn_lb_raw.astype(F32), axis=0), axis=0)

    a_kdim = A_HEADS * A_DK
    b_width = B_HEADS * B_HEADDIM
    conv_dim = b_width + 2 * B_GROUPS * B_STATE
    main_w = 5 * a_kdim + b_width + conv_dim

    def pack_experts(w):
        E, R, C = w.shape
        return _sc_pack_bf16_rows(w.reshape(E * R, C)).reshape(E, R // 2, C)

    for li in range(depth):
        j = li // 2
        ple =(ple_norm_in[li], ple_w_gate[li].astype(BF16), ple_w_proj[li].astype(BF16), ple_norm_post[li])
        if li % 2 == 0:
            w_in = e_w_in[j]
            dtf = w_in[:, main_w:main_w + B_HEADS].reshape(d_model, B_GROUPS, B_HPG)
            dtb = w_in[:, main_w + B_HEADS:].reshape(d_model, B_GROUPS, B_HPG)
            w_dt = jnp.pad(jnp.concatenate([dtf, dtb], axis=2), ((0, 0), (0, 0), (0, LANES - 2 * B_HPG)))
            w_pieces = [w_in[:, :main_w].astype(BF16), w_dt.reshape(d_model, B_GROUPS * LANES).astype(BF16)]
            outs = [(a_kdim, BF16, "silu"), (a_kdim, F32, "log_gate"), (a_kdim, F32, "log_gate"),
                    (a_kdim, BF16, None), (a_kdim, BF16, "silu"),
                    (b_width + conv_dim, BF16, None), (B_GROUPS * LANES, F32, None)]
            q_a, gf_a, gb_a, i_a, gate_a, u_b, dt = _norm_proj(
                xt, e_norm_mix_pre[j], w_pieces, outs, lb=lb_all[li])
            o_a = _hgrn_mixer(q_a, gf_a, gb_a, i_a, gate_a, e_a_norm[j], batch, seq)
            o_b = _ssd_mixer(u_b, dt, e_conv_w[j], e_conv_b[j], e_A_log[j], e_dt_bias[j], e_D[j],
                             e_b_norm[j], batch, seq, 0)
            w_out = e_w_out[j].astype(BF16)
            xt = _mix_ffn_ple(o_a, o_b, w_out[:a_kdim], w_out[a_kdim:], e_norm_mix_post[j], xt,
                              e_norm_ffn_pre[j], e_w_ffn_gate[j].astype(BF16), e_w_ffn_up[j].astype(BF16),
                              e_w_ffn_down[j].astype(BF16), e_norm_ffn_post[j], p_all, *ple, p_row0=li * T)
        else:
            (qkv,) = _norm_proj(xt, o_norm_mix_pre[j], o_w_qkv[j].astype(BF16),
                                [(3 * C_HEADS * C_HEADDIM, BF16, None)])
            o_c = _neighborhood_attention(qkv, o_rpb[j], batch, seq)
            xt, route, h = _proj_router(o_c, o_w_out[j].astype(BF16), o_norm_mix_post[j], xt,
                                        o_norm_ffn_pre[j], o_w_router[j])
            expert_w = [pack_experts(w[j]) for w in (o_w_exp_gate, o_w_exp_up, o_w_exp_down)]
            x_mid, xt = xt, None
            t_grp = T // MOE_GROUPS
            for s in range(MOE_GROUPS):
                tile_expert, n_active, dest = _moe_plan(route[s * t_grp:(s + 1) * t_grp])
                hs = _sc_scatter_tokens(h, dest, tile_expert.shape[0] * MOE_TM, row0=s * t_grp)
                ys = _moe_ffn(hs, *expert_w, tile_expert, n_active)
                y0, y1 = _sc_gather_tokens(ys, dest)
                xt = _combine_ple(y0, y1, route, x_mid, o_norm_ffn_post[j], p_all, *ple,
                                  row0=s * t_grp, prev=xt, p_row0=li * T)
    return xt.reshape(batch, seq, d_model)
```

```python
import functools
import math

import numpy as np
import jax
import jax.numpy as jnp
from jax import lax
from jax.experimental import pallas as pl
from jax.experimental.pallas import tpu as pltpu
from jax.experimental.pallas import tpu_sc as plsc

F32 = jnp.float32
BF16 = jnp.bfloat16
EPS = 1e-6

LANES = 128
SUBLANES = 8
VMEM_LIMIT_BYTES = 56 * 1024 * 1024

GRID_W = 64
A_HEADS, A_DK, A_CHUNK = 4, 128, 64
A_UNROLL = 8
B_HEADS, B_HEADDIM, B_GROUPS, B_STATE, B_CONV, B_CHUNK = 8, 64, 2, 128, 5, 128
B_UNROLL = 2
B_HPG = B_HEADS // B_GROUPS
B_GW = B_HPG * B_HEADDIM
C_HEADS, C_HEADDIM = 16, 64
NA_ROWS, NA_COLS = 8, 16
NA_HG = 4
NA_UNROLL = 16
N_EXPERTS, TOP_K = 8, 2
MOE_GROUPS = 2
MOE_TM = 512


def _cparams(sem):
    return pltpu.CompilerParams(dimension_semantics=sem, vmem_limit_bytes=VMEM_LIMIT_BYTES)


def _rms(x, g):
    return x * lax.rsqrt(jnp.mean(x * x, axis=-1, keepdims=True) + EPS) * g


def _silu(x):
    return x * jax.nn.sigmoid(x)


def _dot(a, b):
    return jnp.dot(a, b, preferred_element_type=F32)


def _dot_nt(a, b):
    return lax.dot_general(a, b, (((1,), (1,)), ((), ())), preferred_element_type=F32)


def _dot_tn(a, b):
    return lax.dot_general(a, b, (((0,), (0,)), ((), ())), preferred_element_type=F32)


def _staggered(stage_fns):
    live = list(stage_fns)
    step = 0
    done = [False] * len(live)
    while not all(done):
        for k, gen in enumerate(live):
            if step >= k and not done[k]:
                try:
                    next(gen)
                except StopIteration:
                    done[k] = True
        step += 1


def _norm_proj_kernel(x_ref, g_ref, lb_ref, *refs, acts, used_cols, col_chunk, sub):
    o_refs = refs[len(refs) - len(acts):]
    w_refs = list(refs[:len(refs) - len(acts)])

    def stages(r0):
        rows = pl.ds(r0, sub)
        h = _rms(x_ref[rows, :], g_ref[...]).astype(BF16)
        yield
        piece, off = 0, 0
        for o_ref, act in zip(o_refs, acts):
            n = o_ref.shape[1]
            if off == used_cols[piece]:
                piece, off = piece + 1, 0
            w_ref = w_refs[piece]
            for c0 in range(0, n, col_chunk):
                c1 = min(c0 + col_chunk, n)
                y = _dot(h, w_ref[:, off + c0:off + c1])
                if act == "silu":
                    y = _silu(y)
                elif act == "log_gate":
                    lb = lb_ref[:, c0:c1]
                    y = jnp.log(lb + (1.0 - lb) * jax.nn.sigmoid(y))
                o_ref[rows, c0:c1] = y.astype(o_ref.dtype)
                yield
            off += n

    _staggered([stages(r0) for r0 in range(0, x_ref.shape[0], sub)])


def _norm_proj(x, g, w, outs, lb=None, tm=512, sub=256, col_chunk=1024):
    T, D = x.shape
    pieces = list(w) if isinstance(w, (list, tuple)) else [(w, w.shape[1])]
    ws = [v for v, _ in pieces]
    used_cols = tuple(n for _, n in pieces)
    widths = [o[0] for o in outs]
    assert sum(widths) == sum(used_cols) and T % tm == 0
    if lb is None:
        lb = jnp.zeros((LANES,), F32)
    return pl.pallas_call(
        functools.partial(_norm_proj_kernel, acts=tuple(o[2] for o in outs), used_cols=used_cols,
                          col_chunk=col_chunk, sub=sub),
        grid=(T // tm,),
        in_specs=[_row_spec(tm, D, 1), _whole_spec((1, D), 1), _whole_spec((1, lb.shape[0]), 1)]
                 + [_whole_spec(v.shape, 1) for v in ws],
        out_specs=[_row_spec(tm, n, 1) for n in widths],
        out_shape=[jax.ShapeDtypeStruct((T, n), dt) for n, dt, _ in outs],
        compiler_params=_cparams(("parallel",)),
        name="norm_proj",
    )(x, g.reshape(1, D), lb.reshape(1, -1), *ws)


def _ple_update(x, p, gin_ref, wg_ref, wp_ref, gpost_ref):
    h = _rms(x, gin_ref[...]).astype(BF16)
    gate = jax.nn.sigmoid(_dot(h, wg_ref[...]))
    proj = _dot(p.astype(BF16), wp_ref[...])
    return x + _rms(gate * proj, gpost_ref[...])


def _row_spec(tm, width, n_grid):
    return pl.BlockSpec((tm, width), (lambda i: (i, 0)) if n_grid == 1 else (lambda i, j: (i, 0)))


def _whole_spec(shape, n_grid):
    zeros = (0,) * len(shape)
    return pl.BlockSpec(shape, (lambda i: zeros) if n_grid == 1 else (lambda i, j: zeros))


def _mix_ffn_ple_kernel(oa_ref, ob_ref, wa_ref, wb_ref, gmix_ref, x_ref, gpre_ref, wg_ref, wu_ref, wd_ref,
                        gpost_ref, p_ref, gin_ref, wpg_ref, wpp_ref, gple_ref, o_ref, *, sub, tf):
    tm = x_ref.shape[0]
    F = wg_ref.shape[1]
    def stages(r0):
        rows = pl.ds(r0, sub)
        mix = _dot(oa_ref[rows, :], wa_ref[...]) + _dot(ob_ref[rows, :], wb_ref[...])
        x1 = x_ref[rows, :] + _rms(mix, gmix_ref[...])
        h = _rms(x1, gpre_ref[...]).astype(BF16)
        yield
        acc = None
        for f0 in range(0, F, tf):
            a = (_silu(_dot(h, wg_ref[:, f0:f0 + tf])) * _dot(h, wu_ref[:, f0:f0 + tf])).astype(BF16)
            part = _dot(a, wd_ref[f0:f0 + tf, :])
            acc = part if acc is None else acc + part
            yield
        x2 = x1 + _rms(acc, gpost_ref[...])
        o_ref[rows, :] = _ple_update(x2, p_ref[rows, :], gin_ref, wpg_ref, wpp_ref, gple_ref)
        yield

    _staggered([stages(r0) for r0 in range(0, tm, sub)])


def _mix_ffn_ple(o_a, o_b, w_a, w_b, g_mix, x, g_pre, wg, wu, wd, g_post, p, g_in, wpg, wpp, g_ple,
                 p_row0=0, tm=512, sub=256, tf=1408):
    T, D = x.shape
    F = wg.shape[1]
    assert F % tf == 0 and T % tm == 0 and tm % sub == 0 and p_row0 % tm == 0
    p_blk0 = p_row0 // tm
    vec = lambda g: g.reshape(1, D)
    resident = lambda w: pl.BlockSpec(w.shape, lambda i: (0, 0), pipeline_mode=pl.Buffered(1))
    return pl.pallas_call(
        functools.partial(_mix_ffn_ple_kernel, sub=sub, tf=tf),
        grid=(T // tm,),
        in_specs=[_row_spec(tm, o_a.shape[1], 1), _row_spec(tm, o_b.shape[1], 1),
                  resident(w_a), resident(w_b), _whole_spec((1, D), 1),
                  _row_spec(tm, D, 1), _whole_spec((1, D), 1),
                  resident(wg), resident(wu), resident(wd),
                  _whole_spec((1, D), 1),
                  pl.BlockSpec((tm, p.shape[1]), lambda i: (i + p_blk0, 0)), _whole_spec((1, D), 1),
                  resident(wpg), resident(wpp), _whole_spec((1, D), 1)],
        out_specs=_row_spec(tm, D, 1),
        out_shape=jax.ShapeDtypeStruct((T, D), F32),
        compiler_params=_cparams(("parallel",)),
        name="mix_ffn_ple",
    )(o_a, o_b, w_a, w_b, vec(g_mix), x, vec(g_pre), wg, wu, wd, vec(g_post), p, vec(g_in), wpg, wpp,
      vec(g_ple))


def _roll_rows(x, s, rev):
    n = x.shape[0]
    return pltpu.roll(x, (n - s) if rev else s, 0)


def _cumsum_rows(x, tau, rev):
    n = x.shape[0]
    s = 1
    while s < n:
        if s % SUBLANES:
            shifted = jnp.where(tau >= s, _roll_rows(x, s, rev), 0.0)
        else:
            zeros = jnp.zeros((s,) + x.shape[1:], x.dtype)
            shifted = (jnp.concatenate([x[s:], zeros], axis=0) if rev
                       else jnp.concatenate([zeros, x[:n - s]], axis=0))
        x = x + shifted
        s *= 2
    return x


def _hgrn_levels(C):
    return [C >> (i + 1) for i in range(C.bit_length() - 1)]


def _hgrn_pair_classes(C):
    t = np.arange(C)[:, None]
    s = np.arange(C)[None, :]
    out = np.full((2 * C, 2 * C), -1, np.int32)
    for d, rev in enumerate((False, True)):
        tau, sig = (C - 1 - t, C - 1 - s) if rev else (t, s)
        blk = np.full((C, C), -1, np.int32)
        blk[t == s] = 0
        for i, L in enumerate(_hgrn_levels(C)):
            m = ((t & -(2 * L)) == (s & -(2 * L))) & ((tau & (2 * L - 1)) >= L) & ((sig & (2 * L - 1)) < L)
            blk[m] = i + 1
        out[d * C:(d + 1) * C, d * C:(d + 1) * C] = blk
    return out


def _hgrn_kernel(q_ref, gf_ref, gb_ref, i_ref, gate_ref, cls_ref, ng_ref, o_ref, acc_scr, st_scr):
    S, DK = q_ref.shape
    C = A_CHUNK
    nc = S // C
    assert nc % (2 * A_UNROLL) == 0
    ng = ng_ref[...]
    row = lax.broadcasted_iota(jnp.int32, (C, DK), 0)
    levels = _hgrn_levels(C)
    zero_half = jnp.zeros((C, DK), BF16)

    def stack(top, bottom):
        return jnp.concatenate([top, bottom], axis=0)

    def block_diag(x):
        return stack(jnp.concatenate([x[:C], zero_half], axis=1), jnp.concatenate([zero_half, x[C:]], axis=1))

    def make_decay(rev):
        tau = (C - 1 - row) if rev else row
        odd_rank = (tau & 1) == 1
        last = 0 if rev else C - 1

        def ref_rows(b, L):
            off = L if rev else L - 1
            if 2 * L >= SUBLANES:
                pieces = [jnp.broadcast_to(b[j * 2 * L + off:j * 2 * L + off + 1, :], (2 * L, DK))
                          for j in range(C // (2 * L))]
                return pieces[0] if len(pieces) == 1 else jnp.concatenate(pieces, axis=0)
            if L == 1:
                return jnp.where(odd_rank, _roll_rows(b, 1, rev), b)
            b3 = b.reshape(C // SUBLANES, SUBLANES, DK)
            sub = lax.broadcasted_iota(jnp.int32, b3.shape, 1)
            out = None
            for j in range(SUBLANES // (2 * L)):
                piece = jnp.broadcast_to(b3[:, j * 2 * L + off:j * 2 * L + off + 1, :], b3.shape)
                out = piece if out is None else jnp.where(sub >= j * 2 * L, piece, out)
            return out.reshape(C, DK)

        def decay(g):
            b = _cumsum_rows(g, tau, rev)
            return b, [ref_rows(b, L) for L in levels], b[last:last + 1, :]

        return decay

    decay_fwd, decay_bwd = make_decay(False), make_decay(True)
    pair_cls = cls_ref[...]
    st_scr[...] = jnp.zeros_like(st_scr)

    def chunk_pair(rows_f, rows_b, st):
        q = stack(q_ref[rows_f, :], q_ref[rows_b, :]).astype(F32)
        g_f, g_b = gf_ref[rows_f, :], gb_ref[rows_b, :]
        k = 1.0 - jnp.exp(stack(g_f, g_b))
        vb = stack(i_ref[rows_f, :], i_ref[rows_b, :])
        b_f, refs_f, last_f = decay_fwd(g_f)
        b_b, refs_b, last_b = decay_bwd(g_b)
        b = stack(b_f, b_b)

        o = _dot_nt(block_diag((q * jnp.exp(b)).astype(BF16)), st.astype(BF16))

        attn = jnp.where(pair_cls == 0, _dot_nt(q.astype(BF16), k.astype(BF16)), 0.0)
        for i in range(len(levels)):
            e = jnp.exp(-jnp.abs(b - stack(refs_f[i], refs_b[i])))
            attn = jnp.where(pair_cls == i + 1, _dot_nt((q * e).astype(BF16), (k * e).astype(BF16)), attn)
        o = o + _dot(attn.astype(BF16), vb)

        b_last = stack(jnp.broadcast_to(last_f, (C, DK)), jnp.broadcast_to(last_b, (C, DK)))
        khat = block_diag((k * jnp.exp(b_last - b)).astype(BF16))
        keep = jnp.exp(jnp.concatenate([last_f, last_b], axis=1))
        return o, st * keep + _dot_tn(vb, khat)

    def make_body(final):
        def body(it, carry):
            st = st_scr[...]
            for u in range(A_UNROLL):
                ci = it * A_UNROLL + u
                rows_f = pl.ds(pl.multiple_of(ci * C, C), C)
                rows_b = pl.ds(pl.multiple_of((nc - 1 - ci) * C, C), C)
                o, st = chunk_pair(rows_f, rows_b, st)
                for rows, part in ((rows_f, o[:C]), (rows_b, o[C:])):
                    if final:
                        tot = acc_scr[rows, :] + part
                        o_ref[rows, :] = (_rms(tot, ng) * gate_ref[rows, :].astype(F32)).astype(o_ref.dtype)
                    else:
                        acc_scr[rows, :] = part
            st_scr[...] = st
            return carry

        return body

    trips = nc // A_UNROLL
    lax.fori_loop(0, trips // 2, make_body(False), 0)
    lax.fori_loop(trips // 2, trips, make_body(True), 0)


def _hgrn_mixer(q, g_fwd, g_bwd, v, gate, norm_g, batch, seq):
    T = q.shape[0]
    H, DK = A_HEADS, A_DK
    head = pl.BlockSpec((seq, DK), lambda b, h: (b, h))
    classes = jnp.asarray(_hgrn_pair_classes(A_CHUNK))
    return pl.pallas_call(
        _hgrn_kernel,
        grid=(batch, H),
        in_specs=[head, head, head, head, head,
                  pl.BlockSpec(classes.shape, lambda b, h: (0, 0)),
                  pl.BlockSpec((1, DK), lambda b, h: (0, 0))],
        out_specs=pl.BlockSpec((seq, DK), lambda b, h: (b, h)),
        out_shape=jax.ShapeDtypeStruct((T, H * DK), BF16),
        scratch_shapes=[pltpu.VMEM((seq, DK), F32), pltpu.VMEM((DK, 2 * DK), F32)],
        compiler_params=_cparams(("parallel", "parallel")),
        name="hgrn2",
    )(q, g_fwd, g_bwd, v, gate, classes, norm_g.reshape(1, DK))


def _expand_heads(col, j0, width):
    q = col.shape[0]
    lane = lax.broadcasted_iota(jnp.int32, (q, width), 1)
    out = jnp.broadcast_to(col[:, j0 + B_HPG - 1:j0 + B_HPG], (q, width))
    for j in range(B_HPG - 2, -1, -1):
        out = jnp.where(lane < (j + 1) * B_HEADDIM,
                        jnp.broadcast_to(col[:, j0 + j:j0 + j + 1], (q, width)), out)
    return out


def _ssd_kernel(z_ref, x_ref, bm_ref, cm_ref, dt_ref, cwx_ref, cwb_ref, cwc_ref,
                cbx_ref, cbb_ref, cbc_ref, hp_ref, dsk_ref, ng_ref, o_ref,
                xs_scr, b_scr, c_scr, y_scr, *st_scrs):
    S = x_ref.shape[0]
    Q = B_CHUNK
    nc = S // Q
    GW = B_GW
    hp = hp_ref[0]
    a_row, dtb_row = hp[0:1, :], hp[1:2, :]
    row = lax.broadcasted_iota(jnp.int32, (Q, LANES), 0)
    t2 = lax.broadcasted_iota(jnp.int32, (Q, Q), 0)
    s2 = lax.broadcasted_iota(jnp.int32, (Q, Q), 1)
    lane_gw = lax.broadcasted_iota(jnp.int32, (Q, GW), 1)
    head_mask = [(lane_gw >= j * B_HEADDIM) & (lane_gw < (j + 1) * B_HEADDIM) for j in range(B_HPG)]
    halo = SUBLANES

    def conv_body(c, carry):
        r0 = pl.multiple_of(c * Q, Q)
        rp = pl.multiple_of(jnp.maximum(r0 - halo, 0), halo)
        rn = pl.multiple_of(jnp.minimum(r0 + Q, S - halo), halo)
        has_prev = c > 0
        has_next = c < nc - 1

        def conv(src_ref, w_ref, bias_ref):
            prev = jnp.where(has_prev, src_ref[pl.ds(rp, halo), :].astype(F32), 0.0)
            nxt = jnp.where(has_next, src_ref[pl.ds(rn, halo), :].astype(F32), 0.0)
            xx = jnp.concatenate([prev, src_ref[pl.ds(r0, Q), :].astype(F32), nxt], axis=0)
            n = Q + 2 * halo
            w = w_ref[0]
            acc = None
            for j in range(B_CONV):
                delta = j - B_CONV // 2
                sh = xx if delta == 0 else pltpu.roll(xx, (-delta) % n, 0)
                term = sh[halo:halo + Q, :] * w[j:j + 1, :]
                acc = term if acc is None else acc + term
            return _silu(acc + bias_ref[0])

        xs_scr[pl.ds(r0, Q), :] = conv(x_ref, cwx_ref, cbx_ref)
        b_scr[pl.ds(r0, Q), :] = conv(bm_ref, cwb_ref, cbb_ref).astype(BF16)
        c_scr[pl.ds(r0, Q), :] = conv(cm_ref, cwc_ref, cbc_ref).astype(BF16)
        return carry

    lax.fori_loop(0, nc, conv_body, 0)

    def make_chunk(rev):
        tau = (Q - 1 - row) if rev else row
        last = 0 if rev else Q - 1
        j0 = B_HPG if rev else 0
        pair_ok = (s2 >= t2) if rev else (t2 >= s2)

        def chunk(rows, st):
            dt = jax.nn.softplus(dt_ref[rows, :] + dtb_row)
            cs = _cumsum_rows(dt * a_row, tau, rev)
            cs_t = cs.T
            xs = xs_scr[rows, :]
            bm = b_scr[rows, :]
            cm = c_scr[rows, :]
            xdt = xs * _expand_heads(dt, j0, GW)
            ecs = _expand_heads(cs, j0, GW)
            ecs_last = ecs[last:last + 1, :]

            y = _dot(cm, st.astype(BF16)) * jnp.exp(ecs)

            scores = _dot_nt(cm, bm)
            for j in range(B_HPG):
                col = jnp.broadcast_to(cs[:, j0 + j:j0 + j + 1], (Q, Q))
                rw = jnp.broadcast_to(cs_t[j0 + j:j0 + j + 1, :], (Q, Q))
                decay = jnp.where(pair_ok, jnp.exp(jnp.minimum(col - rw, 0.0)), 0.0)
                xh = jnp.where(head_mask[j], xdt, 0.0).astype(BF16)
                y = y + _dot((scores * decay).astype(BF16), xh)

            xdec = (xdt * jnp.exp(ecs_last - ecs)).astype(BF16)
            return y, st * jnp.exp(ecs_last) + _dot_tn(bm, xdec)

        return chunk

    chunk_fns = (make_chunk(False), make_chunk(True))
    for st_scr in st_scrs:
        st_scr[...] = jnp.zeros_like(st_scr)

    def make_body(final, unroll):
        def body(it, carry):
            for rev, chunk, st_scr in zip((False, True), chunk_fns, st_scrs):
                st = st_scr[...]
                for u in range(unroll):
                    ci = it * unroll + u
                    c = (nc - 1 - ci) if rev else ci
                    rows = pl.ds(pl.multiple_of(c * Q, Q), Q)
                    y, st = chunk(rows, st)
                    if final:
                        tot = y_scr[rows, :] + y + dsk_ref[0] * xs_scr[rows, :]
                        tot = tot * _silu(z_ref[rows, :].astype(F32))
                        o_ref[rows, :] = _rms(tot, ng_ref[0]).astype(o_ref.dtype)
                    else:
                        y_scr[rows, :] = y
                st_scr[...] = st
            return carry

        return body

    assert nc % (2 * B_UNROLL) == 0
    lax.fori_loop(0, nc // (2 * B_UNROLL), make_body(False, B_UNROLL), 0)
    lax.fori_loop(nc // 2, nc, make_body(True, 1), 0)


def _ssd_mixer(u, dt, conv_w, conv_b, a_log, dt_bias, d_skip, norm_g, batch, seq, col0):
    T = u.shape[0]
    G, GW, N = B_GROUPS, B_GW, B_STATE
    W = B_HEADS * B_HEADDIM
    assert col0 % GW == 0 and GW == 2 * N
    z_blk = col0 // GW
    x_blk = z_blk + W // GW
    b_blk = (col0 + 2 * W) // N
    c_blk = b_blk + G

    def pad_rows(w):
        return jnp.pad(w, ((0, SUBLANES - w.shape[0]), (0, 0)))

    cwx = pad_rows(conv_w[:, :W]).reshape(SUBLANES, G, GW).transpose(1, 0, 2)
    cwb = pad_rows(conv_w[:, W:W + G * N]).reshape(SUBLANES, G, N).transpose(1, 0, 2)
    cwc = pad_rows(conv_w[:, W + G * N:]).reshape(SUBLANES, G, N).transpose(1, 0, 2)
    cbx = conv_b[:W].reshape(G, 1, GW)
    cbb = conv_b[W:W + G * N].reshape(G, 1, N)
    cbc = conv_b[W + G * N:].reshape(G, 1, N)
    a_neg = -jnp.exp(a_log.astype(F32))
    per_dir = lambda v: jnp.concatenate([v[0].reshape(G, B_HPG), v[1].reshape(G, B_HPG)], axis=1)
    hp = jnp.stack([per_dir(a_neg), per_dir(dt_bias.astype(F32))], axis=1)
    hp = jnp.pad(hp, ((0, 0), (0, SUBLANES - 2), (0, LANES - 2 * B_HPG)))
    dsk = jnp.repeat(d_skip.astype(F32), B_HEADDIM).reshape(G, 1, GW)
    ng = norm_g.reshape(G, 1, GW)

    gspec = lambda shape: pl.BlockSpec((1,) + shape, lambda b, g: (g, 0, 0))
    return pl.pallas_call(
        _ssd_kernel,
        grid=(batch, G),
        in_specs=[pl.BlockSpec((seq, GW), lambda b, g: (b, z_blk + g)),
                  pl.BlockSpec((seq, GW), lambda b, g: (b, x_blk + g)),
                  pl.BlockSpec((seq, N), lambda b, g: (b, b_blk + g)),
                  pl.BlockSpec((seq, N), lambda b, g: (b, c_blk + g)),
                  pl.BlockSpec((seq, LANES), lambda b, g: (b, g)),
                  gspec((SUBLANES, GW)), gspec((SUBLANES, N)), gspec((SUBLANES, N)),
                  gspec((1, GW)), gspec((1, N)), gspec((1, N)),
                  gspec((SUBLANES, LANES)), gspec((1, GW)), gspec((1, GW))],
        out_specs=pl.BlockSpec((seq, GW), lambda b, g: (b, g)),
        out_shape=jax.ShapeDtypeStruct((T, W), BF16),
        scratch_shapes=[pltpu.VMEM((seq, GW), F32), pltpu.VMEM((seq, N), BF16),
                        pltpu.VMEM((seq, N), BF16), pltpu.VMEM((seq, GW), F32),
                        pltpu.VMEM((N, GW), F32), pltpu.VMEM((N, GW), F32)],
        compiler_params=_cparams(("parallel", "parallel")),
        name="ssd",
    )(u, u, u, u, dt, cwx, cwb, cwc, cbx, cbb, cbc, hp, dsk, ng)


def _na_kernel(q_ref, k_ref, v_ref, bias_ref, o_ref):
    S, HW = q_ref.shape
    W = GRID_W
    n_rows = S // W
    kh = min(NA_ROWS, n_rows)
    nk = kh * W
    n_pairs = bias_ref.shape[1] // NA_HG
    lane = lax.broadcasted_iota(jnp.int32, (W, HW), 1)
    head_masks = [(lane >= h * C_HEADDIM) & (lane < (h + 1) * C_HEADDIM) for h in range(NA_HG)]
    scale = C_HEADDIM ** -0.5

    def body(r, carry):
        rs = jnp.clip(r - kh // 2, 0, n_rows - kh)
        q = q_ref[pl.ds(pl.multiple_of(r * W, W), W), :] * jnp.asarray(scale, q_ref.dtype)
        kw = k_ref[pl.ds(pl.multiple_of(rs * W, W), nk), :]
        vw = v_ref[pl.ds(pl.multiple_of(rs * W, W), nk), :]
        zero = jnp.zeros_like(q)
        qs = jnp.concatenate([jnp.where(m, q, zero) for m in head_masks], axis=0)
        d0 = (NA_ROWS - 1) - (r - rs)
        bias = jnp.concatenate(
            [jnp.concatenate([bias_ref[0, h * n_pairs + d0 + 2 * m] for m in range(kh // 2)], axis=1)
             for h in range(NA_HG)], axis=0)
        s = _dot_nt(qs, kw) + bias
        m = jnp.max(s, axis=-1, keepdims=True)
        p = jnp.exp(s - m)
        l = jnp.sum(p, axis=-1, keepdims=True)
        res = _dot(p.astype(BF16), vw) / l
        out = jnp.zeros((W, HW), F32)
        for h in range(NA_HG):
            out = jnp.where(head_masks[h], res[h * W:(h + 1) * W, :], out)
        o_ref[pl.ds(pl.multiple_of(r * W, W), W), :] = out.astype(o_ref.dtype)
        return carry

    lax.fori_loop(0, n_rows, body, 0, unroll=NA_UNROLL)


def _na_bias_table(rpb):
    H, n_dr, n_dc = rpb.shape
    W = GRID_W
    c = np.arange(W)
    qs = np.clip(c - NA_COLS // 2, 0, W - NA_COLS)
    valid = (c[None, :] >= qs[:, None]) & (c[None, :] < qs[:, None] + NA_COLS)
    dc = np.clip(c[None, :] - c[:, None] + (NA_COLS - 1), 0, n_dc - 1)
    onehot = np.zeros((2, n_dc, W, 2 * W), np.float32)
    for half in range(2):
        onehot[half, :, :, half * W:(half + 1) * W] = dc[None] == np.arange(n_dc)[:, None, None]
    both = jnp.concatenate([rpb[:, :-1], rpb[:, 1:]], axis=2).astype(F32)
    pairs = jnp.einsum("hre,ecx->hrcx", both, jnp.asarray(onehot.reshape(2 * n_dc, W, 2 * W)),
                       precision=lax.Precision.HIGHEST)
    valid2 = np.concatenate([valid, valid], axis=1)
    pairs = jnp.where(jnp.asarray(valid2)[None, None], pairs, -jnp.inf)
    return pairs.reshape(H // NA_HG, NA_HG * (n_dr - 1), W, 2 * W)


def _neighborhood_attention(qkv, rpb, batch, seq):
    T = qkv.shape[0]
    HW = NA_HG * C_HEADDIM
    n_hg = C_HEADS // NA_HG
    n_rows = seq // GRID_W
    assert n_rows >= NA_ROWS and NA_ROWS % 2 == 0
    table = _na_bias_table(rpb)
    return pl.pallas_call(
        _na_kernel,
        grid=(batch, n_hg),
        in_specs=[pl.BlockSpec((seq, HW), lambda b, g: (b, g)),
                  pl.BlockSpec((seq, HW), lambda b, g: (b, n_hg + g)),
                  pl.BlockSpec((seq, HW), lambda b, g: (b, 2 * n_hg + g)),
                  pl.BlockSpec((1,) + table.shape[1:], lambda b, g: (g, 0, 0, 0))],
        out_specs=pl.BlockSpec((seq, HW), lambda b, g: (b, g)),
        out_shape=jax.ShapeDtypeStruct((T, C_HEADS * C_HEADDIM), BF16),
        compiler_params=_cparams(("parallel", "parallel")),
        name="natten",
    )(qkv, qkv, qkv, table)


def _store_token_tiles(ref, val):
    tm, width = val.shape
    n = width // LANES
    for j in range(n):
        ref[pl.ds(j, tm, stride=n), :] = val[:, j * LANES:(j + 1) * LANES]


def _load_token_tiles(ref, n):
    tm = ref.shape[0] // n
    return jnp.concatenate([ref[pl.ds(j, tm, stride=n), :] for j in range(n)], axis=1)


def _proj_router_kernel(a_ref, w_ref, gmix_ref, x_ref, g_ref, wr_ref, x1_ref, o_ref, h_ref):
    x1 = x_ref[...] + _rms(_dot(a_ref[...], w_ref[...]), gmix_ref[...])
    x1_ref[...] = x1
    h = _rms(x1, g_ref[...])
    _store_token_tiles(h_ref, h)
    h_hi = h.astype(BF16)
    h_lo = (h - h_hi.astype(F32)).astype(BF16)
    both = _dot(h_hi, wr_ref[...])
    logits = both[:, :LANES] + both[:, LANES:] + _dot(h_lo, wr_ref[:, :LANES])
    lane = lax.broadcasted_iota(jnp.int32, logits.shape, 1)
    neg = -jnp.inf
    l1 = jnp.where(lane < N_EXPERTS, logits, neg)
    m1 = jnp.max(l1, axis=-1, keepdims=True)
    i1 = jnp.min(jnp.where(l1 == m1, lane, LANES), axis=-1, keepdims=True)
    l2 = jnp.where(lane == i1, neg, l1)
    m2 = jnp.max(l2, axis=-1, keepdims=True)
    i2 = jnp.min(jnp.where(l2 == m2, lane, LANES), axis=-1, keepdims=True)
    e = jnp.exp(m2 - m1)
    w1 = 1.0 / (1.0 + e)
    w2 = e / (1.0 + e)
    out = jnp.where(lane == 0, i1.astype(F32),
                    jnp.where(lane == 1, i2.astype(F32),
                              jnp.where(lane == 2, w1, jnp.where(lane == 3, w2, 0.0))))
    o_ref[...] = out


def _proj_router(a, w, g_mix, x, g, w_router, tm=512):
    T, D = x.shape
    n = D // LANES
    wr = jnp.pad(w_router.astype(F32), ((0, 0), (0, LANES - w_router.shape[1])))
    wr_hi = wr.astype(BF16)
    wr = jnp.concatenate([wr_hi, (wr - wr_hi.astype(F32)).astype(BF16)], axis=1)
    x1, route, h = pl.pallas_call(
        _proj_router_kernel,
        grid=(T // tm,),
        in_specs=[_row_spec(tm, a.shape[1], 1), _whole_spec(w.shape, 1), _whole_spec((1, D), 1),
                  _row_spec(tm, D, 1), _whole_spec((1, D), 1), _whole_spec((D, 2 * LANES), 1)],
        out_specs=[_row_spec(tm, D, 1), _row_spec(tm, LANES, 1), _row_spec(tm * n, LANES, 1)],
        out_shape=[jax.ShapeDtypeStruct((T, D), F32),
                   jax.ShapeDtypeStruct((T, LANES), F32),
                   jax.ShapeDtypeStruct((T * n, LANES), F32)],
        compiler_params=_cparams(("parallel",)),
        name="proj_router",
    )(a, w, g_mix.reshape(1, D), x, g.reshape(1, D), wr)
    return x1, route, h.reshape(T, n, LANES)


SC_CORES = 2
SC_SUBCORES = 16
SC_WORKERS = SC_CORES * SC_SUBCORES
SC_CHUNK = 32


def _sc_mesh():
    return plsc.VectorSubcoreMesh(core_axis_name="c", subcore_axis_name="s",
                                  num_cores=SC_CORES, num_subcores=SC_SUBCORES)


def _sc_index_blocks(idx):
    return idx.reshape(SC_WORKERS, -1, SC_CHUNK)


def _sc_scatter_tokens(h, dest, n_rows, row0=0):
    T = dest.shape[0]
    _, n, _ = h.shape
    per_worker = T // SC_WORKERS
    n_chunks = per_worker // SC_CHUNK
    assert per_worker * SC_WORKERS == T and n_chunks * SC_CHUNK == per_worker

    @functools.partial(
        pl.kernel, mesh=_sc_mesh(),
        out_type=jax.ShapeDtypeStruct((n_rows, n, LANES), h.dtype),
        scratch_types=[pltpu.VMEM((n_chunks, SC_CHUNK), jnp.int32),
                       pltpu.VMEM((n_chunks, SC_CHUNK), jnp.int32),
                       pltpu.VMEM((SC_CHUNK, n, LANES), h.dtype)],
        name="sc_scatter_tokens",
    )
    def scatter(h_hbm, d0_hbm, d1_hbm, o_hbm, i0_v, i1_v, rows_v):
        wid = lax.axis_index("s") * SC_CORES + lax.axis_index("c")
        pltpu.sync_copy(d0_hbm.at[wid], i0_v)
        pltpu.sync_copy(d1_hbm.at[wid], i1_v)

        @pl.loop(0, n_chunks)
        def _(j):
            pltpu.sync_copy(h_hbm.at[pl.ds(row0 + wid * per_worker + j * SC_CHUNK, SC_CHUNK)], rows_v)
            pltpu.sync_copy(rows_v, o_hbm.at[i0_v.at[j]])
            pltpu.sync_copy(rows_v, o_hbm.at[i1_v.at[j]])

    return scatter(h, _sc_index_blocks(dest[:, 0]), _sc_index_blocks(dest[:, 1]))


def _sc_gather_tokens(y, dest):
    T = dest.shape[0]
    _, n, _ = y.shape
    per_worker = T // SC_WORKERS
    n_chunks = per_worker // SC_CHUNK
    assert per_worker * SC_WORKERS == T and n_chunks * SC_CHUNK == per_worker
    out = jax.ShapeDtypeStruct((T, n, LANES), y.dtype)

    @functools.partial(
        pl.kernel, mesh=_sc_mesh(), out_type=(out, out),
        scratch_types=[pltpu.VMEM((n_chunks, SC_CHUNK), jnp.int32),
                       pltpu.VMEM((n_chunks, SC_CHUNK), jnp.int32),
                       pltpu.VMEM((SC_CHUNK, n, LANES), y.dtype)],
        name="sc_gather_tokens",
    )
    def gather(y_hbm, d0_hbm, d1_hbm, o0_hbm, o1_hbm, i0_v, i1_v, rows_v):
        wid = lax.axis_index("s") * SC_CORES + lax.axis_index("c")
        pltpu.sync_copy(d0_hbm.at[wid], i0_v)
        pltpu.sync_copy(d1_hbm.at[wid], i1_v)

        @pl.loop(0, n_chunks)
        def _(j):
            rows = pl.ds(wid * per_worker + j * SC_CHUNK, SC_CHUNK)
            pltpu.sync_copy(y_hbm.at[i0_v.at[j]], rows_v)
            pltpu.sync_copy(rows_v, o0_hbm.at[rows])
            pltpu.sync_copy(y_hbm.at[i1_v.at[j]], rows_v)
            pltpu.sync_copy(rows_v, o1_hbm.at[rows])

    return gather(y, _sc_index_blocks(dest[:, 0]), _sc_index_blocks(dest[:, 1]))


SC_PACK_PAIRS = 8
SC_LANES = 16


def _sc_pack_bf16_rows(w):
    R, C = w.shape
    pairs = R // 2
    per_worker = pairs // SC_WORKERS
    n_chunks = per_worker // SC_PACK_PAIRS
    assert n_chunks * SC_PACK_PAIRS * SC_WORKERS * 2 == R and C % SC_LANES == 0

    def round_bits(v):
        u = plsc.bitcast(v, jnp.int32)
        return u + 0x7FFF + (lax.shift_right_logical(u, 16) & 1)

    @functools.partial(
        pl.kernel, mesh=_sc_mesh(),
        out_type=jax.ShapeDtypeStruct((pairs, C), jnp.int32),
        scratch_types=[pltpu.VMEM((2 * SC_PACK_PAIRS, C), F32), pltpu.VMEM((SC_PACK_PAIRS, C), jnp.int32)],
        compiler_params=pltpu.CompilerParams(use_tc_tiling_on_sc=True, needs_layout_passes=False),
        name="sc_pack_bf16_rows",
    )
    def pack(w_hbm, o_hbm, in_v, out_v):
        wid = lax.axis_index("s") * SC_CORES + lax.axis_index("c")

        @pl.loop(0, n_chunks)
        def _(j):
            p0 = (wid * n_chunks + j) * SC_PACK_PAIRS
            pltpu.sync_copy(w_hbm.at[pl.ds(2 * p0, 2 * SC_PACK_PAIRS)], in_v)

            @pl.loop(0, C, step=SC_LANES)
            def _(c):
                cols = pl.ds(c, SC_LANES)
                for i in range(SC_PACK_PAIRS):
                    lo = lax.shift_right_logical(round_bits(in_v[2 * i, cols]), 16)
                    hi = round_bits(in_v[2 * i + 1, cols]) & jnp.int32(-65536)
                    out_v[i, cols] = lo | hi

            pltpu.sync_copy(out_v, o_hbm.at[pl.ds(p0, SC_PACK_PAIRS)])

    return pack(w)


def _moe_ffn_kernel(te_ref, na_ref, h_ref, wg_ref, wu_ref, wd_ref, o_ref, acc_scr, *, tf):
    i = pl.program_id(0)
    D = acc_scr.shape[1]
    F = wg_ref.shape[2]
    unpack = lambda packed: pltpu.bitcast(packed, BF16)

    @pl.when(i < na_ref[0])
    def _():
        h = _load_token_tiles(h_ref, D // LANES).astype(BF16)
        for f0 in range(0, F, tf):
            a = (_silu(_dot(h, unpack(wg_ref[0, :, f0:f0 + tf])))
                 * _dot(h, unpack(wu_ref[0, :, f0:f0 + tf]))).astype(BF16)
            part = _dot(a, unpack(wd_ref[0, f0 // 2:(f0 + tf) // 2, :]))
            if f0 == 0:
                acc_scr[...] = part
            else:
                acc_scr[...] += part
        _store_token_tiles(o_ref, acc_scr[...])

    @pl.when(i >= na_ref[0])
    def _():
        o_ref[...] = jnp.zeros_like(o_ref)


def _moe_ffn(hs, wg, wu, wd, tile_expert, n_active, tm=MOE_TM, tf=512):
    rows, n, _ = hs.shape
    E, half_d, F = wg.shape
    D = 2 * half_d
    nt = tile_expert.shape[0]
    assert rows == nt * tm and n * LANES == D and wd.shape == (E, F // 2, D)
    grid_spec = pltpu.PrefetchScalarGridSpec(
        num_scalar_prefetch=2,
        grid=(nt,),
        in_specs=[pl.BlockSpec((tm * n, LANES), lambda i, te, na: (i, 0)),
                  pl.BlockSpec((1, D // 2, F), lambda i, te, na: (te[i], 0, 0)),
                  pl.BlockSpec((1, D // 2, F), lambda i, te, na: (te[i], 0, 0)),
                  pl.BlockSpec((1, F // 2, D), lambda i, te, na: (te[i], 0, 0))],
        out_specs=pl.BlockSpec((tm * n, LANES), lambda i, te, na: (i, 0)),
        scratch_shapes=[pltpu.VMEM((tm, D), F32)],
    )
    y = pl.pallas_call(
        functools.partial(_moe_ffn_kernel, tf=tf),
        grid_spec=grid_spec,
        out_shape=jax.ShapeDtypeStruct((rows * n, LANES), F32),
        compiler_params=_cparams(("arbitrary",)),
        name="moe_ffn",
    )(tile_expert, n_active, hs.reshape(rows * n, LANES), wg, wu, wd)
    return y.reshape(rows, n, LANES)


def _combine_ple_kernel(y0_ref, y1_ref, route_ref, x_ref, g_ref, p_ref, gin_ref, wpg_ref, wpp_ref,
                        gple_ref, *rest):
    o_ref = rest[-1]
    n = x_ref.shape[1] // LANES
    route = route_ref[...]
    moe = route[:, 2:3] * _load_token_tiles(y0_ref, n) + route[:, 3:4] * _load_token_tiles(y1_ref, n)
    x2 = x_ref[...] + _rms(moe, g_ref[...])
    o_ref[...] = _ple_update(x2, p_ref[...], gin_ref, wpg_ref, wpp_ref, gple_ref)


def _combine_ple(y0, y1, route, x, g, p, g_in, wpg, wpp, g_ple, row0=0, prev=None, p_row0=0, tm=512):
    T, D = x.shape
    Tp, n, _ = y0.shape
    assert row0 % tm == 0 and Tp % tm == 0 and n * LANES == D and p_row0 % tm == 0
    blk0 = row0 // tm
    p_blk0 = (row0 + p_row0) // tm
    vec = lambda v: v.reshape(1, D)
    shifted = lambda width: pl.BlockSpec((tm, width), lambda i: (i + blk0, 0))
    in_specs = [_row_spec(tm * n, LANES, 1), _row_spec(tm * n, LANES, 1), shifted(LANES),
                shifted(D), _whole_spec((1, D), 1),
                pl.BlockSpec((tm, p.shape[1]), lambda i: (i + p_blk0, 0)), _whole_spec((1, D), 1),
                _whole_spec(wpg.shape, 1), _whole_spec(wpp.shape, 1), _whole_spec((1, D), 1)]
    args = [y0.reshape(Tp * n, LANES), y1.reshape(Tp * n, LANES), route, x, vec(g), p, vec(g_in), wpg, wpp,
            vec(g_ple)]
    aliases = {}
    if prev is not None:
        in_specs.append(pl.BlockSpec(memory_space=pl.ANY))
        args.append(prev)
        aliases = {len(args) - 1: 0}
    return pl.pallas_call(
        _combine_ple_kernel,
        grid=(Tp // tm,),
        in_specs=in_specs,
        out_specs=shifted(D),
        out_shape=jax.ShapeDtypeStruct((T, D), F32),
        input_output_aliases=aliases,
        compiler_params=_cparams(("parallel",)),
        name="combine_ple",
    )(*args)


def _moe_plan(route, tm=MOE_TM):
    T = route.shape[0]
    e = route[:, :TOP_K].astype(jnp.int32).reshape(-1)
    onehot = (e[:, None] == jnp.arange(N_EXPERTS, dtype=jnp.int32)[None, :]).astype(jnp.int32)
    csum = jnp.cumsum(onehot, axis=0)
    rank = jnp.sum(csum * onehot, axis=1) - 1
    counts = csum[-1]
    padded = ((counts + tm - 1) // tm) * tm
    ends = jnp.cumsum(padded)
    starts = ends - padded
    dest = jnp.sum(starts[None, :] * onehot, axis=1) + rank
    nt = (T * TOP_K) // tm + N_EXPERTS
    tile_start = jnp.arange(nt, dtype=jnp.int32) * tm
    tile_expert = jnp.minimum(jnp.sum((tile_start[:, None] >= ends[None, :]).astype(jnp.int32), axis=1),
                              N_EXPERTS - 1).astype(jnp.int32)
    n_active = (ends[-1] // tm).astype(jnp.int32).reshape(1)
    return tile_expert, n_active, dest.reshape(T, TOP_K).astype(jnp.int32)


def kernel(x, p, hgrn_lb_raw, e_norm_mix_pre, e_w_in, e_conv_w, e_conv_b, e_A_log, e_dt_bias, e_D, e_a_norm, e_b_norm, e_w_out, e_norm_mix_post, e_norm_ffn_pre, e_w_ffn_gate, e_w_ffn_up, e_w_ffn_down, e_norm_ffn_post, o_norm_mix_pre, o_w_qkv, o_rpb, o_w_out, o_norm_mix_post, o_norm_ffn_pre, o_w_router, o_w_exp_gate, o_w_exp_up, o_w_exp_down, o_norm_ffn_post, ple_norm_in, ple_w_gate, ple_w_proj, ple_norm_post):
    batch, seq, d_model = x.shape
    depth = p.shape[0]
    T = batch * seq
    xt = x.reshape(T, d_model)
    p_all = p.reshape(depth * T, -1)
    lb_all =jnp.cumsum(jax.nn.softmax(hgrn_lb_raw.astype(F32), axis=0), axis=0)

    a_kdim = A_HEADS * A_DK
    b_width = B_HEADS * B_HEADDIM
    conv_dim = b_width + 2 * B_GROUPS * B_STATE
    main_w = 5 * a_kdim + b_width + conv_dim

    def pack_experts(w):
        E, R, C = w.shape
        return _sc_pack_bf16_rows(w.reshape(E * R, C)).reshape(E, R // 2, C)

    for li in range(depth):
        j = li // 2
        ple =(ple_norm_in[li], ple_w_gate[li].astype(BF16), ple_w_proj[li].astype(BF16), ple_norm_post[li])
        if li % 2 == 0:
            w_in = e_w_in[j]
            dtf = w_in[:, main_w:main_w + B_HEADS].reshape(d_model, B_GROUPS, B_HPG)
            dtb = w_in[:, main_w + B_HEADS:].reshape(d_model, B_GROUPS, B_HPG)
            w_dt = jnp.pad(jnp.concatenate([dtf, dtb], axis=2), ((0, 0), (0, 0), (0, LANES - 2 * B_HPG)))
            w_pieces = [(w_in.astype(BF16), main_w),
                        (w_dt.reshape(d_model, B_GROUPS * LANES).astype(BF16), B_GROUPS * LANES)]
            outs = [(a_kdim, BF16, "silu"), (a_kdim, F32, "log_gate"), (a_kdim, F32, "log_gate"),
                    (a_kdim, BF16, None), (a_kdim, BF16, "silu"),
                    (b_width + conv_dim, BF16, None), (B_GROUPS * LANES, F32, None)]
            q_a, gf_a, gb_a, i_a, gate_a, u_b, dt = _norm_proj(
                xt, e_norm_mix_pre[j], w_pieces, outs, lb=lb_all[li])
            o_a = _hgrn_mixer(q_a, gf_a, gb_a, i_a, gate_a, e_a_norm[j], batch, seq)
            o_b = _ssd_mixer(u_b, dt, e_conv_w[j], e_conv_b[j], e_A_log[j], e_dt_bias[j], e_D[j],
                             e_b_norm[j], batch, seq, 0)
            w_out = e_w_out[j].astype(BF16)
            xt = _mix_ffn_ple(o_a, o_b, w_out[:a_kdim], w_out[a_kdim:], e_norm_mix_post[j], xt,
                              e_norm_ffn_pre[j], e_w_ffn_gate[j].astype(BF16), e_w_ffn_up[j].astype(BF16),
                              e_w_ffn_down[j].astype(BF16), e_norm_ffn_post[j], p_all, *ple, p_row0=li * T)
        else:
            (qkv,) = _norm_proj(xt, o_norm_mix_pre[j], o_w_qkv[j].astype(BF16),
                                [(3 * C_HEADS * C_HEADDIM, BF16, None)])
            o_c = _neighborhood_attention(qkv, o_rpb[j], batch, seq)
            xt, route, h = _proj_router(o_c, o_w_out[j].astype(BF16), o_norm_mix_post[j], xt,
                                        o_norm_ffn_pre[j], o_w_router[j])
            expert_w = [pack_experts(w[j]) for w in (o_w_exp_gate, o_w_exp_up, o_w_exp_down)]
            x_mid, xt = xt, None
            t_grp = T // MOE_GROUPS
            for s in range(MOE_GROUPS):
                tile_expert, n_active, dest = _moe_plan(route[s * t_grp:(s + 1) * t_grp])
                hs = _sc_scatter_tokens(h, dest, tile_expert.shape[0] * MOE_TM, row0=s * t_grp)
                ys = _moe_ffn(hs, *expert_w, tile_expert, n_active)
                y0, y1 = _sc_gather_tokens(ys, dest)
                xt = _combine_ple(y0, y1, route, x_mid, o_norm_ffn_post[j], p_all, *ple,
                                  row0=s * t_grp, prev=xt, p_row0=li * T)
    return xt.reshape(batch, seq, d_model)
```

```python
import functools
import math

import numpy as np
import jax
import jax.numpy as jnp
from jax import lax
from jax.experimental import pallas as pl
from jax.experimental.pallas import tpu as pltpu
from jax.experimental.pallas import tpu_sc as plsc

F32 = jnp.float32
BF16 = jnp.bfloat16
EPS = 1e-6

LANES = 128
SUBLANES = 8
VMEM_LIMIT_BYTES = 56 * 1024 * 1024

GRID_W = 64
A_HEADS, A_DK, A_CHUNK = 4, 128, 64
A_UNROLL = 8
B_HEADS, B_HEADDIM, B_GROUPS, B_STATE, B_CONV, B_CHUNK = 8, 64, 2, 128, 5, 128
B_UNROLL = 2
B_HPG = B_HEADS // B_GROUPS
B_GW = B_HPG * B_HEADDIM
C_HEADS, C_HEADDIM = 16, 64
NA_ROWS, NA_COLS = 8, 16
NA_HG = 4
NA_UNROLL = 16
N_EXPERTS, TOP_K = 8, 2
MOE_GROUPS = 2
MOE_TM = 512


def _cparams(sem):
    return pltpu.CompilerParams(dimension_semantics=sem, vmem_limit_bytes=VMEM_LIMIT_BYTES)


def _rms(x, g):
    return x * lax.rsqrt(jnp.mean(x * x, axis=-1, keepdims=True) + EPS) * g


def _silu(x):
    return x * jax.nn.sigmoid(x)


def _dot(a, b):
    return jnp.dot(a, b, preferred_element_type=F32)


def _dot_nt(a, b):
    return lax.dot_general(a, b, (((1,), (1,)), ((), ())), preferred_element_type=F32)


def _dot_tn(a, b):
    return lax.dot_general(a, b, (((0,), (0,)), ((), ())), preferred_element_type=F32)


def _staggered(stage_fns):
    live = list(stage_fns)
    step = 0
    done = [False] * len(live)
    while not all(done):
        for k, gen in enumerate(live):
            if step >= k and not done[k]:
                try:
                    next(gen)
                except StopIteration:
                    done[k] = True
        step += 1


def _norm_proj_kernel(x_ref, g_ref, lb_ref, *refs, acts, used_cols, col_chunk, sub):
    o_refs = refs[len(refs) - len(acts):]
    w_refs = list(refs[:len(refs) - len(acts)])

    def stages(r0):
        rows = pl.ds(r0, sub)
        h = _rms(x_ref[rows, :], g_ref[...]).astype(BF16)
        yield
        piece, off = 0, 0
        for o_ref, act in zip(o_refs, acts):
            n = o_ref.shape[1]
            if off == used_cols[piece]:
                piece, off = piece + 1, 0
            w_ref = w_refs[piece]
            for c0 in range(0, n, col_chunk):
                c1 = min(c0 + col_chunk, n)
                y = _dot(h, w_ref[:, off + c0:off + c1])
                if act == "silu":
                    y = _silu(y)
                elif act == "log_gate":
                    lb = lb_ref[:, c0:c1]
                    y = jnp.log(lb + (1.0 - lb) * jax.nn.sigmoid(y))
                o_ref[rows, c0:c1] = y.astype(o_ref.dtype)
                yield
            off += n

    _staggered([stages(r0) for r0 in range(0, x_ref.shape[0], sub)])


def _norm_proj(x, g, w, outs, lb=None, tm=512, sub=256, col_chunk=1024):
    T, D = x.shape
    pieces = list(w) if isinstance(w, (list, tuple)) else [(w, w.shape[1])]
    ws = [v for v, _ in pieces]
    used_cols = tuple(n for _, n in pieces)
    widths = [o[0] for o in outs]
    assert sum(widths) == sum(used_cols) and T % tm == 0
    if lb is None:
        lb = jnp.zeros((LANES,), F32)
    return pl.pallas_call(
        functools.partial(_norm_proj_kernel, acts=tuple(o[2] for o in outs), used_cols=used_cols,
                          col_chunk=col_chunk, sub=sub),
        grid=(T // tm,),
        in_specs=[_row_spec(tm, D, 1), _whole_spec((1, D), 1), _whole_spec((1, lb.shape[0]), 1)]
                 + [_whole_spec(v.shape, 1) for v in ws],
        out_specs=[_row_spec(tm, n, 1) for n in widths],
        out_shape=[jax.ShapeDtypeStruct((T, n), dt) for n, dt, _ in outs],
        compiler_params=_cparams(("parallel",)),
        name="norm_proj",
    )(x, g.reshape(1, D), lb.reshape(1, -1), *ws)


def _ple_update(x, p, gin_ref, wg_ref, wp_ref, gpost_ref):
    h = _rms(x, gin_ref[...]).astype(BF16)
    gate = jax.nn.sigmoid(_dot(h, wg_ref[...]))
    proj = _dot(p.astype(BF16), wp_ref[...])
    return x + _rms(gate * proj, gpost_ref[...])


def _row_spec(tm, width, n_grid):
    return pl.BlockSpec((tm, width), (lambda i: (i, 0)) if n_grid == 1 else (lambda i, j: (i, 0)))


def _whole_spec(shape, n_grid):
    zeros = (0,) * len(shape)
    return pl.BlockSpec(shape, (lambda i: zeros) if n_grid == 1 else (lambda i, j: zeros))


def _mix_ffn_ple_kernel(oa_ref, ob_ref, wa_ref, wb_ref, gmix_ref, x_ref, gpre_ref, wg_ref, wu_ref, wd_ref,
                        gpost_ref, p_ref, gin_ref, wpg_ref, wpp_ref, gple_ref, o_ref, *, sub, tf):
    tm = x_ref.shape[0]
    F = wg_ref.shape[1]
    def stages(r0):
        rows = pl.ds(r0, sub)
        mix = _dot(oa_ref[rows, :], wa_ref[...]) + _dot(ob_ref[rows, :], wb_ref[...])
        x1 = x_ref[rows, :] + _rms(mix, gmix_ref[...])
        h = _rms(x1, gpre_ref[...]).astype(BF16)
        yield
        acc = None
        for f0 in range(0, F, tf):
            a = (_silu(_dot(h, wg_ref[:, f0:f0 + tf])) * _dot(h, wu_ref[:, f0:f0 + tf])).astype(BF16)
            part = _dot(a, wd_ref[f0:f0 + tf, :])
            acc = part if acc is None else acc + part
            yield
        x2 = x1 + _rms(acc, gpost_ref[...])
        o_ref[rows, :] = _ple_update(x2, p_ref[rows, :], gin_ref, wpg_ref, wpp_ref, gple_ref)
        yield

    _staggered([stages(r0) for r0 in range(0, tm, sub)])


def _mix_ffn_ple(o_a, o_b, w_a, w_b, g_mix, x, g_pre, wg, wu, wd, g_post, p, g_in, wpg, wpp, g_ple,
                 p_row0=0, tm=512, sub=256, tf=1408):
    T, D = x.shape
    F = wg.shape[1]
    assert F % tf == 0 and T % tm == 0 and tm % sub == 0 and p_row0 % tm == 0
    p_blk0 = p_row0 // tm
    vec = lambda g: g.reshape(1, D)
    resident = lambda w: pl.BlockSpec(w.shape, lambda i: (0, 0), pipeline_mode=pl.Buffered(1))
    return pl.pallas_call(
        functools.partial(_mix_ffn_ple_kernel, sub=sub, tf=tf),
        grid=(T // tm,),
        in_specs=[_row_spec(tm, o_a.shape[1], 1), _row_spec(tm, o_b.shape[1], 1),
                  resident(w_a), resident(w_b), _whole_spec((1, D), 1),
                  _row_spec(tm, D, 1), _whole_spec((1, D), 1),
                  resident(wg), resident(wu), resident(wd),
                  _whole_spec((1, D), 1),
                  pl.BlockSpec((tm, p.shape[1]), lambda i: (i + p_blk0, 0)), _whole_spec((1, D), 1),
                  resident(wpg), resident(wpp), _whole_spec((1, D), 1)],
        out_specs=_row_spec(tm, D, 1),
        out_shape=jax.ShapeDtypeStruct((T, D), F32),
        compiler_params=_cparams(("parallel",)),
        name="mix_ffn_ple",
    )(o_a, o_b, w_a, w_b, vec(g_mix), x, vec(g_pre), wg, wu, wd, vec(g_post), p, vec(g_in), wpg, wpp,
      vec(g_ple))


def _roll_rows(x, s, rev):
    n = x.shape[0]
    return pltpu.roll(x, (n - s) if rev else s, 0)


def _cumsum_rows(x, tau, rev):
    n = x.shape[0]
    s = 1
    while s < n:
        if s % SUBLANES:
            shifted = jnp.where(tau >= s, _roll_rows(x, s, rev), 0.0)
        else:
            zeros = jnp.zeros((s,) + x.shape[1:], x.dtype)
            shifted = (jnp.concatenate([x[s:], zeros], axis=0) if rev
                       else jnp.concatenate([zeros, x[:n - s]], axis=0))
        x = x + shifted
        s *= 2
    return x


def _hgrn_levels(C):
    return [C >> (i + 1) for i in range(C.bit_length() - 1)]


def _hgrn_pair_classes(C):
    t = np.arange(C)[:, None]
    s = np.arange(C)[None, :]
    out = np.full((2 * C, 2 * C), -1, np.int32)
    for d, rev in enumerate((False, True)):
        tau, sig = (C - 1 - t, C - 1 - s) if rev else (t, s)
        blk = np.full((C, C), -1, np.int32)
        blk[t == s] = 0
        for i, L in enumerate(_hgrn_levels(C)):
            m = ((t & -(2 * L)) == (s & -(2 * L))) & ((tau & (2 * L - 1)) >= L) & ((sig & (2 * L - 1)) < L)
            blk[m] = i + 1
        out[d * C:(d + 1) * C, d * C:(d + 1) * C] = blk
    return out


def _hgrn_kernel(q_ref, gf_ref, gb_ref, i_ref, gate_ref, cls_ref, ng_ref, o_ref, acc_scr, st_scr):
    S, DK = q_ref.shape
    C = A_CHUNK
    nc = S // C
    assert nc % (2 * A_UNROLL) == 0
    ng = ng_ref[...]
    row = lax.broadcasted_iota(jnp.int32, (C, DK), 0)
    levels = _hgrn_levels(C)
    zero_half = jnp.zeros((C, DK), BF16)

    def stack(top, bottom):
        return jnp.concatenate([top, bottom], axis=0)

    def block_diag(x):
        return stack(jnp.concatenate([x[:C], zero_half], axis=1), jnp.concatenate([zero_half, x[C:]], axis=1))

    def make_decay(rev):
        tau = (C - 1 - row) if rev else row
        odd_rank = (tau & 1) == 1
        last = 0 if rev else C - 1

        def ref_rows(b, L):
            off = L if rev else L - 1
            if 2 * L >= SUBLANES:
                pieces = [jnp.broadcast_to(b[j * 2 * L + off:j * 2 * L + off + 1, :], (2 * L, DK))
                          for j in range(C // (2 * L))]
                return pieces[0] if len(pieces) == 1 else jnp.concatenate(pieces, axis=0)
            if L == 1:
                return jnp.where(odd_rank, _roll_rows(b, 1, rev), b)
            b3 = b.reshape(C // SUBLANES, SUBLANES, DK)
            sub = lax.broadcasted_iota(jnp.int32, b3.shape, 1)
            out = None
            for j in range(SUBLANES // (2 * L)):
                piece = jnp.broadcast_to(b3[:, j * 2 * L + off:j * 2 * L + off + 1, :], b3.shape)
                out = piece if out is None else jnp.where(sub >= j * 2 * L, piece, out)
            return out.reshape(C, DK)

        def decay(g):
            b = _cumsum_rows(g, tau, rev)
            return b, [ref_rows(b, L) for L in levels], b[last:last + 1, :]

        return decay

    decay_fwd, decay_bwd = make_decay(False), make_decay(True)
    pair_cls = cls_ref[...]
    st_scr[...] = jnp.zeros_like(st_scr)

    def chunk_pair(rows_f, rows_b, st):
        q = stack(q_ref[rows_f, :], q_ref[rows_b, :]).astype(F32)
        g_f, g_b = gf_ref[rows_f, :], gb_ref[rows_b, :]
        k = 1.0 - jnp.exp(stack(g_f, g_b))
        vb = stack(i_ref[rows_f, :], i_ref[rows_b, :])
        b_f, refs_f, last_f = decay_fwd(g_f)
        b_b, refs_b, last_b = decay_bwd(g_b)
        b = stack(b_f, b_b)

        o = _dot_nt(block_diag((q * jnp.exp(b)).astype(BF16)), st.astype(BF16))

        qb, kb = q.astype(BF16), k.astype(BF16)
        attn = jnp.where(pair_cls == 0, _dot_nt(qb, kb), 0.0)
        for i in range(len(levels)):
            e = jnp.exp(-jnp.abs(b - stack(refs_f[i], refs_b[i]))).astype(BF16)
            attn = jnp.where(pair_cls == i + 1, _dot_nt(qb * e, kb * e), attn)
        o = o + _dot(attn.astype(BF16), vb)

        b_last = stack(jnp.broadcast_to(last_f, (C, DK)), jnp.broadcast_to(last_b, (C, DK)))
        khat = block_diag((k * jnp.exp(b_last - b)).astype(BF16))
        keep = jnp.exp(jnp.concatenate([last_f, last_b], axis=1))
        return o, st * keep + _dot_tn(vb, khat)

    def make_body(final):
        def body(it, carry):
            st = st_scr[...]
            for u in range(A_UNROLL):
                ci = it * A_UNROLL + u
                rows_f = pl.ds(pl.multiple_of(ci * C, C), C)
                rows_b = pl.ds(pl.multiple_of((nc - 1 - ci) * C, C), C)
                o, st = chunk_pair(rows_f, rows_b, st)
                for rows, part in ((rows_f, o[:C]), (rows_b, o[C:])):
                    if final:
                        tot = acc_scr[rows, :] + part
                        o_ref[rows, :] = (_rms(tot, ng) * gate_ref[rows, :].astype(F32)).astype(o_ref.dtype)
                    else:
                        acc_scr[rows, :] = part
            st_scr[...] = st
            return carry

        return body

    trips = nc // A_UNROLL
    lax.fori_loop(0, trips // 2, make_body(False), 0)
    lax.fori_loop(trips // 2, trips, make_body(True), 0)


def _hgrn_mixer(q, g_fwd, g_bwd, v, gate, norm_g, batch, seq):
    T = q.shape[0]
    H, DK = A_HEADS, A_DK
    head = pl.BlockSpec((seq, DK), lambda b, h: (b, h))
    classes = jnp.asarray(_hgrn_pair_classes(A_CHUNK))
    return pl.pallas_call(
        _hgrn_kernel,
        grid=(batch, H),
        in_specs=[head, head, head, head, head,
                  pl.BlockSpec(classes.shape, lambda b, h: (0, 0)),
                  pl.BlockSpec((1, DK), lambda b, h: (0, 0))],
        out_specs=pl.BlockSpec((seq, DK), lambda b, h: (b, h)),
        out_shape=jax.ShapeDtypeStruct((T, H * DK), BF16),
        scratch_shapes=[pltpu.VMEM((seq, DK), F32), pltpu.VMEM((DK, 2 * DK), F32)],
        compiler_params=_cparams(("parallel", "parallel")),
        name="hgrn2",
    )(q, g_fwd, g_bwd, v, gate, classes, norm_g.reshape(1, DK))


def _expand_heads(col, j0, width):
    q = col.shape[0]
    lane = lax.broadcasted_iota(jnp.int32, (q, width), 1)
    out = jnp.broadcast_to(col[:, j0 + B_HPG - 1:j0 + B_HPG], (q, width))
    for j in range(B_HPG - 2, -1, -1):
        out = jnp.where(lane < (j + 1) * B_HEADDIM,
                        jnp.broadcast_to(col[:, j0 + j:j0 + j + 1], (q, width)), out)
    return out


def _ssd_kernel(z_ref, x_ref, bm_ref, cm_ref, dt_ref, cwx_ref, cwb_ref, cwc_ref,
                cbx_ref, cbb_ref, cbc_ref, hp_ref, dsk_ref, ng_ref, o_ref,
                xs_scr, b_scr, c_scr, y_scr, *st_scrs):
    S = x_ref.shape[0]
    Q = B_CHUNK
    nc = S // Q
    GW = B_GW
    hp = hp_ref[0]
    a_row, dtb_row = hp[0:1, :], hp[1:2, :]
    row = lax.broadcasted_iota(jnp.int32, (Q, LANES), 0)
    t2 = lax.broadcasted_iota(jnp.int32, (Q, Q), 0)
    s2 = lax.broadcasted_iota(jnp.int32, (Q, Q), 1)
    lane_gw = lax.broadcasted_iota(jnp.int32, (Q, GW), 1)
    head_mask = [(lane_gw >= j * B_HEADDIM) & (lane_gw < (j + 1) * B_HEADDIM) for j in range(B_HPG)]
    halo = SUBLANES

    def conv_body(c, carry):
        r0 = pl.multiple_of(c * Q, Q)
        rp = pl.multiple_of(jnp.maximum(r0 - halo, 0), halo)
        rn = pl.multiple_of(jnp.minimum(r0 + Q, S - halo), halo)
        has_prev = c > 0
        has_next = c < nc - 1

        def conv(src_ref, w_ref, bias_ref):
            prev = jnp.where(has_prev, src_ref[pl.ds(rp, halo), :].astype(F32), 0.0)
            nxt = jnp.where(has_next, src_ref[pl.ds(rn, halo), :].astype(F32), 0.0)
            xx = jnp.concatenate([prev, src_ref[pl.ds(r0, Q), :].astype(F32), nxt], axis=0)
            n = Q + 2 * halo
            w = w_ref[0]
            acc = None
            for j in range(B_CONV):
                delta = j - B_CONV // 2
                sh = xx if delta == 0 else pltpu.roll(xx, (-delta) % n, 0)
                term = sh[halo:halo + Q, :] * w[j:j + 1, :]
                acc = term if acc is None else acc + term
            return _silu(acc + bias_ref[0])

        xs_scr[pl.ds(r0, Q), :] = conv(x_ref, cwx_ref, cbx_ref)
        b_scr[pl.ds(r0, Q), :] = conv(bm_ref, cwb_ref, cbb_ref).astype(BF16)
        c_scr[pl.ds(r0, Q), :] = conv(cm_ref, cwc_ref, cbc_ref).astype(BF16)
        return carry

    lax.fori_loop(0, nc, conv_body, 0)

    def make_chunk(rev):
        tau = (Q - 1 - row) if rev else row
        last = 0 if rev else Q - 1
        j0 = B_HPG if rev else 0
        pair_ok = (s2 >= t2) if rev else (t2 >= s2)

        def chunk(rows, st):
            dt = jax.nn.softplus(dt_ref[rows, :] + dtb_row)
            cs = _cumsum_rows(dt * a_row, tau, rev)
            cs_t = cs.T
            xs = xs_scr[rows, :]
            bm = b_scr[rows, :]
            cm = c_scr[rows, :]
            xdt = xs * _expand_heads(dt, j0, GW)
            ecs = _expand_heads(cs, j0, GW)
            ecs_last = ecs[last:last + 1, :]

            y = _dot(cm, st.astype(BF16)) * jnp.exp(ecs)

            scores = _dot_nt(cm, bm)
            for j in range(B_HPG):
                col = jnp.broadcast_to(cs[:, j0 + j:j0 + j + 1], (Q, Q))
                rw = jnp.broadcast_to(cs_t[j0 + j:j0 + j + 1, :], (Q, Q))
                decay = jnp.where(pair_ok, jnp.exp(jnp.minimum(col - rw, 0.0)), 0.0)
                xh = jnp.where(head_mask[j], xdt, 0.0).astype(BF16)
                y = y + _dot((scores * decay).astype(BF16), xh)

            xdec = (xdt * jnp.exp(ecs_last - ecs)).astype(BF16)
            return y, st * jnp.exp(ecs_last) + _dot_tn(bm, xdec)

        return chunk

    chunk_fns = (make_chunk(False), make_chunk(True))
    for st_scr in st_scrs:
        st_scr[...] = jnp.zeros_like(st_scr)

    def make_body(final, unroll):
        def body(it, carry):
            for rev, chunk, st_scr in zip((False, True), chunk_fns, st_scrs):
                st = st_scr[...]
                for u in range(unroll):
                    ci = it * unroll + u
                    c = (nc - 1 - ci) if rev else ci
                    rows = pl.ds(pl.multiple_of(c * Q, Q), Q)
                    y, st = chunk(rows, st)
                    if final:
                        tot = y_scr[rows, :] + y + dsk_ref[0] * xs_scr[rows, :]
                        tot = tot * _silu(z_ref[rows, :].astype(F32))
                        o_ref[rows, :] = _rms(tot, ng_ref[0]).astype(o_ref.dtype)
                    else:
                        y_scr[rows, :] = y
                st_scr[...] = st
            return carry

        return body

    assert nc % (2 * B_UNROLL) == 0
    lax.fori_loop(0, nc // (2 * B_UNROLL), make_body(False, B_UNROLL), 0)
    lax.fori_loop(nc // 2, nc, make_body(True, 1), 0)


def _ssd_mixer(u, dt, conv_w, conv_b, a_log, dt_bias, d_skip, norm_g, batch, seq, col0):
    T = u.shape[0]
    G, GW, N = B_GROUPS, B_GW, B_STATE
    W = B_HEADS * B_HEADDIM
    assert col0 % GW == 0 and GW == 2 * N
    z_blk = col0 // GW
    x_blk = z_blk + W // GW
    b_blk = (col0 + 2 * W) // N
    c_blk = b_blk + G

    def pad_rows(w):
        return jnp.pad(w, ((0, SUBLANES - w.shape[0]), (0, 0)))

    cwx = pad_rows(conv_w[:, :W]).reshape(SUBLANES, G, GW).transpose(1, 0, 2)
    cwb = pad_rows(conv_w[:, W:W + G * N]).reshape(SUBLANES, G, N).transpose(1, 0, 2)
    cwc = pad_rows(conv_w[:, W + G * N:]).reshape(SUBLANES, G, N).transpose(1, 0, 2)
    cbx = conv_b[:W].reshape(G, 1, GW)
    cbb = conv_b[W:W + G * N].reshape(G, 1, N)
    cbc = conv_b[W + G * N:].reshape(G, 1, N)
    a_neg = -jnp.exp(a_log.astype(F32))
    per_dir = lambda v: jnp.concatenate([v[0].reshape(G, B_HPG), v[1].reshape(G, B_HPG)], axis=1)
    hp = jnp.stack([per_dir(a_neg), per_dir(dt_bias.astype(F32))], axis=1)
    hp = jnp.pad(hp, ((0, 0), (0, SUBLANES - 2), (0, LANES - 2 * B_HPG)))
    dsk = jnp.repeat(d_skip.astype(F32), B_HEADDIM).reshape(G, 1, GW)
    ng = norm_g.reshape(G, 1, GW)

    gspec = lambda shape: pl.BlockSpec((1,) + shape, lambda b, g: (g, 0, 0))
    return pl.pallas_call(
        _ssd_kernel,
        grid=(batch, G),
        in_specs=[pl.BlockSpec((seq, GW), lambda b, g: (b, z_blk + g)),
                  pl.BlockSpec((seq, GW), lambda b, g: (b, x_blk + g)),
                  pl.BlockSpec((seq, N), lambda b, g: (b, b_blk + g)),
                  pl.BlockSpec((seq, N), lambda b, g: (b, c_blk + g)),
                  pl.BlockSpec((seq, LANES), lambda b, g: (b, g)),
                  gspec((SUBLANES, GW)), gspec((SUBLANES, N)), gspec((SUBLANES, N)),
                  gspec((1, GW)), gspec((1, N)), gspec((1, N)),
                  gspec((SUBLANES, LANES)), gspec((1, GW)), gspec((1, GW))],
        out_specs=pl.BlockSpec((seq, GW), lambda b, g: (b, g)),
        out_shape=jax.ShapeDtypeStruct((T, W), BF16),
        scratch_shapes=[pltpu.VMEM((seq, GW), F32), pltpu.VMEM((seq, N), BF16),
                        pltpu.VMEM((seq, N), BF16), pltpu.VMEM((seq, GW), F32),
                        pltpu.VMEM((N, GW), F32), pltpu.VMEM((N, GW), F32)],
        compiler_params=_cparams(("parallel", "parallel")),
        name="ssd",
    )(u, u, u, u, dt, cwx, cwb, cwc, cbx, cbb, cbc, hp, dsk, ng)


def _na_kernel(q_ref, k_ref, v_ref, bias_ref, o_ref):
    S, HW = q_ref.shape
    W = GRID_W
    n_rows = S // W
    kh = min(NA_ROWS, n_rows)
    nk = kh * W
    n_pairs = bias_ref.shape[1] // NA_HG
    lane = lax.broadcasted_iota(jnp.int32, (W, HW), 1)
    head_masks = [(lane >= h * C_HEADDIM) & (lane < (h + 1) * C_HEADDIM) for h in range(NA_HG)]
    scale = C_HEADDIM ** -0.5

    def body(r, carry):
        rs = jnp.clip(r - kh // 2, 0, n_rows - kh)
        q = q_ref[pl.ds(pl.multiple_of(r * W, W), W), :] * jnp.asarray(scale, q_ref.dtype)
        kw = k_ref[pl.ds(pl.multiple_of(rs * W, W), nk), :]
        vw = v_ref[pl.ds(pl.multiple_of(rs * W, W), nk), :]
        zero = jnp.zeros_like(q)
        qs = jnp.concatenate([jnp.where(m, q, zero) for m in head_masks], axis=0)
        d0 = (NA_ROWS - 1) - (r - rs)
        bias = jnp.concatenate(
            [jnp.concatenate([bias_ref[0, h * n_pairs + d0 + 2 * m] for m in range(kh // 2)], axis=1)
             for h in range(NA_HG)], axis=0)
        s = _dot_nt(qs, kw) + bias
        m = jnp.max(s, axis=-1, keepdims=True)
        p = jnp.exp(s - m)
        l = jnp.sum(p, axis=-1, keepdims=True)
        res = _dot(p.astype(BF16), vw) / l
        out = jnp.zeros((W, HW), F32)
        for h in range(NA_HG):
            out = jnp.where(head_masks[h], res[h * W:(h + 1) * W, :], out)
        o_ref[pl.ds(pl.multiple_of(r * W, W), W), :] = out.astype(o_ref.dtype)
        return carry

    lax.fori_loop(0, n_rows, body, 0, unroll=NA_UNROLL)


def _na_bias_table(rpb):
    H, n_dr, n_dc = rpb.shape
    W = GRID_W
    c = np.arange(W)
    qs = np.clip(c - NA_COLS // 2, 0, W - NA_COLS)
    valid = (c[None, :] >= qs[:, None]) & (c[None, :] < qs[:, None] + NA_COLS)
    dc = np.clip(c[None, :] - c[:, None] + (NA_COLS - 1), 0, n_dc - 1)
    onehot = np.zeros((2, n_dc, W, 2 * W), np.float32)
    for half in range(2):
        onehot[half, :, :, half * W:(half + 1) * W] = dc[None] == np.arange(n_dc)[:, None, None]
    both = jnp.concatenate([rpb[:, :-1], rpb[:, 1:]], axis=2).astype(F32)
    pairs = jnp.einsum("hre,ecx->hrcx", both, jnp.asarray(onehot.reshape(2 * n_dc, W, 2 * W)),
                       precision=lax.Precision.HIGHEST)
    valid2 = np.concatenate([valid, valid], axis=1)
    pairs = jnp.where(jnp.asarray(valid2)[None, None], pairs, -jnp.inf)
    return pairs.reshape(H // NA_HG, NA_HG * (n_dr - 1), W, 2 * W)


def _neighborhood_attention(qkv, rpb, batch, seq):
    T = qkv.shape[0]
    HW = NA_HG * C_HEADDIM
    n_hg = C_HEADS // NA_HG
    n_rows = seq // GRID_W
    assert n_rows >= NA_ROWS and NA_ROWS % 2 == 0
    table = _na_bias_table(rpb)
    return pl.pallas_call(
        _na_kernel,
        grid=(batch, n_hg),
        in_specs=[pl.BlockSpec((seq, HW), lambda b, g: (b, g)),
                  pl.BlockSpec((seq, HW), lambda b, g: (b, n_hg + g)),
                  pl.BlockSpec((seq, HW), lambda b, g: (b, 2 * n_hg + g)),
                  pl.BlockSpec((1,) + table.shape[1:], lambda b, g: (g, 0, 0, 0))],
        out_specs=pl.BlockSpec((seq, HW), lambda b, g: (b, g)),
        out_shape=jax.ShapeDtypeStruct((T, C_HEADS * C_HEADDIM), BF16),
        compiler_params=_cparams(("parallel", "parallel")),
        name="natten",
    )(qkv, qkv, qkv, table)


def _store_token_tiles(ref, val, t0=0):
    count, width = val.shape
    n = width // LANES
    for j in range(n):
        ref[pl.ds(t0 * n + j, count, stride=n), :] = val[:, j * LANES:(j + 1) * LANES]


def _load_token_tiles(ref, n, t0=0, count=None):
    if count is None:
        count = ref.shape[0] // n - t0
    return jnp.concatenate([ref[pl.ds(t0 * n + j, count, stride=n), :] for j in range(n)], axis=1)


def _proj_router_kernel(a_ref, w_ref, gmix_ref, x_ref, g_ref, wr_ref, x1_ref, o_ref, h_ref):
    tm = x_ref.shape[0]
    sub = min(tm, 256)

    def stages(r0):
        rows = pl.ds(r0, sub)
        x1 = x_ref[rows, :] + _rms(_dot(a_ref[rows, :], w_ref[...]), gmix_ref[...])
        x1_ref[rows, :] = x1
        yield
        h = _rms(x1, g_ref[...])
        _store_token_tiles(h_ref, h, r0)
        h_hi = h.astype(BF16)
        h_lo = (h - h_hi.astype(F32)).astype(BF16)
        both = _dot(h_hi, wr_ref[...])
        logits = both[:, :LANES] + both[:, LANES:] + _dot(h_lo, wr_ref[:, :LANES])
        yield
        lane = lax.broadcasted_iota(jnp.int32, logits.shape, 1)
        neg = -jnp.inf
        l1 = jnp.where(lane < N_EXPERTS, logits, neg)
        m1 = jnp.max(l1, axis=-1, keepdims=True)
        i1 = jnp.min(jnp.where(l1 == m1, lane, LANES), axis=-1, keepdims=True)
        l2 = jnp.where(lane == i1, neg, l1)
        m2 = jnp.max(l2, axis=-1, keepdims=True)
        i2 = jnp.min(jnp.where(l2 == m2, lane, LANES), axis=-1, keepdims=True)
        e = jnp.exp(m2 - m1)
        w1 = 1.0 / (1.0 + e)
        w2 = e / (1.0 + e)
        o_ref[rows, :] = jnp.where(lane == 0, i1.astype(F32),
                                   jnp.where(lane == 1, i2.astype(F32),
                                             jnp.where(lane == 2, w1, jnp.where(lane == 3, w2, 0.0))))
        yield

    _staggered([stages(r0) for r0 in range(0, tm, sub)])


def _proj_router(a, w, g_mix, x, g, w_router, tm=512):
    T, D = x.shape
    n = D // LANES
    wr = jnp.pad(w_router.astype(F32), ((0, 0), (0, LANES - w_router.shape[1])))
    wr_hi = wr.astype(BF16)
    wr = jnp.concatenate([wr_hi, (wr - wr_hi.astype(F32)).astype(BF16)], axis=1)
    x1, route, h = pl.pallas_call(
        _proj_router_kernel,
        grid=(T // tm,),
        in_specs=[_row_spec(tm, a.shape[1], 1), _whole_spec(w.shape, 1), _whole_spec((1, D), 1),
                  _row_spec(tm, D, 1), _whole_spec((1, D), 1), _whole_spec((D, 2 * LANES), 1)],
        out_specs=[_row_spec(tm, D, 1), _row_spec(tm, LANES, 1), _row_spec(tm * n, LANES, 1)],
        out_shape=[jax.ShapeDtypeStruct((T, D), F32),
                   jax.ShapeDtypeStruct((T, LANES), F32),
                   jax.ShapeDtypeStruct((T * n, LANES), F32)],
        compiler_params=_cparams(("parallel",)),
        name="proj_router",
    )(a, w, g_mix.reshape(1, D), x, g.reshape(1, D), wr)
    return x1, route, h.reshape(T, n, LANES)


SC_CORES = 2
SC_SUBCORES = 16
SC_WORKERS = SC_CORES * SC_SUBCORES
SC_CHUNK = 32


def _sc_mesh():
    return plsc.VectorSubcoreMesh(core_axis_name="c", subcore_axis_name="s",
                                  num_cores=SC_CORES, num_subcores=SC_SUBCORES)


def _sc_index_blocks(idx):
    return idx.reshape(SC_WORKERS, -1, SC_CHUNK)


def _sc_scatter_tokens(h, dest, n_rows, row0=0):
    T = dest.shape[0]
    _, n, _ = h.shape
    per_worker = T // SC_WORKERS
    n_chunks = per_worker // SC_CHUNK
    assert per_worker * SC_WORKERS == T and n_chunks * SC_CHUNK == per_worker

    @functools.partial(
        pl.kernel, mesh=_sc_mesh(),
        out_type=jax.ShapeDtypeStruct((n_rows, n, LANES), h.dtype),
        scratch_types=[pltpu.VMEM((n_chunks, SC_CHUNK), jnp.int32),
                       pltpu.VMEM((n_chunks, SC_CHUNK), jnp.int32),
                       pltpu.VMEM((SC_CHUNK, n, LANES), h.dtype)],
        name="sc_scatter_tokens",
    )
    def scatter(h_hbm, d0_hbm, d1_hbm, o_hbm, i0_v, i1_v, rows_v):
        wid = lax.axis_index("s") * SC_CORES + lax.axis_index("c")
        pltpu.sync_copy(d0_hbm.at[wid], i0_v)
        pltpu.sync_copy(d1_hbm.at[wid], i1_v)

        @pl.loop(0, n_chunks)
        def _(j):
            pltpu.sync_copy(h_hbm.at[pl.ds(row0 + wid * per_worker + j * SC_CHUNK, SC_CHUNK)], rows_v)
            pltpu.sync_copy(rows_v, o_hbm.at[i0_v.at[j]])
            pltpu.sync_copy(rows_v, o_hbm.at[i1_v.at[j]])

    return scatter(h, _sc_index_blocks(dest[:, 0]), _sc_index_blocks(dest[:, 1]))


def _sc_gather_tokens(y, dest):
    T = dest.shape[0]
    _, n, _ = y.shape
    per_worker = T // SC_WORKERS
    n_chunks = per_worker // SC_CHUNK
    assert per_worker * SC_WORKERS == T and n_chunks * SC_CHUNK == per_worker
    out = jax.ShapeDtypeStruct((T, n, LANES), y.dtype)

    @functools.partial(
        pl.kernel, mesh=_sc_mesh(), out_type=(out, out),
        scratch_types=[pltpu.VMEM((n_chunks, SC_CHUNK), jnp.int32),
                       pltpu.VMEM((n_chunks, SC_CHUNK), jnp.int32),
                       pltpu.VMEM((SC_CHUNK, n, LANES), y.dtype)],
        name="sc_gather_tokens",
    )
    def gather(y_hbm, d0_hbm, d1_hbm, o0_hbm, o1_hbm, i0_v, i1_v, rows_v):
        wid = lax.axis_index("s") * SC_CORES + lax.axis_index("c")
        pltpu.sync_copy(d0_hbm.at[wid], i0_v)
        pltpu.sync_copy(d1_hbm.at[wid], i1_v)

        @pl.loop(0, n_chunks)
        def _(j):
            rows = pl.ds(wid * per_worker + j * SC_CHUNK, SC_CHUNK)
            pltpu.sync_copy(y_hbm.at[i0_v.at[j]], rows_v)
            pltpu.sync_copy(rows_v, o0_hbm.at[rows])
            pltpu.sync_copy(y_hbm.at[i1_v.at[j]], rows_v)
            pltpu.sync_copy(rows_v, o1_hbm.at[rows])

    return gather(y, _sc_index_blocks(dest[:, 0]), _sc_index_blocks(dest[:, 1]))


SC_PACK_PAIRS = 8
SC_LANES = 16


def _sc_pack_bf16_rows(w):
    R, C = w.shape
    pairs = R // 2
    per_worker = pairs // SC_WORKERS
    n_chunks = per_worker // SC_PACK_PAIRS
    assert n_chunks * SC_PACK_PAIRS * SC_WORKERS * 2 == R and C % SC_LANES == 0

    def round_bits(v):
        u = plsc.bitcast(v, jnp.int32)
        return u + 0x7FFF + (lax.shift_right_logical(u, 16) & 1)

    @functools.partial(
        pl.kernel, mesh=_sc_mesh(),
        out_type=jax.ShapeDtypeStruct((pairs, C), jnp.int32),
        scratch_types=[pltpu.VMEM((2 * SC_PACK_PAIRS, C), F32), pltpu.VMEM((SC_PACK_PAIRS, C), jnp.int32)],
        compiler_params=pltpu.CompilerParams(use_tc_tiling_on_sc=True, needs_layout_passes=False),
        name="sc_pack_bf16_rows",
    )
    def pack(w_hbm, o_hbm, in_v, out_v):
        wid = lax.axis_index("s") * SC_CORES + lax.axis_index("c")

        @pl.loop(0, n_chunks)
        def _(j):
            p0 = (wid * n_chunks + j) * SC_PACK_PAIRS
            pltpu.sync_copy(w_hbm.at[pl.ds(2 * p0, 2 * SC_PACK_PAIRS)], in_v)

            @pl.loop(0, C, step=SC_LANES)
            def _(c):
                cols = pl.ds(c, SC_LANES)
                for i in range(SC_PACK_PAIRS):
                    lo = lax.shift_right_logical(round_bits(in_v[2 * i, cols]), 16)
                    hi = round_bits(in_v[2 * i + 1, cols]) & jnp.int32(-65536)
                    out_v[i, cols] = lo | hi

            pltpu.sync_copy(out_v, o_hbm.at[pl.ds(p0, SC_PACK_PAIRS)])

    return pack(w)


def _moe_ffn_kernel(te_ref, na_ref, h_ref, wg_ref, wu_ref, wd_ref, o_ref, acc_scr, *, tf):
    i = pl.program_id(0)
    D = acc_scr.shape[1]
    F = wg_ref.shape[2]
    unpack = lambda packed: pltpu.bitcast(packed, BF16)

    @pl.when(i < na_ref[0])
    def _():
        h = _load_token_tiles(h_ref, D // LANES).astype(BF16)
        for f0 in range(0, F, tf):
            a = (_silu(_dot(h, unpack(wg_ref[0, :, f0:f0 + tf])))
                 * _dot(h, unpack(wu_ref[0, :, f0:f0 + tf]))).astype(BF16)
            part = _dot(a, unpack(wd_ref[0, f0 // 2:(f0 + tf) // 2, :]))
            if f0 == 0:
                acc_scr[...] = part
            else:
                acc_scr[...] += part
        _store_token_tiles(o_ref, acc_scr[...])

    @pl.when(i >= na_ref[0])
    def _():
        o_ref[...] = jnp.zeros_like(o_ref)


def _moe_ffn(hs, wg, wu, wd, tile_expert, n_active, tm=MOE_TM, tf=512):
    rows, n, _ = hs.shape
    E, half_d, F = wg.shape
    D = 2 * half_d
    nt = tile_expert.shape[0]
    assert rows == nt * tm and n * LANES == D and wd.shape == (E, F // 2, D)
    grid_spec = pltpu.PrefetchScalarGridSpec(
        num_scalar_prefetch=2,
        grid=(nt,),
        in_specs=[pl.BlockSpec((tm * n, LANES), lambda i, te, na: (i, 0)),
                  pl.BlockSpec((1, D // 2, F), lambda i, te, na: (te[i], 0, 0)),
                  pl.BlockSpec((1, D // 2, F), lambda i, te, na: (te[i], 0, 0)),
                  pl.BlockSpec((1, F // 2, D), lambda i, te, na: (te[i], 0, 0))],
        out_specs=pl.BlockSpec((tm * n, LANES), lambda i, te, na: (i, 0)),
        scratch_shapes=[pltpu.VMEM((tm, D), F32)],
    )
    y = pl.pallas_call(
        functools.partial(_moe_ffn_kernel, tf=tf),
        grid_spec=grid_spec,
        out_shape=jax.ShapeDtypeStruct((rows * n, LANES), F32),
        compiler_params=_cparams(("arbitrary",)),
        name="moe_ffn",
    )(tile_expert, n_active, hs.reshape(rows * n, LANES), wg, wu, wd)
    return y.reshape(rows, n, LANES)


def _combine_ple_kernel(y0_ref, y1_ref, route_ref, x_ref, g_ref, p_ref, gin_ref, wpg_ref, wpp_ref,
                        gple_ref, *rest):
    o_ref = rest[-1]
    tm, D = x_ref.shape
    n = D // LANES
    sub = min(tm, 256)

    def stages(r0):
        rows = pl.ds(r0, sub)
        route = route_ref[rows, :]
        moe = (route[:, 2:3] * _load_token_tiles(y0_ref, n, r0, sub)
               + route[:, 3:4] * _load_token_tiles(y1_ref, n, r0, sub))
        x2 = x_ref[rows, :] + _rms(moe, g_ref[...])
        yield
        o_ref[rows, :] = _ple_update(x2, p_ref[rows, :], gin_ref, wpg_ref, wpp_ref, gple_ref)
        yield

    _staggered([stages(r0) for r0 in range(0, tm, sub)])


def _combine_ple(y0, y1, route, x, g, p, g_in, wpg, wpp, g_ple, row0=0, prev=None, p_row0=0, tm=512):
    T, D = x.shape
    Tp, n, _ = y0.shape
    assert row0 % tm == 0 and Tp % tm == 0 and n * LANES == D and p_row0 % tm == 0
    blk0 = row0 // tm
    p_blk0 = (row0 + p_row0) // tm
    vec = lambda v: v.reshape(1, D)
    shifted = lambda width: pl.BlockSpec((tm, width), lambda i: (i + blk0, 0))
    in_specs = [_row_spec(tm * n, LANES, 1), _row_spec(tm * n, LANES, 1), shifted(LANES),
                shifted(D), _whole_spec((1, D), 1),
                pl.BlockSpec((tm, p.shape[1]), lambda i: (i + p_blk0, 0)), _whole_spec((1, D), 1),
                _whole_spec(wpg.shape, 1), _whole_spec(wpp.shape, 1), _whole_spec((1, D), 1)]
    args = [y0.reshape(Tp * n, LANES), y1.reshape(Tp * n, LANES), route, x, vec(g), p, vec(g_in), wpg, wpp,
            vec(g_ple)]
    aliases = {}
    if prev is not None:
        in_specs.append(pl.BlockSpec(memory_space=pl.ANY))
        args.append(prev)
        aliases = {len(args) - 1: 0}
    return pl.pallas_call(
        _combine_ple_kernel,
        grid=(Tp // tm,),
        in_specs=in_specs,
        out_specs=shifted(D),
        out_shape=jax.ShapeDtypeStruct((T, D), F32),
        input_output_aliases=aliases,
        compiler_params=_cparams(("parallel",)),
        name="combine_ple",
    )(*args)


def _moe_plan(route, tm=MOE_TM):
    T = route.shape[0]
    e = route[:, :TOP_K].astype(jnp.int32).reshape(-1)
    onehot = (e[:, None] == jnp.arange(N_EXPERTS, dtype=jnp.int32)[None, :]).astype(jnp.int32)
    csum = jnp.cumsum(onehot, axis=0)
    rank = jnp.sum(csum * onehot, axis=1) - 1
    counts = csum[-1]
    padded = ((counts + tm - 1) // tm) * tm
    ends = jnp.cumsum(padded)
    starts = ends - padded
    dest = jnp.sum(starts[None, :] * onehot, axis=1) + rank
    nt = (T * TOP_K) // tm + N_EXPERTS
    tile_start = jnp.arange(nt, dtype=jnp.int32) * tm
    tile_expert = jnp.minimum(jnp.sum((tile_start[:, None] >= ends[None, :]).astype(jnp.int32), axis=1),
                              N_EXPERTS - 1).astype(jnp.int32)
    n_active = (ends[-1] // tm).astype(jnp.int32).reshape(1)
    return tile_expert, n_active, dest.reshape(T, TOP_K).astype(jnp.int32)


def kernel(x, p, hgrn_lb_raw, e_norm_mix_pre, e_w_in, e_conv_w, e_conv_b, e_A_log, e_dt_bias, e_D, e_a_norm, e_b_norm, e_w_out, e_norm_mix_post, e_norm_ffn_pre, e_w_ffn_gate, e_w_ffn_up, e_w_ffn_down, e_norm_ffn_post, o_norm_mix_pre, o_w_qkv, o_rpb, o_w_out, o_norm_mix_post, o_norm_ffn_pre, o_w_router, o_w_exp_gate, o_w_exp_up, o_w_exp_down, o_norm_ffn_post, ple_norm_in, ple_w_gate, ple_w_proj, ple_norm_post):
    batch, seq, d_model = x.shape
    depth = p.shape[0]
    T = batch * seq
    xt = x.reshape(T, d_model)
    p_all = p.reshape(depth * T, -1)
    lb_all =jnp.cumsum(jax.nn.softmax(hgrn_lb_raw.astype(F32), axis=0), axis=0)

    a_kdim = A_HEADS * A_DK
    b_width = B_HEADS * B_HEADDIM
    conv_dim = b_width + 2 * B_GROUPS * B_STATE
    main_w = 5 * a_kdim + b_width + conv_dim

    def pack_experts(w):
        E, R, C = w.shape
        return _sc_pack_bf16_rows(w.reshape(E * R, C)).reshape(E, R // 2, C)

    for li in range(depth):
        j = li // 2
        ple =(ple_norm_in[li], ple_w_gate[li].astype(BF16), ple_w_proj[li].astype(BF16), ple_norm_post[li])
        if li % 2 == 0:
            w_in = e_w_in[j]
            dtf = w_in[:, main_w:main_w + B_HEADS].reshape(d_model, B_GROUPS, B_HPG)
            dtb = w_in[:, main_w + B_HEADS:].reshape(d_model, B_GROUPS, B_HPG)
            w_dt = jnp.pad(jnp.concatenate([dtf, dtb], axis=2), ((0, 0), (0, 0), (0, LANES - 2 * B_HPG)))
            w_pieces = [(w_in.astype(BF16), main_w),
                        (w_dt.reshape(d_model, B_GROUPS * LANES).astype(BF16), B_GROUPS * LANES)]
            outs = [(a_kdim, BF16, "silu"), (a_kdim, F32, "log_gate"), (a_kdim, F32, "log_gate"),
                    (a_kdim, BF16, None), (a_kdim, BF16, "silu"),
                    (b_width + conv_dim, BF16, None), (B_GROUPS * LANES, F32, None)]
            q_a, gf_a, gb_a, i_a, gate_a, u_b, dt = _norm_proj(
                xt, e_norm_mix_pre[j], w_pieces, outs, lb=lb_all[li])
            o_a = _hgrn_mixer(q_a, gf_a, gb_a, i_a, gate_a, e_a_norm[j], batch, seq)
            o_b = _ssd_mixer(u_b, dt, e_conv_w[j], e_conv_b[j], e_A_log[j], e_dt_bias[j], e_D[j],
                             e_b_norm[j], batch, seq, 0)
            w_out = e_w_out[j].astype(BF16)
            xt = _mix_ffn_ple(o_a, o_b, w_out[:a_kdim], w_out[a_kdim:], e_norm_mix_post[j], xt,
                              e_norm_ffn_pre[j], e_w_ffn_gate[j].astype(BF16), e_w_ffn_up[j].astype(BF16),
                              e_w_ffn_down[j].astype(BF16), e_norm_ffn_post[j], p_all, *ple, p_row0=li * T)
        else:
            (qkv,) = _norm_proj(xt, o_norm_mix_pre[j], o_w_qkv[j].astype(BF16),
                                [(3 * C_HEADS * C_HEADDIM, BF16, None)])
            o_c = _neighborhood_attention(qkv, o_rpb[j], batch, seq)
            xt, route, h = _proj_router(o_c, o_w_out[j].astype(BF16), o_norm_mix_post[j], xt,
                                        o_norm_ffn_pre[j], o_w_router[j])
            expert_w = [pack_experts(w[j]) for w in (o_w_exp_gate, o_w_exp_up, o_w_exp_down)]
            x_mid, xt = xt, None
            t_grp = T // MOE_GROUPS
            for s in range(MOE_GROUPS):
                tile_expert, n_active, dest = _moe_plan(route[s * t_grp:(s + 1) * t_grp])
                hs = _sc_scatter_tokens(h, dest, tile_expert.shape[0] * MOE_TM, row0=s * t_grp)
                ys = _moe_ffn(hs, *expert_w, tile_expert, n_active)
                y0, y1 = _sc_gather_tokens(ys, dest)
                xt = _combine_ple(y0, y1, route, x_mid, o_norm_ffn_post[j], p_all, *ple,
                                  row0=s * t_grp, prev=xt, p_row0=li * T)
    return xt.reshape(batch, seq, d_model)
```

```python
import functools
import math

import numpy as np
import jax
import jax.numpy as jnp
from jax import lax
from jax.experimental import pallas as pl
from jax.experimental.pallas import tpu as pltpu
from jax.experimental.pallas import tpu_sc as plsc

F32 = jnp.float32
BF16 = jnp.bfloat16
EPS = 1e-6

LANES = 128
SUBLANES = 8
VMEM_LIMIT_BYTES = 56 * 1024 * 1024

GRID_W = 64
A_HEADS, A_DK, A_CHUNK = 4, 128, 64
A_UNROLL = 8
B_HEADS, B_HEADDIM, B_GROUPS, B_STATE, B_CONV, B_CHUNK = 8, 64, 2, 128, 5, 128
B_UNROLL = 2
B_HPG = B_HEADS // B_GROUPS
B_GW = B_HPG * B_HEADDIM
C_HEADS, C_HEADDIM = 16, 64
NA_ROWS, NA_COLS = 8, 16
NA_HG = 4
NA_UNROLL = 16
N_EXPERTS, TOP_K = 8, 2
MOE_GROUPS = 2
MOE_TM = 512


def _cparams(sem):
    return pltpu.CompilerParams(dimension_semantics=sem, vmem_limit_bytes=VMEM_LIMIT_BYTES)


def _rms(x, g):
    return x * lax.rsqrt(jnp.mean(x * x, axis=-1, keepdims=True) + EPS) * g


def _silu(x):
    return x * jax.nn.sigmoid(x)


def _dot(a, b):
    return jnp.dot(a, b, preferred_element_type=F32)


def _dot_nt(a, b):
    return lax.dot_general(a, b, (((1,), (1,)), ((), ())), preferred_element_type=F32)


def _dot_tn(a, b):
    return lax.dot_general(a, b, (((0,), (0,)), ((), ())), preferred_element_type=F32)


def _staggered(stage_fns):
    live = list(stage_fns)
    step = 0
    done = [False] * len(live)
    while not all(done):
        for k, gen in enumerate(live):
            if step >= k and not done[k]:
                try:
                    next(gen)
                except StopIteration:
                    done[k] = True
        step += 1


def _norm_proj_kernel(x_ref, g_ref, lb_ref, *refs, acts, used_cols, col_chunk, sub):
    o_refs = refs[len(refs) - len(acts):]
    w_refs = list(refs[:len(refs) - len(acts)])

    def stages(r0):
        rows = pl.ds(r0, sub)
        h = _rms(x_ref[rows, :], g_ref[...]).astype(BF16)
        yield
        piece, off = 0, 0
        for o_ref, act in zip(o_refs, acts):
            n = o_ref.shape[1]
            if off == used_cols[piece]:
                piece, off = piece + 1, 0
            w_ref = w_refs[piece]
            for c0 in range(0, n, col_chunk):
                c1 = min(c0 + col_chunk, n)
                y = _dot(h, w_ref[:, off + c0:off + c1])
                if act == "silu":
                    y = _silu(y)
                elif act == "log_gate":
                    lb = lb_ref[:, c0:c1]
                    y = jnp.log(lb + (1.0 - lb) * jax.nn.sigmoid(y))
                o_ref[rows, c0:c1] = y.astype(o_ref.dtype)
                yield
            off += n

    _staggered([stages(r0) for r0 in range(0, x_ref.shape[0], sub)])


def _norm_proj(x, g, w, outs, lb=None, tm=512, sub=256, col_chunk=1024):
    T, D = x.shape
    pieces = list(w) if isinstance(w, (list, tuple)) else [(w, w.shape[1])]
    ws = [v for v, _ in pieces]
    used_cols = tuple(n for _, n in pieces)
    widths = [o[0] for o in outs]
    assert sum(widths) == sum(used_cols) and T % tm == 0
    if lb is None:
        lb = jnp.zeros((LANES,), F32)
    return pl.pallas_call(
        functools.partial(_norm_proj_kernel, acts=tuple(o[2] for o in outs), used_cols=used_cols,
                          col_chunk=col_chunk, sub=sub),
        grid=(T // tm,),
        in_specs=[_row_spec(tm, D, 1), _whole_spec((1, D), 1), _whole_spec((1, lb.shape[0]), 1)]
                 + [_whole_spec(v.shape, 1) for v in ws],
        out_specs=[_row_spec(tm, n, 1) for n in widths],
        out_shape=[jax.ShapeDtypeStruct((T, n), dt) for n, dt, _ in outs],
        compiler_params=_cparams(("parallel",)),
        name="norm_proj",
    )(x, g.reshape(1, D), lb.reshape(1, -1), *ws)


def _ple_update(x, p, gin_ref, wg_ref, wp_ref, gpost_ref):
    h = _rms(x, gin_ref[...]).astype(BF16)
    gate = jax.nn.sigmoid(_dot(h, wg_ref[...]))
    proj = _dot(p.astype(BF16), wp_ref[...])
    return x + _rms(gate * proj, gpost_ref[...])


def _row_spec(tm, width, n_grid):
    return pl.BlockSpec((tm, width), (lambda i: (i, 0)) if n_grid == 1 else (lambda i, j: (i, 0)))


def _whole_spec(shape, n_grid):
    zeros = (0,) * len(shape)
    return pl.BlockSpec(shape, (lambda i: zeros) if n_grid == 1 else (lambda i, j: zeros))


def _mix_ffn_ple_kernel(oa_ref, ob_ref, wa_ref, wb_ref, gmix_ref, x_ref, gpre_ref, wg_ref, wu_ref, wd_ref,
                        gpost_ref, p_ref, gin_ref, wpg_ref, wpp_ref, gple_ref, o_ref, *, sub, tf):
    tm = x_ref.shape[0]
    F = wg_ref.shape[1]
    def stages(r0):
        rows = pl.ds(r0, sub)
        mix = _dot(oa_ref[rows, :], wa_ref[...]) + _dot(ob_ref[rows, :], wb_ref[...])
        x1 = x_ref[rows, :] + _rms(mix, gmix_ref[...])
        h = _rms(x1, gpre_ref[...]).astype(BF16)
        yield
        acc = None
        for f0 in range(0, F, tf):
            f1 = min(f0 + tf, F)
            a = (_silu(_dot(h, wg_ref[:, f0:f1])) * _dot(h, wu_ref[:, f0:f1])).astype(BF16)
            part = _dot(a, wd_ref[f0:f1, :])
            acc = part if acc is None else acc + part
            yield
        x2 = x1 + _rms(acc, gpost_ref[...])
        o_ref[rows, :] = _ple_update(x2, p_ref[rows, :], gin_ref, wpg_ref, wpp_ref, gple_ref)
        yield

    _staggered([stages(r0) for r0 in range(0, tm, sub)])


def _mix_ffn_ple(o_a, o_b, w_a, w_b, g_mix, x, g_pre, wg, wu, wd, g_post, p, g_in, wpg, wpp, g_ple,
                 p_row0=0, tm=512, sub=256, tf=1536):
    T, D = x.shape
    F = wg.shape[1]
    assert T % tm == 0 and tm % sub == 0 and p_row0 % tm == 0
    p_blk0 = p_row0 // tm
    vec = lambda g: g.reshape(1, D)
    resident = lambda w: pl.BlockSpec(w.shape, lambda i: (0, 0), pipeline_mode=pl.Buffered(1))
    return pl.pallas_call(
        functools.partial(_mix_ffn_ple_kernel, sub=sub, tf=tf),
        grid=(T // tm,),
        in_specs=[_row_spec(tm, o_a.shape[1], 1), _row_spec(tm, o_b.shape[1], 1),
                  resident(w_a), resident(w_b), _whole_spec((1, D), 1),
                  _row_spec(tm, D, 1), _whole_spec((1, D), 1),
                  resident(wg), resident(wu), resident(wd),
                  _whole_spec((1, D), 1),
                  pl.BlockSpec((tm, p.shape[1]), lambda i: (i + p_blk0, 0)), _whole_spec((1, D), 1),
                  resident(wpg), resident(wpp), _whole_spec((1, D), 1)],
        out_specs=_row_spec(tm, D, 1),
        out_shape=jax.ShapeDtypeStruct((T, D), F32),
        compiler_params=_cparams(("parallel",)),
        name="mix_ffn_ple",
    )(o_a, o_b, w_a, w_b, vec(g_mix), x, vec(g_pre), wg, wu, wd, vec(g_post), p, vec(g_in), wpg, wpp,
      vec(g_ple))


def _roll_rows(x, s, rev):
    n = x.shape[0]
    return pltpu.roll(x, (n - s) if rev else s, 0)


def _cumsum_rows(x, tau, rev):
    n = x.shape[0]
    s = 1
    while s < n:
        if s % SUBLANES:
            shifted = jnp.where(tau >= s, _roll_rows(x, s, rev), 0.0)
        else:
            zeros = jnp.zeros((s,) + x.shape[1:], x.dtype)
            shifted = (jnp.concatenate([x[s:], zeros], axis=0) if rev
                       else jnp.concatenate([zeros, x[:n - s]], axis=0))
        x = x + shifted
        s *= 2
    return x


def _hgrn_levels(C):
    return [C >> (i + 1) for i in range(C.bit_length() - 1)]


def _hgrn_pair_classes(C):
    t = np.arange(C)[:, None]
    s = np.arange(C)[None, :]
    out = np.full((2 * C, 2 * C), -1, np.int32)
    for d, rev in enumerate((False, True)):
        tau, sig = (C - 1 - t, C - 1 - s) if rev else (t, s)
        blk = np.full((C, C), -1, np.int32)
        blk[t == s] = 0
        for i, L in enumerate(_hgrn_levels(C)):
            m = ((t & -(2 * L)) == (s & -(2 * L))) & ((tau & (2 * L - 1)) >= L) & ((sig & (2 * L - 1)) < L)
            blk[m] = i + 1
        out[d * C:(d + 1) * C, d * C:(d + 1) * C] = blk
    return out


def _hgrn_kernel(q_ref, gf_ref, gb_ref, i_ref, gate_ref, cls_ref, ng_ref, o_ref, acc_scr, st_scr):
    S, DK = q_ref.shape
    C = A_CHUNK
    nc = S // C
    assert nc % (2 * A_UNROLL) == 0
    ng = ng_ref[...]
    row = lax.broadcasted_iota(jnp.int32, (C, DK), 0)
    levels = _hgrn_levels(C)
    zero_half = jnp.zeros((C, DK), BF16)

    def stack(top, bottom):
        return jnp.concatenate([top, bottom], axis=0)

    def block_diag(x):
        return stack(jnp.concatenate([x[:C], zero_half], axis=1), jnp.concatenate([zero_half, x[C:]], axis=1))

    def make_decay(rev):
        tau = (C - 1 - row) if rev else row
        odd_rank = (tau & 1) == 1
        last = 0 if rev else C - 1

        def ref_rows(b, L):
            off = L if rev else L - 1
            if 2 * L >= SUBLANES:
                pieces = [jnp.broadcast_to(b[j * 2 * L + off:j * 2 * L + off + 1, :], (2 * L, DK))
                          for j in range(C // (2 * L))]
                return pieces[0] if len(pieces) == 1 else jnp.concatenate(pieces, axis=0)
            if L == 1:
                return jnp.where(odd_rank, _roll_rows(b, 1, rev), b)
            b3 = b.reshape(C // SUBLANES, SUBLANES, DK)
            sub = lax.broadcasted_iota(jnp.int32, b3.shape, 1)
            out = None
            for j in range(SUBLANES // (2 * L)):
                piece = jnp.broadcast_to(b3[:, j * 2 * L + off:j * 2 * L + off + 1, :], b3.shape)
                out = piece if out is None else jnp.where(sub >= j * 2 * L, piece, out)
            return out.reshape(C, DK)

        def decay(g):
            b = _cumsum_rows(g, tau, rev)
            return b, [ref_rows(b, L) for L in levels], b[last:last + 1, :]

        return decay

    decay_fwd, decay_bwd = make_decay(False), make_decay(True)
    pair_cls = cls_ref[...]
    st_scr[...] = jnp.zeros_like(st_scr)

    def chunk_pair(rows_f, rows_b, st):
        q = stack(q_ref[rows_f, :], q_ref[rows_b, :]).astype(F32)
        g_f, g_b = gf_ref[rows_f, :], gb_ref[rows_b, :]
        k = 1.0 - jnp.exp(stack(g_f, g_b))
        vb = stack(i_ref[rows_f, :], i_ref[rows_b, :])
        b_f, refs_f, last_f = decay_fwd(g_f)
        b_b, refs_b, last_b = decay_bwd(g_b)
        b = stack(b_f, b_b)

        o = _dot_nt(block_diag((q * jnp.exp(b)).astype(BF16)), st.astype(BF16))

        qb, kb = q.astype(BF16), k.astype(BF16)
        attn = jnp.where(pair_cls == 0, _dot_nt(qb, kb), 0.0)
        for i in range(len(levels)):
            e = jnp.exp(-jnp.abs(b - stack(refs_f[i], refs_b[i]))).astype(BF16)
            attn = jnp.where(pair_cls == i + 1, _dot_nt(qb * e, kb * e), attn)
        o = o + _dot(attn.astype(BF16), vb)

        b_last = stack(jnp.broadcast_to(last_f, (C, DK)), jnp.broadcast_to(last_b, (C, DK)))
        khat = block_diag((k * jnp.exp(b_last - b)).astype(BF16))
        keep = jnp.exp(jnp.concatenate([last_f, last_b], axis=1))
        return o, st * keep + _dot_tn(vb, khat)

    def make_body(final):
        def body(it, carry):
            st = st_scr[...]
            for u in range(A_UNROLL):
                ci = it * A_UNROLL + u
                rows_f = pl.ds(pl.multiple_of(ci * C, C), C)
                rows_b = pl.ds(pl.multiple_of((nc - 1 - ci) * C, C), C)
                o, st = chunk_pair(rows_f, rows_b, st)
                for rows, part in ((rows_f, o[:C]), (rows_b, o[C:])):
                    if final:
                        tot = acc_scr[rows, :] + part
                        o_ref[rows, :] = (_rms(tot, ng) * gate_ref[rows, :].astype(F32)).astype(o_ref.dtype)
                    else:
                        acc_scr[rows, :] = part
            st_scr[...] = st
            return carry

        return body

    trips = nc // A_UNROLL
    lax.fori_loop(0, trips // 2, make_body(False), 0)
    lax.fori_loop(trips // 2, trips, make_body(True), 0)


def _hgrn_mixer(q, g_fwd, g_bwd, v, gate, norm_g, batch, seq):
    T = q.shape[0]
    H, DK = A_HEADS, A_DK
    head = pl.BlockSpec((seq, DK), lambda b, h: (b, h))
    classes = jnp.asarray(_hgrn_pair_classes(A_CHUNK))
    return pl.pallas_call(
        _hgrn_kernel,
        grid=(batch, H),
        in_specs=[head, head, head, head, head,
                  pl.BlockSpec(classes.shape, lambda b, h: (0, 0)),
                  pl.BlockSpec((1, DK), lambda b, h: (0, 0))],
        out_specs=pl.BlockSpec((seq, DK), lambda b, h: (b, h)),
        out_shape=jax.ShapeDtypeStruct((T, H * DK), BF16),
        scratch_shapes=[pltpu.VMEM((seq, DK), F32), pltpu.VMEM((DK, 2 * DK), F32)],
        compiler_params=_cparams(("parallel", "parallel")),
        name="hgrn2",
    )(q, g_fwd, g_bwd, v, gate, classes, norm_g.reshape(1, DK))


def _expand_heads(col, j0, width):
    q = col.shape[0]
    lane = lax.broadcasted_iota(jnp.int32, (q, width), 1)
    out = jnp.broadcast_to(col[:, j0 + B_HPG - 1:j0 + B_HPG], (q, width))
    for j in range(B_HPG - 2, -1, -1):
        out = jnp.where(lane < (j + 1) * B_HEADDIM,
                        jnp.broadcast_to(col[:, j0 + j:j0 + j + 1], (q, width)), out)
    return out


def _ssd_kernel(z_ref, x_ref, bm_ref, cm_ref, dt_ref, cwx_ref, cwb_ref, cwc_ref,
                cbx_ref, cbb_ref, cbc_ref, hp_ref, dsk_ref, ng_ref, o_ref,
                xs_scr, b_scr, c_scr, y_scr, *st_scrs):
    S = x_ref.shape[0]
    Q = B_CHUNK
    nc = S // Q
    GW = B_GW
    hp = hp_ref[0]
    a_row, dtb_row = hp[0:1, :], hp[1:2, :]
    row = lax.broadcasted_iota(jnp.int32, (Q, LANES), 0)
    t2 = lax.broadcasted_iota(jnp.int32, (Q, Q), 0)
    s2 = lax.broadcasted_iota(jnp.int32, (Q, Q), 1)
    lane_gw = lax.broadcasted_iota(jnp.int32, (Q, GW), 1)
    head_mask = [(lane_gw >= j * B_HEADDIM) & (lane_gw < (j + 1) * B_HEADDIM) for j in range(B_HPG)]
    halo = SUBLANES

    def conv_body(c, carry):
        r0 = pl.multiple_of(c * Q, Q)
        rp = pl.multiple_of(jnp.maximum(r0 - halo, 0), halo)
        rn = pl.multiple_of(jnp.minimum(r0 + Q, S - halo), halo)
        has_prev = c > 0
        has_next = c < nc - 1

        def conv(src_ref, w_ref, bias_ref):
            prev = jnp.where(has_prev, src_ref[pl.ds(rp, halo), :].astype(F32), 0.0)
            nxt = jnp.where(has_next, src_ref[pl.ds(rn, halo), :].astype(F32), 0.0)
            xx = jnp.concatenate([prev, src_ref[pl.ds(r0, Q), :].astype(F32), nxt], axis=0)
            n = Q + 2 * halo
            w = w_ref[0]
            acc = None
            for j in range(B_CONV):
                delta = j - B_CONV // 2
                sh = xx if delta == 0 else pltpu.roll(xx, (-delta) % n, 0)
                term = sh[halo:halo + Q, :] * w[j:j + 1, :]
                acc = term if acc is None else acc + term
            return _silu(acc + bias_ref[0])

        xs_scr[pl.ds(r0, Q), :] = conv(x_ref, cwx_ref, cbx_ref)
        b_scr[pl.ds(r0, Q), :] = conv(bm_ref, cwb_ref, cbb_ref).astype(BF16)
        c_scr[pl.ds(r0, Q), :] = conv(cm_ref, cwc_ref, cbc_ref).astype(BF16)
        return carry

    lax.fori_loop(0, nc, conv_body, 0)

    def make_chunk(rev):
        tau = (Q - 1 - row) if rev else row
        last = 0 if rev else Q - 1
        j0 = B_HPG if rev else 0
        pair_ok = (s2 >= t2) if rev else (t2 >= s2)

        def chunk(rows, st):
            dt = jax.nn.softplus(dt_ref[rows, :] + dtb_row)
            cs = _cumsum_rows(dt * a_row, tau, rev)
            cs_t = cs.T
            xs = xs_scr[rows, :]
            bm = b_scr[rows, :]
            cm = c_scr[rows, :]
            xdt = xs * _expand_heads(dt, j0, GW)
            ecs = _expand_heads(cs, j0, GW)
            ecs_last = ecs[last:last + 1, :]

            y = _dot(cm, st.astype(BF16)) * jnp.exp(ecs)

            scores = _dot_nt(cm, bm)
            for j in range(B_HPG):
                col = jnp.broadcast_to(cs[:, j0 + j:j0 + j + 1], (Q, Q))
                rw = jnp.broadcast_to(cs_t[j0 + j:j0 + j + 1, :], (Q, Q))
                decay = jnp.where(pair_ok, jnp.exp(jnp.minimum(col - rw, 0.0)), 0.0)
                xh = jnp.where(head_mask[j], xdt, 0.0).astype(BF16)
                y = y + _dot((scores * decay).astype(BF16), xh)

            xdec = (xdt * jnp.exp(ecs_last - ecs)).astype(BF16)
            return y, st * jnp.exp(ecs_last) + _dot_tn(bm, xdec)

        return chunk

    chunk_fns = (make_chunk(False), make_chunk(True))
    for st_scr in st_scrs:
        st_scr[...] = jnp.zeros_like(st_scr)

    def make_body(final, unroll):
        def body(it, carry):
            for rev, chunk, st_scr in zip((False, True), chunk_fns, st_scrs):
                st = st_scr[...]
                for u in range(unroll):
                    ci = it * unroll + u
                    c = (nc - 1 - ci) if rev else ci
                    rows = pl.ds(pl.multiple_of(c * Q, Q), Q)
                    y, st = chunk(rows, st)
                    if final:
                        tot = y_scr[rows, :] + y + dsk_ref[0] * xs_scr[rows, :]
                        tot = tot * _silu(z_ref[rows, :].astype(F32))
                        o_ref[rows, :] = _rms(tot, ng_ref[0]).astype(o_ref.dtype)
                    else:
                        y_scr[rows, :] = y
                st_scr[...] = st
            return carry

        return body

    assert nc % (2 * B_UNROLL) == 0
    lax.fori_loop(0, nc // (2 * B_UNROLL), make_body(False, B_UNROLL), 0)
    lax.fori_loop(nc // 2, nc, make_body(True, 1), 0)


def _ssd_mixer(u, dt, conv_w, conv_b, a_log, dt_bias, d_skip, norm_g, batch, seq, col0):
    T = u.shape[0]
    G, GW, N = B_GROUPS, B_GW, B_STATE
    W = B_HEADS * B_HEADDIM
    assert col0 % GW == 0 and GW == 2 * N
    z_blk = col0 // GW
    x_blk = z_blk + W // GW
    b_blk = (col0 + 2 * W) // N
    c_blk = b_blk + G

    def pad_rows(w):
        return jnp.pad(w, ((0, SUBLANES - w.shape[0]), (0, 0)))

    cwx = pad_rows(conv_w[:, :W]).reshape(SUBLANES, G, GW).transpose(1, 0, 2)
    cwb = pad_rows(conv_w[:, W:W + G * N]).reshape(SUBLANES, G, N).transpose(1, 0, 2)
    cwc = pad_rows(conv_w[:, W + G * N:]).reshape(SUBLANES, G, N).transpose(1, 0, 2)
    cbx = conv_b[:W].reshape(G, 1, GW)
    cbb = conv_b[W:W + G * N].reshape(G, 1, N)
    cbc = conv_b[W + G * N:].reshape(G, 1, N)
    a_neg = -jnp.exp(a_log.astype(F32))
    per_dir = lambda v: jnp.concatenate([v[0].reshape(G, B_HPG), v[1].reshape(G, B_HPG)], axis=1)
    hp = jnp.stack([per_dir(a_neg), per_dir(dt_bias.astype(F32))], axis=1)
    hp = jnp.pad(hp, ((0, 0), (0, SUBLANES - 2), (0, LANES - 2 * B_HPG)))
    dsk = jnp.repeat(d_skip.astype(F32), B_HEADDIM).reshape(G, 1, GW)
    ng = norm_g.reshape(G, 1, GW)

    gspec = lambda shape: pl.BlockSpec((1,) + shape, lambda b, g: (g, 0, 0))
    return pl.pallas_call(
        _ssd_kernel,
        grid=(batch, G),
        in_specs=[pl.BlockSpec((seq, GW), lambda b, g: (b, z_blk + g)),
                  pl.BlockSpec((seq, GW), lambda b, g: (b, x_blk + g)),
                  pl.BlockSpec((seq, N), lambda b, g: (b, b_blk + g)),
                  pl.BlockSpec((seq, N), lambda b, g: (b, c_blk + g)),
                  pl.BlockSpec((seq, LANES), lambda b, g: (b, g)),
                  gspec((SUBLANES, GW)), gspec((SUBLANES, N)), gspec((SUBLANES, N)),
                  gspec((1, GW)), gspec((1, N)), gspec((1, N)),
                  gspec((SUBLANES, LANES)), gspec((1, GW)), gspec((1, GW))],
        out_specs=pl.BlockSpec((seq, GW), lambda b, g: (b, g)),
        out_shape=jax.ShapeDtypeStruct((T, W), BF16),
        scratch_shapes=[pltpu.VMEM((seq, GW), F32), pltpu.VMEM((seq, N), BF16),
                        pltpu.VMEM((seq, N), BF16), pltpu.VMEM((seq, GW), F32),
                        pltpu.VMEM((N, GW), F32), pltpu.VMEM((N, GW), F32)],
        compiler_params=_cparams(("parallel", "parallel")),
        name="ssd",
    )(u, u, u, u, dt, cwx, cwb, cwc, cbx, cbb, cbc, hp, dsk, ng)


def _na_kernel(q_ref, k_ref, v_ref, bias_ref, o_ref):
    S, HW = q_ref.shape
    W = GRID_W
    n_rows = S // W
    kh = min(NA_ROWS, n_rows)
    nk = kh * W
    n_pairs = bias_ref.shape[1] // NA_HG
    lane = lax.broadcasted_iota(jnp.int32, (W, HW), 1)
    head_masks = [(lane >= h * C_HEADDIM) & (lane < (h + 1) * C_HEADDIM) for h in range(NA_HG)]
    scale = C_HEADDIM ** -0.5

    def body(r, carry):
        rs = jnp.clip(r - kh // 2, 0, n_rows - kh)
        q = q_ref[pl.ds(pl.multiple_of(r * W, W), W), :] * jnp.asarray(scale, q_ref.dtype)
        kw = k_ref[pl.ds(pl.multiple_of(rs * W, W), nk), :]
        vw = v_ref[pl.ds(pl.multiple_of(rs * W, W), nk), :]
        zero = jnp.zeros_like(q)
        qs = jnp.concatenate([jnp.where(m, q, zero) for m in head_masks], axis=0)
        d0 = (NA_ROWS - 1) - (r - rs)
        bias = jnp.concatenate(
            [jnp.concatenate([bias_ref[0, h * n_pairs + d0 + 2 * m] for m in range(kh // 2)], axis=1)
             for h in range(NA_HG)], axis=0)
        s = _dot_nt(qs, kw) + bias
        m = jnp.max(s, axis=-1, keepdims=True)
        p = jnp.exp(s - m)
        l = jnp.sum(p, axis=-1, keepdims=True)
        res = _dot(p.astype(BF16), vw) / l
        out = jnp.zeros((W, HW), F32)
        for h in range(NA_HG):
            out = jnp.where(head_masks[h], res[h * W:(h + 1) * W, :], out)
        o_ref[pl.ds(pl.multiple_of(r * W, W), W), :] = out.astype(o_ref.dtype)
        return carry

    lax.fori_loop(0, n_rows, body, 0, unroll=NA_UNROLL)


def _na_bias_table(rpb):
    H, n_dr, n_dc = rpb.shape
    W = GRID_W
    c = np.arange(W)
    qs = np.clip(c - NA_COLS // 2, 0, W - NA_COLS)
    valid = (c[None, :] >= qs[:, None]) & (c[None, :] < qs[:, None] + NA_COLS)
    dc = np.clip(c[None, :] - c[:, None] + (NA_COLS - 1), 0, n_dc - 1)
    onehot = np.zeros((2, n_dc, W, 2 * W), np.float32)
    for half in range(2):
        onehot[half, :, :, half * W:(half + 1) * W] = dc[None] == np.arange(n_dc)[:, None, None]
    both = jnp.concatenate([rpb[:, :-1], rpb[:, 1:]], axis=2).astype(F32)
    pairs = jnp.einsum("hre,ecx->hrcx", both, jnp.asarray(onehot.reshape(2 * n_dc, W, 2 * W)),
                       precision=lax.Precision.HIGHEST)
    valid2 = np.concatenate([valid, valid], axis=1)
    pairs = jnp.where(jnp.asarray(valid2)[None, None], pairs, -jnp.inf)
    return pairs.reshape(H // NA_HG, NA_HG * (n_dr - 1), W, 2 * W)


def _neighborhood_attention(qkv, rpb, batch, seq):
    T = qkv.shape[0]
    HW = NA_HG * C_HEADDIM
    n_hg = C_HEADS // NA_HG
    n_rows = seq // GRID_W
    assert n_rows >= NA_ROWS and NA_ROWS % 2 == 0
    table = _na_bias_table(rpb)
    return pl.pallas_call(
        _na_kernel,
        grid=(batch, n_hg),
        in_specs=[pl.BlockSpec((seq, HW), lambda b, g: (b, g)),
                  pl.BlockSpec((seq, HW), lambda b, g: (b, n_hg + g)),
                  pl.BlockSpec((seq, HW), lambda b, g: (b, 2 * n_hg + g)),
                  pl.BlockSpec((1,) + table.shape[1:], lambda b, g: (g, 0, 0, 0))],
        out_specs=pl.BlockSpec((seq, HW), lambda b, g: (b, g)),
        out_shape=jax.ShapeDtypeStruct((T, C_HEADS * C_HEADDIM), BF16),
        compiler_params=_cparams(("parallel", "parallel")),
        name="natten",
    )(qkv, qkv, qkv, table)


def _store_token_tiles(ref, val, t0=0):
    count, width = val.shape
    n = width // LANES
    for j in range(n):
        ref[pl.ds(t0 * n + j, count, stride=n), :] = val[:, j * LANES:(j + 1) * LANES]


def _load_token_tiles(ref, n, t0=0, count=None):
    if count is None:
        count = ref.shape[0] // n - t0
    return jnp.concatenate([ref[pl.ds(t0 * n + j, count, stride=n), :] for j in range(n)], axis=1)


def _proj_router_kernel(a_ref, w_ref, gmix_ref, x_ref, g_ref, wr_ref, x1_ref, o_ref, h_ref):
    tm = x_ref.shape[0]
    sub = tm

    def stages(r0):
        rows = pl.ds(r0, sub)
        x1 = x_ref[rows, :] + _rms(_dot(a_ref[rows, :], w_ref[...]), gmix_ref[...])
        x1_ref[rows, :] = x1
        yield
        h = _rms(x1, g_ref[...])
        _store_token_tiles(h_ref, h, r0)
        h_hi = h.astype(BF16)
        h_lo = (h - h_hi.astype(F32)).astype(BF16)
        both = _dot(h_hi, wr_ref[...])
        logits = both[:, :LANES] + both[:, LANES:] + _dot(h_lo, wr_ref[:, :LANES])
        yield
        lane = lax.broadcasted_iota(jnp.int32, logits.shape, 1)
        neg = -jnp.inf
        l1 = jnp.where(lane < N_EXPERTS, logits, neg)
        m1 = jnp.max(l1, axis=-1, keepdims=True)
        i1 = jnp.min(jnp.where(l1 == m1, lane, LANES), axis=-1, keepdims=True)
        l2 = jnp.where(lane == i1, neg, l1)
        m2 = jnp.max(l2, axis=-1, keepdims=True)
        i2 = jnp.min(jnp.where(l2 == m2, lane, LANES), axis=-1, keepdims=True)
        e = jnp.exp(m2 - m1)
        w1 = 1.0 / (1.0 + e)
        w2 = e / (1.0 + e)
        o_ref[rows, :] = jnp.where(lane == 0, i1.astype(F32),
                                   jnp.where(lane == 1, i2.astype(F32),
                                             jnp.where(lane == 2, w1, jnp.where(lane == 3, w2, 0.0))))
        yield

    _staggered([stages(r0) for r0 in range(0, tm, sub)])


def _proj_router(a, w, g_mix, x, g, w_router, tm=512):
    T, D = x.shape
    n = D // LANES
    wr = jnp.pad(w_router.astype(F32), ((0, 0), (0, LANES - w_router.shape[1])))
    wr_hi = wr.astype(BF16)
    wr = jnp.concatenate([wr_hi, (wr - wr_hi.astype(F32)).astype(BF16)], axis=1)
    x1, route, h = pl.pallas_call(
        _proj_router_kernel,
        grid=(T // tm,),
        in_specs=[_row_spec(tm, a.shape[1], 1), _whole_spec(w.shape, 1), _whole_spec((1, D), 1),
                  _row_spec(tm, D, 1), _whole_spec((1, D), 1), _whole_spec((D, 2 * LANES), 1)],
        out_specs=[_row_spec(tm, D, 1), _row_spec(tm, LANES, 1), _row_spec(tm * n, LANES, 1)],
        out_shape=[jax.ShapeDtypeStruct((T, D), F32),
                   jax.ShapeDtypeStruct((T, LANES), F32),
                   jax.ShapeDtypeStruct((T * n, LANES), F32)],
        compiler_params=_cparams(("parallel",)),
        name="proj_router",
    )(a, w, g_mix.reshape(1, D), x, g.reshape(1, D), wr)
    return x1, route, h.reshape(T, n, LANES)


SC_CORES = 2
SC_SUBCORES = 16
SC_WORKERS = SC_CORES * SC_SUBCORES
SC_CHUNK = 32


def _sc_mesh():
    return plsc.VectorSubcoreMesh(core_axis_name="c", subcore_axis_name="s",
                                  num_cores=SC_CORES, num_subcores=SC_SUBCORES)


def _sc_index_blocks(idx):
    return idx.reshape(SC_WORKERS, -1, SC_CHUNK)


def _sc_scatter_tokens(h, dest, n_rows, row0=0):
    T = dest.shape[0]
    _, n, _ = h.shape
    per_worker = T // SC_WORKERS
    n_chunks = per_worker // SC_CHUNK
    assert per_worker * SC_WORKERS == T and n_chunks * SC_CHUNK == per_worker

    @functools.partial(
        pl.kernel, mesh=_sc_mesh(),
        out_type=jax.ShapeDtypeStruct((n_rows, n, LANES), h.dtype),
        scratch_types=[pltpu.VMEM((n_chunks, SC_CHUNK), jnp.int32),
                       pltpu.VMEM((n_chunks, SC_CHUNK), jnp.int32),
                       pltpu.VMEM((SC_CHUNK, n, LANES), h.dtype)],
        name="sc_scatter_tokens",
    )
    def scatter(h_hbm, d0_hbm, d1_hbm, o_hbm, i0_v, i1_v, rows_v):
        wid = lax.axis_index("s") * SC_CORES + lax.axis_index("c")
        pltpu.sync_copy(d0_hbm.at[wid], i0_v)
        pltpu.sync_copy(d1_hbm.at[wid], i1_v)

        @pl.loop(0, n_chunks)
        def _(j):
            pltpu.sync_copy(h_hbm.at[pl.ds(row0 + wid * per_worker + j * SC_CHUNK, SC_CHUNK)], rows_v)
            pltpu.sync_copy(rows_v, o_hbm.at[i0_v.at[j]])
            pltpu.sync_copy(rows_v, o_hbm.at[i1_v.at[j]])

    return scatter(h, _sc_index_blocks(dest[:, 0]), _sc_index_blocks(dest[:, 1]))


def _sc_gather_tokens(y, dest):
    T = dest.shape[0]
    _, n, _ = y.shape
    per_worker = T // SC_WORKERS
    n_chunks = per_worker // SC_CHUNK
    assert per_worker * SC_WORKERS == T and n_chunks * SC_CHUNK == per_worker
    out = jax.ShapeDtypeStruct((T, n, LANES), y.dtype)

    @functools.partial(
        pl.kernel, mesh=_sc_mesh(), out_type=(out, out),
        scratch_types=[pltpu.VMEM((n_chunks, SC_CHUNK), jnp.int32),
                       pltpu.VMEM((n_chunks, SC_CHUNK), jnp.int32),
                       pltpu.VMEM((SC_CHUNK, n, LANES), y.dtype)],
        name="sc_gather_tokens",
    )
    def gather(y_hbm, d0_hbm, d1_hbm, o0_hbm, o1_hbm, i0_v, i1_v, rows_v):
        wid = lax.axis_index("s") * SC_CORES + lax.axis_index("c")
        pltpu.sync_copy(d0_hbm.at[wid], i0_v)
        pltpu.sync_copy(d1_hbm.at[wid], i1_v)

        @pl.loop(0, n_chunks)
        def _(j):
            rows = pl.ds(wid * per_worker + j * SC_CHUNK, SC_CHUNK)
            pltpu.sync_copy(y_hbm.at[i0_v.at[j]], rows_v)
            pltpu.sync_copy(rows_v, o0_hbm.at[rows])
            pltpu.sync_copy(y_hbm.at[i1_v.at[j]], rows_v)
            pltpu.sync_copy(rows_v, o1_hbm.at[rows])

    return gather(y, _sc_index_blocks(dest[:, 0]), _sc_index_blocks(dest[:, 1]))


SC_PACK_PAIRS = 8
SC_LANES = 16


def _sc_pack_bf16_rows(w):
    R, C = w.shape
    pairs = R // 2
    per_worker = pairs // SC_WORKERS
    n_chunks = per_worker // SC_PACK_PAIRS
    assert n_chunks * SC_PACK_PAIRS * SC_WORKERS * 2 == R and C % SC_LANES == 0

    def round_bits(v):
        u = plsc.bitcast(v, jnp.int32)
        return u + 0x7FFF + (lax.shift_right_logical(u, 16) & 1)

    @functools.partial(
        pl.kernel, mesh=_sc_mesh(),
        out_type=jax.ShapeDtypeStruct((pairs, C), jnp.int32),
        scratch_types=[pltpu.VMEM((2 * SC_PACK_PAIRS, C), F32), pltpu.VMEM((SC_PACK_PAIRS, C), jnp.int32)],
        compiler_params=pltpu.CompilerParams(use_tc_tiling_on_sc=True, needs_layout_passes=False),
        name="sc_pack_bf16_rows",
    )
    def pack(w_hbm, o_hbm, in_v, out_v):
        wid = lax.axis_index("s") * SC_CORES + lax.axis_index("c")

        @pl.loop(0, n_chunks)
        def _(j):
            p0 = (wid * n_chunks + j) * SC_PACK_PAIRS
            pltpu.sync_copy(w_hbm.at[pl.ds(2 * p0, 2 * SC_PACK_PAIRS)], in_v)

            @pl.loop(0, C, step=SC_LANES)
            def _(c):
                cols = pl.ds(c, SC_LANES)
                for i in range(SC_PACK_PAIRS):
                    lo = lax.shift_right_logical(round_bits(in_v[2 * i, cols]), 16)
                    hi = round_bits(in_v[2 * i + 1, cols]) & jnp.int32(-65536)
                    out_v[i, cols] = lo | hi

            pltpu.sync_copy(out_v, o_hbm.at[pl.ds(p0, SC_PACK_PAIRS)])

    return pack(w)


def _moe_ffn_kernel(te_ref, na_ref, h_ref, wg_ref, wu_ref, wd_ref, o_ref, acc_scr, *, tf):
    i = pl.program_id(0)
    D = acc_scr.shape[1]
    F = wg_ref.shape[2]
    unpack = lambda packed: pltpu.bitcast(packed, BF16)

    @pl.when(i < na_ref[0])
    def _():
        h = _load_token_tiles(h_ref, D // LANES).astype(BF16)
        for f0 in range(0, F, tf):
            a = (_silu(_dot(h, unpack(wg_ref[0, :, f0:f0 + tf])))
                 * _dot(h, unpack(wu_ref[0, :, f0:f0 + tf]))).astype(BF16)
            part = _dot(a, unpack(wd_ref[0, f0 // 2:(f0 + tf) // 2, :]))
            if f0 == 0:
                acc_scr[...] = part
            else:
                acc_scr[...] += part
        _store_token_tiles(o_ref, acc_scr[...])

    @pl.when(i >= na_ref[0])
    def _():
        o_ref[...] = jnp.zeros_like(o_ref)


def _moe_ffn(hs, wg, wu, wd, tile_expert, n_active, tm=MOE_TM, tf=512):
    rows, n, _ = hs.shape
    E, half_d, F = wg.shape
    D = 2 * half_d
    nt = tile_expert.shape[0]
    assert rows == nt * tm and n * LANES == D and wd.shape == (E, F // 2, D)
    grid_spec = pltpu.PrefetchScalarGridSpec(
        num_scalar_prefetch=2,
        grid=(nt,),
        in_specs=[pl.BlockSpec((tm * n, LANES), lambda i, te, na: (i, 0)),
                  pl.BlockSpec((1, D // 2, F), lambda i, te, na: (te[i], 0, 0)),
                  pl.BlockSpec((1, D // 2, F), lambda i, te, na: (te[i], 0, 0)),
                  pl.BlockSpec((1, F // 2, D), lambda i, te, na: (te[i], 0, 0))],
        out_specs=pl.BlockSpec((tm * n, LANES), lambda i, te, na: (i, 0)),
        scratch_shapes=[pltpu.VMEM((tm, D), F32)],
    )
    y = pl.pallas_call(
        functools.partial(_moe_ffn_kernel, tf=tf),
        grid_spec=grid_spec,
        out_shape=jax.ShapeDtypeStruct((rows * n, LANES), F32),
        compiler_params=_cparams(("arbitrary",)),
        name="moe_ffn",
    )(tile_expert, n_active, hs.reshape(rows * n, LANES), wg, wu, wd)
    return y.reshape(rows, n, LANES)


def _combine_ple_kernel(y0_ref, y1_ref, route_ref, x_ref, g_ref, p_ref, gin_ref, wpg_ref, wpp_ref,
                        gple_ref, *rest):
    o_ref = rest[-1]
    tm, D = x_ref.shape
    n = D // LANES
    sub = min(tm, 256)

    def stages(r0):
        rows = pl.ds(r0, sub)
        route = route_ref[rows, :]
        moe = (route[:, 2:3] * _load_token_tiles(y0_ref, n, r0, sub)
               + route[:, 3:4] * _load_token_tiles(y1_ref, n, r0, sub))
        x2 = x_ref[rows, :] + _rms(moe, g_ref[...])
        yield
        o_ref[rows, :] = _ple_update(x2, p_ref[rows, :], gin_ref, wpg_ref, wpp_ref, gple_ref)
        yield

    _staggered([stages(r0) for r0 in range(0, tm, sub)])


def _combine_ple(y0, y1, route, x, g, p, g_in, wpg, wpp, g_ple, row0=0, prev=None, p_row0=0, tm=512):
    T, D = x.shape
    Tp, n, _ = y0.shape
    assert row0 % tm == 0 and Tp % tm == 0 and n * LANES == D and p_row0 % tm == 0
    blk0 = row0 // tm
    p_blk0 = (row0 + p_row0) // tm
    vec = lambda v: v.reshape(1, D)
    shifted = lambda width: pl.BlockSpec((tm, width), lambda i: (i + blk0, 0))
    in_specs = [_row_spec(tm * n, LANES, 1), _row_spec(tm * n, LANES, 1), shifted(LANES),
                shifted(D), _whole_spec((1, D), 1),
                pl.BlockSpec((tm, p.shape[1]), lambda i: (i + p_blk0, 0)), _whole_spec((1, D), 1),
                _whole_spec(wpg.shape, 1), _whole_spec(wpp.shape, 1), _whole_spec((1, D), 1)]
    args = [y0.reshape(Tp * n, LANES), y1.reshape(Tp * n, LANES), route, x, vec(g), p, vec(g_in), wpg, wpp,
            vec(g_ple)]
    aliases = {}
    if prev is not None:
        in_specs.append(pl.BlockSpec(memory_space=pl.ANY))
        args.append(prev)
        aliases = {len(args) - 1: 0}
    return pl.pallas_call(
        _combine_ple_kernel,
        grid=(Tp // tm,),
        in_specs=in_specs,
        out_specs=shifted(D),
        out_shape=jax.ShapeDtypeStruct((T, D), F32),
        input_output_aliases=aliases,
        compiler_params=_cparams(("parallel",)),
        name="combine_ple",
    )(*args)


def _moe_plan(route, tm=MOE_TM):
    T = route.shape[0]
    e = route[:, :TOP_K].astype(jnp.int32).reshape(-1)
    onehot = (e[:, None] == jnp.arange(N_EXPERTS, dtype=jnp.int32)[None, :]).astype(jnp.int32)
    csum = jnp.cumsum(onehot, axis=0)
    rank = jnp.sum(csum * onehot, axis=1) - 1
    counts = csum[-1]
    padded = ((counts + tm - 1) // tm) * tm
    ends = jnp.cumsum(padded)
    starts = ends - padded
    dest = jnp.sum(starts[None, :] * onehot, axis=1) + rank
    nt = (T * TOP_K) // tm + N_EXPERTS
    tile_start = jnp.arange(nt, dtype=jnp.int32) * tm
    tile_expert = jnp.minimum(jnp.sum((tile_start[:, None] >= ends[None, :]).astype(jnp.int32), axis=1),
                              N_EXPERTS - 1).astype(jnp.int32)
    n_active = (ends[-1] // tm).astype(jnp.int32).reshape(1)
    return tile_expert, n_active, dest.reshape(T, TOP_K).astype(jnp.int32)


def kernel(x, p, hgrn_lb_raw, e_norm_mix_pre, e_w_in, e_conv_w, e_conv_b, e_A_log, e_dt_bias, e_D, e_a_norm, e_b_norm, e_w_out, e_norm_mix_post, e_norm_ffn_pre, e_w_ffn_gate, e_w_ffn_up, e_w_ffn_down, e_norm_ffn_post, o_norm_mix_pre, o_w_qkv, o_rpb, o_w_out, o_norm_mix_post, o_norm_ffn_pre, o_w_router, o_w_exp_gate, o_w_exp_up, o_w_exp_down, o_norm_ffn_post, ple_norm_in, ple_w_gate, ple_w_proj, ple_norm_post):
    batch, seq, d_model = x.shape
    depth = p.shape[0]
    T = batch * seq
    xt = x.reshape(T, d_model)
    p_all = p.reshape(depth * T, -1)
    lb_all =jnp.cumsum(jax.nn.softmax(hgrn_lb_raw.astype(F32), axis=0), axis=0)

    a_kdim = A_HEADS * A_DK
    b_width = B_HEADS * B_HEADDIM
    conv_dim = b_width + 2 * B_GROUPS * B_STATE
    main_w = 5 * a_kdim + b_width + conv_dim

    def pack_experts(w):
        E, R, C = w.shape
        return _sc_pack_bf16_rows(w.reshape(E * R, C)).reshape(E, R // 2, C)

    for li in range(depth):
        j = li // 2
        ple =(ple_norm_in[li], ple_w_gate[li].astype(BF16), ple_w_proj[li].astype(BF16), ple_norm_post[li])
        if li % 2 == 0:
            w_in = e_w_in[j]
            dtf = w_in[:, main_w:main_w + B_HEADS].reshape(d_model, B_GROUPS, B_HPG)
            dtb = w_in[:, main_w + B_HEADS:].reshape(d_model, B_GROUPS, B_HPG)
            w_dt = jnp.pad(jnp.concatenate([dtf, dtb], axis=2), ((0, 0), (0, 0), (0, LANES - 2 * B_HPG)))
            w_pieces = [(w_in.astype(BF16), main_w),
                        (w_dt.reshape(d_model, B_GROUPS * LANES).astype(BF16), B_GROUPS * LANES)]
            outs = [(a_kdim, BF16, "silu"), (a_kdim, F32, "log_gate"), (a_kdim, F32, "log_gate"),
                    (a_kdim, BF16, None), (a_kdim, BF16, "silu"),
                    (b_width + conv_dim, BF16, None), (B_GROUPS * LANES, F32, None)]
            q_a, gf_a, gb_a, i_a, gate_a, u_b, dt = _norm_proj(
                xt, e_norm_mix_pre[j], w_pieces, outs, lb=lb_all[li])
            o_a = _hgrn_mixer(q_a, gf_a, gb_a, i_a, gate_a, e_a_norm[j], batch, seq)
            o_b = _ssd_mixer(u_b, dt, e_conv_w[j], e_conv_b[j], e_A_log[j], e_dt_bias[j], e_D[j],
                             e_b_norm[j], batch, seq, 0)
            w_out = e_w_out[j].astype(BF16)
            xt = _mix_ffn_ple(o_a, o_b, w_out[:a_kdim], w_out[a_kdim:], e_norm_mix_post[j], xt,
                              e_norm_ffn_pre[j], e_w_ffn_gate[j].astype(BF16), e_w_ffn_up[j].astype(BF16),
                              e_w_ffn_down[j].astype(BF16), e_norm_ffn_post[j], p_all, *ple, p_row0=li * T)
        else:
            (qkv,) = _norm_proj(xt, o_norm_mix_pre[j], o_w_qkv[j].astype(BF16),
                                [(3 * C_HEADS * C_HEADDIM, BF16, None)])
            o_c = _neighborhood_attention(qkv, o_rpb[j], batch, seq)
            xt, route, h = _proj_router(o_c, o_w_out[j].astype(BF16), o_norm_mix_post[j], xt,
                                        o_norm_ffn_pre[j], o_w_router[j])
            expert_w = [pack_experts(w[j]) for w in (o_w_exp_gate, o_w_exp_up, o_w_exp_down)]
            x_mid, xt = xt, None
            t_grp = T // MOE_GROUPS
            for s in range(MOE_GROUPS):
                tile_expert, n_active, dest = _moe_plan(route[s * t_grp:(s + 1) * t_grp])
                hs = _sc_scatter_tokens(h, dest, tile_expert.shape[0] * MOE_TM, row0=s * t_grp)
                ys = _moe_ffn(hs, *expert_w, tile_expert, n_active)
                y0, y1 = _sc_gather_tokens(ys, dest)
                xt = _combine_ple(y0, y1, route, x_mid, o_norm_ffn_post[j], p_all, *ple,
                                  row0=s * t_grp, prev=xt, p_row0=li * T)
    return xt.reshape(batch, seq, d_model)
```

```python
import functools
import math

import numpy as np
import jax
import jax.numpy as jnp
from jax import lax
from jax.experimental import pallas as pl
from jax.experimental.pallas import tpu as pltpu
from jax.experimental.pallas import tpu_sc as plsc

F32 = jnp.float32
BF16 = jnp.bfloat16
EPS = 1e-6

LANES = 128
SUBLANES = 8
VMEM_LIMIT_BYTES = 56 * 1024 * 1024

GRID_W = 64
A_HEADS, A_DK, A_CHUNK = 4, 128, 64
A_UNROLL = 16
B_HEADS, B_HEADDIM, B_GROUPS, B_STATE, B_CONV, B_CHUNK = 8, 64, 2, 128, 5, 128
B_UNROLL = 2
B_HPG = B_HEADS // B_GROUPS
B_GW = B_HPG * B_HEADDIM
C_HEADS, C_HEADDIM = 16, 64
NA_ROWS, NA_COLS = 8, 16
NA_HG = 4
NA_UNROLL = 16
N_EXPERTS, TOP_K = 8, 2
MOE_GROUPS = 2
MOE_TM = 512


def _cparams(sem):
    return pltpu.CompilerParams(dimension_semantics=sem, vmem_limit_bytes=VMEM_LIMIT_BYTES)


def _rms(x, g):
    return x * lax.rsqrt(jnp.mean(x * x, axis=-1, keepdims=True) + EPS) * g


def _silu(x):
    return x * jax.nn.sigmoid(x)


def _dot(a, b):
    return jnp.dot(a, b, preferred_element_type=F32)


def _dot_nt(a, b):
    return lax.dot_general(a, b, (((1,), (1,)), ((), ())), preferred_element_type=F32)


def _dot_tn(a, b):
    return lax.dot_general(a, b, (((0,), (0,)), ((), ())), preferred_element_type=F32)


def _staggered(stage_fns):
    live = list(stage_fns)
    step = 0
    done = [False] * len(live)
    while not all(done):
        for k, gen in enumerate(live):
            if step >= k and not done[k]:
                try:
                    next(gen)
                except StopIteration:
                    done[k] = True
        step += 1


def _norm_proj_kernel(x_ref, g_ref, lb_ref, *refs, acts, used_cols, col_chunk, sub):
    n_w = len(used_cols)
    w_src = refs[:n_w]
    o_refs = refs[n_w:n_w + len(acts)]
    w_refs = refs[n_w + len(acts):]

    @pl.when(pl.program_id(0) == 0)
    def _():
        for src, dst, used in zip(w_src, w_refs, used_cols):
            for c0 in range(0, used, col_chunk):
                c1 = min(c0 + col_chunk, used)
                dst[:, c0:c1] = src[:, c0:c1].astype(BF16)

    def stages(r0):
        rows = pl.ds(r0, sub)
        h = _rms(x_ref[rows, :], g_ref[...]).astype(BF16)
        yield
        piece, off = 0, 0
        for o_ref, act in zip(o_refs, acts):
            n = o_ref.shape[1]
            if off == used_cols[piece]:
                piece, off = piece + 1, 0
            w_ref = w_refs[piece]
            for c0 in range(0, n, col_chunk):
                c1 = min(c0 + col_chunk, n)
                y = _dot(h, w_ref[:, off + c0:off + c1])
                if act == "silu":
                    y = _silu(y)
                elif act == "log_gate":
                    lb = lb_ref[:, c0:c1]
                    y = jnp.log(lb + (1.0 - lb) * jax.nn.sigmoid(y))
                o_ref[rows, c0:c1] = y.astype(o_ref.dtype)
                yield
            off += n

    _staggered([stages(r0) for r0 in range(0, x_ref.shape[0], sub)])


def _norm_proj(x, g, w, outs, lb=None, tm=512, sub=256, col_chunk=1024):
    T, D = x.shape
    pieces = list(w) if isinstance(w, (list, tuple)) else [(w, w.shape[1])]
    ws = [v for v, _ in pieces]
    used_cols = tuple(n for _, n in pieces)
    widths = [o[0] for o in outs]
    assert sum(widths) == sum(used_cols) and T % tm == 0
    if lb is None:
        lb = jnp.zeros((LANES,), F32)
    resident = lambda v: pl.BlockSpec(v.shape, lambda i: (0, 0), pipeline_mode=pl.Buffered(1))
    return pl.pallas_call(
        functools.partial(_norm_proj_kernel, acts=tuple(o[2] for o in outs), used_cols=used_cols,
                          col_chunk=col_chunk, sub=sub),
        grid=(T // tm,),
        in_specs=[_row_spec(tm, D, 1), _whole_spec((1, D), 1), _whole_spec((1, lb.shape[0]), 1)]
                 + [resident(v) for v in ws],
        out_specs=[_row_spec(tm, n, 1) for n in widths],
        out_shape=[jax.ShapeDtypeStruct((T, n), dt) for n, dt, _ in outs],
        scratch_shapes=[pltpu.VMEM((D, n), BF16) for n in used_cols],
        compiler_params=_cparams(("arbitrary",)),
        name="norm_proj",
    )(x, g.reshape(1, D), lb.reshape(1, -1), *ws)


def _ple_update(x, p, gin_ref, wg_ref, wp_ref, gpost_ref):
    h = _rms(x, gin_ref[...]).astype(BF16)
    gate = jax.nn.sigmoid(_dot(h, wg_ref[...]))
    proj = _dot(p.astype(BF16), wp_ref[...])
    return x + _rms(gate * proj, gpost_ref[...])


def _row_spec(tm, width, n_grid):
    return pl.BlockSpec((tm, width), (lambda i: (i, 0)) if n_grid == 1 else (lambda i, j: (i, 0)))


def _whole_spec(shape, n_grid):
    zeros = (0,) * len(shape)
    return pl.BlockSpec(shape, (lambda i: zeros) if n_grid == 1 else (lambda i, j: zeros))


def _mix_ffn_ple_kernel(oa_ref, ob_ref, wa_ref, wb_ref, gmix_ref, x_ref, gpre_ref, wg_ref, wu_ref, wd_ref,
                        gpost_ref, p_ref, gin_ref, wpg_ref, wpp_ref, gple_ref, o_ref, *, sub, tf):
    tm = x_ref.shape[0]
    F = wg_ref.shape[1]
    def stages(r0):
        rows = pl.ds(r0, sub)
        mix = _dot(oa_ref[rows, :], wa_ref[...]) + _dot(ob_ref[rows, :], wb_ref[...])
        x1 = x_ref[rows, :] + _rms(mix, gmix_ref[...])
        h = _rms(x1, gpre_ref[...]).astype(BF16)
        yield
        acc = None
        for f0 in range(0, F, tf):
            f1 = min(f0 + tf, F)
            a = (_silu(_dot(h, wg_ref[:, f0:f1])) * _dot(h, wu_ref[:, f0:f1])).astype(BF16)
            part = _dot(a, wd_ref[f0:f1, :])
            acc = part if acc is None else acc + part
            yield
        x2 = x1 + _rms(acc, gpost_ref[...])
        o_ref[rows, :] = _ple_update(x2, p_ref[rows, :], gin_ref, wpg_ref, wpp_ref, gple_ref)
        yield

    _staggered([stages(r0) for r0 in range(0, tm, sub)])


def _mix_ffn_ple(o_a, o_b, w_a, w_b, g_mix, x, g_pre, wg, wu, wd, g_post, p, g_in, wpg, wpp, g_ple,
                 p_row0=0, tm=512, sub=256, tf=1536):
    T, D = x.shape
    F = wg.shape[1]
    assert T % tm == 0 and tm % sub == 0 and p_row0 % tm == 0
    p_blk0 = p_row0 // tm
    vec = lambda g: g.reshape(1, D)
    resident = lambda w: pl.BlockSpec(w.shape, lambda i: (0, 0), pipeline_mode=pl.Buffered(1))
    return pl.pallas_call(
        functools.partial(_mix_ffn_ple_kernel, sub=sub, tf=tf),
        grid=(T // tm,),
        in_specs=[_row_spec(tm, o_a.shape[1], 1), _row_spec(tm, o_b.shape[1], 1),
                  resident(w_a), resident(w_b), _whole_spec((1, D), 1),
                  _row_spec(tm, D, 1), _whole_spec((1, D), 1),
                  resident(wg), resident(wu), resident(wd),
                  _whole_spec((1, D), 1),
                  pl.BlockSpec((tm, p.shape[1]), lambda i: (i + p_blk0, 0)), _whole_spec((1, D), 1),
                  resident(wpg), resident(wpp), _whole_spec((1, D), 1)],
        out_specs=_row_spec(tm, D, 1),
        out_shape=jax.ShapeDtypeStruct((T, D), F32),
        compiler_params=_cparams(("parallel",)),
        name="mix_ffn_ple",
    )(o_a, o_b, w_a, w_b, vec(g_mix), x, vec(g_pre), wg, wu, wd, vec(g_post), p, vec(g_in), wpg, wpp,
      vec(g_ple))


def _roll_rows(x, s, rev):
    n = x.shape[0]
    return pltpu.roll(x, (n - s) if rev else s, 0)


def _cumsum_rows(x, tau, rev):
    n = x.shape[0]
    s = 1
    while s < n:
        if s % SUBLANES:
            shifted = jnp.where(tau >= s, _roll_rows(x, s, rev), 0.0)
        else:
            zeros = jnp.zeros((s,) + x.shape[1:], x.dtype)
            shifted = (jnp.concatenate([x[s:], zeros], axis=0) if rev
                       else jnp.concatenate([zeros, x[:n - s]], axis=0))
        x = x + shifted
        s *= 2
    return x


def _hgrn_levels(C):
    return [C >> (i + 1) for i in range(C.bit_length() - 1)]


def _hgrn_pair_classes(C):
    t = np.arange(C)[:, None]
    s = np.arange(C)[None, :]
    out = np.full((2 * C, 2 * C), -1, np.int32)
    for d, rev in enumerate((False, True)):
        tau, sig = (C - 1 - t, C - 1 - s) if rev else (t, s)
        blk = np.full((C, C), -1, np.int32)
        blk[t == s] = 0
        for i, L in enumerate(_hgrn_levels(C)):
            m = ((t & -(2 * L)) == (s & -(2 * L))) & ((tau & (2 * L - 1)) >= L) & ((sig & (2 * L - 1)) < L)
            blk[m] = i + 1
        out[d * C:(d + 1) * C, d * C:(d + 1) * C] = blk
    return out


def _hgrn_kernel(q_ref, gf_ref, gb_ref, i_ref, gate_ref, cls_ref, ng_ref, o_ref, acc_scr, st_scr):
    S, DK = q_ref.shape
    C = A_CHUNK
    nc = S // C
    assert nc % (2 * A_UNROLL) == 0
    ng = ng_ref[...]
    row = lax.broadcasted_iota(jnp.int32, (C, DK), 0)
    levels = _hgrn_levels(C)
    zero_half = jnp.zeros((C, DK), BF16)

    def stack(top, bottom):
        return jnp.concatenate([top, bottom], axis=0)

    def block_diag(x):
        return stack(jnp.concatenate([x[:C], zero_half], axis=1), jnp.concatenate([zero_half, x[C:]], axis=1))

    def make_decay(rev):
        tau = (C - 1 - row) if rev else row
        odd_rank = (tau & 1) == 1
        last = 0 if rev else C - 1

        def ref_rows(b, L):
            off = L if rev else L - 1
            if 2 * L >= SUBLANES:
                pieces = [jnp.broadcast_to(b[j * 2 * L + off:j * 2 * L + off + 1, :], (2 * L, DK))
                          for j in range(C // (2 * L))]
                return pieces[0] if len(pieces) == 1 else jnp.concatenate(pieces, axis=0)
            if L == 1:
                return jnp.where(odd_rank, _roll_rows(b, 1, rev), b)
            b3 = b.reshape(C // SUBLANES, SUBLANES, DK)
            sub = lax.broadcasted_iota(jnp.int32, b3.shape, 1)
            out = None
            for j in range(SUBLANES // (2 * L)):
                piece = jnp.broadcast_to(b3[:, j * 2 * L + off:j * 2 * L + off + 1, :], b3.shape)
                out = piece if out is None else jnp.where(sub >= j * 2 * L, piece, out)
            return out.reshape(C, DK)

        def decay(g):
            b = _cumsum_rows(g, tau, rev)
            return b, [ref_rows(b, L) for L in levels], b[last:last + 1, :]

        return decay

    decay_fwd, decay_bwd = make_decay(False), make_decay(True)
    pair_cls = cls_ref[...]
    st_scr[...] = jnp.zeros_like(st_scr)

    def chunk_pair(rows_f, rows_b, st):
        q = stack(q_ref[rows_f, :], q_ref[rows_b, :]).astype(F32)
        g_f, g_b = gf_ref[rows_f, :], gb_ref[rows_b, :]
        k = 1.0 - jnp.exp(stack(g_f, g_b))
        vb = stack(i_ref[rows_f, :], i_ref[rows_b, :])
        b_f, refs_f, last_f = decay_fwd(g_f)
        b_b, refs_b, last_b = decay_bwd(g_b)
        b = stack(b_f, b_b)

        o = _dot_nt(block_diag((q * jnp.exp(b)).astype(BF16)), st.astype(BF16))

        qb, kb = q.astype(BF16), k.astype(BF16)
        attn = jnp.where(pair_cls == 0, _dot_nt(qb, kb), 0.0)
        for i in range(len(levels)):
            e = jnp.exp(-jnp.abs(b - stack(refs_f[i], refs_b[i]))).astype(BF16)
            attn = jnp.where(pair_cls == i + 1, _dot_nt(qb * e, kb * e), attn)
        o = o + _dot(attn.astype(BF16), vb)

        b_last = stack(jnp.broadcast_to(last_f, (C, DK)), jnp.broadcast_to(last_b, (C, DK)))
        khat = block_diag((k * jnp.exp(b_last - b)).astype(BF16))
        keep = jnp.exp(jnp.concatenate([last_f, last_b], axis=1))
        return o, st * keep + _dot_tn(vb, khat)

    def make_body(final):
        def body(it, carry):
            st = st_scr[...]
            for u in range(A_UNROLL):
                ci = it * A_UNROLL + u
                rows_f = pl.ds(pl.multiple_of(ci * C, C), C)
                rows_b = pl.ds(pl.multiple_of((nc - 1 - ci) * C, C), C)
                o, st = chunk_pair(rows_f, rows_b, st)
                for rows, part in ((rows_f, o[:C]), (rows_b, o[C:])):
                    if final:
                        tot = acc_scr[rows, :] + part
                        o_ref[rows, :] = (_rms(tot, ng) * gate_ref[rows, :].astype(F32)).astype(o_ref.dtype)
                    else:
                        acc_scr[rows, :] = part
            st_scr[...] = st
            return carry

        return body

    trips = nc // A_UNROLL
    lax.fori_loop(0, trips // 2, make_body(False), 0)
    lax.fori_loop(trips // 2, trips, make_body(True), 0)


def _hgrn_mixer(q, g_fwd, g_bwd, v, gate, norm_g, batch, seq):
    T = q.shape[0]
    H, DK = A_HEADS, A_DK
    head = pl.BlockSpec((seq, DK), lambda b, h: (b, h))
    classes = jnp.asarray(_hgrn_pair_classes(A_CHUNK))
    return pl.pallas_call(
        _hgrn_kernel,
        grid=(batch, H),
        in_specs=[head, head, head, head, head,
                  pl.BlockSpec(classes.shape, lambda b, h: (0, 0)),
                  pl.BlockSpec((1, DK), lambda b, h: (0, 0))],
        out_specs=pl.BlockSpec((seq, DK), lambda b, h: (b, h)),
        out_shape=jax.ShapeDtypeStruct((T, H * DK), BF16),
        scratch_shapes=[pltpu.VMEM((seq, DK), F32), pltpu.VMEM((DK, 2 * DK), F32)],
        compiler_params=_cparams(("parallel", "parallel")),
        name="hgrn2",
    )(q, g_fwd, g_bwd, v, gate, classes, norm_g.reshape(1, DK))


def _expand_heads(col, j0, width):
    q = col.shape[0]
    lane = lax.broadcasted_iota(jnp.int32, (q, width), 1)
    out = jnp.broadcast_to(col[:, j0 + B_HPG - 1:j0 + B_HPG], (q, width))
    for j in range(B_HPG - 2, -1, -1):
        out = jnp.where(lane < (j + 1) * B_HEADDIM,
                        jnp.broadcast_to(col[:, j0 + j:j0 + j + 1], (q, width)), out)
    return out


def _ssd_kernel(z_ref, x_ref, bm_ref, cm_ref, dt_ref, cwx_ref, cwb_ref, cwc_ref,
                cbx_ref, cbb_ref, cbc_ref, hp_ref, dsk_ref, ng_ref, o_ref,
                xs_scr, b_scr, c_scr, y_scr, *st_scrs):
    S = x_ref.shape[0]
    Q = B_CHUNK
    nc = S // Q
    GW = B_GW
    hp = hp_ref[0]
    a_row, dtb_row = hp[0:1, :], hp[1:2, :]
    row = lax.broadcasted_iota(jnp.int32, (Q, LANES), 0)
    t2 = lax.broadcasted_iota(jnp.int32, (Q, Q), 0)
    s2 = lax.broadcasted_iota(jnp.int32, (Q, Q), 1)
    lane_gw = lax.broadcasted_iota(jnp.int32, (Q, GW), 1)
    head_mask = [(lane_gw >= j * B_HEADDIM) & (lane_gw < (j + 1) * B_HEADDIM) for j in range(B_HPG)]
    halo = SUBLANES

    def conv_body(c, carry):
        r0 = pl.multiple_of(c * Q, Q)
        rp = pl.multiple_of(jnp.maximum(r0 - halo, 0), halo)
        rn = pl.multiple_of(jnp.minimum(r0 + Q, S - halo), halo)
        has_prev = c > 0
        has_next = c < nc - 1

        def conv(src_ref, w_ref, bias_ref):
            prev = jnp.where(has_prev, src_ref[pl.ds(rp, halo), :].astype(F32), 0.0)
            nxt = jnp.where(has_next, src_ref[pl.ds(rn, halo), :].astype(F32), 0.0)
            xx = jnp.concatenate([prev, src_ref[pl.ds(r0, Q), :].astype(F32), nxt], axis=0)
            n = Q + 2 * halo
            w = w_ref[0]
            acc = None
            for j in range(B_CONV):
                delta = j - B_CONV // 2
                sh = xx if delta == 0 else pltpu.roll(xx, (-delta) % n, 0)
                term = sh[halo:halo + Q, :] * w[j:j + 1, :]
                acc = term if acc is None else acc + term
            return _silu(acc + bias_ref[0])

        xs_scr[pl.ds(r0, Q), :] = conv(x_ref, cwx_ref, cbx_ref)
        b_scr[pl.ds(r0, Q), :] = conv(bm_ref, cwb_ref, cbb_ref).astype(BF16)
        c_scr[pl.ds(r0, Q), :] = conv(cm_ref, cwc_ref, cbc_ref).astype(BF16)
        return carry

    lax.fori_loop(0, nc, conv_body, 0)

    def make_chunk(rev):
        tau = (Q - 1 - row) if rev else row
        last = 0 if rev else Q - 1
        j0 = B_HPG if rev else 0
        pair_ok = (s2 >= t2) if rev else (t2 >= s2)

        def chunk(rows, st):
            dt = jax.nn.softplus(dt_ref[rows, :] + dtb_row)
            cs = _cumsum_rows(dt * a_row, tau, rev)
            cs_t = cs.T
            xs = xs_scr[rows, :]
            bm = b_scr[rows, :]
            cm = c_scr[rows, :]
            xdt = xs * _expand_heads(dt, j0, GW)
            ecs = _expand_heads(cs, j0, GW)
            ecs_last = ecs[last:last + 1, :]

            y = _dot(cm, st.astype(BF16)) * jnp.exp(ecs)

            scores = _dot_nt(cm, bm)
            for j in range(B_HPG):
                col = jnp.broadcast_to(cs[:, j0 + j:j0 + j + 1], (Q, Q))
                rw = jnp.broadcast_to(cs_t[j0 + j:j0 + j + 1, :], (Q, Q))
                decay = jnp.where(pair_ok, jnp.exp(jnp.minimum(col - rw, 0.0)), 0.0)
                xh = jnp.where(head_mask[j], xdt, 0.0).astype(BF16)
                y = y + _dot((scores * decay).astype(BF16), xh)

            xdec = (xdt * jnp.exp(ecs_last - ecs)).astype(BF16)
            return y, st * jnp.exp(ecs_last) + _dot_tn(bm, xdec)

        return chunk

    chunk_fns = (make_chunk(False), make_chunk(True))
    for st_scr in st_scrs:
        st_scr[...] = jnp.zeros_like(st_scr)

    def make_body(final, unroll):
        def body(it, carry):
            for rev, chunk, st_scr in zip((False, True), chunk_fns, st_scrs):
                st = st_scr[...]
                for u in range(unroll):
                    ci = it * unroll + u
                    c = (nc - 1 - ci) if rev else ci
                    rows = pl.ds(pl.multiple_of(c * Q, Q), Q)
                    y, st = chunk(rows, st)
                    if final:
                        tot = y_scr[rows, :] + y + dsk_ref[0] * xs_scr[rows, :]
                        tot = tot * _silu(z_ref[rows, :].astype(F32))
                        o_ref[rows, :] = _rms(tot, ng_ref[0]).astype(o_ref.dtype)
                    else:
                        y_scr[rows, :] = y
                st_scr[...] = st
            return carry

        return body

    assert nc % (2 * B_UNROLL) == 0
    lax.fori_loop(0, nc // (2 * B_UNROLL), make_body(False, B_UNROLL), 0)
    lax.fori_loop(nc // 2, nc, make_body(True, 1), 0)


def _ssd_mixer(u, dt, conv_w, conv_b, a_log, dt_bias, d_skip, norm_g, batch, seq, col0):
    T = u.shape[0]
    G, GW, N = B_GROUPS, B_GW, B_STATE
    W = B_HEADS * B_HEADDIM
    assert col0 % GW == 0 and GW == 2 * N
    z_blk = col0 // GW
    x_blk = z_blk + W // GW
    b_blk = (col0 + 2 * W) // N
    c_blk = b_blk + G

    def pad_rows(w):
        return jnp.pad(w, ((0, SUBLANES - w.shape[0]), (0, 0)))

    cwx = pad_rows(conv_w[:, :W]).reshape(SUBLANES, G, GW).transpose(1, 0, 2)
    cwb = pad_rows(conv_w[:, W:W + G * N]).reshape(SUBLANES, G, N).transpose(1, 0, 2)
    cwc = pad_rows(conv_w[:, W + G * N:]).reshape(SUBLANES, G, N).transpose(1, 0, 2)
    cbx = conv_b[:W].reshape(G, 1, GW)
    cbb = conv_b[W:W + G * N].reshape(G, 1, N)
    cbc = conv_b[W + G * N:].reshape(G, 1, N)
    a_neg = -jnp.exp(a_log.astype(F32))
    per_dir = lambda v: jnp.concatenate([v[0].reshape(G, B_HPG), v[1].reshape(G, B_HPG)], axis=1)
    hp = jnp.stack([per_dir(a_neg), per_dir(dt_bias.astype(F32))], axis=1)
    hp = jnp.pad(hp, ((0, 0), (0, SUBLANES - 2), (0, LANES - 2 * B_HPG)))
    dsk = jnp.repeat(d_skip.astype(F32), B_HEADDIM).reshape(G, 1, GW)
    ng = norm_g.reshape(G, 1, GW)

    gspec = lambda shape: pl.BlockSpec((1,) + shape, lambda b, g: (g, 0, 0))
    return pl.pallas_call(
        _ssd_kernel,
        grid=(batch, G),
        in_specs=[pl.BlockSpec((seq, GW), lambda b, g: (b, z_blk + g)),
                  pl.BlockSpec((seq, GW), lambda b, g: (b, x_blk + g)),
                  pl.BlockSpec((seq, N), lambda b, g: (b, b_blk + g)),
                  pl.BlockSpec((seq, N), lambda b, g: (b, c_blk + g)),
                  pl.BlockSpec((seq, LANES), lambda b, g: (b, g)),
                  gspec((SUBLANES, GW)), gspec((SUBLANES, N)), gspec((SUBLANES, N)),
                  gspec((1, GW)), gspec((1, N)), gspec((1, N)),
                  gspec((SUBLANES, LANES)), gspec((1, GW)), gspec((1, GW))],
        out_specs=pl.BlockSpec((seq, GW), lambda b, g: (b, g)),
        out_shape=jax.ShapeDtypeStruct((T, W), BF16),
        scratch_shapes=[pltpu.VMEM((seq, GW), F32), pltpu.VMEM((seq, N), BF16),
                        pltpu.VMEM((seq, N), BF16), pltpu.VMEM((seq, GW), F32),
                        pltpu.VMEM((N, GW), F32), pltpu.VMEM((N, GW), F32)],
        compiler_params=_cparams(("parallel", "parallel")),
        name="ssd",
    )(u, u, u, u, dt, cwx, cwb, cwc, cbx, cbb, cbc, hp, dsk, ng)


def _na_kernel(q_ref, k_ref, v_ref, bias_ref, o_ref):
    S, HW = q_ref.shape
    W = GRID_W
    n_rows = S // W
    kh = min(NA_ROWS, n_rows)
    nk = kh * W
    n_pairs = bias_ref.shape[1] // NA_HG
    lane = lax.broadcasted_iota(jnp.int32, (W, HW), 1)
    head_masks = [(lane >= h * C_HEADDIM) & (lane < (h + 1) * C_HEADDIM) for h in range(NA_HG)]
    scale = C_HEADDIM ** -0.5

    def body(r, carry):
        rs = jnp.clip(r - kh // 2, 0, n_rows - kh)
        q = q_ref[pl.ds(pl.multiple_of(r * W, W), W), :] * jnp.asarray(scale, q_ref.dtype)
        kw = k_ref[pl.ds(pl.multiple_of(rs * W, W), nk), :]
        vw = v_ref[pl.ds(pl.multiple_of(rs * W, W), nk), :]
        zero = jnp.zeros_like(q)
        qs = jnp.concatenate([jnp.where(m, q, zero) for m in head_masks], axis=0)
        d0 = (NA_ROWS - 1) - (r - rs)
        bias = jnp.concatenate(
            [jnp.concatenate([bias_ref[0, h * n_pairs + d0 + 2 * m] for m in range(kh // 2)], axis=1)
             for h in range(NA_HG)], axis=0)
        s = _dot_nt(qs, kw) + bias
        m = jnp.max(s, axis=-1, keepdims=True)
        p = jnp.exp(s - m)
        l = jnp.sum(p, axis=-1, keepdims=True)
        res = _dot(p.astype(BF16), vw) / l
        out = jnp.zeros((W, HW), F32)
        for h in range(NA_HG):
            out = jnp.where(head_masks[h], res[h * W:(h + 1) * W, :], out)
        o_ref[pl.ds(pl.multiple_of(r * W, W), W), :] = out.astype(o_ref.dtype)
        return carry

    lax.fori_loop(0, n_rows, body, 0, unroll=NA_UNROLL)


def _na_bias_table(rpb):
    H, n_dr, n_dc = rpb.shape
    W = GRID_W
    c = np.arange(W)
    qs = np.clip(c - NA_COLS // 2, 0, W - NA_COLS)
    valid = (c[None, :] >= qs[:, None]) & (c[None, :] < qs[:, None] + NA_COLS)
    dc = np.clip(c[None, :] - c[:, None] + (NA_COLS - 1), 0, n_dc - 1)
    onehot = np.zeros((2, n_dc, W, 2 * W), np.float32)
    for half in range(2):
        onehot[half, :, :, half * W:(half + 1) * W] = dc[None] == np.arange(n_dc)[:, None, None]
    both = jnp.concatenate([rpb[:, :-1], rpb[:, 1:]], axis=2).astype(F32)
    pairs = jnp.einsum("hre,ecx->hrcx", both, jnp.asarray(onehot.reshape(2 * n_dc, W, 2 * W)),
                       precision=lax.Precision.HIGHEST)
    valid2 = np.concatenate([valid, valid], axis=1)
    pairs = jnp.where(jnp.asarray(valid2)[None, None], pairs, -jnp.inf)
    return pairs.reshape(H // NA_HG, NA_HG * (n_dr - 1), W, 2 * W)


def _neighborhood_attention(qkv, rpb, batch, seq):
    T = qkv.shape[0]
    HW = NA_HG * C_HEADDIM
    n_hg = C_HEADS // NA_HG
    n_rows = seq // GRID_W
    assert n_rows >= NA_ROWS and NA_ROWS % 2 == 0
    table = _na_bias_table(rpb)
    return pl.pallas_call(
        _na_kernel,
        grid=(batch, n_hg),
        in_specs=[pl.BlockSpec((seq, HW), lambda b, g: (b, g)),
                  pl.BlockSpec((seq, HW), lambda b, g: (b, n_hg + g)),
                  pl.BlockSpec((seq, HW), lambda b, g: (b, 2 * n_hg + g)),
                  pl.BlockSpec((1,) + table.shape[1:], lambda b, g: (g, 0, 0, 0))],
        out_specs=pl.BlockSpec((seq, HW), lambda b, g: (b, g)),
        out_shape=jax.ShapeDtypeStruct((T, C_HEADS * C_HEADDIM), BF16),
        compiler_params=_cparams(("parallel", "parallel")),
        name="natten",
    )(qkv, qkv, qkv, table)


def _store_token_tiles(ref, val, t0=0):
    count, width = val.shape
    n = width // LANES
    for j in range(n):
        ref[pl.ds(t0 * n + j, count, stride=n), :] = val[:, j * LANES:(j + 1) * LANES]


def _load_token_tiles(ref, n, t0=0, count=None):
    if count is None:
        count = ref.shape[0] // n - t0
    return jnp.concatenate([ref[pl.ds(t0 * n + j, count, stride=n), :] for j in range(n)], axis=1)


def _proj_router_kernel(a_ref, w_ref, gmix_ref, x_ref, g_ref, wr_ref, x1_ref, o_ref, h_ref):
    tm = x_ref.shape[0]
    sub = tm

    def stages(r0):
        rows = pl.ds(r0, sub)
        x1 = x_ref[rows, :] + _rms(_dot(a_ref[rows, :], w_ref[...]), gmix_ref[...])
        x1_ref[rows, :] = x1
        yield
        h = _rms(x1, g_ref[...])
        _store_token_tiles(h_ref, h, r0)
        h_hi = h.astype(BF16)
        h_lo = (h - h_hi.astype(F32)).astype(BF16)
        both = _dot(h_hi, wr_ref[...])
        logits = both[:, :LANES] + both[:, LANES:] + _dot(h_lo, wr_ref[:, :LANES])
        yield
        lane = lax.broadcasted_iota(jnp.int32, logits.shape, 1)
        neg = -jnp.inf
        l1 = jnp.where(lane < N_EXPERTS, logits, neg)
        m1 = jnp.max(l1, axis=-1, keepdims=True)
        i1 = jnp.min(jnp.where(l1 == m1, lane, LANES), axis=-1, keepdims=True)
        l2 = jnp.where(lane == i1, neg, l1)
        m2 = jnp.max(l2, axis=-1, keepdims=True)
        i2 = jnp.min(jnp.where(l2 == m2, lane, LANES), axis=-1, keepdims=True)
        e = jnp.exp(m2 - m1)
        w1 = 1.0 / (1.0 + e)
        w2 = e / (1.0 + e)
        o_ref[rows, :] = jnp.where(lane == 0, i1.astype(F32),
                                   jnp.where(lane == 1, i2.astype(F32),
                                             jnp.where(lane == 2, w1, jnp.where(lane == 3, w2, 0.0))))
        yield

    _staggered([stages(r0) for r0 in range(0, tm, sub)])


def _proj_router(a, w, g_mix, x, g, w_router, tm=512):
    T, D = x.shape
    n = D // LANES
    wr = jnp.pad(w_router.astype(F32), ((0, 0), (0, LANES - w_router.shape[1])))
    wr_hi = wr.astype(BF16)
    wr = jnp.concatenate([wr_hi, (wr - wr_hi.astype(F32)).astype(BF16)], axis=1)
    x1, route, h = pl.pallas_call(
        _proj_router_kernel,
        grid=(T // tm,),
        in_specs=[_row_spec(tm, a.shape[1], 1), _whole_spec(w.shape, 1), _whole_spec((1, D), 1),
                  _row_spec(tm, D, 1), _whole_spec((1, D), 1), _whole_spec((D, 2 * LANES), 1)],
        out_specs=[_row_spec(tm, D, 1), _row_spec(tm, LANES, 1), _row_spec(tm * n, LANES, 1)],
        out_shape=[jax.ShapeDtypeStruct((T, D), F32),
                   jax.ShapeDtypeStruct((T, LANES), F32),
                   jax.ShapeDtypeStruct((T * n, LANES), F32)],
        compiler_params=_cparams(("parallel",)),
        name="proj_router",
    )(a, w, g_mix.reshape(1, D), x, g.reshape(1, D), wr)
    return x1, route, h.reshape(T, n, LANES)


SC_CORES = 2
SC_SUBCORES = 16
SC_WORKERS = SC_CORES * SC_SUBCORES
SC_CHUNK = 32


def _sc_mesh():
    return plsc.VectorSubcoreMesh(core_axis_name="c", subcore_axis_name="s",
                                  num_cores=SC_CORES, num_subcores=SC_SUBCORES)


def _sc_index_blocks(idx):
    return idx.reshape(SC_WORKERS, -1, SC_CHUNK)


def _sc_scatter_tokens(h, dest, n_rows, row0=0):
    T = dest.shape[0]
    _, n, _ = h.shape
    per_worker = T // SC_WORKERS
    n_chunks = per_worker // SC_CHUNK
    assert per_worker * SC_WORKERS == T and n_chunks * SC_CHUNK == per_worker

    @functools.partial(
        pl.kernel, mesh=_sc_mesh(),
        out_type=jax.ShapeDtypeStruct((n_rows, n, LANES), h.dtype),
        scratch_types=[pltpu.VMEM((n_chunks, SC_CHUNK), jnp.int32),
                       pltpu.VMEM((n_chunks, SC_CHUNK), jnp.int32),
                       pltpu.VMEM((SC_CHUNK, n, LANES), h.dtype)],
        name="sc_scatter_tokens",
    )
    def scatter(h_hbm, d0_hbm, d1_hbm, o_hbm, i0_v, i1_v, rows_v):
        wid = lax.axis_index("s") * SC_CORES + lax.axis_index("c")
        pltpu.sync_copy(d0_hbm.at[wid], i0_v)
        pltpu.sync_copy(d1_hbm.at[wid], i1_v)

        @pl.loop(0, n_chunks)
        def _(j):
            pltpu.sync_copy(h_hbm.at[pl.ds(row0 + wid * per_worker + j * SC_CHUNK, SC_CHUNK)], rows_v)
            pltpu.sync_copy(rows_v, o_hbm.at[i0_v.at[j]])
            pltpu.sync_copy(rows_v, o_hbm.at[i1_v.at[j]])

    return scatter(h, _sc_index_blocks(dest[:, 0]), _sc_index_blocks(dest[:, 1]))


def _sc_gather_tokens(y, dest):
    T = dest.shape[0]
    _, n, _ = y.shape
    per_worker = T // SC_WORKERS
    n_chunks = per_worker // SC_CHUNK
    assert per_worker * SC_WORKERS == T and n_chunks * SC_CHUNK == per_worker
    out = jax.ShapeDtypeStruct((T, n, LANES), y.dtype)

    @functools.partial(
        pl.kernel, mesh=_sc_mesh(), out_type=(out, out),
        scratch_types=[pltpu.VMEM((n_chunks, SC_CHUNK), jnp.int32),
                       pltpu.VMEM((n_chunks, SC_CHUNK), jnp.int32),
                       pltpu.VMEM((SC_CHUNK, n, LANES), y.dtype)],
        name="sc_gather_tokens",
    )
    def gather(y_hbm, d0_hbm, d1_hbm, o0_hbm, o1_hbm, i0_v, i1_v, rows_v):
        wid = lax.axis_index("s") * SC_CORES + lax.axis_index("c")
        pltpu.sync_copy(d0_hbm.at[wid], i0_v)
        pltpu.sync_copy(d1_hbm.at[wid], i1_v)

        @pl.loop(0, n_chunks)
        def _(j):
            rows = pl.ds(wid * per_worker + j * SC_CHUNK, SC_CHUNK)
            pltpu.sync_copy(y_hbm.at[i0_v.at[j]], rows_v)
            pltpu.sync_copy(rows_v, o0_hbm.at[rows])
            pltpu.sync_copy(y_hbm.at[i1_v.at[j]], rows_v)
            pltpu.sync_copy(rows_v, o1_hbm.at[rows])

    return gather(y, _sc_index_blocks(dest[:, 0]), _sc_index_blocks(dest[:, 1]))


SC_PACK_PAIRS = 8
SC_LANES = 16


def _sc_pack_bf16_rows(w):
    R, C = w.shape
    pairs = R // 2
    per_worker = pairs // SC_WORKERS
    n_chunks = per_worker // SC_PACK_PAIRS
    assert n_chunks * SC_PACK_PAIRS * SC_WORKERS * 2 == R and C % SC_LANES == 0

    def round_bits(v):
        u = plsc.bitcast(v, jnp.int32)
        return u + 0x7FFF + (lax.shift_right_logical(u, 16) & 1)

    @functools.partial(
        pl.kernel, mesh=_sc_mesh(),
        out_type=jax.ShapeDtypeStruct((pairs, C), jnp.int32),
        scratch_types=[pltpu.VMEM((2 * SC_PACK_PAIRS, C), F32), pltpu.VMEM((SC_PACK_PAIRS, C), jnp.int32)],
        compiler_params=pltpu.CompilerParams(use_tc_tiling_on_sc=True, needs_layout_passes=False),
        name="sc_pack_bf16_rows",
    )
    def pack(w_hbm, o_hbm, in_v, out_v):
        wid = lax.axis_index("s") * SC_CORES + lax.axis_index("c")

        @pl.loop(0, n_chunks)
        def _(j):
            p0 = (wid * n_chunks + j) * SC_PACK_PAIRS
            pltpu.sync_copy(w_hbm.at[pl.ds(2 * p0, 2 * SC_PACK_PAIRS)], in_v)

            @pl.loop(0, C, step=SC_LANES)
            def _(c):
                cols = pl.ds(c, SC_LANES)
                for i in range(SC_PACK_PAIRS):
                    lo = lax.shift_right_logical(round_bits(in_v[2 * i, cols]), 16)
                    hi = round_bits(in_v[2 * i + 1, cols]) & jnp.int32(-65536)
                    out_v[i, cols] = lo | hi

            pltpu.sync_copy(out_v, o_hbm.at[pl.ds(p0, SC_PACK_PAIRS)])

    return pack(w)


def _moe_ffn_kernel(te_ref, na_ref, h_ref, wg_ref, wu_ref, wd_ref, o_ref, acc_scr, *, tf):
    i = pl.program_id(0)
    D = acc_scr.shape[1]
    F = wg_ref.shape[2]
    unpack = lambda packed: pltpu.bitcast(packed, BF16)

    @pl.when(i < na_ref[0])
    def _():
        h = _load_token_tiles(h_ref, D // LANES).astype(BF16)
        for f0 in range(0, F, tf):
            a = (_silu(_dot(h, unpack(wg_ref[0, :, f0:f0 + tf])))
                 * _dot(h, unpack(wu_ref[0, :, f0:f0 + tf]))).astype(BF16)
            part = _dot(a, unpack(wd_ref[0, f0 // 2:(f0 + tf) // 2, :]))
            if f0 == 0:
                acc_scr[...] = part
            else:
                acc_scr[...] += part
        _store_token_tiles(o_ref, acc_scr[...])

    @pl.when(i >= na_ref[0])
    def _():
        o_ref[...] = jnp.zeros_like(o_ref)


def _moe_ffn(hs, wg, wu, wd, tile_expert, n_active, tm=MOE_TM, tf=512):
    rows, n, _ = hs.shape
    E, half_d, F = wg.shape
    D = 2 * half_d
    nt = tile_expert.shape[0]
    assert rows == nt * tm and n * LANES == D and wd.shape == (E, F // 2, D)
    grid_spec = pltpu.PrefetchScalarGridSpec(
        num_scalar_prefetch=2,
        grid=(nt,),
        in_specs=[pl.BlockSpec((tm * n, LANES), lambda i, te, na: (i, 0)),
                  pl.BlockSpec((1, D // 2, F), lambda i, te, na: (te[i], 0, 0)),
                  pl.BlockSpec((1, D // 2, F), lambda i, te, na: (te[i], 0, 0)),
                  pl.BlockSpec((1, F // 2, D), lambda i, te, na: (te[i], 0, 0))],
        out_specs=pl.BlockSpec((tm * n, LANES), lambda i, te, na: (i, 0)),
        scratch_shapes=[pltpu.VMEM((tm, D), F32)],
    )
    y = pl.pallas_call(
        functools.partial(_moe_ffn_kernel, tf=tf),
        grid_spec=grid_spec,
        out_shape=jax.ShapeDtypeStruct((rows * n, LANES), F32),
        compiler_params=_cparams(("arbitrary",)),
        name="moe_ffn",
    )(tile_expert, n_active, hs.reshape(rows * n, LANES), wg, wu, wd)
    return y.reshape(rows, n, LANES)


def _combine_ple_kernel(y0_ref, y1_ref, route_ref, x_ref, g_ref, p_ref, gin_ref, wpg_ref, wpp_ref,
                        gple_ref, *rest):
    o_ref = rest[-1]
    tm, D = x_ref.shape
    n = D // LANES
    sub = min(tm, 256)

    def stages(r0):
        rows = pl.ds(r0, sub)
        route = route_ref[rows, :]
        moe = (route[:, 2:3] * _load_token_tiles(y0_ref, n, r0, sub)
               + route[:, 3:4] * _load_token_tiles(y1_ref, n, r0, sub))
        x2 = x_ref[rows, :] + _rms(moe, g_ref[...])
        yield
        o_ref[rows, :] = _ple_update(x2, p_ref[rows, :], gin_ref, wpg_ref, wpp_ref, gple_ref)
        yield

    _staggered([stages(r0) for r0 in range(0, tm, sub)])


def _combine_ple(y0, y1, route, x, g, p, g_in, wpg, wpp, g_ple, row0=0, prev=None, p_row0=0, tm=512):
    T, D = x.shape
    Tp, n, _ = y0.shape
    assert row0 % tm == 0 and Tp % tm == 0 and n * LANES == D and p_row0 % tm == 0
    blk0 = row0 // tm
    p_blk0 = (row0 + p_row0) // tm
    vec = lambda v: v.reshape(1, D)
    shifted = lambda width: pl.BlockSpec((tm, width), lambda i: (i + blk0, 0))
    in_specs = [_row_spec(tm * n, LANES, 1), _row_spec(tm * n, LANES, 1), shifted(LANES),
                shifted(D), _whole_spec((1, D), 1),
                pl.BlockSpec((tm, p.shape[1]), lambda i: (i + p_blk0, 0)), _whole_spec((1, D), 1),
                _whole_spec(wpg.shape, 1), _whole_spec(wpp.shape, 1), _whole_spec((1, D), 1)]
    args = [y0.reshape(Tp * n, LANES), y1.reshape(Tp * n, LANES), route, x, vec(g), p, vec(g_in), wpg, wpp,
            vec(g_ple)]
    aliases = {}
    if prev is not None:
        in_specs.append(pl.BlockSpec(memory_space=pl.ANY))
        args.append(prev)
        aliases = {len(args) - 1: 0}
    return pl.pallas_call(
        _combine_ple_kernel,
        grid=(Tp // tm,),
        in_specs=in_specs,
        out_specs=shifted(D),
        out_shape=jax.ShapeDtypeStruct((T, D), F32),
        input_output_aliases=aliases,
        compiler_params=_cparams(("parallel",)),
        name="combine_ple",
    )(*args)


def _moe_plan(route, tm=MOE_TM):
    T = route.shape[0]
    e = route[:, :TOP_K].astype(jnp.int32).reshape(-1)
    onehot = (e[:, None] == jnp.arange(N_EXPERTS, dtype=jnp.int32)[None, :]).astype(jnp.int32)
    csum = jnp.cumsum(onehot, axis=0)
    rank = jnp.sum(csum * onehot, axis=1) - 1
    counts = csum[-1]
    padded = ((counts + tm - 1) // tm) * tm
    ends = jnp.cumsum(padded)
    starts = ends - padded
    dest = jnp.sum(starts[None, :] * onehot, axis=1) + rank
    nt = (T * TOP_K) // tm + N_EXPERTS
    tile_start = jnp.arange(nt, dtype=jnp.int32) * tm
    tile_expert = jnp.minimum(jnp.sum((tile_start[:, None] >= ends[None, :]).astype(jnp.int32), axis=1),
                              N_EXPERTS - 1).astype(jnp.int32)
    n_active = (ends[-1] // tm).astype(jnp.int32).reshape(1)
    return tile_expert, n_active, dest.reshape(T, TOP_K).astype(jnp.int32)


def kernel(x, p, hgrn_lb_raw, e_norm_mix_pre, e_w_in, e_conv_w, e_conv_b, e_A_log, e_dt_bias, e_D, e_a_norm, e_b_norm, e_w_out, e_norm_mix_post, e_norm_ffn_pre, e_w_ffn_gate, e_w_ffn_up, e_w_ffn_down, e_norm_ffn_post, o_norm_mix_pre, o_w_qkv, o_rpb, o_w_out, o_norm_mix_post, o_norm_ffn_pre, o_w_router, o_w_exp_gate, o_w_exp_up, o_w_exp_down, o_norm_ffn_post, ple_norm_in, ple_w_gate, ple_w_proj, ple_norm_post):
    batch, seq, d_model = x.shape
    depth = p.shape[0]
    T = batch * seq
    xt = x.reshape(T, d_model)
    p_all = p.reshape(depth * T, -1)
    lb_all =jnp.cumsum(jax.nn.softmax(hgrn_lb_raw.astype(F32), axis=0), axis=0)

    a_kdim = A_HEADS * A_DK
    b_width = B_HEADS * B_HEADDIM
    conv_dim = b_width + 2 * B_GROUPS * B_STATE
    main_w = 5 * a_kdim + b_width + conv_dim

    def pack_experts(w):
        E, R, C = w.shape
        return _sc_pack_bf16_rows(w.reshape(E * R, C)).reshape(E, R // 2, C)

    for li in range(depth):
        j = li // 2
        ple =(ple_norm_in[li], ple_w_gate[li].astype(BF16), ple_w_proj[li].astype(BF16), ple_norm_post[li])
        if li % 2 == 0:
            w_in = e_w_in[j]
            dtf = w_in[:, main_w:main_w + B_HEADS].reshape(d_model, B_GROUPS, B_HPG)
            dtb = w_in[:, main_w + B_HEADS:].reshape(d_model, B_GROUPS, B_HPG)
            w_dt = jnp.pad(jnp.concatenate([dtf, dtb], axis=2), ((0, 0), (0, 0), (0, LANES - 2 * B_HPG)))
            w_pieces = [(w_in, main_w), (w_dt.reshape(d_model, B_GROUPS * LANES), B_GROUPS * LANES)]
            outs = [(a_kdim, BF16, "silu"), (a_kdim, F32, "log_gate"), (a_kdim, F32, "log_gate"),
                    (a_kdim, BF16, None), (a_kdim, BF16, "silu"),
                    (b_width + conv_dim, BF16, None), (B_GROUPS * LANES, F32, None)]
            q_a, gf_a, gb_a, i_a, gate_a, u_b, dt = _norm_proj(
                xt, e_norm_mix_pre[j], w_pieces, outs, lb=lb_all[li])
            o_a = _hgrn_mixer(q_a, gf_a, gb_a, i_a, gate_a, e_a_norm[j], batch, seq)
            o_b = _ssd_mixer(u_b, dt, e_conv_w[j], e_conv_b[j], e_A_log[j], e_dt_bias[j], e_D[j],
                             e_b_norm[j], batch, seq, 0)
            w_out = e_w_out[j].astype(BF16)
            xt = _mix_ffn_ple(o_a, o_b, w_out[:a_kdim], w_out[a_kdim:], e_norm_mix_post[j], xt,
                              e_norm_ffn_pre[j], e_w_ffn_gate[j].astype(BF16), e_w_ffn_up[j].astype(BF16),
                              e_w_ffn_down[j].astype(BF16), e_norm_ffn_post[j], p_all, *ple, p_row0=li * T)
        else:
            (qkv,) = _norm_proj(xt, o_norm_mix_pre[j], o_w_qkv[j],
                                [(3 * C_HEADS * C_HEADDIM, BF16, None)])
            o_c = _neighborhood_attention(qkv, o_rpb[j], batch, seq)
            xt, route, h = _proj_router(o_c, o_w_out[j].astype(BF16), o_norm_mix_post[j], xt,
                                        o_norm_ffn_pre[j], o_w_router[j])
            expert_w = [pack_experts(w[j]) for w in (o_w_exp_gate, o_w_exp_up, o_w_exp_down)]
            x_mid, xt = xt, None
            t_grp = T // MOE_GROUPS
            for s in range(MOE_GROUPS):
                tile_expert, n_active, dest = _moe_plan(route[s * t_grp:(s + 1) * t_grp])
                hs = _sc_scatter_tokens(h, dest, tile_expert.shape[0] * MOE_TM, row0=s * t_grp)
                ys = _moe_ffn(hs, *expert_w, tile_expert, n_active)
                y0, y1 = _sc_gather_tokens(ys, dest)
                xt = _combine_ple(y0, y1, route, x_mid, o_norm_ffn_post[j], p_all, *ple,
                                  row0=s * t_grp, prev=xt, p_row0=li * T)
    return xt.reshape(batch, seq, d_model)
```

```python
import functools
import math

import numpy as np
import jax
import jax.numpy as jnp
from jax import lax
from jax.experimental import pallas as pl
from jax.experimental.pallas import tpu as pltpu
from jax.experimental.pallas import tpu_sc as plsc

F32 = jnp.float32
BF16 = jnp.bfloat16
EPS = 1e-6

LANES = 128
SUBLANES = 8
VMEM_LIMIT_BYTES = 56 * 1024 * 1024

GRID_W = 64
A_HEADS, A_DK, A_CHUNK = 4, 128, 64
A_UNROLL = 16
B_HEADS, B_HEADDIM, B_GROUPS, B_STATE, B_CONV, B_CHUNK = 8, 64, 2, 128, 5, 128
B_UNROLL = 2
B_HPG = B_HEADS // B_GROUPS
B_GW = B_HPG * B_HEADDIM
C_HEADS, C_HEADDIM = 16, 64
NA_ROWS, NA_COLS = 8, 16
NA_HG = 4
NA_UNROLL = 16
N_EXPERTS, TOP_K = 8, 2
MOE_GROUPS = 2
MOE_TM = 512


def _cparams(sem):
    return pltpu.CompilerParams(dimension_semantics=sem, vmem_limit_bytes=VMEM_LIMIT_BYTES)


def _rms(x, g):
    return x * lax.rsqrt(jnp.mean(x * x, axis=-1, keepdims=True) + EPS) * g


def _silu(x):
    return x * jax.nn.sigmoid(x)


def _dot(a, b):
    return jnp.dot(a, b, preferred_element_type=F32)


def _dot_nt(a, b):
    return lax.dot_general(a, b, (((1,), (1,)), ((), ())), preferred_element_type=F32)


def _dot_tn(a, b):
    return lax.dot_general(a, b, (((0,), (0,)), ((), ())), preferred_element_type=F32)


def _staggered(stage_fns):
    live = list(stage_fns)
    step = 0
    done = [False] * len(live)
    while not all(done):
        for k, gen in enumerate(live):
            if step >= k and not done[k]:
                try:
                    next(gen)
                except StopIteration:
                    done[k] = True
        step += 1


def _norm_proj_kernel(x_ref, g_ref, lb_ref, *refs, acts, used_cols, transposed, col_chunk, sub):
    n_w = len(used_cols)
    w_src = refs[:n_w]
    o_refs = refs[n_w:n_w + len(acts)]
    w_refs = refs[n_w + len(acts):]

    @pl.when(pl.program_id(0) == 0)
    def _():
        for src, dst, used, flipped in zip(w_src, w_refs, used_cols, transposed):
            for c0 in range(0, used, col_chunk):
                c1 = min(c0 + col_chunk, used)
                if len(src.shape) == 2:
                    cols = src[:, c0:c1]
                elif flipped:
                    cols = src[0, c0:c1, :].T
                else:
                    cols = src[0, :, c0:c1]
                dst[:, c0:c1] = cols.astype(BF16)

    def stages(r0):
        rows = pl.ds(r0, sub)
        h = _rms(x_ref[rows, :], g_ref[...]).astype(BF16)
        yield
        piece, off = 0, 0
        for o_ref, act in zip(o_refs, acts):
            n = o_ref.shape[1]
            if off == used_cols[piece]:
                piece, off = piece + 1, 0
            w_ref = w_refs[piece]
            for c0 in range(0, n, col_chunk):
                c1 = min(c0 + col_chunk, n)
                y = _dot(h, w_ref[:, off + c0:off + c1])
                if act == "silu":
                    y = _silu(y)
                elif act == "log_gate":
                    lb = lb_ref[:, c0:c1]
                    y = jnp.log(lb + (1.0 - lb) * jax.nn.sigmoid(y))
                o_ref[rows, c0:c1] = y.astype(o_ref.dtype)
                yield
            off += n

    _staggered([stages(r0) for r0 in range(0, x_ref.shape[0], sub)])


def _norm_proj(x, g, w, outs, lb=None, layer=0, tm=512, sub=256, col_chunk=1024):
    T, D = x.shape
    pieces = list(w) if isinstance(w, (list, tuple)) else [(w, w.shape[-1])]
    ws = [pc[0] for pc in pieces]
    used_cols = tuple(pc[1] for pc in pieces)
    transposed = tuple(len(pc) > 2 and pc[2] for pc in pieces)
    widths = [o[0] for o in outs]
    assert sum(widths) == sum(used_cols) and T % tm == 0
    if lb is None:
        lb = jnp.zeros((LANES,), F32)

    def resident(v):
        if v.ndim == 3:
            return pl.BlockSpec((1,) + v.shape[1:], lambda i: (layer, 0, 0), pipeline_mode=pl.Buffered(1))
        return pl.BlockSpec(v.shape, lambda i: (0, 0), pipeline_mode=pl.Buffered(1))

    return pl.pallas_call(
        functools.partial(_norm_proj_kernel, acts=tuple(o[2] for o in outs), used_cols=used_cols,
                          transposed=transposed, col_chunk=col_chunk, sub=sub),
        grid=(T // tm,),
        in_specs=[_row_spec(tm, D, 1), _whole_spec((1, D), 1), _whole_spec((1, lb.shape[0]), 1)]
                 + [resident(v) for v in ws],
        out_specs=[_row_spec(tm, n, 1) for n in widths],
        out_shape=[jax.ShapeDtypeStruct((T, n), dt) for n, dt, _ in outs],
        scratch_shapes=[pltpu.VMEM((D, n), BF16) for n in used_cols],
        compiler_params=_cparams(("arbitrary",)),
        name="norm_proj",
    )(x, g.reshape(1, D), lb.reshape(1, -1), *ws)


def _ple_update(x, p, gin_ref, wg_ref, wp_ref, gpost_ref):
    h = _rms(x, gin_ref[...]).astype(BF16)
    gate = jax.nn.sigmoid(_dot(h, wg_ref[...]))
    proj = _dot(p.astype(BF16), wp_ref[...])
    return x + _rms(gate * proj, gpost_ref[...])


def _row_spec(tm, width, n_grid):
    return pl.BlockSpec((tm, width), (lambda i: (i, 0)) if n_grid == 1 else (lambda i, j: (i, 0)))


def _whole_spec(shape, n_grid):
    zeros = (0,) * len(shape)
    return pl.BlockSpec(shape, (lambda i: zeros) if n_grid == 1 else (lambda i, j: zeros))


def _mix_ffn_ple_kernel(oa_ref, ob_ref, wa_ref, wb_ref, gmix_ref, x_ref, gpre_ref, wg_ref, wu_ref, wd_ref,
                        gpost_ref, p_ref, gin_ref, wpg_ref, wpp_ref, gple_ref, o_ref, *, sub, tf):
    tm = x_ref.shape[0]
    F = wg_ref.shape[1]
    def stages(r0):
        rows = pl.ds(r0, sub)
        mix = _dot(oa_ref[rows, :], wa_ref[...]) + _dot(ob_ref[rows, :], wb_ref[...])
        x1 = x_ref[rows, :] + _rms(mix, gmix_ref[...])
        h = _rms(x1, gpre_ref[...]).astype(BF16)
        yield
        acc = None
        for f0 in range(0, F, tf):
            f1 = min(f0 + tf, F)
            a = (_silu(_dot(h, wg_ref[:, f0:f1])) * _dot(h, wu_ref[:, f0:f1])).astype(BF16)
            part = _dot(a, wd_ref[f0:f1, :])
            acc = part if acc is None else acc + part
            yield
        x2 = x1 + _rms(acc, gpost_ref[...])
        o_ref[rows, :] = _ple_update(x2, p_ref[rows, :], gin_ref, wpg_ref, wpp_ref, gple_ref)
        yield

    _staggered([stages(r0) for r0 in range(0, tm, sub)])


def _mix_ffn_ple(o_a, o_b, w_a, w_b, g_mix, x, g_pre, wg, wu, wd, g_post, p, g_in, wpg, wpp, g_ple,
                 p_row0=0, tm=512, sub=256, tf=1536):
    T, D = x.shape
    F = wg.shape[1]
    assert T % tm == 0 and tm % sub == 0 and p_row0 % tm == 0
    p_blk0 = p_row0 // tm
    vec = lambda g: g.reshape(1, D)
    resident = lambda w: pl.BlockSpec(w.shape, lambda i: (0, 0), pipeline_mode=pl.Buffered(1))
    return pl.pallas_call(
        functools.partial(_mix_ffn_ple_kernel, sub=sub, tf=tf),
        grid=(T // tm,),
        in_specs=[_row_spec(tm, o_a.shape[1], 1), _row_spec(tm, o_b.shape[1], 1),
                  resident(w_a), resident(w_b), _whole_spec((1, D), 1),
                  _row_spec(tm, D, 1), _whole_spec((1, D), 1),
                  resident(wg), resident(wu), resident(wd),
                  _whole_spec((1, D), 1),
                  pl.BlockSpec((tm, p.shape[1]), lambda i: (i + p_blk0, 0)), _whole_spec((1, D), 1),
                  resident(wpg), resident(wpp), _whole_spec((1, D), 1)],
        out_specs=_row_spec(tm, D, 1),
        out_shape=jax.ShapeDtypeStruct((T, D), F32),
        compiler_params=_cparams(("parallel",)),
        name="mix_ffn_ple",
    )(o_a, o_b, w_a, w_b, vec(g_mix), x, vec(g_pre), wg, wu, wd, vec(g_post), p, vec(g_in), wpg, wpp,
      vec(g_ple))


def _roll_rows(x, s, rev):
    n = x.shape[0]
    return pltpu.roll(x, (n - s) if rev else s, 0)


def _cumsum_rows(x, tau, rev):
    n = x.shape[0]
    s = 1
    while s < n:
        if s % SUBLANES:
            shifted = jnp.where(tau >= s, _roll_rows(x, s, rev), 0.0)
        else:
            zeros = jnp.zeros((s,) + x.shape[1:], x.dtype)
            shifted = (jnp.concatenate([x[s:], zeros], axis=0) if rev
                       else jnp.concatenate([zeros, x[:n - s]], axis=0))
        x = x + shifted
        s *= 2
    return x


def _hgrn_levels(C):
    return [C >> (i + 1) for i in range(C.bit_length() - 1)]


def _hgrn_pair_classes(C):
    t = np.arange(C)[:, None]
    s = np.arange(C)[None, :]
    out = np.full((2 * C, 2 * C), -1, np.int32)
    for d, rev in enumerate((False, True)):
        tau, sig = (C - 1 - t, C - 1 - s) if rev else (t, s)
        blk = np.full((C, C), -1, np.int32)
        blk[t == s] = 0
        for i, L in enumerate(_hgrn_levels(C)):
            m = ((t & -(2 * L)) == (s & -(2 * L))) & ((tau & (2 * L - 1)) >= L) & ((sig & (2 * L - 1)) < L)
            blk[m] = i + 1
        out[d * C:(d + 1) * C, d * C:(d + 1) * C] = blk
    return out


def _hgrn_kernel(q_ref, gf_ref, gb_ref, i_ref, gate_ref, cls_ref, ng_ref, o_ref, acc_scr, st_scr):
    S, DK = q_ref.shape
    C = A_CHUNK
    nc = S // C
    assert nc % (2 * A_UNROLL) == 0
    ng = ng_ref[...]
    row = lax.broadcasted_iota(jnp.int32, (C, DK), 0)
    levels = _hgrn_levels(C)
    zero_half = jnp.zeros((C, DK), BF16)

    def stack(top, bottom):
        return jnp.concatenate([top, bottom], axis=0)

    def block_diag(x):
        return stack(jnp.concatenate([x[:C], zero_half], axis=1), jnp.concatenate([zero_half, x[C:]], axis=1))

    def make_decay(rev):
        tau = (C - 1 - row) if rev else row
        odd_rank = (tau & 1) == 1
        last = 0 if rev else C - 1

        def ref_rows(b, L):
            off = L if rev else L - 1
            if 2 * L >= SUBLANES:
                pieces = [jnp.broadcast_to(b[j * 2 * L + off:j * 2 * L + off + 1, :], (2 * L, DK))
                          for j in range(C // (2 * L))]
                return pieces[0] if len(pieces) == 1 else jnp.concatenate(pieces, axis=0)
            if L == 1:
                return jnp.where(odd_rank, _roll_rows(b, 1, rev), b)
            b3 = b.reshape(C // SUBLANES, SUBLANES, DK)
            sub = lax.broadcasted_iota(jnp.int32, b3.shape, 1)
            out = None
            for j in range(SUBLANES // (2 * L)):
                piece = jnp.broadcast_to(b3[:, j * 2 * L + off:j * 2 * L + off + 1, :], b3.shape)
                out = piece if out is None else jnp.where(sub >= j * 2 * L, piece, out)
            return out.reshape(C, DK)

        def decay(g):
            b = _cumsum_rows(g, tau, rev)
            return b, [ref_rows(b, L) for L in levels], b[last:last + 1, :]

        return decay

    decay_fwd, decay_bwd = make_decay(False), make_decay(True)
    pair_cls = cls_ref[...]
    st_scr[...] = jnp.zeros_like(st_scr)

    def chunk_pair(rows_f, rows_b, st):
        q = stack(q_ref[rows_f, :], q_ref[rows_b, :]).astype(F32)
        g_f, g_b = gf_ref[rows_f, :], gb_ref[rows_b, :]
        k = 1.0 - jnp.exp(stack(g_f, g_b))
        vb = stack(i_ref[rows_f, :], i_ref[rows_b, :])
        b_f, refs_f, last_f = decay_fwd(g_f)
        b_b, refs_b, last_b = decay_bwd(g_b)
        b = stack(b_f, b_b)

        o = _dot_nt(block_diag((q * jnp.exp(b)).astype(BF16)), st.astype(BF16))

        qb, kb = q.astype(BF16), k.astype(BF16)
        attn = jnp.where(pair_cls == 0, _dot_nt(qb, kb), 0.0)
        for i in range(len(levels)):
            e = jnp.exp(-jnp.abs(b - stack(refs_f[i], refs_b[i]))).astype(BF16)
            attn = jnp.where(pair_cls == i + 1, _dot_nt(qb * e, kb * e), attn)
        o = o + _dot(attn.astype(BF16), vb)

        b_last = stack(jnp.broadcast_to(last_f, (C, DK)), jnp.broadcast_to(last_b, (C, DK)))
        khat = block_diag((k * jnp.exp(b_last - b)).astype(BF16))
        keep = jnp.exp(jnp.concatenate([last_f, last_b], axis=1))
        return o, st * keep + _dot_tn(vb, khat)

    def make_body(final):
        def body(it, carry):
            st = st_scr[...]
            for u in range(A_UNROLL):
                ci = it * A_UNROLL + u
                rows_f = pl.ds(pl.multiple_of(ci * C, C), C)
                rows_b = pl.ds(pl.multiple_of((nc - 1 - ci) * C, C), C)
                o, st = chunk_pair(rows_f, rows_b, st)
                for rows, part in ((rows_f, o[:C]), (rows_b, o[C:])):
                    if final:
                        tot = acc_scr[rows, :] + part
                        o_ref[rows, :] = (_rms(tot, ng) * gate_ref[rows, :].astype(F32)).astype(o_ref.dtype)
                    else:
                        acc_scr[rows, :] = part
            st_scr[...] = st
            return carry

        return body

    trips = nc // A_UNROLL
    lax.fori_loop(0, trips // 2, make_body(False), 0)
    lax.fori_loop(trips // 2, trips, make_body(True), 0)


def _hgrn_mixer(q, g_fwd, g_bwd, v, gate, norm_g, batch, seq):
    T = q.shape[0]
    H, DK = A_HEADS, A_DK
    head = pl.BlockSpec((seq, DK), lambda b, h: (b, h))
    classes = jnp.asarray(_hgrn_pair_classes(A_CHUNK))
    return pl.pallas_call(
        _hgrn_kernel,
        grid=(batch, H),
        in_specs=[head, head, head, head, head,
                  pl.BlockSpec(classes.shape, lambda b, h: (0, 0)),
                  pl.BlockSpec((1, DK), lambda b, h: (0, 0))],
        out_specs=pl.BlockSpec((seq, DK), lambda b, h: (b, h)),
        out_shape=jax.ShapeDtypeStruct((T, H * DK), BF16),
        scratch_shapes=[pltpu.VMEM((seq, DK), F32), pltpu.VMEM((DK, 2 * DK), F32)],
        compiler_params=_cparams(("parallel", "parallel")),
        name="hgrn2",
    )(q, g_fwd, g_bwd, v, gate, classes, norm_g.reshape(1, DK))


def _expand_heads(col, j0, width):
    q = col.shape[0]
    lane = lax.broadcasted_iota(jnp.int32, (q, width), 1)
    out = jnp.broadcast_to(col[:, j0 + B_HPG - 1:j0 + B_HPG], (q, width))
    for j in range(B_HPG - 2, -1, -1):
        out = jnp.where(lane < (j + 1) * B_HEADDIM,
                        jnp.broadcast_to(col[:, j0 + j:j0 + j + 1], (q, width)), out)
    return out


def _ssd_kernel(z_ref, x_ref, bm_ref, cm_ref, dt_ref, cwx_ref, cwb_ref, cwc_ref,
                cbx_ref, cbb_ref, cbc_ref, hp_ref, dsk_ref, ng_ref, o_ref,
                xs_scr, b_scr, c_scr, y_scr, *st_scrs):
    S = x_ref.shape[0]
    Q = B_CHUNK
    nc = S // Q
    GW = B_GW
    hp = hp_ref[0]
    a_row, dtb_row = hp[0:1, :], hp[1:2, :]
    row = lax.broadcasted_iota(jnp.int32, (Q, LANES), 0)
    t2 = lax.broadcasted_iota(jnp.int32, (Q, Q), 0)
    s2 = lax.broadcasted_iota(jnp.int32, (Q, Q), 1)
    lane_gw = lax.broadcasted_iota(jnp.int32, (Q, GW), 1)
    head_mask = [(lane_gw >= j * B_HEADDIM) & (lane_gw < (j + 1) * B_HEADDIM) for j in range(B_HPG)]
    halo = SUBLANES

    def conv_body(c, carry):
        r0 = pl.multiple_of(c * Q, Q)
        rp = pl.multiple_of(jnp.maximum(r0 - halo, 0), halo)
        rn = pl.multiple_of(jnp.minimum(r0 + Q, S - halo), halo)
        has_prev = c > 0
        has_next = c < nc - 1

        def conv(src_ref, w_ref, bias_ref):
            prev = jnp.where(has_prev, src_ref[pl.ds(rp, halo), :].astype(F32), 0.0)
            nxt = jnp.where(has_next, src_ref[pl.ds(rn, halo), :].astype(F32), 0.0)
            xx = jnp.concatenate([prev, src_ref[pl.ds(r0, Q), :].astype(F32), nxt], axis=0)
            n = Q + 2 * halo
            w = w_ref[0]
            acc = None
            for j in range(B_CONV):
                delta = j - B_CONV // 2
                sh = xx if delta == 0 else pltpu.roll(xx, (-delta) % n, 0)
                term = sh[halo:halo + Q, :] * w[j:j + 1, :]
                acc = term if acc is None else acc + term
            return _silu(acc + bias_ref[0])

        xs_scr[pl.ds(r0, Q), :] = conv(x_ref, cwx_ref, cbx_ref)
        b_scr[pl.ds(r0, Q), :] = conv(bm_ref, cwb_ref, cbb_ref).astype(BF16)
        c_scr[pl.ds(r0, Q), :] = conv(cm_ref, cwc_ref, cbc_ref).astype(BF16)
        return carry

    lax.fori_loop(0, nc, conv_body, 0)

    def make_chunk(rev):
        tau = (Q - 1 - row) if rev else row
        last = 0 if rev else Q - 1
        j0 = B_HPG if rev else 0
        pair_ok = (s2 >= t2) if rev else (t2 >= s2)

        def chunk(rows, st):
            dt = jax.nn.softplus(dt_ref[rows, :] + dtb_row)
            cs = _cumsum_rows(dt * a_row, tau, rev)
            cs_t = cs.T
            xs = xs_scr[rows, :]
            bm = b_scr[rows, :]
            cm = c_scr[rows, :]
            xdt = xs * _expand_heads(dt, j0, GW)
            ecs = _expand_heads(cs, j0, GW)
            ecs_last = ecs[last:last + 1, :]

            y = _dot(cm, st.astype(BF16)) * jnp.exp(ecs)

            scores = _dot_nt(cm, bm)
            for j in range(B_HPG):
                col = jnp.broadcast_to(cs[:, j0 + j:j0 + j + 1], (Q, Q))
                rw = jnp.broadcast_to(cs_t[j0 + j:j0 + j + 1, :], (Q, Q))
                decay = jnp.where(pair_ok, jnp.exp(jnp.minimum(col - rw, 0.0)), 0.0)
                xh = jnp.where(head_mask[j], xdt, 0.0).astype(BF16)
                y = y + _dot((scores * decay).astype(BF16), xh)

            xdec = (xdt * jnp.exp(ecs_last - ecs)).astype(BF16)
            return y, st * jnp.exp(ecs_last) + _dot_tn(bm, xdec)

        return chunk

    chunk_fns = (make_chunk(False), make_chunk(True))
    for st_scr in st_scrs:
        st_scr[...] = jnp.zeros_like(st_scr)

    def make_body(final, unroll):
        def body(it, carry):
            for rev, chunk, st_scr in zip((False, True), chunk_fns, st_scrs):
                st = st_scr[...]
                for u in range(unroll):
                    ci = it * unroll + u
                    c = (nc - 1 - ci) if rev else ci
                    rows = pl.ds(pl.multiple_of(c * Q, Q), Q)
                    y, st = chunk(rows, st)
                    if final:
                        tot = y_scr[rows, :] + y + dsk_ref[0] * xs_scr[rows, :]
                        tot = tot * _silu(z_ref[rows, :].astype(F32))
                        o_ref[rows, :] = _rms(tot, ng_ref[0]).astype(o_ref.dtype)
                    else:
                        y_scr[rows, :] = y
                st_scr[...] = st
            return carry

        return body

    assert nc % (2 * B_UNROLL) == 0
    lax.fori_loop(0, nc // (2 * B_UNROLL), make_body(False, B_UNROLL), 0)
    lax.fori_loop(nc // 2, nc, make_body(True, 1), 0)


def _ssd_mixer(u, dt, conv_w, conv_b, a_log, dt_bias, d_skip, norm_g, batch, seq, col0):
    T = u.shape[0]
    G, GW, N = B_GROUPS, B_GW, B_STATE
    W = B_HEADS * B_HEADDIM
    assert col0 % GW == 0 and GW == 2 * N
    z_blk = col0 // GW
    x_blk = z_blk + W // GW
    b_blk = (col0 + 2 * W) // N
    c_blk = b_blk + G

    def pad_rows(w):
        return jnp.pad(w, ((0, SUBLANES - w.shape[0]), (0, 0)))

    cwx = pad_rows(conv_w[:, :W]).reshape(SUBLANES, G, GW).transpose(1, 0, 2)
    cwb = pad_rows(conv_w[:, W:W + G * N]).reshape(SUBLANES, G, N).transpose(1, 0, 2)
    cwc = pad_rows(conv_w[:, W + G * N:]).reshape(SUBLANES, G, N).transpose(1, 0, 2)
    cbx = conv_b[:W].reshape(G, 1, GW)
    cbb = conv_b[W:W + G * N].reshape(G, 1, N)
    cbc = conv_b[W + G * N:].reshape(G, 1, N)
    a_neg = -jnp.exp(a_log.astype(F32))
    per_dir = lambda v: jnp.concatenate([v[0].reshape(G, B_HPG), v[1].reshape(G, B_HPG)], axis=1)
    hp = jnp.stack([per_dir(a_neg), per_dir(dt_bias.astype(F32))], axis=1)
    hp = jnp.pad(hp, ((0, 0), (0, SUBLANES - 2), (0, LANES - 2 * B_HPG)))
    dsk = jnp.repeat(d_skip.astype(F32), B_HEADDIM).reshape(G, 1, GW)
    ng = norm_g.reshape(G, 1, GW)

    gspec = lambda shape: pl.BlockSpec((1,) + shape, lambda b, g: (g, 0, 0))
    return pl.pallas_call(
        _ssd_kernel,
        grid=(batch, G),
        in_specs=[pl.BlockSpec((seq, GW), lambda b, g: (b, z_blk + g)),
                  pl.BlockSpec((seq, GW), lambda b, g: (b, x_blk + g)),
                  pl.BlockSpec((seq, N), lambda b, g: (b, b_blk + g)),
                  pl.BlockSpec((seq, N), lambda b, g: (b, c_blk + g)),
                  pl.BlockSpec((seq, LANES), lambda b, g: (b, g)),
                  gspec((SUBLANES, GW)), gspec((SUBLANES, N)), gspec((SUBLANES, N)),
                  gspec((1, GW)), gspec((1, N)), gspec((1, N)),
                  gspec((SUBLANES, LANES)), gspec((1, GW)), gspec((1, GW))],
        out_specs=pl.BlockSpec((seq, GW), lambda b, g: (b, g)),
        out_shape=jax.ShapeDtypeStruct((T, W), BF16),
        scratch_shapes=[pltpu.VMEM((seq, GW), F32), pltpu.VMEM((seq, N), BF16),
                        pltpu.VMEM((seq, N), BF16), pltpu.VMEM((seq, GW), F32),
                        pltpu.VMEM((N, GW), F32), pltpu.VMEM((N, GW), F32)],
        compiler_params=_cparams(("parallel", "parallel")),
        name="ssd",
    )(u, u, u, u, dt, cwx, cwb, cwc, cbx, cbb, cbc, hp, dsk, ng)


def _na_kernel(q_ref, k_ref, v_ref, bias_ref, o_ref):
    S, HW = q_ref.shape
    W = GRID_W
    n_rows = S // W
    kh = min(NA_ROWS, n_rows)
    nk = kh * W
    n_pairs = bias_ref.shape[1] // NA_HG
    lane = lax.broadcasted_iota(jnp.int32, (W, HW), 1)
    head_masks = [(lane >= h * C_HEADDIM) & (lane < (h + 1) * C_HEADDIM) for h in range(NA_HG)]
    scale = C_HEADDIM ** -0.5

    def body(r, carry):
        rs = jnp.clip(r - kh // 2, 0, n_rows - kh)
        q = q_ref[pl.ds(pl.multiple_of(r * W, W), W), :] * jnp.asarray(scale, q_ref.dtype)
        kw = k_ref[pl.ds(pl.multiple_of(rs * W, W), nk), :]
        vw = v_ref[pl.ds(pl.multiple_of(rs * W, W), nk), :]
        zero = jnp.zeros_like(q)
        qs = jnp.concatenate([jnp.where(m, q, zero) for m in head_masks], axis=0)
        d0 = (NA_ROWS - 1) - (r - rs)
        bias = jnp.concatenate(
            [jnp.concatenate([bias_ref[0, h * n_pairs + d0 + 2 * m] for m in range(kh // 2)], axis=1)
             for h in range(NA_HG)], axis=0)
        s = _dot_nt(qs, kw) + bias
        m = jnp.max(s, axis=-1, keepdims=True)
        p = jnp.exp(s - m)
        l = jnp.sum(p, axis=-1, keepdims=True)
        res = _dot(p.astype(BF16), vw) / l
        out = jnp.zeros((W, HW), F32)
        for h in range(NA_HG):
            out = jnp.where(head_masks[h], res[h * W:(h + 1) * W, :], out)
        o_ref[pl.ds(pl.multiple_of(r * W, W), W), :] = out.astype(o_ref.dtype)
        return carry

    lax.fori_loop(0, n_rows, body, 0, unroll=NA_UNROLL)


def _na_bias_table(rpb):
    H, n_dr, n_dc = rpb.shape
    W = GRID_W
    c = np.arange(W)
    qs = np.clip(c - NA_COLS // 2, 0, W - NA_COLS)
    valid = (c[None, :] >= qs[:, None]) & (c[None, :] < qs[:, None] + NA_COLS)
    dc = np.clip(c[None, :] - c[:, None] + (NA_COLS - 1), 0, n_dc - 1)
    onehot = np.zeros((2, n_dc, W, 2 * W), np.float32)
    for half in range(2):
        onehot[half, :, :, half * W:(half + 1) * W] = dc[None] == np.arange(n_dc)[:, None, None]
    both = jnp.concatenate([rpb[:, :-1], rpb[:, 1:]], axis=2).astype(F32)
    pairs = jnp.einsum("hre,ecx->hrcx", both, jnp.asarray(onehot.reshape(2 * n_dc, W, 2 * W)),
                       precision=lax.Precision.HIGHEST)
    valid2 = np.concatenate([valid, valid], axis=1)
    pairs = jnp.where(jnp.asarray(valid2)[None, None], pairs, -jnp.inf)
    return pairs.reshape(H // NA_HG, NA_HG * (n_dr - 1), W, 2 * W)


def _neighborhood_attention(qkv, rpb, batch, seq):
    T = qkv.shape[0]
    HW = NA_HG * C_HEADDIM
    n_hg = C_HEADS // NA_HG
    n_rows = seq // GRID_W
    assert n_rows >= NA_ROWS and NA_ROWS % 2 == 0
    table = _na_bias_table(rpb)
    return pl.pallas_call(
        _na_kernel,
        grid=(batch, n_hg),
        in_specs=[pl.BlockSpec((seq, HW), lambda b, g: (b, g)),
                  pl.BlockSpec((seq, HW), lambda b, g: (b, n_hg + g)),
                  pl.BlockSpec((seq, HW), lambda b, g: (b, 2 * n_hg + g)),
                  pl.BlockSpec((1,) + table.shape[1:], lambda b, g: (g, 0, 0, 0))],
        out_specs=pl.BlockSpec((seq, HW), lambda b, g: (b, g)),
        out_shape=jax.ShapeDtypeStruct((T, C_HEADS * C_HEADDIM), BF16),
        compiler_params=_cparams(("parallel", "parallel")),
        name="natten",
    )(qkv, qkv, qkv, table)


def _store_token_tiles(ref, val, t0=0):
    count, width = val.shape
    n = width // LANES
    for j in range(n):
        ref[pl.ds(t0 * n + j, count, stride=n), :] = val[:, j * LANES:(j + 1) * LANES]


def _load_token_tiles(ref, n, t0=0, count=None):
    if count is None:
        count = ref.shape[0] // n - t0
    return jnp.concatenate([ref[pl.ds(t0 * n + j, count, stride=n), :] for j in range(n)], axis=1)


def _proj_router_kernel(a_ref, w_ref, gmix_ref, x_ref, g_ref, wr_ref, x1_ref, o_ref, h_ref):
    tm = x_ref.shape[0]
    sub = tm

    def stages(r0):
        rows = pl.ds(r0, sub)
        x1 = x_ref[rows, :] + _rms(_dot(a_ref[rows, :], w_ref[...]), gmix_ref[...])
        x1_ref[rows, :] = x1
        yield
        h = _rms(x1, g_ref[...])
        _store_token_tiles(h_ref, h, r0)
        h_hi = h.astype(BF16)
        h_lo = (h - h_hi.astype(F32)).astype(BF16)
        both = _dot(h_hi, wr_ref[...])
        logits = both[:, :LANES] + both[:, LANES:] + _dot(h_lo, wr_ref[:, :LANES])
        yield
        lane = lax.broadcasted_iota(jnp.int32, logits.shape, 1)
        neg = -jnp.inf
        l1 = jnp.where(lane < N_EXPERTS, logits, neg)
        m1 = jnp.max(l1, axis=-1, keepdims=True)
        i1 = jnp.min(jnp.where(l1 == m1, lane, LANES), axis=-1, keepdims=True)
        l2 = jnp.where(lane == i1, neg, l1)
        m2 = jnp.max(l2, axis=-1, keepdims=True)
        i2 = jnp.min(jnp.where(l2 == m2, lane, LANES), axis=-1, keepdims=True)
        e = jnp.exp(m2 - m1)
        w1 = 1.0 / (1.0 + e)
        w2 = e / (1.0 + e)
        o_ref[rows, :] = jnp.where(lane == 0, i1.astype(F32),
                                   jnp.where(lane == 1, i2.astype(F32),
                                             jnp.where(lane == 2, w1, jnp.where(lane == 3, w2, 0.0))))
        yield

    _staggered([stages(r0) for r0 in range(0, tm, sub)])


def _proj_router(a, w, g_mix, x, g, w_router, tm=512):
    T, D = x.shape
    n = D // LANES
    wr = jnp.pad(w_router.astype(F32), ((0, 0), (0, LANES - w_router.shape[1])))
    wr_hi = wr.astype(BF16)
    wr = jnp.concatenate([wr_hi, (wr - wr_hi.astype(F32)).astype(BF16)], axis=1)
    x1, route, h = pl.pallas_call(
        _proj_router_kernel,
        grid=(T // tm,),
        in_specs=[_row_spec(tm, a.shape[1], 1), _whole_spec(w.shape, 1), _whole_spec((1, D), 1),
                  _row_spec(tm, D, 1), _whole_spec((1, D), 1), _whole_spec((D, 2 * LANES), 1)],
        out_specs=[_row_spec(tm, D, 1), _row_spec(tm, LANES, 1), _row_spec(tm * n, LANES, 1)],
        out_shape=[jax.ShapeDtypeStruct((T, D), F32),
                   jax.ShapeDtypeStruct((T, LANES), F32),
                   jax.ShapeDtypeStruct((T * n, LANES), F32)],
        compiler_params=_cparams(("parallel",)),
        name="proj_router",
    )(a, w, g_mix.reshape(1, D), x, g.reshape(1, D), wr)
    return x1, route, h.reshape(T, n, LANES)


SC_CORES = 2
SC_SUBCORES = 16
SC_WORKERS = SC_CORES * SC_SUBCORES
SC_CHUNK = 32


def _sc_mesh():
    return plsc.VectorSubcoreMesh(core_axis_name="c", subcore_axis_name="s",
                                  num_cores=SC_CORES, num_subcores=SC_SUBCORES)


def _sc_index_blocks(idx):
    return idx.reshape(SC_WORKERS, -1, SC_CHUNK)


def _sc_scatter_tokens(h, dest, n_rows, row0=0):
    T = dest.shape[0]
    _, n, _ = h.shape
    per_worker = T // SC_WORKERS
    n_chunks = per_worker // SC_CHUNK
    assert per_worker * SC_WORKERS == T and n_chunks * SC_CHUNK == per_worker

    @functools.partial(
        pl.kernel, mesh=_sc_mesh(),
        out_type=jax.ShapeDtypeStruct((n_rows, n, LANES), h.dtype),
        scratch_types=[pltpu.VMEM((n_chunks, SC_CHUNK), jnp.int32),
                       pltpu.VMEM((n_chunks, SC_CHUNK), jnp.int32),
                       pltpu.VMEM((SC_CHUNK, n, LANES), h.dtype)],
        name="sc_scatter_tokens",
    )
    def scatter(h_hbm, d0_hbm, d1_hbm, o_hbm, i0_v, i1_v, rows_v):
        wid = lax.axis_index("s") * SC_CORES + lax.axis_index("c")
        pltpu.sync_copy(d0_hbm.at[wid], i0_v)
        pltpu.sync_copy(d1_hbm.at[wid], i1_v)

        @pl.loop(0, n_chunks)
        def _(j):
            pltpu.sync_copy(h_hbm.at[pl.ds(row0 + wid * per_worker + j * SC_CHUNK, SC_CHUNK)], rows_v)
            pltpu.sync_copy(rows_v, o_hbm.at[i0_v.at[j]])
            pltpu.sync_copy(rows_v, o_hbm.at[i1_v.at[j]])

    return scatter(h, _sc_index_blocks(dest[:, 0]), _sc_index_blocks(dest[:, 1]))


def _sc_gather_tokens(y, dest):
    T = dest.shape[0]
    _, n, _ = y.shape
    per_worker = T // SC_WORKERS
    n_chunks = per_worker // SC_CHUNK
    assert per_worker * SC_WORKERS == T and n_chunks * SC_CHUNK == per_worker
    out = jax.ShapeDtypeStruct((T, n, LANES), y.dtype)

    @functools.partial(
        pl.kernel, mesh=_sc_mesh(), out_type=(out, out),
        scratch_types=[pltpu.VMEM((n_chunks, SC_CHUNK), jnp.int32),
                       pltpu.VMEM((n_chunks, SC_CHUNK), jnp.int32),
                       pltpu.VMEM((SC_CHUNK, n, LANES), y.dtype)],
        name="sc_gather_tokens",
    )
    def gather(y_hbm, d0_hbm, d1_hbm, o0_hbm, o1_hbm, i0_v, i1_v, rows_v):
        wid = lax.axis_index("s") * SC_CORES + lax.axis_index("c")
        pltpu.sync_copy(d0_hbm.at[wid], i0_v)
        pltpu.sync_copy(d1_hbm.at[wid], i1_v)

        @pl.loop(0, n_chunks)
        def _(j):
            rows = pl.ds(wid * per_worker + j * SC_CHUNK, SC_CHUNK)
            pltpu.sync_copy(y_hbm.at[i0_v.at[j]], rows_v)
            pltpu.sync_copy(rows_v, o0_hbm.at[rows])
            pltpu.sync_copy(y_hbm.at[i1_v.at[j]], rows_v)
            pltpu.sync_copy(rows_v, o1_hbm.at[rows])

    return gather(y, _sc_index_blocks(dest[:, 0]), _sc_index_blocks(dest[:, 1]))


SC_PACK_PAIRS = 8
SC_LANES = 16


def _sc_pack_bf16_rows(w):
    R, C = w.shape
    pairs = R // 2
    per_worker = pairs // SC_WORKERS
    n_chunks = per_worker // SC_PACK_PAIRS
    assert n_chunks * SC_PACK_PAIRS * SC_WORKERS * 2 == R and C % SC_LANES == 0

    def round_bits(v):
        u = plsc.bitcast(v, jnp.int32)
        return u + 0x7FFF + (lax.shift_right_logical(u, 16) & 1)

    @functools.partial(
        pl.kernel, mesh=_sc_mesh(),
        out_type=jax.ShapeDtypeStruct((pairs, C), jnp.int32),
        scratch_types=[pltpu.VMEM((2 * SC_PACK_PAIRS, C), F32), pltpu.VMEM((SC_PACK_PAIRS, C), jnp.int32)],
        compiler_params=pltpu.CompilerParams(use_tc_tiling_on_sc=True, needs_layout_passes=False),
        name="sc_pack_bf16_rows",
    )
    def pack(w_hbm, o_hbm, in_v, out_v):
        wid = lax.axis_index("s") * SC_CORES + lax.axis_index("c")

        @pl.loop(0, n_chunks)
        def _(j):
            p0 = (wid * n_chunks + j) * SC_PACK_PAIRS
            pltpu.sync_copy(w_hbm.at[pl.ds(2 * p0, 2 * SC_PACK_PAIRS)], in_v)

            @pl.loop(0, C, step=SC_LANES)
            def _(c):
                cols = pl.ds(c, SC_LANES)
                for i in range(SC_PACK_PAIRS):
                    lo = lax.shift_right_logical(round_bits(in_v[2 * i, cols]), 16)
                    hi = round_bits(in_v[2 * i + 1, cols]) & jnp.int32(-65536)
                    out_v[i, cols] = lo | hi

            pltpu.sync_copy(out_v, o_hbm.at[pl.ds(p0, SC_PACK_PAIRS)])

    return pack(w)


def _moe_ffn_kernel(te_ref, na_ref, h_ref, wg_ref, wu_ref, wd_ref, o_ref, acc_scr, *, tf):
    i = pl.program_id(0)
    D = acc_scr.shape[1]
    F = wg_ref.shape[2]
    unpack = lambda packed: pltpu.bitcast(packed, BF16)

    @pl.when(i < na_ref[0])
    def _():
        h = _load_token_tiles(h_ref, D // LANES).astype(BF16)
        for f0 in range(0, F, tf):
            a = (_silu(_dot(h, unpack(wg_ref[0, :, f0:f0 + tf])))
                 * _dot(h, unpack(wu_ref[0, :, f0:f0 + tf]))).astype(BF16)
            part = _dot(a, unpack(wd_ref[0, f0 // 2:(f0 + tf) // 2, :]))
            if f0 == 0:
                acc_scr[...] = part
            else:
                acc_scr[...] += part
        _store_token_tiles(o_ref, acc_scr[...])

    @pl.when(i >= na_ref[0])
    def _():
        o_ref[...] = jnp.zeros_like(o_ref)


def _moe_ffn(hs, wg, wu, wd, tile_expert, n_active, tm=MOE_TM, tf=512):
    rows, n, _ = hs.shape
    E, half_d, F = wg.shape
    D = 2 * half_d
    nt = tile_expert.shape[0]
    assert rows == nt * tm and n * LANES == D and wd.shape == (E, F // 2, D)
    grid_spec = pltpu.PrefetchScalarGridSpec(
        num_scalar_prefetch=2,
        grid=(nt,),
        in_specs=[pl.BlockSpec((tm * n, LANES), lambda i, te, na: (i, 0)),
                  pl.BlockSpec((1, D // 2, F), lambda i, te, na: (te[i], 0, 0)),
                  pl.BlockSpec((1, D // 2, F), lambda i, te, na: (te[i], 0, 0)),
                  pl.BlockSpec((1, F // 2, D), lambda i, te, na: (te[i], 0, 0))],
        out_specs=pl.BlockSpec((tm * n, LANES), lambda i, te, na: (i, 0)),
        scratch_shapes=[pltpu.VMEM((tm, D), F32)],
    )
    y = pl.pallas_call(
        functools.partial(_moe_ffn_kernel, tf=tf),
        grid_spec=grid_spec,
        out_shape=jax.ShapeDtypeStruct((rows * n, LANES), F32),
        compiler_params=_cparams(("arbitrary",)),
        name="moe_ffn",
    )(tile_expert, n_active, hs.reshape(rows * n, LANES), wg, wu, wd)
    return y.reshape(rows, n, LANES)


def _combine_ple_kernel(y0_ref, y1_ref, route_ref, x_ref, g_ref, p_ref, gin_ref, wpg_ref, wpp_ref,
                        gple_ref, *rest):
    o_ref = rest[-1]
    tm, D = x_ref.shape
    n = D // LANES
    sub = min(tm, 256)

    def stages(r0):
        rows = pl.ds(r0, sub)
        route = route_ref[rows, :]
        moe = (route[:, 2:3] * _load_token_tiles(y0_ref, n, r0, sub)
               + route[:, 3:4] * _load_token_tiles(y1_ref, n, r0, sub))
        x2 = x_ref[rows, :] + _rms(moe, g_ref[...])
        yield
        o_ref[rows, :] = _ple_update(x2, p_ref[rows, :], gin_ref, wpg_ref, wpp_ref, gple_ref)
        yield

    _staggered([stages(r0) for r0 in range(0, tm, sub)])


def _combine_ple(y0, y1, route, x, g, p, g_in, wpg, wpp, g_ple, row0=0, prev=None, p_row0=0, tm=512):
    T, D = x.shape
    Tp, n, _ = y0.shape
    assert row0 % tm == 0 and Tp % tm == 0 and n * LANES == D and p_row0 % tm == 0
    blk0 = row0 // tm
    p_blk0 = (row0 + p_row0) // tm
    vec = lambda v: v.reshape(1, D)
    shifted = lambda width: pl.BlockSpec((tm, width), lambda i: (i + blk0, 0))
    in_specs = [_row_spec(tm * n, LANES, 1), _row_spec(tm * n, LANES, 1), shifted(LANES),
                shifted(D), _whole_spec((1, D), 1),
                pl.BlockSpec((tm, p.shape[1]), lambda i: (i + p_blk0, 0)), _whole_spec((1, D), 1),
                _whole_spec(wpg.shape, 1), _whole_spec(wpp.shape, 1), _whole_spec((1, D), 1)]
    args = [y0.reshape(Tp * n, LANES), y1.reshape(Tp * n, LANES), route, x, vec(g), p, vec(g_in), wpg, wpp,
            vec(g_ple)]
    aliases = {}
    if prev is not None:
        in_specs.append(pl.BlockSpec(memory_space=pl.ANY))
        args.append(prev)
        aliases = {len(args) - 1: 0}
    return pl.pallas_call(
        _combine_ple_kernel,
        grid=(Tp // tm,),
        in_specs=in_specs,
        out_specs=shifted(D),
        out_shape=jax.ShapeDtypeStruct((T, D), F32),
        input_output_aliases=aliases,
        compiler_params=_cparams(("parallel",)),
        name="combine_ple",
    )(*args)


def _moe_plan(route, tm=MOE_TM):
    T = route.shape[0]
    e = route[:, :TOP_K].astype(jnp.int32).reshape(-1)
    onehot = (e[:, None] == jnp.arange(N_EXPERTS, dtype=jnp.int32)[None, :]).astype(jnp.int32)
    csum = jnp.cumsum(onehot, axis=0)
    rank = jnp.sum(csum * onehot, axis=1) - 1
    counts = csum[-1]
    padded = ((counts + tm - 1) // tm) * tm
    ends = jnp.cumsum(padded)
    starts = ends - padded
    dest = jnp.sum(starts[None, :] * onehot, axis=1) + rank
    nt = (T * TOP_K) // tm + N_EXPERTS
    tile_start = jnp.arange(nt, dtype=jnp.int32) * tm
    tile_expert = jnp.minimum(jnp.sum((tile_start[:, None] >= ends[None, :]).astype(jnp.int32), axis=1),
                              N_EXPERTS - 1).astype(jnp.int32)
    n_active = (ends[-1] // tm).astype(jnp.int32).reshape(1)
    return tile_expert, n_active, dest.reshape(T, TOP_K).astype(jnp.int32)


def kernel(x, p, hgrn_lb_raw, e_norm_mix_pre, e_w_in, e_conv_w, e_conv_b, e_A_log, e_dt_bias, e_D, e_a_norm, e_b_norm, e_w_out, e_norm_mix_post, e_norm_ffn_pre, e_w_ffn_gate, e_w_ffn_up, e_w_ffn_down, e_norm_ffn_post, o_norm_mix_pre, o_w_qkv, o_rpb, o_w_out, o_norm_mix_post, o_norm_ffn_pre, o_w_router, o_w_exp_gate, o_w_exp_up, o_w_exp_down, o_norm_ffn_post, ple_norm_in, ple_w_gate, ple_w_proj, ple_norm_post):
    batch, seq, d_model = x.shape
    depth = p.shape[0]
    T = batch * seq
    xt = x.reshape(T, d_model)
    p_all = p.reshape(depth * T, -1)
    lb_all =jnp.cumsum(jax.nn.softmax(hgrn_lb_raw.astype(F32), axis=0), axis=0)

    a_kdim = A_HEADS * A_DK
    b_width = B_HEADS * B_HEADDIM
    conv_dim = b_width + 2 * B_GROUPS * B_STATE
    main_w = 5 * a_kdim + b_width + conv_dim

    def pack_experts(w):
        E, R, C = w.shape
        return _sc_pack_bf16_rows(w.reshape(E * R, C)).reshape(E, R // 2, C)

    for li in range(depth):
        j = li // 2
        ple =(ple_norm_in[li], ple_w_gate[li].astype(BF16), ple_w_proj[li].astype(BF16), ple_norm_post[li])
        if li % 2 == 0:
            w_in = e_w_in[j]
            dtf = w_in[:, main_w:main_w + B_HEADS].reshape(d_model, B_GROUPS, B_HPG)
            dtb = w_in[:, main_w + B_HEADS:].reshape(d_model, B_GROUPS, B_HPG)
            w_dt = jnp.pad(jnp.concatenate([dtf, dtb], axis=2), ((0, 0), (0, 0), (0, LANES - 2 * B_HPG)))
            w_pieces = [(jnp.swapaxes(e_w_in, 1, 2), main_w, True),
                        (w_dt.reshape(d_model, B_GROUPS * LANES), B_GROUPS * LANES)]
            outs = [(a_kdim, BF16, "silu"), (a_kdim, F32, "log_gate"), (a_kdim, F32, "log_gate"),
                    (a_kdim, BF16, None), (a_kdim, BF16, "silu"),
                    (b_width + conv_dim, BF16, None), (B_GROUPS * LANES, F32, None)]
            q_a, gf_a, gb_a, i_a, gate_a, u_b, dt = _norm_proj(
                xt, e_norm_mix_pre[j], w_pieces, outs, lb=lb_all[li], layer=j)
            o_a = _hgrn_mixer(q_a, gf_a, gb_a, i_a, gate_a, e_a_norm[j], batch, seq)
            o_b = _ssd_mixer(u_b, dt, e_conv_w[j], e_conv_b[j], e_A_log[j], e_dt_bias[j], e_D[j],
                             e_b_norm[j], batch, seq, 0)
            w_out = e_w_out[j].astype(BF16)
            xt = _mix_ffn_ple(o_a, o_b, w_out[:a_kdim], w_out[a_kdim:], e_norm_mix_post[j], xt,
                              e_norm_ffn_pre[j], e_w_ffn_gate[j].astype(BF16), e_w_ffn_up[j].astype(BF16),
                              e_w_ffn_down[j].astype(BF16), e_norm_ffn_post[j], p_all, *ple, p_row0=li * T)
        else:
            (qkv,) = _norm_proj(xt, o_norm_mix_pre[j], o_w_qkv,
                                [(3 * C_HEADS * C_HEADDIM, BF16, None)], layer=j)
            o_c = _neighborhood_attention(qkv, o_rpb[j], batch, seq)
            xt, route, h = _proj_router(o_c, o_w_out[j].astype(BF16), o_norm_mix_post[j], xt,
                                        o_norm_ffn_pre[j], o_w_router[j])
            expert_w = [pack_experts(w[j]) for w in (o_w_exp_gate, o_w_exp_up, o_w_exp_down)]
            x_mid, xt = xt, None
            t_grp = T // MOE_GROUPS
            for s in range(MOE_GROUPS):
                tile_expert, n_active, dest = _moe_plan(route[s * t_grp:(s + 1) * t_grp])
                hs = _sc_scatter_tokens(h, dest, tile_expert.shape[0] * MOE_TM, row0=s * t_grp)
                ys = _moe_ffn(hs, *expert_w, tile_expert, n_active)
                y0, y1 = _sc_gather_tokens(ys, dest)
                xt = _combine_ple(y0, y1, route, x_mid, o_norm_ffn_post[j], p_all, *ple,
                                  row0=s * t_grp, prev=xt, p_row0=li * T)
    return xt.reshape(batch, seq, d_model)
```

```python
import functools
import math

import numpy as np
import jax
import jax.numpy as jnp
from jax import lax
from jax.experimental import pallas as pl
from jax.experimental.pallas import tpu as pltpu
from jax.experimental.pallas import tpu_sc as plsc

F32 = jnp.float32
BF16 = jnp.bfloat16
EPS = 1e-6

LANES = 128
SUBLANES = 8
VMEM_LIMIT_BYTES = 56 * 1024 * 1024

GRID_W = 64
A_HEADS, A_DK, A_CHUNK = 4, 128, 64
A_UNROLL = 16
B_HEADS, B_HEADDIM, B_GROUPS, B_STATE, B_CONV, B_CHUNK = 8, 64, 2, 128, 5, 128
B_UNROLL = 2
B_HPG = B_HEADS // B_GROUPS
B_GW = B_HPG * B_HEADDIM
C_HEADS, C_HEADDIM = 16, 64
NA_ROWS, NA_COLS = 8, 16
NA_HG = 4
NA_UNROLL = 16
N_EXPERTS, TOP_K = 8, 2
MOE_GROUPS = 2
MOE_TM = 512


def _cparams(sem):
    return pltpu.CompilerParams(dimension_semantics=sem, vmem_limit_bytes=VMEM_LIMIT_BYTES)


def _rms(x, g):
    return x * lax.rsqrt(jnp.mean(x * x, axis=-1, keepdims=True) + EPS) * g


def _silu(x):
    return x * jax.nn.sigmoid(x)


def _dot(a, b):
    return jnp.dot(a, b, preferred_element_type=F32)


def _dot_nt(a, b):
    return lax.dot_general(a, b, (((1,), (1,)), ((), ())), preferred_element_type=F32)


def _dot_tn(a, b):
    return lax.dot_general(a, b, (((0,), (0,)), ((), ())), preferred_element_type=F32)


def _staggered(stage_fns):
    live = list(stage_fns)
    step = 0
    done = [False] * len(live)
    while not all(done):
        for k, gen in enumerate(live):
            if step >= k and not done[k]:
                try:
                    next(gen)
                except StopIteration:
                    done[k] = True
        step += 1


def _norm_proj_kernel(x_ref, g_ref, lb_ref, *refs, acts, used_cols, transposed, col_chunk, sub):
    n_w = len(used_cols)
    w_src = refs[:n_w]
    o_refs = refs[n_w:n_w + len(acts)]
    w_refs = refs[n_w + len(acts):]

    @pl.when(pl.program_id(0) == 0)
    def _():
        for src, dst, used, flipped in zip(w_src, w_refs, used_cols, transposed):
            for c0 in range(0, used, col_chunk):
                c1 = min(c0 + col_chunk, used)
                if len(src.shape) == 2:
                    cols = src[:, c0:c1]
                elif flipped:
                    cols = src[0, c0:c1, :].T
                else:
                    cols = src[0, :, c0:c1]
                dst[:, c0:c1] = cols.astype(BF16)

    def stages(r0):
        rows = pl.ds(r0, sub)
        h = _rms(x_ref[rows, :], g_ref[...]).astype(BF16)
        yield
        piece, off = 0, 0
        for o_ref, act in zip(o_refs, acts):
            n = o_ref.shape[1]
            if off == used_cols[piece]:
                piece, off = piece + 1, 0
            w_ref = w_refs[piece]
            for c0 in range(0, n, col_chunk):
                c1 = min(c0 + col_chunk, n)
                y = _dot(h, w_ref[:, off + c0:off + c1])
                if act == "silu":
                    y = _silu(y)
                elif act == "log_gate":
                    lb = lb_ref[:, c0:c1]
                    y = jnp.log(lb + (1.0 - lb) * jax.nn.sigmoid(y))
                o_ref[rows, c0:c1] = y.astype(o_ref.dtype)
                yield
            off += n

    _staggered([stages(r0) for r0 in range(0, x_ref.shape[0], sub)])


def _norm_proj(x, g, w, outs, lb=None, layer=0, tm=512, sub=256, col_chunk=1024):
    T, D = x.shape
    pieces = list(w) if isinstance(w, (list, tuple)) else [(w, w.shape[-1])]
    ws = [pc[0] for pc in pieces]
    used_cols = tuple(pc[1] for pc in pieces)
    transposed = tuple(len(pc) > 2 and pc[2] for pc in pieces)
    widths = [o[0] for o in outs]
    assert sum(widths) == sum(used_cols) and T % tm == 0
    if lb is None:
        lb = jnp.zeros((LANES,), F32)

    def resident(v):
        if v.ndim == 3:
            return pl.BlockSpec((1,) + v.shape[1:], lambda i: (layer, 0, 0), pipeline_mode=pl.Buffered(1))
        return pl.BlockSpec(v.shape, lambda i: (0, 0), pipeline_mode=pl.Buffered(1))

    return pl.pallas_call(
        functools.partial(_norm_proj_kernel, acts=tuple(o[2] for o in outs), used_cols=used_cols,
                          transposed=transposed, col_chunk=col_chunk, sub=sub),
        grid=(T // tm,),
        in_specs=[_row_spec(tm, D, 1), _whole_spec((1, D), 1), _whole_spec((1, lb.shape[0]), 1)]
                 + [resident(v) for v in ws],
        out_specs=[_row_spec(tm, n, 1) for n in widths],
        out_shape=[jax.ShapeDtypeStruct((T, n), dt) for n, dt, _ in outs],
        scratch_shapes=[pltpu.VMEM((D, n), BF16) for n in used_cols],
        compiler_params=_cparams(("arbitrary",)),
        name="norm_proj",
    )(x, g.reshape(1, D), lb.reshape(1, -1), *ws)


def _ple_update(x, p, gin_ref, wg_ref, wp_ref, gpost_ref):
    h = _rms(x, gin_ref[...]).astype(BF16)
    gate = jax.nn.sigmoid(_dot(h, wg_ref[...]))
    proj = _dot(p.astype(BF16), wp_ref[...])
    return x + _rms(gate * proj, gpost_ref[...])


def _row_spec(tm, width, n_grid):
    return pl.BlockSpec((tm, width), (lambda i: (i, 0)) if n_grid == 1 else (lambda i, j: (i, 0)))


def _whole_spec(shape, n_grid):
    zeros = (0,) * len(shape)
    return pl.BlockSpec(shape, (lambda i: zeros) if n_grid == 1 else (lambda i, j: zeros))


def _mix_ffn_ple_kernel(oa_ref, ob_ref, wa_ref, wb_ref, gmix_ref, x_ref, gpre_ref, wg_ref, wu_ref, wd_ref,
                        gpost_ref, p_ref, gin_ref, wpg_ref, wpp_ref, gple_ref, o_ref, *, sub, tf):
    tm = x_ref.shape[0]
    F = wg_ref.shape[1]
    def stages(r0):
        rows = pl.ds(r0, sub)
        mix = _dot(oa_ref[rows, :], wa_ref[...]) + _dot(ob_ref[rows, :], wb_ref[...])
        x1 = x_ref[rows, :] + _rms(mix, gmix_ref[...])
        h = _rms(x1, gpre_ref[...]).astype(BF16)
        yield
        acc = None
        for f0 in range(0, F, tf):
            f1 = min(f0 + tf, F)
            a = (_silu(_dot(h, wg_ref[:, f0:f1])) * _dot(h, wu_ref[:, f0:f1])).astype(BF16)
            part = _dot(a, wd_ref[f0:f1, :])
            acc = part if acc is None else acc + part
            yield
        x2 = x1 + _rms(acc, gpost_ref[...])
        o_ref[rows, :] = _ple_update(x2, p_ref[rows, :], gin_ref, wpg_ref, wpp_ref, gple_ref)
        yield

    _staggered([stages(r0) for r0 in range(0, tm, sub)])


def _mix_ffn_ple(o_a, o_b, w_a, w_b, g_mix, x, g_pre, wg, wu, wd, g_post, p, g_in, wpg, wpp, g_ple,
                 p_row0=0, tm=512, sub=256, tf=1536):
    T, D = x.shape
    F = wg.shape[1]
    assert T % tm == 0 and tm % sub == 0 and p_row0 % tm == 0
    p_blk0 = p_row0 // tm
    vec = lambda g: g.reshape(1, D)
    resident = lambda w: pl.BlockSpec(w.shape, lambda i: (0, 0), pipeline_mode=pl.Buffered(1))
    return pl.pallas_call(
        functools.partial(_mix_ffn_ple_kernel, sub=sub, tf=tf),
        grid=(T // tm,),
        in_specs=[_row_spec(tm, o_a.shape[1], 1), _row_spec(tm, o_b.shape[1], 1),
                  resident(w_a), resident(w_b), _whole_spec((1, D), 1),
                  _row_spec(tm, D, 1), _whole_spec((1, D), 1),
                  resident(wg), resident(wu), resident(wd),
                  _whole_spec((1, D), 1),
                  pl.BlockSpec((tm, p.shape[1]), lambda i: (i + p_blk0, 0)), _whole_spec((1, D), 1),
                  resident(wpg), resident(wpp), _whole_spec((1, D), 1)],
        out_specs=_row_spec(tm, D, 1),
        out_shape=jax.ShapeDtypeStruct((T, D), F32),
        compiler_params=_cparams(("parallel",)),
        name="mix_ffn_ple",
    )(o_a, o_b, w_a, w_b, vec(g_mix), x, vec(g_pre), wg, wu, wd, vec(g_post), p, vec(g_in), wpg, wpp,
      vec(g_ple))


def _roll_rows(x, s, rev):
    n = x.shape[0]
    return pltpu.roll(x, (n - s) if rev else s, 0)


def _cumsum_rows(x, tau, rev):
    n = x.shape[0]
    s = 1
    while s < n:
        if s % SUBLANES:
            shifted = jnp.where(tau >= s, _roll_rows(x, s, rev), 0.0)
        else:
            zeros = jnp.zeros((s,) + x.shape[1:], x.dtype)
            shifted = (jnp.concatenate([x[s:], zeros], axis=0) if rev
                       else jnp.concatenate([zeros, x[:n - s]], axis=0))
        x = x + shifted
        s *= 2
    return x


def _hgrn_levels(C):
    return [C >> (i + 1) for i in range(C.bit_length() - 1)]


def _hgrn_pair_classes(C):
    t = np.arange(C)[:, None]
    s = np.arange(C)[None, :]
    out = np.full((2 * C, 2 * C), -1, np.int32)
    for d, rev in enumerate((False, True)):
        tau, sig = (C - 1 - t, C - 1 - s) if rev else (t, s)
        blk = np.full((C, C), -1, np.int32)
        blk[t == s] = 0
        for i, L in enumerate(_hgrn_levels(C)):
            m = ((t & -(2 * L)) == (s & -(2 * L))) & ((tau & (2 * L - 1)) >= L) & ((sig & (2 * L - 1)) < L)
            blk[m] = i + 1
        out[d * C:(d + 1) * C, d * C:(d + 1) * C] = blk
    return out


def _hgrn_kernel(q_ref, gf_ref, gb_ref, i_ref, gate_ref, cls_ref, ng_ref, o_ref, acc_scr, st_scr):
    S, DK = q_ref.shape
    C = A_CHUNK
    nc = S // C
    assert nc % (2 * A_UNROLL) == 0
    ng = ng_ref[...]
    row = lax.broadcasted_iota(jnp.int32, (C, DK), 0)
    levels = _hgrn_levels(C)
    zero_half = jnp.zeros((C, DK), BF16)

    def stack(top, bottom):
        return jnp.concatenate([top, bottom], axis=0)

    def block_diag(x):
        return stack(jnp.concatenate([x[:C], zero_half], axis=1), jnp.concatenate([zero_half, x[C:]], axis=1))

    def make_decay(rev):
        tau = (C - 1 - row) if rev else row
        odd_rank = (tau & 1) == 1
        last = 0 if rev else C - 1

        def ref_rows(b, L):
            off = L if rev else L - 1
            if 2 * L >= SUBLANES:
                pieces = [jnp.broadcast_to(b[j * 2 * L + off:j * 2 * L + off + 1, :], (2 * L, DK))
                          for j in range(C // (2 * L))]
                return pieces[0] if len(pieces) == 1 else jnp.concatenate(pieces, axis=0)
            if L == 1:
                return jnp.where(odd_rank, _roll_rows(b, 1, rev), b)
            b3 = b.reshape(C // SUBLANES, SUBLANES, DK)
            sub = lax.broadcasted_iota(jnp.int32, b3.shape, 1)
            out = None
            for j in range(SUBLANES // (2 * L)):
                piece = jnp.broadcast_to(b3[:, j * 2 * L + off:j * 2 * L + off + 1, :], b3.shape)
                out = piece if out is None else jnp.where(sub >= j * 2 * L, piece, out)
            return out.reshape(C, DK)

        def decay(g):
            b = _cumsum_rows(g, tau, rev)
            return b, [ref_rows(b, L) for L in levels], b[last:last + 1, :]

        return decay

    decay_fwd, decay_bwd = make_decay(False), make_decay(True)
    pair_cls = cls_ref[...]
    st_scr[...] = jnp.zeros_like(st_scr)

    def chunk_pair(rows_f, rows_b, st):
        q = stack(q_ref[rows_f, :], q_ref[rows_b, :]).astype(F32)
        g_f, g_b = gf_ref[rows_f, :], gb_ref[rows_b, :]
        k = 1.0 - jnp.exp(stack(g_f, g_b))
        vb = stack(i_ref[rows_f, :], i_ref[rows_b, :])
        b_f, refs_f, last_f = decay_fwd(g_f)
        b_b, refs_b, last_b = decay_bwd(g_b)
        b = stack(b_f, b_b)

        o = _dot_nt(block_diag((q * jnp.exp(b)).astype(BF16)), st.astype(BF16))

        qb, kb = q.astype(BF16), k.astype(BF16)
        attn = jnp.where(pair_cls == 0, _dot_nt(qb, kb), 0.0)
        for i in range(len(levels)):
            e = jnp.exp(-jnp.abs(b - stack(refs_f[i], refs_b[i]))).astype(BF16)
            attn = jnp.where(pair_cls == i + 1, _dot_nt(qb * e, kb * e), attn)
        o = o + _dot(attn.astype(BF16), vb)

        b_last = stack(jnp.broadcast_to(last_f, (C, DK)), jnp.broadcast_to(last_b, (C, DK)))
        khat = block_diag((k * jnp.exp(b_last - b)).astype(BF16))
        keep = jnp.exp(jnp.concatenate([last_f, last_b], axis=1))
        return o, st * keep + _dot_tn(vb, khat)

    def make_body(final):
        def body(it, carry):
            st = st_scr[...]
            for u in range(A_UNROLL):
                ci = it * A_UNROLL + u
                rows_f = pl.ds(pl.multiple_of(ci * C, C), C)
                rows_b = pl.ds(pl.multiple_of((nc - 1 - ci) * C, C), C)
                o, st = chunk_pair(rows_f, rows_b, st)
                for rows, part in ((rows_f, o[:C]), (rows_b, o[C:])):
                    if final:
                        tot = acc_scr[rows, :] + part
                        o_ref[rows, :] = (_rms(tot, ng) * gate_ref[rows, :].astype(F32)).astype(o_ref.dtype)
                    else:
                        acc_scr[rows, :] = part
            st_scr[...] = st
            return carry

        return body

    trips = nc // A_UNROLL
    lax.fori_loop(0, trips // 2, make_body(False), 0)
    lax.fori_loop(trips // 2, trips, make_body(True), 0)


def _hgrn_mixer(q, g_fwd, g_bwd, v, gate, norm_g, batch, seq):
    T = q.shape[0]
    H, DK = A_HEADS, A_DK
    head = pl.BlockSpec((seq, DK), lambda b, h: (b, h))
    classes = jnp.asarray(_hgrn_pair_classes(A_CHUNK))
    return pl.pallas_call(
        _hgrn_kernel,
        grid=(batch, H),
        in_specs=[head, head, head, head, head,
                  pl.BlockSpec(classes.shape, lambda b, h: (0, 0)),
                  pl.BlockSpec((1, DK), lambda b, h: (0, 0))],
        out_specs=pl.BlockSpec((seq, DK), lambda b, h: (b, h)),
        out_shape=jax.ShapeDtypeStruct((T, H * DK), BF16),
        scratch_shapes=[pltpu.VMEM((seq, DK), F32), pltpu.VMEM((DK, 2 * DK), F32)],
        compiler_params=_cparams(("parallel", "parallel")),
        name="hgrn2",
    )(q, g_fwd, g_bwd, v, gate, classes, norm_g.reshape(1, DK))


def _expand_heads(col, j0, width):
    q = col.shape[0]
    lane = lax.broadcasted_iota(jnp.int32, (q, width), 1)
    out = jnp.broadcast_to(col[:, j0 + B_HPG - 1:j0 + B_HPG], (q, width))
    for j in range(B_HPG - 2, -1, -1):
        out = jnp.where(lane < (j + 1) * B_HEADDIM,
                        jnp.broadcast_to(col[:, j0 + j:j0 + j + 1], (q, width)), out)
    return out


def _ssd_kernel(z_ref, x_ref, bm_ref, cm_ref, dt_ref, cwx_ref, cwb_ref, cwc_ref,
                cbx_ref, cbb_ref, cbc_ref, hp_ref, dsk_ref, ng_ref, o_ref,
                xs_scr, b_scr, c_scr, y_scr, *st_scrs):
    S = x_ref.shape[0]
    Q = B_CHUNK
    nc = S // Q
    GW = B_GW
    hp = hp_ref[0]
    a_row, dtb_row = hp[0:1, :], hp[1:2, :]
    row = lax.broadcasted_iota(jnp.int32, (Q, LANES), 0)
    t2 = lax.broadcasted_iota(jnp.int32, (Q, Q), 0)
    s2 = lax.broadcasted_iota(jnp.int32, (Q, Q), 1)
    lane_gw = lax.broadcasted_iota(jnp.int32, (Q, GW), 1)
    head_mask = [(lane_gw >= j * B_HEADDIM) & (lane_gw < (j + 1) * B_HEADDIM) for j in range(B_HPG)]
    halo = SUBLANES

    def conv_body(c, carry):
        r0 = pl.multiple_of(c * Q, Q)
        rp = pl.multiple_of(jnp.maximum(r0 - halo, 0), halo)
        rn = pl.multiple_of(jnp.minimum(r0 + Q, S - halo), halo)
        has_prev = c > 0
        has_next = c < nc - 1

        def conv(src_ref, w_ref, bias_ref):
            prev = jnp.where(has_prev, src_ref[pl.ds(rp, halo), :].astype(F32), 0.0)
            nxt = jnp.where(has_next, src_ref[pl.ds(rn, halo), :].astype(F32), 0.0)
            xx = jnp.concatenate([prev, src_ref[pl.ds(r0, Q), :].astype(F32), nxt], axis=0)
            n = Q + 2 * halo
            w = w_ref[0]
            acc = None
            for j in range(B_CONV):
                delta = j - B_CONV // 2
                sh = xx if delta == 0 else pltpu.roll(xx, (-delta) % n, 0)
                term = sh[halo:halo + Q, :] * w[j:j + 1, :]
                acc = term if acc is None else acc + term
            return _silu(acc + bias_ref[0])

        xs_scr[pl.ds(r0, Q), :] = conv(x_ref, cwx_ref, cbx_ref)
        b_scr[pl.ds(r0, Q), :] = conv(bm_ref, cwb_ref, cbb_ref).astype(BF16)
        c_scr[pl.ds(r0, Q), :] = conv(cm_ref, cwc_ref, cbc_ref).astype(BF16)
        return carry

    lax.fori_loop(0, nc, conv_body, 0)

    def make_chunk(rev):
        tau = (Q - 1 - row) if rev else row
        last = 0 if rev else Q - 1
        j0 = B_HPG if rev else 0
        pair_ok = (s2 >= t2) if rev else (t2 >= s2)

        def chunk(rows, st):
            dt = jax.nn.softplus(dt_ref[rows, :] + dtb_row)
            cs = _cumsum_rows(dt * a_row, tau, rev)
            cs_t = cs.T
            xs = xs_scr[rows, :]
            bm = b_scr[rows, :]
            cm = c_scr[rows, :]
            xdt = xs * _expand_heads(dt, j0, GW)
            ecs = _expand_heads(cs, j0, GW)
            ecs_last = ecs[last:last + 1, :]

            y = _dot(cm, st.astype(BF16)) * jnp.exp(ecs)

            scores = _dot_nt(cm, bm)
            for j in range(B_HPG):
                col = jnp.broadcast_to(cs[:, j0 + j:j0 + j + 1], (Q, Q))
                rw = jnp.broadcast_to(cs_t[j0 + j:j0 + j + 1, :], (Q, Q))
                decay = jnp.where(pair_ok, jnp.exp(jnp.minimum(col - rw, 0.0)), 0.0)
                xh = jnp.where(head_mask[j], xdt, 0.0).astype(BF16)
                y = y + _dot((scores * decay).astype(BF16), xh)

            xdec = (xdt * jnp.exp(ecs_last - ecs)).astype(BF16)
            return y, st * jnp.exp(ecs_last) + _dot_tn(bm, xdec)

        return chunk

    chunk_fns = (make_chunk(False), make_chunk(True))
    for st_scr in st_scrs:
        st_scr[...] = jnp.zeros_like(st_scr)

    def make_body(final, unroll):
        def body(it, carry):
            for rev, chunk, st_scr in zip((False, True), chunk_fns, st_scrs):
                st = st_scr[...]
                for u in range(unroll):
                    ci = it * unroll + u
                    c = (nc - 1 - ci) if rev else ci
                    rows = pl.ds(pl.multiple_of(c * Q, Q), Q)
                    y, st = chunk(rows, st)
                    if final:
                        tot = y_scr[rows, :] + y + dsk_ref[0] * xs_scr[rows, :]
                        tot = tot * _silu(z_ref[rows, :].astype(F32))
                        o_ref[rows, :] = _rms(tot, ng_ref[0]).astype(o_ref.dtype)
                    else:
                        y_scr[rows, :] = y
                st_scr[...] = st
            return carry

        return body

    assert nc % (2 * B_UNROLL) == 0
    lax.fori_loop(0, nc // (2 * B_UNROLL), make_body(False, B_UNROLL), 0)
    lax.fori_loop(nc // 2, nc, make_body(True, 1), 0)


def _ssd_mixer(u, dt, conv_w, conv_b, a_log, dt_bias, d_skip, norm_g, batch, seq, col0):
    T = u.shape[0]
    G, GW, N = B_GROUPS, B_GW, B_STATE
    W = B_HEADS * B_HEADDIM
    assert col0 % GW == 0 and GW == 2 * N
    z_blk = col0 // GW
    x_blk = z_blk + W // GW
    b_blk = (col0 + 2 * W) // N
    c_blk = b_blk + G

    def pad_rows(w):
        return jnp.pad(w, ((0, SUBLANES - w.shape[0]), (0, 0)))

    cwx = pad_rows(conv_w[:, :W]).reshape(SUBLANES, G, GW).transpose(1, 0, 2)
    cwb = pad_rows(conv_w[:, W:W + G * N]).reshape(SUBLANES, G, N).transpose(1, 0, 2)
    cwc = pad_rows(conv_w[:, W + G * N:]).reshape(SUBLANES, G, N).transpose(1, 0, 2)
    cbx = conv_b[:W].reshape(G, 1, GW)
    cbb = conv_b[W:W + G * N].reshape(G, 1, N)
    cbc = conv_b[W + G * N:].reshape(G, 1, N)
    a_neg = -jnp.exp(a_log.astype(F32))
    per_dir = lambda v: jnp.concatenate([v[0].reshape(G, B_HPG), v[1].reshape(G, B_HPG)], axis=1)
    hp = jnp.stack([per_dir(a_neg), per_dir(dt_bias.astype(F32))], axis=1)
    hp = jnp.pad(hp, ((0, 0), (0, SUBLANES - 2), (0, LANES - 2 * B_HPG)))
    dsk = jnp.repeat(d_skip.astype(F32), B_HEADDIM).reshape(G, 1, GW)
    ng = norm_g.reshape(G, 1, GW)

    gspec = lambda shape: pl.BlockSpec((1,) + shape, lambda b, g: (g, 0, 0))
    return pl.pallas_call(
        _ssd_kernel,
        grid=(batch, G),
        in_specs=[pl.BlockSpec((seq, GW), lambda b, g: (b, z_blk + g)),
                  pl.BlockSpec((seq, GW), lambda b, g: (b, x_blk + g)),
                  pl.BlockSpec((seq, N), lambda b, g: (b, b_blk + g)),
                  pl.BlockSpec((seq, N), lambda b, g: (b, c_blk + g)),
                  pl.BlockSpec((seq, LANES), lambda b, g: (b, g)),
                  gspec((SUBLANES, GW)), gspec((SUBLANES, N)), gspec((SUBLANES, N)),
                  gspec((1, GW)), gspec((1, N)), gspec((1, N)),
                  gspec((SUBLANES, LANES)), gspec((1, GW)), gspec((1, GW))],
        out_specs=pl.BlockSpec((seq, GW), lambda b, g: (b, g)),
        out_shape=jax.ShapeDtypeStruct((T, W), BF16),
        scratch_shapes=[pltpu.VMEM((seq, GW), F32), pltpu.VMEM((seq, N), BF16),
                        pltpu.VMEM((seq, N), BF16), pltpu.VMEM((seq, GW), F32),
                        pltpu.VMEM((N, GW), F32), pltpu.VMEM((N, GW), F32)],
        compiler_params=_cparams(("parallel", "parallel")),
        name="ssd",
    )(u, u, u, u, dt, cwx, cwb, cwc, cbx, cbb, cbc, hp, dsk, ng)


def _na_kernel(q_ref, k_ref, v_ref, bias_ref, o_ref):
    S, HW = q_ref.shape
    W = GRID_W
    n_rows = S // W
    kh = min(NA_ROWS, n_rows)
    nk = kh * W
    n_pairs = bias_ref.shape[1] // NA_HG
    lane = lax.broadcasted_iota(jnp.int32, (W, HW), 1)
    head_masks = [(lane >= h * C_HEADDIM) & (lane < (h + 1) * C_HEADDIM) for h in range(NA_HG)]
    scale = C_HEADDIM ** -0.5

    def body(r, carry):
        rs = jnp.clip(r - kh // 2, 0, n_rows - kh)
        q = q_ref[pl.ds(pl.multiple_of(r * W, W), W), :] * jnp.asarray(scale, q_ref.dtype)
        kw = k_ref[pl.ds(pl.multiple_of(rs * W, W), nk), :]
        vw = v_ref[pl.ds(pl.multiple_of(rs * W, W), nk), :]
        zero = jnp.zeros_like(q)
        qs = jnp.concatenate([jnp.where(m, q, zero) for m in head_masks], axis=0)
        d0 = (NA_ROWS - 1) - (r - rs)
        bias = jnp.concatenate(
            [jnp.concatenate([bias_ref[0, h * n_pairs + d0 + 2 * m] for m in range(kh // 2)], axis=1)
             for h in range(NA_HG)], axis=0)
        s = _dot_nt(qs, kw) + bias
        m = jnp.max(s, axis=-1, keepdims=True)
        p = jnp.exp(s - m)
        l = jnp.sum(p, axis=-1, keepdims=True)
        res = _dot(p.astype(BF16), vw) / l
        out = jnp.zeros((W, HW), F32)
        for h in range(NA_HG):
            out = jnp.where(head_masks[h], res[h * W:(h + 1) * W, :], out)
        o_ref[pl.ds(pl.multiple_of(r * W, W), W), :] = out.astype(o_ref.dtype)
        return carry

    lax.fori_loop(0, n_rows, body, 0, unroll=NA_UNROLL)


def _na_bias_table(rpb):
    H, n_dr, n_dc = rpb.shape
    W = GRID_W
    c = np.arange(W)
    qs = np.clip(c - NA_COLS // 2, 0, W - NA_COLS)
    valid = (c[None, :] >= qs[:, None]) & (c[None, :] < qs[:, None] + NA_COLS)
    dc = np.clip(c[None, :] - c[:, None] + (NA_COLS - 1), 0, n_dc - 1)
    onehot = np.zeros((2, n_dc, W, 2 * W), np.float32)
    for half in range(2):
        onehot[half, :, :, half * W:(half + 1) * W] = dc[None] == np.arange(n_dc)[:, None, None]
    both = jnp.concatenate([rpb[:, :-1], rpb[:, 1:]], axis=2).astype(F32)
    pairs = jnp.einsum("hre,ecx->hrcx", both, jnp.asarray(onehot.reshape(2 * n_dc, W, 2 * W)),
                       precision=lax.Precision.HIGHEST)
    valid2 = np.concatenate([valid, valid], axis=1)
    pairs = jnp.where(jnp.asarray(valid2)[None, None], pairs, -jnp.inf)
    return pairs.reshape(H // NA_HG, NA_HG * (n_dr - 1), W, 2 * W)


def _neighborhood_attention(qkv, rpb, batch, seq):
    T = qkv.shape[0]
    HW = NA_HG * C_HEADDIM
    n_hg = C_HEADS // NA_HG
    n_rows = seq // GRID_W
    assert n_rows >= NA_ROWS and NA_ROWS % 2 == 0
    table = _na_bias_table(rpb)
    return pl.pallas_call(
        _na_kernel,
        grid=(batch, n_hg),
        in_specs=[pl.BlockSpec((seq, HW), lambda b, g: (b, g)),
                  pl.BlockSpec((seq, HW), lambda b, g: (b, n_hg + g)),
                  pl.BlockSpec((seq, HW), lambda b, g: (b, 2 * n_hg + g)),
                  pl.BlockSpec((1,) + table.shape[1:], lambda b, g: (g, 0, 0, 0))],
        out_specs=pl.BlockSpec((seq, HW), lambda b, g: (b, g)),
        out_shape=jax.ShapeDtypeStruct((T, C_HEADS * C_HEADDIM), BF16),
        compiler_params=_cparams(("parallel", "parallel")),
        name="natten",
    )(qkv, qkv, qkv, table)


def _store_token_tiles(ref, val, t0=0):
    count, width = val.shape
    n = width // LANES
    for j in range(n):
        ref[pl.ds(t0 * n + j, count, stride=n), :] = val[:, j * LANES:(j + 1) * LANES]


def _load_token_tiles(ref, n, t0=0, count=None):
    if count is None:
        count = ref.shape[0] // n - t0
    return jnp.concatenate([ref[pl.ds(t0 * n + j, count, stride=n), :] for j in range(n)], axis=1)


def _proj_router_kernel(a_ref, w_ref, gmix_ref, x_ref, g_ref, wr_ref, x1_ref, o_ref, h_ref):
    tm = x_ref.shape[0]
    sub = tm

    def stages(r0):
        rows = pl.ds(r0, sub)
        x1 = x_ref[rows, :] + _rms(_dot(a_ref[rows, :], w_ref[...]), gmix_ref[...])
        x1_ref[rows, :] = x1
        yield
        h = _rms(x1, g_ref[...])
        _store_token_tiles(h_ref, h, r0)
        h_hi = h.astype(BF16)
        h_lo = (h - h_hi.astype(F32)).astype(BF16)
        both = _dot(h_hi, wr_ref[...])
        logits = both[:, :LANES] + both[:, LANES:] + _dot(h_lo, wr_ref[:, :LANES])
        yield
        lane = lax.broadcasted_iota(jnp.int32, logits.shape, 1)
        neg = -jnp.inf
        l1 = jnp.where(lane < N_EXPERTS, logits, neg)
        m1 = jnp.max(l1, axis=-1, keepdims=True)
        i1 = jnp.min(jnp.where(l1 == m1, lane, LANES), axis=-1, keepdims=True)
        l2 = jnp.where(lane == i1, neg, l1)
        m2 = jnp.max(l2, axis=-1, keepdims=True)
        i2 = jnp.min(jnp.where(l2 == m2, lane, LANES), axis=-1, keepdims=True)
        e = jnp.exp(m2 - m1)
        w1 = 1.0 / (1.0 + e)
        w2 = e / (1.0 + e)
        o_ref[rows, :] = jnp.where(lane == 0, i1.astype(F32),
                                   jnp.where(lane == 1, i2.astype(F32),
                                             jnp.where(lane == 2, w1, jnp.where(lane == 3, w2, 0.0))))
        yield

    _staggered([stages(r0) for r0 in range(0, tm, sub)])


def _proj_router(a, w, g_mix, x, g, w_router, tm=512):
    T, D = x.shape
    n = D // LANES
    wr = jnp.pad(w_router.astype(F32), ((0, 0), (0, LANES - w_router.shape[1])))
    wr_hi = wr.astype(BF16)
    wr = jnp.concatenate([wr_hi, (wr - wr_hi.astype(F32)).astype(BF16)], axis=1)
    x1, route, h = pl.pallas_call(
        _proj_router_kernel,
        grid=(T // tm,),
        in_specs=[_row_spec(tm, a.shape[1], 1), _whole_spec(w.shape, 1), _whole_spec((1, D), 1),
                  _row_spec(tm, D, 1), _whole_spec((1, D), 1), _whole_spec((D, 2 * LANES), 1)],
        out_specs=[_row_spec(tm, D, 1), _row_spec(tm, LANES, 1), _row_spec(tm * n, LANES, 1)],
        out_shape=[jax.ShapeDtypeStruct((T, D), F32),
                   jax.ShapeDtypeStruct((T, LANES), F32),
                   jax.ShapeDtypeStruct((T * n, LANES), F32)],
        compiler_params=_cparams(("parallel",)),
        name="proj_router",
    )(a, w, g_mix.reshape(1, D), x, g.reshape(1, D), wr)
    return x1, route, h.reshape(T, n, LANES)


SC_CORES = 2
SC_SUBCORES = 16
SC_WORKERS = SC_CORES * SC_SUBCORES
SC_CHUNK = 32


def _sc_mesh():
    return plsc.VectorSubcoreMesh(core_axis_name="c", subcore_axis_name="s",
                                  num_cores=SC_CORES, num_subcores=SC_SUBCORES)


def _sc_index_blocks(idx):
    return idx.reshape(SC_WORKERS, -1, SC_CHUNK)


def _sc_scatter_tokens(h, dest, n_rows, row0=0):
    T = dest.shape[0]
    _, n, _ = h.shape
    per_worker = T // SC_WORKERS
    n_chunks = per_worker // SC_CHUNK
    assert per_worker * SC_WORKERS == T and n_chunks * SC_CHUNK == per_worker

    @functools.partial(
        pl.kernel, mesh=_sc_mesh(),
        out_type=jax.ShapeDtypeStruct((n_rows, n, LANES), h.dtype),
        scratch_types=[pltpu.VMEM((n_chunks, SC_CHUNK), jnp.int32),
                       pltpu.VMEM((n_chunks, SC_CHUNK), jnp.int32),
                       pltpu.VMEM((SC_CHUNK, n, LANES), h.dtype)],
        name="sc_scatter_tokens",
    )
    def scatter(h_hbm, d0_hbm, d1_hbm, o_hbm, i0_v, i1_v, rows_v):
        wid = lax.axis_index("s") * SC_CORES + lax.axis_index("c")
        pltpu.sync_copy(d0_hbm.at[wid], i0_v)
        pltpu.sync_copy(d1_hbm.at[wid], i1_v)

        @pl.loop(0, n_chunks)
        def _(j):
            pltpu.sync_copy(h_hbm.at[pl.ds(row0 + wid * per_worker + j * SC_CHUNK, SC_CHUNK)], rows_v)
            pltpu.sync_copy(rows_v, o_hbm.at[i0_v.at[j]])
            pltpu.sync_copy(rows_v, o_hbm.at[i1_v.at[j]])

    return scatter(h, _sc_index_blocks(dest[:, 0]), _sc_index_blocks(dest[:, 1]))


def _sc_gather_tokens(y, dest):
    T = dest.shape[0]
    _, n, _ = y.shape
    per_worker = T // SC_WORKERS
    n_chunks = per_worker // SC_CHUNK
    assert per_worker * SC_WORKERS == T and n_chunks * SC_CHUNK == per_worker
    out = jax.ShapeDtypeStruct((T, n, LANES), y.dtype)

    @functools.partial(
        pl.kernel, mesh=_sc_mesh(), out_type=(out, out),
        scratch_types=[pltpu.VMEM((n_chunks, SC_CHUNK), jnp.int32),
                       pltpu.VMEM((n_chunks, SC_CHUNK), jnp.int32),
                       pltpu.VMEM((SC_CHUNK, n, LANES), y.dtype)],
        name="sc_gather_tokens",
    )
    def gather(y_hbm, d0_hbm, d1_hbm, o0_hbm, o1_hbm, i0_v, i1_v, rows_v):
        wid = lax.axis_index("s") * SC_CORES + lax.axis_index("c")
        pltpu.sync_copy(d0_hbm.at[wid], i0_v)
        pltpu.sync_copy(d1_hbm.at[wid], i1_v)

        @pl.loop(0, n_chunks)
        def _(j):
            rows = pl.ds(wid * per_worker + j * SC_CHUNK, SC_CHUNK)
            pltpu.sync_copy(y_hbm.at[i0_v.at[j]], rows_v)
            pltpu.sync_copy(rows_v, o0_hbm.at[rows])
            pltpu.sync_copy(y_hbm.at[i1_v.at[j]], rows_v)
            pltpu.sync_copy(rows_v, o1_hbm.at[rows])

    return gather(y, _sc_index_blocks(dest[:, 0]), _sc_index_blocks(dest[:, 1]))


SC_PACK_PAIRS = 8
SC_LANES = 16


def _sc_pack_bf16_rows(w):
    R, C = w.shape
    pairs = R // 2
    per_worker = pairs // SC_WORKERS
    n_chunks = per_worker // SC_PACK_PAIRS
    assert n_chunks * SC_PACK_PAIRS * SC_WORKERS * 2 == R and C % SC_LANES == 0

    def round_bits(v):
        u = plsc.bitcast(v, jnp.int32)
        return u + 0x7FFF + (lax.shift_right_logical(u, 16) & 1)

    @functools.partial(
        pl.kernel, mesh=_sc_mesh(),
        out_type=jax.ShapeDtypeStruct((pairs, C), jnp.int32),
        scratch_types=[pltpu.VMEM((2 * SC_PACK_PAIRS, C), F32), pltpu.VMEM((SC_PACK_PAIRS, C), jnp.int32)],
        compiler_params=pltpu.CompilerParams(use_tc_tiling_on_sc=True, needs_layout_passes=False),
        name="sc_pack_bf16_rows",
    )
    def pack(w_hbm, o_hbm, in_v, out_v):
        wid = lax.axis_index("s") * SC_CORES + lax.axis_index("c")

        @pl.loop(0, n_chunks)
        def _(j):
            p0 = (wid * n_chunks + j) * SC_PACK_PAIRS
            pltpu.sync_copy(w_hbm.at[pl.ds(2 * p0, 2 * SC_PACK_PAIRS)], in_v)

            @pl.loop(0, C, step=SC_LANES)
            def _(c):
                cols = pl.ds(c, SC_LANES)
                for i in range(SC_PACK_PAIRS):
                    lo = lax.shift_right_logical(round_bits(in_v[2 * i, cols]), 16)
                    hi = round_bits(in_v[2 * i + 1, cols]) & jnp.int32(-65536)
                    out_v[i, cols] = lo | hi

            pltpu.sync_copy(out_v, o_hbm.at[pl.ds(p0, SC_PACK_PAIRS)])

    return pack(w)


def _moe_ffn_kernel(te_ref, rows_ref, h_ref, wg_ref, wu_ref, wd_ref, o_ref, acc_scr, *, tf):
    i = pl.program_id(0)
    tm, D = acc_scr.shape
    n = D // LANES
    F = wg_ref.shape[2]
    valid = rows_ref[i]
    unpack = lambda packed: pltpu.bitcast(packed, BF16)

    def ffn(count):
        h = _load_token_tiles(h_ref, n, 0, count).astype(BF16)
        acc = acc_scr.at[pl.ds(0, count)]
        for f0 in range(0, F, tf):
            a = (_silu(_dot(h, unpack(wg_ref[0, :, f0:f0 + tf])))
                 * _dot(h, unpack(wu_ref[0, :, f0:f0 + tf]))).astype(BF16)
            part = _dot(a, unpack(wd_ref[0, f0 // 2:(f0 + tf) // 2, :]))
            if f0 == 0:
                acc[...] = part
            else:
                acc[...] += part
        _store_token_tiles(o_ref, acc[...])
        if count < tm:
            o_ref[pl.ds(count * n, (tm - count) * n), :] = jnp.zeros(((tm - count) * n, LANES), o_ref.dtype)

    @pl.when(valid > tm // 2)
    def _():
        ffn(tm)

    @pl.when((valid > 0) & (valid <= tm // 2))
    def _():
        ffn(tm // 2)

    @pl.when(valid == 0)
    def _():
        o_ref[...] = jnp.zeros_like(o_ref)


def _moe_ffn(hs, wg, wu, wd, tile_expert, tile_rows, tm=MOE_TM, tf=512):
    rows, n, _ = hs.shape
    E, half_d, F = wg.shape
    D = 2 * half_d
    nt = tile_expert.shape[0]
    assert rows == nt * tm and n * LANES == D and wd.shape == (E, F // 2, D)
    grid_spec = pltpu.PrefetchScalarGridSpec(
        num_scalar_prefetch=2,
        grid=(nt,),
        in_specs=[pl.BlockSpec((tm * n, LANES), lambda i, te, na: (i, 0)),
                  pl.BlockSpec((1, D // 2, F), lambda i, te, na: (te[i], 0, 0)),
                  pl.BlockSpec((1, D // 2, F), lambda i, te, na: (te[i], 0, 0)),
                  pl.BlockSpec((1, F // 2, D), lambda i, te, na: (te[i], 0, 0))],
        out_specs=pl.BlockSpec((tm * n, LANES), lambda i, te, na: (i, 0)),
        scratch_shapes=[pltpu.VMEM((tm, D), F32)],
    )
    y = pl.pallas_call(
        functools.partial(_moe_ffn_kernel, tf=tf),
        grid_spec=grid_spec,
        out_shape=jax.ShapeDtypeStruct((rows * n, LANES), F32),
        compiler_params=_cparams(("arbitrary",)),
        name="moe_ffn",
    )(tile_expert, tile_rows, hs.reshape(rows * n, LANES), wg, wu, wd)
    return y.reshape(rows, n, LANES)


def _combine_ple_kernel(y0_ref, y1_ref, route_ref, x_ref, g_ref, p_ref, gin_ref, wpg_ref, wpp_ref,
                        gple_ref, *rest):
    o_ref = rest[-1]
    tm, D = x_ref.shape
    n = D // LANES
    sub = min(tm, 256)

    def stages(r0):
        rows = pl.ds(r0, sub)
        route = route_ref[rows, :]
        moe = (route[:, 2:3] * _load_token_tiles(y0_ref, n, r0, sub)
               + route[:, 3:4] * _load_token_tiles(y1_ref, n, r0, sub))
        x2 = x_ref[rows, :] + _rms(moe, g_ref[...])
        yield
        o_ref[rows, :] = _ple_update(x2, p_ref[rows, :], gin_ref, wpg_ref, wpp_ref, gple_ref)
        yield

    _staggered([stages(r0) for r0 in range(0, tm, sub)])


def _combine_ple(y0, y1, route, x, g, p, g_in, wpg, wpp, g_ple, row0=0, prev=None, p_row0=0, tm=512):
    T, D = x.shape
    Tp, n, _ = y0.shape
    assert row0 % tm == 0 and Tp % tm == 0 and n * LANES == D and p_row0 % tm == 0
    blk0 = row0 // tm
    p_blk0 = (row0 + p_row0) // tm
    vec = lambda v: v.reshape(1, D)
    shifted = lambda width: pl.BlockSpec((tm, width), lambda i: (i + blk0, 0))
    in_specs = [_row_spec(tm * n, LANES, 1), _row_spec(tm * n, LANES, 1), shifted(LANES),
                shifted(D), _whole_spec((1, D), 1),
                pl.BlockSpec((tm, p.shape[1]), lambda i: (i + p_blk0, 0)), _whole_spec((1, D), 1),
                _whole_spec(wpg.shape, 1), _whole_spec(wpp.shape, 1), _whole_spec((1, D), 1)]
    args = [y0.reshape(Tp * n, LANES), y1.reshape(Tp * n, LANES), route, x, vec(g), p, vec(g_in), wpg, wpp,
            vec(g_ple)]
    aliases = {}
    if prev is not None:
        in_specs.append(pl.BlockSpec(memory_space=pl.ANY))
        args.append(prev)
        aliases = {len(args) - 1: 0}
    return pl.pallas_call(
        _combine_ple_kernel,
        grid=(Tp // tm,),
        in_specs=in_specs,
        out_specs=shifted(D),
        out_shape=jax.ShapeDtypeStruct((T, D), F32),
        input_output_aliases=aliases,
        compiler_params=_cparams(("parallel",)),
        name="combine_ple",
    )(*args)


def _moe_plan(route, tm=MOE_TM):
    T = route.shape[0]
    e = route[:, :TOP_K].astype(jnp.int32).reshape(-1)
    onehot = (e[:, None] == jnp.arange(N_EXPERTS, dtype=jnp.int32)[None, :]).astype(jnp.int32)
    csum = jnp.cumsum(onehot, axis=0)
    rank = jnp.sum(csum * onehot, axis=1) - 1
    counts = csum[-1]
    padded = ((counts + tm - 1) // tm) * tm
    ends = jnp.cumsum(padded)
    starts = ends - padded
    dest = jnp.sum(starts[None, :] * onehot, axis=1) + rank
    nt = (T * TOP_K) // tm + N_EXPERTS
    tile_start = jnp.arange(nt, dtype=jnp.int32) * tm
    tile_expert = jnp.minimum(jnp.sum((tile_start[:, None] >= ends[None, :]).astype(jnp.int32), axis=1),
                              N_EXPERTS - 1).astype(jnp.int32)
    tile_rows = jnp.clip((starts + counts)[tile_expert] - tile_start, 0, tm).astype(jnp.int32)
    return tile_expert, tile_rows, dest.reshape(T, TOP_K).astype(jnp.int32)


def kernel(x, p, hgrn_lb_raw, e_norm_mix_pre, e_w_in, e_conv_w, e_conv_b, e_A_log, e_dt_bias, e_D, e_a_norm, e_b_norm, e_w_out, e_norm_mix_post, e_norm_ffn_pre, e_w_ffn_gate, e_w_ffn_up, e_w_ffn_down, e_norm_ffn_post, o_norm_mix_pre, o_w_qkv, o_rpb, o_w_out, o_norm_mix_post, o_norm_ffn_pre, o_w_router, o_w_exp_gate, o_w_exp_up, o_w_exp_down, o_norm_ffn_post, ple_norm_in, ple_w_gate, ple_w_proj, ple_norm_post):
    batch, seq, d_model = x.shape
    depth = p.shape[0]
    T = batch * seq
    xt = x.reshape(T, d_model)
    p_all = p.reshape(depth * T, -1)
    lb_all =jnp.cumsum(jax.nn.softmax(hgrn_lb_raw.astype(F32), axis=0), axis=0)

    a_kdim = A_HEADS * A_DK
    b_width = B_HEADS * B_HEADDIM
    conv_dim = b_width + 2 * B_GROUPS * B_STATE
    main_w = 5 * a_kdim + b_width + conv_dim

    def pack_experts(w):
        E, R, C = w.shape
        return _sc_pack_bf16_rows(w.reshape(E * R, C)).reshape(E, R // 2, C)

    for li in range(depth):
        j = li // 2
        ple =(ple_norm_in[li], ple_w_gate[li].astype(BF16), ple_w_proj[li].astype(BF16), ple_norm_post[li])
        if li % 2 == 0:
            w_in = e_w_in[j]
            dtf = w_in[:, main_w:main_w + B_HEADS].reshape(d_model, B_GROUPS, B_HPG)
            dtb = w_in[:, main_w + B_HEADS:].reshape(d_model, B_GROUPS, B_HPG)
            w_dt = jnp.pad(jnp.concatenate([dtf, dtb], axis=2), ((0, 0), (0, 0), (0, LANES - 2 * B_HPG)))
            w_pieces = [(jnp.swapaxes(e_w_in, 1, 2), main_w, True),
                        (w_dt.reshape(d_model, B_GROUPS * LANES), B_GROUPS * LANES)]
            outs = [(a_kdim, BF16, "silu"), (a_kdim, F32, "log_gate"), (a_kdim, F32, "log_gate"),
                    (a_kdim, BF16, None), (a_kdim, BF16, "silu"),
                    (b_width + conv_dim, BF16, None), (B_GROUPS * LANES, F32, None)]
            q_a, gf_a, gb_a, i_a, gate_a, u_b, dt = _norm_proj(
                xt, e_norm_mix_pre[j], w_pieces, outs, lb=lb_all[li], layer=j)
            o_a = _hgrn_mixer(q_a, gf_a, gb_a, i_a, gate_a, e_a_norm[j], batch, seq)
            o_b = _ssd_mixer(u_b, dt, e_conv_w[j], e_conv_b[j], e_A_log[j], e_dt_bias[j], e_D[j],
                             e_b_norm[j], batch, seq, 0)
            w_out = e_w_out[j].astype(BF16)
            xt = _mix_ffn_ple(o_a, o_b, w_out[:a_kdim], w_out[a_kdim:], e_norm_mix_post[j], xt,
                              e_norm_ffn_pre[j], e_w_ffn_gate[j].astype(BF16), e_w_ffn_up[j].astype(BF16),
                              e_w_ffn_down[j].astype(BF16), e_norm_ffn_post[j], p_all, *ple, p_row0=li * T)
        else:
            (qkv,) = _norm_proj(xt, o_norm_mix_pre[j], o_w_qkv,
                                [(3 * C_HEADS * C_HEADDIM, BF16, None)], layer=j)
            o_c = _neighborhood_attention(qkv, o_rpb[j], batch, seq)
            xt, route, h = _proj_router(o_c, o_w_out[j].astype(BF16), o_norm_mix_post[j], xt,
                                        o_norm_ffn_pre[j], o_w_router[j])
            expert_w = [pack_experts(w[j]) for w in (o_w_exp_gate, o_w_exp_up, o_w_exp_down)]
            x_mid, xt = xt, None
            t_grp = T // MOE_GROUPS
            for s in range(MOE_GROUPS):
                tile_expert, tile_rows, dest = _moe_plan(route[s * t_grp:(s + 1) * t_grp])
                hs = _sc_scatter_tokens(h, dest, tile_expert.shape[0] * MOE_TM, row0=s * t_grp)
                ys = _moe_ffn(hs, *expert_w, tile_expert, tile_rows)
                y0, y1 = _sc_gather_tokens(ys, dest)
                xt = _combine_ple(y0, y1, route, x_mid, o_norm_ffn_post[j], p_all, *ple,
                                  row0=s * t_grp, prev=xt, p_row0=li * T)
    return xt.reshape(batch, seq, d_model)
```

```python
import functools
import math

import numpy as np
import jax
import jax.numpy as jnp
from jax import lax
from jax.experimental import pallas as pl
from jax.experimental.pallas import tpu as pltpu
from jax.experimental.pallas import tpu_sc as plsc

F32 = jnp.float32
BF16 = jnp.bfloat16
EPS = 1e-6

LANES = 128
SUBLANES = 8
VMEM_LIMIT_BYTES = 56 * 1024 * 1024

GRID_W = 64
A_HEADS, A_DK, A_CHUNK = 4, 128, 64
A_UNROLL = 32
B_HEADS, B_HEADDIM, B_GROUPS, B_STATE, B_CONV, B_CHUNK = 8, 64, 2, 128, 5, 128
B_UNROLL = 2
B_HPG = B_HEADS // B_GROUPS
B_GW = B_HPG * B_HEADDIM
C_HEADS, C_HEADDIM = 16, 64
NA_ROWS, NA_COLS = 8, 16
NA_HG = 4
NA_UNROLL = 32
N_EXPERTS, TOP_K = 8, 2
MOE_GROUPS = 2
MOE_TM = 512


def _cparams(sem):
    return pltpu.CompilerParams(dimension_semantics=sem, vmem_limit_bytes=VMEM_LIMIT_BYTES)


def _rms(x, g):
    return x * lax.rsqrt(jnp.mean(x * x, axis=-1, keepdims=True) + EPS) * g


def _silu(x):
    return x * jax.nn.sigmoid(x)


def _dot(a, b):
    return jnp.dot(a, b, preferred_element_type=F32)


def _dot_nt(a, b):
    return lax.dot_general(a, b, (((1,), (1,)), ((), ())), preferred_element_type=F32)


def _dot_tn(a, b):
    return lax.dot_general(a, b, (((0,), (0,)), ((), ())), preferred_element_type=F32)


def _staggered(stage_fns):
    live = list(stage_fns)
    step = 0
    done = [False] * len(live)
    while not all(done):
        for k, gen in enumerate(live):
            if step >= k and not done[k]:
                try:
                    next(gen)
                except StopIteration:
                    done[k] = True
        step += 1


def _norm_proj_kernel(x_ref, g_ref, lb_ref, *refs, acts, used_cols, transposed, col_chunk, sub):
    n_w = len(used_cols)
    w_src = refs[:n_w]
    o_refs = refs[n_w:n_w + len(acts)]
    w_refs = refs[n_w + len(acts):]

    @pl.when(pl.program_id(0) == 0)
    def _():
        for src, dst, used, flipped in zip(w_src, w_refs, used_cols, transposed):
            for c0 in range(0, used, col_chunk):
                c1 = min(c0 + col_chunk, used)
                if len(src.shape) == 2:
                    cols = src[:, c0:c1]
                elif flipped:
                    cols = src[0, c0:c1, :].T
                else:
                    cols = src[0, :, c0:c1]
                dst[:, c0:c1] = cols.astype(BF16)

    def stages(r0):
        rows = pl.ds(r0, sub)
        h = _rms(x_ref[rows, :], g_ref[...]).astype(BF16)
        yield
        piece, off = 0, 0
        for o_ref, act in zip(o_refs, acts):
            n = o_ref.shape[1]
            if off == used_cols[piece]:
                piece, off = piece + 1, 0
            w_ref = w_refs[piece]
            for c0 in range(0, n, col_chunk):
                c1 = min(c0 + col_chunk, n)
                y = _dot(h, w_ref[:, off + c0:off + c1])
                if act == "silu":
                    y = _silu(y)
                elif act == "log_gate":
                    lb = lb_ref[:, c0:c1]
                    y = jnp.log(lb + (1.0 - lb) * jax.nn.sigmoid(y))
                o_ref[rows, c0:c1] = y.astype(o_ref.dtype)
                yield
            off += n

    _staggered([stages(r0) for r0 in range(0, x_ref.shape[0], sub)])


def _norm_proj(x, g, w, outs, lb=None, layer=0, tm=512, sub=256, col_chunk=1024):
    T, D = x.shape
    pieces = list(w) if isinstance(w, (list, tuple)) else [(w, w.shape[-1])]
    ws = [pc[0] for pc in pieces]
    used_cols = tuple(pc[1] for pc in pieces)
    transposed = tuple(len(pc) > 2 and pc[2] for pc in pieces)
    widths = [o[0] for o in outs]
    assert sum(widths) == sum(used_cols) and T % tm == 0
    if lb is None:
        lb = jnp.zeros((LANES,), F32)

    def resident(v):
        if v.ndim == 3:
            return pl.BlockSpec((1,) + v.shape[1:], lambda i: (layer, 0, 0), pipeline_mode=pl.Buffered(1))
        return pl.BlockSpec(v.shape, lambda i: (0, 0), pipeline_mode=pl.Buffered(1))

    return pl.pallas_call(
        functools.partial(_norm_proj_kernel, acts=tuple(o[2] for o in outs), used_cols=used_cols,
                          transposed=transposed, col_chunk=col_chunk, sub=sub),
        grid=(T // tm,),
        in_specs=[_row_spec(tm, D, 1), _whole_spec((1, D), 1), _whole_spec((1, lb.shape[0]), 1)]
                 + [resident(v) for v in ws],
        out_specs=[_row_spec(tm, n, 1) for n in widths],
        out_shape=[jax.ShapeDtypeStruct((T, n), dt) for n, dt, _ in outs],
        scratch_shapes=[pltpu.VMEM((D, n), BF16) for n in used_cols],
        compiler_params=_cparams(("arbitrary",)),
        name="norm_proj",
    )(x, g.reshape(1, D), lb.reshape(1, -1), *ws)


def _ple_update(x, p, gin_ref, wg_ref, wp_ref, gpost_ref):
    h = _rms(x, gin_ref[...]).astype(BF16)
    gate = jax.nn.sigmoid(_dot(h, wg_ref[...]))
    proj = _dot(p.astype(BF16), wp_ref[...])
    return x + _rms(gate * proj, gpost_ref[...])


def _row_spec(tm, width, n_grid):
    return pl.BlockSpec((tm, width), (lambda i: (i, 0)) if n_grid == 1 else (lambda i, j: (i, 0)))


def _whole_spec(shape, n_grid):
    zeros = (0,) * len(shape)
    return pl.BlockSpec(shape, (lambda i: zeros) if n_grid == 1 else (lambda i, j: zeros))


def _mix_ffn_ple_kernel(oa_ref, ob_ref, wa_ref, wb_ref, gmix_ref, x_ref, gpre_ref, wg_ref, wu_ref, wd_ref,
                        gpost_ref, p_ref, gin_ref, wpg_ref, wpp_ref, gple_ref, o_ref, *, sub, tf):
    tm = x_ref.shape[0]
    F = wg_ref.shape[1]
    def stages(r0):
        rows = pl.ds(r0, sub)
        mix = _dot(oa_ref[rows, :], wa_ref[...]) + _dot(ob_ref[rows, :], wb_ref[...])
        x1 = x_ref[rows, :] + _rms(mix, gmix_ref[...])
        h = _rms(x1, gpre_ref[...]).astype(BF16)
        yield
        acc = None
        for f0 in range(0, F, tf):
            f1 = min(f0 + tf, F)
            a = (_silu(_dot(h, wg_ref[:, f0:f1])) * _dot(h, wu_ref[:, f0:f1])).astype(BF16)
            part = _dot(a, wd_ref[f0:f1, :])
            acc = part if acc is None else acc + part
            yield
        x2 = x1 + _rms(acc, gpost_ref[...])
        o_ref[rows, :] = _ple_update(x2, p_ref[rows, :], gin_ref, wpg_ref, wpp_ref, gple_ref)
        yield

    _staggered([stages(r0) for r0 in range(0, tm, sub)])


def _mix_ffn_ple(o_a, o_b, w_a, w_b, g_mix, x, g_pre, wg, wu, wd, g_post, p, g_in, wpg, wpp, g_ple,
                 p_row0=0, tm=512, sub=256, tf=1536):
    T, D = x.shape
    F = wg.shape[1]
    assert T % tm == 0 and tm % sub == 0 and p_row0 % tm == 0
    p_blk0 = p_row0 // tm
    vec = lambda g: g.reshape(1, D)
    resident = lambda w: pl.BlockSpec(w.shape, lambda i: (0, 0), pipeline_mode=pl.Buffered(1))
    return pl.pallas_call(
        functools.partial(_mix_ffn_ple_kernel, sub=sub, tf=tf),
        grid=(T // tm,),
        in_specs=[_row_spec(tm, o_a.shape[1], 1), _row_spec(tm, o_b.shape[1], 1),
                  resident(w_a), resident(w_b), _whole_spec((1, D), 1),
                  _row_spec(tm, D, 1), _whole_spec((1, D), 1),
                  resident(wg), resident(wu), resident(wd),
                  _whole_spec((1, D), 1),
                  pl.BlockSpec((tm, p.shape[1]), lambda i: (i + p_blk0, 0)), _whole_spec((1, D), 1),
                  resident(wpg), resident(wpp), _whole_spec((1, D), 1)],
        out_specs=_row_spec(tm, D, 1),
        out_shape=jax.ShapeDtypeStruct((T, D), F32),
        compiler_params=_cparams(("parallel",)),
        name="mix_ffn_ple",
    )(o_a, o_b, w_a, w_b, vec(g_mix), x, vec(g_pre), wg, wu, wd, vec(g_post), p, vec(g_in), wpg, wpp,
      vec(g_ple))


def _roll_rows(x, s, rev):
    n = x.shape[0]
    return pltpu.roll(x, (n - s) if rev else s, 0)


def _cumsum_rows(x, tau, rev):
    n = x.shape[0]
    s = 1
    while s < n:
        if s % SUBLANES:
            shifted = jnp.where(tau >= s, _roll_rows(x, s, rev), 0.0)
        else:
            zeros = jnp.zeros((s,) + x.shape[1:], x.dtype)
            shifted = (jnp.concatenate([x[s:], zeros], axis=0) if rev
                       else jnp.concatenate([zeros, x[:n - s]], axis=0))
        x = x + shifted
        s *= 2
    return x


def _hgrn_levels(C):
    return [C >> (i + 1) for i in range(C.bit_length() - 1)]


def _hgrn_pair_classes(C):
    t = np.arange(C)[:, None]
    s = np.arange(C)[None, :]
    out = np.full((2 * C, 2 * C), -1, np.int32)
    for d, rev in enumerate((False, True)):
        tau, sig = (C - 1 - t, C - 1 - s) if rev else (t, s)
        blk = np.full((C, C), -1, np.int32)
        blk[t == s] = 0
        for i, L in enumerate(_hgrn_levels(C)):
            m = ((t & -(2 * L)) == (s & -(2 * L))) & ((tau & (2 * L - 1)) >= L) & ((sig & (2 * L - 1)) < L)
            blk[m] = i + 1
        out[d * C:(d + 1) * C, d * C:(d + 1) * C] = blk
    return out


def _hgrn_kernel(q_ref, gf_ref, gb_ref, i_ref, gate_ref, cls_ref, ng_ref, o_ref, acc_scr, st_scr):
    S, DK = q_ref.shape
    C = A_CHUNK
    nc = S // C
    assert nc % (2 * A_UNROLL) == 0
    ng = ng_ref[...]
    row = lax.broadcasted_iota(jnp.int32, (C, DK), 0)
    levels = _hgrn_levels(C)
    zero_half = jnp.zeros((C, DK), BF16)

    def stack(top, bottom):
        return jnp.concatenate([top, bottom], axis=0)

    def block_diag(x):
        return stack(jnp.concatenate([x[:C], zero_half], axis=1), jnp.concatenate([zero_half, x[C:]], axis=1))

    def make_decay(rev):
        tau = (C - 1 - row) if rev else row
        odd_rank = (tau & 1) == 1
        last = 0 if rev else C - 1

        def ref_rows(b, L):
            off = L if rev else L - 1
            if 2 * L >= SUBLANES:
                pieces = [jnp.broadcast_to(b[j * 2 * L + off:j * 2 * L + off + 1, :], (2 * L, DK))
                          for j in range(C // (2 * L))]
                return pieces[0] if len(pieces) == 1 else jnp.concatenate(pieces, axis=0)
            if L == 1:
                return jnp.where(odd_rank, _roll_rows(b, 1, rev), b)
            b3 = b.reshape(C // SUBLANES, SUBLANES, DK)
            sub = lax.broadcasted_iota(jnp.int32, b3.shape, 1)
            out = None
            for j in range(SUBLANES // (2 * L)):
                piece = jnp.broadcast_to(b3[:, j * 2 * L + off:j * 2 * L + off + 1, :], b3.shape)
                out = piece if out is None else jnp.where(sub >= j * 2 * L, piece, out)
            return out.reshape(C, DK)

        def decay(g):
            b = _cumsum_rows(g, tau, rev)
            return b, [ref_rows(b, L) for L in levels], b[last:last + 1, :]

        return decay

    decay_fwd, decay_bwd = make_decay(False), make_decay(True)
    pair_cls = cls_ref[...]
    st_scr[...] = jnp.zeros_like(st_scr)

    def chunk_pair(rows_f, rows_b, st):
        q = stack(q_ref[rows_f, :], q_ref[rows_b, :]).astype(F32)
        g_f, g_b = gf_ref[rows_f, :], gb_ref[rows_b, :]
        k = 1.0 - jnp.exp(stack(g_f, g_b))
        vb = stack(i_ref[rows_f, :], i_ref[rows_b, :])
        b_f, refs_f, last_f = decay_fwd(g_f)
        b_b, refs_b, last_b = decay_bwd(g_b)
        b = stack(b_f, b_b)

        o = _dot_nt(block_diag((q * jnp.exp(b)).astype(BF16)), st.astype(BF16))

        qb, kb = q.astype(BF16), k.astype(BF16)
        attn = jnp.where(pair_cls == 0, _dot_nt(qb, kb), 0.0)
        for i in range(len(levels)):
            e = jnp.exp(-jnp.abs(b - stack(refs_f[i], refs_b[i]))).astype(BF16)
            attn = jnp.where(pair_cls == i + 1, _dot_nt(qb * e, kb * e), attn)
        o = o + _dot(attn.astype(BF16), vb)

        b_last = stack(jnp.broadcast_to(last_f, (C, DK)), jnp.broadcast_to(last_b, (C, DK)))
        khat = block_diag((k * jnp.exp(b_last - b)).astype(BF16))
        keep = jnp.exp(jnp.concatenate([last_f, last_b], axis=1))
        return o, st * keep + _dot_tn(vb, khat)

    def make_body(final):
        def body(it, carry):
            st = st_scr[...]
            for u in range(A_UNROLL):
                ci = it * A_UNROLL + u
                rows_f = pl.ds(pl.multiple_of(ci * C, C), C)
                rows_b = pl.ds(pl.multiple_of((nc - 1 - ci) * C, C), C)
                o, st = chunk_pair(rows_f, rows_b, st)
                for rows, part in ((rows_f, o[:C]), (rows_b, o[C:])):
                    if final:
                        tot = acc_scr[rows, :] + part
                        o_ref[rows, :] = (_rms(tot, ng) * gate_ref[rows, :].astype(F32)).astype(o_ref.dtype)
                    else:
                        acc_scr[rows, :] = part
            st_scr[...] = st
            return carry

        return body

    trips = nc // A_UNROLL
    lax.fori_loop(0, trips // 2, make_body(False), 0)
    lax.fori_loop(trips // 2, trips, make_body(True), 0)


def _hgrn_mixer(q, g_fwd, g_bwd, v, gate, norm_g, batch, seq):
    T = q.shape[0]
    H, DK = A_HEADS, A_DK
    head = pl.BlockSpec((seq, DK), lambda b, h: (b, h))
    classes = jnp.asarray(_hgrn_pair_classes(A_CHUNK))
    return pl.pallas_call(
        _hgrn_kernel,
        grid=(batch, H),
        in_specs=[head, head, head, head, head,
                  pl.BlockSpec(classes.shape, lambda b, h: (0, 0)),
                  pl.BlockSpec((1, DK), lambda b, h: (0, 0))],
        out_specs=pl.BlockSpec((seq, DK), lambda b, h: (b, h)),
        out_shape=jax.ShapeDtypeStruct((T, H * DK), BF16),
        scratch_shapes=[pltpu.VMEM((seq, DK), F32), pltpu.VMEM((DK, 2 * DK), F32)],
        compiler_params=_cparams(("parallel", "parallel")),
        name="hgrn2",
    )(q, g_fwd, g_bwd, v, gate, classes, norm_g.reshape(1, DK))


def _expand_heads(col, j0, width):
    q = col.shape[0]
    lane = lax.broadcasted_iota(jnp.int32, (q, width), 1)
    out = jnp.broadcast_to(col[:, j0 + B_HPG - 1:j0 + B_HPG], (q, width))
    for j in range(B_HPG - 2, -1, -1):
        out = jnp.where(lane < (j + 1) * B_HEADDIM,
                        jnp.broadcast_to(col[:, j0 + j:j0 + j + 1], (q, width)), out)
    return out


def _ssd_kernel(z_ref, x_ref, bm_ref, cm_ref, dt_ref, cwx_ref, cwb_ref, cwc_ref,
                cbx_ref, cbb_ref, cbc_ref, hp_ref, dsk_ref, ng_ref, o_ref,
                xs_scr, b_scr, c_scr, y_scr, *st_scrs):
    S = x_ref.shape[0]
    Q = B_CHUNK
    nc = S // Q
    GW = B_GW
    hp = hp_ref[0]
    a_row, dtb_row = hp[0:1, :], hp[1:2, :]
    row = lax.broadcasted_iota(jnp.int32, (Q, LANES), 0)
    t2 = lax.broadcasted_iota(jnp.int32, (Q, Q), 0)
    s2 = lax.broadcasted_iota(jnp.int32, (Q, Q), 1)
    lane_gw = lax.broadcasted_iota(jnp.int32, (Q, GW), 1)
    head_mask = [(lane_gw >= j * B_HEADDIM) & (lane_gw < (j + 1) * B_HEADDIM) for j in range(B_HPG)]
    halo = SUBLANES

    def conv_body(c, carry):
        r0 = pl.multiple_of(c * Q, Q)
        rp = pl.multiple_of(jnp.maximum(r0 - halo, 0), halo)
        rn = pl.multiple_of(jnp.minimum(r0 + Q, S - halo), halo)
        has_prev = c > 0
        has_next = c < nc - 1

        def conv(src_ref, w_ref, bias_ref):
            prev = jnp.where(has_prev, src_ref[pl.ds(rp, halo), :].astype(F32), 0.0)
            nxt = jnp.where(has_next, src_ref[pl.ds(rn, halo), :].astype(F32), 0.0)
            xx = jnp.concatenate([prev, src_ref[pl.ds(r0, Q), :].astype(F32), nxt], axis=0)
            n = Q + 2 * halo
            w = w_ref[0]
            acc = None
            for j in range(B_CONV):
                delta = j - B_CONV // 2
                sh = xx if delta == 0 else pltpu.roll(xx, (-delta) % n, 0)
                term = sh[halo:halo + Q, :] * w[j:j + 1, :]
                acc = term if acc is None else acc + term
            return _silu(acc + bias_ref[0])

        xs_scr[pl.ds(r0, Q), :] = conv(x_ref, cwx_ref, cbx_ref)
        b_scr[pl.ds(r0, Q), :] = conv(bm_ref, cwb_ref, cbb_ref).astype(BF16)
        c_scr[pl.ds(r0, Q), :] = conv(cm_ref, cwc_ref, cbc_ref).astype(BF16)
        return carry

    lax.fori_loop(0, nc, conv_body, 0)

    def make_chunk(rev):
        tau = (Q - 1 - row) if rev else row
        last = 0 if rev else Q - 1
        j0 = B_HPG if rev else 0
        pair_ok = (s2 >= t2) if rev else (t2 >= s2)

        def chunk(rows, st):
            dt = jax.nn.softplus(dt_ref[rows, :] + dtb_row)
            cs = _cumsum_rows(dt * a_row, tau, rev)
            cs_t = cs.T
            xs = xs_scr[rows, :]
            bm = b_scr[rows, :]
            cm = c_scr[rows, :]
            xdt = xs * _expand_heads(dt, j0, GW)
            ecs = _expand_heads(cs, j0, GW)
            ecs_last = ecs[last:last + 1, :]

            y = _dot(cm, st.astype(BF16)) * jnp.exp(ecs)

            scores = _dot_nt(cm, bm)
            for j in range(B_HPG):
                col = jnp.broadcast_to(cs[:, j0 + j:j0 + j + 1], (Q, Q))
                rw = jnp.broadcast_to(cs_t[j0 + j:j0 + j + 1, :], (Q, Q))
                decay = jnp.where(pair_ok, jnp.exp(jnp.minimum(col - rw, 0.0)), 0.0)
                xh = jnp.where(head_mask[j], xdt, 0.0).astype(BF16)
                y = y + _dot((scores * decay).astype(BF16), xh)

            xdec = (xdt * jnp.exp(ecs_last - ecs)).astype(BF16)
            return y, st * jnp.exp(ecs_last) + _dot_tn(bm, xdec)

        return chunk

    chunk_fns = (make_chunk(False), make_chunk(True))
    for st_scr in st_scrs:
        st_scr[...] = jnp.zeros_like(st_scr)

    def make_body(final, unroll):
        def body(it, carry):
            for rev, chunk, st_scr in zip((False, True), chunk_fns, st_scrs):
                st = st_scr[...]
                for u in range(unroll):
                    ci = it * unroll + u
                    c = (nc - 1 - ci) if rev else ci
                    rows = pl.ds(pl.multiple_of(c * Q, Q), Q)
                    y, st = chunk(rows, st)
                    if final:
                        tot = y_scr[rows, :] + y + dsk_ref[0] * xs_scr[rows, :]
                        tot = tot * _silu(z_ref[rows, :].astype(F32))
                        o_ref[rows, :] = _rms(tot, ng_ref[0]).astype(o_ref.dtype)
                    else:
                        y_scr[rows, :] = y
                st_scr[...] = st
            return carry

        return body

    assert nc % (2 * B_UNROLL) == 0
    lax.fori_loop(0, nc // (2 * B_UNROLL), make_body(False, B_UNROLL), 0)
    lax.fori_loop(nc // 2, nc, make_body(True, 1), 0)


def _ssd_mixer(u, dt, conv_w, conv_b, a_log, dt_bias, d_skip, norm_g, batch, seq, col0):
    T = u.shape[0]
    G, GW, N = B_GROUPS, B_GW, B_STATE
    W = B_HEADS * B_HEADDIM
    assert col0 % GW == 0 and GW == 2 * N
    z_blk = col0 // GW
    x_blk = z_blk + W // GW
    b_blk = (col0 + 2 * W) // N
    c_blk = b_blk + G

    def pad_rows(w):
        return jnp.pad(w, ((0, SUBLANES - w.shape[0]), (0, 0)))

    cwx = pad_rows(conv_w[:, :W]).reshape(SUBLANES, G, GW).transpose(1, 0, 2)
    cwb = pad_rows(conv_w[:, W:W + G * N]).reshape(SUBLANES, G, N).transpose(1, 0, 2)
    cwc = pad_rows(conv_w[:, W + G * N:]).reshape(SUBLANES, G, N).transpose(1, 0, 2)
    cbx = conv_b[:W].reshape(G, 1, GW)
    cbb = conv_b[W:W + G * N].reshape(G, 1, N)
    cbc = conv_b[W + G * N:].reshape(G, 1, N)
    a_neg = -jnp.exp(a_log.astype(F32))
    per_dir = lambda v: jnp.concatenate([v[0].reshape(G, B_HPG), v[1].reshape(G, B_HPG)], axis=1)
    hp = jnp.stack([per_dir(a_neg), per_dir(dt_bias.astype(F32))], axis=1)
    hp = jnp.pad(hp, ((0, 0), (0, SUBLANES - 2), (0, LANES - 2 * B_HPG)))
    dsk = jnp.repeat(d_skip.astype(F32), B_HEADDIM).reshape(G, 1, GW)
    ng = norm_g.reshape(G, 1, GW)

    gspec = lambda shape: pl.BlockSpec((1,) + shape, lambda b, g: (g, 0, 0))
    return pl.pallas_call(
        _ssd_kernel,
        grid=(batch, G),
        in_specs=[pl.BlockSpec((seq, GW), lambda b, g: (b, z_blk + g)),
                  pl.BlockSpec((seq, GW), lambda b, g: (b, x_blk + g)),
                  pl.BlockSpec((seq, N), lambda b, g: (b, b_blk + g)),
                  pl.BlockSpec((seq, N), lambda b, g: (b, c_blk + g)),
                  pl.BlockSpec((seq, LANES), lambda b, g: (b, g)),
                  gspec((SUBLANES, GW)), gspec((SUBLANES, N)), gspec((SUBLANES, N)),
                  gspec((1, GW)), gspec((1, N)), gspec((1, N)),
                  gspec((SUBLANES, LANES)), gspec((1, GW)), gspec((1, GW))],
        out_specs=pl.BlockSpec((seq, GW), lambda b, g: (b, g)),
        out_shape=jax.ShapeDtypeStruct((T, W), BF16),
        scratch_shapes=[pltpu.VMEM((seq, GW), F32), pltpu.VMEM((seq, N), BF16),
                        pltpu.VMEM((seq, N), BF16), pltpu.VMEM((seq, GW), F32),
                        pltpu.VMEM((N, GW), F32), pltpu.VMEM((N, GW), F32)],
        compiler_params=_cparams(("parallel", "parallel")),
        name="ssd",
    )(u, u, u, u, dt, cwx, cwb, cwc, cbx, cbb, cbc, hp, dsk, ng)


def _na_kernel(q_ref, k_ref, v_ref, bias_ref, o_ref):
    S, HW = q_ref.shape
    W = GRID_W
    n_rows = S // W
    kh = min(NA_ROWS, n_rows)
    nk = kh * W
    n_pairs = bias_ref.shape[1] // NA_HG
    lane = lax.broadcasted_iota(jnp.int32, (W, HW), 1)
    head_masks = [(lane >= h * C_HEADDIM) & (lane < (h + 1) * C_HEADDIM) for h in range(NA_HG)]
    scale = C_HEADDIM ** -0.5

    def body(r, carry):
        rs = jnp.clip(r - kh // 2, 0, n_rows - kh)
        q = q_ref[pl.ds(pl.multiple_of(r * W, W), W), :] * jnp.asarray(scale, q_ref.dtype)
        kw = k_ref[pl.ds(pl.multiple_of(rs * W, W), nk), :]
        vw = v_ref[pl.ds(pl.multiple_of(rs * W, W), nk), :]
        zero = jnp.zeros_like(q)
        qs = jnp.concatenate([jnp.where(m, q, zero) for m in head_masks], axis=0)
        d0 = (NA_ROWS - 1) - (r - rs)
        bias = jnp.concatenate(
            [jnp.concatenate([bias_ref[0, h * n_pairs + d0 + 2 * m] for m in range(kh // 2)], axis=1)
             for h in range(NA_HG)], axis=0)
        s = _dot_nt(qs, kw) + bias
        m = jnp.max(s, axis=-1, keepdims=True)
        p = jnp.exp(s - m)
        l = jnp.sum(p, axis=-1, keepdims=True)
        res = _dot(p.astype(BF16), vw) / l
        out = jnp.zeros((W, HW), F32)
        for h in range(NA_HG):
            out = jnp.where(head_masks[h], res[h * W:(h + 1) * W, :], out)
        o_ref[pl.ds(pl.multiple_of(r * W, W), W), :] = out.astype(o_ref.dtype)
        return carry

    lax.fori_loop(0, n_rows, body, 0, unroll=NA_UNROLL)


def _na_bias_table(rpb):
    H, n_dr, n_dc = rpb.shape
    W = GRID_W
    c = np.arange(W)
    qs = np.clip(c - NA_COLS // 2, 0, W - NA_COLS)
    valid = (c[None, :] >= qs[:, None]) & (c[None, :] < qs[:, None] + NA_COLS)
    dc = np.clip(c[None, :] - c[:, None] + (NA_COLS - 1), 0, n_dc - 1)
    onehot = np.zeros((2, n_dc, W, 2 * W), np.float32)
    for half in range(2):
        onehot[half, :, :, half * W:(half + 1) * W] = dc[None] == np.arange(n_dc)[:, None, None]
    both = jnp.concatenate([rpb[:, :-1], rpb[:, 1:]], axis=2).astype(F32)
    pairs = jnp.einsum("hre,ecx->hrcx", both, jnp.asarray(onehot.reshape(2 * n_dc, W, 2 * W)),
                       precision=lax.Precision.HIGHEST)
    valid2 = np.concatenate([valid, valid], axis=1)
    pairs = jnp.where(jnp.asarray(valid2)[None, None], pairs, -jnp.inf)
    return pairs.reshape(H // NA_HG, NA_HG * (n_dr - 1), W, 2 * W)


def _neighborhood_attention(qkv, rpb, batch, seq):
    T = qkv.shape[0]
    HW = NA_HG * C_HEADDIM
    n_hg = C_HEADS // NA_HG
    n_rows = seq // GRID_W
    assert n_rows >= NA_ROWS and NA_ROWS % 2 == 0
    table = _na_bias_table(rpb)
    return pl.pallas_call(
        _na_kernel,
        grid=(batch, n_hg),
        in_specs=[pl.BlockSpec((seq, HW), lambda b, g: (b, g)),
                  pl.BlockSpec((seq, HW), lambda b, g: (b, n_hg + g)),
                  pl.BlockSpec((seq, HW), lambda b, g: (b, 2 * n_hg + g)),
                  pl.BlockSpec((1,) + table.shape[1:], lambda b, g: (g, 0, 0, 0))],
        out_specs=pl.BlockSpec((seq, HW), lambda b, g: (b, g)),
        out_shape=jax.ShapeDtypeStruct((T, C_HEADS * C_HEADDIM), BF16),
        compiler_params=_cparams(("parallel", "parallel")),
        name="natten",
    )(qkv, qkv, qkv, table)


def _store_token_tiles(ref, val, t0=0):
    count, width = val.shape
    n = width // LANES
    for j in range(n):
        ref[pl.ds(t0 * n + j, count, stride=n), :] = val[:, j * LANES:(j + 1) * LANES]


def _load_token_tiles(ref, n, t0=0, count=None):
    if count is None:
        count = ref.shape[0] // n - t0
    return jnp.concatenate([ref[pl.ds(t0 * n + j, count, stride=n), :] for j in range(n)], axis=1)


def _proj_router_kernel(a_ref, w_ref, gmix_ref, x_ref, g_ref, wr_ref, x1_ref, o_ref, h_ref):
    tm = x_ref.shape[0]
    sub = tm

    def stages(r0):
        rows = pl.ds(r0, sub)
        x1 = x_ref[rows, :] + _rms(_dot(a_ref[rows, :], w_ref[...]), gmix_ref[...])
        x1_ref[rows, :] = x1
        yield
        h = _rms(x1, g_ref[...])
        _store_token_tiles(h_ref, h, r0)
        h_hi = h.astype(BF16)
        h_lo = (h - h_hi.astype(F32)).astype(BF16)
        both = _dot(h_hi, wr_ref[...])
        logits = both[:, :LANES] + both[:, LANES:] + _dot(h_lo, wr_ref[:, :LANES])
        yield
        lane = lax.broadcasted_iota(jnp.int32, logits.shape, 1)
        neg = -jnp.inf
        l1 = jnp.where(lane < N_EXPERTS, logits, neg)
        m1 = jnp.max(l1, axis=-1, keepdims=True)
        i1 = jnp.min(jnp.where(l1 == m1, lane, LANES), axis=-1, keepdims=True)
        l2 = jnp.where(lane == i1, neg, l1)
        m2 = jnp.max(l2, axis=-1, keepdims=True)
        i2 = jnp.min(jnp.where(l2 == m2, lane, LANES), axis=-1, keepdims=True)
        e = jnp.exp(m2 - m1)
        w1 = 1.0 / (1.0 + e)
        w2 = e / (1.0 + e)
        o_ref[rows, :] = jnp.where(lane == 0, i1.astype(F32),
                                   jnp.where(lane == 1, i2.astype(F32),
                                             jnp.where(lane == 2, w1, jnp.where(lane == 3, w2, 0.0))))
        yield

    _staggered([stages(r0) for r0 in range(0, tm, sub)])


def _proj_router(a, w, g_mix, x, g, w_router, tm=512):
    T, D = x.shape
    n = D // LANES
    wr = jnp.pad(w_router.astype(F32), ((0, 0), (0, LANES - w_router.shape[1])))
    wr_hi = wr.astype(BF16)
    wr = jnp.concatenate([wr_hi, (wr - wr_hi.astype(F32)).astype(BF16)], axis=1)
    x1, route, h = pl.pallas_call(
        _proj_router_kernel,
        grid=(T // tm,),
        in_specs=[_row_spec(tm, a.shape[1], 1), _whole_spec(w.shape, 1), _whole_spec((1, D), 1),
                  _row_spec(tm, D, 1), _whole_spec((1, D), 1), _whole_spec((D, 2 * LANES), 1)],
        out_specs=[_row_spec(tm, D, 1), _row_spec(tm, LANES, 1), _row_spec(tm * n, LANES, 1)],
        out_shape=[jax.ShapeDtypeStruct((T, D), F32),
                   jax.ShapeDtypeStruct((T, LANES), F32),
                   jax.ShapeDtypeStruct((T * n, LANES), F32)],
        compiler_params=_cparams(("parallel",)),
        name="proj_router",
    )(a, w, g_mix.reshape(1, D), x, g.reshape(1, D), wr)
    return x1, route, h.reshape(T, n, LANES)


SC_CORES = 2
SC_SUBCORES = 16
SC_WORKERS = SC_CORES * SC_SUBCORES
SC_CHUNK = 32


def _sc_mesh():
    return plsc.VectorSubcoreMesh(core_axis_name="c", subcore_axis_name="s",
                                  num_cores=SC_CORES, num_subcores=SC_SUBCORES)


def _sc_index_blocks(idx):
    return idx.reshape(SC_WORKERS, -1, SC_CHUNK)


def _sc_scatter_tokens(h, dest, n_rows, row0=0):
    T = dest.shape[0]
    _, n, _ = h.shape
    per_worker = T // SC_WORKERS
    n_chunks = per_worker // SC_CHUNK
    assert per_worker * SC_WORKERS == T and n_chunks * SC_CHUNK == per_worker

    @functools.partial(
        pl.kernel, mesh=_sc_mesh(),
        out_type=jax.ShapeDtypeStruct((n_rows, n, LANES), h.dtype),
        scratch_types=[pltpu.VMEM((n_chunks, SC_CHUNK), jnp.int32),
                       pltpu.VMEM((n_chunks, SC_CHUNK), jnp.int32),
                       pltpu.VMEM((SC_CHUNK, n, LANES), h.dtype)],
        name="sc_scatter_tokens",
    )
    def scatter(h_hbm, d0_hbm, d1_hbm, o_hbm, i0_v, i1_v, rows_v):
        wid = lax.axis_index("s") * SC_CORES + lax.axis_index("c")
        pltpu.sync_copy(d0_hbm.at[wid], i0_v)
        pltpu.sync_copy(d1_hbm.at[wid], i1_v)

        @pl.loop(0, n_chunks)
        def _(j):
            pltpu.sync_copy(h_hbm.at[pl.ds(row0 + wid * per_worker + j * SC_CHUNK, SC_CHUNK)], rows_v)
            pltpu.sync_copy(rows_v, o_hbm.at[i0_v.at[j]])
            pltpu.sync_copy(rows_v, o_hbm.at[i1_v.at[j]])

    return scatter(h, _sc_index_blocks(dest[:, 0]), _sc_index_blocks(dest[:, 1]))


def _sc_gather_tokens(y, dest):
    T = dest.shape[0]
    _, n, _ = y.shape
    per_worker = T // SC_WORKERS
    n_chunks = per_worker // SC_CHUNK
    assert per_worker * SC_WORKERS == T and n_chunks * SC_CHUNK == per_worker
    out = jax.ShapeDtypeStruct((T, n, LANES), y.dtype)

    @functools.partial(
        pl.kernel, mesh=_sc_mesh(), out_type=(out, out),
        scratch_types=[pltpu.VMEM((n_chunks, SC_CHUNK), jnp.int32),
                       pltpu.VMEM((n_chunks, SC_CHUNK), jnp.int32),
                       pltpu.VMEM((SC_CHUNK, n, LANES), y.dtype)],
        name="sc_gather_tokens",
    )
    def gather(y_hbm, d0_hbm, d1_hbm, o0_hbm, o1_hbm, i0_v, i1_v, rows_v):
        wid = lax.axis_index("s") * SC_CORES + lax.axis_index("c")
        pltpu.sync_copy(d0_hbm.at[wid], i0_v)
        pltpu.sync_copy(d1_hbm.at[wid], i1_v)

        @pl.loop(0, n_chunks)
        def _(j):
            rows = pl.ds(wid * per_worker + j * SC_CHUNK, SC_CHUNK)
            pltpu.sync_copy(y_hbm.at[i0_v.at[j]], rows_v)
            pltpu.sync_copy(rows_v, o0_hbm.at[rows])
            pltpu.sync_copy(y_hbm.at[i1_v.at[j]], rows_v)
            pltpu.sync_copy(rows_v, o1_hbm.at[rows])

    return gather(y, _sc_index_blocks(dest[:, 0]), _sc_index_blocks(dest[:, 1]))


SC_PACK_PAIRS = 8
SC_LANES = 16


def _sc_pack_bf16_rows(w):
    R, C = w.shape
    pairs = R // 2
    per_worker = pairs // SC_WORKERS
    n_chunks = per_worker // SC_PACK_PAIRS
    assert n_chunks * SC_PACK_PAIRS * SC_WORKERS * 2 == R and C % SC_LANES == 0

    def round_bits(v):
        u = plsc.bitcast(v, jnp.int32)
        return u + 0x7FFF + (lax.shift_right_logical(u, 16) & 1)

    @functools.partial(
        pl.kernel, mesh=_sc_mesh(),
        out_type=jax.ShapeDtypeStruct((pairs, C), jnp.int32),
        scratch_types=[pltpu.VMEM((2 * SC_PACK_PAIRS, C), F32), pltpu.VMEM((SC_PACK_PAIRS, C), jnp.int32)],
        compiler_params=pltpu.CompilerParams(use_tc_tiling_on_sc=True, needs_layout_passes=False),
        name="sc_pack_bf16_rows",
    )
    def pack(w_hbm, o_hbm, in_v, out_v):
        wid = lax.axis_index("s") * SC_CORES + lax.axis_index("c")

        @pl.loop(0, n_chunks)
        def _(j):
            p0 = (wid * n_chunks + j) * SC_PACK_PAIRS
            pltpu.sync_copy(w_hbm.at[pl.ds(2 * p0, 2 * SC_PACK_PAIRS)], in_v)

            @pl.loop(0, C, step=SC_LANES)
            def _(c):
                cols = pl.ds(c, SC_LANES)
                for i in range(SC_PACK_PAIRS):
                    lo = lax.shift_right_logical(round_bits(in_v[2 * i, cols]), 16)
                    hi = round_bits(in_v[2 * i + 1, cols]) & jnp.int32(-65536)
                    out_v[i, cols] = lo | hi

            pltpu.sync_copy(out_v, o_hbm.at[pl.ds(p0, SC_PACK_PAIRS)])

    return pack(w)


def _moe_ffn_kernel(te_ref, na_ref, h_ref, wg_ref, wu_ref, wd_ref, o_ref, acc_scr, *, tf):
    i = pl.program_id(0)
    D = acc_scr.shape[1]
    F = wg_ref.shape[2]
    unpack = lambda packed: pltpu.bitcast(packed, BF16)

    @pl.when(i < na_ref[0])
    def _():
        h = _load_token_tiles(h_ref, D // LANES).astype(BF16)
        for f0 in range(0, F, tf):
            a = (_silu(_dot(h, unpack(wg_ref[0, :, f0:f0 + tf])))
                 * _dot(h, unpack(wu_ref[0, :, f0:f0 + tf]))).astype(BF16)
            part = _dot(a, unpack(wd_ref[0, f0 // 2:(f0 + tf) // 2, :]))
            if f0 == 0:
                acc_scr[...] = part
            else:
                acc_scr[...] += part
        _store_token_tiles(o_ref, acc_scr[...])

    @pl.when(i >= na_ref[0])
    def _():
        o_ref[...] = jnp.zeros_like(o_ref)


def _moe_ffn(hs, wg, wu, wd, tile_expert, n_active, tm=MOE_TM, tf=512):
    rows, n, _ = hs.shape
    E, half_d, F = wg.shape
    D = 2 * half_d
    nt = tile_expert.shape[0]
    assert rows == nt * tm and n * LANES == D and wd.shape == (E, F // 2, D)
    grid_spec = pltpu.PrefetchScalarGridSpec(
        num_scalar_prefetch=2,
        grid=(nt,),
        in_specs=[pl.BlockSpec((tm * n, LANES), lambda i, te, na: (i, 0)),
                  pl.BlockSpec((1, D // 2, F), lambda i, te, na: (te[i], 0, 0)),
                  pl.BlockSpec((1, D // 2, F), lambda i, te, na: (te[i], 0, 0)),
                  pl.BlockSpec((1, F // 2, D), lambda i, te, na: (te[i], 0, 0))],
        out_specs=pl.BlockSpec((tm * n, LANES), lambda i, te, na: (i, 0)),
        scratch_shapes=[pltpu.VMEM((tm, D), F32)],
    )
    y = pl.pallas_call(
        functools.partial(_moe_ffn_kernel, tf=tf),
        grid_spec=grid_spec,
        out_shape=jax.ShapeDtypeStruct((rows * n, LANES), F32),
        compiler_params=_cparams(("arbitrary",)),
        name="moe_ffn",
    )(tile_expert, n_active, hs.reshape(rows * n, LANES), wg, wu, wd)
    return y.reshape(rows, n, LANES)


def _combine_ple_kernel(y0_ref, y1_ref, route_ref, x_ref, g_ref, p_ref, gin_ref, wpg_ref, wpp_ref,
                        gple_ref, *rest):
    o_ref = rest[-1]
    tm, D = x_ref.shape
    n = D // LANES
    sub = min(tm, 256)

    def stages(r0):
        rows = pl.ds(r0, sub)
        route = route_ref[rows, :]
        moe = (route[:, 2:3] * _load_token_tiles(y0_ref, n, r0, sub)
               + route[:, 3:4] * _load_token_tiles(y1_ref, n, r0, sub))
        x2 = x_ref[rows, :] + _rms(moe, g_ref[...])
        yield
        o_ref[rows, :] = _ple_update(x2, p_ref[rows, :], gin_ref, wpg_ref, wpp_ref, gple_ref)
        yield

    _staggered([stages(r0) for r0 in range(0, tm, sub)])


def _combine_ple(y0, y1, route, x, g, p, g_in, wpg, wpp, g_ple, row0=0, prev=None, p_row0=0, tm=512):
    T, D = x.shape
    Tp, n, _ = y0.shape
    assert row0 % tm == 0 and Tp % tm == 0 and n * LANES == D and p_row0 % tm == 0
    blk0 = row0 // tm
    p_blk0 = (row0 + p_row0) // tm
    vec = lambda v: v.reshape(1, D)
    shifted = lambda width: pl.BlockSpec((tm, width), lambda i: (i + blk0, 0))
    in_specs = [_row_spec(tm * n, LANES, 1), _row_spec(tm * n, LANES, 1), shifted(LANES),
                shifted(D), _whole_spec((1, D), 1),
                pl.BlockSpec((tm, p.shape[1]), lambda i: (i + p_blk0, 0)), _whole_spec((1, D), 1),
                _whole_spec(wpg.shape, 1), _whole_spec(wpp.shape, 1), _whole_spec((1, D), 1)]
    args = [y0.reshape(Tp * n, LANES), y1.reshape(Tp * n, LANES), route, x, vec(g), p, vec(g_in), wpg, wpp,
            vec(g_ple)]
    aliases = {}
    if prev is not None:
        in_specs.append(pl.BlockSpec(memory_space=pl.ANY))
        args.append(prev)
        aliases = {len(args) - 1: 0}
    return pl.pallas_call(
        _combine_ple_kernel,
        grid=(Tp // tm,),
        in_specs=in_specs,
        out_specs=shifted(D),
        out_shape=jax.ShapeDtypeStruct((T, D), F32),
        input_output_aliases=aliases,
        compiler_params=_cparams(("parallel",)),
        name="combine_ple",
    )(*args)


def _moe_plan(route, tm=MOE_TM):
    T = route.shape[0]
    e = route[:, :TOP_K].astype(jnp.int32).reshape(-1)
    onehot = (e[:, None] == jnp.arange(N_EXPERTS, dtype=jnp.int32)[None, :]).astype(jnp.int32)
    csum = jnp.cumsum(onehot, axis=0)
    rank = jnp.sum(csum * onehot, axis=1) - 1
    counts = csum[-1]
    padded = ((counts + tm - 1) // tm) * tm
    ends = jnp.cumsum(padded)
    starts = ends - padded
    dest = jnp.sum(starts[None, :] * onehot, axis=1) + rank
    nt = (T * TOP_K) // tm + N_EXPERTS
    tile_start = jnp.arange(nt, dtype=jnp.int32) * tm
    tile_expert = jnp.minimum(jnp.sum((tile_start[:, None] >= ends[None, :]).astype(jnp.int32), axis=1),
                              N_EXPERTS - 1).astype(jnp.int32)
    n_active = (ends[-1] // tm).astype(jnp.int32).reshape(1)
    return tile_expert, n_active, dest.reshape(T, TOP_K).astype(jnp.int32)


def kernel(x, p, hgrn_lb_raw, e_norm_mix_pre, e_w_in, e_conv_w, e_conv_b, e_A_log, e_dt_bias, e_D, e_a_norm, e_b_norm, e_w_out, e_norm_mix_post, e_norm_ffn_pre, e_w_ffn_gate, e_w_ffn_up, e_w_ffn_down, e_norm_ffn_post, o_norm_mix_pre, o_w_qkv, o_rpb, o_w_out, o_norm_mix_post, o_norm_ffn_pre, o_w_router, o_w_exp_gate, o_w_exp_up, o_w_exp_down, o_norm_ffn_post, ple_norm_in, ple_w_gate, ple_w_proj, ple_norm_post):
    batch, seq, d_model = x.shape
    depth = p.shape[0]
    T = batch * seq
    xt = x.reshape(T, d_model)
    p_all = p.reshape(depth * T, -1)
    lb_all =jnp.cumsum(jax.nn.softmax(hgrn_lb_raw.astype(F32), axis=0), axis=0)

    a_kdim = A_HEADS * A_DK
    b_width = B_HEADS * B_HEADDIM
    conv_dim = b_width + 2 * B_GROUPS * B_STATE
    main_w = 5 * a_kdim + b_width + conv_dim

    def pack_experts(w):
        E, R, C = w.shape
        return _sc_pack_bf16_rows(w.reshape(E * R, C)).reshape(E, R // 2, C)

    for li in range(depth):
        j = li // 2
        ple =(ple_norm_in[li], ple_w_gate[li].astype(BF16), ple_w_proj[li].astype(BF16), ple_norm_post[li])
        if li % 2 == 0:
            w_in = e_w_in[j]
            dtf = w_in[:, main_w:main_w + B_HEADS].reshape(d_model, B_GROUPS, B_HPG)
            dtb = w_in[:, main_w + B_HEADS:].reshape(d_model, B_GROUPS, B_HPG)
            w_dt = jnp.pad(jnp.concatenate([dtf, dtb], axis=2), ((0, 0), (0, 0), (0, LANES - 2 * B_HPG)))
            w_pieces = [(jnp.swapaxes(e_w_in, 1, 2), main_w, True),
                        (w_dt.reshape(d_model, B_GROUPS * LANES), B_GROUPS * LANES)]
            outs = [(a_kdim, BF16, "silu"), (a_kdim, F32, "log_gate"), (a_kdim, F32, "log_gate"),
                    (a_kdim, BF16, None), (a_kdim, BF16, "silu"),
                    (b_width + conv_dim, BF16, None), (B_GROUPS * LANES, F32, None)]
            q_a, gf_a, gb_a, i_a, gate_a, u_b, dt = _norm_proj(
                xt, e_norm_mix_pre[j], w_pieces, outs, lb=lb_all[li], layer=j)
            o_a = _hgrn_mixer(q_a, gf_a, gb_a, i_a, gate_a, e_a_norm[j], batch, seq)
            o_b = _ssd_mixer(u_b, dt, e_conv_w[j], e_conv_b[j], e_A_log[j], e_dt_bias[j], e_D[j],
                             e_b_norm[j], batch, seq, 0)
            w_out = e_w_out[j].astype(BF16)
            xt = _mix_ffn_ple(o_a, o_b, w_out[:a_kdim], w_out[a_kdim:], e_norm_mix_post[j], xt,
                              e_norm_ffn_pre[j], e_w_ffn_gate[j].astype(BF16), e_w_ffn_up[j].astype(BF16),
                              e_w_ffn_down[j].astype(BF16), e_norm_ffn_post[j], p_all, *ple, p_row0=li * T)
        else:
            (qkv,) = _norm_proj(xt, o_norm_mix_pre[j], o_w_qkv,
                                [(3 * C_HEADS * C_HEADDIM, BF16, None)], layer=j)
            o_c = _neighborhood_attention(qkv, o_rpb[j], batch, seq)
            xt, route, h = _proj_router(o_c, o_w_out[j].astype(BF16), o_norm_mix_post[j], xt,
                                        o_norm_ffn_pre[j], o_w_router[j])
            expert_w = [pack_experts(w[j]) for w in (o_w_exp_gate, o_w_exp_up, o_w_exp_down)]
            x_mid, xt = xt, None
            t_grp = T // MOE_GROUPS
            for s in range(MOE_GROUPS):
                tile_expert, n_active, dest = _moe_plan(route[s * t_grp:(s + 1) * t_grp])
                hs = _sc_scatter_tokens(h, dest, tile_expert.shape[0] * MOE_TM, row0=s * t_grp)
                ys = _moe_ffn(hs, *expert_w, tile_expert, n_active)
                y0, y1 = _sc_gather_tokens(ys, dest)
                xt = _combine_ple(y0, y1, route, x_mid, o_norm_ffn_post[j], p_all, *ple,
                                  row0=s * t_grp, prev=xt, p_row0=li * T)
    return xt.reshape(batch, seq, d_model)
```

```python
import functools
import math

import numpy as np
import jax
import jax.numpy as jnp
from jax import lax
from jax.experimental import pallas as pl
from jax.experimental.pallas import tpu as pltpu
from jax.experimental.pallas import tpu_sc as plsc

F32 = jnp.float32
BF16 = jnp.bfloat16
EPS = 1e-6

LANES = 128
SUBLANES = 8
VMEM_LIMIT_BYTES = 56 * 1024 * 1024

GRID_W = 64
A_HEADS, A_DK, A_CHUNK = 4, 128, 64
A_UNROLL = 32
B_HEADS, B_HEADDIM, B_GROUPS, B_STATE, B_CONV, B_CHUNK = 8, 64, 2, 128, 5, 128
B_UNROLL = 2
B_HPG = B_HEADS // B_GROUPS
B_GW = B_HPG * B_HEADDIM
C_HEADS, C_HEADDIM = 16, 64
NA_ROWS, NA_COLS = 8, 16
NA_HG = 4
NA_UNROLL = 32
N_EXPERTS, TOP_K = 8, 2
MOE_GROUPS = 2
MOE_TM = 512


def _cparams(sem):
    return pltpu.CompilerParams(dimension_semantics=sem, vmem_limit_bytes=VMEM_LIMIT_BYTES)


def _rms(x, g):
    return x * lax.rsqrt(jnp.mean(x * x, axis=-1, keepdims=True) + EPS) * g


def _silu(x):
    return x * jax.nn.sigmoid(x)


def _dot(a, b):
    return jnp.dot(a, b, preferred_element_type=F32)


def _dot_nt(a, b):
    return lax.dot_general(a, b, (((1,), (1,)), ((), ())), preferred_element_type=F32)


def _dot_tn(a, b):
    return lax.dot_general(a, b, (((0,), (0,)), ((), ())), preferred_element_type=F32)


def _staggered(stage_fns):
    live = list(stage_fns)
    step = 0
    done = [False] * len(live)
    while not all(done):
        for k, gen in enumerate(live):
            if step >= k and not done[k]:
                try:
                    next(gen)
                except StopIteration:
                    done[k] = True
        step += 1


def _norm_proj_kernel(x_ref, g_ref, lb_ref, *refs, acts, used_cols, transposed, col_chunk, sub):
    n_w = len(used_cols)
    w_src = refs[:n_w]
    o_refs = refs[n_w:n_w + len(acts)]
    w_refs = refs[n_w + len(acts):]

    @pl.when(pl.program_id(0) == 0)
    def _():
        for src, dst, used, flipped in zip(w_src, w_refs, used_cols, transposed):
            for c0 in range(0, used, col_chunk):
                c1 = min(c0 + col_chunk, used)
                if len(src.shape) == 2:
                    cols = src[:, c0:c1]
                elif flipped:
                    cols = src[0, c0:c1, :].T
                else:
                    cols = src[0, :, c0:c1]
                dst[:, c0:c1] = cols.astype(BF16)

    def stages(r0):
        rows = pl.ds(r0, sub)
        h = _rms(x_ref[rows, :], g_ref[...]).astype(BF16)
        yield
        piece, off = 0, 0
        for o_ref, act in zip(o_refs, acts):
            n = o_ref.shape[1]
            if off == used_cols[piece]:
                piece, off = piece + 1, 0
            w_ref = w_refs[piece]
            for c0 in range(0, n, col_chunk):
                c1 = min(c0 + col_chunk, n)
                y = _dot(h, w_ref[:, off + c0:off + c1])
                if act == "silu":
                    y = _silu(y)
                elif act == "log_gate":
                    lb = lb_ref[:, c0:c1]
                    y = jnp.log(lb + (1.0 - lb) * jax.nn.sigmoid(y))
                o_ref[rows, c0:c1] = y.astype(o_ref.dtype)
                yield
            off += n

    _staggered([stages(r0) for r0 in range(0, x_ref.shape[0], sub)])


def _norm_proj(x, g, w, outs, lb=None, layer=0, tm=512, sub=256, col_chunk=1024):
    T, D = x.shape
    pieces = list(w) if isinstance(w, (list, tuple)) else [(w, w.shape[-1])]
    ws = [pc[0] for pc in pieces]
    used_cols = tuple(pc[1] for pc in pieces)
    transposed = tuple(len(pc) > 2 and pc[2] for pc in pieces)
    widths = [o[0] for o in outs]
    assert sum(widths) == sum(used_cols) and T % tm == 0
    if lb is None:
        lb = jnp.zeros((LANES,), F32)

    def resident(v):
        if v.ndim == 3:
            return pl.BlockSpec((1,) + v.shape[1:], lambda i: (layer, 0, 0), pipeline_mode=pl.Buffered(1))
        return pl.BlockSpec(v.shape, lambda i: (0, 0), pipeline_mode=pl.Buffered(1))

    return pl.pallas_call(
        functools.partial(_norm_proj_kernel, acts=tuple(o[2] for o in outs), used_cols=used_cols,
                          transposed=transposed, col_chunk=col_chunk, sub=sub),
        grid=(T // tm,),
        in_specs=[_row_spec(tm, D, 1), _whole_spec((1, D), 1), _whole_spec((1, lb.shape[0]), 1)]
                 + [resident(v) for v in ws],
        out_specs=[_row_spec(tm, n, 1) for n in widths],
        out_shape=[jax.ShapeDtypeStruct((T, n), dt) for n, dt, _ in outs],
        scratch_shapes=[pltpu.VMEM((D, n), BF16) for n in used_cols],
        compiler_params=_cparams(("arbitrary",)),
        name="norm_proj",
    )(x, g.reshape(1, D), lb.reshape(1, -1), *ws)


def _ple_update(x, p, gin_ref, wg_ref, wp_ref, gpost_ref):
    h = _rms(x, gin_ref[...]).astype(BF16)
    gate = jax.nn.sigmoid(_dot(h, wg_ref[...]))
    proj = _dot(p.astype(BF16), wp_ref[...])
    return x + _rms(gate * proj, gpost_ref[...])


def _row_spec(tm, width, n_grid):
    return pl.BlockSpec((tm, width), (lambda i: (i, 0)) if n_grid == 1 else (lambda i, j: (i, 0)))


def _whole_spec(shape, n_grid):
    zeros = (0,) * len(shape)
    return pl.BlockSpec(shape, (lambda i: zeros) if n_grid == 1 else (lambda i, j: zeros))


def _mix_ffn_ple_kernel(oa_ref, ob_ref, wa_ref, wb_ref, gmix_ref, x_ref, gpre_ref, wg_ref, wu_ref, wd_ref,
                        gpost_ref, p_ref, gin_ref, wpg_ref, wpp_ref, gple_ref, o_ref, *, sub, tf):
    tm = x_ref.shape[0]
    F = wg_ref.shape[1]
    def stages(r0):
        rows = pl.ds(r0, sub)
        mix = _dot(oa_ref[rows, :], wa_ref[...]) + _dot(ob_ref[rows, :], wb_ref[...])
        x1 = x_ref[rows, :] + _rms(mix, gmix_ref[...])
        h = _rms(x1, gpre_ref[...]).astype(BF16)
        yield
        acc = None
        for f0 in range(0, F, tf):
            f1 = min(f0 + tf, F)
            a = (_silu(_dot(h, wg_ref[:, f0:f1])) * _dot(h, wu_ref[:, f0:f1])).astype(BF16)
            part = _dot(a, wd_ref[f0:f1, :])
            acc = part if acc is None else acc + part
            yield
        x2 = x1 + _rms(acc, gpost_ref[...])
        o_ref[rows, :] = _ple_update(x2, p_ref[rows, :], gin_ref, wpg_ref, wpp_ref, gple_ref)
        yield

    _staggered([stages(r0) for r0 in range(0, tm, sub)])


def _mix_ffn_ple(o_a, o_b, w_a, w_b, g_mix, x, g_pre, wg, wu, wd, g_post, p, g_in, wpg, wpp, g_ple,
                 p_row0=0, tm=512, sub=256, tf=1536):
    T, D = x.shape
    F = wg.shape[1]
    assert T % tm == 0 and tm % sub == 0 and p_row0 % tm == 0
    p_blk0 = p_row0 // tm
    vec = lambda g: g.reshape(1, D)
    resident = lambda w: pl.BlockSpec(w.shape, lambda i: (0, 0), pipeline_mode=pl.Buffered(1))
    return pl.pallas_call(
        functools.partial(_mix_ffn_ple_kernel, sub=sub, tf=tf),
        grid=(T // tm,),
        in_specs=[_row_spec(tm, o_a.shape[1], 1), _row_spec(tm, o_b.shape[1], 1),
                  resident(w_a), resident(w_b), _whole_spec((1, D), 1),
                  _row_spec(tm, D, 1), _whole_spec((1, D), 1),
                  resident(wg), resident(wu), resident(wd),
                  _whole_spec((1, D), 1),
                  pl.BlockSpec((tm, p.shape[1]), lambda i: (i + p_blk0, 0)), _whole_spec((1, D), 1),
                  resident(wpg), resident(wpp), _whole_spec((1, D), 1)],
        out_specs=_row_spec(tm, D, 1),
        out_shape=jax.ShapeDtypeStruct((T, D), F32),
        compiler_params=_cparams(("parallel",)),
        name="mix_ffn_ple",
    )(o_a, o_b, w_a, w_b, vec(g_mix), x, vec(g_pre), wg, wu, wd, vec(g_post), p, vec(g_in), wpg, wpp,
      vec(g_ple))


def _roll_rows(x, s, rev):
    n = x.shape[0]
    return pltpu.roll(x, (n - s) if rev else s, 0)


def _cumsum_rows(x, tau, rev):
    n = x.shape[0]
    s = 1
    while s < n:
        if s % SUBLANES:
            shifted = jnp.where(tau >= s, _roll_rows(x, s, rev), 0.0)
        else:
            zeros = jnp.zeros((s,) + x.shape[1:], x.dtype)
            shifted = (jnp.concatenate([x[s:], zeros], axis=0) if rev
                       else jnp.concatenate([zeros, x[:n - s]], axis=0))
        x = x + shifted
        s *= 2
    return x


def _hgrn_levels(C):
    return [C >> (i + 1) for i in range(C.bit_length() - 1)]


def _hgrn_pair_classes(C):
    t = np.arange(C)[:, None]
    s = np.arange(C)[None, :]
    out = np.full((2 * C, 2 * C), -1, np.int32)
    for d, rev in enumerate((False, True)):
        tau, sig = (C - 1 - t, C - 1 - s) if rev else (t, s)
        blk = np.full((C, C), -1, np.int32)
        blk[t == s] = 0
        for i, L in enumerate(_hgrn_levels(C)):
            m = ((t & -(2 * L)) == (s & -(2 * L))) & ((tau & (2 * L - 1)) >= L) & ((sig & (2 * L - 1)) < L)
            blk[m] = i + 1
        out[d * C:(d + 1) * C, d * C:(d + 1) * C] = blk
    return out


def _hgrn_kernel(q_ref, gf_ref, gb_ref, i_ref, gate_ref, cls_ref, ng_ref, o_ref, acc_scr, st_scr):
    S, DK = q_ref.shape
    C = A_CHUNK
    nc = S // C
    assert nc % (2 * A_UNROLL) == 0
    ng = ng_ref[...]
    row = lax.broadcasted_iota(jnp.int32, (C, DK), 0)
    levels = _hgrn_levels(C)
    zero_half = jnp.zeros((C, DK), BF16)

    def stack(top, bottom):
        return jnp.concatenate([top, bottom], axis=0)

    def block_diag(x):
        return stack(jnp.concatenate([x[:C], zero_half], axis=1), jnp.concatenate([zero_half, x[C:]], axis=1))

    def make_decay(rev):
        tau = (C - 1 - row) if rev else row
        odd_rank = (tau & 1) == 1
        last = 0 if rev else C - 1

        def ref_rows(b, L):
            off = L if rev else L - 1
            if 2 * L >= SUBLANES:
                pieces = [jnp.broadcast_to(b[j * 2 * L + off:j * 2 * L + off + 1, :], (2 * L, DK))
                          for j in range(C // (2 * L))]
                return pieces[0] if len(pieces) == 1 else jnp.concatenate(pieces, axis=0)
            if L == 1:
                return jnp.where(odd_rank, _roll_rows(b, 1, rev), b)
            b3 = b.reshape(C // SUBLANES, SUBLANES, DK)
            sub = lax.broadcasted_iota(jnp.int32, b3.shape, 1)
            out = None
            for j in range(SUBLANES // (2 * L)):
                piece = jnp.broadcast_to(b3[:, j * 2 * L + off:j * 2 * L + off + 1, :], b3.shape)
                out = piece if out is None else jnp.where(sub >= j * 2 * L, piece, out)
            return out.reshape(C, DK)

        def decay(g):
            b = _cumsum_rows(g, tau, rev)
            return b, [ref_rows(b, L) for L in levels], b[last:last + 1, :]

        return decay

    decay_fwd, decay_bwd = make_decay(False), make_decay(True)
    pair_cls = cls_ref[...]
    st_scr[...] = jnp.zeros_like(st_scr)

    def chunk_pair(rows_f, rows_b, st):
        q = stack(q_ref[rows_f, :], q_ref[rows_b, :]).astype(F32)
        g_f, g_b = gf_ref[rows_f, :], gb_ref[rows_b, :]
        k = 1.0 - jnp.exp(stack(g_f, g_b))
        vb = stack(i_ref[rows_f, :], i_ref[rows_b, :])
        b_f, refs_f, last_f = decay_fwd(g_f)
        b_b, refs_b, last_b = decay_bwd(g_b)
        b = stack(b_f, b_b)

        o = _dot_nt(block_diag((q * jnp.exp(b)).astype(BF16)), st.astype(BF16))

        qb, kb = q.astype(BF16), k.astype(BF16)
        attn = jnp.where(pair_cls == 0, _dot_nt(qb, kb), 0.0)
        for i in range(len(levels)):
            e = jnp.exp(-jnp.abs(b - stack(refs_f[i], refs_b[i]))).astype(BF16)
            attn = jnp.where(pair_cls == i + 1, _dot_nt(qb * e, kb * e), attn)
        o = o + _dot(attn.astype(BF16), vb)

        b_last = stack(jnp.broadcast_to(last_f, (C, DK)), jnp.broadcast_to(last_b, (C, DK)))
        khat = block_diag((k * jnp.exp(b_last - b)).astype(BF16))
        keep = jnp.exp(jnp.concatenate([last_f, last_b], axis=1))
        return o, st * keep + _dot_tn(vb, khat)

    def make_body(final):
        def body(it, carry):
            st = st_scr[...]
            for u in range(A_UNROLL):
                ci = it * A_UNROLL + u
                rows_f = pl.ds(pl.multiple_of(ci * C, C), C)
                rows_b = pl.ds(pl.multiple_of((nc - 1 - ci) * C, C), C)
                o, st = chunk_pair(rows_f, rows_b, st)
                for rows, part in ((rows_f, o[:C]), (rows_b, o[C:])):
                    if final:
                        tot = acc_scr[rows, :] + part
                        o_ref[rows, :] = (_rms(tot, ng) * gate_ref[rows, :].astype(F32)).astype(o_ref.dtype)
                    else:
                        acc_scr[rows, :] = part
            st_scr[...] = st
            return carry

        return body

    trips = nc // A_UNROLL
    lax.fori_loop(0, trips // 2, make_body(False), 0)
    lax.fori_loop(trips // 2, trips, make_body(True), 0)


def _hgrn_mixer(q, g_fwd, g_bwd, v, gate, norm_g, batch, seq):
    T = q.shape[0]
    H, DK = A_HEADS, A_DK
    head = pl.BlockSpec((seq, DK), lambda b, h: (b, h))
    classes = jnp.asarray(_hgrn_pair_classes(A_CHUNK))
    return pl.pallas_call(
        _hgrn_kernel,
        grid=(batch, H),
        in_specs=[head, head, head, head, head,
                  pl.BlockSpec(classes.shape, lambda b, h: (0, 0)),
                  pl.BlockSpec((1, DK), lambda b, h: (0, 0))],
        out_specs=pl.BlockSpec((seq, DK), lambda b, h: (b, h)),
        out_shape=jax.ShapeDtypeStruct((T, H * DK), BF16),
        scratch_shapes=[pltpu.VMEM((seq, DK), F32), pltpu.VMEM((DK, 2 * DK), F32)],
        compiler_params=_cparams(("parallel", "parallel")),
        name="hgrn2",
    )(q, g_fwd, g_bwd, v, gate, classes, norm_g.reshape(1, DK))


def _expand_heads(col, j0, width):
    q = col.shape[0]
    lane = lax.broadcasted_iota(jnp.int32, (q, width), 1)
    out = jnp.broadcast_to(col[:, j0 + B_HPG - 1:j0 + B_HPG], (q, width))
    for j in range(B_HPG - 2, -1, -1):
        out = jnp.where(lane < (j + 1) * B_HEADDIM,
                        jnp.broadcast_to(col[:, j0 + j:j0 + j + 1], (q, width)), out)
    return out


def _ssd_kernel(z_ref, x_ref, bm_ref, cm_ref, dt_ref, cwx_ref, cwb_ref, cwc_ref,
                cbx_ref, cbb_ref, cbc_ref, hp_ref, dsk_ref, ng_ref, o_ref,
                xs_scr, b_scr, c_scr, y_scr, *st_scrs):
    S = x_ref.shape[0]
    Q = B_CHUNK
    nc = S // Q
    GW = B_GW
    hp = hp_ref[0]
    a_row, dtb_row = hp[0:1, :], hp[1:2, :]
    row = lax.broadcasted_iota(jnp.int32, (Q, LANES), 0)
    t2 = lax.broadcasted_iota(jnp.int32, (Q, Q), 0)
    s2 = lax.broadcasted_iota(jnp.int32, (Q, Q), 1)
    lane_gw = lax.broadcasted_iota(jnp.int32, (Q, GW), 1)
    head_mask = [(lane_gw >= j * B_HEADDIM) & (lane_gw < (j + 1) * B_HEADDIM) for j in range(B_HPG)]
    halo = SUBLANES

    def conv_body(c, carry):
        r0 = pl.multiple_of(c * Q, Q)
        rp = pl.multiple_of(jnp.maximum(r0 - halo, 0), halo)
        rn = pl.multiple_of(jnp.minimum(r0 + Q, S - halo), halo)
        has_prev = c > 0
        has_next = c < nc - 1

        def conv(src_ref, w_ref, bias_ref):
            prev = jnp.where(has_prev, src_ref[pl.ds(rp, halo), :].astype(F32), 0.0)
            nxt = jnp.where(has_next, src_ref[pl.ds(rn, halo), :].astype(F32), 0.0)
            xx = jnp.concatenate([prev, src_ref[pl.ds(r0, Q), :].astype(F32), nxt], axis=0)
            n = Q + 2 * halo
            w = w_ref[0]
            acc = None
            for j in range(B_CONV):
                delta = j - B_CONV // 2
                sh = xx if delta == 0 else pltpu.roll(xx, (-delta) % n, 0)
                term = sh[halo:halo + Q, :] * w[j:j + 1, :]
                acc = term if acc is None else acc + term
            return _silu(acc + bias_ref[0])

        xs_scr[pl.ds(r0, Q), :] = conv(x_ref, cwx_ref, cbx_ref)
        b_scr[pl.ds(r0, Q), :] = conv(bm_ref, cwb_ref, cbb_ref).astype(BF16)
        c_scr[pl.ds(r0, Q), :] = conv(cm_ref, cwc_ref, cbc_ref).astype(BF16)
        return carry

    lax.fori_loop(0, nc, conv_body, 0)

    def make_chunk(rev):
        tau = (Q - 1 - row) if rev else row
        last = 0 if rev else Q - 1
        j0 = B_HPG if rev else 0
        pair_ok = (s2 >= t2) if rev else (t2 >= s2)

        def chunk(rows, st):
            dt = jax.nn.softplus(dt_ref[rows, :] + dtb_row)
            cs = _cumsum_rows(dt * a_row, tau, rev)
            cs_t = cs.T
            xs = xs_scr[rows, :]
            bm = b_scr[rows, :]
            cm = c_scr[rows, :]
            xdt = xs * _expand_heads(dt, j0, GW)
            ecs = _expand_heads(cs, j0, GW)
            ecs_last = ecs[last:last + 1, :]

            y = _dot(cm, st.astype(BF16)) * jnp.exp(ecs)

            scores = _dot_nt(cm, bm)
            for j in range(B_HPG):
                col = jnp.broadcast_to(cs[:, j0 + j:j0 + j + 1], (Q, Q))
                rw = jnp.broadcast_to(cs_t[j0 + j:j0 + j + 1, :], (Q, Q))
                decay = jnp.where(pair_ok, jnp.exp(jnp.minimum(col - rw, 0.0)), 0.0)
                xh = jnp.where(head_mask[j], xdt, 0.0).astype(BF16)
                y = y + _dot((scores * decay).astype(BF16), xh)

            xdec = (xdt * jnp.exp(ecs_last - ecs)).astype(BF16)
            return y, st * jnp.exp(ecs_last) + _dot_tn(bm, xdec)

        return chunk

    chunk_fns = (make_chunk(False), make_chunk(True))
    for st_scr in st_scrs:
        st_scr[...] = jnp.zeros_like(st_scr)

    def make_body(second_visit):
        def body(it, carry):
            for rev, chunk, st_scr in zip((False, True), chunk_fns, st_scrs):
                st = st_scr[...]
                for u in range(B_UNROLL):
                    ci = it * B_UNROLL + u
                    c = (nc - 1 - ci) if rev else ci
                    rows = pl.ds(pl.multiple_of(c * Q, Q), Q)
                    y, st = chunk(rows, st)
                    y_scr[rows, :] = (y_scr[rows, :] + y) if second_visit else y
                st_scr[...] = st
            return carry

        return body

    assert nc % (2 * B_UNROLL) == 0
    trips = nc // B_UNROLL
    lax.fori_loop(0, trips // 2, make_body(False), 0)
    lax.fori_loop(trips // 2, trips, make_body(True), 0)

    def finish(c, carry):
        rows = pl.ds(pl.multiple_of(c * Q, Q), Q)
        tot = y_scr[rows, :] + dsk_ref[0] * xs_scr[rows, :]
        tot = tot * _silu(z_ref[rows, :].astype(F32))
        o_ref[rows, :] = _rms(tot, ng_ref[0]).astype(o_ref.dtype)
        return carry

    lax.fori_loop(0, nc, finish, 0, unroll=4)


def _ssd_mixer(u, dt, conv_w, conv_b, a_log, dt_bias, d_skip, norm_g, batch, seq, col0):
    T = u.shape[0]
    G, GW, N = B_GROUPS, B_GW, B_STATE
    W = B_HEADS * B_HEADDIM
    assert col0 % GW == 0 and GW == 2 * N
    z_blk = col0 // GW
    x_blk = z_blk + W // GW
    b_blk = (col0 + 2 * W) // N
    c_blk = b_blk + G

    def pad_rows(w):
        return jnp.pad(w, ((0, SUBLANES - w.shape[0]), (0, 0)))

    cwx = pad_rows(conv_w[:, :W]).reshape(SUBLANES, G, GW).transpose(1, 0, 2)
    cwb = pad_rows(conv_w[:, W:W + G * N]).reshape(SUBLANES, G, N).transpose(1, 0, 2)
    cwc = pad_rows(conv_w[:, W + G * N:]).reshape(SUBLANES, G, N).transpose(1, 0, 2)
    cbx = conv_b[:W].reshape(G, 1, GW)
    cbb = conv_b[W:W + G * N].reshape(G, 1, N)
    cbc = conv_b[W + G * N:].reshape(G, 1, N)
    a_neg = -jnp.exp(a_log.astype(F32))
    per_dir = lambda v: jnp.concatenate([v[0].reshape(G, B_HPG), v[1].reshape(G, B_HPG)], axis=1)
    hp = jnp.stack([per_dir(a_neg), per_dir(dt_bias.astype(F32))], axis=1)
    hp = jnp.pad(hp, ((0, 0), (0, SUBLANES - 2), (0, LANES - 2 * B_HPG)))
    dsk = jnp.repeat(d_skip.astype(F32), B_HEADDIM).reshape(G, 1, GW)
    ng = norm_g.reshape(G, 1, GW)

    gspec = lambda shape: pl.BlockSpec((1,) + shape, lambda b, g: (g, 0, 0))
    return pl.pallas_call(
        _ssd_kernel,
        grid=(batch, G),
        in_specs=[pl.BlockSpec((seq, GW), lambda b, g: (b, z_blk + g)),
                  pl.BlockSpec((seq, GW), lambda b, g: (b, x_blk + g)),
                  pl.BlockSpec((seq, N), lambda b, g: (b, b_blk + g)),
                  pl.BlockSpec((seq, N), lambda b, g: (b, c_blk + g)),
                  pl.BlockSpec((seq, LANES), lambda b, g: (b, g)),
                  gspec((SUBLANES, GW)), gspec((SUBLANES, N)), gspec((SUBLANES, N)),
                  gspec((1, GW)), gspec((1, N)), gspec((1, N)),
                  gspec((SUBLANES, LANES)), gspec((1, GW)), gspec((1, GW))],
        out_specs=pl.BlockSpec((seq, GW), lambda b, g: (b, g)),
        out_shape=jax.ShapeDtypeStruct((T, W), BF16),
        scratch_shapes=[pltpu.VMEM((seq, GW), F32), pltpu.VMEM((seq, N), BF16),
                        pltpu.VMEM((seq, N), BF16), pltpu.VMEM((seq, GW), F32),
                        pltpu.VMEM((N, GW), F32), pltpu.VMEM((N, GW), F32)],
        compiler_params=_cparams(("parallel", "parallel")),
        name="ssd",
    )(u, u, u, u, dt, cwx, cwb, cwc, cbx, cbb, cbc, hp, dsk, ng)


def _na_kernel(q_ref, k_ref, v_ref, bias_ref, o_ref):
    S, HW = q_ref.shape
    W = GRID_W
    n_rows = S // W
    kh = min(NA_ROWS, n_rows)
    nk = kh * W
    n_pairs = bias_ref.shape[1] // NA_HG
    lane = lax.broadcasted_iota(jnp.int32, (W, HW), 1)
    head_masks = [(lane >= h * C_HEADDIM) & (lane < (h + 1) * C_HEADDIM) for h in range(NA_HG)]
    scale = C_HEADDIM ** -0.5

    def body(r, carry):
        rs = jnp.clip(r - kh // 2, 0, n_rows - kh)
        q = q_ref[pl.ds(pl.multiple_of(r * W, W), W), :] * jnp.asarray(scale, q_ref.dtype)
        kw = k_ref[pl.ds(pl.multiple_of(rs * W, W), nk), :]
        vw = v_ref[pl.ds(pl.multiple_of(rs * W, W), nk), :]
        zero = jnp.zeros_like(q)
        qs = jnp.concatenate([jnp.where(m, q, zero) for m in head_masks], axis=0)
        d0 = (NA_ROWS - 1) - (r - rs)
        bias = jnp.concatenate(
            [jnp.concatenate([bias_ref[0, h * n_pairs + d0 + 2 * m] for m in range(kh // 2)], axis=1)
             for h in range(NA_HG)], axis=0)
        s = _dot_nt(qs, kw) + bias
        m = jnp.max(s, axis=-1, keepdims=True)
        p = jnp.exp(s - m)
        l = jnp.sum(p, axis=-1, keepdims=True)
        res = _dot(p.astype(BF16), vw) / l
        out = jnp.zeros((W, HW), F32)
        for h in range(NA_HG):
            out = jnp.where(head_masks[h], res[h * W:(h + 1) * W, :], out)
        o_ref[pl.ds(pl.multiple_of(r * W, W), W), :] = out.astype(o_ref.dtype)
        return carry

    lax.fori_loop(0, n_rows, body, 0, unroll=NA_UNROLL)


def _na_bias_table(rpb):
    H, n_dr, n_dc = rpb.shape
    W = GRID_W
    c = np.arange(W)
    qs = np.clip(c - NA_COLS // 2, 0, W - NA_COLS)
    valid = (c[None, :] >= qs[:, None]) & (c[None, :] < qs[:, None] + NA_COLS)
    dc = np.clip(c[None, :] - c[:, None] + (NA_COLS - 1), 0, n_dc - 1)
    onehot = np.zeros((2, n_dc, W, 2 * W), np.float32)
    for half in range(2):
        onehot[half, :, :, half * W:(half + 1) * W] = dc[None] == np.arange(n_dc)[:, None, None]
    both = jnp.concatenate([rpb[:, :-1], rpb[:, 1:]], axis=2).astype(F32)
    pairs = jnp.einsum("hre,ecx->hrcx", both, jnp.asarray(onehot.reshape(2 * n_dc, W, 2 * W)),
                       precision=lax.Precision.HIGHEST)
    valid2 = np.concatenate([valid, valid], axis=1)
    pairs = jnp.where(jnp.asarray(valid2)[None, None], pairs, -jnp.inf)
    return pairs.reshape(H // NA_HG, NA_HG * (n_dr - 1), W, 2 * W)


def _neighborhood_attention(qkv, rpb, batch, seq):
    T = qkv.shape[0]
    HW = NA_HG * C_HEADDIM
    n_hg = C_HEADS // NA_HG
    n_rows = seq // GRID_W
    assert n_rows >= NA_ROWS and NA_ROWS % 2 == 0
    table = _na_bias_table(rpb)
    return pl.pallas_call(
        _na_kernel,
        grid=(batch, n_hg),
        in_specs=[pl.BlockSpec((seq, HW), lambda b, g: (b, g)),
                  pl.BlockSpec((seq, HW), lambda b, g: (b, n_hg + g)),
                  pl.BlockSpec((seq, HW), lambda b, g: (b, 2 * n_hg + g)),
                  pl.BlockSpec((1,) + table.shape[1:], lambda b, g: (g, 0, 0, 0))],
        out_specs=pl.BlockSpec((seq, HW), lambda b, g: (b, g)),
        out_shape=jax.ShapeDtypeStruct((T, C_HEADS * C_HEADDIM), BF16),
        compiler_params=_cparams(("parallel", "parallel")),
        name="natten",
    )(qkv, qkv, qkv, table)


def _store_token_tiles(ref, val, t0=0):
    count, width = val.shape
    n = width // LANES
    for j in range(n):
        ref[pl.ds(t0 * n + j, count, stride=n), :] = val[:, j * LANES:(j + 1) * LANES]


def _load_token_tiles(ref, n, t0=0, count=None):
    if count is None:
        count = ref.shape[0] // n - t0
    return jnp.concatenate([ref[pl.ds(t0 * n + j, count, stride=n), :] for j in range(n)], axis=1)


def _proj_router_kernel(a_ref, w_ref, gmix_ref, x_ref, g_ref, wr_ref, x1_ref, o_ref, h_ref):
    tm = x_ref.shape[0]
    sub = tm

    def stages(r0):
        rows = pl.ds(r0, sub)
        x1 = x_ref[rows, :] + _rms(_dot(a_ref[rows, :], w_ref[...]), gmix_ref[...])
        x1_ref[rows, :] = x1
        yield
        h = _rms(x1, g_ref[...])
        _store_token_tiles(h_ref, h, r0)
        h_hi = h.astype(BF16)
        h_lo = (h - h_hi.astype(F32)).astype(BF16)
        both = _dot(h_hi, wr_ref[...])
        logits = both[:, :LANES] + both[:, LANES:] + _dot(h_lo, wr_ref[:, :LANES])
        yield
        lane = lax.broadcasted_iota(jnp.int32, logits.shape, 1)
        neg = -jnp.inf
        l1 = jnp.where(lane < N_EXPERTS, logits, neg)
        m1 = jnp.max(l1, axis=-1, keepdims=True)
        i1 = jnp.min(jnp.where(l1 == m1, lane, LANES), axis=-1, keepdims=True)
        l2 = jnp.where(lane == i1, neg, l1)
        m2 = jnp.max(l2, axis=-1, keepdims=True)
        i2 = jnp.min(jnp.where(l2 == m2, lane, LANES), axis=-1, keepdims=True)
        e = jnp.exp(m2 - m1)
        w1 = 1.0 / (1.0 + e)
        w2 = e / (1.0 + e)
        o_ref[rows, :] = jnp.where(lane == 0, i1.astype(F32),
                                   jnp.where(lane == 1, i2.astype(F32),
                                             jnp.where(lane == 2, w1, jnp.where(lane == 3, w2, 0.0))))
        yield

    _staggered([stages(r0) for r0 in range(0, tm, sub)])


def _proj_router(a, w, g_mix, x, g, w_router, tm=512):
    T, D = x.shape
    n = D // LANES
    wr = jnp.pad(w_router.astype(F32), ((0, 0), (0, LANES - w_router.shape[1])))
    wr_hi = wr.astype(BF16)
    wr = jnp.concatenate([wr_hi, (wr - wr_hi.astype(F32)).astype(BF16)], axis=1)
    x1, route, h = pl.pallas_call(
        _proj_router_kernel,
        grid=(T // tm,),
        in_specs=[_row_spec(tm, a.shape[1], 1), _whole_spec(w.shape, 1), _whole_spec((1, D), 1),
                  _row_spec(tm, D, 1), _whole_spec((1, D), 1), _whole_spec((D, 2 * LANES), 1)],
        out_specs=[_row_spec(tm, D, 1), _row_spec(tm, LANES, 1), _row_spec(tm * n, LANES, 1)],
        out_shape=[jax.ShapeDtypeStruct((T, D), F32),
                   jax.ShapeDtypeStruct((T, LANES), F32),
                   jax.ShapeDtypeStruct((T * n, LANES), F32)],
        compiler_params=_cparams(("parallel",)),
        name="proj_router",
    )(a, w, g_mix.reshape(1, D), x, g.reshape(1, D), wr)
    return x1, route, h.reshape(T, n, LANES)


SC_CORES = 2
SC_SUBCORES = 16
SC_WORKERS = SC_CORES * SC_SUBCORES
SC_CHUNK = 32


def _sc_mesh():
    return plsc.VectorSubcoreMesh(core_axis_name="c", subcore_axis_name="s",
                                  num_cores=SC_CORES, num_subcores=SC_SUBCORES)


def _sc_index_blocks(idx):
    return idx.reshape(SC_WORKERS, -1, SC_CHUNK)


def _sc_scatter_tokens(h, dest, n_rows, row0=0):
    T = dest.shape[0]
    _, n, _ = h.shape
    per_worker = T // SC_WORKERS
    n_chunks = per_worker // SC_CHUNK
    assert per_worker * SC_WORKERS == T and n_chunks * SC_CHUNK == per_worker

    @functools.partial(
        pl.kernel, mesh=_sc_mesh(),
        out_type=jax.ShapeDtypeStruct((n_rows, n, LANES), h.dtype),
        scratch_types=[pltpu.VMEM((n_chunks, SC_CHUNK), jnp.int32),
                       pltpu.VMEM((n_chunks, SC_CHUNK), jnp.int32),
                       pltpu.VMEM((SC_CHUNK, n, LANES), h.dtype)],
        name="sc_scatter_tokens",
    )
    def scatter(h_hbm, d0_hbm, d1_hbm, o_hbm, i0_v, i1_v, rows_v):
        wid = lax.axis_index("s") * SC_CORES + lax.axis_index("c")
        pltpu.sync_copy(d0_hbm.at[wid], i0_v)
        pltpu.sync_copy(d1_hbm.at[wid], i1_v)

        @pl.loop(0, n_chunks)
        def _(j):
            pltpu.sync_copy(h_hbm.at[pl.ds(row0 + wid * per_worker + j * SC_CHUNK, SC_CHUNK)], rows_v)
            pltpu.sync_copy(rows_v, o_hbm.at[i0_v.at[j]])
            pltpu.sync_copy(rows_v, o_hbm.at[i1_v.at[j]])

    return scatter(h, _sc_index_blocks(dest[:, 0]), _sc_index_blocks(dest[:, 1]))


def _sc_gather_tokens(y, dest):
    T = dest.shape[0]
    _, n, _ = y.shape
    per_worker = T // SC_WORKERS
    n_chunks = per_worker // SC_CHUNK
    assert per_worker * SC_WORKERS == T and n_chunks * SC_CHUNK == per_worker
    out = jax.ShapeDtypeStruct((T, n, LANES), y.dtype)

    @functools.partial(
        pl.kernel, mesh=_sc_mesh(), out_type=(out, out),
        scratch_types=[pltpu.VMEM((n_chunks, SC_CHUNK), jnp.int32),
                       pltpu.VMEM((n_chunks, SC_CHUNK), jnp.int32),
                       pltpu.VMEM((SC_CHUNK, n, LANES), y.dtype)],
        name="sc_gather_tokens",
    )
    def gather(y_hbm, d0_hbm, d1_hbm, o0_hbm, o1_hbm, i0_v, i1_v, rows_v):
        wid = lax.axis_index("s") * SC_CORES + lax.axis_index("c")
        pltpu.sync_copy(d0_hbm.at[wid], i0_v)
        pltpu.sync_copy(d1_hbm.at[wid], i1_v)

        @pl.loop(0, n_chunks)
        def _(j):
            rows = pl.ds(wid * per_worker + j * SC_CHUNK, SC_CHUNK)
            pltpu.sync_copy(y_hbm.at[i0_v.at[j]], rows_v)
            pltpu.sync_copy(rows_v, o0_hbm.at[rows])
            pltpu.sync_copy(y_hbm.at[i1_v.at[j]], rows_v)
            pltpu.sync_copy(rows_v, o1_hbm.at[rows])

    return gather(y, _sc_index_blocks(dest[:, 0]), _sc_index_blocks(dest[:, 1]))


SC_PACK_PAIRS = 8
SC_LANES = 16


def _sc_pack_bf16_rows(w):
    R, C = w.shape
    pairs = R // 2
    per_worker = pairs // SC_WORKERS
    n_chunks = per_worker // SC_PACK_PAIRS
    assert n_chunks * SC_PACK_PAIRS * SC_WORKERS * 2 == R and C % SC_LANES == 0

    def round_bits(v):
        u = plsc.bitcast(v, jnp.int32)
        return u + 0x7FFF + (lax.shift_right_logical(u, 16) & 1)

    @functools.partial(
        pl.kernel, mesh=_sc_mesh(),
        out_type=jax.ShapeDtypeStruct((pairs, C), jnp.int32),
        scratch_types=[pltpu.VMEM((2 * SC_PACK_PAIRS, C), F32), pltpu.VMEM((SC_PACK_PAIRS, C), jnp.int32)],
        compiler_params=pltpu.CompilerParams(use_tc_tiling_on_sc=True, needs_layout_passes=False),
        name="sc_pack_bf16_rows",
    )
    def pack(w_hbm, o_hbm, in_v, out_v):
        wid = lax.axis_index("s") * SC_CORES + lax.axis_index("c")

        @pl.loop(0, n_chunks)
        def _(j):
            p0 = (wid * n_chunks + j) * SC_PACK_PAIRS
            pltpu.sync_copy(w_hbm.at[pl.ds(2 * p0, 2 * SC_PACK_PAIRS)], in_v)

            @pl.loop(0, C, step=SC_LANES)
            def _(c):
                cols = pl.ds(c, SC_LANES)
                for i in range(SC_PACK_PAIRS):
                    lo = lax.shift_right_logical(round_bits(in_v[2 * i, cols]), 16)
                    hi = round_bits(in_v[2 * i + 1, cols]) & jnp.int32(-65536)
                    out_v[i, cols] = lo | hi

            pltpu.sync_copy(out_v, o_hbm.at[pl.ds(p0, SC_PACK_PAIRS)])

    return pack(w)


def _moe_ffn_kernel(te_ref, na_ref, h_ref, wg_ref, wu_ref, wd_ref, o_ref, acc_scr, *, tf):
    i = pl.program_id(0)
    D = acc_scr.shape[1]
    F = wg_ref.shape[2]
    unpack = lambda packed: pltpu.bitcast(packed, BF16)

    @pl.when(i < na_ref[0])
    def _():
        h = _load_token_tiles(h_ref, D // LANES).astype(BF16)
        for f0 in range(0, F, tf):
            a = (_silu(_dot(h, unpack(wg_ref[0, :, f0:f0 + tf])))
                 * _dot(h, unpack(wu_ref[0, :, f0:f0 + tf]))).astype(BF16)
            part = _dot(a, unpack(wd_ref[0, f0 // 2:(f0 + tf) // 2, :]))
            if f0 == 0:
                acc_scr[...] = part
            else:
                acc_scr[...] += part
        _store_token_tiles(o_ref, acc_scr[...])

    @pl.when(i >= na_ref[0])
    def _():
        o_ref[...] = jnp.zeros_like(o_ref)


def _moe_ffn(hs, wg, wu, wd, tile_expert, n_active, tm=MOE_TM, tf=512):
    rows, n, _ = hs.shape
    E, half_d, F = wg.shape
    D = 2 * half_d
    nt = tile_expert.shape[0]
    assert rows == nt * tm and n * LANES == D and wd.shape == (E, F // 2, D)
    grid_spec = pltpu.PrefetchScalarGridSpec(
        num_scalar_prefetch=2,
        grid=(nt,),
        in_specs=[pl.BlockSpec((tm * n, LANES), lambda i, te, na: (i, 0)),
                  pl.BlockSpec((1, D // 2, F), lambda i, te, na: (te[i], 0, 0)),
                  pl.BlockSpec((1, D // 2, F), lambda i, te, na: (te[i], 0, 0)),
                  pl.BlockSpec((1, F // 2, D), lambda i, te, na: (te[i], 0, 0))],
        out_specs=pl.BlockSpec((tm * n, LANES), lambda i, te, na: (i, 0)),
        scratch_shapes=[pltpu.VMEM((tm, D), F32)],
    )
    y = pl.pallas_call(
        functools.partial(_moe_ffn_kernel, tf=tf),
        grid_spec=grid_spec,
        out_shape=jax.ShapeDtypeStruct((rows * n, LANES), F32),
        compiler_params=_cparams(("arbitrary",)),
        name="moe_ffn",
    )(tile_expert, n_active, hs.reshape(rows * n, LANES), wg, wu, wd)
    return y.reshape(rows, n, LANES)


def _combine_ple_kernel(y0_ref, y1_ref, route_ref, x_ref, g_ref, p_ref, gin_ref, wpg_ref, wpp_ref,
                        gple_ref, *rest):
    o_ref = rest[-1]
    tm, D = x_ref.shape
    n = D // LANES
    sub = min(tm, 256)

    def stages(r0):
        rows = pl.ds(r0, sub)
        route = route_ref[rows, :]
        moe = (route[:, 2:3] * _load_token_tiles(y0_ref, n, r0, sub)
               + route[:, 3:4] * _load_token_tiles(y1_ref, n, r0, sub))
        x2 = x_ref[rows, :] + _rms(moe, g_ref[...])
        yield
        o_ref[rows, :] = _ple_update(x2, p_ref[rows, :], gin_ref, wpg_ref, wpp_ref, gple_ref)
        yield

    _staggered([stages(r0) for r0 in range(0, tm, sub)])


def _combine_ple(y0, y1, route, x, g, p, g_in, wpg, wpp, g_ple, row0=0, prev=None, p_row0=0, tm=512):
    T, D = x.shape
    Tp, n, _ = y0.shape
    assert row0 % tm == 0 and Tp % tm == 0 and n * LANES == D and p_row0 % tm == 0
    blk0 = row0 // tm
    p_blk0 = (row0 + p_row0) // tm
    vec = lambda v: v.reshape(1, D)
    shifted = lambda width: pl.BlockSpec((tm, width), lambda i: (i + blk0, 0))
    in_specs = [_row_spec(tm * n, LANES, 1), _row_spec(tm * n, LANES, 1), shifted(LANES),
                shifted(D), _whole_spec((1, D), 1),
                pl.BlockSpec((tm, p.shape[1]), lambda i: (i + p_blk0, 0)), _whole_spec((1, D), 1),
                _whole_spec(wpg.shape, 1), _whole_spec(wpp.shape, 1), _whole_spec((1, D), 1)]
    args = [y0.reshape(Tp * n, LANES), y1.reshape(Tp * n, LANES), route, x, vec(g), p, vec(g_in), wpg, wpp,
            vec(g_ple)]
    aliases = {}
    if prev is not None:
        in_specs.append(pl.BlockSpec(memory_space=pl.ANY))
        args.append(prev)
        aliases = {len(args) - 1: 0}
    return pl.pallas_call(
        _combine_ple_kernel,
        grid=(Tp // tm,),
        in_specs=in_specs,
        out_specs=shifted(D),
        out_shape=jax.ShapeDtypeStruct((T, D), F32),
        input_output_aliases=aliases,
        compiler_params=_cparams(("parallel",)),
        name="combine_ple",
    )(*args)


def _moe_plan(route, tm=MOE_TM):
    T = route.shape[0]
    e = route[:, :TOP_K].astype(jnp.int32).reshape(-1)
    onehot = (e[:, None] == jnp.arange(N_EXPERTS, dtype=jnp.int32)[None, :]).astype(jnp.int32)
    csum = jnp.cumsum(onehot, axis=0)
    rank = jnp.sum(csum * onehot, axis=1) - 1
    counts = csum[-1]
    padded = ((counts + tm - 1) // tm) * tm
    ends = jnp.cumsum(padded)
    starts = ends - padded
    dest = jnp.sum(starts[None, :] * onehot, axis=1) + rank
    nt = (T * TOP_K) // tm + N_EXPERTS
    tile_start = jnp.arange(nt, dtype=jnp.int32) * tm
    tile_expert = jnp.minimum(jnp.sum((tile_start[:, None] >= ends[None, :]).astype(jnp.int32), axis=1),
                              N_EXPERTS - 1).astype(jnp.int32)
    n_active = (ends[-1] // tm).astype(jnp.int32).reshape(1)
    return tile_expert, n_active, dest.reshape(T, TOP_K).astype(jnp.int32)


def kernel(x, p, hgrn_lb_raw, e_norm_mix_pre, e_w_in, e_conv_w, e_conv_b, e_A_log, e_dt_bias, e_D, e_a_norm, e_b_norm, e_w_out, e_norm_mix_post, e_norm_ffn_pre, e_w_ffn_gate, e_w_ffn_up, e_w_ffn_down, e_norm_ffn_post, o_norm_mix_pre, o_w_qkv, o_rpb, o_w_out, o_norm_mix_post, o_norm_ffn_pre, o_w_router, o_w_exp_gate, o_w_exp_up, o_w_exp_down, o_norm_ffn_post, ple_norm_in, ple_w_gate, ple_w_proj, ple_norm_post):
    batch, seq, d_model = x.shape
    depth = p.shape[0]
    T = batch * seq
    xt = x.reshape(T, d_model)
    p_all = p.reshape(depth * T, -1)
    lb_all =jnp.cumsum(jax.nn.softmax(hgrn_lb_raw.astype(F32), axis=0), axis=0)

    a_kdim = A_HEADS * A_DK
    b_width = B_HEADS * B_HEADDIM
    conv_dim = b_width + 2 * B_GROUPS * B_STATE
    main_w = 5 * a_kdim + b_width + conv_dim

    def pack_experts(w):
        E, R, C = w.shape
        return _sc_pack_bf16_rows(w.reshape(E * R, C)).reshape(E, R // 2, C)

    for li in range(depth):
        j = li // 2
        ple =(ple_norm_in[li], ple_w_gate[li].astype(BF16), ple_w_proj[li].astype(BF16), ple_norm_post[li])
        if li % 2 == 0:
            w_in = e_w_in[j]
            dtf = w_in[:, main_w:main_w + B_HEADS].reshape(d_model, B_GROUPS, B_HPG)
            dtb = w_in[:, main_w + B_HEADS:].reshape(d_model, B_GROUPS, B_HPG)
            w_dt = jnp.pad(jnp.concatenate([dtf, dtb], axis=2), ((0, 0), (0, 0), (0, LANES - 2 * B_HPG)))
            w_pieces = [(jnp.swapaxes(e_w_in, 1, 2), main_w, True),
                        (w_dt.reshape(d_model, B_GROUPS * LANES), B_GROUPS * LANES)]
            outs = [(a_kdim, BF16, "silu"), (a_kdim, F32, "log_gate"), (a_kdim, F32, "log_gate"),
                    (a_kdim, BF16, None), (a_kdim, BF16, "silu"),
                    (b_width + conv_dim, BF16, None), (B_GROUPS * LANES, F32, None)]
            q_a, gf_a, gb_a, i_a, gate_a, u_b, dt = _norm_proj(
                xt, e_norm_mix_pre[j], w_pieces, outs, lb=lb_all[li], layer=j)
            o_a = _hgrn_mixer(q_a, gf_a, gb_a, i_a, gate_a, e_a_norm[j], batch, seq)
            o_b = _ssd_mixer(u_b, dt, e_conv_w[j], e_conv_b[j], e_A_log[j], e_dt_bias[j], e_D[j],
                             e_b_norm[j], batch, seq, 0)
            w_out = e_w_out[j].astype(BF16)
            xt = _mix_ffn_ple(o_a, o_b, w_out[:a_kdim], w_out[a_kdim:], e_norm_mix_post[j], xt,
                              e_norm_ffn_pre[j], e_w_ffn_gate[j].astype(BF16), e_w_ffn_up[j].astype(BF16),
                              e_w_ffn_down[j].astype(BF16), e_norm_ffn_post[j], p_all, *ple, p_row0=li * T)
        else:
            (qkv,) = _norm_proj(xt, o_norm_mix_pre[j], o_w_qkv,
                                [(3 * C_HEADS * C_HEADDIM, BF16, None)], layer=j)
            o_c = _neighborhood_attention(qkv, o_rpb[j], batch, seq)
            xt, route, h = _proj_router(o_c, o_w_out[j].astype(BF16), o_norm_mix_post[j], xt,
                                        o_norm_ffn_pre[j], o_w_router[j])
            expert_w = [pack_experts(w[j]) for w in (o_w_exp_gate, o_w_exp_up, o_w_exp_down)]
            x_mid, xt = xt, None
            t_grp = T // MOE_GROUPS
            for s in range(MOE_GROUPS):
                tile_expert, n_active, dest = _moe_plan(route[s * t_grp:(s + 1) * t_grp])
                hs = _sc_scatter_tokens(h, dest, tile_expert.shape[0] * MOE_TM, row0=s * t_grp)
                ys = _moe_ffn(hs, *expert_w, tile_expert, n_active)
                y0, y1 = _sc_gather_tokens(ys, dest)
                xt = _combine_ple(y0, y1, route, x_mid, o_norm_ffn_post[j], p_all, *ple,
                                  row0=s * t_grp, prev=xt, p_row0=li * T)
    return xt.reshape(batch, seq, d_model)
```

```python
import functools
import math

import numpy as np
import jax
import jax.numpy as jnp
from jax import lax
from jax.experimental import pallas as pl
from jax.experimental.pallas import tpu as pltpu
from jax.experimental.pallas import tpu_sc as plsc

F32 = jnp.float32
BF16 = jnp.bfloat16
EPS = 1e-6

LANES = 128
SUBLANES = 8
VMEM_LIMIT_BYTES = 56 * 1024 * 1024

GRID_W = 64
A_HEADS, A_DK, A_CHUNK = 4, 128, 64
A_UNROLL = 32
B_HEADS, B_HEADDIM, B_GROUPS, B_STATE, B_CONV, B_CHUNK = 8, 64, 2, 128, 5, 128
B_UNROLL = 2
B_HPG = B_HEADS // B_GROUPS
B_GW = B_HPG * B_HEADDIM
C_HEADS, C_HEADDIM = 16, 64
NA_ROWS, NA_COLS = 8, 16
NA_HG = 4
NA_UNROLL = 32
N_EXPERTS, TOP_K = 8, 2
MOE_GROUPS = 2
MOE_TM = 512


def _cparams(sem):
    return pltpu.CompilerParams(dimension_semantics=sem, vmem_limit_bytes=VMEM_LIMIT_BYTES)


def _rms(x, g):
    return x * lax.rsqrt(jnp.mean(x * x, axis=-1, keepdims=True) + EPS) * g


def _silu(x):
    return x * jax.nn.sigmoid(x)


def _dot(a, b):
    return jnp.dot(a, b, preferred_element_type=F32)


def _dot_nt(a, b):
    return lax.dot_general(a, b, (((1,), (1,)), ((), ())), preferred_element_type=F32)


def _dot_tn(a, b):
    return lax.dot_general(a, b, (((0,), (0,)), ((), ())), preferred_element_type=F32)


def _staggered(stage_fns):
    live = list(stage_fns)
    step = 0
    done = [False] * len(live)
    while not all(done):
        for k, gen in enumerate(live):
            if step >= k and not done[k]:
                try:
                    next(gen)
                except StopIteration:
                    done[k] = True
        step += 1


def _norm_proj_kernel(x_ref, g_ref, lb_ref, *refs, acts, used_cols, transposed, col_chunk, sub):
    n_w = len(used_cols)
    w_src = refs[:n_w]
    o_refs = refs[n_w:n_w + len(acts)]
    w_refs = refs[n_w + len(acts):]

    @pl.when(pl.program_id(0) == 0)
    def _():
        for src, dst, used, flipped in zip(w_src, w_refs, used_cols, transposed):
            for c0 in range(0, used, col_chunk):
                c1 = min(c0 + col_chunk, used)
                if len(src.shape) == 2:
                    cols = src[:, c0:c1]
                elif flipped:
                    cols = src[0, c0:c1, :].T
                else:
                    cols = src[0, :, c0:c1]
                dst[:, c0:c1] = cols.astype(BF16)

    def stages(r0):
        rows = pl.ds(r0, sub)
        h = _rms(x_ref[rows, :], g_ref[...]).astype(BF16)
        yield
        piece, off = 0, 0
        for o_ref, act in zip(o_refs, acts):
            n = o_ref.shape[1]
            if off == used_cols[piece]:
                piece, off = piece + 1, 0
            w_ref = w_refs[piece]
            for c0 in range(0, n, col_chunk):
                c1 = min(c0 + col_chunk, n)
                y = _dot(h, w_ref[:, off + c0:off + c1])
                if act == "silu":
                    y = _silu(y)
                elif act == "log_gate":
                    lb = lb_ref[:, c0:c1]
                    y = jnp.log(lb + (1.0 - lb) * jax.nn.sigmoid(y))
                o_ref[rows, c0:c1] = y.astype(o_ref.dtype)
                yield
            off += n

    _staggered([stages(r0) for r0 in range(0, x_ref.shape[0], sub)])


def _norm_proj(x, g, w, outs, lb=None, layer=0, tm=512, sub=256, col_chunk=1024):
    T, D = x.shape
    pieces = list(w) if isinstance(w, (list, tuple)) else [(w, w.shape[-1])]
    ws = [pc[0] for pc in pieces]
    used_cols = tuple(pc[1] for pc in pieces)
    transposed = tuple(len(pc) > 2 and pc[2] for pc in pieces)
    widths = [o[0] for o in outs]
    assert sum(widths) == sum(used_cols) and T % tm == 0
    if lb is None:
        lb = jnp.zeros((LANES,), F32)

    def resident(v):
        if v.ndim == 3:
            return pl.BlockSpec((1,) + v.shape[1:], lambda i: (layer, 0, 0), pipeline_mode=pl.Buffered(1))
        return pl.BlockSpec(v.shape, lambda i: (0, 0), pipeline_mode=pl.Buffered(1))

    return pl.pallas_call(
        functools.partial(_norm_proj_kernel, acts=tuple(o[2] for o in outs), used_cols=used_cols,
                          transposed=transposed, col_chunk=col_chunk, sub=sub),
        grid=(T // tm,),
        in_specs=[_row_spec(tm, D, 1), _whole_spec((1, D), 1), _whole_spec((1, lb.shape[0]), 1)]
                 + [resident(v) for v in ws],
        out_specs=[_row_spec(tm, n, 1) for n in widths],
        out_shape=[jax.ShapeDtypeStruct((T, n), dt) for n, dt, _ in outs],
        scratch_shapes=[pltpu.VMEM((D, n), BF16) for n in used_cols],
        compiler_params=_cparams(("arbitrary",)),
        name="norm_proj",
    )(x, g.reshape(1, D), lb.reshape(1, -1), *ws)


def _ple_update(x, p, gin_ref, wg_ref, wp_ref, gpost_ref):
    h = _rms(x, gin_ref[...]).astype(BF16)
    gate = jax.nn.sigmoid(_dot(h, wg_ref[...]))
    proj = _dot(p.astype(BF16), wp_ref[...])
    return x + _rms(gate * proj, gpost_ref[...])


def _row_spec(tm, width, n_grid):
    return pl.BlockSpec((tm, width), (lambda i: (i, 0)) if n_grid == 1 else (lambda i, j: (i, 0)))


def _whole_spec(shape, n_grid):
    zeros = (0,) * len(shape)
    return pl.BlockSpec(shape, (lambda i: zeros) if n_grid == 1 else (lambda i, j: zeros))


def _mix_ffn_ple_kernel(oa_ref, ob_ref, wa_ref, wb_ref, gmix_ref, x_ref, gpre_ref, wg_ref, wu_ref, wd_ref,
                        gpost_ref, p_ref, gin_ref, wpg_ref, wpp_ref, gple_ref, o_ref, *, sub, tf):
    tm = x_ref.shape[0]
    F = wg_ref.shape[1]
    def stages(r0):
        rows = pl.ds(r0, sub)
        mix = _dot(oa_ref[rows, :], wa_ref[...]) + _dot(ob_ref[rows, :], wb_ref[...])
        x1 = x_ref[rows, :] + _rms(mix, gmix_ref[...])
        h = _rms(x1, gpre_ref[...]).astype(BF16)
        yield
        acc = None
        for f0 in range(0, F, tf):
            f1 = min(f0 + tf, F)
            a = (_silu(_dot(h, wg_ref[:, f0:f1])) * _dot(h, wu_ref[:, f0:f1])).astype(BF16)
            part = _dot(a, wd_ref[f0:f1, :])
            acc = part if acc is None else acc + part
            yield
        x2 = x1 + _rms(acc, gpost_ref[...])
        o_ref[rows, :] = _ple_update(x2, p_ref[rows, :], gin_ref, wpg_ref, wpp_ref, gple_ref)
        yield

    _staggered([stages(r0) for r0 in range(0, tm, sub)])


def _mix_ffn_ple(o_a, o_b, w_a, w_b, g_mix, x, g_pre, wg, wu, wd, g_post, p, g_in, wpg, wpp, g_ple,
                 p_row0=0, tm=512, sub=256, tf=1536):
    T, D = x.shape
    F = wg.shape[1]
    assert T % tm == 0 and tm % sub == 0 and p_row0 % tm == 0
    p_blk0 = p_row0 // tm
    vec = lambda g: g.reshape(1, D)
    resident = lambda w: pl.BlockSpec(w.shape, lambda i: (0, 0), pipeline_mode=pl.Buffered(1))
    return pl.pallas_call(
        functools.partial(_mix_ffn_ple_kernel, sub=sub, tf=tf),
        grid=(T // tm,),
        in_specs=[_row_spec(tm, o_a.shape[1], 1), _row_spec(tm, o_b.shape[1], 1),
                  resident(w_a), resident(w_b), _whole_spec((1, D), 1),
                  _row_spec(tm, D, 1), _whole_spec((1, D), 1),
                  resident(wg), resident(wu), resident(wd),
                  _whole_spec((1, D), 1),
                  pl.BlockSpec((tm, p.shape[1]), lambda i: (i + p_blk0, 0)), _whole_spec((1, D), 1),
                  resident(wpg), resident(wpp), _whole_spec((1, D), 1)],
        out_specs=_row_spec(tm, D, 1),
        out_shape=jax.ShapeDtypeStruct((T, D), F32),
        compiler_params=_cparams(("parallel",)),
        name="mix_ffn_ple",
    )(o_a, o_b, w_a, w_b, vec(g_mix), x, vec(g_pre), wg, wu, wd, vec(g_post), p, vec(g_in), wpg, wpp,
      vec(g_ple))


def _roll_rows(x, s, rev):
    n = x.shape[0]
    return pltpu.roll(x, (n - s) if rev else s, 0)


def _cumsum_rows(x, tau, rev):
    n = x.shape[0]
    s = 1
    while s < n:
        if s % SUBLANES:
            shifted = jnp.where(tau >= s, _roll_rows(x, s, rev), 0.0)
        else:
            zeros = jnp.zeros((s,) + x.shape[1:], x.dtype)
            shifted = (jnp.concatenate([x[s:], zeros], axis=0) if rev
                       else jnp.concatenate([zeros, x[:n - s]], axis=0))
        x = x + shifted
        s *= 2
    return x


def _hgrn_levels(C):
    return [C >> (i + 1) for i in range(C.bit_length() - 1)]


def _hgrn_pair_classes(C):
    t = np.arange(C)[:, None]
    s = np.arange(C)[None, :]
    out = np.full((2 * C, 2 * C), -1, np.int32)
    for d, rev in enumerate((False, True)):
        tau, sig = (C - 1 - t, C - 1 - s) if rev else (t, s)
        blk = np.full((C, C), -1, np.int32)
        blk[t == s] = 0
        for i, L in enumerate(_hgrn_levels(C)):
            m = ((t & -(2 * L)) == (s & -(2 * L))) & ((tau & (2 * L - 1)) >= L) & ((sig & (2 * L - 1)) < L)
            blk[m] = i + 1
        out[d * C:(d + 1) * C, d * C:(d + 1) * C] = blk
    return out


def _hgrn_kernel(q_ref, gf_ref, gb_ref, i_ref, gate_ref, cls_ref, ng_ref, o_ref, acc_scr, st_scr):
    S, DK = q_ref.shape
    C = A_CHUNK
    nc = S // C
    assert nc % (2 * A_UNROLL) == 0
    ng = ng_ref[...]
    row = lax.broadcasted_iota(jnp.int32, (C, DK), 0)
    levels = _hgrn_levels(C)
    zero_half = jnp.zeros((C, DK), BF16)

    def stack(top, bottom):
        return jnp.concatenate([top, bottom], axis=0)

    def block_diag(x):
        return stack(jnp.concatenate([x[:C], zero_half], axis=1), jnp.concatenate([zero_half, x[C:]], axis=1))

    def make_decay(rev):
        tau = (C - 1 - row) if rev else row
        odd_rank = (tau & 1) == 1
        last = 0 if rev else C - 1

        def ref_rows(b, L):
            off = L if rev else L - 1
            if 2 * L >= SUBLANES:
                pieces = [jnp.broadcast_to(b[j * 2 * L + off:j * 2 * L + off + 1, :], (2 * L, DK))
                          for j in range(C // (2 * L))]
                return pieces[0] if len(pieces) == 1 else jnp.concatenate(pieces, axis=0)
            if L == 1:
                return jnp.where(odd_rank, _roll_rows(b, 1, rev), b)
            b3 = b.reshape(C // SUBLANES, SUBLANES, DK)
            sub = lax.broadcasted_iota(jnp.int32, b3.shape, 1)
            out = None
            for j in range(SUBLANES // (2 * L)):
                piece = jnp.broadcast_to(b3[:, j * 2 * L + off:j * 2 * L + off + 1, :], b3.shape)
                out = piece if out is None else jnp.where(sub >= j * 2 * L, piece, out)
            return out.reshape(C, DK)

        def decay(g):
            b = _cumsum_rows(g, tau, rev)
            return b, [ref_rows(b, L) for L in levels], b[last:last + 1, :]

        return decay

    decay_fwd, decay_bwd = make_decay(False), make_decay(True)
    pair_cls = cls_ref[...]
    st_scr[...] = jnp.zeros_like(st_scr)

    def chunk_pair(rows_f, rows_b, st):
        q = stack(q_ref[rows_f, :], q_ref[rows_b, :]).astype(F32)
        g_f, g_b = gf_ref[rows_f, :], gb_ref[rows_b, :]
        k = 1.0 - jnp.exp(stack(g_f, g_b))
        vb = stack(i_ref[rows_f, :], i_ref[rows_b, :])
        b_f, refs_f, last_f = decay_fwd(g_f)
        b_b, refs_b, last_b = decay_bwd(g_b)
        b = stack(b_f, b_b)

        o = _dot_nt(block_diag((q * jnp.exp(b)).astype(BF16)), st.astype(BF16))

        qb, kb = q.astype(BF16), k.astype(BF16)
        attn = jnp.where(pair_cls == 0, _dot_nt(qb, kb), 0.0)
        for i in range(len(levels)):
            e = jnp.exp(-jnp.abs(b - stack(refs_f[i], refs_b[i]))).astype(BF16)
            attn = jnp.where(pair_cls == i + 1, _dot_nt(qb * e, kb * e), attn)
        o = o + _dot(attn.astype(BF16), vb)

        b_last = stack(jnp.broadcast_to(last_f, (C, DK)), jnp.broadcast_to(last_b, (C, DK)))
        khat = block_diag((k * jnp.exp(b_last - b)).astype(BF16))
        keep = jnp.exp(jnp.concatenate([last_f, last_b], axis=1))
        return o, st * keep + _dot_tn(vb, khat)

    def make_body(final):
        def body(it, carry):
            st = st_scr[...]
            for u in range(A_UNROLL):
                ci = it * A_UNROLL + u
                rows_f = pl.ds(pl.multiple_of(ci * C, C), C)
                rows_b = pl.ds(pl.multiple_of((nc - 1 - ci) * C, C), C)
                o, st = chunk_pair(rows_f, rows_b, st)
                for rows, part in ((rows_f, o[:C]), (rows_b, o[C:])):
                    acc_scr[rows, :] = (acc_scr[rows, :] + part) if final else part
            st_scr[...] = st
            return carry

        return body

    trips = nc // A_UNROLL
    lax.fori_loop(0, trips // 2, make_body(False), 0)
    lax.fori_loop(trips // 2, trips, make_body(True), 0)

    fin = 4 * C

    def finish(c, carry):
        rows = pl.ds(pl.multiple_of(c * fin, fin), fin)
        o_ref[rows, :] = (_rms(acc_scr[rows, :], ng) * gate_ref[rows, :].astype(F32)).astype(o_ref.dtype)
        return carry

    lax.fori_loop(0, S // fin, finish, 0, unroll=4)


def _hgrn_mixer(q, g_fwd, g_bwd, v, gate, norm_g, batch, seq):
    T = q.shape[0]
    H, DK = A_HEADS, A_DK
    head = pl.BlockSpec((seq, DK), lambda b, h: (b, h))
    classes = jnp.asarray(_hgrn_pair_classes(A_CHUNK))
    return pl.pallas_call(
        _hgrn_kernel,
        grid=(batch, H),
        in_specs=[head, head, head, head, head,
                  pl.BlockSpec(classes.shape, lambda b, h: (0, 0)),
                  pl.BlockSpec((1, DK), lambda b, h: (0, 0))],
        out_specs=pl.BlockSpec((seq, DK), lambda b, h: (b, h)),
        out_shape=jax.ShapeDtypeStruct((T, H * DK), BF16),
        scratch_shapes=[pltpu.VMEM((seq, DK), F32), pltpu.VMEM((DK, 2 * DK), F32)],
        compiler_params=_cparams(("parallel", "parallel")),
        name="hgrn2",
    )(q, g_fwd, g_bwd, v, gate, classes, norm_g.reshape(1, DK))


def _expand_heads(col, j0, width):
    q = col.shape[0]
    lane = lax.broadcasted_iota(jnp.int32, (q, width), 1)
    out = jnp.broadcast_to(col[:, j0 + B_HPG - 1:j0 + B_HPG], (q, width))
    for j in range(B_HPG - 2, -1, -1):
        out = jnp.where(lane < (j + 1) * B_HEADDIM,
                        jnp.broadcast_to(col[:, j0 + j:j0 + j + 1], (q, width)), out)
    return out


def _ssd_kernel(z_ref, x_ref, bm_ref, cm_ref, dt_ref, cwx_ref, cwb_ref, cwc_ref,
                cbx_ref, cbb_ref, cbc_ref, hp_ref, dsk_ref, ng_ref, o_ref,
                xs_scr, b_scr, c_scr, y_scr, *st_scrs):
    S = x_ref.shape[0]
    Q = B_CHUNK
    nc = S // Q
    GW = B_GW
    hp = hp_ref[0]
    a_row, dtb_row = hp[0:1, :], hp[1:2, :]
    row = lax.broadcasted_iota(jnp.int32, (Q, LANES), 0)
    t2 = lax.broadcasted_iota(jnp.int32, (Q, Q), 0)
    s2 = lax.broadcasted_iota(jnp.int32, (Q, Q), 1)
    lane_gw = lax.broadcasted_iota(jnp.int32, (Q, GW), 1)
    head_mask = [(lane_gw >= j * B_HEADDIM) & (lane_gw < (j + 1) * B_HEADDIM) for j in range(B_HPG)]
    halo = SUBLANES

    def conv_body(c, carry):
        r0 = pl.multiple_of(c * Q, Q)
        rp = pl.multiple_of(jnp.maximum(r0 - halo, 0), halo)
        rn = pl.multiple_of(jnp.minimum(r0 + Q, S - halo), halo)
        has_prev = c > 0
        has_next = c < nc - 1

        def conv(src_ref, w_ref, bias_ref):
            prev = jnp.where(has_prev, src_ref[pl.ds(rp, halo), :].astype(F32), 0.0)
            nxt = jnp.where(has_next, src_ref[pl.ds(rn, halo), :].astype(F32), 0.0)
            xx = jnp.concatenate([prev, src_ref[pl.ds(r0, Q), :].astype(F32), nxt], axis=0)
            n = Q + 2 * halo
            w = w_ref[0]
            acc = None
            for j in range(B_CONV):
                delta = j - B_CONV // 2
                sh = xx if delta == 0 else pltpu.roll(xx, (-delta) % n, 0)
                term = sh[halo:halo + Q, :] * w[j:j + 1, :]
                acc = term if acc is None else acc + term
            return _silu(acc + bias_ref[0])

        xs_scr[pl.ds(r0, Q), :] = conv(x_ref, cwx_ref, cbx_ref)
        b_scr[pl.ds(r0, Q), :] = conv(bm_ref, cwb_ref, cbb_ref).astype(BF16)
        c_scr[pl.ds(r0, Q), :] = conv(cm_ref, cwc_ref, cbc_ref).astype(BF16)
        return carry

    lax.fori_loop(0, nc, conv_body, 0)

    def make_chunk(rev):
        tau = (Q - 1 - row) if rev else row
        last = 0 if rev else Q - 1
        j0 = B_HPG if rev else 0
        pair_ok = (s2 >= t2) if rev else (t2 >= s2)

        def chunk(rows, st):
            dt = jax.nn.softplus(dt_ref[rows, :] + dtb_row)
            cs = _cumsum_rows(dt * a_row, tau, rev)
            cs_t = cs.T
            xs = xs_scr[rows, :]
            bm = b_scr[rows, :]
            cm = c_scr[rows, :]
            xdt = xs * _expand_heads(dt, j0, GW)
            ecs = _expand_heads(cs, j0, GW)
            ecs_last = ecs[last:last + 1, :]

            y = _dot(cm, st.astype(BF16)) * jnp.exp(ecs)

            scores = _dot_nt(cm, bm)
            for j in range(B_HPG):
                col = jnp.broadcast_to(cs[:, j0 + j:j0 + j + 1], (Q, Q))
                rw = jnp.broadcast_to(cs_t[j0 + j:j0 + j + 1, :], (Q, Q))
                decay = jnp.where(pair_ok, jnp.exp(jnp.minimum(col - rw, 0.0)), 0.0)
                xh = jnp.where(head_mask[j], xdt, 0.0).astype(BF16)
                y = y + _dot((scores * decay).astype(BF16), xh)

            xdec = (xdt * jnp.exp(ecs_last - ecs)).astype(BF16)
            return y, st * jnp.exp(ecs_last) + _dot_tn(bm, xdec)

        return chunk

    chunk_fns = (make_chunk(False), make_chunk(True))
    for st_scr in st_scrs:
        st_scr[...] = jnp.zeros_like(st_scr)

    def make_body(second_visit):
        def body(it, carry):
            for rev, chunk, st_scr in zip((False, True), chunk_fns, st_scrs):
                st = st_scr[...]
                for u in range(B_UNROLL):
                    ci = it * B_UNROLL + u
                    c = (nc - 1 - ci) if rev else ci
                    rows = pl.ds(pl.multiple_of(c * Q, Q), Q)
                    y, st = chunk(rows, st)
                    y_scr[rows, :] = (y_scr[rows, :] + y) if second_visit else y
                st_scr[...] = st
            return carry

        return body

    assert nc % (2 * B_UNROLL) == 0
    trips = nc // B_UNROLL
    lax.fori_loop(0, trips // 2, make_body(False), 0)
    lax.fori_loop(trips // 2, trips, make_body(True), 0)

    def finish(c, carry):
        rows = pl.ds(pl.multiple_of(c * Q, Q), Q)
        tot = y_scr[rows, :] + dsk_ref[0] * xs_scr[rows, :]
        tot = tot * _silu(z_ref[rows, :].astype(F32))
        o_ref[rows, :] = _rms(tot, ng_ref[0]).astype(o_ref.dtype)
        return carry

    lax.fori_loop(0, nc, finish, 0, unroll=4)


def _ssd_mixer(u, dt, conv_w, conv_b, a_log, dt_bias, d_skip, norm_g, batch, seq, col0):
    T = u.shape[0]
    G, GW, N = B_GROUPS, B_GW, B_STATE
    W = B_HEADS * B_HEADDIM
    assert col0 % GW == 0 and GW == 2 * N
    z_blk = col0 // GW
    x_blk = z_blk + W // GW
    b_blk = (col0 + 2 * W) // N
    c_blk = b_blk + G

    def pad_rows(w):
        return jnp.pad(w, ((0, SUBLANES - w.shape[0]), (0, 0)))

    cwx = pad_rows(conv_w[:, :W]).reshape(SUBLANES, G, GW).transpose(1, 0, 2)
    cwb = pad_rows(conv_w[:, W:W + G * N]).reshape(SUBLANES, G, N).transpose(1, 0, 2)
    cwc = pad_rows(conv_w[:, W + G * N:]).reshape(SUBLANES, G, N).transpose(1, 0, 2)
    cbx = conv_b[:W].reshape(G, 1, GW)
    cbb = conv_b[W:W + G * N].reshape(G, 1, N)
    cbc = conv_b[W + G * N:].reshape(G, 1, N)
    a_neg = -jnp.exp(a_log.astype(F32))
    per_dir = lambda v: jnp.concatenate([v[0].reshape(G, B_HPG), v[1].reshape(G, B_HPG)], axis=1)
    hp = jnp.stack([per_dir(a_neg), per_dir(dt_bias.astype(F32))], axis=1)
    hp = jnp.pad(hp, ((0, 0), (0, SUBLANES - 2), (0, LANES - 2 * B_HPG)))
    dsk = jnp.repeat(d_skip.astype(F32), B_HEADDIM).reshape(G, 1, GW)
    ng = norm_g.reshape(G, 1, GW)

    gspec = lambda shape: pl.BlockSpec((1,) + shape, lambda b, g: (g, 0, 0))
    return pl.pallas_call(
        _ssd_kernel,
        grid=(batch, G),
        in_specs=[pl.BlockSpec((seq, GW), lambda b, g: (b, z_blk + g)),
                  pl.BlockSpec((seq, GW), lambda b, g: (b, x_blk + g)),
                  pl.BlockSpec((seq, N), lambda b, g: (b, b_blk + g)),
                  pl.BlockSpec((seq, N), lambda b, g: (b, c_blk + g)),
                  pl.BlockSpec((seq, LANES), lambda b, g: (b, g)),
                  gspec((SUBLANES, GW)), gspec((SUBLANES, N)), gspec((SUBLANES, N)),
                  gspec((1, GW)), gspec((1, N)), gspec((1, N)),
                  gspec((SUBLANES, LANES)), gspec((1, GW)), gspec((1, GW))],
        out_specs=pl.BlockSpec((seq, GW), lambda b, g: (b, g)),
        out_shape=jax.ShapeDtypeStruct((T, W), BF16),
        scratch_shapes=[pltpu.VMEM((seq, GW), F32), pltpu.VMEM((seq, N), BF16),
                        pltpu.VMEM((seq, N), BF16), pltpu.VMEM((seq, GW), F32),
                        pltpu.VMEM((N, GW), F32), pltpu.VMEM((N, GW), F32)],
        compiler_params=_cparams(("parallel", "parallel")),
        name="ssd",
    )(u, u, u, u, dt, cwx, cwb, cwc, cbx, cbb, cbc, hp, dsk, ng)


def _na_kernel(q_ref, k_ref, v_ref, bias_ref, o_ref):
    S, HW = q_ref.shape
    W = GRID_W
    n_rows = S // W
    kh = min(NA_ROWS, n_rows)
    nk = kh * W
    n_pairs = bias_ref.shape[1] // NA_HG
    lane = lax.broadcasted_iota(jnp.int32, (W, HW), 1)
    head_masks = [(lane >= h * C_HEADDIM) & (lane < (h + 1) * C_HEADDIM) for h in range(NA_HG)]
    scale = C_HEADDIM ** -0.5

    def body(r, carry):
        rs = jnp.clip(r - kh // 2, 0, n_rows - kh)
        q = q_ref[pl.ds(pl.multiple_of(r * W, W), W), :] * jnp.asarray(scale, q_ref.dtype)
        kw = k_ref[pl.ds(pl.multiple_of(rs * W, W), nk), :]
        vw = v_ref[pl.ds(pl.multiple_of(rs * W, W), nk), :]
        zero = jnp.zeros_like(q)
        qs = jnp.concatenate([jnp.where(m, q, zero) for m in head_masks], axis=0)
        d0 = (NA_ROWS - 1) - (r - rs)
        bias = jnp.concatenate(
            [jnp.concatenate([bias_ref[0, h * n_pairs + d0 + 2 * m] for m in range(kh // 2)], axis=1)
             for h in range(NA_HG)], axis=0)
        s = _dot_nt(qs, kw) + bias
        m = jnp.max(s, axis=-1, keepdims=True)
        p = jnp.exp(s - m)
        l = jnp.sum(p, axis=-1, keepdims=True)
        res = _dot(p.astype(BF16), vw) / l
        out = jnp.zeros((W, HW), F32)
        for h in range(NA_HG):
            out = jnp.where(head_masks[h], res[h * W:(h + 1) * W, :], out)
        o_ref[pl.ds(pl.multiple_of(r * W, W), W), :] = out.astype(o_ref.dtype)
        return carry

    lax.fori_loop(0, n_rows, body, 0, unroll=NA_UNROLL)


def _na_bias_table(rpb):
    H, n_dr, n_dc = rpb.shape
    W = GRID_W
    c = np.arange(W)
    qs = np.clip(c - NA_COLS // 2, 0, W - NA_COLS)
    valid = (c[None, :] >= qs[:, None]) & (c[None, :] < qs[:, None] + NA_COLS)
    dc = np.clip(c[None, :] - c[:, None] + (NA_COLS - 1), 0, n_dc - 1)
    onehot = np.zeros((2, n_dc, W, 2 * W), np.float32)
    for half in range(2):
        onehot[half, :, :, half * W:(half + 1) * W] = dc[None] == np.arange(n_dc)[:, None, None]
    both = jnp.concatenate([rpb[:, :-1], rpb[:, 1:]], axis=2).astype(F32)
    pairs = jnp.einsum("hre,ecx->hrcx", both, jnp.asarray(onehot.reshape(2 * n_dc, W, 2 * W)),
                       precision=lax.Precision.HIGHEST)
    valid2 = np.concatenate([valid, valid], axis=1)
    pairs = jnp.where(jnp.asarray(valid2)[None, None], pairs, -jnp.inf)
    return pairs.reshape(H // NA_HG, NA_HG * (n_dr - 1), W, 2 * W)


def _neighborhood_attention(qkv, rpb, batch, seq):
    T = qkv.shape[0]
    HW = NA_HG * C_HEADDIM
    n_hg = C_HEADS // NA_HG
    n_rows = seq // GRID_W
    assert n_rows >= NA_ROWS and NA_ROWS % 2 == 0
    table = _na_bias_table(rpb)
    return pl.pallas_call(
        _na_kernel,
        grid=(batch, n_hg),
        in_specs=[pl.BlockSpec((seq, HW), lambda b, g: (b, g)),
                  pl.BlockSpec((seq, HW), lambda b, g: (b, n_hg + g)),
                  pl.BlockSpec((seq, HW), lambda b, g: (b, 2 * n_hg + g)),
                  pl.BlockSpec((1,) + table.shape[1:], lambda b, g: (g, 0, 0, 0))],
        out_specs=pl.BlockSpec((seq, HW), lambda b, g: (b, g)),
        out_shape=jax.ShapeDtypeStruct((T, C_HEADS * C_HEADDIM), BF16),
        compiler_params=_cparams(("parallel", "parallel")),
        name="natten",
    )(qkv, qkv, qkv, table)


def _store_token_tiles(ref, val, t0=0):
    count, width = val.shape
    n = width // LANES
    for j in range(n):
        ref[pl.ds(t0 * n + j, count, stride=n), :] = val[:, j * LANES:(j + 1) * LANES]


def _load_token_tiles(ref, n, t0=0, count=None):
    if count is None:
        count = ref.shape[0] // n - t0
    return jnp.concatenate([ref[pl.ds(t0 * n + j, count, stride=n), :] for j in range(n)], axis=1)


def _proj_router_kernel(a_ref, w_ref, gmix_ref, x_ref, g_ref, wr_ref, x1_ref, o_ref, h_ref):
    tm = x_ref.shape[0]
    sub = tm

    def stages(r0):
        rows = pl.ds(r0, sub)
        x1 = x_ref[rows, :] + _rms(_dot(a_ref[rows, :], w_ref[...]), gmix_ref[...])
        x1_ref[rows, :] = x1
        yield
        h = _rms(x1, g_ref[...])
        _store_token_tiles(h_ref, h, r0)
        h_hi = h.astype(BF16)
        h_lo = (h - h_hi.astype(F32)).astype(BF16)
        both = _dot(h_hi, wr_ref[...])
        logits = both[:, :LANES] + both[:, LANES:] + _dot(h_lo, wr_ref[:, :LANES])
        yield
        lane = lax.broadcasted_iota(jnp.int32, logits.shape, 1)
        neg = -jnp.inf
        l1 = jnp.where(lane < N_EXPERTS, logits, neg)
        m1 = jnp.max(l1, axis=-1, keepdims=True)
        i1 = jnp.min(jnp.where(l1 == m1, lane, LANES), axis=-1, keepdims=True)
        l2 = jnp.where(lane == i1, neg, l1)
        m2 = jnp.max(l2, axis=-1, keepdims=True)
        i2 = jnp.min(jnp.where(l2 == m2, lane, LANES), axis=-1, keepdims=True)
        e = jnp.exp(m2 - m1)
        w1 = 1.0 / (1.0 + e)
        w2 = e / (1.0 + e)
        o_ref[rows, :] = jnp.where(lane == 0, i1.astype(F32),
                                   jnp.where(lane == 1, i2.astype(F32),
                                             jnp.where(lane == 2, w1, jnp.where(lane == 3, w2, 0.0))))
        yield

    _staggered([stages(r0) for r0 in range(0, tm, sub)])


def _proj_router(a, w, g_mix, x, g, w_router, tm=512):
    T, D = x.shape
    n = D // LANES
    wr = jnp.pad(w_router.astype(F32), ((0, 0), (0, LANES - w_router.shape[1])))
    wr_hi = wr.astype(BF16)
    wr = jnp.concatenate([wr_hi, (wr - wr_hi.astype(F32)).astype(BF16)], axis=1)
    x1, route, h = pl.pallas_call(
        _proj_router_kernel,
        grid=(T // tm,),
        in_specs=[_row_spec(tm, a.shape[1], 1), _whole_spec(w.shape, 1), _whole_spec((1, D), 1),
                  _row_spec(tm, D, 1), _whole_spec((1, D), 1), _whole_spec((D, 2 * LANES), 1)],
        out_specs=[_row_spec(tm, D, 1), _row_spec(tm, LANES, 1), _row_spec(tm * n, LANES, 1)],
        out_shape=[jax.ShapeDtypeStruct((T, D), F32),
                   jax.ShapeDtypeStruct((T, LANES), F32),
                   jax.ShapeDtypeStruct((T * n, LANES), F32)],
        compiler_params=_cparams(("parallel",)),
        name="proj_router",
    )(a, w, g_mix.reshape(1, D), x, g.reshape(1, D), wr)
    return x1, route, h.reshape(T, n, LANES)


SC_CORES = 2
SC_SUBCORES = 16
SC_WORKERS = SC_CORES * SC_SUBCORES
SC_CHUNK = 32


def _sc_mesh():
    return plsc.VectorSubcoreMesh(core_axis_name="c", subcore_axis_name="s",
                                  num_cores=SC_CORES, num_subcores=SC_SUBCORES)


def _sc_index_blocks(idx):
    return idx.reshape(SC_WORKERS, -1, SC_CHUNK)


def _sc_scatter_tokens(h, dest, n_rows, row0=0):
    T = dest.shape[0]
    _, n, _ = h.shape
    per_worker = T // SC_WORKERS
    n_chunks = per_worker // SC_CHUNK
    assert per_worker * SC_WORKERS == T and n_chunks * SC_CHUNK == per_worker

    @functools.partial(
        pl.kernel, mesh=_sc_mesh(),
        out_type=jax.ShapeDtypeStruct((n_rows, n, LANES), h.dtype),
        scratch_types=[pltpu.VMEM((n_chunks, SC_CHUNK), jnp.int32),
                       pltpu.VMEM((n_chunks, SC_CHUNK), jnp.int32),
                       pltpu.VMEM((SC_CHUNK, n, LANES), h.dtype)],
        name="sc_scatter_tokens",
    )
    def scatter(h_hbm, d0_hbm, d1_hbm, o_hbm, i0_v, i1_v, rows_v):
        wid = lax.axis_index("s") * SC_CORES + lax.axis_index("c")
        pltpu.sync_copy(d0_hbm.at[wid], i0_v)
        pltpu.sync_copy(d1_hbm.at[wid], i1_v)

        @pl.loop(0, n_chunks)
        def _(j):
            pltpu.sync_copy(h_hbm.at[pl.ds(row0 + wid * per_worker + j * SC_CHUNK, SC_CHUNK)], rows_v)
            pltpu.sync_copy(rows_v, o_hbm.at[i0_v.at[j]])
            pltpu.sync_copy(rows_v, o_hbm.at[i1_v.at[j]])

    return scatter(h, _sc_index_blocks(dest[:, 0]), _sc_index_blocks(dest[:, 1]))


def _sc_gather_tokens(y, dest):
    T = dest.shape[0]
    _, n, _ = y.shape
    per_worker = T // SC_WORKERS
    n_chunks = per_worker // SC_CHUNK
    assert per_worker * SC_WORKERS == T and n_chunks * SC_CHUNK == per_worker
    out = jax.ShapeDtypeStruct((T, n, LANES), y.dtype)

    @functools.partial(
        pl.kernel, mesh=_sc_mesh(), out_type=(out, out),
        scratch_types=[pltpu.VMEM((n_chunks, SC_CHUNK), jnp.int32),
                       pltpu.VMEM((n_chunks, SC_CHUNK), jnp.int32),
                       pltpu.VMEM((SC_CHUNK, n, LANES), y.dtype)],
        name="sc_gather_tokens",
    )
    def gather(y_hbm, d0_hbm, d1_hbm, o0_hbm, o1_hbm, i0_v, i1_v, rows_v):
        wid = lax.axis_index("s") * SC_CORES + lax.axis_index("c")
        pltpu.sync_copy(d0_hbm.at[wid], i0_v)
        pltpu.sync_copy(d1_hbm.at[wid], i1_v)

        @pl.loop(0, n_chunks)
        def _(j):
            rows = pl.ds(wid * per_worker + j * SC_CHUNK, SC_CHUNK)
            pltpu.sync_copy(y_hbm.at[i0_v.at[j]], rows_v)
            pltpu.sync_copy(rows_v, o0_hbm.at[rows])
            pltpu.sync_copy(y_hbm.at[i1_v.at[j]], rows_v)
            pltpu.sync_copy(rows_v, o1_hbm.at[rows])

    return gather(y, _sc_index_blocks(dest[:, 0]), _sc_index_blocks(dest[:, 1]))


SC_PACK_PAIRS = 8
SC_LANES = 16


def _sc_pack_bf16_rows(w):
    R, C = w.shape
    pairs = R // 2
    per_worker = pairs // SC_WORKERS
    n_chunks = per_worker // SC_PACK_PAIRS
    assert n_chunks * SC_PACK_PAIRS * SC_WORKERS * 2 == R and C % SC_LANES == 0

    def round_bits(v):
        u = plsc.bitcast(v, jnp.int32)
        return u + 0x7FFF + (lax.shift_right_logical(u, 16) & 1)

    @functools.partial(
        pl.kernel, mesh=_sc_mesh(),
        out_type=jax.ShapeDtypeStruct((pairs, C), jnp.int32),
        scratch_types=[pltpu.VMEM((2 * SC_PACK_PAIRS, C), F32), pltpu.VMEM((SC_PACK_PAIRS, C), jnp.int32)],
        compiler_params=pltpu.CompilerParams(use_tc_tiling_on_sc=True, needs_layout_passes=False),
        name="sc_pack_bf16_rows",
    )
    def pack(w_hbm, o_hbm, in_v, out_v):
        wid = lax.axis_index("s") * SC_CORES + lax.axis_index("c")

        @pl.loop(0, n_chunks)
        def _(j):
            p0 = (wid * n_chunks + j) * SC_PACK_PAIRS
            pltpu.sync_copy(w_hbm.at[pl.ds(2 * p0, 2 * SC_PACK_PAIRS)], in_v)

            @pl.loop(0, C, step=SC_LANES)
            def _(c):
                cols = pl.ds(c, SC_LANES)
                for i in range(SC_PACK_PAIRS):
                    lo = lax.shift_right_logical(round_bits(in_v[2 * i, cols]), 16)
                    hi = round_bits(in_v[2 * i + 1, cols]) & jnp.int32(-65536)
                    out_v[i, cols] = lo | hi

            pltpu.sync_copy(out_v, o_hbm.at[pl.ds(p0, SC_PACK_PAIRS)])

    return pack(w)


def _moe_ffn_kernel(te_ref, na_ref, h_ref, wg_ref, wu_ref, wd_ref, o_ref, acc_scr, *, tf):
    i = pl.program_id(0)
    D = acc_scr.shape[1]
    F = wg_ref.shape[2]
    unpack = lambda packed: pltpu.bitcast(packed, BF16)

    @pl.when(i < na_ref[0])
    def _():
        h = _load_token_tiles(h_ref, D // LANES).astype(BF16)
        for f0 in range(0, F, tf):
            a = (_silu(_dot(h, unpack(wg_ref[0, :, f0:f0 + tf])))
                 * _dot(h, unpack(wu_ref[0, :, f0:f0 + tf]))).astype(BF16)
            part = _dot(a, unpack(wd_ref[0, f0 // 2:(f0 + tf) // 2, :]))
            if f0 == 0:
                acc_scr[...] = part
            else:
                acc_scr[...] += part
        _store_token_tiles(o_ref, acc_scr[...])

    @pl.when(i >= na_ref[0])
    def _():
        o_ref[...] = jnp.zeros_like(o_ref)


def _moe_ffn(hs, wg, wu, wd, tile_expert, n_active, tm=MOE_TM, tf=512):
    rows, n, _ = hs.shape
    E, half_d, F = wg.shape
    D = 2 * half_d
    nt = tile_expert.shape[0]
    assert rows == nt * tm and n * LANES == D and wd.shape == (E, F // 2, D)
    grid_spec = pltpu.PrefetchScalarGridSpec(
        num_scalar_prefetch=2,
        grid=(nt,),
        in_specs=[pl.BlockSpec((tm * n, LANES), lambda i, te, na: (i, 0)),
                  pl.BlockSpec((1, D // 2, F), lambda i, te, na: (te[i], 0, 0)),
                  pl.BlockSpec((1, D // 2, F), lambda i, te, na: (te[i], 0, 0)),
                  pl.BlockSpec((1, F // 2, D), lambda i, te, na: (te[i], 0, 0))],
        out_specs=pl.BlockSpec((tm * n, LANES), lambda i, te, na: (i, 0)),
        scratch_shapes=[pltpu.VMEM((tm, D), F32)],
    )
    y = pl.pallas_call(
        functools.partial(_moe_ffn_kernel, tf=tf),
        grid_spec=grid_spec,
        out_shape=jax.ShapeDtypeStruct((rows * n, LANES), F32),
        compiler_params=_cparams(("arbitrary",)),
        name="moe_ffn",
    )(tile_expert, n_active, hs.reshape(rows * n, LANES), wg, wu, wd)
    return y.reshape(rows, n, LANES)


def _combine_ple_kernel(y0_ref, y1_ref, route_ref, x_ref, g_ref, p_ref, gin_ref, wpg_ref, wpp_ref,
                        gple_ref, *rest):
    o_ref = rest[-1]
    tm, D = x_ref.shape
    n = D // LANES
    sub = min(tm, 256)

    def stages(r0):
        rows = pl.ds(r0, sub)
        route = route_ref[rows, :]
        moe = (route[:, 2:3] * _load_token_tiles(y0_ref, n, r0, sub)
               + route[:, 3:4] * _load_token_tiles(y1_ref, n, r0, sub))
        x2 = x_ref[rows, :] + _rms(moe, g_ref[...])
        yield
        o_ref[rows, :] = _ple_update(x2, p_ref[rows, :], gin_ref, wpg_ref, wpp_ref, gple_ref)
        yield

    _staggered([stages(r0) for r0 in range(0, tm, sub)])


def _combine_ple(y0, y1, route, x, g, p, g_in, wpg, wpp, g_ple, row0=0, prev=None, p_row0=0, tm=512):
    T, D = x.shape
    Tp, n, _ = y0.shape
    assert row0 % tm == 0 and Tp % tm == 0 and n * LANES == D and p_row0 % tm == 0
    blk0 = row0 // tm
    p_blk0 = (row0 + p_row0) // tm
    vec = lambda v: v.reshape(1, D)
    shifted = lambda width: pl.BlockSpec((tm, width), lambda i: (i + blk0, 0))
    in_specs = [_row_spec(tm * n, LANES, 1), _row_spec(tm * n, LANES, 1), shifted(LANES),
                shifted(D), _whole_spec((1, D), 1),
                pl.BlockSpec((tm, p.shape[1]), lambda i: (i + p_blk0, 0)), _whole_spec((1, D), 1),
                _whole_spec(wpg.shape, 1), _whole_spec(wpp.shape, 1), _whole_spec((1, D), 1)]
    args = [y0.reshape(Tp * n, LANES), y1.reshape(Tp * n, LANES), route, x, vec(g), p, vec(g_in), wpg, wpp,
            vec(g_ple)]
    aliases = {}
    if prev is not None:
        in_specs.append(pl.BlockSpec(memory_space=pl.ANY))
        args.append(prev)
        aliases = {len(args) - 1: 0}
    return pl.pallas_call(
        _combine_ple_kernel,
        grid=(Tp // tm,),
        in_specs=in_specs,
        out_specs=shifted(D),
        out_shape=jax.ShapeDtypeStruct((T, D), F32),
        input_output_aliases=aliases,
        compiler_params=_cparams(("parallel",)),
        name="combine_ple",
    )(*args)


def _moe_plan(route, tm=MOE_TM):
    T = route.shape[0]
    e = route[:, :TOP_K].astype(jnp.int32).reshape(-1)
    onehot = (e[:, None] == jnp.arange(N_EXPERTS, dtype=jnp.int32)[None, :]).astype(jnp.int32)
    csum = jnp.cumsum(onehot, axis=0)
    rank = jnp.sum(csum * onehot, axis=1) - 1
    counts = csum[-1]
    padded = ((counts + tm - 1) // tm) * tm
    ends = jnp.cumsum(padded)
    starts = ends - padded
    dest = jnp.sum(starts[None, :] * onehot, axis=1) + rank
    nt = (T * TOP_K) // tm + N_EXPERTS
    tile_start = jnp.arange(nt, dtype=jnp.int32) * tm
    tile_expert = jnp.minimum(jnp.sum((tile_start[:, None] >= ends[None, :]).astype(jnp.int32), axis=1),
                              N_EXPERTS - 1).astype(jnp.int32)
    n_active = (ends[-1] // tm).astype(jnp.int32).reshape(1)
    return tile_expert, n_active, dest.reshape(T, TOP_K).astype(jnp.int32)


def kernel(x, p, hgrn_lb_raw, e_norm_mix_pre, e_w_in, e_conv_w, e_conv_b, e_A_log, e_dt_bias, e_D, e_a_norm, e_b_norm, e_w_out, e_norm_mix_post, e_norm_ffn_pre, e_w_ffn_gate, e_w_ffn_up, e_w_ffn_down, e_norm_ffn_post, o_norm_mix_pre, o_w_qkv, o_rpb, o_w_out, o_norm_mix_post, o_norm_ffn_pre, o_w_router, o_w_exp_gate, o_w_exp_up, o_w_exp_down, o_norm_ffn_post, ple_norm_in, ple_w_gate, ple_w_proj, ple_norm_post):
    batch, seq, d_model = x.shape
    depth = p.shape[0]
    T = batch * seq
    xt = x.reshape(T, d_model)
    p_all = p.reshape(depth * T, -1)
    lb_all =jnp.cumsum(jax.nn.softmax(hgrn_lb_raw.astype(F32), axis=0), axis=0)

    a_kdim = A_HEADS * A_DK
    b_width = B_HEADS * B_HEADDIM
    conv_dim = b_width + 2 * B_GROUPS * B_STATE
    main_w = 5 * a_kdim + b_width + conv_dim

    def pack_experts(w):
        E, R, C = w.shape
        return _sc_pack_bf16_rows(w.reshape(E * R, C)).reshape(E, R // 2, C)

    for li in range(depth):
        j = li // 2
        ple =(ple_norm_in[li], ple_w_gate[li].astype(BF16), ple_w_proj[li].astype(BF16), ple_norm_post[li])
        if li % 2 == 0:
            w_in = e_w_in[j]
            dtf = w_in[:, main_w:main_w + B_HEADS].reshape(d_model, B_GROUPS, B_HPG)
            dtb = w_in[:, main_w + B_HEADS:].reshape(d_model, B_GROUPS, B_HPG)
            w_dt = jnp.pad(jnp.concatenate([dtf, dtb], axis=2), ((0, 0), (0, 0), (0, LANES - 2 * B_HPG)))
            w_pieces = [(jnp.swapaxes(e_w_in, 1, 2), main_w, True),
                        (w_dt.reshape(d_model, B_GROUPS * LANES), B_GROUPS * LANES)]
            outs = [(a_kdim, BF16, "silu"), (a_kdim, F32, "log_gate"), (a_kdim, F32, "log_gate"),
                    (a_kdim, BF16, None), (a_kdim, BF16, "silu"),
                    (b_width + conv_dim, BF16, None), (B_GROUPS * LANES, F32, None)]
            q_a, gf_a, gb_a, i_a, gate_a, u_b, dt = _norm_proj(
                xt, e_norm_mix_pre[j], w_pieces, outs, lb=lb_all[li], layer=j)
            o_a = _hgrn_mixer(q_a, gf_a, gb_a, i_a, gate_a, e_a_norm[j], batch, seq)
            o_b = _ssd_mixer(u_b, dt, e_conv_w[j], e_conv_b[j], e_A_log[j], e_dt_bias[j], e_D[j],
                             e_b_norm[j], batch, seq, 0)
            w_out = e_w_out[j].astype(BF16)
            xt = _mix_ffn_ple(o_a, o_b, w_out[:a_kdim], w_out[a_kdim:], e_norm_mix_post[j], xt,
                              e_norm_ffn_pre[j], e_w_ffn_gate[j].astype(BF16), e_w_ffn_up[j].astype(BF16),
                              e_w_ffn_down[j].astype(BF16), e_norm_ffn_post[j], p_all, *ple, p_row0=li * T)
        else:
            (qkv,) = _norm_proj(xt, o_norm_mix_pre[j], o_w_qkv,
                                [(3 * C_HEADS * C_HEADDIM, BF16, None)], layer=j)
            o_c = _neighborhood_attention(qkv, o_rpb[j], batch, seq)
            xt, route, h = _proj_router(o_c, o_w_out[j].astype(BF16), o_norm_mix_post[j], xt,
                                        o_norm_ffn_pre[j], o_w_router[j])
            expert_w = [pack_experts(w[j]) for w in (o_w_exp_gate, o_w_exp_up, o_w_exp_down)]
            x_mid, xt = xt, None
            t_grp = T // MOE_GROUPS
            for s in range(MOE_GROUPS):
                tile_expert, n_active, dest = _moe_plan(route[s * t_grp:(s + 1) * t_grp])
                hs = _sc_scatter_tokens(h, dest, tile_expert.shape[0] * MOE_TM, row0=s * t_grp)
                ys = _moe_ffn(hs, *expert_w, tile_expert, n_active)
                y0, y1 = _sc_gather_tokens(ys, dest)
                xt = _combine_ple(y0, y1, route, x_mid, o_norm_ffn_post[j], p_all, *ple,
                                  row0=s * t_grp, prev=xt, p_row0=li * T)
    return xt.reshape(batch, seq, d_model)
```
